```python
import math
import jax, jax.numpy as jnp
from jax import lax
import numpy as np

D_MODEL = 1024
BATCH = 8
SEQ = 8192
DEPTH = 1

CHUNK = 64
Q_BLOCK = 128
EPS = 1e-6

FOX_WIDTH = D_MODEL // 2
FOX_HEAD_DIM = 64
FOX_HEADS = FOX_WIDTH // FOX_HEAD_DIM

S5_WIDTH = D_MODEL // 2
S5_GROUP = 16
S5_GROUPS = S5_WIDTH // S5_GROUP
S5_STATE = 64
DT_MIN = 1e-3
DT_MAX = 1e-1

PROJ_SIZES = (FOX_WIDTH, FOX_WIDTH, FOX_WIDTH, FOX_HEADS, FOX_WIDTH,
              S5_WIDTH, S5_WIDTH,
              D_MODEL, D_MODEL)
PROJ_WIDTH = sum(PROJ_SIZES)

kernel_name = "fox_s5_gated_hybrid_block"


def rmsnorm(x, g):
    xf = x.astype(jnp.float32)
    y = xf * lax.rsqrt(jnp.mean(xf * xf, axis=-1, keepdims=True) + EPS)
    return (y * g.astype(jnp.float32)).astype(x.dtype)


def forgetting_attention(q, k, v, log_f):
    S = q.shape[1]
    Dh = q.shape[-1]
    F = jnp.cumsum(log_f.astype(jnp.float32), axis=1).transpose(0, 2, 1)
    scale = Dh ** -0.5
    neg = jnp.finfo(jnp.float32).min
    outs = []
    for i in range(S // Q_BLOCK):
        q0 = i * Q_BLOCK
        kend = q0 + Q_BLOCK
        qb = q[:, q0:kend]
        kb = k[:, :kend]
        vb = v[:, :kend]
        s = jnp.einsum('bqhd,bkhd->bhqk', qb, kb).astype(jnp.float32) * scale
        s = s + F[:, :, q0:kend, None] - F[:, :, None, :kend]
        t_idx = q0 + jnp.arange(Q_BLOCK)
        s_idx = jnp.arange(kend)
        s = jnp.where(s_idx[None, :] <= t_idx[:, None], s, neg)
        p = jax.nn.softmax(s, axis=-1).astype(v.dtype)
        outs.append(jnp.einsum('bhqk,bkhd->bqhd', p, vb))
    return jnp.concatenate(outs, axis=1)


def s5_ssm(u, a_re, a_im, log_dt, b_re, b_im, c_re, c_im, d_skip):
    Bsz, S, W = u.shape
    uf = u.astype(jnp.float32)
    ug = uf.reshape(Bsz, S, S5_GROUPS, S5_GROUP).astype(jnp.complex64)
    lam = lax.complex(a_re.astype(jnp.float32), a_im.astype(jnp.float32))
    dt = jnp.exp(log_dt.astype(jnp.float32))[:, None]
    a_bar = jnp.exp(lam * dt)
    bmat = lax.complex(b_re.astype(jnp.float32), b_im.astype(jnp.float32))
    b_bar = ((a_bar - 1.0) / lam)[:, :, None] * bmat
    bu = jnp.einsum('bsgc,gpc->sbgp', ug, b_bar)
    a_seq = jnp.broadcast_to(a_bar, bu.shape)

    def combine(left, right):
        return (right[0] * left[0], right[0] * left[1] + right[1])

    _, states = lax.associative_scan(combine, (a_seq, bu), axis=0)
    cmat = lax.complex(c_re.astype(jnp.float32), c_im.astype(jnp.float32))
    y = jnp.einsum('sbgp,gcp->bsgc', states, cmat).real.reshape(Bsz, S, W)
    y = y + d_skip.astype(jnp.float32) * uf
    return y.astype(u.dtype)


def _fwd_setup_inputs(seed: int = 0) -> dict:
    key = jax.random.key(seed)
    ks = jax.random.split(key, 24)
    D, L, G, P, Cg = D_MODEL, DEPTH, S5_GROUPS, S5_STATE, S5_GROUP
    f32 = jnp.float32

    def nrm(k, shape, fan_in, s=1.0):
        return s * jax.random.normal(k, shape, f32) * fan_in ** -0.5

    x = jax.random.normal(ks[0], (BATCH, SEQ, D), f32)
    c = jax.random.normal(ks[1], (BATCH, D), f32)
    w_ada = nrm(ks[2], (L, D, 3 * D), D, 0.3)
    b_ada = 0.02 * jax.random.normal(ks[3], (L, 3 * D), f32)
    g_norm = 1.0 + 0.05 * jax.random.normal(ks[4], (L, D), f32)
    w_in = nrm(ks[5], (L, D, PROJ_WIDTH), D)
    b_f = 3.0 + 0.5 * jax.random.normal(ks[6], (L, FOX_HEADS), f32)
    n_idx = jnp.arange(P, dtype=f32)
    a_re = -0.5 + 0.01 * jax.random.normal(ks[7], (L, G, P), f32)
    a_im = jnp.broadcast_to(math.pi * n_idx, (L, G, P)) + 0.01 * jax.random.normal(ks[8], (L, G, P), f32)
    log_dt = jax.random.uniform(ks[9], (L, G), f32, math.log(DT_MIN), math.log(DT_MAX))
    b_re = nrm(ks[10], (L, G, P, Cg), 2 * Cg)
    b_im = nrm(ks[11], (L, G, P, Cg), 2 * Cg)
    c_re = nrm(ks[12], (L, G, Cg, P), 2 * P)
    c_im = nrm(ks[13], (L, G, Cg, P), 2 * P)
    d_skip = jax.random.normal(ks[14], (L, S5_WIDTH), f32)
    w_glu = nrm(ks[15], (L, S5_WIDTH, S5_WIDTH), S5_WIDTH)
    b_glu = 0.02 * jax.random.normal(ks[16], (L, S5_WIDTH), f32)
    w_up_a = nrm(ks[17], (L, FOX_WIDTH, D), FOX_WIDTH)
    w_up_b = nrm(ks[18], (L, S5_WIDTH, D), S5_WIDTH)
    w_out = nrm(ks[19], (L, D, D), D)
    g_final = 1.0 + 0.05 * jax.random.normal(ks[20], (D,), f32)
    return {"x": x, "c": c, "w_ada": w_ada, "b_ada": b_ada, "g_norm": g_norm,
            "w_in": w_in, "b_f": b_f, "a_re": a_re, "a_im": a_im, "log_dt": log_dt,
            "b_re": b_re, "b_im": b_im, "c_re": c_re, "c_im": c_im, "d_skip": d_skip,
            "w_glu": w_glu, "b_glu": b_glu, "w_up_a": w_up_a, "w_up_b": w_up_b,
            "w_out": w_out, "g_final": g_final}


def _fwd_reference(x, c, w_ada, b_ada, g_norm, w_in, b_f, a_re, a_im, log_dt, b_re, b_im,
              c_re, c_im, d_skip, w_glu, b_glu, w_up_a, w_up_b, w_out, g_final):
    Bsz, S, D = x.shape
    offsets = []
    acc = 0
    for sz in PROJ_SIZES[:-1]:
        acc += sz
        offsets.append(acc)
    for l in range(DEPTH):
        mod = c @ w_ada[l] + b_ada[l]
        shift, scale, gate = jnp.split(mod, 3, axis=-1)
        h = rmsnorm(x, g_norm[l]) * (1.0 + scale[:, None, :]) + shift[:, None, :]

        proj = h @ w_in[l]
        q, k, v, f_logit, z_a, u, z_b, g_a, g_b = jnp.split(proj, offsets, axis=-1)

        hs = (Bsz, S, FOX_HEADS, FOX_HEAD_DIM)
        log_f = jax.nn.log_sigmoid((f_logit + b_f[l]).astype(jnp.float32))
        y_a = forgetting_attention(q.reshape(hs), k.reshape(hs), v.reshape(hs), log_f)
        y_a = y_a.reshape(Bsz, S, FOX_WIDTH) * jax.nn.silu(z_a)

        y_b = s5_ssm(u, a_re[l], a_im[l], log_dt[l], b_re[l], b_im[l], c_re[l], c_im[l], d_skip[l])
        y_b = jax.nn.gelu(y_b)
        y_b = y_b * jax.nn.sigmoid(y_b @ w_glu[l] + b_glu[l])
        y_b = y_b * jax.nn.silu(z_b)

        merged = jax.nn.sigmoid(g_a) * (y_a @ w_up_a[l]) + jax.nn.sigmoid(g_b) * (y_b @ w_up_b[l])
        x = x + gate[:, None, :] * (merged @ w_out[l])
    return rmsnorm(x, g_final)


import jax as _jax
import jax.numpy as _jnp

TWIN_FORMAT = 'train_step'
FWD_PARAMS = ['x', 'c', 'w_ada', 'b_ada', 'g_norm', 'w_in', 'b_f', 'a_re', 'a_im', 'log_dt', 'b_re', 'b_im', 'c_re', 'c_im', 'd_skip', 'w_glu', 'b_glu', 'w_up_a', 'w_up_b', 'w_out', 'g_final']
TWIN_WEIGHTS = ['w_ada', 'b_ada', 'g_norm', 'w_in', 'b_f', 'a_re', 'a_im', 'log_dt', 'b_re', 'b_im', 'c_re', 'c_im', 'd_skip', 'w_glu', 'b_glu', 'w_up_a', 'w_up_b', 'w_out', 'g_final']
TWIN_DIFF_INPUT = 'x'
TWIN_INPUTS = ['x', 'c', 'w_ada', 'b_ada', 'g_norm', 'w_in', 'b_f', 'a_re', 'a_im', 'log_dt', 'b_re', 'b_im', 'c_re', 'c_im', 'd_skip', 'w_glu', 'b_glu', 'w_up_a', 'w_up_b', 'w_out', 'g_final', 'loss_target', 'm_w_ada', 'm_b_ada', 'm_g_norm', 'm_w_in', 'm_b_f', 'm_a_re', 'm_a_im', 'm_log_dt', 'm_b_re', 'm_b_im', 'm_c_re', 'm_c_im', 'm_d_skip', 'm_w_glu', 'm_b_glu', 'm_w_up_a', 'm_w_up_b', 'm_w_out', 'm_g_final', 'v_w_ada', 'v_b_ada', 'v_g_norm', 'v_w_in', 'v_b_f', 'v_a_re', 'v_a_im', 'v_log_dt', 'v_b_re', 'v_b_im', 'v_c_re', 'v_c_im', 'v_d_skip', 'v_w_glu', 'v_b_glu', 'v_w_up_a', 'v_w_up_b', 'v_w_out', 'v_g_final']
TWIN_OUTPUTS = ['loss', 'grad_x', 'grad_w_ada', 'grad_b_ada', 'grad_g_norm', 'grad_w_in', 'grad_b_f', 'grad_a_re', 'grad_a_im', 'grad_log_dt', 'grad_b_re', 'grad_b_im', 'grad_c_re', 'grad_c_im', 'grad_d_skip', 'grad_w_glu', 'grad_b_glu', 'grad_w_up_a', 'grad_w_up_b', 'grad_w_out', 'grad_g_final', 'delta_w_ada', 'delta_b_ada', 'delta_g_norm', 'delta_w_in', 'delta_b_f', 'delta_a_re', 'delta_a_im', 'delta_log_dt', 'delta_b_re', 'delta_b_im', 'delta_c_re', 'delta_c_im', 'delta_d_skip', 'delta_w_glu', 'delta_b_glu', 'delta_w_up_a', 'delta_w_up_b', 'delta_w_out', 'delta_g_final', 'new_m_w_ada', 'new_m_b_ada', 'new_m_g_norm', 'new_m_w_in', 'new_m_b_f', 'new_m_a_re', 'new_m_a_im', 'new_m_log_dt', 'new_m_b_re', 'new_m_b_im', 'new_m_c_re', 'new_m_c_im', 'new_m_d_skip', 'new_m_w_glu', 'new_m_b_glu', 'new_m_w_up_a', 'new_m_w_up_b', 'new_m_w_out', 'new_m_g_final', 'new_v_w_ada', 'new_v_b_ada', 'new_v_g_norm', 'new_v_w_in', 'new_v_b_f', 'new_v_a_re', 'new_v_a_im', 'new_v_log_dt', 'new_v_b_re', 'new_v_b_im', 'new_v_c_re', 'new_v_c_im', 'new_v_d_skip', 'new_v_w_glu', 'new_v_b_glu', 'new_v_w_up_a', 'new_v_w_up_b', 'new_v_w_out', 'new_v_g_final']
TWIN_LEAF_KINDS = {'loss': 'loss', 'grad_x': 'grad_x', 'grad_w_ada': 'grad_w', 'grad_b_ada': 'grad_w', 'grad_g_norm': 'grad_w', 'grad_w_in': 'grad_w', 'grad_b_f': 'grad_w', 'grad_a_re': 'grad_w', 'grad_a_im': 'grad_w', 'grad_log_dt': 'grad_w', 'grad_b_re': 'grad_w', 'grad_b_im': 'grad_w', 'grad_c_re': 'grad_w', 'grad_c_im': 'grad_w', 'grad_d_skip': 'grad_w', 'grad_w_glu': 'grad_w', 'grad_b_glu': 'grad_w', 'grad_w_up_a': 'grad_w', 'grad_w_up_b': 'grad_w', 'grad_w_out': 'grad_w', 'grad_g_final': 'grad_w', 'delta_w_ada': 'delta_w', 'delta_b_ada': 'delta_w', 'delta_g_norm': 'delta_w', 'delta_w_in': 'delta_w', 'delta_b_f': 'delta_w', 'delta_a_re': 'delta_w', 'delta_a_im': 'delta_w', 'delta_log_dt': 'delta_w', 'delta_b_re': 'delta_w', 'delta_b_im': 'delta_w', 'delta_c_re': 'delta_w', 'delta_c_im': 'delta_w', 'delta_d_skip': 'delta_w', 'delta_w_glu': 'delta_w', 'delta_b_glu': 'delta_w', 'delta_w_up_a': 'delta_w', 'delta_w_up_b': 'delta_w', 'delta_w_out': 'delta_w', 'delta_g_final': 'delta_w', 'new_m_w_ada': 'new_m', 'new_m_b_ada': 'new_m', 'new_m_g_norm': 'new_m', 'new_m_w_in': 'new_m', 'new_m_b_f': 'new_m', 'new_m_a_re': 'new_m', 'new_m_a_im': 'new_m', 'new_m_log_dt': 'new_m', 'new_m_b_re': 'new_m', 'new_m_b_im': 'new_m', 'new_m_c_re': 'new_m', 'new_m_c_im': 'new_m', 'new_m_d_skip': 'new_m', 'new_m_w_glu': 'new_m', 'new_m_b_glu': 'new_m', 'new_m_w_up_a': 'new_m', 'new_m_w_up_b': 'new_m', 'new_m_w_out': 'new_m', 'new_m_g_final': 'new_m', 'new_v_w_ada': 'new_v', 'new_v_b_ada': 'new_v', 'new_v_g_norm': 'new_v', 'new_v_w_in': 'new_v', 'new_v_b_f': 'new_v', 'new_v_a_re': 'new_v', 'new_v_a_im': 'new_v', 'new_v_log_dt': 'new_v', 'new_v_b_re': 'new_v', 'new_v_b_im': 'new_v', 'new_v_c_re': 'new_v', 'new_v_c_im': 'new_v', 'new_v_d_skip': 'new_v', 'new_v_w_glu': 'new_v', 'new_v_b_glu': 'new_v', 'new_v_w_up_a': 'new_v', 'new_v_w_up_b': 'new_v', 'new_v_w_out': 'new_v', 'new_v_g_final': 'new_v'}


def _forward(args):
    return _fwd_reference(*[args[k] for k in FWD_PARAMS])


def _output_shape():
    def fwd():
        inp = _fwd_setup_inputs(0)
        return _fwd_reference(*[inp[k] for k in FWD_PARAMS])
    out = _jax.eval_shape(fwd)
    return out.shape, out.dtype

N_MICROBATCH = 1
ADAM_LR = 0.001
ADAM_B1 = 0.9
ADAM_B2 = 0.999
ADAM_EPS = 1e-08
ADAM_WD = 0.01
ADAM_STEP = 10
PER_EXAMPLE_BATCH_AXIS = {'x': 0, 'c': 0, 'loss_target': 0}
SHARED_INPUTS = []
_WEIGHT_DTYPES = {'w_ada': _jnp.float32, 'b_ada': _jnp.float32, 'g_norm': _jnp.float32, 'w_in': _jnp.float32, 'b_f': _jnp.float32, 'a_re': _jnp.float32, 'a_im': _jnp.float32, 'log_dt': _jnp.float32, 'b_re': _jnp.float32, 'b_im': _jnp.float32, 'c_re': _jnp.float32, 'c_im': _jnp.float32, 'd_skip': _jnp.float32, 'w_glu': _jnp.float32, 'b_glu': _jnp.float32, 'w_up_a': _jnp.float32, 'w_up_b': _jnp.float32, 'w_out': _jnp.float32, 'g_final': _jnp.float32}
MOMENT_SCALE = {'w_ada': 4.127353e-02, 'b_ada': 4.134033e-02, 'g_norm': 2.568478e-02, 'w_in': 1.271932e-02, 'b_f': 5.309738e-02, 'a_re': 1.248537e-03, 'a_im': 7.097886e-04, 'log_dt': 6.732070e-01, 'b_re': 6.365908e-04, 'b_im': 6.886992e-04, 'c_re': 1.362698e-03, 'c_im': 1.391356e-03, 'd_skip': 1.810821e-02, 'w_glu': 4.771087e-03, 'b_glu': 7.593296e-03, 'w_up_a': 1.281345e-02, 'w_up_b': 1.057326e-02, 'w_out': 1.656140e-02, 'g_final': 6.420964e+01}


def _to_microbatches(a, axis):
    t = _jnp.moveaxis(a, axis, 0)
    t = t.reshape((N_MICROBATCH, t.shape[0] // N_MICROBATCH) + t.shape[1:])
    return _jnp.moveaxis(t, 1, axis + 1)


def setup_inputs(seed: int = 0) -> dict:
    inp = _fwd_setup_inputs(seed)
    key = _jax.random.fold_in(_jax.random.key(seed), 7919)
    shape, _ = _output_shape()
    out = dict(inp)
    out["loss_target"] = _jax.random.normal(_jax.random.fold_in(key, 0), shape, _jnp.float32)
    for i, name in enumerate(TWIN_WEIGHTS):
        w = inp[name].astype(_jnp.float32)
        if MOMENT_SCALE is None:
            s = _jnp.sqrt(_jnp.mean(_jnp.square(w)) + 1e-30)
        else:
            s = MOMENT_SCALE[name]
        km, kv = _jax.random.split(_jax.random.fold_in(key, i + 1))
        out[name] = w
        out["m_" + name] = s * _jax.random.normal(km, w.shape, _jnp.float32)
        out["v_" + name] = (s * s) * _jax.random.uniform(kv, w.shape, _jnp.float32, 0.5, 1.5)
    if N_MICROBATCH > 1:
        for name, axis in PER_EXAMPLE_BATCH_AXIS.items():
            out[name] = _to_microbatches(out[name], axis)
    return {'x': out['x'], 'c': out['c'], 'w_ada': out['w_ada'], 'b_ada': out['b_ada'], 'g_norm': out['g_norm'], 'w_in': out['w_in'], 'b_f': out['b_f'], 'a_re': out['a_re'], 'a_im': out['a_im'], 'log_dt': out['log_dt'], 'b_re': out['b_re'], 'b_im': out['b_im'], 'c_re': out['c_re'], 'c_im': out['c_im'], 'd_skip': out['d_skip'], 'w_glu': out['w_glu'], 'b_glu': out['b_glu'], 'w_up_a': out['w_up_a'], 'w_up_b': out['w_up_b'], 'w_out': out['w_out'], 'g_final': out['g_final'], 'loss_target': out['loss_target'], 'm_w_ada': out['m_w_ada'], 'm_b_ada': out['m_b_ada'], 'm_g_norm': out['m_g_norm'], 'm_w_in': out['m_w_in'], 'm_b_f': out['m_b_f'], 'm_a_re': out['m_a_re'], 'm_a_im': out['m_a_im'], 'm_log_dt': out['m_log_dt'], 'm_b_re': out['m_b_re'], 'm_b_im': out['m_b_im'], 'm_c_re': out['m_c_re'], 'm_c_im': out['m_c_im'], 'm_d_skip': out['m_d_skip'], 'm_w_glu': out['m_w_glu'], 'm_b_glu': out['m_b_glu'], 'm_w_up_a': out['m_w_up_a'], 'm_w_up_b': out['m_w_up_b'], 'm_w_out': out['m_w_out'], 'm_g_final': out['m_g_final'], 'v_w_ada': out['v_w_ada'], 'v_b_ada': out['v_b_ada'], 'v_g_norm': out['v_g_norm'], 'v_w_in': out['v_w_in'], 'v_b_f': out['v_b_f'], 'v_a_re': out['v_a_re'], 'v_a_im': out['v_a_im'], 'v_log_dt': out['v_log_dt'], 'v_b_re': out['v_b_re'], 'v_b_im': out['v_b_im'], 'v_c_re': out['v_c_re'], 'v_c_im': out['v_c_im'], 'v_d_skip': out['v_d_skip'], 'v_w_glu': out['v_w_glu'], 'v_b_glu': out['v_b_glu'], 'v_w_up_a': out['v_w_up_a'], 'v_w_up_b': out['v_w_up_b'], 'v_w_out': out['v_w_out'], 'v_g_final': out['v_g_final']}


def _loss(weights, diff, rest, loss_target):
    with _jax.named_scope("forward"):
        args = {**rest, TWIN_DIFF_INPUT: diff, **{k: w.astype(_WEIGHT_DTYPES[k]) for k, w in weights.items()}}
        y = _forward(args)
    with _jax.named_scope("loss_head"):
        err = _jnp.square(y.astype(_jnp.float32) - loss_target)
        return 0.5 * _jnp.sum(_jnp.mean(err, axis=-1)) if err.ndim else 0.5 * err


def _adamw(w, g, m, v):
    m = ADAM_B1 * m + (1.0 - ADAM_B1) * g
    v = ADAM_B2 * v + (1.0 - ADAM_B2) * _jnp.square(g)
    m_hat = m / (1.0 - ADAM_B1 ** ADAM_STEP)
    v_hat = v / (1.0 - ADAM_B2 ** ADAM_STEP)
    delta = -ADAM_LR * (m_hat / (_jnp.sqrt(v_hat) + ADAM_EPS) + ADAM_WD * w)
    return delta, m, v


def reference(x, c, w_ada, b_ada, g_norm, w_in, b_f, a_re, a_im, log_dt, b_re, b_im, c_re, c_im, d_skip, w_glu, b_glu, w_up_a, w_up_b, w_out, g_final, loss_target, m_w_ada, m_b_ada, m_g_norm, m_w_in, m_b_f, m_a_re, m_a_im, m_log_dt, m_b_re, m_b_im, m_c_re, m_c_im, m_d_skip, m_w_glu, m_b_glu, m_w_up_a, m_w_up_b, m_w_out, m_g_final, v_w_ada, v_b_ada, v_g_norm, v_w_in, v_b_f, v_a_re, v_a_im, v_log_dt, v_b_re, v_b_im, v_c_re, v_c_im, v_d_skip, v_w_glu, v_b_glu, v_w_up_a, v_w_up_b, v_w_out, v_g_final):
    given = dict(x=x, c=c, w_ada=w_ada, b_ada=b_ada, g_norm=g_norm, w_in=w_in, b_f=b_f, a_re=a_re, a_im=a_im, log_dt=log_dt, b_re=b_re, b_im=b_im, c_re=c_re, c_im=c_im, d_skip=d_skip, w_glu=w_glu, b_glu=b_glu, w_up_a=w_up_a, w_up_b=w_up_b, w_out=w_out, g_final=g_final, loss_target=loss_target, m_w_ada=m_w_ada, m_b_ada=m_b_ada, m_g_norm=m_g_norm, m_w_in=m_w_in, m_b_f=m_b_f, m_a_re=m_a_re, m_a_im=m_a_im, m_log_dt=m_log_dt, m_b_re=m_b_re, m_b_im=m_b_im, m_c_re=m_c_re, m_c_im=m_c_im, m_d_skip=m_d_skip, m_w_glu=m_w_glu, m_b_glu=m_b_glu, m_w_up_a=m_w_up_a, m_w_up_b=m_w_up_b, m_w_out=m_w_out, m_g_final=m_g_final, v_w_ada=v_w_ada, v_b_ada=v_b_ada, v_g_norm=v_g_norm, v_w_in=v_w_in, v_b_f=v_b_f, v_a_re=v_a_re, v_a_im=v_a_im, v_log_dt=v_log_dt, v_b_re=v_b_re, v_b_im=v_b_im, v_c_re=v_c_re, v_c_im=v_c_im, v_d_skip=v_d_skip, v_w_glu=v_w_glu, v_b_glu=v_b_glu, v_w_up_a=v_w_up_a, v_w_up_b=v_w_up_b, v_w_out=v_w_out, v_g_final=v_g_final)
    weights = {n: given[n] for n in TWIN_WEIGHTS}
    shared = {n: given[n] for n in SHARED_INPUTS}
    per_example = {n: given[n] for n in ['x', 'c']}
    grad_fn = _jax.value_and_grad(_loss, argnums=(0, 1))

    def one_microbatch(ex, loss_target):
        ex = dict(ex)
        diff = ex.pop(TWIN_DIFF_INPUT)
        return grad_fn(weights, diff, {**shared, **ex}, loss_target)

    if N_MICROBATCH == 1:
        loss, (grad_w, grad_x) = one_microbatch(per_example, given["loss_target"])
    else:
        def body(carry, xs):
            loss_sum, grad_sum = carry
            l_k, (gw_k, gx_k) = one_microbatch(xs[0], xs[1])
            with _jax.named_scope("update"):
                return (loss_sum + l_k, _jax.tree.map(_jnp.add, grad_sum, gw_k)), gx_k

        init = (_jnp.zeros((), _jnp.float32), _jax.tree.map(_jnp.zeros_like, weights))
        (loss, grad_w), grad_x = _jax.lax.scan(body, init, (per_example, given["loss_target"]))
    with _jax.named_scope("update"):
        delta_w, new_m, new_v = {}, {}, {}
        for n in TWIN_WEIGHTS:
            delta_w[n], new_m[n], new_v[n] = _adamw(weights[n], grad_w[n], given["m_" + n], given["v_" + n])
    return (loss, grad_x, *[grad_w[n] for n in TWIN_WEIGHTS], *[delta_w[n] for n in TWIN_WEIGHTS],
            *[new_m[n] for n in TWIN_WEIGHTS], *[new_v[n] for n in TWIN_WEIGHTS])
```

```python
import functools
import math

import jax
import jax.numpy as jnp
import numpy as np
from jax import lax
from jax.experimental import pallas as pl
from jax.experimental.pallas import tpu as pltpu

F32 = jnp.float32
MXU_DTYPE = jnp.bfloat16
HI = lax.Precision.HIGHEST

N_DEV = 8
D_MODEL = 1024
WIDTH = 512
HEADS = 8
PAIRS = HEADS // 2
GROUPS = 32
STATE = 64
CG = 16
CHUNK = 16
EPS = 1e-6
NEG = float(np.finfo(np.float32).min)

ADAM_LR = 0.001
ADAM_B1 = 0.9
ADAM_B2 = 0.999
ADAM_EPS = 1e-08
ADAM_WD = 0.01
ADAM_STEP = 10

VMEM_BIG = 56 * 1024 * 1024
VMEM_MID = 40 * 1024 * 1024

OFF_F = 3 * WIDTH
PROJ_WIDTH = 5128
M_Q, M_K, M_V, M_ZA, M_U, M_ZB, M_GA, M_GB = 0, 512, 1024, 1536, 2048, 2560, 3072, 4096


def _mm(a, b):
    return jnp.dot(a.astype(MXU_DTYPE), b.astype(MXU_DTYPE), preferred_element_type=F32)


def _mm_nt(a, b):
    return lax.dot_general(a.astype(MXU_DTYPE), b.astype(MXU_DTYPE), (((1,), (1,)), ((), ())),
                           preferred_element_type=F32)


def _mm_tn(a, b):
    return lax.dot_general(a.astype(MXU_DTYPE), b.astype(MXU_DTYPE), (((0,), (0,)), ((), ())),
                           preferred_element_type=F32)


def _mm32(a, b):
    return jnp.dot(a, b, precision=HI, preferred_element_type=F32)


def _mm32_nt(a, b):
    return lax.dot_general(a, b, (((1,), (1,)), ((), ())), precision=HI, preferred_element_type=F32)


def _mm32_tn(a, b):
    return lax.dot_general(a, b, (((0,), (0,)), ((), ())), precision=HI, preferred_element_type=F32)


def _sigmoid(x):
    return 1.0 / (1.0 + jnp.exp(-x))


def _params(sem=None, vmem=None):
    kw = {}
    if sem is not None:
        kw["dimension_semantics"] = sem
    if vmem is not None:
        kw["vmem_limit_bytes"] = vmem
    return pltpu.CompilerParams(**kw)


def _my_index():
    return 4 * lax.axis_index("x") + 2 * lax.axis_index("y") + lax.axis_index("c")


def _dev(p):
    return (p // 4, (p // 2) % 2, p % 2)


ANY = pl.BlockSpec(memory_space=pl.ANY)
VMEM = pl.BlockSpec(memory_space=pltpu.VMEM)
MESH = pl.DeviceIdType.MESH


def _comm_in(c, w_ada, b_ada8, shards):
    n = len(shards)
    cols = w_ada.shape[1]

    def body(c_ref, wada_ref, bada_ref, *rest):
        srcs = rest[:n]
        mod_ref, call_ref = rest[n], rest[n + 1]
        dsts = rest[n + 2:2 * n + 2]
        modp, wsend, wrecv, wloc, csend, crecv, msend, mrecv = rest[2 * n + 2:]
        me = _my_index()

        def wcopy(a, src_dev, d, to):
            return pltpu.make_async_remote_copy(src_ref=srcs[a], dst_ref=dsts[a].at[src_dev],
                                                send_sem=wsend.at[a, d], recv_sem=wrecv.at[a, src_dev],
                                                device_id=_dev(to), device_id_type=MESH)

        def ccopy(src_dev, d, to):
            return pltpu.make_async_remote_copy(src_ref=c_ref, dst_ref=call_ref.at[pl.ds(src_dev, 1)],
                                                send_sem=csend.at[d], recv_sem=crecv.at[src_dev],
                                                device_id=_dev(to), device_id_type=MESH)

        def mcopy(src_dev, d, to):
            return pltpu.make_async_remote_copy(src_ref=modp.at[pl.ds(to, 1)], dst_ref=mod_ref.at[pl.ds(src_dev, 1)],
                                                send_sem=msend.at[d], recv_sem=mrecv.at[src_dev],
                                                device_id=_dev(to), device_id_type=MESH)

        local = [pltpu.make_async_copy(srcs[a], dsts[a].at[me], wloc.at[a]) for a in range(n)]
        for cp in local:
            cp.start()
        peers = [(me + d) % N_DEV for d in range(1, N_DEV)]
        for d, p in enumerate(peers):
            for a in range(n):
                wcopy(a, me, d, p).start()
        call_ref[pl.ds(me, 1), :] = c_ref[...]
        for d, p in enumerate(peers):
            ccopy(me, d, p).start()
        for d, p in enumerate(peers):
            ccopy(p, d, p).wait_recv()
        modp[...] = _mm32(call_ref[...], wada_ref[...]) + bada_ref[pl.ds(me, 1), :]
        mod_ref[pl.ds(me, 1), :] = modp[pl.ds(me, 1), :]
        for d, p in enumerate(peers):
            mcopy(me, d, p).start()
        for d, p in enumerate(peers):
            mcopy(p, d, p).wait_recv()
        for d, p in enumerate(peers):
            for a in range(n):
                wcopy(a, p, d, p).wait_recv()
        for d, p in enumerate(peers):
            for a in range(n):
                wcopy(a, me, d, p).wait_send()
            ccopy(me, d, p).wait_send()
            mcopy(me, d, p).wait_send()
        for cp in local:
            cp.wait()

    out_shape = ([jax.ShapeDtypeStruct((N_DEV, cols), F32), jax.ShapeDtypeStruct((N_DEV, D_MODEL), F32)]
                 + [jax.ShapeDtypeStruct((N_DEV,) + s.shape, s.dtype) for s in shards])
    res = pl.pallas_call(
        body, name="comm_in", out_shape=out_shape,
        in_specs=[VMEM, VMEM, VMEM] + [ANY] * n,
        out_specs=[VMEM, VMEM] + [ANY] * n,
        scratch_shapes=[pltpu.VMEM((N_DEV, cols), F32),
                        pltpu.SemaphoreType.DMA((n, N_DEV)), pltpu.SemaphoreType.DMA((n, N_DEV)),
                        pltpu.SemaphoreType.DMA((n,)),
                        pltpu.SemaphoreType.DMA((N_DEV,)), pltpu.SemaphoreType.DMA((N_DEV,)),
                        pltpu.SemaphoreType.DMA((N_DEV,)), pltpu.SemaphoreType.DMA((N_DEV,))],
        compiler_params=_params(vmem=VMEM_MID),
    )(c, w_ada, b_ada8, *shards)
    return res[0], res[1], list(res[2:])


def _comm_out(planes, small):
    n = len(planes)

    def body(*rest):
        srcs = rest[:n]
        small_ref = rest[n]
        dsts = rest[n + 1:2 * n + 1]
        sall_ref = rest[2 * n + 1]
        wsend, wrecv, wloc, ssend, srecv, sloc = rest[2 * n + 2:]
        me = _my_index()

        def wcopy(a, src_dev, d, to):
            return pltpu.make_async_remote_copy(src_ref=srcs[a].at[to], dst_ref=dsts[a].at[src_dev],
                                                send_sem=wsend.at[a, d], recv_sem=wrecv.at[a, src_dev],
                                                device_id=_dev(to), device_id_type=MESH)

        def scopy(src_dev, d, to):
            return pltpu.make_async_remote_copy(src_ref=small_ref, dst_ref=sall_ref.at[src_dev],
                                                send_sem=ssend.at[d], recv_sem=srecv.at[src_dev],
                                                device_id=_dev(to), device_id_type=MESH)

        local = [pltpu.make_async_copy(srcs[a].at[me], dsts[a].at[me], wloc.at[a]) for a in range(n)]
        local.append(pltpu.make_async_copy(small_ref, sall_ref.at[me], sloc))
        for cp in local:
            cp.start()
        peers = [(me + d) % N_DEV for d in range(1, N_DEV)]
        for d, p in enumerate(peers):
            scopy(me, d, p).start()
            for a in range(n):
                wcopy(a, me, d, p).start()
        for d, p in enumerate(peers):
            scopy(p, d, p).wait_recv()
            for a in range(n):
                wcopy(a, p, d, p).wait_recv()
        for d, p in enumerate(peers):
            scopy(me, d, p).wait_send()
            for a in range(n):
                wcopy(a, me, d, p).wait_send()
        for cp in local:
            cp.wait()

    out_shape = ([jax.ShapeDtypeStruct(p.shape, p.dtype) for p in planes]
                 + [jax.ShapeDtypeStruct((N_DEV,) + small.shape, small.dtype)])
    res = pl.pallas_call(
        body, name="comm_out", out_shape=out_shape,
        in_specs=[ANY] * (n + 1), out_specs=[ANY] * (n + 1),
        scratch_shapes=[pltpu.SemaphoreType.DMA((n, N_DEV)), pltpu.SemaphoreType.DMA((n, N_DEV)),
                        pltpu.SemaphoreType.DMA((n,)),
                        pltpu.SemaphoreType.DMA((N_DEV,)), pltpu.SemaphoreType.DMA((N_DEV,)),
                        pltpu.SemaphoreType.DMA(())],
    )(*planes, small)
    return list(res[:n]), res[n]


def _proj_fwd(x, shift, scale, g_norm, w_main, w_f, w_ft, ts):
    S = x.shape[0]

    def body(x_ref, sh_ref, sc_ref, gn_ref, w_ref, wf_ref, wft_ref,
             q_ref, k_ref, v_ref, za_ref, u_ref, zb_ref, ga_ref, gb_ref, flc_ref, flr_ref, h_ref):
        xv = x_ref[...]
        r = lax.rsqrt(jnp.mean(xv * xv, axis=-1, keepdims=True) + EPS)
        h = (xv * r) * gn_ref[...] * (1.0 + sc_ref[...]) + sh_ref[...]
        hb = h.astype(MXU_DTYPE)
        h_ref[...] = hb

        def seg(off, n):
            return jnp.dot(hb, w_ref[:, off:off + n], preferred_element_type=F32)

        q_ref[...] = (seg(M_Q, WIDTH) * 0.125).astype(q_ref.dtype)
        k_ref[...] = seg(M_K, WIDTH).astype(k_ref.dtype)
        v_ref[...] = seg(M_V, WIDTH).astype(v_ref.dtype)
        za_ref[...] = seg(M_ZA, WIDTH)
        u_ref[...] = seg(M_U, WIDTH)
        zb_ref[...] = seg(M_ZB, WIDTH)
        ga_ref[...] = seg(M_GA, D_MODEL)
        gb_ref[...] = seg(M_GB, D_MODEL)
        flc_ref[...] = _mm32(h, wf_ref[...])
        flr_ref[...] = _mm32_nt(wft_ref[...], h)

    row = lambda n: pl.BlockSpec((ts, n), lambda i: (i, 0))
    full = lambda a: pl.BlockSpec(a.shape, lambda i: (0,) * a.ndim)
    sds = jax.ShapeDtypeStruct
    return pl.pallas_call(
        body, name="proj_fwd", grid=(S // ts,),
        in_specs=[row(D_MODEL), full(shift), full(scale), full(g_norm), full(w_main), full(w_f), full(w_ft)],
        out_specs=[row(WIDTH)] * 6 + [row(D_MODEL)] * 2 + [row(HEADS), pl.BlockSpec((HEADS, ts), lambda i: (0, i)),
                                                          row(D_MODEL)],
        out_shape=[sds((S, WIDTH), MXU_DTYPE)] * 3 + [sds((S, WIDTH), F32)] * 3 + [sds((S, D_MODEL), F32)] * 2
                  + [sds((S, HEADS), F32), sds((HEADS, S), F32), sds((S, D_MODEL), MXU_DTYPE)],
        compiler_params=_params(("parallel",), VMEM_BIG),
    )(x, shift, scale, g_norm, w_main, w_f, w_ft)


def _log_sigmoid(z):
    return jnp.minimum(z, 0.0) - jnp.log(1.0 + jnp.exp(-jnp.abs(z)))


def _fgate_fwd(flc, flr, bf_row, bf_col, ts):
    S = flc.shape[0]

    def body(flc_ref, flr_ref, bfr_ref, bfc_ref, fc_ref, fr_ref, carry_c, carry_r):
        @pl.when(pl.program_id(0) == 0)
        def _():
            carry_c[...] = jnp.zeros_like(carry_c)
            carry_r[...] = jnp.zeros_like(carry_r)

        ri = lax.broadcasted_iota(jnp.int32, (ts, ts), 0)
        ci = lax.broadcasted_iota(jnp.int32, (ts, ts), 1)
        lower = (ci <= ri).astype(F32)
        upper = (ri <= ci).astype(F32)
        fc = _mm32(lower, _log_sigmoid(flc_ref[...] + bfr_ref[...])) + carry_c[...]
        fc_ref[...] = fc
        carry_c[...] = fc[ts - 1:ts, :]
        fr = _mm32(_log_sigmoid(flr_ref[...] + bfc_ref[...]), upper) + carry_r[...]
        fr_ref[...] = fr
        carry_r[...] = fr[:, ts - 1:ts]

    col = pl.BlockSpec((ts, HEADS), lambda i: (i, 0))
    rowb = pl.BlockSpec((HEADS, ts), lambda i: (0, i))
    return pl.pallas_call(
        body, name="fgate_fwd", grid=(S // ts,),
        in_specs=[col, rowb, pl.BlockSpec((1, HEADS), lambda i: (0, 0)), pl.BlockSpec((HEADS, 1), lambda i: (0, 0))],
        out_specs=[col, rowb],
        out_shape=[jax.ShapeDtypeStruct((S, HEADS), F32), jax.ShapeDtypeStruct((HEADS, S), F32)],
        scratch_shapes=[pltpu.VMEM((1, HEADS), F32), pltpu.VMEM((HEADS, 1), F32)],
        compiler_params=_params(("arbitrary",)),
    )(flc, flr, bf_row, bf_col)


def _fgate_bwd(dfc, flc, bf_row, ts):
    S = flc.shape[0]
    n = S // ts

    def body(df_ref, flc_ref, bfr_ref, dfl_ref, dbf_ref, carry):
        @pl.when(pl.program_id(0) == 0)
        def _():
            carry[...] = jnp.zeros_like(carry)
            dbf_ref[...] = jnp.zeros_like(dbf_ref)

        ri = lax.broadcasted_iota(jnp.int32, (ts, ts), 0)
        ci = lax.broadcasted_iota(jnp.int32, (ts, ts), 1)
        upper = (ci >= ri).astype(F32)
        rc = _mm32(upper, df_ref[...]) + carry[...]
        carry[...] = rc[0:1, :]
        z = flc_ref[...] + bfr_ref[...]
        dfl = rc * _sigmoid(-z)
        dfl_ref[...] = dfl
        dbf_ref[...] += jnp.sum(dfl, axis=0, keepdims=True)

    col = pl.BlockSpec((ts, HEADS), lambda i: (n - 1 - i, 0))
    one = pl.BlockSpec((1, HEADS), lambda i: (0, 0))
    return pl.pallas_call(
        body, name="fgate_bwd", grid=(n,),
        in_specs=[col, col, one], out_specs=[col, one],
        out_shape=[jax.ShapeDtypeStruct((S, HEADS), F32), jax.ShapeDtypeStruct((1, HEADS), F32)],
        scratch_shapes=[pltpu.VMEM((1, HEADS), F32)],
        compiler_params=_params(("arbitrary",)),
    )(dfc, flc, bf_row)


def _attn_fwd(q, k, v, fcol4, frow4, t):
    S = q.shape[0]
    nb = S // t

    def body(q_ref, k_ref, v_ref, fc_ref, fr_ref, o_ref, lse_ref, qh_s, m_s, l_s, acc_s):
        qi = pl.program_id(1)
        lane = lax.broadcasted_iota(jnp.int32, (t, 128), 1)
        is_a = lane < 64
        qp = q_ref[...]
        zero = jnp.zeros_like(qp)
        qh_s[0] = jnp.where(is_a, qp, zero)
        qh_s[1] = jnp.where(is_a, zero, qp)
        m_s[...] = jnp.full(m_s.shape, -jnp.inf, F32)
        l_s[...] = jnp.zeros_like(l_s)
        acc_s[...] = jnp.zeros_like(acc_s)
        fc = fc_ref[...]

        def step(ki, masked):
            ks = pl.multiple_of(ki * t, t)
            kb = k_ref[pl.ds(ks, t), :]
            vb = v_ref[pl.ds(ks, t), :]
            fr = fr_ref[ki]
            for h in range(2):
                s = _mm_nt(qh_s[h], kb) + (fc[:, h:h + 1] - fr[h:h + 1, :])
                if masked:
                    ri = lax.broadcasted_iota(jnp.int32, (t, t), 0)
                    ci = lax.broadcasted_iota(jnp.int32, (t, t), 1)
                    s = jnp.where(ci <= ri, s, NEG)
                m_old = m_s[h]
                m_new = jnp.maximum(m_old, jnp.max(s, axis=1, keepdims=True))
                alpha = jnp.exp(m_old - m_new)
                pm = jnp.exp(s - m_new)
                l_s[h] = alpha * l_s[h] + jnp.sum(pm, axis=1, keepdims=True)
                m_s[h] = m_new
                acc_s[h] = alpha * acc_s[h] + _mm(pm, vb)

        def loop_body(ki, carry):
            step(ki, False)
            return carry

        lax.fori_loop(0, qi, loop_body, 0)
        step(qi, True)
        o_ref[...] = jnp.where(is_a, acc_s[0] / l_s[0], acc_s[1] / l_s[1])
        lse_ref[...] = jnp.where(is_a, m_s[0] + jnp.log(l_s[0]), m_s[1] + jnp.log(l_s[1]))

    blk = pl.BlockSpec((t, 128), lambda p, i: (i, p))
    res = pl.BlockSpec((S, 128), lambda p, i: (0, p))
    return pl.pallas_call(
        body, name="attn_fwd", grid=(PAIRS, nb),
        in_specs=[blk, res, res, pl.BlockSpec((None, t, 2), lambda p, i: (p, i, 0)),
                  pl.BlockSpec((None, nb, 2, t), lambda p, i: (p, 0, 0, 0))],
        out_specs=[blk, blk],
        out_shape=[jax.ShapeDtypeStruct((S, WIDTH), F32)] * 2,
        scratch_shapes=[pltpu.VMEM((2, t, 128), q.dtype), pltpu.VMEM((2, t, 1), F32), pltpu.VMEM((2, t, 1), F32),
                        pltpu.VMEM((2, t, 128), F32)],
        compiler_params=_params(("parallel", "parallel"), VMEM_MID),
    )(q, k, v, fcol4, frow4)


def _attn_bwd(q, do, k, v, fcol4, frow4, lse4, dl4, t):
    S = q.shape[0]
    nb = S // t

    def body(q_ref, do_ref, k_ref, v_ref, fc_ref, frq_ref, lse_ref, dl_ref,
             dq_ref, dk_ref, dv_ref, dfq_ref, dfk_ref, kh_s, vh_s, dk_s, dv_s, dfk_s):
        kj = pl.program_id(1)
        lane = lax.broadcasted_iota(jnp.int32, (t, 128), 1)
        is_a = lane < 64

        @pl.when(kj == 0)
        def _():
            dq_ref[...] = jnp.zeros_like(dq_ref)
            dfq_ref[...] = jnp.zeros_like(dfq_ref)

        kp = k_ref[...]
        vp = v_ref[...]
        zero = jnp.zeros_like(kp)
        kh_s[0] = jnp.where(is_a, kp, zero)
        kh_s[1] = jnp.where(is_a, zero, kp)
        vh_s[0] = jnp.where(is_a, vp, zero)
        vh_s[1] = jnp.where(is_a, zero, vp)
        dk_s[...] = jnp.zeros_like(dk_s)
        dv_s[...] = jnp.zeros_like(dv_s)
        dfk_s[...] = jnp.zeros_like(dfk_s)
        fck = fc_ref[...]

        def step(qi, masked):
            qs = pl.multiple_of(qi * t, t)
            qb = q_ref[pl.ds(qs, t), :]
            dob = do_ref[pl.ds(qs, t), :]
            frq = frq_ref[qi]
            lse = lse_ref[qi]
            dl = dl_ref[qi]
            zq = jnp.zeros_like(qb)
            over_keys = []
            for h in range(2):
                sel = is_a if h == 0 else jnp.logical_not(is_a)
                st = _mm_nt(kh_s[h], qb) + (frq[h:h + 1, :] - fck[:, h:h + 1]) - lse[h:h + 1, :]
                if masked:
                    ri = lax.broadcasted_iota(jnp.int32, (t, t), 0)
                    ci = lax.broadcasted_iota(jnp.int32, (t, t), 1)
                    st = jnp.where(ci >= ri, st, NEG)
                pt = jnp.exp(st)
                dv_s[...] += _mm(pt, jnp.where(sel, dob, zq))
                dpt = _mm_nt(vh_s[h], dob)
                dst = pt * (dpt - dl[h:h + 1, :])
                dfk_s[h] += jnp.sum(dst, axis=1, keepdims=True)
                over_keys.append(jnp.sum(dst, axis=0, keepdims=True))
                dk_s[...] += _mm(dst, jnp.where(sel, qb, zq))
                dq_ref[pl.ds(qs, t), :] += _mm_tn(dst, kh_s[h])
            dfq_ref[qi] += jnp.concatenate(over_keys, axis=0)

        step(kj, True)

        def loop_body(qi, carry):
            step(qi, False)
            return carry

        lax.fori_loop(kj + 1, nb, loop_body, 0)
        dk_ref[...] = dk_s[...].astype(dk_ref.dtype)
        dv_ref[...] = dv_s[...].astype(dv_ref.dtype)
        dfk_ref[...] = jnp.where(lax.broadcasted_iota(jnp.int32, (t, 2), 1) == 0, dfk_s[0], dfk_s[1])

        @pl.when(kj == nb - 1)
        def _():
            dq_ref[...] = dq_ref[...] * 0.125

    blk = pl.BlockSpec((t, 128), lambda p, j: (j, p))
    res = pl.BlockSpec((S, 128), lambda p, j: (0, p))
    rows4 = pl.BlockSpec((None, nb, 2, t), lambda p, j: (p, 0, 0, 0))
    cols4 = pl.BlockSpec((None, t, 2), lambda p, j: (p, j, 0))
    return pl.pallas_call(
        body, name="attn_bwd", grid=(PAIRS, nb),
        in_specs=[res, res, blk, blk, cols4, rows4, rows4, rows4],
        out_specs=[res, blk, blk, rows4, cols4],
        out_shape=[jax.ShapeDtypeStruct((S, WIDTH), F32), jax.ShapeDtypeStruct((S, WIDTH), MXU_DTYPE),
                   jax.ShapeDtypeStruct((S, WIDTH), MXU_DTYPE), jax.ShapeDtypeStruct((PAIRS, nb, 2, t), F32),
                   jax.ShapeDtypeStruct((PAIRS, S, 2), F32)],
        scratch_shapes=[pltpu.VMEM((2, t, 128), k.dtype), pltpu.VMEM((2, t, 128), v.dtype),
                        pltpu.VMEM((t, 128), F32), pltpu.VMEM((t, 128), F32), pltpu.VMEM((2, t, 1), F32)],
        compiler_params=_params(("parallel", "arbitrary"), VMEM_MID),
    )(q, do, k, v, fcol4, frow4, lse4, dl4)


def _s5_mats(a_re, a_im, log_dt, b_re, b_im, c_re, c_im, d_skip):
    Lc = CHUNK
    dt = jnp.exp(log_dt)[:, None]
    lr, li = a_re * dt, a_im * dt

    def apow(n):
        n = jnp.asarray(n, F32)[None, :, None]
        mag = jnp.exp(n * lr[:, None, :])
        ang = n * li[:, None, :]
        return mag * jnp.cos(ang), mag * jnp.sin(ang)

    ar, ai = apow([1.0])
    ar, ai = ar[:, 0], ai[:, 0]
    den = a_re * a_re + a_im * a_im
    nr, ni = ar - 1.0, ai
    fr = (nr * a_re + ni * a_im) / den
    fi = (ni * a_re - nr * a_im) / den
    bbr = fr[:, :, None] * b_re - fi[:, :, None] * b_im
    bbi = fr[:, :, None] * b_im + fi[:, :, None] * b_re
    steps = np.arange(Lc, dtype=np.float32)
    pr, pi = apow(steps)
    car = c_re[:, None] * pr[:, :, None, :] - c_im[:, None] * pi[:, :, None, :]
    cai = c_re[:, None] * pi[:, :, None, :] + c_im[:, None] * pr[:, :, None, :]
    kern = (jnp.einsum('glcp,gpd->glcd', car, bbr, precision=HI)
            - jnp.einsum('glcp,gpd->glcd', cai, bbi, precision=HI))
    skip = d_skip.reshape(GROUPS, CG)[:, :, None] * jnp.eye(CG, dtype=F32)[None]
    kern = kern.at[:, 0].add(skip)
    sel = (steps[None, None, :] - steps[None, :, None] == steps[:, None, None]).astype(np.float32)
    tmat = jnp.einsum('tsl,gtcd->gsdlc', sel, kern, precision=HI).reshape(GROUPS, Lc * CG, Lc * CG)
    p1r, p1i = apow(steps + 1.0)
    cr = c_re[:, None] * p1r[:, :, None, :] - c_im[:, None] * p1i[:, :, None, :]
    ci = c_re[:, None] * p1i[:, :, None, :] + c_im[:, None] * p1r[:, :, None, :]
    to_rows = lambda m: m.transpose(0, 3, 1, 2).reshape(GROUPS, STATE, Lc * CG)
    camat = jnp.concatenate([to_rows(cr), -to_rows(ci)], axis=1)
    qr, qi = apow(Lc - 1.0 - steps)
    zr = qr[:, :, None, :] * bbr.transpose(0, 2, 1)[:, None] - qi[:, :, None, :] * bbi.transpose(0, 2, 1)[:, None]
    zi = qr[:, :, None, :] * bbi.transpose(0, 2, 1)[:, None] + qi[:, :, None, :] * bbr.transpose(0, 2, 1)[:, None]
    bzmat = jnp.concatenate([zr, zi], axis=-1).reshape(GROUPS, Lc * CG, 2 * STATE)
    lr_, li_ = apow([float(Lc)])
    al = jnp.concatenate([lr_[:, 0], li_[:, 0]], axis=-1)
    return tmat, camat, bzmat, al


def _s5_scan_powers(a_re, a_im, log_dt, n_steps):
    dt = jnp.exp(log_dt)[:, None]
    lr, li = a_re * dt, a_im * dt
    n = (CHUNK * 2.0 ** np.arange(n_steps)).astype(np.float32)[None, :, None]
    mag = jnp.exp(n * lr[:, None, :])
    pr, pi = mag * jnp.cos(n * li[:, None, :]), mag * jnp.sin(n * li[:, None, :])
    fwd = jnp.stack([jnp.concatenate([pr, pr], -1), jnp.concatenate([-pi, pi], -1)], axis=2)
    bwd = jnp.stack([jnp.concatenate([pr, pr], -1), jnp.concatenate([pi, -pi], -1)], axis=2)
    return fwd, bwd


def _shift_rows(x, sh, down):
    n = x.shape[0]
    ri = lax.broadcasted_iota(jnp.int32, x.shape, 0)
    if down:
        return jnp.where(ri >= sh, pltpu.roll(x, sh, 0), 0.0)
    return jnp.where(ri < n - sh, pltpu.roll(x, n - sh, 0), 0.0)


def _s5_fwd(uc, tmat, camat, bzmat, pw):
    g, nch, lw = uc.shape
    n_steps = pw.shape[1]

    def body(u_ref, t_ref, ca_ref, bz_ref, pw_ref, y_ref, xp_ref):
        u = u_ref[...]
        x = _mm32(u, bz_ref[...])
        for kk in range(n_steps):
            xs = _shift_rows(x, 2 ** kk, True)
            m = pw_ref[kk]
            x = x + m[0:1, :] * xs + m[1:2, :] * pltpu.roll(xs, STATE, 1)
        xp = _shift_rows(x, 1, True)
        xp_ref[...] = xp
        y_ref[...] = _mm32(u, t_ref[...]) + _mm32(xp, ca_ref[...])

    per = lambda a: pl.BlockSpec((None,) + a.shape[1:], lambda i: (i,) + (0,) * (a.ndim - 1))
    return pl.pallas_call(
        body, name="s5_fwd", grid=(g,),
        in_specs=[per(uc), per(tmat), per(camat), per(bzmat), per(pw)],
        out_specs=[pl.BlockSpec((None, nch, lw), lambda i: (i, 0, 0)),
                   pl.BlockSpec((None, nch, 2 * STATE), lambda i: (i, 0, 0))],
        out_shape=[jax.ShapeDtypeStruct((g, nch, lw), F32), jax.ShapeDtypeStruct((g, nch, 2 * STATE), F32)],
        compiler_params=_params(("parallel",)),
    )(uc, tmat, camat, bzmat, pw)


def _s5_bwd(uc, dyc, xp, tmat, camat, bzmat, pwc):
    g, nch, lw = uc.shape
    n_steps = pwc.shape[1]

    def body(u_ref, dy_ref, xp_ref, t_ref, ca_ref, bz_ref, pw_ref, du_ref, dt_ref, dca_ref, dbz_ref, dal_ref):
        u = u_ref[...]
        dy = dy_ref[...]
        xpv = xp_ref[...]
        dt_ref[...] = _mm32_tn(u, dy)
        dca_ref[...] = _mm32_tn(xpv, dy)
        dx = _shift_rows(_mm32_nt(dy, ca_ref[...]), 1, False)
        for kk in range(n_steps):
            xs = _shift_rows(dx, 2 ** kk, False)
            m = pw_ref[kk]
            dx = dx + m[0:1, :] * xs + m[1:2, :] * pltpu.roll(xs, STATE, 1)
        dbz_ref[...] = _mm32_tn(u, dx)
        du_ref[...] = _mm32_nt(dy, t_ref[...]) + _mm32_nt(dx, bz_ref[...])
        dal_ref[0:1, :] = jnp.sum(dx * xpv, axis=0, keepdims=True)
        dal_ref[1:2, :] = jnp.sum(dx * pltpu.roll(xpv, STATE, 1), axis=0, keepdims=True)

    per = lambda a: pl.BlockSpec((None,) + a.shape[1:], lambda i: (i,) + (0,) * (a.ndim - 1))
    sds = jax.ShapeDtypeStruct
    outs = [sds((g, nch, lw), F32), sds(tmat.shape, F32), sds(camat.shape, F32), sds(bzmat.shape, F32),
            sds((g, 2, 2 * STATE), F32)]
    return pl.pallas_call(
        body, name="s5_bwd", grid=(g,),
        in_specs=[per(uc), per(dyc), per(xp), per(tmat), per(camat), per(bzmat), per(pwc)],
        out_specs=[per(o) for o in outs], out_shape=outs,
        compiler_params=_params(("parallel",)),
    )(uc, dyc, xp, tmat, camat, bzmat, pwc)


def _to_chunks(u):
    s = u.shape[0]
    return u.reshape(s // CHUNK, CHUNK, GROUPS, CG).transpose(2, 0, 1, 3).reshape(GROUPS, s // CHUNK, CHUNK * CG)


def _from_chunks(y):
    nch = y.shape[1]
    return y.reshape(GROUPS, nch, CHUNK, CG).transpose(1, 2, 0, 3).reshape(nch * CHUNK, GROUPS * CG)


GELU_C0 = math.sqrt(2.0 / math.pi)
GELU_C1 = 0.044715


def _mix(o, za, ys, zb, ga, gb, x, tgt, gate, b_glu, g_final, w_glu, w_up_a, w_up_b, w_out, hsel, ts):
    S = o.shape[0]

    def body(o_ref, za_ref, ys_ref, zb_ref, ga_ref, gb_ref, x_ref, t_ref, gate_ref, bglu_ref, gf_ref,
             wglu_ref, wua_ref, wub_ref, wout_ref, hsel_ref,
             dx2_ref, do_ref, dza_ref, dzb_ref, dga_ref, dgb_ref, dys_ref, dl_ref,
             mg_ref, dmo_ref, ya_ref, dua_ref, yb_ref, dub_ref, yg_ref, dgl_ref,
             dbglu_ref, dgate_ref, dgf_ref, loss_ref):
        @pl.when(pl.program_id(0) == 0)
        def _():
            dbglu_ref[...] = jnp.zeros_like(dbglu_ref)
            dgate_ref[...] = jnp.zeros_like(dgate_ref)
            dgf_ref[...] = jnp.zeros_like(dgf_ref)
            loss_ref[...] = jnp.zeros_like(loss_ref)

        ov = o_ref[...]
        za = za_ref[...]
        sza = _sigmoid(za)
        silu_a = za * sza
        ya = ov * silu_a
        ysv = ys_ref[...]
        th = jnp.tanh(GELU_C0 * (ysv + GELU_C1 * ysv * ysv * ysv))
        yg = 0.5 * ysv * (1.0 + th)
        sg = _sigmoid(_mm(yg, wglu_ref[...]) + bglu_ref[...])
        yb1 = yg * sg
        zb = zb_ref[...]
        szb = _sigmoid(zb)
        silu_b = zb * szb
        yb = yb1 * silu_b
        ua = _mm(ya, wua_ref[...])
        ub = _mm(yb, wub_ref[...])
        sa = _sigmoid(ga_ref[...])
        sb = _sigmoid(gb_ref[...])
        merged = sa * ua + sb * ub
        mo = _mm(merged, wout_ref[...])
        gate_v = gate_ref[...]
        x2 = x_ref[...] + gate_v * mo
        r2 = lax.rsqrt(jnp.mean(x2 * x2, axis=-1, keepdims=True) + EPS)
        x2n = x2 * r2
        gf = gf_ref[...]
        diff = x2n * gf - t_ref[...]
        loss_ref[...] += jnp.sum(jnp.sum(diff * diff, axis=-1, keepdims=True), axis=0, keepdims=True) * (0.5 / D_MODEL)
        dy = diff * (1.0 / D_MODEL)
        dgf_ref[...] += jnp.sum(dy * x2n, axis=0, keepdims=True)
        dyg = dy * gf
        dx2 = r2 * (dyg - x2n * jnp.mean(dyg * x2n, axis=-1, keepdims=True))
        dx2_ref[...] = dx2
        dgate_ref[...] += jnp.sum(dx2 * mo, axis=0, keepdims=True)
        dmo = dx2 * gate_v
        dmerged = _mm_nt(dmo, wout_ref[...])
        dua = dmerged * sa
        dub = dmerged * sb
        dga_ref[...] = (dmerged * ua * sa * (1.0 - sa)).astype(dga_ref.dtype)
        dgb_ref[...] = (dmerged * ub * sb * (1.0 - sb)).astype(dgb_ref.dtype)
        dya = _mm_nt(dua, wua_ref[...])
        dyb = _mm_nt(dub, wub_ref[...])
        dov = dya * silu_a
        do_ref[...] = dov.astype(do_ref.dtype)
        dl_ref[...] = _mm32_nt(hsel_ref[...], dov * ov)
        dza_ref[...] = (dya * ov * (sza * (1.0 + za * (1.0 - sza)))).astype(dza_ref.dtype)
        dyb1 = dyb * silu_b
        dzb_ref[...] = (dyb * yb1 * (szb * (1.0 + zb * (1.0 - szb)))).astype(dzb_ref.dtype)
        dgl = dyb1 * yg * sg * (1.0 - sg)
        dbglu_ref[...] += jnp.sum(dgl, axis=0, keepdims=True)
        dyg2 = dyb1 * sg + _mm_nt(dgl, wglu_ref[...])
        dgelu = 0.5 * (1.0 + th) + 0.5 * ysv * (1.0 - th * th) * GELU_C0 * (1.0 + 3.0 * GELU_C1 * ysv * ysv)
        dys_ref[...] = dyg2 * dgelu
        mg_ref[...] = merged.astype(mg_ref.dtype)
        dmo_ref[...] = dmo.astype(dmo_ref.dtype)
        ya_ref[...] = ya.astype(ya_ref.dtype)
        dua_ref[...] = dua.astype(dua_ref.dtype)
        yb_ref[...] = yb.astype(yb_ref.dtype)
        dub_ref[...] = dub.astype(dub_ref.dtype)
        yg_ref[...] = yg.astype(yg_ref.dtype)
        dgl_ref[...] = dgl.astype(dgl_ref.dtype)

    row = lambda n: pl.BlockSpec((ts, n), lambda i: (i, 0))
    full = lambda a: pl.BlockSpec(a.shape, lambda i: (0,) * a.ndim)
    vec = lambda n: pl.BlockSpec((1, n), lambda i: (0, 0))
    sds = jax.ShapeDtypeStruct
    W, Dm = WIDTH, D_MODEL
    return pl.pallas_call(
        body, name="mix", grid=(S // ts,),
        in_specs=[row(W), row(W), row(W), row(W), row(Dm), row(Dm), row(Dm), row(Dm),
                  full(gate), full(b_glu), full(g_final), full(w_glu), full(w_up_a), full(w_up_b), full(w_out), full(hsel)],
        out_specs=[row(Dm), row(W), row(W), row(W), row(Dm), row(Dm), row(W), pl.BlockSpec((HEADS, ts), lambda i: (0, i)),
                   row(Dm), row(Dm), row(W), row(Dm), row(W), row(Dm), row(W), row(W),
                   vec(W), vec(Dm), vec(Dm), vec(1)],
        out_shape=[sds((S, Dm), F32), sds((S, W), MXU_DTYPE), sds((S, W), MXU_DTYPE), sds((S, W), MXU_DTYPE),
                   sds((S, Dm), MXU_DTYPE), sds((S, Dm), MXU_DTYPE), sds((S, W), F32), sds((HEADS, S), F32),
                   sds((S, Dm), MXU_DTYPE), sds((S, Dm), MXU_DTYPE), sds((S, W), MXU_DTYPE), sds((S, Dm), MXU_DTYPE),
                   sds((S, W), MXU_DTYPE), sds((S, Dm), MXU_DTYPE), sds((S, W), MXU_DTYPE), sds((S, W), MXU_DTYPE),
                   sds((1, W), F32), sds((1, Dm), F32), sds((1, Dm), F32), sds((1, 1), F32)],
        compiler_params=_params(("arbitrary",), VMEM_BIG),
    )(o, za, ys, zb, ga, gb, x, tgt, gate, b_glu, g_final, w_glu, w_up_a, w_up_b, w_out, hsel)


def _matmul_tn(name, a, b, ts, exact=False):
    S, M = a.shape
    N = b.shape[1]
    tn = min(N, 512)
    mm = _mm32_tn if exact else _mm_tn

    def body(a_ref, b_ref, o_ref):
        @pl.when(pl.program_id(1) == 0)
        def _():
            o_ref[...] = jnp.zeros_like(o_ref)

        av, bv = a_ref[...], b_ref[...]
        if exact:
            av, bv = av.astype(F32), bv.astype(F32)
        o_ref[...] += mm(av, bv)

    return pl.pallas_call(
        body, name=name, grid=(N // tn, S // ts),
        in_specs=[pl.BlockSpec((ts, M), lambda j, i: (i, 0)), pl.BlockSpec((ts, tn), lambda j, i: (i, j))],
        out_specs=pl.BlockSpec((M, tn), lambda j, i: (0, j)),
        out_shape=jax.ShapeDtypeStruct((M, N), F32),
        compiler_params=_params(("parallel", "arbitrary"), VMEM_MID),
    )(a, b)


def _proj_bwd(dq, dk, dv, dza, du, dzb, dga, dgb, dfl, x, dx2, shift, scale, g_norm, w_main, w_ft, ts):
    S = x.shape[0]

    def body(dq_ref, dk_ref, dv_ref, dza_ref, du_ref, dzb_ref, dga_ref, dgb_ref, dfl_ref, x_ref, dx2_ref,
             sc_ref, gn_ref, w_ref, wft_ref, gx_ref, dsh_ref, dsc_ref, dgn_ref):
        @pl.when(pl.program_id(0) == 0)
        def _():
            dsh_ref[...] = jnp.zeros_like(dsh_ref)
            dsc_ref[...] = jnp.zeros_like(dsc_ref)
            dgn_ref[...] = jnp.zeros_like(dgn_ref)

        def seg(ref, off, n):
            return _mm_nt(ref[...], w_ref[:, off:off + n])

        dh = (seg(dq_ref, M_Q, WIDTH) + seg(dk_ref, M_K, WIDTH) + seg(dv_ref, M_V, WIDTH)
              + seg(dza_ref, M_ZA, WIDTH) + seg(du_ref, M_U, WIDTH) + seg(dzb_ref, M_ZB, WIDTH)
              + seg(dga_ref, M_GA, D_MODEL) + seg(dgb_ref, M_GB, D_MODEL)
              + _mm32(dfl_ref[...], wft_ref[...]))
        xv = x_ref[...]
        r = lax.rsqrt(jnp.mean(xv * xv, axis=-1, keepdims=True) + EPS)
        xn = xv * r
        gn = gn_ref[...]
        s1 = 1.0 + sc_ref[...]
        dsh_ref[...] += jnp.sum(dh, axis=0, keepdims=True)
        dhx = dh * xn
        dsc_ref[...] += jnp.sum(dhx, axis=0, keepdims=True) * gn
        dgn_ref[...] += jnp.sum(dhx, axis=0, keepdims=True) * s1
        dxn = dh * (gn * s1)
        gx_ref[...] = dx2_ref[...] + r * (dxn - xn * jnp.mean(dxn * xn, axis=-1, keepdims=True))

    row = lambda n: pl.BlockSpec((ts, n), lambda i: (i, 0))
    full = lambda a: pl.BlockSpec(a.shape, lambda i: (0,) * a.ndim)
    vec = pl.BlockSpec((1, D_MODEL), lambda i: (0, 0))
    W, Dm = WIDTH, D_MODEL
    del shift
    return pl.pallas_call(
        body, name="proj_bwd", grid=(S // ts,),
        in_specs=[row(W)] * 6 + [row(Dm)] * 2 + [row(HEADS), row(Dm), row(Dm),
                                                 full(scale), full(g_norm), full(w_main), full(w_ft)],
        out_specs=[row(Dm), vec, vec, vec],
        out_shape=[jax.ShapeDtypeStruct((S, Dm), F32)] + [jax.ShapeDtypeStruct((1, Dm), F32)] * 3,
        compiler_params=_params(("arbitrary",), VMEM_BIG),
    )(dq, dk, dv, dza, du, dzb, dga, dgb, dfl, x, dx2, scale, g_norm, w_main, w_ft)


def _adamw(name, planes, w, m, v, tr):
    n, R, C = planes.shape
    bc1 = 1.0 - ADAM_B1 ** ADAM_STEP
    bc2 = 1.0 - ADAM_B2 ** ADAM_STEP

    def body(p_ref, w_ref, m_ref, v_ref, g_ref, d_ref, nm_ref, nv_ref):
        g = p_ref[0]
        for i in range(1, n):
            g = g + p_ref[i]
        g_ref[...] = g
        nm = ADAM_B1 * m_ref[...] + (1.0 - ADAM_B1) * g
        nv = ADAM_B2 * v_ref[...] + (1.0 - ADAM_B2) * (g * g)
        nm_ref[...] = nm
        nv_ref[...] = nv
        d_ref[...] = -ADAM_LR * ((nm / bc1) / (jnp.sqrt(nv / bc2) + ADAM_EPS) + ADAM_WD * w_ref[...])

    blk = pl.BlockSpec((tr, C), lambda i: (i, 0))
    return pl.pallas_call(
        body, name=name, grid=(R // tr,),
        in_specs=[pl.BlockSpec((n, tr, C), lambda i: (0, i, 0)), blk, blk, blk],
        out_specs=[blk] * 4, out_shape=[jax.ShapeDtypeStruct((R, C), F32)] * 4,
        compiler_params=_params(("parallel",), VMEM_MID),
    )(planes, w, m, v)


def _wada_grad(c_all, dmod_cols):
    def body(c_ref, d_ref, o_ref):
        o_ref[0] = _mm32_tn(c_ref[...], d_ref[...])

    return pl.pallas_call(
        body, name="wada_grad",
        out_shape=jax.ShapeDtypeStruct((1, c_all.shape[1], dmod_cols.shape[1]), F32),
        in_specs=[VMEM, VMEM], out_specs=VMEM,
    )(c_all, dmod_cols)


SMALL_ORDER = ("b_ada", "g_norm", "b_f", "a_re", "a_im", "log_dt", "b_re", "b_im", "c_re", "c_im",
               "d_skip", "b_glu", "g_final")
BIG_ORDER = ("w_ada", "w_in", "w_glu", "w_up_a", "w_up_b", "w_out")
ALL_ORDER = ("w_ada", "b_ada", "g_norm", "w_in", "b_f", "a_re", "a_im", "log_dt", "b_re", "b_im", "c_re", "c_im",
             "d_skip", "w_glu", "b_glu", "w_up_a", "w_up_b", "w_out", "g_final")


def _pack_small(parts, rows):
    flat = jnp.concatenate([p.reshape(-1).astype(F32) for p in parts])
    return jnp.pad(flat, (0, rows * 128 - flat.shape[0])).reshape(rows, 128)


def kernel(x, c, w_ada, b_ada, g_norm, w_in, b_f, a_re, a_im, log_dt, b_re, b_im, c_re, c_im, d_skip, w_glu, b_glu, w_up_a, w_up_b, w_out, g_final, loss_target, m_w_ada, m_b_ada, m_g_norm, m_w_in, m_b_f, m_a_re, m_a_im, m_log_dt, m_b_re, m_b_im, m_c_re, m_c_im, m_d_skip, m_w_glu, m_b_glu, m_w_up_a, m_w_up_b, m_w_out, m_g_final, v_w_ada, v_b_ada, v_g_norm, v_w_in, v_b_f, v_a_re, v_a_im, v_log_dt, v_b_re, v_b_im, v_c_re, v_c_im, v_d_skip, v_w_glu, v_b_glu, v_w_up_a, v_w_up_b, v_w_out, v_g_final):
    weights = dict(w_ada=w_ada, b_ada=b_ada, g_norm=g_norm, w_in=w_in, b_f=b_f, a_re=a_re, a_im=a_im, log_dt=log_dt,
                   b_re=b_re, b_im=b_im, c_re=c_re, c_im=c_im, d_skip=d_skip, w_glu=w_glu, b_glu=b_glu,
                   w_up_a=w_up_a, w_up_b=w_up_b, w_out=w_out, g_final=g_final)
    mom_m = dict(w_ada=m_w_ada, b_ada=m_b_ada, g_norm=m_g_norm, w_in=m_w_in, b_f=m_b_f, a_re=m_a_re, a_im=m_a_im,
                 log_dt=m_log_dt, b_re=m_b_re, b_im=m_b_im, c_re=m_c_re, c_im=m_c_im, d_skip=m_d_skip, w_glu=m_w_glu,
                 b_glu=m_b_glu, w_up_a=m_w_up_a, w_up_b=m_w_up_b, w_out=m_w_out, g_final=m_g_final)
    mom_v = dict(w_ada=v_w_ada, b_ada=v_b_ada, g_norm=v_g_norm, w_in=v_w_in, b_f=v_b_f, a_re=v_a_re, a_im=v_a_im,
                 log_dt=v_log_dt, b_re=v_b_re, b_im=v_b_im, c_re=v_c_re, c_im=v_c_im, d_skip=v_d_skip, w_glu=v_w_glu,
                 b_glu=v_b_glu, w_up_a=v_w_up_a, w_up_b=v_w_up_b, w_out=v_w_out, g_final=v_g_final)
    xs = x[0]
    tgt = loss_target[0]
    S = xs.shape[0]
    ts = min(256, S)
    ta = min(512, S)
    tw = min(1024, S)
    nch = S // CHUNK
    n_steps = max(1, int(math.ceil(math.log2(nch))))
    me = _my_index()

    shards = [w_in[0], w_glu[0], w_up_a[0], w_up_b[0], w_out[0]]
    mod8, c_all, gathered = _comm_in(c, w_ada[0], b_ada.reshape(N_DEV, -1), shards)
    mod = mod8.reshape(1, 3 * D_MODEL)
    shift, scale, gate = mod[:, :D_MODEL], mod[:, D_MODEL:2 * D_MODEL], mod[:, 2 * D_MODEL:]
    w_in_full = gathered[0].transpose(1, 0, 2).reshape(D_MODEL, PROJ_WIDTH)
    w_main = jnp.concatenate([w_in_full[:, :OFF_F], w_in_full[:, OFF_F + HEADS:]], axis=1).astype(MXU_DTYPE)
    w_f = w_in_full[:, OFF_F:OFF_F + HEADS]
    w_ft = w_f.T
    w_glu_full = gathered[1].reshape(WIDTH, WIDTH).astype(MXU_DTYPE)
    w_up_a_full = gathered[2].transpose(1, 0, 2).reshape(WIDTH, D_MODEL).astype(MXU_DTYPE)
    w_up_b_full = gathered[3].transpose(1, 0, 2).reshape(WIDTH, D_MODEL).astype(MXU_DTYPE)
    w_out_full = gathered[4].reshape(D_MODEL, D_MODEL).astype(MXU_DTYPE)

    q, k, v, za, u, zb, ga, gb, flc, flr, hb = _proj_fwd(xs, shift, scale, g_norm, w_main, w_f, w_ft, ts)
    fcol, frow = _fgate_fwd(flc, flr, b_f, b_f.reshape(HEADS, 1), ta)
    nb = S // ta
    rows4 = lambda r: r.reshape(PAIRS, 2, nb, ta).transpose(0, 2, 1, 3)
    fcol4 = fcol.reshape(S, PAIRS, 2).transpose(1, 0, 2)
    frow4 = rows4(frow)
    o, lse_full = _attn_fwd(q, k, v, fcol4, frow4, ta)

    s5_params = (a_re[0], a_im[0], log_dt[0], b_re[0], b_im[0], c_re[0], c_im[0], d_skip[0])
    (tmat, camat, bzmat, al), mats_vjp = jax.vjp(_s5_mats, *s5_params)
    del al
    pw_f, pw_b = _s5_scan_powers(a_re[0], a_im[0], log_dt[0], n_steps)
    uc = _to_chunks(u)
    yc, xprev = _s5_fwd(uc, tmat, camat, bzmat, pw_f)
    ys = _from_chunks(yc)

    hsel = (np.arange(WIDTH)[None, :] // 64 == np.arange(HEADS)[:, None]).astype(np.float32)
    (dx2, do, dza, dzb, dga, dgb, dys, dl_row, merged, dmo, ya, dua, yb, dub, yg, dgl,
     db_glu, dgate, dg_final, loss_part) = _mix(o, za, ys, zb, ga, gb, xs, tgt, gate, b_glu, g_final.reshape(1, -1),
                                                w_glu_full, w_up_a_full, w_up_b_full, w_out_full, jnp.asarray(hsel), ts)

    gw_out = _matmul_tn("dw_out", merged, dmo, tw)
    gw_up_a = _matmul_tn("dw_up_a", ya, dua, tw)
    gw_up_b = _matmul_tn("dw_up_b", yb, dub, tw)
    gw_glu = _matmul_tn("dw_glu", yg, dgl, tw)

    duc, d_tmat, d_camat, d_bzmat, dal2 = _s5_bwd(uc, _to_chunks(dys), xprev, tmat, camat, bzmat, pw_b)
    du = _from_chunks(duc)
    d_al = jnp.concatenate([dal2[:, 0, :STATE] + dal2[:, 0, STATE:], dal2[:, 1, STATE:] - dal2[:, 1, :STATE]], axis=-1)
    gs5 = mats_vjp((d_tmat, d_camat, d_bzmat, d_al))

    lse4 = rows4(lse_full[:, ::64].T)
    dl4 = rows4(dl_row)
    dq, dk, dv, dfq4, dfk4 = _attn_bwd(q, do, k, v, fcol4, frow4, lse4, dl4, ta)
    d_fcol = dfq4.transpose(0, 2, 1, 3).reshape(HEADS, S).T - dfk4.transpose(1, 0, 2).reshape(S, HEADS)
    dfl, db_f = _fgate_bwd(d_fcol, flc, b_f, ta)

    grad_x, dshift, dscale, dg_norm = _proj_bwd(dq, dk, dv, dza, du, dzb, dga, dgb, dfl, xs, dx2,
                                                shift, scale, g_norm, w_main, w_ft, ts)
    segs = [("dw_q", dq), ("dw_k", dk), ("dw_v", dv), ("dw_f", dfl), ("dw_za", dza), ("dw_u", du), ("dw_zb", dzb),
            ("dw_ga", dga), ("dw_gb", dgb)]
    gw_in = jnp.concatenate([_matmul_tn(nm, hb, d, tw, exact=(nm == "dw_f")) for nm, d in segs], axis=1)

    planes = [gw_in.reshape(D_MODEL, N_DEV, -1).transpose(1, 0, 2),
              gw_glu.reshape(N_DEV, -1, WIDTH),
              gw_up_a.reshape(WIDTH, N_DEV, -1).transpose(1, 0, 2),
              gw_up_b.reshape(WIDTH, N_DEV, -1).transpose(1, 0, 2),
              gw_out.reshape(N_DEV, -1, D_MODEL)]
    dmod = jnp.concatenate([dshift, dscale, dgate], axis=1)
    small_parts = [dmod, dg_norm, db_f, gs5[0], gs5[1], gs5[2], gs5[3], gs5[4], gs5[5], gs5[6], gs5[7],
                   db_glu, dg_final, loss_part]
    n_small = sum(int(np.prod(p.shape)) for p in small_parts)
    rows = -(-n_small // (8 * 128)) * 8
    small = _pack_small(small_parts, rows)
    recv, small_all = _comm_out(planes, small)

    grads, deltas, new_m, new_v = {}, {}, {}, {}

    def put(name, res, shape):
        grads[name], deltas[name], new_m[name], new_v[name] = [r.reshape(shape) for r in res]

    names = ("w_in", "w_glu", "w_up_a", "w_up_b", "w_out")
    for name, pr in zip(names, recv):
        w2 = weights[name][0]
        tr = 256 if w2.shape[0] % 256 == 0 else w2.shape[0]
        put(name, _adamw("adamw_" + name, pr, w2, mom_m[name][0], mom_v[name][0], tr), weights[name].shape)
    cols = w_ada.shape[2]
    dmod_all = small_all[:, :24, :].reshape(N_DEV, 3 * D_MODEL)
    dmod_cols = lax.dynamic_slice_in_dim(dmod_all, me * cols, cols, axis=1)
    g_wada = _wada_grad(c_all, dmod_cols)
    put("w_ada", _adamw("adamw_w_ada", g_wada, w_ada[0], m_w_ada[0], v_w_ada[0], 256), w_ada.shape)
    pack = lambda d: _pack_small([d[n] for n in SMALL_ORDER] + [jnp.zeros((1,), F32)], rows)
    res_small = _adamw("adamw_small", small_all, pack(weights), pack(mom_m), pack(mom_v), rows)
    flat = [r.reshape(-1) for r in res_small]
    off = 0
    for name in SMALL_ORDER:
        shape = weights[name].shape
        size = int(np.prod(shape))
        put(name, [f[off:off + size] for f in flat], shape)
        off += size
    loss = flat[0][off]

    return (loss, grad_x[None], *[grads[n] for n in ALL_ORDER], *[deltas[n] for n in ALL_ORDER],
            *[new_m[n] for n in ALL_ORDER], *[new_v[n] for n in ALL_ORDER])
```

```python
import functools
import math

import jax
import jax.numpy as jnp
import numpy as np
from jax import lax
from jax.experimental import pallas as pl
from jax.experimental.pallas import tpu as pltpu

F32 = jnp.float32
MXU_DTYPE = jnp.bfloat16
HI = lax.Precision.HIGHEST

N_DEV = 8
D_MODEL = 1024
WIDTH = 512
HEADS = 8
PAIRS = HEADS // 2
GROUPS = 32
STATE = 64
CG = 16
CHUNK = 16
EPS = 1e-6
NEG = float(np.finfo(np.float32).min)

ADAM_LR = 0.001
ADAM_B1 = 0.9
ADAM_B2 = 0.999
ADAM_EPS = 1e-08
ADAM_WD = 0.01
ADAM_STEP = 10

VMEM_BIG = 56 * 1024 * 1024
VMEM_MID = 40 * 1024 * 1024

OFF_F = 3 * WIDTH
PROJ_WIDTH = 5128
M_Q, M_K, M_V, M_ZA, M_U, M_ZB, M_GA, M_GB = 0, 512, 1024, 1536, 2048, 2560, 3072, 4096


def _mm(a, b):
    return jnp.dot(a.astype(MXU_DTYPE), b.astype(MXU_DTYPE), preferred_element_type=F32)


def _mm_nt(a, b):
    return lax.dot_general(a.astype(MXU_DTYPE), b.astype(MXU_DTYPE), (((1,), (1,)), ((), ())),
                           preferred_element_type=F32)


def _mm_tn(a, b):
    return lax.dot_general(a.astype(MXU_DTYPE), b.astype(MXU_DTYPE), (((0,), (0,)), ((), ())),
                           preferred_element_type=F32)


def _mm32(a, b):
    return jnp.dot(a, b, precision=HI, preferred_element_type=F32)


def _mm32_nt(a, b):
    return lax.dot_general(a, b, (((1,), (1,)), ((), ())), precision=HI, preferred_element_type=F32)


def _mm32_tn(a, b):
    return lax.dot_general(a, b, (((0,), (0,)), ((), ())), precision=HI, preferred_element_type=F32)


def _sigmoid(x):
    return 1.0 / (1.0 + jnp.exp(-x))


def _params(sem=None, vmem=None):
    kw = {}
    if sem is not None:
        kw["dimension_semantics"] = sem
    if vmem is not None:
        kw["vmem_limit_bytes"] = vmem
    return pltpu.CompilerParams(**kw)


def _my_index():
    return 4 * lax.axis_index("x") + 2 * lax.axis_index("y") + lax.axis_index("c")


def _dev(p):
    return (p // 4, (p // 2) % 2, p % 2)


ANY = pl.BlockSpec(memory_space=pl.ANY)
VMEM = pl.BlockSpec(memory_space=pltpu.VMEM)
MESH = pl.DeviceIdType.MESH


def _comm_in(c, w_ada, b_ada8, shards):
    n = len(shards)
    cols = w_ada.shape[1]

    def body(c_ref, wada_ref, bada_ref, *rest):
        srcs = rest[:n]
        mod_ref, call_ref = rest[n], rest[n + 1]
        dsts = rest[n + 2:2 * n + 2]
        modp, wsend, wrecv, wloc, csend, crecv, msend, mrecv = rest[2 * n + 2:]
        me = _my_index()

        def wcopy(a, src_dev, d, to):
            return pltpu.make_async_remote_copy(src_ref=srcs[a], dst_ref=dsts[a].at[src_dev],
                                                send_sem=wsend.at[a, d], recv_sem=wrecv.at[a, src_dev],
                                                device_id=_dev(to), device_id_type=MESH)

        def ccopy(src_dev, d, to):
            return pltpu.make_async_remote_copy(src_ref=c_ref, dst_ref=call_ref.at[pl.ds(src_dev, 1)],
                                                send_sem=csend.at[d], recv_sem=crecv.at[src_dev],
                                                device_id=_dev(to), device_id_type=MESH)

        def mcopy(src_dev, d, to):
            return pltpu.make_async_remote_copy(src_ref=modp.at[pl.ds(to, 1)], dst_ref=mod_ref.at[pl.ds(src_dev, 1)],
                                                send_sem=msend.at[d], recv_sem=mrecv.at[src_dev],
                                                device_id=_dev(to), device_id_type=MESH)

        local = [pltpu.make_async_copy(srcs[a], dsts[a].at[me], wloc.at[a]) for a in range(n)]
        for cp in local:
            cp.start()
        peers = [(me + d) % N_DEV for d in range(1, N_DEV)]
        for d, p in enumerate(peers):
            for a in range(n):
                wcopy(a, me, d, p).start()
        call_ref[pl.ds(me, 1), :] = c_ref[...]
        for d, p in enumerate(peers):
            ccopy(me, d, p).start()
        for d, p in enumerate(peers):
            ccopy(p, d, p).wait_recv()
        modp[...] = _mm32(call_ref[...], wada_ref[...]) + bada_ref[pl.ds(me, 1), :]
        mod_ref[pl.ds(me, 1), :] = modp[pl.ds(me, 1), :]
        for d, p in enumerate(peers):
            mcopy(me, d, p).start()
        for d, p in enumerate(peers):
            mcopy(p, d, p).wait_recv()
        for d, p in enumerate(peers):
            for a in range(n):
                wcopy(a, p, d, p).wait_recv()
        for d, p in enumerate(peers):
            for a in range(n):
                wcopy(a, me, d, p).wait_send()
            ccopy(me, d, p).wait_send()
            mcopy(me, d, p).wait_send()
        for cp in local:
            cp.wait()

    out_shape = ([jax.ShapeDtypeStruct((N_DEV, cols), F32), jax.ShapeDtypeStruct((N_DEV, D_MODEL), F32)]
                 + [jax.ShapeDtypeStruct((N_DEV,) + s.shape, s.dtype) for s in shards])
    res = pl.pallas_call(
        body, name="comm_in", out_shape=out_shape,
        in_specs=[VMEM, VMEM, VMEM] + [ANY] * n,
        out_specs=[VMEM, VMEM] + [ANY] * n,
        scratch_shapes=[pltpu.VMEM((N_DEV, cols), F32),
                        pltpu.SemaphoreType.DMA((n, N_DEV)), pltpu.SemaphoreType.DMA((n, N_DEV)),
                        pltpu.SemaphoreType.DMA((n,)),
                        pltpu.SemaphoreType.DMA((N_DEV,)), pltpu.SemaphoreType.DMA((N_DEV,)),
                        pltpu.SemaphoreType.DMA((N_DEV,)), pltpu.SemaphoreType.DMA((N_DEV,))],
        compiler_params=_params(vmem=VMEM_MID),
    )(c, w_ada, b_ada8, *shards)
    return res[0], res[1], list(res[2:])


def _comm_out(planes, small):
    n = len(planes)

    def body(*rest):
        srcs = rest[:n]
        small_ref = rest[n]
        dsts = rest[n + 1:2 * n + 1]
        sall_ref = rest[2 * n + 1]
        wsend, wrecv, wloc, ssend, srecv, sloc = rest[2 * n + 2:]
        me = _my_index()

        def wcopy(a, src_dev, d, to):
            return pltpu.make_async_remote_copy(src_ref=srcs[a].at[to], dst_ref=dsts[a].at[src_dev],
                                                send_sem=wsend.at[a, d], recv_sem=wrecv.at[a, src_dev],
                                                device_id=_dev(to), device_id_type=MESH)

        def scopy(src_dev, d, to):
            return pltpu.make_async_remote_copy(src_ref=small_ref, dst_ref=sall_ref.at[src_dev],
                                                send_sem=ssend.at[d], recv_sem=srecv.at[src_dev],
                                                device_id=_dev(to), device_id_type=MESH)

        local = [pltpu.make_async_copy(srcs[a].at[me], dsts[a].at[me], wloc.at[a]) for a in range(n)]
        local.append(pltpu.make_async_copy(small_ref, sall_ref.at[me], sloc))
        for cp in local:
            cp.start()
        peers = [(me + d) % N_DEV for d in range(1, N_DEV)]
        for d, p in enumerate(peers):
            scopy(me, d, p).start()
            for a in range(n):
                wcopy(a, me, d, p).start()
        for d, p in enumerate(peers):
            scopy(p, d, p).wait_recv()
            for a in range(n):
                wcopy(a, p, d, p).wait_recv()
        for d, p in enumerate(peers):
            scopy(me, d, p).wait_send()
            for a in range(n):
                wcopy(a, me, d, p).wait_send()
        for cp in local:
            cp.wait()

    out_shape = ([jax.ShapeDtypeStruct(p.shape, p.dtype) for p in planes]
                 + [jax.ShapeDtypeStruct((N_DEV,) + small.shape, small.dtype)])
    res = pl.pallas_call(
        body, name="comm_out", out_shape=out_shape,
        in_specs=[ANY] * (n + 1), out_specs=[ANY] * (n + 1),
        scratch_shapes=[pltpu.SemaphoreType.DMA((n, N_DEV)), pltpu.SemaphoreType.DMA((n, N_DEV)),
                        pltpu.SemaphoreType.DMA((n,)),
                        pltpu.SemaphoreType.DMA((N_DEV,)), pltpu.SemaphoreType.DMA((N_DEV,)),
                        pltpu.SemaphoreType.DMA(())],
    )(*planes, small)
    return list(res[:n]), res[n]


def _proj_fwd(x, shift, scale, g_norm, w_main, w_f, ts):
    S = x.shape[0]

    def body(x_ref, sh_ref, sc_ref, gn_ref, w_ref, wf_ref,
             q_ref, k_ref, v_ref, za_ref, u_ref, zb_ref, ga_ref, gb_ref, flc_ref, h_ref):
        xv = x_ref[...]
        r = lax.rsqrt(jnp.mean(xv * xv, axis=-1, keepdims=True) + EPS)
        h = (xv * r) * gn_ref[...] * (1.0 + sc_ref[...]) + sh_ref[...]
        hb = h.astype(MXU_DTYPE)
        h_ref[...] = hb

        def seg(off, n):
            return jnp.dot(hb, w_ref[:, off:off + n], preferred_element_type=F32)

        q_ref[...] = (seg(M_Q, WIDTH) * 0.125).astype(q_ref.dtype)
        k_ref[...] = seg(M_K, WIDTH).astype(k_ref.dtype)
        v_ref[...] = seg(M_V, WIDTH).astype(v_ref.dtype)
        za_ref[...] = seg(M_ZA, WIDTH)
        u_ref[...] = seg(M_U, WIDTH)
        zb_ref[...] = seg(M_ZB, WIDTH)
        ga_ref[...] = seg(M_GA, D_MODEL)
        gb_ref[...] = seg(M_GB, D_MODEL)
        flc_ref[...] = _mm32(h, wf_ref[...])

    row = lambda n: pl.BlockSpec((ts, n), lambda i: (i, 0))
    full = lambda a: pl.BlockSpec(a.shape, lambda i: (0,) * a.ndim)
    sds = jax.ShapeDtypeStruct
    return pl.pallas_call(
        body, name="proj_fwd", grid=(S // ts,),
        in_specs=[row(D_MODEL), full(shift), full(scale), full(g_norm), full(w_main), full(w_f)],
        out_specs=[row(WIDTH)] * 6 + [row(D_MODEL)] * 2 + [row(HEADS), row(D_MODEL)],
        out_shape=[sds((S, WIDTH), MXU_DTYPE)] * 3 + [sds((S, WIDTH), F32)] * 3 + [sds((S, D_MODEL), F32)] * 2
                  + [sds((S, HEADS), F32), sds((S, D_MODEL), MXU_DTYPE)],
        compiler_params=_params(("parallel",), VMEM_BIG),
    )(x, shift, scale, g_norm, w_main, w_f)


def _log_sigmoid(z):
    return jnp.minimum(z, 0.0) - jnp.log(1.0 + jnp.exp(-jnp.abs(z)))


def _fgate_fwd(flc, bf_row, ts):
    S = flc.shape[0]

    def body(flc_ref, bfr_ref, fc_ref, carry_c):
        @pl.when(pl.program_id(0) == 0)
        def _():
            carry_c[...] = jnp.zeros_like(carry_c)

        ri = lax.broadcasted_iota(jnp.int32, (ts, ts), 0)
        ci = lax.broadcasted_iota(jnp.int32, (ts, ts), 1)
        lower = (ci <= ri).astype(F32)
        fc = _mm32(lower, _log_sigmoid(flc_ref[...] + bfr_ref[...])) + carry_c[...]
        fc_ref[...] = fc
        carry_c[...] = fc[ts - 1:ts, :]

    col = pl.BlockSpec((ts, HEADS), lambda i: (i, 0))
    return pl.pallas_call(
        body, name="fgate_fwd", grid=(S // ts,),
        in_specs=[col, pl.BlockSpec((1, HEADS), lambda i: (0, 0))],
        out_specs=col, out_shape=jax.ShapeDtypeStruct((S, HEADS), F32),
        scratch_shapes=[pltpu.VMEM((1, HEADS), F32)],
        compiler_params=_params(("arbitrary",)),
    )(flc, bf_row)


def _fgate_bwd(dfc, flc, bf_row, ts):
    S = flc.shape[0]
    n = S // ts

    def body(df_ref, flc_ref, bfr_ref, dfl_ref, dbf_ref, carry):
        @pl.when(pl.program_id(0) == 0)
        def _():
            carry[...] = jnp.zeros_like(carry)
            dbf_ref[...] = jnp.zeros_like(dbf_ref)

        ri = lax.broadcasted_iota(jnp.int32, (ts, ts), 0)
        ci = lax.broadcasted_iota(jnp.int32, (ts, ts), 1)
        upper = (ci >= ri).astype(F32)
        rc = _mm32(upper, df_ref[...]) + carry[...]
        carry[...] = rc[0:1, :]
        z = flc_ref[...] + bfr_ref[...]
        dfl = rc * _sigmoid(-z)
        dfl_ref[...] = dfl
        dbf_ref[...] += jnp.sum(dfl, axis=0, keepdims=True)

    col = pl.BlockSpec((ts, HEADS), lambda i: (n - 1 - i, 0))
    one = pl.BlockSpec((1, HEADS), lambda i: (0, 0))
    return pl.pallas_call(
        body, name="fgate_bwd", grid=(n,),
        in_specs=[col, col, one], out_specs=[col, one],
        out_shape=[jax.ShapeDtypeStruct((S, HEADS), F32), jax.ShapeDtypeStruct((1, HEADS), F32)],
        scratch_shapes=[pltpu.VMEM((1, HEADS), F32)],
        compiler_params=_params(("arbitrary",)),
    )(dfc, flc, bf_row)


N_EXTRA = 3


def _attn_prep(q, k, v, fcol, t):
    S = q.shape[0]
    nb = S // t

    def body(q_ref, k_ref, v_ref, f_ref, qh_ref, kh_ref, vt_ref):
        lane = lax.broadcasted_iota(jnp.int32, (t, 128), 1)
        f = f_ref[...]
        for p in range(PAIRS):
            qp = q_ref[:, p * 128:(p + 1) * 128]
            kp = k_ref[:, p * 128:(p + 1) * 128]
            vt_ref[p, 0] = v_ref[:, p * 128:(p + 1) * 128].T
            for h in range(2):
                own = (lane < 64) if h == 0 else (lane >= 64)
                base = 64 if h == 0 else 0
                fh = f[:, 2 * p + h:2 * p + h + 1]
                parts = []
                rest = fh
                for _ in range(N_EXTRA):
                    part = rest.astype(qh_ref.dtype)
                    parts.append(part)
                    rest = rest - part.astype(F32)
                one = jnp.ones((t, 1), qh_ref.dtype)
                eq = jnp.zeros((t, 128), qh_ref.dtype)
                ek = jnp.zeros((t, 128), qh_ref.dtype)
                for j in range(N_EXTRA):
                    eq = jnp.where(lane == base + j, parts[j], eq)
                    eq = jnp.where(lane == base + N_EXTRA + j, one, eq)
                    ek = jnp.where(lane == base + j, one, ek)
                    ek = jnp.where(lane == base + N_EXTRA + j, -parts[j], ek)
                qh_ref[2 * p + h] = jnp.where(own, qp, eq)
                kh_ref[2 * p + h] = jnp.where(own, kp, ek)

    row = pl.BlockSpec((t, WIDTH), lambda i: (i, 0))
    heads = pl.BlockSpec((HEADS, t, 128), lambda i: (0, i, 0))
    return pl.pallas_call(
        body, name="attn_prep", grid=(nb,),
        in_specs=[row, row, row, pl.BlockSpec((t, HEADS), lambda i: (i, 0))],
        out_specs=[heads, heads, pl.BlockSpec((PAIRS, 1, 128, t), lambda i: (0, i, 0, 0))],
        out_shape=[jax.ShapeDtypeStruct((HEADS, S, 128), q.dtype), jax.ShapeDtypeStruct((HEADS, S, 128), k.dtype),
                   jax.ShapeDtypeStruct((PAIRS, nb, 128, t), v.dtype)],
        compiler_params=_params(("parallel",), VMEM_MID),
    )(q, k, v, fcol)


def _attn_fwd(qh, kh, vt, t):
    S = qh.shape[1]
    nb = S // t

    def body(q_ref, k_ref, vt_ref, o_ref, lse_ref, m_s, l_s, acc_s):
        qi = pl.program_id(1)
        m_s[...] = jnp.full(m_s.shape, -jnp.inf, F32)
        l_s[...] = jnp.zeros_like(l_s)
        acc_s[...] = jnp.zeros_like(acc_s)

        def step(ki, masked):
            ks = pl.multiple_of(ki * t, t)
            vtb = vt_ref[ki]
            for h in range(2):
                st = _mm_nt(k_ref[h, pl.ds(ks, t), :], q_ref[h])
                if masked:
                    ri = lax.broadcasted_iota(jnp.int32, (t, t), 0)
                    ci = lax.broadcasted_iota(jnp.int32, (t, t), 1)
                    st = jnp.where(ci >= ri, st, NEG)
                m_old = m_s[h]
                m_new = jnp.maximum(m_old, jnp.max(st, axis=0, keepdims=True))
                alpha = jnp.exp(m_old - m_new)
                pt = jnp.exp(st - m_new)
                l_s[h] = alpha * l_s[h] + jnp.sum(pt, axis=0, keepdims=True)
                m_s[h] = m_new
                acc_s[h] = alpha * acc_s[h] + _mm(vtb, pt)

        def loop_body(ki, carry):
            step(ki, False)
            return carry

        lax.fori_loop(0, qi, loop_body, 0)
        step(qi, True)
        first = lax.broadcasted_iota(jnp.int32, (128, t), 0) < 64
        o_ref[...] = jnp.where(first, acc_s[0] / l_s[0], acc_s[1] / l_s[1]).T
        lse_ref[...] = jnp.concatenate([m_s[0] + jnp.log(l_s[0]), m_s[1] + jnp.log(l_s[1])], axis=0)

    return pl.pallas_call(
        body, name="attn_fwd", grid=(PAIRS, nb),
        in_specs=[pl.BlockSpec((2, t, 128), lambda p, i: (p, i, 0)), pl.BlockSpec((2, S, 128), lambda p, i: (p, 0, 0)),
                  pl.BlockSpec((None, nb, 128, t), lambda p, i: (p, 0, 0, 0))],
        out_specs=[pl.BlockSpec((t, 128), lambda p, i: (i, p)), pl.BlockSpec((None, None, 2, t), lambda p, i: (p, i, 0, 0))],
        out_shape=[jax.ShapeDtypeStruct((S, WIDTH), F32), jax.ShapeDtypeStruct((PAIRS, nb, 2, t), F32)],
        scratch_shapes=[pltpu.VMEM((2, 1, t), F32), pltpu.VMEM((2, 1, t), F32), pltpu.VMEM((2, 128, t), F32)],
        compiler_params=_params(("parallel", "parallel"), VMEM_MID),
    )(qh, kh, vt)


def _attn_bwd(qh, do, kh, v, lse4, dl4, t):
    S = qh.shape[1]
    nb = S // t

    def body(q_ref, do_ref, k_ref, v_ref, lse_ref, dl_ref,
             dq_ref, dk_ref, dv_ref, dfq_ref, dfk_ref, kc_s, vh_s, dk_s, dv_s, dfk_s):
        kj = pl.program_id(1)
        lane = lax.broadcasted_iota(jnp.int32, (t, 128), 1)
        is_a = lane < 64

        @pl.when(kj == 0)
        def _():
            dq_ref[...] = jnp.zeros_like(dq_ref)
            dfq_ref[...] = jnp.zeros_like(dfq_ref)

        vp = v_ref[...]
        zero = jnp.zeros_like(vp)
        kc_s[0] = jnp.where(is_a, k_ref[0], zero.astype(kc_s.dtype))
        kc_s[1] = jnp.where(is_a, zero.astype(kc_s.dtype), k_ref[1])
        vh_s[0] = jnp.where(is_a, vp, zero)
        vh_s[1] = jnp.where(is_a, zero, vp)
        dk_s[...] = jnp.zeros_like(dk_s)
        dv_s[...] = jnp.zeros_like(dv_s)
        dfk_s[...] = jnp.zeros_like(dfk_s)

        def step(qi, masked):
            qs = pl.multiple_of(qi * t, t)
            dob = do_ref[pl.ds(qs, t), :]
            lse = lse_ref[qi]
            dl = dl_ref[qi]
            zq = jnp.zeros_like(dob)
            over_keys = []
            for h in range(2):
                sel = is_a if h == 0 else jnp.logical_not(is_a)
                qb = q_ref[h, pl.ds(qs, t), :]
                st = _mm_nt(k_ref[h], qb) - lse[h:h + 1, :]
                if masked:
                    ri = lax.broadcasted_iota(jnp.int32, (t, t), 0)
                    ci = lax.broadcasted_iota(jnp.int32, (t, t), 1)
                    st = jnp.where(ci >= ri, st, NEG)
                pt = jnp.exp(st)
                dv_s[...] += _mm(pt, jnp.where(sel, dob, zq))
                dpt = _mm_nt(vh_s[h], dob)
                dst = pt * (dpt - dl[h:h + 1, :])
                dfk_s[h] += jnp.sum(dst, axis=1, keepdims=True)
                over_keys.append(jnp.sum(dst, axis=0, keepdims=True))
                dk_s[...] += _mm(dst, jnp.where(sel, qb, jnp.zeros_like(qb)))
                dq_ref[pl.ds(qs, t), :] += _mm_tn(dst, kc_s[h])
            dfq_ref[qi] += jnp.concatenate(over_keys, axis=0)

        step(kj, True)

        def loop_body(qi, carry):
            step(qi, False)
            return carry

        lax.fori_loop(kj + 1, nb, loop_body, 0)
        dk_ref[...] = dk_s[...].astype(dk_ref.dtype)
        dv_ref[...] = dv_s[...].astype(dv_ref.dtype)
        dfk_ref[...] = jnp.where(lax.broadcasted_iota(jnp.int32, (t, 2), 1) == 0, dfk_s[0], dfk_s[1])

        @pl.when(kj == nb - 1)
        def _():
            dq_ref[...] = dq_ref[...] * 0.125

    blk = pl.BlockSpec((t, 128), lambda p, j: (j, p))
    res = pl.BlockSpec((S, 128), lambda p, j: (0, p))
    rows4 = pl.BlockSpec((None, nb, 2, t), lambda p, j: (p, 0, 0, 0))
    cols4 = pl.BlockSpec((None, t, 2), lambda p, j: (p, j, 0))
    return pl.pallas_call(
        body, name="attn_bwd", grid=(PAIRS, nb),
        in_specs=[pl.BlockSpec((2, S, 128), lambda p, j: (p, 0, 0)), res,
                  pl.BlockSpec((2, t, 128), lambda p, j: (p, j, 0)), blk, rows4, rows4],
        out_specs=[res, blk, blk, rows4, cols4],
        out_shape=[jax.ShapeDtypeStruct((S, WIDTH), F32), jax.ShapeDtypeStruct((S, WIDTH), MXU_DTYPE),
                   jax.ShapeDtypeStruct((S, WIDTH), MXU_DTYPE), jax.ShapeDtypeStruct((PAIRS, nb, 2, t), F32),
                   jax.ShapeDtypeStruct((PAIRS, S, 2), F32)],
        scratch_shapes=[pltpu.VMEM((2, t, 128), kh.dtype), pltpu.VMEM((2, t, 128), v.dtype),
                        pltpu.VMEM((t, 128), F32), pltpu.VMEM((t, 128), F32), pltpu.VMEM((2, t, 1), F32)],
        compiler_params=_params(("parallel", "arbitrary"), VMEM_MID),
    )(qh, do, kh, v, lse4, dl4)


def _s5_mats(a_re, a_im, log_dt, b_re, b_im, c_re, c_im, d_skip):
    Lc = CHUNK
    dt = jnp.exp(log_dt)[:, None]
    lr, li = a_re * dt, a_im * dt

    def apow(n):
        n = jnp.asarray(n, F32)[None, :, None]
        mag = jnp.exp(n * lr[:, None, :])
        ang = n * li[:, None, :]
        return mag * jnp.cos(ang), mag * jnp.sin(ang)

    ar, ai = apow([1.0])
    ar, ai = ar[:, 0], ai[:, 0]
    den = a_re * a_re + a_im * a_im
    nr, ni = ar - 1.0, ai
    fr = (nr * a_re + ni * a_im) / den
    fi = (ni * a_re - nr * a_im) / den
    bbr = fr[:, :, None] * b_re - fi[:, :, None] * b_im
    bbi = fr[:, :, None] * b_im + fi[:, :, None] * b_re
    steps = np.arange(Lc, dtype=np.float32)
    pr, pi = apow(steps)
    car = c_re[:, None] * pr[:, :, None, :] - c_im[:, None] * pi[:, :, None, :]
    cai = c_re[:, None] * pi[:, :, None, :] + c_im[:, None] * pr[:, :, None, :]
    kern = (jnp.einsum('glcp,gpd->glcd', car, bbr, precision=HI)
            - jnp.einsum('glcp,gpd->glcd', cai, bbi, precision=HI))
    skip = d_skip.reshape(GROUPS, CG)[:, :, None] * jnp.eye(CG, dtype=F32)[None]
    kern = kern.at[:, 0].add(skip)
    sel = (steps[None, None, :] - steps[None, :, None] == steps[:, None, None]).astype(np.float32)
    tmat = jnp.einsum('tsl,gtcd->gsdlc', sel, kern, precision=HI).reshape(GROUPS, Lc * CG, Lc * CG)
    p1r, p1i = apow(steps + 1.0)
    cr = c_re[:, None] * p1r[:, :, None, :] - c_im[:, None] * p1i[:, :, None, :]
    ci = c_re[:, None] * p1i[:, :, None, :] + c_im[:, None] * p1r[:, :, None, :]
    to_rows = lambda m: m.transpose(0, 3, 1, 2).reshape(GROUPS, STATE, Lc * CG)
    camat = jnp.concatenate([to_rows(cr), -to_rows(ci)], axis=1)
    qr, qi = apow(Lc - 1.0 - steps)
    zr = qr[:, :, None, :] * bbr.transpose(0, 2, 1)[:, None] - qi[:, :, None, :] * bbi.transpose(0, 2, 1)[:, None]
    zi = qr[:, :, None, :] * bbi.transpose(0, 2, 1)[:, None] + qi[:, :, None, :] * bbr.transpose(0, 2, 1)[:, None]
    bzmat = jnp.concatenate([zr, zi], axis=-1).reshape(GROUPS, Lc * CG, 2 * STATE)
    lr_, li_ = apow([float(Lc)])
    al = jnp.concatenate([lr_[:, 0], li_[:, 0]], axis=-1)
    return tmat, camat, bzmat, al


def _s5_scan_powers(a_re, a_im, log_dt, n_steps):
    dt = jnp.exp(log_dt)[:, None]
    lr, li = a_re * dt, a_im * dt
    n = (CHUNK * 2.0 ** np.arange(n_steps)).astype(np.float32)[None, :, None]
    mag = jnp.exp(n * lr[:, None, :])
    pr, pi = mag * jnp.cos(n * li[:, None, :]), mag * jnp.sin(n * li[:, None, :])
    fwd = jnp.stack([jnp.concatenate([pr, pr], -1), jnp.concatenate([-pi, pi], -1)], axis=2)
    bwd = jnp.stack([jnp.concatenate([pr, pr], -1), jnp.concatenate([pi, -pi], -1)], axis=2)
    return fwd, bwd


def _shift_rows(x, sh, down):
    n = x.shape[0]
    ri = lax.broadcasted_iota(jnp.int32, x.shape, 0)
    if down:
        return jnp.where(ri >= sh, pltpu.roll(x, sh, 0), 0.0)
    return jnp.where(ri < n - sh, pltpu.roll(x, n - sh, 0), 0.0)


def _s5_fwd(uc, tmat, camat, bzmat, pw):
    g, nch, lw = uc.shape
    n_steps = pw.shape[1]

    def body(u_ref, t_ref, ca_ref, bz_ref, pw_ref, y_ref, xp_ref):
        u = u_ref[...]
        x = _mm32(u, bz_ref[...])
        for kk in range(n_steps):
            xs = _shift_rows(x, 2 ** kk, True)
            m = pw_ref[kk]
            x = x + m[0:1, :] * xs + m[1:2, :] * pltpu.roll(xs, STATE, 1)
        xp = _shift_rows(x, 1, True)
        xp_ref[...] = xp
        y_ref[...] = _mm32(u, t_ref[...]) + _mm32(xp, ca_ref[...])

    per = lambda a: pl.BlockSpec((None,) + a.shape[1:], lambda i: (i,) + (0,) * (a.ndim - 1))
    return pl.pallas_call(
        body, name="s5_fwd", grid=(g,),
        in_specs=[per(uc), per(tmat), per(camat), per(bzmat), per(pw)],
        out_specs=[pl.BlockSpec((None, nch, lw), lambda i: (i, 0, 0)),
                   pl.BlockSpec((None, nch, 2 * STATE), lambda i: (i, 0, 0))],
        out_shape=[jax.ShapeDtypeStruct((g, nch, lw), F32), jax.ShapeDtypeStruct((g, nch, 2 * STATE), F32)],
        compiler_params=_params(("parallel",)),
    )(uc, tmat, camat, bzmat, pw)


def _s5_bwd(uc, dyc, xp, tmat, camat, bzmat, pwc):
    g, nch, lw = uc.shape
    n_steps = pwc.shape[1]

    def body(u_ref, dy_ref, xp_ref, t_ref, ca_ref, bz_ref, pw_ref, du_ref, dt_ref, dca_ref, dbz_ref, dal_ref):
        u = u_ref[...]
        dy = dy_ref[...]
        xpv = xp_ref[...]
        dt_ref[...] = _mm32_tn(u, dy)
        dca_ref[...] = _mm32_tn(xpv, dy)
        dx = _shift_rows(_mm32_nt(dy, ca_ref[...]), 1, False)
        for kk in range(n_steps):
            xs = _shift_rows(dx, 2 ** kk, False)
            m = pw_ref[kk]
            dx = dx + m[0:1, :] * xs + m[1:2, :] * pltpu.roll(xs, STATE, 1)
        dbz_ref[...] = _mm32_tn(u, dx)
        du_ref[...] = _mm32_nt(dy, t_ref[...]) + _mm32_nt(dx, bz_ref[...])
        dal_ref[0:1, :] = jnp.sum(dx * xpv, axis=0, keepdims=True)
        dal_ref[1:2, :] = jnp.sum(dx * pltpu.roll(xpv, STATE, 1), axis=0, keepdims=True)

    per = lambda a: pl.BlockSpec((None,) + a.shape[1:], lambda i: (i,) + (0,) * (a.ndim - 1))
    sds = jax.ShapeDtypeStruct
    outs = [sds((g, nch, lw), F32), sds(tmat.shape, F32), sds(camat.shape, F32), sds(bzmat.shape, F32),
            sds((g, 2, 2 * STATE), F32)]
    return pl.pallas_call(
        body, name="s5_bwd", grid=(g,),
        in_specs=[per(uc), per(dyc), per(xp), per(tmat), per(camat), per(bzmat), per(pwc)],
        out_specs=[per(o) for o in outs], out_shape=outs,
        compiler_params=_params(("parallel",)),
    )(uc, dyc, xp, tmat, camat, bzmat, pwc)


def _to_chunks(u):
    s = u.shape[0]
    return u.reshape(s // CHUNK, CHUNK, GROUPS, CG).transpose(2, 0, 1, 3).reshape(GROUPS, s // CHUNK, CHUNK * CG)


def _from_chunks(y):
    nch = y.shape[1]
    return y.reshape(GROUPS, nch, CHUNK, CG).transpose(1, 2, 0, 3).reshape(nch * CHUNK, GROUPS * CG)


GELU_C0 = math.sqrt(2.0 / math.pi)
GELU_C1 = 0.044715


def _mix(o, za, ys, zb, ga, gb, x, tgt, gate, b_glu, g_final, w_glu, w_up_a, w_up_b, w_out, hsel, ts):
    S = o.shape[0]

    def body(o_ref, za_ref, ys_ref, zb_ref, ga_ref, gb_ref, x_ref, t_ref, gate_ref, bglu_ref, gf_ref,
             wglu_ref, wua_ref, wub_ref, wout_ref, hsel_ref,
             dx2_ref, do_ref, dza_ref, dzb_ref, dga_ref, dgb_ref, dys_ref, dl_ref,
             mg_ref, dmo_ref, ya_ref, dua_ref, yb_ref, dub_ref, yg_ref, dgl_ref,
             dbglu_ref, dgate_ref, dgf_ref, loss_ref):
        @pl.when(pl.program_id(0) == 0)
        def _():
            dbglu_ref[...] = jnp.zeros_like(dbglu_ref)
            dgate_ref[...] = jnp.zeros_like(dgate_ref)
            dgf_ref[...] = jnp.zeros_like(dgf_ref)
            loss_ref[...] = jnp.zeros_like(loss_ref)

        ov = o_ref[...]
        za = za_ref[...]
        sza = _sigmoid(za)
        silu_a = za * sza
        ya = ov * silu_a
        ysv = ys_ref[...]
        th = jnp.tanh(GELU_C0 * (ysv + GELU_C1 * ysv * ysv * ysv))
        yg = 0.5 * ysv * (1.0 + th)
        sg = _sigmoid(_mm(yg, wglu_ref[...]) + bglu_ref[...])
        yb1 = yg * sg
        zb = zb_ref[...]
        szb = _sigmoid(zb)
        silu_b = zb * szb
        yb = yb1 * silu_b
        ua = _mm(ya, wua_ref[...])
        ub = _mm(yb, wub_ref[...])
        sa = _sigmoid(ga_ref[...])
        sb = _sigmoid(gb_ref[...])
        merged = sa * ua + sb * ub
        mo = _mm(merged, wout_ref[...])
        gate_v = gate_ref[...]
        x2 = x_ref[...] + gate_v * mo
        r2 = lax.rsqrt(jnp.mean(x2 * x2, axis=-1, keepdims=True) + EPS)
        x2n = x2 * r2
        gf = gf_ref[...]
        diff = x2n * gf - t_ref[...]
        loss_ref[...] += jnp.sum(jnp.sum(diff * diff, axis=-1, keepdims=True), axis=0, keepdims=True) * (0.5 / D_MODEL)
        dy = diff * (1.0 / D_MODEL)
        dgf_ref[...] += jnp.sum(dy * x2n, axis=0, keepdims=True)
        dyg = dy * gf
        dx2 = r2 * (dyg - x2n * jnp.mean(dyg * x2n, axis=-1, keepdims=True))
        dx2_ref[...] = dx2
        dgate_ref[...] += jnp.sum(dx2 * mo, axis=0, keepdims=True)
        dmo = dx2 * gate_v
        dmerged = _mm_nt(dmo, wout_ref[...])
        dua = dmerged * sa
        dub = dmerged * sb
        dga_ref[...] = (dmerged * ua * sa * (1.0 - sa)).astype(dga_ref.dtype)
        dgb_ref[...] = (dmerged * ub * sb * (1.0 - sb)).astype(dgb_ref.dtype)
        dya = _mm_nt(dua, wua_ref[...])
        dyb = _mm_nt(dub, wub_ref[...])
        dov = dya * silu_a
        do_ref[...] = dov.astype(do_ref.dtype)
        dl_ref[...] = _mm32_nt(hsel_ref[...], dov * ov)
        dza_ref[...] = (dya * ov * (sza * (1.0 + za * (1.0 - sza)))).astype(dza_ref.dtype)
        dyb1 = dyb * silu_b
        dzb_ref[...] = (dyb * yb1 * (szb * (1.0 + zb * (1.0 - szb)))).astype(dzb_ref.dtype)
        dgl = dyb1 * yg * sg * (1.0 - sg)
        dbglu_ref[...] += jnp.sum(dgl, axis=0, keepdims=True)
        dyg2 = dyb1 * sg + _mm_nt(dgl, wglu_ref[...])
        dgelu = 0.5 * (1.0 + th) + 0.5 * ysv * (1.0 - th * th) * GELU_C0 * (1.0 + 3.0 * GELU_C1 * ysv * ysv)
        dys_ref[...] = dyg2 * dgelu
        mg_ref[...] = merged.astype(mg_ref.dtype)
        dmo_ref[...] = dmo.astype(dmo_ref.dtype)
        ya_ref[...] = ya.astype(ya_ref.dtype)
        dua_ref[...] = dua.astype(dua_ref.dtype)
        yb_ref[...] = yb.astype(yb_ref.dtype)
        dub_ref[...] = dub.astype(dub_ref.dtype)
        yg_ref[...] = yg.astype(yg_ref.dtype)
        dgl_ref[...] = dgl.astype(dgl_ref.dtype)

    row = lambda n: pl.BlockSpec((ts, n), lambda i: (i, 0))
    full = lambda a: pl.BlockSpec(a.shape, lambda i: (0,) * a.ndim)
    vec = lambda n: pl.BlockSpec((1, n), lambda i: (0, 0))
    sds = jax.ShapeDtypeStruct
    W, Dm = WIDTH, D_MODEL
    return pl.pallas_call(
        body, name="mix", grid=(S // ts,),
        in_specs=[row(W), row(W), row(W), row(W), row(Dm), row(Dm), row(Dm), row(Dm),
                  full(gate), full(b_glu), full(g_final), full(w_glu), full(w_up_a), full(w_up_b), full(w_out), full(hsel)],
        out_specs=[row(Dm), row(W), row(W), row(W), row(Dm), row(Dm), row(W), pl.BlockSpec((HEADS, ts), lambda i: (0, i)),
                   row(Dm), row(Dm), row(W), row(Dm), row(W), row(Dm), row(W), row(W),
                   vec(W), vec(Dm), vec(Dm), vec(1)],
        out_shape=[sds((S, Dm), F32), sds((S, W), MXU_DTYPE), sds((S, W), MXU_DTYPE), sds((S, W), MXU_DTYPE),
                   sds((S, Dm), MXU_DTYPE), sds((S, Dm), MXU_DTYPE), sds((S, W), F32), sds((HEADS, S), F32),
                   sds((S, Dm), MXU_DTYPE), sds((S, Dm), MXU_DTYPE), sds((S, W), MXU_DTYPE), sds((S, Dm), MXU_DTYPE),
                   sds((S, W), MXU_DTYPE), sds((S, Dm), MXU_DTYPE), sds((S, W), MXU_DTYPE), sds((S, W), MXU_DTYPE),
                   sds((1, W), F32), sds((1, Dm), F32), sds((1, Dm), F32), sds((1, 1), F32)],
        compiler_params=_params(("arbitrary",), VMEM_BIG),
    )(o, za, ys, zb, ga, gb, x, tgt, gate, b_glu, g_final, w_glu, w_up_a, w_up_b, w_out, hsel)


def _matmul_tn(name, a, b, ts, exact=False):
    S, M = a.shape
    N = b.shape[1]
    tn = min(N, 512)
    mm = _mm32_tn if exact else _mm_tn

    def body(a_ref, b_ref, o_ref):
        @pl.when(pl.program_id(1) == 0)
        def _():
            o_ref[...] = jnp.zeros_like(o_ref)

        av, bv = a_ref[...], b_ref[...]
        if exact:
            av, bv = av.astype(F32), bv.astype(F32)
        o_ref[...] += mm(av, bv)

    return pl.pallas_call(
        body, name=name, grid=(N // tn, S // ts),
        in_specs=[pl.BlockSpec((ts, M), lambda j, i: (i, 0)), pl.BlockSpec((ts, tn), lambda j, i: (i, j))],
        out_specs=pl.BlockSpec((M, tn), lambda j, i: (0, j)),
        out_shape=jax.ShapeDtypeStruct((M, N), F32),
        compiler_params=_params(("parallel", "arbitrary"), VMEM_MID),
    )(a, b)


def _proj_bwd(dq, dk, dv, dza, du, dzb, dga, dgb, dfl, x, dx2, shift, scale, g_norm, w_main, w_ft, ts):
    S = x.shape[0]

    def body(dq_ref, dk_ref, dv_ref, dza_ref, du_ref, dzb_ref, dga_ref, dgb_ref, dfl_ref, x_ref, dx2_ref,
             sc_ref, gn_ref, w_ref, wft_ref, gx_ref, dsh_ref, dsc_ref, dgn_ref):
        @pl.when(pl.program_id(0) == 0)
        def _():
            dsh_ref[...] = jnp.zeros_like(dsh_ref)
            dsc_ref[...] = jnp.zeros_like(dsc_ref)
            dgn_ref[...] = jnp.zeros_like(dgn_ref)

        def seg(ref, off, n):
            return _mm_nt(ref[...], w_ref[:, off:off + n])

        dh = (seg(dq_ref, M_Q, WIDTH) + seg(dk_ref, M_K, WIDTH) + seg(dv_ref, M_V, WIDTH)
              + seg(dza_ref, M_ZA, WIDTH) + seg(du_ref, M_U, WIDTH) + seg(dzb_ref, M_ZB, WIDTH)
              + seg(dga_ref, M_GA, D_MODEL) + seg(dgb_ref, M_GB, D_MODEL)
              + _mm32(dfl_ref[...], wft_ref[...]))
        xv = x_ref[...]
        r = lax.rsqrt(jnp.mean(xv * xv, axis=-1, keepdims=True) + EPS)
        xn = xv * r
        gn = gn_ref[...]
        s1 = 1.0 + sc_ref[...]
        dsh_ref[...] += jnp.sum(dh, axis=0, keepdims=True)
        dhx = dh * xn
        dsc_ref[...] += jnp.sum(dhx, axis=0, keepdims=True) * gn
        dgn_ref[...] += jnp.sum(dhx, axis=0, keepdims=True) * s1
        dxn = dh * (gn * s1)
        gx_ref[...] = dx2_ref[...] + r * (dxn - xn * jnp.mean(dxn * xn, axis=-1, keepdims=True))

    row = lambda n: pl.BlockSpec((ts, n), lambda i: (i, 0))
    full = lambda a: pl.BlockSpec(a.shape, lambda i: (0,) * a.ndim)
    vec = pl.BlockSpec((1, D_MODEL), lambda i: (0, 0))
    W, Dm = WIDTH, D_MODEL
    del shift
    return pl.pallas_call(
        body, name="proj_bwd", grid=(S // ts,),
        in_specs=[row(W)] * 6 + [row(Dm)] * 2 + [row(HEADS), row(Dm), row(Dm),
                                                 full(scale), full(g_norm), full(w_main), full(w_ft)],
        out_specs=[row(Dm), vec, vec, vec],
        out_shape=[jax.ShapeDtypeStruct((S, Dm), F32)] + [jax.ShapeDtypeStruct((1, Dm), F32)] * 3,
        compiler_params=_params(("arbitrary",), VMEM_BIG),
    )(dq, dk, dv, dza, du, dzb, dga, dgb, dfl, x, dx2, scale, g_norm, w_main, w_ft)


def _adamw(name, planes, w, m, v, tr):
    n, R, C = planes.shape
    bc1 = 1.0 - ADAM_B1 ** ADAM_STEP
    bc2 = 1.0 - ADAM_B2 ** ADAM_STEP

    def body(p_ref, w_ref, m_ref, v_ref, g_ref, d_ref, nm_ref, nv_ref):
        g = p_ref[0].astype(F32)
        for i in range(1, n):
            g = g + p_ref[i].astype(F32)
        g_ref[...] = g
        nm = ADAM_B1 * m_ref[...] + (1.0 - ADAM_B1) * g
        nv = ADAM_B2 * v_ref[...] + (1.0 - ADAM_B2) * (g * g)
        nm_ref[...] = nm
        nv_ref[...] = nv
        d_ref[...] = -ADAM_LR * ((nm / bc1) / (jnp.sqrt(nv / bc2) + ADAM_EPS) + ADAM_WD * w_ref[...])

    blk = pl.BlockSpec((tr, C), lambda i: (i, 0))
    return pl.pallas_call(
        body, name=name, grid=(R // tr,),
        in_specs=[pl.BlockSpec((n, tr, C), lambda i: (0, i, 0)), blk, blk, blk],
        out_specs=[blk] * 4, out_shape=[jax.ShapeDtypeStruct((R, C), F32)] * 4,
        compiler_params=_params(("parallel",), VMEM_MID),
    )(planes, w, m, v)


def _wada_grad(c_all, dmod_cols):
    def body(c_ref, d_ref, o_ref):
        o_ref[0] = _mm32_tn(c_ref[...], d_ref[...])

    return pl.pallas_call(
        body, name="wada_grad",
        out_shape=jax.ShapeDtypeStruct((1, c_all.shape[1], dmod_cols.shape[1]), F32),
        in_specs=[VMEM, VMEM], out_specs=VMEM,
    )(c_all, dmod_cols)


SMALL_ORDER = ("b_ada", "g_norm", "b_f", "a_re", "a_im", "log_dt", "b_re", "b_im", "c_re", "c_im",
               "d_skip", "b_glu", "g_final")
BIG_ORDER = ("w_ada", "w_in", "w_glu", "w_up_a", "w_up_b", "w_out")
ALL_ORDER = ("w_ada", "b_ada", "g_norm", "w_in", "b_f", "a_re", "a_im", "log_dt", "b_re", "b_im", "c_re", "c_im",
             "d_skip", "w_glu", "b_glu", "w_up_a", "w_up_b", "w_out", "g_final")


def _pack_small(parts, rows):
    flat = jnp.concatenate([p.reshape(-1).astype(F32) for p in parts])
    return jnp.pad(flat, (0, rows * 128 - flat.shape[0])).reshape(rows, 128)


def kernel(x, c, w_ada, b_ada, g_norm, w_in, b_f, a_re, a_im, log_dt, b_re, b_im, c_re, c_im, d_skip, w_glu, b_glu, w_up_a, w_up_b, w_out, g_final, loss_target, m_w_ada, m_b_ada, m_g_norm, m_w_in, m_b_f, m_a_re, m_a_im, m_log_dt, m_b_re, m_b_im, m_c_re, m_c_im, m_d_skip, m_w_glu, m_b_glu, m_w_up_a, m_w_up_b, m_w_out, m_g_final, v_w_ada, v_b_ada, v_g_norm, v_w_in, v_b_f, v_a_re, v_a_im, v_log_dt, v_b_re, v_b_im, v_c_re, v_c_im, v_d_skip, v_w_glu, v_b_glu, v_w_up_a, v_w_up_b, v_w_out, v_g_final):
    weights = dict(w_ada=w_ada, b_ada=b_ada, g_norm=g_norm, w_in=w_in, b_f=b_f, a_re=a_re, a_im=a_im, log_dt=log_dt,
                   b_re=b_re, b_im=b_im, c_re=c_re, c_im=c_im, d_skip=d_skip, w_glu=w_glu, b_glu=b_glu,
                   w_up_a=w_up_a, w_up_b=w_up_b, w_out=w_out, g_final=g_final)
    mom_m = dict(w_ada=m_w_ada, b_ada=m_b_ada, g_norm=m_g_norm, w_in=m_w_in, b_f=m_b_f, a_re=m_a_re, a_im=m_a_im,
                 log_dt=m_log_dt, b_re=m_b_re, b_im=m_b_im, c_re=m_c_re, c_im=m_c_im, d_skip=m_d_skip, w_glu=m_w_glu,
                 b_glu=m_b_glu, w_up_a=m_w_up_a, w_up_b=m_w_up_b, w_out=m_w_out, g_final=m_g_final)
    mom_v = dict(w_ada=v_w_ada, b_ada=v_b_ada, g_norm=v_g_norm, w_in=v_w_in, b_f=v_b_f, a_re=v_a_re, a_im=v_a_im,
                 log_dt=v_log_dt, b_re=v_b_re, b_im=v_b_im, c_re=v_c_re, c_im=v_c_im, d_skip=v_d_skip, w_glu=v_w_glu,
                 b_glu=v_b_glu, w_up_a=v_w_up_a, w_up_b=v_w_up_b, w_out=v_w_out, g_final=v_g_final)
    xs = x[0]
    tgt = loss_target[0]
    S = xs.shape[0]
    ts = min(256, S)
    ta = min(512, S)
    tw = min(1024, S)
    nch = S // CHUNK
    n_steps = max(1, int(math.ceil(math.log2(nch))))
    me = _my_index()

    shards = [w.astype(MXU_DTYPE) for w in (w_in[0], w_glu[0], w_up_a[0], w_up_b[0], w_out[0])]
    mod8, c_all, gathered = _comm_in(c, w_ada[0], b_ada.reshape(N_DEV, -1), shards)
    mod = mod8.reshape(1, 3 * D_MODEL)
    shift, scale, gate = mod[:, :D_MODEL], mod[:, D_MODEL:2 * D_MODEL], mod[:, 2 * D_MODEL:]
    w_in_full = gathered[0].transpose(1, 0, 2).reshape(D_MODEL, PROJ_WIDTH)
    w_main = jnp.concatenate([w_in_full[:, :OFF_F], w_in_full[:, OFF_F + HEADS:]], axis=1)
    w_f = w_in_full[:, OFF_F:OFF_F + HEADS].astype(F32)
    w_ft = w_f.T
    w_glu_full = gathered[1].reshape(WIDTH, WIDTH)
    w_up_a_full = gathered[2].transpose(1, 0, 2).reshape(WIDTH, D_MODEL)
    w_up_b_full = gathered[3].transpose(1, 0, 2).reshape(WIDTH, D_MODEL)
    w_out_full = gathered[4].reshape(D_MODEL, D_MODEL)

    q, k, v, za, u, zb, ga, gb, flc, hb = _proj_fwd(xs, shift, scale, g_norm, w_main, w_f, ts)
    fcol = _fgate_fwd(flc, b_f, ta)
    nb = S // ta
    rows4 = lambda r: r.reshape(PAIRS, 2, nb, ta).transpose(0, 2, 1, 3)
    qh, kh, vt = _attn_prep(q, k, v, fcol, ta)
    o, lse4 = _attn_fwd(qh, kh, vt, ta)

    s5_params = (a_re[0], a_im[0], log_dt[0], b_re[0], b_im[0], c_re[0], c_im[0], d_skip[0])
    (tmat, camat, bzmat, al), mats_vjp = jax.vjp(_s5_mats, *s5_params)
    del al
    pw_f, pw_b = _s5_scan_powers(a_re[0], a_im[0], log_dt[0], n_steps)
    uc = _to_chunks(u)
    yc, xprev = _s5_fwd(uc, tmat, camat, bzmat, pw_f)
    ys = _from_chunks(yc)

    hsel = (np.arange(WIDTH)[None, :] // 64 == np.arange(HEADS)[:, None]).astype(np.float32)
    (dx2, do, dza, dzb, dga, dgb, dys, dl_row, merged, dmo, ya, dua, yb, dub, yg, dgl,
     db_glu, dgate, dg_final, loss_part) = _mix(o, za, ys, zb, ga, gb, xs, tgt, gate, b_glu, g_final.reshape(1, -1),
                                                w_glu_full, w_up_a_full, w_up_b_full, w_out_full, jnp.asarray(hsel), ts)

    gw_out = _matmul_tn("dw_out", merged, dmo, tw)
    gw_up_a = _matmul_tn("dw_up_a", ya, dua, tw)
    gw_up_b = _matmul_tn("dw_up_b", yb, dub, tw)
    gw_glu = _matmul_tn("dw_glu", yg, dgl, tw)

    duc, d_tmat, d_camat, d_bzmat, dal2 = _s5_bwd(uc, _to_chunks(dys), xprev, tmat, camat, bzmat, pw_b)
    du = _from_chunks(duc)
    d_al = jnp.concatenate([dal2[:, 0, :STATE] + dal2[:, 0, STATE:], dal2[:, 1, STATE:] - dal2[:, 1, :STATE]], axis=-1)
    gs5 = mats_vjp((d_tmat, d_camat, d_bzmat, d_al))

    dl4 = rows4(dl_row)
    dq, dk, dv, dfq4, dfk4 = _attn_bwd(qh, do, kh, v, lse4, dl4, ta)
    d_fcol = dfq4.transpose(0, 2, 1, 3).reshape(HEADS, S).T - dfk4.transpose(1, 0, 2).reshape(S, HEADS)
    dfl, db_f = _fgate_bwd(d_fcol, flc, b_f, ta)

    grad_x, dshift, dscale, dg_norm = _proj_bwd(dq, dk, dv, dza, du, dzb, dga, dgb, dfl, xs, dx2,
                                                shift, scale, g_norm, w_main, w_ft, ts)
    segs = [("dw_q", dq), ("dw_k", dk), ("dw_v", dv), ("dw_f", dfl), ("dw_za", dza), ("dw_u", du), ("dw_zb", dzb),
            ("dw_ga", dga), ("dw_gb", dgb)]
    gw_in = jnp.concatenate([_matmul_tn(nm, hb, d, tw, exact=(nm == "dw_f")) for nm, d in segs], axis=1)

    planes = [gw_in.reshape(D_MODEL, N_DEV, -1).transpose(1, 0, 2),
              gw_glu.reshape(N_DEV, -1, WIDTH),
              gw_up_a.reshape(WIDTH, N_DEV, -1).transpose(1, 0, 2),
              gw_up_b.reshape(WIDTH, N_DEV, -1).transpose(1, 0, 2),
              gw_out.reshape(N_DEV, -1, D_MODEL)]
    planes = [p.astype(MXU_DTYPE) for p in planes]
    dmod = jnp.concatenate([dshift, dscale, dgate], axis=1)
    small_parts = [dmod, dg_norm, db_f, gs5[0], gs5[1], gs5[2], gs5[3], gs5[4], gs5[5], gs5[6], gs5[7],
                   db_glu, dg_final, loss_part]
    n_small = sum(int(np.prod(p.shape)) for p in small_parts)
    rows = -(-n_small // (8 * 128)) * 8
    small = _pack_small(small_parts, rows)
    recv, small_all = _comm_out(planes, small)

    grads, deltas, new_m, new_v = {}, {}, {}, {}

    def put(name, res, shape):
        grads[name], deltas[name], new_m[name], new_v[name] = [r.reshape(shape) for r in res]

    names = ("w_in", "w_glu", "w_up_a", "w_up_b", "w_out")
    for name, pr in zip(names, recv):
        w2 = weights[name][0]
        tr = 256 if w2.shape[0] % 256 == 0 else w2.shape[0]
        put(name, _adamw("adamw_" + name, pr, w2, mom_m[name][0], mom_v[name][0], tr), weights[name].shape)
    cols = w_ada.shape[2]
    dmod_all = small_all[:, :24, :].reshape(N_DEV, 3 * D_MODEL)
    dmod_cols = lax.dynamic_slice_in_dim(dmod_all, me * cols, cols, axis=1)
    g_wada = _wada_grad(c_all, dmod_cols)
    put("w_ada", _adamw("adamw_w_ada", g_wada, w_ada[0], m_w_ada[0], v_w_ada[0], 256), w_ada.shape)
    pack = lambda d: _pack_small([d[n] for n in SMALL_ORDER] + [jnp.zeros((1,), F32)], rows)
    res_small = _adamw("adamw_small", small_all, pack(weights), pack(mom_m), pack(mom_v), rows)
    flat = [r.reshape(-1) for r in res_small]
    off = 0
    for name in SMALL_ORDER:
        shape = weights[name].shape
        size = int(np.prod(shape))
        put(name, [f[off:off + size] for f in flat], shape)
        off += size
    loss = flat[0][off]

    return (loss, grad_x[None], *[grads[n] for n in ALL_ORDER], *[deltas[n] for n in ALL_ORDER],
            *[new_m[n] for n in ALL_ORDER], *[new_v[n] for n in ALL_ORDER])
```

```python
import functools
import math

import jax
import jax.numpy as jnp
import numpy as np
from jax import lax
from jax.experimental import pallas as pl
from jax.experimental.pallas import tpu as pltpu

F32 = jnp.float32
MXU_DTYPE = jnp.bfloat16
HI = lax.Precision.HIGHEST

N_DEV = 8
D_MODEL = 1024
WIDTH = 512
HEADS = 8
PAIRS = HEADS // 2
GROUPS = 32
STATE = 64
CG = 16
CHUNK = 16
EPS = 1e-6
NEG = float(np.finfo(np.float32).min)

ADAM_LR = 0.001
ADAM_B1 = 0.9
ADAM_B2 = 0.999
ADAM_EPS = 1e-08
ADAM_WD = 0.01
ADAM_STEP = 10

VMEM_BIG = 56 * 1024 * 1024
VMEM_MID = 40 * 1024 * 1024

OFF_F = 3 * WIDTH
PROJ_WIDTH = 5128
M_Q, M_K, M_V, M_ZA, M_U, M_ZB, M_GA, M_GB = 0, 512, 1024, 1536, 2048, 2560, 3072, 4096


def _mm(a, b):
    return jnp.dot(a.astype(MXU_DTYPE), b.astype(MXU_DTYPE), preferred_element_type=F32)


def _mm_nt(a, b):
    return lax.dot_general(a.astype(MXU_DTYPE), b.astype(MXU_DTYPE), (((1,), (1,)), ((), ())),
                           preferred_element_type=F32)


def _mm_tn(a, b):
    return lax.dot_general(a.astype(MXU_DTYPE), b.astype(MXU_DTYPE), (((0,), (0,)), ((), ())),
                           preferred_element_type=F32)


def _mm32(a, b):
    return jnp.dot(a, b, precision=HI, preferred_element_type=F32)


def _mm32_nt(a, b):
    return lax.dot_general(a, b, (((1,), (1,)), ((), ())), precision=HI, preferred_element_type=F32)


def _mm32_tn(a, b):
    return lax.dot_general(a, b, (((0,), (0,)), ((), ())), precision=HI, preferred_element_type=F32)


def _sigmoid(x):
    return 1.0 / (1.0 + jnp.exp(-x))


def _params(sem=None, vmem=None):
    kw = {}
    if sem is not None:
        kw["dimension_semantics"] = sem
    if vmem is not None:
        kw["vmem_limit_bytes"] = vmem
    return pltpu.CompilerParams(**kw)


def _my_index():
    return 4 * lax.axis_index("x") + 2 * lax.axis_index("y") + lax.axis_index("c")


def _dev(p):
    return (p // 4, (p // 2) % 2, p % 2)


ANY = pl.BlockSpec(memory_space=pl.ANY)
VMEM = pl.BlockSpec(memory_space=pltpu.VMEM)
MESH = pl.DeviceIdType.MESH


def _comm_in(c, w_ada, b_ada8, shards):
    n = len(shards)
    cols = w_ada.shape[1]

    def body(c_ref, wada_ref, bada_ref, *rest):
        srcs = rest[:n]
        mod_ref, call_ref = rest[n], rest[n + 1]
        dsts = rest[n + 2:2 * n + 2]
        modp, wsend, wrecv, wloc, csend, crecv, msend, mrecv = rest[2 * n + 2:]
        me = _my_index()

        def wcopy(a, src_dev, d, to):
            return pltpu.make_async_remote_copy(src_ref=srcs[a], dst_ref=dsts[a].at[src_dev],
                                                send_sem=wsend.at[a, d], recv_sem=wrecv.at[a, src_dev],
                                                device_id=_dev(to), device_id_type=MESH)

        def ccopy(src_dev, d, to):
            return pltpu.make_async_remote_copy(src_ref=c_ref, dst_ref=call_ref.at[pl.ds(src_dev, 1)],
                                                send_sem=csend.at[d], recv_sem=crecv.at[src_dev],
                                                device_id=_dev(to), device_id_type=MESH)

        def mcopy(src_dev, d, to):
            return pltpu.make_async_remote_copy(src_ref=modp.at[pl.ds(to, 1)], dst_ref=mod_ref.at[pl.ds(src_dev, 1)],
                                                send_sem=msend.at[d], recv_sem=mrecv.at[src_dev],
                                                device_id=_dev(to), device_id_type=MESH)

        local = [pltpu.make_async_copy(srcs[a], dsts[a].at[me], wloc.at[a]) for a in range(n)]
        for cp in local:
            cp.start()
        peers = [(me + d) % N_DEV for d in range(1, N_DEV)]
        for d, p in enumerate(peers):
            for a in range(n):
                wcopy(a, me, d, p).start()
        call_ref[pl.ds(me, 1), :] = c_ref[...]
        for d, p in enumerate(peers):
            ccopy(me, d, p).start()
        for d, p in enumerate(peers):
            ccopy(p, d, p).wait_recv()
        modp[...] = _mm32(call_ref[...], wada_ref[...]) + bada_ref[pl.ds(me, 1), :]
        mod_ref[pl.ds(me, 1), :] = modp[pl.ds(me, 1), :]
        for d, p in enumerate(peers):
            mcopy(me, d, p).start()
        for d, p in enumerate(peers):
            mcopy(p, d, p).wait_recv()
        for d, p in enumerate(peers):
            for a in range(n):
                wcopy(a, p, d, p).wait_recv()
        for d, p in enumerate(peers):
            for a in range(n):
                wcopy(a, me, d, p).wait_send()
            ccopy(me, d, p).wait_send()
            mcopy(me, d, p).wait_send()
        for cp in local:
            cp.wait()

    out_shape = ([jax.ShapeDtypeStruct((N_DEV, cols), F32), jax.ShapeDtypeStruct((N_DEV, D_MODEL), F32)]
                 + [jax.ShapeDtypeStruct((N_DEV,) + s.shape, s.dtype) for s in shards])
    res = pl.pallas_call(
        body, name="comm_in", out_shape=out_shape,
        in_specs=[VMEM, VMEM, VMEM] + [ANY] * n,
        out_specs=[VMEM, VMEM] + [ANY] * n,
        scratch_shapes=[pltpu.VMEM((N_DEV, cols), F32),
                        pltpu.SemaphoreType.DMA((n, N_DEV)), pltpu.SemaphoreType.DMA((n, N_DEV)),
                        pltpu.SemaphoreType.DMA((n,)),
                        pltpu.SemaphoreType.DMA((N_DEV,)), pltpu.SemaphoreType.DMA((N_DEV,)),
                        pltpu.SemaphoreType.DMA((N_DEV,)), pltpu.SemaphoreType.DMA((N_DEV,))],
        compiler_params=_params(vmem=VMEM_MID),
    )(c, w_ada, b_ada8, *shards)
    return res[0], res[1], list(res[2:])


def _comm_out(planes, small):
    n = len(planes)

    def body(*rest):
        srcs = rest[:n]
        small_ref = rest[n]
        dsts = rest[n + 1:2 * n + 1]
        sall_ref = rest[2 * n + 1]
        wsend, wrecv, wloc, ssend, srecv, sloc = rest[2 * n + 2:]
        me = _my_index()

        def wcopy(a, src_dev, d, to):
            return pltpu.make_async_remote_copy(src_ref=srcs[a].at[to], dst_ref=dsts[a].at[src_dev],
                                                send_sem=wsend.at[a, d], recv_sem=wrecv.at[a, src_dev],
                                                device_id=_dev(to), device_id_type=MESH)

        def scopy(src_dev, d, to):
            return pltpu.make_async_remote_copy(src_ref=small_ref, dst_ref=sall_ref.at[src_dev],
                                                send_sem=ssend.at[d], recv_sem=srecv.at[src_dev],
                                                device_id=_dev(to), device_id_type=MESH)

        local = [pltpu.make_async_copy(srcs[a].at[me], dsts[a].at[me], wloc.at[a]) for a in range(n)]
        local.append(pltpu.make_async_copy(small_ref, sall_ref.at[me], sloc))
        for cp in local:
            cp.start()
        peers = [(me + d) % N_DEV for d in range(1, N_DEV)]
        for d, p in enumerate(peers):
            scopy(me, d, p).start()
            for a in range(n):
                wcopy(a, me, d, p).start()
        for d, p in enumerate(peers):
            scopy(p, d, p).wait_recv()
            for a in range(n):
                wcopy(a, p, d, p).wait_recv()
        for d, p in enumerate(peers):
            scopy(me, d, p).wait_send()
            for a in range(n):
                wcopy(a, me, d, p).wait_send()
        for cp in local:
            cp.wait()

    out_shape = ([jax.ShapeDtypeStruct(p.shape, p.dtype) for p in planes]
                 + [jax.ShapeDtypeStruct((N_DEV,) + small.shape, small.dtype)])
    res = pl.pallas_call(
        body, name="comm_out", out_shape=out_shape,
        in_specs=[ANY] * (n + 1), out_specs=[ANY] * (n + 1),
        scratch_shapes=[pltpu.SemaphoreType.DMA((n, N_DEV)), pltpu.SemaphoreType.DMA((n, N_DEV)),
                        pltpu.SemaphoreType.DMA((n,)),
                        pltpu.SemaphoreType.DMA((N_DEV,)), pltpu.SemaphoreType.DMA((N_DEV,)),
                        pltpu.SemaphoreType.DMA(())],
    )(*planes, small)
    return list(res[:n]), res[n]


def _proj_fwd(x, shift, scale, g_norm, w_main, w_f, ts):
    S = x.shape[0]

    def body(x_ref, sh_ref, sc_ref, gn_ref, w_ref, wf_ref,
             q_ref, k_ref, v_ref, za_ref, u_ref, zb_ref, ga_ref, gb_ref, flc_ref, h_ref):
        xv = x_ref[...]
        r = lax.rsqrt(jnp.mean(xv * xv, axis=-1, keepdims=True) + EPS)
        h = (xv * r) * gn_ref[...] * (1.0 + sc_ref[...]) + sh_ref[...]
        hb = h.astype(MXU_DTYPE)
        h_ref[...] = hb

        def seg(off, n):
            return jnp.dot(hb, w_ref[:, off:off + n], preferred_element_type=F32)

        q_ref[...] = (seg(M_Q, WIDTH) * 0.125).astype(q_ref.dtype)
        k_ref[...] = seg(M_K, WIDTH).astype(k_ref.dtype)
        v_ref[...] = seg(M_V, WIDTH).astype(v_ref.dtype)
        za_ref[...] = seg(M_ZA, WIDTH)
        u_ref[...] = seg(M_U, WIDTH)
        zb_ref[...] = seg(M_ZB, WIDTH)
        ga_ref[...] = seg(M_GA, D_MODEL)
        gb_ref[...] = seg(M_GB, D_MODEL)
        flc_ref[...] = _mm32(h, wf_ref[...])

    row = lambda n: pl.BlockSpec((ts, n), lambda i: (i, 0))
    full = lambda a: pl.BlockSpec(a.shape, lambda i: (0,) * a.ndim)
    sds = jax.ShapeDtypeStruct
    return pl.pallas_call(
        body, name="proj_fwd", grid=(S // ts,),
        in_specs=[row(D_MODEL), full(shift), full(scale), full(g_norm), full(w_main), full(w_f)],
        out_specs=[row(WIDTH)] * 6 + [row(D_MODEL)] * 2 + [row(HEADS), row(D_MODEL)],
        out_shape=[sds((S, WIDTH), MXU_DTYPE)] * 3 + [sds((S, WIDTH), F32)] * 3 + [sds((S, D_MODEL), F32)] * 2
                  + [sds((S, HEADS), F32), sds((S, D_MODEL), MXU_DTYPE)],
        compiler_params=_params(("parallel",), VMEM_BIG),
    )(x, shift, scale, g_norm, w_main, w_f)


def _log_sigmoid(z):
    return jnp.minimum(z, 0.0) - jnp.log(1.0 + jnp.exp(-jnp.abs(z)))


def _fgate_fwd(flc, bf_row, ts):
    S = flc.shape[0]

    def body(flc_ref, bfr_ref, fc_ref, carry_c):
        @pl.when(pl.program_id(0) == 0)
        def _():
            carry_c[...] = jnp.zeros_like(carry_c)

        ri = lax.broadcasted_iota(jnp.int32, (ts, ts), 0)
        ci = lax.broadcasted_iota(jnp.int32, (ts, ts), 1)
        lower = (ci <= ri).astype(F32)
        fc = _mm32(lower, _log_sigmoid(flc_ref[...] + bfr_ref[...])) + carry_c[...]
        fc_ref[...] = fc
        carry_c[...] = fc[ts - 1:ts, :]

    col = pl.BlockSpec((ts, HEADS), lambda i: (i, 0))
    return pl.pallas_call(
        body, name="fgate_fwd", grid=(S // ts,),
        in_specs=[col, pl.BlockSpec((1, HEADS), lambda i: (0, 0))],
        out_specs=col, out_shape=jax.ShapeDtypeStruct((S, HEADS), F32),
        scratch_shapes=[pltpu.VMEM((1, HEADS), F32)],
        compiler_params=_params(("arbitrary",)),
    )(flc, bf_row)


def _fgate_bwd(dfc, flc, bf_row, ts):
    S = flc.shape[0]
    n = S // ts

    def body(df_ref, flc_ref, bfr_ref, dfl_ref, dbf_ref, carry):
        @pl.when(pl.program_id(0) == 0)
        def _():
            carry[...] = jnp.zeros_like(carry)
            dbf_ref[...] = jnp.zeros_like(dbf_ref)

        ri = lax.broadcasted_iota(jnp.int32, (ts, ts), 0)
        ci = lax.broadcasted_iota(jnp.int32, (ts, ts), 1)
        upper = (ci >= ri).astype(F32)
        rc = _mm32(upper, df_ref[...]) + carry[...]
        carry[...] = rc[0:1, :]
        z = flc_ref[...] + bfr_ref[...]
        dfl = rc * _sigmoid(-z)
        dfl_ref[...] = dfl
        dbf_ref[...] += jnp.sum(dfl, axis=0, keepdims=True)

    col = pl.BlockSpec((ts, HEADS), lambda i: (n - 1 - i, 0))
    one = pl.BlockSpec((1, HEADS), lambda i: (0, 0))
    return pl.pallas_call(
        body, name="fgate_bwd", grid=(n,),
        in_specs=[col, col, one], out_specs=[col, one],
        out_shape=[jax.ShapeDtypeStruct((S, HEADS), F32), jax.ShapeDtypeStruct((1, HEADS), F32)],
        scratch_shapes=[pltpu.VMEM((1, HEADS), F32)],
        compiler_params=_params(("arbitrary",)),
    )(dfc, flc, bf_row)


N_EXTRA = 3


def _attn_prep(q, k, v, fcol, t):
    S = q.shape[0]
    nb = S // t

    def body(q_ref, k_ref, v_ref, f_ref, qh_ref, kh_ref, vt_ref):
        lane = lax.broadcasted_iota(jnp.int32, (t, 128), 1)
        f = f_ref[...]
        for p in range(PAIRS):
            qp = q_ref[:, p * 128:(p + 1) * 128]
            kp = k_ref[:, p * 128:(p + 1) * 128]
            vt_ref[p, 0] = v_ref[:, p * 128:(p + 1) * 128].T
            for h in range(2):
                own = (lane < 64) if h == 0 else (lane >= 64)
                base = 64 if h == 0 else 0
                fh = f[:, 2 * p + h:2 * p + h + 1]
                parts = []
                rest = fh
                for _ in range(N_EXTRA):
                    part = rest.astype(qh_ref.dtype)
                    parts.append(part)
                    rest = rest - part.astype(F32)
                one = jnp.ones((t, 1), qh_ref.dtype)
                eq = jnp.zeros((t, 128), qh_ref.dtype)
                ek = jnp.zeros((t, 128), qh_ref.dtype)
                for j in range(N_EXTRA):
                    eq = jnp.where(lane == base + j, parts[j], eq)
                    eq = jnp.where(lane == base + N_EXTRA + j, one, eq)
                    ek = jnp.where(lane == base + j, one, ek)
                    ek = jnp.where(lane == base + N_EXTRA + j, -parts[j], ek)
                qh_ref[2 * p + h] = jnp.where(own, qp, eq)
                kh_ref[2 * p + h] = jnp.where(own, kp, ek)

    row = pl.BlockSpec((t, WIDTH), lambda i: (i, 0))
    heads = pl.BlockSpec((HEADS, t, 128), lambda i: (0, i, 0))
    return pl.pallas_call(
        body, name="attn_prep", grid=(nb,),
        in_specs=[row, row, row, pl.BlockSpec((t, HEADS), lambda i: (i, 0))],
        out_specs=[heads, heads, pl.BlockSpec((PAIRS, 1, 128, t), lambda i: (0, i, 0, 0))],
        out_shape=[jax.ShapeDtypeStruct((HEADS, S, 128), q.dtype), jax.ShapeDtypeStruct((HEADS, S, 128), k.dtype),
                   jax.ShapeDtypeStruct((PAIRS, nb, 128, t), v.dtype)],
        compiler_params=_params(("parallel",), VMEM_MID),
    )(q, k, v, fcol)


def _attn_fwd(qh, kh, vt, t):
    S = qh.shape[1]
    nb = S // t

    def body(q_ref, k_ref, vt_ref, o_ref, lse_ref, acc_s):
        qi = pl.program_id(1)
        acc_s[...] = jnp.zeros_like(acc_s)

        def step(ki, nblk, masked, carry):
            m_old, l_old = carry[:2], carry[2:]
            ks = pl.multiple_of(ki * t, t)
            rows = nblk * t
            sts = [_mm_nt(k_ref[h, pl.ds(ks, rows), :], q_ref[h]) for h in range(2)]
            if masked:
                ri = lax.broadcasted_iota(jnp.int32, (t, t), 0)
                ci = lax.broadcasted_iota(jnp.int32, (t, t), 1)
                sts = [jnp.where(ci >= ri, st, NEG) for st in sts]
            m_new = [jnp.maximum(m_old[h], jnp.max(sts[h], axis=0, keepdims=True)) for h in range(2)]
            alpha = [jnp.exp(m_old[h] - m_new[h]) for h in range(2)]
            pts = [jnp.exp(sts[h] - m_new[h]) for h in range(2)]
            l_new = [alpha[h] * l_old[h] + jnp.sum(pts[h], axis=0, keepdims=True) for h in range(2)]
            for h in range(2):
                pv = _mm(vt_ref[ki], pts[h][:t])
                for b in range(1, nblk):
                    pv = pv + _mm(vt_ref[ki + b], pts[h][b * t:(b + 1) * t])
                acc_s[h] = alpha[h] * acc_s[h] + pv
            return (*m_new, *l_new)

        init = (jnp.full((1, t), -jnp.inf, F32),) * 2 + (jnp.zeros((1, t), F32),) * 2
        carry = lax.fori_loop(0, qi // 2, lambda j, c: step(2 * j, 2, False, c), init)
        carry = lax.cond(qi % 2 == 1, lambda c: step(qi - 1, 1, False, c), lambda c: c, carry)
        m0, m1, l0, l1 = step(qi, 1, True, carry)
        first = lax.broadcasted_iota(jnp.int32, (128, t), 0) < 64
        o_ref[...] = jnp.where(first, acc_s[0] / l0, acc_s[1] / l1).T
        lse_ref[...] = jnp.concatenate([m0 + jnp.log(l0), m1 + jnp.log(l1)], axis=0)

    return pl.pallas_call(
        body, name="attn_fwd", grid=(PAIRS, nb),
        in_specs=[pl.BlockSpec((2, t, 128), lambda p, i: (p, i, 0)), pl.BlockSpec((2, S, 128), lambda p, i: (p, 0, 0)),
                  pl.BlockSpec((None, nb, 128, t), lambda p, i: (p, 0, 0, 0))],
        out_specs=[pl.BlockSpec((t, 128), lambda p, i: (i, p)), pl.BlockSpec((None, None, 2, t), lambda p, i: (p, i, 0, 0))],
        out_shape=[jax.ShapeDtypeStruct((S, WIDTH), F32), jax.ShapeDtypeStruct((PAIRS, nb, 2, t), F32)],
        scratch_shapes=[pltpu.VMEM((2, 128, t), F32)],
        compiler_params=_params(("parallel", "parallel"), VMEM_MID),
    )(qh, kh, vt)


def _attn_bwd(qh, do, kh, v, lse4, dl4, t):
    S = qh.shape[1]
    nb = S // t

    def body(q_ref, do_ref, k_ref, v_ref, lse_ref, dl_ref,
             dq_ref, dk_ref, dv_ref, dfq_ref, dfk_ref, kc_s, vh_s, dk_s, dv_s, dfk_s):
        kj = pl.program_id(1)
        lane = lax.broadcasted_iota(jnp.int32, (t, 128), 1)
        is_a = lane < 64

        @pl.when(kj == 0)
        def _():
            dq_ref[...] = jnp.zeros_like(dq_ref)
            dfq_ref[...] = jnp.zeros_like(dfq_ref)

        vp = v_ref[...]
        zero = jnp.zeros_like(vp)
        kc_s[0] = jnp.where(is_a, k_ref[0], zero.astype(kc_s.dtype))
        kc_s[1] = jnp.where(is_a, zero.astype(kc_s.dtype), k_ref[1])
        vh_s[0] = jnp.where(is_a, vp, zero)
        vh_s[1] = jnp.where(is_a, zero, vp)
        dk_s[...] = jnp.zeros_like(dk_s)
        dv_s[...] = jnp.zeros_like(dv_s)
        dfk_s[...] = jnp.zeros_like(dfk_s)

        def step(qi, masked):
            qs = pl.multiple_of(qi * t, t)
            dob = do_ref[pl.ds(qs, t), :]
            lse = lse_ref[qi]
            dl = dl_ref[qi]
            zq = jnp.zeros_like(dob)
            over_keys = []
            for h in range(2):
                sel = is_a if h == 0 else jnp.logical_not(is_a)
                qb = q_ref[h, pl.ds(qs, t), :]
                st = _mm_nt(k_ref[h], qb) - lse[h:h + 1, :]
                if masked:
                    ri = lax.broadcasted_iota(jnp.int32, (t, t), 0)
                    ci = lax.broadcasted_iota(jnp.int32, (t, t), 1)
                    st = jnp.where(ci >= ri, st, NEG)
                pt = jnp.exp(st)
                dv_s[...] += _mm(pt, jnp.where(sel, dob, zq))
                dpt = _mm_nt(vh_s[h], dob)
                dst = pt * (dpt - dl[h:h + 1, :])
                dfk_s[h] += jnp.sum(dst, axis=1, keepdims=True)
                over_keys.append(jnp.sum(dst, axis=0, keepdims=True))
                dk_s[...] += _mm(dst, jnp.where(sel, qb, jnp.zeros_like(qb)))
                dq_ref[pl.ds(qs, t), :] += _mm_tn(dst, kc_s[h])
            dfq_ref[qi] += jnp.concatenate(over_keys, axis=0)

        step(kj, True)

        def loop_body(qi, carry):
            step(qi, False)
            return carry

        lax.fori_loop(kj + 1, nb, loop_body, 0)
        dk_ref[...] = dk_s[...].astype(dk_ref.dtype)
        dv_ref[...] = dv_s[...].astype(dv_ref.dtype)
        dfk_ref[...] = jnp.where(lax.broadcasted_iota(jnp.int32, (t, 2), 1) == 0, dfk_s[0], dfk_s[1])

        @pl.when(kj == nb - 1)
        def _():
            dq_ref[...] = dq_ref[...] * 0.125

    blk = pl.BlockSpec((t, 128), lambda p, j: (j, p))
    res = pl.BlockSpec((S, 128), lambda p, j: (0, p))
    rows4 = pl.BlockSpec((None, nb, 2, t), lambda p, j: (p, 0, 0, 0))
    cols4 = pl.BlockSpec((None, t, 2), lambda p, j: (p, j, 0))
    return pl.pallas_call(
        body, name="attn_bwd", grid=(PAIRS, nb),
        in_specs=[pl.BlockSpec((2, S, 128), lambda p, j: (p, 0, 0)), res,
                  pl.BlockSpec((2, t, 128), lambda p, j: (p, j, 0)), blk, rows4, rows4],
        out_specs=[res, blk, blk, rows4, cols4],
        out_shape=[jax.ShapeDtypeStruct((S, WIDTH), F32), jax.ShapeDtypeStruct((S, WIDTH), MXU_DTYPE),
                   jax.ShapeDtypeStruct((S, WIDTH), MXU_DTYPE), jax.ShapeDtypeStruct((PAIRS, nb, 2, t), F32),
                   jax.ShapeDtypeStruct((PAIRS, S, 2), F32)],
        scratch_shapes=[pltpu.VMEM((2, t, 128), kh.dtype), pltpu.VMEM((2, t, 128), v.dtype),
                        pltpu.VMEM((t, 128), F32), pltpu.VMEM((t, 128), F32), pltpu.VMEM((2, t, 1), F32)],
        compiler_params=_params(("parallel", "arbitrary"), VMEM_MID),
    )(qh, do, kh, v, lse4, dl4)


def _s5_mats(a_re, a_im, log_dt, b_re, b_im, c_re, c_im, d_skip):
    Lc = CHUNK
    dt = jnp.exp(log_dt)[:, None]
    lr, li = a_re * dt, a_im * dt

    def apow(n):
        n = jnp.asarray(n, F32)[None, :, None]
        mag = jnp.exp(n * lr[:, None, :])
        ang = n * li[:, None, :]
        return mag * jnp.cos(ang), mag * jnp.sin(ang)

    ar, ai = apow([1.0])
    ar, ai = ar[:, 0], ai[:, 0]
    den = a_re * a_re + a_im * a_im
    nr, ni = ar - 1.0, ai
    fr = (nr * a_re + ni * a_im) / den
    fi = (ni * a_re - nr * a_im) / den
    bbr = fr[:, :, None] * b_re - fi[:, :, None] * b_im
    bbi = fr[:, :, None] * b_im + fi[:, :, None] * b_re
    steps = np.arange(Lc, dtype=np.float32)
    pr, pi = apow(steps)
    car = c_re[:, None] * pr[:, :, None, :] - c_im[:, None] * pi[:, :, None, :]
    cai = c_re[:, None] * pi[:, :, None, :] + c_im[:, None] * pr[:, :, None, :]
    kern = (jnp.einsum('glcp,gpd->glcd', car, bbr, precision=HI)
            - jnp.einsum('glcp,gpd->glcd', cai, bbi, precision=HI))
    skip = d_skip.reshape(GROUPS, CG)[:, :, None] * jnp.eye(CG, dtype=F32)[None]
    kern = kern.at[:, 0].add(skip)
    sel = (steps[None, None, :] - steps[None, :, None] == steps[:, None, None]).astype(np.float32)
    tmat = jnp.einsum('tsl,gtcd->gsdlc', sel, kern, precision=HI).reshape(GROUPS, Lc * CG, Lc * CG)
    p1r, p1i = apow(steps + 1.0)
    cr = c_re[:, None] * p1r[:, :, None, :] - c_im[:, None] * p1i[:, :, None, :]
    ci = c_re[:, None] * p1i[:, :, None, :] + c_im[:, None] * p1r[:, :, None, :]
    to_rows = lambda m: m.transpose(0, 3, 1, 2).reshape(GROUPS, STATE, Lc * CG)
    camat = jnp.concatenate([to_rows(cr), -to_rows(ci)], axis=1)
    qr, qi = apow(Lc - 1.0 - steps)
    zr = qr[:, :, None, :] * bbr.transpose(0, 2, 1)[:, None] - qi[:, :, None, :] * bbi.transpose(0, 2, 1)[:, None]
    zi = qr[:, :, None, :] * bbi.transpose(0, 2, 1)[:, None] + qi[:, :, None, :] * bbr.transpose(0, 2, 1)[:, None]
    bzmat = jnp.concatenate([zr, zi], axis=-1).reshape(GROUPS, Lc * CG, 2 * STATE)
    lr_, li_ = apow([float(Lc)])
    al = jnp.concatenate([lr_[:, 0], li_[:, 0]], axis=-1)
    return tmat, camat, bzmat, al


def _s5_scan_powers(a_re, a_im, log_dt, n_steps):
    dt = jnp.exp(log_dt)[:, None]
    lr, li = a_re * dt, a_im * dt
    n = (CHUNK * 2.0 ** np.arange(n_steps)).astype(np.float32)[None, :, None]
    mag = jnp.exp(n * lr[:, None, :])
    pr, pi = mag * jnp.cos(n * li[:, None, :]), mag * jnp.sin(n * li[:, None, :])
    fwd = jnp.stack([jnp.concatenate([pr, pr], -1), jnp.concatenate([-pi, pi], -1)], axis=2)
    bwd = jnp.stack([jnp.concatenate([pr, pr], -1), jnp.concatenate([pi, -pi], -1)], axis=2)
    return fwd, bwd


def _shift_rows(x, sh, down):
    n = x.shape[0]
    ri = lax.broadcasted_iota(jnp.int32, x.shape, 0)
    if down:
        return jnp.where(ri >= sh, pltpu.roll(x, sh, 0), 0.0)
    return jnp.where(ri < n - sh, pltpu.roll(x, n - sh, 0), 0.0)


GPB = 128 // CG


def _group_chunks(ref, nch, g):
    lane = lax.broadcasted_iota(jnp.int32, (nch, 128), 1)
    halves = []
    for half in range(CHUNK * CG // 128):
        acc = jnp.zeros((nch, 128), F32)
        for l8 in range(GPB):
            x = ref[pl.ds(half * GPB + l8, nch, stride=CHUNK), :]
            x = pltpu.roll(x, lax.rem(CG * l8 - CG * g + 128, 128), 1)
            acc = jnp.where((lane >= CG * l8) & (lane < CG * l8 + CG), x, acc)
        halves.append(acc)
    return jnp.concatenate(halves, axis=1)


def _store_group(ref, yc, nch, g):
    lane = lax.broadcasted_iota(jnp.int32, (nch, 128), 1)
    mine = (lane >= CG * g) & (lane < CG * g + CG)
    for l in range(CHUNK):
        half, l8 = divmod(l, GPB)
        x = pltpu.roll(yc[:, half * 128:(half + 1) * 128], lax.rem(CG * g - CG * l8 + 128, 128), 1)
        rows = pl.ds(l, nch, stride=CHUNK)
        ref[rows, :] = jnp.where(mine, x, ref[rows, :])


def _s5_fwd(u, tmat, camat, bzmat, pw):
    S = u.shape[0]
    nch = S // CHUNK
    n_steps = pw.shape[1]

    def body(u_ref, t_ref, ca_ref, bz_ref, pw_ref, y_ref, xp_ref):
        g = pl.program_id(1)

        @pl.when(g == 0)
        def _():
            y_ref[...] = jnp.zeros_like(y_ref)

        uc = _group_chunks(u_ref, nch, g)
        x = _mm32(uc, bz_ref[...])
        for kk in range(n_steps):
            xs = _shift_rows(x, 2 ** kk, True)
            m = pw_ref[kk]
            x = x + m[0:1, :] * xs + m[1:2, :] * pltpu.roll(xs, STATE, 1)
        xp = _shift_rows(x, 1, True)
        xp_ref[...] = xp
        _store_group(y_ref, _mm32(uc, t_ref[...]) + _mm32(xp, ca_ref[...]), nch, g)

    per = lambda a: pl.BlockSpec((None,) + a.shape[1:], lambda b, g: (b * GPB + g,) + (0,) * (a.ndim - 1))
    nat = pl.BlockSpec((S, 128), lambda b, g: (0, b))
    return pl.pallas_call(
        body, name="s5_fwd", grid=(GROUPS // GPB, GPB),
        in_specs=[nat, per(tmat), per(camat), per(bzmat), per(pw)],
        out_specs=[nat, pl.BlockSpec((None, nch, 2 * STATE), lambda b, g: (b * GPB + g, 0, 0))],
        out_shape=[jax.ShapeDtypeStruct((S, GROUPS * CG), F32), jax.ShapeDtypeStruct((GROUPS, nch, 2 * STATE), F32)],
        compiler_params=_params(("parallel", "arbitrary"), VMEM_MID),
    )(u, tmat, camat, bzmat, pw)


def _s5_bwd(u, dy, xp, tmat, camat, bzmat, pwc):
    S = u.shape[0]
    nch = S // CHUNK
    n_steps = pwc.shape[1]

    def body(u_ref, dy_ref, xp_ref, t_ref, ca_ref, bz_ref, pw_ref, du_ref, dt_ref, dca_ref, dbz_ref, dal_ref):
        g = pl.program_id(1)

        @pl.when(g == 0)
        def _():
            du_ref[...] = jnp.zeros_like(du_ref)

        uc = _group_chunks(u_ref, nch, g)
        dyc = _group_chunks(dy_ref, nch, g)
        xpv = xp_ref[...]
        dt_ref[...] = _mm32_tn(uc, dyc)
        dca_ref[...] = _mm32_tn(xpv, dyc)
        dx = _shift_rows(_mm32_nt(dyc, ca_ref[...]), 1, False)
        for kk in range(n_steps):
            xs = _shift_rows(dx, 2 ** kk, False)
            m = pw_ref[kk]
            dx = dx + m[0:1, :] * xs + m[1:2, :] * pltpu.roll(xs, STATE, 1)
        dbz_ref[...] = _mm32_tn(uc, dx)
        dal_ref[0:1, :] = jnp.sum(dx * xpv, axis=0, keepdims=True)
        dal_ref[1:2, :] = jnp.sum(dx * pltpu.roll(xpv, STATE, 1), axis=0, keepdims=True)
        _store_group(du_ref, _mm32_nt(dyc, t_ref[...]) + _mm32_nt(dx, bz_ref[...]), nch, g)

    per = lambda a: pl.BlockSpec((None,) + a.shape[1:], lambda b, g: (b * GPB + g,) + (0,) * (a.ndim - 1))
    nat = pl.BlockSpec((S, 128), lambda b, g: (0, b))
    sds = jax.ShapeDtypeStruct
    mats = [sds(tmat.shape, F32), sds(camat.shape, F32), sds(bzmat.shape, F32), sds((GROUPS, 2, 2 * STATE), F32)]
    return pl.pallas_call(
        body, name="s5_bwd", grid=(GROUPS // GPB, GPB),
        in_specs=[nat, nat, per(xp), per(tmat), per(camat), per(bzmat), per(pwc)],
        out_specs=[nat] + [per(o) for o in mats], out_shape=[sds((S, GROUPS * CG), F32)] + mats,
        compiler_params=_params(("parallel", "arbitrary"), VMEM_BIG),
    )(u, dy, xp, tmat, camat, bzmat, pwc)


GELU_C0 = math.sqrt(2.0 / math.pi)
GELU_C1 = 0.044715


def _mix(o, za, ys, zb, ga, gb, x, tgt, gate, b_glu, g_final, w_glu, w_up_a, w_up_b, w_out, hsel, ts):
    S = o.shape[0]

    def body(o_ref, za_ref, ys_ref, zb_ref, ga_ref, gb_ref, x_ref, t_ref, gate_ref, bglu_ref, gf_ref,
             wglu_ref, wua_ref, wub_ref, wout_ref, hsel_ref,
             dx2_ref, do_ref, dza_ref, dzb_ref, dga_ref, dgb_ref, dys_ref, dl_ref,
             mg_ref, dmo_ref, ya_ref, dua_ref, yb_ref, dub_ref, yg_ref, dgl_ref,
             dbglu_ref, dgate_ref, dgf_ref, loss_ref):
        @pl.when(pl.program_id(0) == 0)
        def _():
            dbglu_ref[...] = jnp.zeros_like(dbglu_ref)
            dgate_ref[...] = jnp.zeros_like(dgate_ref)
            dgf_ref[...] = jnp.zeros_like(dgf_ref)
            loss_ref[...] = jnp.zeros_like(loss_ref)

        ov = o_ref[...]
        za = za_ref[...]
        sza = _sigmoid(za)
        silu_a = za * sza
        ya = ov * silu_a
        ysv = ys_ref[...]
        th = jnp.tanh(GELU_C0 * (ysv + GELU_C1 * ysv * ysv * ysv))
        yg = 0.5 * ysv * (1.0 + th)
        sg = _sigmoid(_mm(yg, wglu_ref[...]) + bglu_ref[...])
        yb1 = yg * sg
        zb = zb_ref[...]
        szb = _sigmoid(zb)
        silu_b = zb * szb
        yb = yb1 * silu_b
        ua = _mm(ya, wua_ref[...])
        ub = _mm(yb, wub_ref[...])
        sa = _sigmoid(ga_ref[...])
        sb = _sigmoid(gb_ref[...])
        merged = sa * ua + sb * ub
        mo = _mm(merged, wout_ref[...])
        gate_v = gate_ref[...]
        x2 = x_ref[...] + gate_v * mo
        r2 = lax.rsqrt(jnp.mean(x2 * x2, axis=-1, keepdims=True) + EPS)
        x2n = x2 * r2
        gf = gf_ref[...]
        diff = x2n * gf - t_ref[...]
        loss_ref[...] += jnp.sum(jnp.sum(diff * diff, axis=-1, keepdims=True), axis=0, keepdims=True) * (0.5 / D_MODEL)
        dy = diff * (1.0 / D_MODEL)
        dgf_ref[...] += jnp.sum(dy * x2n, axis=0, keepdims=True)
        dyg = dy * gf
        dx2 = r2 * (dyg - x2n * jnp.mean(dyg * x2n, axis=-1, keepdims=True))
        dx2_ref[...] = dx2
        dgate_ref[...] += jnp.sum(dx2 * mo, axis=0, keepdims=True)
        dmo = dx2 * gate_v
        dmerged = _mm_nt(dmo, wout_ref[...])
        dua = dmerged * sa
        dub = dmerged * sb
        dga_ref[...] = (dmerged * ua * sa * (1.0 - sa)).astype(dga_ref.dtype)
        dgb_ref[...] = (dmerged * ub * sb * (1.0 - sb)).astype(dgb_ref.dtype)
        dya = _mm_nt(dua, wua_ref[...])
        dyb = _mm_nt(dub, wub_ref[...])
        dov = dya * silu_a
        do_ref[...] = dov.astype(do_ref.dtype)
        dl_ref[...] = _mm32_nt(hsel_ref[...], dov * ov)
        dza_ref[...] = (dya * ov * (sza * (1.0 + za * (1.0 - sza)))).astype(dza_ref.dtype)
        dyb1 = dyb * silu_b
        dzb_ref[...] = (dyb * yb1 * (szb * (1.0 + zb * (1.0 - szb)))).astype(dzb_ref.dtype)
        dgl = dyb1 * yg * sg * (1.0 - sg)
        dbglu_ref[...] += jnp.sum(dgl, axis=0, keepdims=True)
        dyg2 = dyb1 * sg + _mm_nt(dgl, wglu_ref[...])
        dgelu = 0.5 * (1.0 + th) + 0.5 * ysv * (1.0 - th * th) * GELU_C0 * (1.0 + 3.0 * GELU_C1 * ysv * ysv)
        dys_ref[...] = dyg2 * dgelu
        mg_ref[...] = merged.astype(mg_ref.dtype)
        dmo_ref[...] = dmo.astype(dmo_ref.dtype)
        ya_ref[...] = ya.astype(ya_ref.dtype)
        dua_ref[...] = dua.astype(dua_ref.dtype)
        yb_ref[...] = yb.astype(yb_ref.dtype)
        dub_ref[...] = dub.astype(dub_ref.dtype)
        yg_ref[...] = yg.astype(yg_ref.dtype)
        dgl_ref[...] = dgl.astype(dgl_ref.dtype)

    row = lambda n: pl.BlockSpec((ts, n), lambda i: (i, 0))
    full = lambda a: pl.BlockSpec(a.shape, lambda i: (0,) * a.ndim)
    vec = lambda n: pl.BlockSpec((1, n), lambda i: (0, 0))
    sds = jax.ShapeDtypeStruct
    W, Dm = WIDTH, D_MODEL
    return pl.pallas_call(
        body, name="mix", grid=(S // ts,),
        in_specs=[row(W), row(W), row(W), row(W), row(Dm), row(Dm), row(Dm), row(Dm),
                  full(gate), full(b_glu), full(g_final), full(w_glu), full(w_up_a), full(w_up_b), full(w_out), full(hsel)],
        out_specs=[row(Dm), row(W), row(W), row(W), row(Dm), row(Dm), row(W), pl.BlockSpec((HEADS, ts), lambda i: (0, i)),
                   row(Dm), row(Dm), row(W), row(Dm), row(W), row(Dm), row(W), row(W),
                   vec(W), vec(Dm), vec(Dm), vec(1)],
        out_shape=[sds((S, Dm), F32), sds((S, W), MXU_DTYPE), sds((S, W), MXU_DTYPE), sds((S, W), MXU_DTYPE),
                   sds((S, Dm), MXU_DTYPE), sds((S, Dm), MXU_DTYPE), sds((S, W), F32), sds((HEADS, S), F32),
                   sds((S, Dm), MXU_DTYPE), sds((S, Dm), MXU_DTYPE), sds((S, W), MXU_DTYPE), sds((S, Dm), MXU_DTYPE),
                   sds((S, W), MXU_DTYPE), sds((S, Dm), MXU_DTYPE), sds((S, W), MXU_DTYPE), sds((S, W), MXU_DTYPE),
                   sds((1, W), F32), sds((1, Dm), F32), sds((1, Dm), F32), sds((1, 1), F32)],
        compiler_params=_params(("arbitrary",), VMEM_BIG),
    )(o, za, ys, zb, ga, gb, x, tgt, gate, b_glu, g_final, w_glu, w_up_a, w_up_b, w_out, hsel)


def _matmul_tn(name, a, b, ts, exact=False):
    S, M = a.shape
    N = b.shape[1]
    tn = min(N, 512)
    mm = _mm32_tn if exact else _mm_tn

    def body(a_ref, b_ref, o_ref):
        @pl.when(pl.program_id(1) == 0)
        def _():
            o_ref[...] = jnp.zeros_like(o_ref)

        av, bv = a_ref[...], b_ref[...]
        if exact:
            av, bv = av.astype(F32), bv.astype(F32)
        o_ref[...] += mm(av, bv)

    return pl.pallas_call(
        body, name=name, grid=(N // tn, S // ts),
        in_specs=[pl.BlockSpec((ts, M), lambda j, i: (i, 0)), pl.BlockSpec((ts, tn), lambda j, i: (i, j))],
        out_specs=pl.BlockSpec((M, tn), lambda j, i: (0, j)),
        out_shape=jax.ShapeDtypeStruct((M, N), F32),
        compiler_params=_params(("parallel", "arbitrary"), VMEM_MID),
    )(a, b)


def _proj_bwd(dq, dk, dv, dza, du, dzb, dga, dgb, dfl, x, dx2, shift, scale, g_norm, w_main, w_ft, ts):
    S = x.shape[0]

    def body(dq_ref, dk_ref, dv_ref, dza_ref, du_ref, dzb_ref, dga_ref, dgb_ref, dfl_ref, x_ref, dx2_ref,
             sc_ref, gn_ref, w_ref, wft_ref, gx_ref, dsh_ref, dsc_ref, dgn_ref):
        @pl.when(pl.program_id(0) == 0)
        def _():
            dsh_ref[...] = jnp.zeros_like(dsh_ref)
            dsc_ref[...] = jnp.zeros_like(dsc_ref)
            dgn_ref[...] = jnp.zeros_like(dgn_ref)

        def seg(ref, off, n):
            return _mm_nt(ref[...], w_ref[:, off:off + n])

        dh = (seg(dq_ref, M_Q, WIDTH) + seg(dk_ref, M_K, WIDTH) + seg(dv_ref, M_V, WIDTH)
              + seg(dza_ref, M_ZA, WIDTH) + seg(du_ref, M_U, WIDTH) + seg(dzb_ref, M_ZB, WIDTH)
              + seg(dga_ref, M_GA, D_MODEL) + seg(dgb_ref, M_GB, D_MODEL)
              + _mm32(dfl_ref[...], wft_ref[...]))
        xv = x_ref[...]
        r = lax.rsqrt(jnp.mean(xv * xv, axis=-1, keepdims=True) + EPS)
        xn = xv * r
        gn = gn_ref[...]
        s1 = 1.0 + sc_ref[...]
        dsh_ref[...] += jnp.sum(dh, axis=0, keepdims=True)
        dhx = dh * xn
        dsc_ref[...] += jnp.sum(dhx, axis=0, keepdims=True) * gn
        dgn_ref[...] += jnp.sum(dhx, axis=0, keepdims=True) * s1
        dxn = dh * (gn * s1)
        gx_ref[...] = dx2_ref[...] + r * (dxn - xn * jnp.mean(dxn * xn, axis=-1, keepdims=True))

    row = lambda n: pl.BlockSpec((ts, n), lambda i: (i, 0))
    full = lambda a: pl.BlockSpec(a.shape, lambda i: (0,) * a.ndim)
    vec = pl.BlockSpec((1, D_MODEL), lambda i: (0, 0))
    W, Dm = WIDTH, D_MODEL
    del shift
    return pl.pallas_call(
        body, name="proj_bwd", grid=(S // ts,),
        in_specs=[row(W)] * 6 + [row(Dm)] * 2 + [row(HEADS), row(Dm), row(Dm),
                                                 full(scale), full(g_norm), full(w_main), full(w_ft)],
        out_specs=[row(Dm), vec, vec, vec],
        out_shape=[jax.ShapeDtypeStruct((S, Dm), F32)] + [jax.ShapeDtypeStruct((1, Dm), F32)] * 3,
        compiler_params=_params(("arbitrary",), VMEM_BIG),
    )(dq, dk, dv, dza, du, dzb, dga, dgb, dfl, x, dx2, scale, g_norm, w_main, w_ft)


def _adamw(name, planes, w, m, v, tr):
    n, R, C = planes.shape
    bc1 = 1.0 - ADAM_B1 ** ADAM_STEP
    bc2 = 1.0 - ADAM_B2 ** ADAM_STEP

    def body(p_ref, w_ref, m_ref, v_ref, g_ref, d_ref, nm_ref, nv_ref):
        g = p_ref[0].astype(F32)
        for i in range(1, n):
            g = g + p_ref[i].astype(F32)
        g_ref[...] = g
        nm = ADAM_B1 * m_ref[...] + (1.0 - ADAM_B1) * g
        nv = ADAM_B2 * v_ref[...] + (1.0 - ADAM_B2) * (g * g)
        nm_ref[...] = nm
        nv_ref[...] = nv
        d_ref[...] = -ADAM_LR * ((nm / bc1) / (jnp.sqrt(nv / bc2) + ADAM_EPS) + ADAM_WD * w_ref[...])

    blk = pl.BlockSpec((tr, C), lambda i: (i, 0))
    return pl.pallas_call(
        body, name=name, grid=(R // tr,),
        in_specs=[pl.BlockSpec((n, tr, C), lambda i: (0, i, 0)), blk, blk, blk],
        out_specs=[blk] * 4, out_shape=[jax.ShapeDtypeStruct((R, C), F32)] * 4,
        compiler_params=_params(("parallel",), VMEM_MID),
    )(planes, w, m, v)


def _wada_grad(c_all, dmod_cols):
    def body(c_ref, d_ref, o_ref):
        o_ref[0] = _mm32_tn(c_ref[...], d_ref[...])

    return pl.pallas_call(
        body, name="wada_grad",
        out_shape=jax.ShapeDtypeStruct((1, c_all.shape[1], dmod_cols.shape[1]), F32),
        in_specs=[VMEM, VMEM], out_specs=VMEM,
    )(c_all, dmod_cols)


SMALL_ORDER = ("b_ada", "g_norm", "b_f", "a_re", "a_im", "log_dt", "b_re", "b_im", "c_re", "c_im",
               "d_skip", "b_glu", "g_final")
BIG_ORDER = ("w_ada", "w_in", "w_glu", "w_up_a", "w_up_b", "w_out")
ALL_ORDER = ("w_ada", "b_ada", "g_norm", "w_in", "b_f", "a_re", "a_im", "log_dt", "b_re", "b_im", "c_re", "c_im",
             "d_skip", "w_glu", "b_glu", "w_up_a", "w_up_b", "w_out", "g_final")


def _pack_small(parts, rows):
    flat = jnp.concatenate([p.reshape(-1).astype(F32) for p in parts])
    return jnp.pad(flat, (0, rows * 128 - flat.shape[0])).reshape(rows, 128)


def kernel(x, c, w_ada, b_ada, g_norm, w_in, b_f, a_re, a_im, log_dt, b_re, b_im, c_re, c_im, d_skip, w_glu, b_glu, w_up_a, w_up_b, w_out, g_final, loss_target, m_w_ada, m_b_ada, m_g_norm, m_w_in, m_b_f, m_a_re, m_a_im, m_log_dt, m_b_re, m_b_im, m_c_re, m_c_im, m_d_skip, m_w_glu, m_b_glu, m_w_up_a, m_w_up_b, m_w_out, m_g_final, v_w_ada, v_b_ada, v_g_norm, v_w_in, v_b_f, v_a_re, v_a_im, v_log_dt, v_b_re, v_b_im, v_c_re, v_c_im, v_d_skip, v_w_glu, v_b_glu, v_w_up_a, v_w_up_b, v_w_out, v_g_final):
    weights = dict(w_ada=w_ada, b_ada=b_ada, g_norm=g_norm, w_in=w_in, b_f=b_f, a_re=a_re, a_im=a_im, log_dt=log_dt,
                   b_re=b_re, b_im=b_im, c_re=c_re, c_im=c_im, d_skip=d_skip, w_glu=w_glu, b_glu=b_glu,
                   w_up_a=w_up_a, w_up_b=w_up_b, w_out=w_out, g_final=g_final)
    mom_m = dict(w_ada=m_w_ada, b_ada=m_b_ada, g_norm=m_g_norm, w_in=m_w_in, b_f=m_b_f, a_re=m_a_re, a_im=m_a_im,
                 log_dt=m_log_dt, b_re=m_b_re, b_im=m_b_im, c_re=m_c_re, c_im=m_c_im, d_skip=m_d_skip, w_glu=m_w_glu,
                 b_glu=m_b_glu, w_up_a=m_w_up_a, w_up_b=m_w_up_b, w_out=m_w_out, g_final=m_g_final)
    mom_v = dict(w_ada=v_w_ada, b_ada=v_b_ada, g_norm=v_g_norm, w_in=v_w_in, b_f=v_b_f, a_re=v_a_re, a_im=v_a_im,
                 log_dt=v_log_dt, b_re=v_b_re, b_im=v_b_im, c_re=v_c_re, c_im=v_c_im, d_skip=v_d_skip, w_glu=v_w_glu,
                 b_glu=v_b_glu, w_up_a=v_w_up_a, w_up_b=v_w_up_b, w_out=v_w_out, g_final=v_g_final)
    xs = x[0]
    tgt = loss_target[0]
    S = xs.shape[0]
    ts = min(256, S)
    ta = min(512, S)
    tw = min(1024, S)
    nch = S // CHUNK
    n_steps = max(1, int(math.ceil(math.log2(nch))))
    me = _my_index()

    shards = [w.astype(MXU_DTYPE) for w in (w_in[0], w_glu[0], w_up_a[0], w_up_b[0], w_out[0])]
    mod8, c_all, gathered = _comm_in(c, w_ada[0], b_ada.reshape(N_DEV, -1), shards)
    mod = mod8.reshape(1, 3 * D_MODEL)
    shift, scale, gate = mod[:, :D_MODEL], mod[:, D_MODEL:2 * D_MODEL], mod[:, 2 * D_MODEL:]
    w_in_full = gathered[0].transpose(1, 0, 2).reshape(D_MODEL, PROJ_WIDTH)
    w_main = jnp.concatenate([w_in_full[:, :OFF_F], w_in_full[:, OFF_F + HEADS:]], axis=1)
    w_f = w_in_full[:, OFF_F:OFF_F + HEADS].astype(F32)
    w_ft = w_f.T
    w_glu_full = gathered[1].reshape(WIDTH, WIDTH)
    w_up_a_full = gathered[2].transpose(1, 0, 2).reshape(WIDTH, D_MODEL)
    w_up_b_full = gathered[3].transpose(1, 0, 2).reshape(WIDTH, D_MODEL)
    w_out_full = gathered[4].reshape(D_MODEL, D_MODEL)

    q, k, v, za, u, zb, ga, gb, flc, hb = _proj_fwd(xs, shift, scale, g_norm, w_main, w_f, ts)
    fcol = _fgate_fwd(flc, b_f, ta)
    nb = S // ta
    rows4 = lambda r: r.reshape(PAIRS, 2, nb, ta).transpose(0, 2, 1, 3)
    qh, kh, vt = _attn_prep(q, k, v, fcol, ta)
    o, lse4 = _attn_fwd(qh, kh, vt, ta)

    s5_params = (a_re[0], a_im[0], log_dt[0], b_re[0], b_im[0], c_re[0], c_im[0], d_skip[0])
    (tmat, camat, bzmat, al), mats_vjp = jax.vjp(_s5_mats, *s5_params)
    del al
    pw_f, pw_b = _s5_scan_powers(a_re[0], a_im[0], log_dt[0], n_steps)
    ys, xprev = _s5_fwd(u, tmat, camat, bzmat, pw_f)

    hsel = (np.arange(WIDTH)[None, :] // 64 == np.arange(HEADS)[:, None]).astype(np.float32)
    (dx2, do, dza, dzb, dga, dgb, dys, dl_row, merged, dmo, ya, dua, yb, dub, yg, dgl,
     db_glu, dgate, dg_final, loss_part) = _mix(o, za, ys, zb, ga, gb, xs, tgt, gate, b_glu, g_final.reshape(1, -1),
                                                w_glu_full, w_up_a_full, w_up_b_full, w_out_full, jnp.asarray(hsel), ts)

    gw_out = _matmul_tn("dw_out", merged, dmo, tw)
    gw_up_a = _matmul_tn("dw_up_a", ya, dua, tw)
    gw_up_b = _matmul_tn("dw_up_b", yb, dub, tw)
    gw_glu = _matmul_tn("dw_glu", yg, dgl, tw)

    du, d_tmat, d_camat, d_bzmat, dal2 = _s5_bwd(u, dys, xprev, tmat, camat, bzmat, pw_b)
    d_al = jnp.concatenate([dal2[:, 0, :STATE] + dal2[:, 0, STATE:], dal2[:, 1, STATE:] - dal2[:, 1, :STATE]], axis=-1)
    gs5 = mats_vjp((d_tmat, d_camat, d_bzmat, d_al))

    dl4 = rows4(dl_row)
    dq, dk, dv, dfq4, dfk4 = _attn_bwd(qh, do, kh, v, lse4, dl4, ta)
    d_fcol = dfq4.transpose(0, 2, 1, 3).reshape(HEADS, S).T - dfk4.transpose(1, 0, 2).reshape(S, HEADS)
    dfl, db_f = _fgate_bwd(d_fcol, flc, b_f, ta)

    grad_x, dshift, dscale, dg_norm = _proj_bwd(dq, dk, dv, dza, du, dzb, dga, dgb, dfl, xs, dx2,
                                                shift, scale, g_norm, w_main, w_ft, ts)
    segs = [("dw_q", dq), ("dw_k", dk), ("dw_v", dv), ("dw_f", dfl), ("dw_za", dza), ("dw_u", du), ("dw_zb", dzb),
            ("dw_ga", dga), ("dw_gb", dgb)]
    gw_in = jnp.concatenate([_matmul_tn(nm, hb, d, tw, exact=(nm == "dw_f")) for nm, d in segs], axis=1)

    planes = [gw_in.reshape(D_MODEL, N_DEV, -1).transpose(1, 0, 2),
              gw_glu.reshape(N_DEV, -1, WIDTH),
              gw_up_a.reshape(WIDTH, N_DEV, -1).transpose(1, 0, 2),
              gw_up_b.reshape(WIDTH, N_DEV, -1).transpose(1, 0, 2),
              gw_out.reshape(N_DEV, -1, D_MODEL)]
    planes = [p.astype(MXU_DTYPE) for p in planes]
    dmod = jnp.concatenate([dshift, dscale, dgate], axis=1)
    small_parts = [dmod, dg_norm, db_f, gs5[0], gs5[1], gs5[2], gs5[3], gs5[4], gs5[5], gs5[6], gs5[7],
                   db_glu, dg_final, loss_part]
    n_small = sum(int(np.prod(p.shape)) for p in small_parts)
    rows = -(-n_small // (8 * 128)) * 8
    small = _pack_small(small_parts, rows)
    recv, small_all = _comm_out(planes, small)

    grads, deltas, new_m, new_v = {}, {}, {}, {}

    def put(name, res, shape):
        grads[name], deltas[name], new_m[name], new_v[name] = [r.reshape(shape) for r in res]

    names = ("w_in", "w_glu", "w_up_a", "w_up_b", "w_out")
    for name, pr in zip(names, recv):
        w2 = weights[name][0]
        tr = 256 if w2.shape[0] % 256 == 0 else w2.shape[0]
        put(name, _adamw("adamw_" + name, pr, w2, mom_m[name][0], mom_v[name][0], tr), weights[name].shape)
    cols = w_ada.shape[2]
    dmod_all = small_all[:, :24, :].reshape(N_DEV, 3 * D_MODEL)
    dmod_cols = lax.dynamic_slice_in_dim(dmod_all, me * cols, cols, axis=1)
    g_wada = _wada_grad(c_all, dmod_cols)
    put("w_ada", _adamw("adamw_w_ada", g_wada, w_ada[0], m_w_ada[0], v_w_ada[0], 256), w_ada.shape)
    pack = lambda d: _pack_small([d[n] for n in SMALL_ORDER] + [jnp.zeros((1,), F32)], rows)
    res_small = _adamw("adamw_small", small_all, pack(weights), pack(mom_m), pack(mom_v), rows)
    flat = [r.reshape(-1) for r in res_small]
    off = 0
    for name in SMALL_ORDER:
        shape = weights[name].shape
        size = int(np.prod(shape))
        put(name, [f[off:off + size] for f in flat], shape)
        off += size
    loss = flat[0][off]

    return (loss, grad_x[None], *[grads[n] for n in ALL_ORDER], *[deltas[n] for n in ALL_ORDER],
            *[new_m[n] for n in ALL_ORDER], *[new_v[n] for n in ALL_ORDER])
```

```python
import functools
import math

import jax
import jax.numpy as jnp
import numpy as np
from jax import lax
from jax.experimental import pallas as pl
from jax.experimental.pallas import tpu as pltpu

F32 = jnp.float32
MXU_DTYPE = jnp.bfloat16
HI = lax.Precision.HIGHEST

N_DEV = 8
D_MODEL = 1024
WIDTH = 512
HEADS = 8
PAIRS = HEADS // 2
GROUPS = 32
STATE = 64
CG = 16
CHUNK = 16
EPS = 1e-6
NEG = float(np.finfo(np.float32).min)

ADAM_LR = 0.001
ADAM_B1 = 0.9
ADAM_B2 = 0.999
ADAM_EPS = 1e-08
ADAM_WD = 0.01
ADAM_STEP = 10

VMEM_BIG = 56 * 1024 * 1024
VMEM_MID = 40 * 1024 * 1024

OFF_F = 3 * WIDTH
PROJ_WIDTH = 5128
M_Q, M_K, M_V, M_ZA, M_U, M_ZB, M_GA, M_GB = 0, 512, 1024, 1536, 2048, 2560, 3072, 4096


def _mm(a, b):
    return jnp.dot(a.astype(MXU_DTYPE), b.astype(MXU_DTYPE), preferred_element_type=F32)


def _mm_nt(a, b):
    return lax.dot_general(a.astype(MXU_DTYPE), b.astype(MXU_DTYPE), (((1,), (1,)), ((), ())),
                           preferred_element_type=F32)


def _mm_tn(a, b):
    return lax.dot_general(a.astype(MXU_DTYPE), b.astype(MXU_DTYPE), (((0,), (0,)), ((), ())),
                           preferred_element_type=F32)


def _mm32(a, b):
    return jnp.dot(a, b, precision=HI, preferred_element_type=F32)


def _mm32_nt(a, b):
    return lax.dot_general(a, b, (((1,), (1,)), ((), ())), precision=HI, preferred_element_type=F32)


def _mm32_tn(a, b):
    return lax.dot_general(a, b, (((0,), (0,)), ((), ())), precision=HI, preferred_element_type=F32)


def _sigmoid(x):
    return 1.0 / (1.0 + jnp.exp(-x))


def _params(sem=None, vmem=None):
    kw = {}
    if sem is not None:
        kw["dimension_semantics"] = sem
    if vmem is not None:
        kw["vmem_limit_bytes"] = vmem
    return pltpu.CompilerParams(**kw)


def _my_index():
    return 4 * lax.axis_index("x") + 2 * lax.axis_index("y") + lax.axis_index("c")


def _dev(p):
    return (p // 4, (p // 2) % 2, p % 2)


ANY = pl.BlockSpec(memory_space=pl.ANY)
VMEM = pl.BlockSpec(memory_space=pltpu.VMEM)
MESH = pl.DeviceIdType.MESH


def _comm_in(c, w_ada, b_ada8, shards):
    n = len(shards)
    cols = w_ada.shape[1]

    def body(c_ref, wada_ref, bada_ref, *rest):
        srcs = rest[:n]
        mod_ref, call_ref = rest[n], rest[n + 1]
        dsts = rest[n + 2:2 * n + 2]
        modp, wsend, wrecv, wloc, csend, crecv, msend, mrecv = rest[2 * n + 2:]
        me = _my_index()

        x, y, cc = lax.axis_index("x"), lax.axis_index("y"), lax.axis_index("c")
        sib = (x, y, 1 - cc)
        chips = [(1 - x, y), (x, 1 - y), (1 - x, 1 - y)]

        def wcopy(a, k, block, to, src=None):
            ref = dsts[a].at[4 * block[0] + 2 * block[1] + block[2]]
            return pltpu.make_async_remote_copy(src_ref=ref if src is None else src, dst_ref=ref,
                                                send_sem=wsend.at[a, k], recv_sem=wrecv.at[a, k],
                                                device_id=to, device_id_type=MESH)

        def ccopy(src_dev, d, to):
            return pltpu.make_async_remote_copy(src_ref=c_ref, dst_ref=call_ref.at[pl.ds(src_dev, 1)],
                                                send_sem=csend.at[d], recv_sem=crecv.at[src_dev],
                                                device_id=_dev(to), device_id_type=MESH)

        def mcopy(src_dev, d, to):
            return pltpu.make_async_remote_copy(src_ref=modp.at[pl.ds(to, 1)], dst_ref=mod_ref.at[pl.ds(src_dev, 1)],
                                                send_sem=msend.at[d], recv_sem=mrecv.at[src_dev],
                                                device_id=_dev(to), device_id_type=MESH)

        local = [pltpu.make_async_copy(srcs[a], dsts[a].at[me], wloc.at[a]) for a in range(n)]
        for cp in local:
            cp.start()
        peers = [(me + d) % N_DEV for d in range(1, N_DEV)]
        first = []
        for a in range(n):
            first.append(wcopy(a, 0, (x, y, cc), sib, src=srcs[a]))
            first += [wcopy(a, 1 + j, (x, y, cc), (*chip, cc), src=srcs[a]) for j, chip in enumerate(chips)]
        for cp in first:
            cp.start()
        call_ref[pl.ds(me, 1), :] = c_ref[...]
        for d, p in enumerate(peers):
            ccopy(me, d, p).start()
        for d, p in enumerate(peers):
            ccopy(p, d, p).wait_recv()
        modp[...] = _mm32(call_ref[...], wada_ref[...]) + bada_ref[pl.ds(me, 1), :]
        mod_ref[pl.ds(me, 1), :] = modp[pl.ds(me, 1), :]
        for d, p in enumerate(peers):
            mcopy(me, d, p).start()
        for d, p in enumerate(peers):
            mcopy(p, d, p).wait_recv()
        passed = []
        for j, chip in enumerate(chips):
            for a in range(n):
                wcopy(a, 1 + j, (*chip, cc), (x, y, cc)).wait_recv()
                cp = wcopy(a, 4 + j, (*chip, cc), sib)
                cp.start()
                passed.append(cp)
        for a in range(n):
            wcopy(a, 0, sib, (x, y, cc)).wait_recv()
            for j, chip in enumerate(chips):
                wcopy(a, 4 + j, (*chip, 1 - cc), (x, y, cc)).wait_recv()
        for cp in first + passed:
            cp.wait_send()
        for d, p in enumerate(peers):
            ccopy(me, d, p).wait_send()
            mcopy(me, d, p).wait_send()
        for cp in local:
            cp.wait()

    out_shape = ([jax.ShapeDtypeStruct((N_DEV, cols), F32), jax.ShapeDtypeStruct((N_DEV, D_MODEL), F32)]
                 + [jax.ShapeDtypeStruct((N_DEV,) + s.shape, s.dtype) for s in shards])
    res = pl.pallas_call(
        body, name="comm_in", out_shape=out_shape,
        in_specs=[VMEM, VMEM, VMEM] + [ANY] * n,
        out_specs=[VMEM, VMEM] + [ANY] * n,
        scratch_shapes=[pltpu.VMEM((N_DEV, cols), F32),
                        pltpu.SemaphoreType.DMA((n, N_DEV)), pltpu.SemaphoreType.DMA((n, N_DEV)),
                        pltpu.SemaphoreType.DMA((n,)),
                        pltpu.SemaphoreType.DMA((N_DEV,)), pltpu.SemaphoreType.DMA((N_DEV,)),
                        pltpu.SemaphoreType.DMA((N_DEV,)), pltpu.SemaphoreType.DMA((N_DEV,))],
        compiler_params=_params(vmem=VMEM_MID),
    )(c, w_ada, b_ada8, *shards)
    return res[0], res[1], list(res[2:])


N_CHIP = 4


def _comm_pair(planes):
    n = len(planes)

    def body(*rest):
        srcs, dsts = rest[:n], rest[n:2 * n]
        send, recv = rest[2 * n:]
        x, y, cc = lax.axis_index("x"), lax.axis_index("y"), lax.axis_index("c")
        copies = [pltpu.make_async_remote_copy(src_ref=srcs[a].at[2 * ch + 1 - cc], dst_ref=dsts[a].at[ch],
                                               send_sem=send.at[a, ch], recv_sem=recv.at[a, ch],
                                               device_id=(x, y, 1 - cc), device_id_type=MESH)
                  for a in range(n) for ch in range(N_CHIP)]
        for cp in copies:
            cp.start()
        for cp in copies:
            cp.wait()

    out_shape = [jax.ShapeDtypeStruct((N_CHIP,) + p.shape[1:], p.dtype) for p in planes]
    return pl.pallas_call(
        body, name="comm_pair", out_shape=out_shape, in_specs=[ANY] * n, out_specs=[ANY] * n,
        scratch_shapes=[pltpu.SemaphoreType.DMA((n, N_CHIP)), pltpu.SemaphoreType.DMA((n, N_CHIP))],
    )(*planes)


def _pair_sum(name, a, b, tr):
    _, R, C = a.shape

    def body(a_ref, b_ref, o_ref):
        o_ref[...] = (a_ref[...].astype(F32) + b_ref[...].astype(F32)).astype(o_ref.dtype)

    blk = pl.BlockSpec((None, tr, C), lambda i, j: (i, j, 0))
    return pl.pallas_call(
        body, name=name, grid=(N_CHIP, R // tr), in_specs=[blk, blk], out_specs=blk,
        out_shape=jax.ShapeDtypeStruct(a.shape, a.dtype),
        compiler_params=_params(("parallel", "parallel"), VMEM_MID),
    )(a, b)


def _comm_out(chip_planes, small):
    n = len(chip_planes)

    def body(*rest):
        srcs = rest[:n]
        small_ref = rest[n]
        dsts = rest[n + 1:2 * n + 1]
        sall_ref = rest[2 * n + 1]
        wsend, wrecv, wloc, ssend, srecv, sloc = rest[2 * n + 2:]
        me = _my_index()
        x, y, cc = lax.axis_index("x"), lax.axis_index("y"), lax.axis_index("c")
        mine = 2 * x + y
        chips = [(1 - x, y), (x, 1 - y), (1 - x, 1 - y)]

        def wcopy(a, j, sending):
            chip = chips[j]
            there = 2 * chip[0] + chip[1]
            return pltpu.make_async_remote_copy(src_ref=srcs[a].at[there], dst_ref=dsts[a].at[mine if sending else there],
                                                send_sem=wsend.at[a, j], recv_sem=wrecv.at[a, j],
                                                device_id=(*chip, cc), device_id_type=MESH)

        def scopy(src_dev, d, to):
            return pltpu.make_async_remote_copy(src_ref=small_ref, dst_ref=sall_ref.at[src_dev],
                                                send_sem=ssend.at[d], recv_sem=srecv.at[src_dev],
                                                device_id=_dev(to), device_id_type=MESH)

        local = [pltpu.make_async_copy(srcs[a].at[mine], dsts[a].at[mine], wloc.at[a]) for a in range(n)]
        local.append(pltpu.make_async_copy(small_ref, sall_ref.at[me], sloc))
        for cp in local:
            cp.start()
        peers = [(me + d) % N_DEV for d in range(1, N_DEV)]
        for d, p in enumerate(peers):
            scopy(me, d, p).start()
        for j in range(len(chips)):
            for a in range(n):
                wcopy(a, j, True).start()
        for d, p in enumerate(peers):
            scopy(p, d, p).wait_recv()
        for j in range(len(chips)):
            for a in range(n):
                wcopy(a, j, False).wait_recv()
        for d, p in enumerate(peers):
            scopy(me, d, p).wait_send()
        for j in range(len(chips)):
            for a in range(n):
                wcopy(a, j, True).wait_send()
        for cp in local:
            cp.wait()

    out_shape = ([jax.ShapeDtypeStruct(p.shape, p.dtype) for p in chip_planes]
                 + [jax.ShapeDtypeStruct((N_DEV,) + small.shape, small.dtype)])
    res = pl.pallas_call(
        body, name="comm_out", out_shape=out_shape,
        in_specs=[ANY] * (n + 1), out_specs=[ANY] * (n + 1),
        scratch_shapes=[pltpu.SemaphoreType.DMA((n, N_CHIP)), pltpu.SemaphoreType.DMA((n, N_CHIP)),
                        pltpu.SemaphoreType.DMA((n,)),
                        pltpu.SemaphoreType.DMA((N_DEV,)), pltpu.SemaphoreType.DMA((N_DEV,)),
                        pltpu.SemaphoreType.DMA(())],
    )(*chip_planes, small)
    return list(res[:n]), res[n]


def _proj_fwd(x, shift, scale, g_norm, w_main, w_f, ts):
    S = x.shape[0]

    def body(x_ref, sh_ref, sc_ref, gn_ref, w_ref, wf_ref,
             q_ref, k_ref, v_ref, za_ref, u_ref, zb_ref, ga_ref, gb_ref, flc_ref, h_ref):
        xv = x_ref[...]
        r = lax.rsqrt(jnp.mean(xv * xv, axis=-1, keepdims=True) + EPS)
        h = (xv * r) * gn_ref[...] * (1.0 + sc_ref[...]) + sh_ref[...]
        hb = h.astype(MXU_DTYPE)
        h_ref[...] = hb

        def seg(off, n):
            return jnp.dot(hb, w_ref[:, off:off + n], preferred_element_type=F32)

        q_ref[...] = (seg(M_Q, WIDTH) * 0.125).astype(q_ref.dtype)
        k_ref[...] = seg(M_K, WIDTH).astype(k_ref.dtype)
        v_ref[...] = seg(M_V, WIDTH).astype(v_ref.dtype)
        za_ref[...] = seg(M_ZA, WIDTH)
        u_ref[...] = seg(M_U, WIDTH)
        zb_ref[...] = seg(M_ZB, WIDTH)
        ga_ref[...] = seg(M_GA, D_MODEL)
        gb_ref[...] = seg(M_GB, D_MODEL)
        flc_ref[...] = _mm32(h, wf_ref[...])

    row = lambda n: pl.BlockSpec((ts, n), lambda i: (i, 0))
    full = lambda a: pl.BlockSpec(a.shape, lambda i: (0,) * a.ndim)
    sds = jax.ShapeDtypeStruct
    return pl.pallas_call(
        body, name="proj_fwd", grid=(S // ts,),
        in_specs=[row(D_MODEL), full(shift), full(scale), full(g_norm), full(w_main), full(w_f)],
        out_specs=[row(WIDTH)] * 6 + [row(D_MODEL)] * 2 + [row(HEADS), row(D_MODEL)],
        out_shape=[sds((S, WIDTH), MXU_DTYPE)] * 3 + [sds((S, WIDTH), F32)] * 3 + [sds((S, D_MODEL), F32)] * 2
                  + [sds((S, HEADS), F32), sds((S, D_MODEL), MXU_DTYPE)],
        compiler_params=_params(("parallel",), VMEM_BIG),
    )(x, shift, scale, g_norm, w_main, w_f)


def _log_sigmoid(z):
    return jnp.minimum(z, 0.0) - jnp.log(1.0 + jnp.exp(-jnp.abs(z)))


def _fgate_fwd(flc, bf_row, ts):
    S = flc.shape[0]

    def body(flc_ref, bfr_ref, fc_ref, carry_c):
        @pl.when(pl.program_id(0) == 0)
        def _():
            carry_c[...] = jnp.zeros_like(carry_c)

        ri = lax.broadcasted_iota(jnp.int32, (ts, ts), 0)
        ci = lax.broadcasted_iota(jnp.int32, (ts, ts), 1)
        lower = (ci <= ri).astype(F32)
        fc = _mm32(lower, _log_sigmoid(flc_ref[...] + bfr_ref[...])) + carry_c[...]
        fc_ref[...] = fc
        carry_c[...] = fc[ts - 1:ts, :]

    col = pl.BlockSpec((ts, HEADS), lambda i: (i, 0))
    return pl.pallas_call(
        body, name="fgate_fwd", grid=(S // ts,),
        in_specs=[col, pl.BlockSpec((1, HEADS), lambda i: (0, 0))],
        out_specs=col, out_shape=jax.ShapeDtypeStruct((S, HEADS), F32),
        scratch_shapes=[pltpu.VMEM((1, HEADS), F32)],
        compiler_params=_params(("arbitrary",)),
    )(flc, bf_row)


def _fgate_bwd(dfc, flc, bf_row, ts):
    S = flc.shape[0]
    n = S // ts

    def body(df_ref, flc_ref, bfr_ref, dfl_ref, dbf_ref, carry):
        @pl.when(pl.program_id(0) == 0)
        def _():
            carry[...] = jnp.zeros_like(carry)
            dbf_ref[...] = jnp.zeros_like(dbf_ref)

        ri = lax.broadcasted_iota(jnp.int32, (ts, ts), 0)
        ci = lax.broadcasted_iota(jnp.int32, (ts, ts), 1)
        upper = (ci >= ri).astype(F32)
        rc = _mm32(upper, df_ref[...]) + carry[...]
        carry[...] = rc[0:1, :]
        z = flc_ref[...] + bfr_ref[...]
        dfl = rc * _sigmoid(-z)
        dfl_ref[...] = dfl
        dbf_ref[...] += jnp.sum(dfl, axis=0, keepdims=True)

    col = pl.BlockSpec((ts, HEADS), lambda i: (n - 1 - i, 0))
    one = pl.BlockSpec((1, HEADS), lambda i: (0, 0))
    return pl.pallas_call(
        body, name="fgate_bwd", grid=(n,),
        in_specs=[col, col, one], out_specs=[col, one],
        out_shape=[jax.ShapeDtypeStruct((S, HEADS), F32), jax.ShapeDtypeStruct((1, HEADS), F32)],
        scratch_shapes=[pltpu.VMEM((1, HEADS), F32)],
        compiler_params=_params(("arbitrary",)),
    )(dfc, flc, bf_row)


N_EXTRA = 3


def _attn_prep(q, k, v, fcol, t):
    S = q.shape[0]
    nb = S // t

    def body(q_ref, k_ref, v_ref, f_ref, qh_ref, kh_ref, vt_ref):
        lane = lax.broadcasted_iota(jnp.int32, (t, 128), 1)
        f = f_ref[...]
        for p in range(PAIRS):
            qp = q_ref[:, p * 128:(p + 1) * 128]
            kp = k_ref[:, p * 128:(p + 1) * 128]
            vt_ref[p, 0] = v_ref[:, p * 128:(p + 1) * 128].T
            for h in range(2):
                own = (lane < 64) if h == 0 else (lane >= 64)
                base = 64 if h == 0 else 0
                fh = f[:, 2 * p + h:2 * p + h + 1]
                parts = []
                rest = fh
                for _ in range(N_EXTRA):
                    part = rest.astype(qh_ref.dtype)
                    parts.append(part)
                    rest = rest - part.astype(F32)
                one = jnp.ones((t, 1), qh_ref.dtype)
                eq = jnp.zeros((t, 128), qh_ref.dtype)
                ek = jnp.zeros((t, 128), qh_ref.dtype)
                for j in range(N_EXTRA):
                    eq = jnp.where(lane == base + j, parts[j], eq)
                    eq = jnp.where(lane == base + N_EXTRA + j, one, eq)
                    ek = jnp.where(lane == base + j, one, ek)
                    ek = jnp.where(lane == base + N_EXTRA + j, -parts[j], ek)
                qh_ref[2 * p + h] = jnp.where(own, qp, eq)
                kh_ref[2 * p + h] = jnp.where(own, kp, ek)

    row = pl.BlockSpec((t, WIDTH), lambda i: (i, 0))
    heads = pl.BlockSpec((HEADS, t, 128), lambda i: (0, i, 0))
    return pl.pallas_call(
        body, name="attn_prep", grid=(nb,),
        in_specs=[row, row, row, pl.BlockSpec((t, HEADS), lambda i: (i, 0))],
        out_specs=[heads, heads, pl.BlockSpec((PAIRS, 1, 128, t), lambda i: (0, i, 0, 0))],
        out_shape=[jax.ShapeDtypeStruct((HEADS, S, 128), q.dtype), jax.ShapeDtypeStruct((HEADS, S, 128), k.dtype),
                   jax.ShapeDtypeStruct((PAIRS, nb, 128, t), v.dtype)],
        compiler_params=_params(("parallel",), VMEM_MID),
    )(q, k, v, fcol)


def _attn_fwd(qh, kh, vt, t):
    S = qh.shape[1]
    nb = S // t

    def body(q_ref, k_ref, vt_ref, o_ref, lse_ref, acc_s):
        qi = pl.program_id(1)
        acc_s[...] = jnp.zeros_like(acc_s)

        def step(ki, nblk, masked, carry):
            m_old, l_old = carry[:2], carry[2:]
            ks = pl.multiple_of(ki * t, t)
            rows = nblk * t
            sts = [_mm_nt(k_ref[h, pl.ds(ks, rows), :], q_ref[h]) for h in range(2)]
            if masked:
                ri = lax.broadcasted_iota(jnp.int32, (t, t), 0)
                ci = lax.broadcasted_iota(jnp.int32, (t, t), 1)
                sts = [jnp.where(ci >= ri, st, NEG) for st in sts]
            m_new = [jnp.maximum(m_old[h], jnp.max(sts[h], axis=0, keepdims=True)) for h in range(2)]
            alpha = [jnp.exp(m_old[h] - m_new[h]) for h in range(2)]
            pts = [jnp.exp(sts[h] - m_new[h]) for h in range(2)]
            l_new = [alpha[h] * l_old[h] + jnp.sum(pts[h], axis=0, keepdims=True) for h in range(2)]
            for h in range(2):
                pv = _mm(vt_ref[ki], pts[h][:t])
                for b in range(1, nblk):
                    pv = pv + _mm(vt_ref[ki + b], pts[h][b * t:(b + 1) * t])
                acc_s[h] = alpha[h] * acc_s[h] + pv
            return (*m_new, *l_new)

        init = (jnp.full((1, t), -jnp.inf, F32),) * 2 + (jnp.zeros((1, t), F32),) * 2
        carry = lax.fori_loop(0, qi // 2, lambda j, c: step(2 * j, 2, False, c), init)
        carry = lax.cond(qi % 2 == 1, lambda c: step(qi - 1, 1, False, c), lambda c: c, carry)
        m0, m1, l0, l1 = step(qi, 1, True, carry)
        first = lax.broadcasted_iota(jnp.int32, (128, t), 0) < 64
        o_ref[...] = jnp.where(first, acc_s[0] / l0, acc_s[1] / l1).T
        lse_ref[...] = jnp.concatenate([m0 + jnp.log(l0), m1 + jnp.log(l1)], axis=0)

    return pl.pallas_call(
        body, name="attn_fwd", grid=(PAIRS, nb),
        in_specs=[pl.BlockSpec((2, t, 128), lambda p, i: (p, i, 0)), pl.BlockSpec((2, S, 128), lambda p, i: (p, 0, 0)),
                  pl.BlockSpec((None, nb, 128, t), lambda p, i: (p, 0, 0, 0))],
        out_specs=[pl.BlockSpec((t, 128), lambda p, i: (i, p)), pl.BlockSpec((None, None, 2, t), lambda p, i: (p, i, 0, 0))],
        out_shape=[jax.ShapeDtypeStruct((S, WIDTH), F32), jax.ShapeDtypeStruct((PAIRS, nb, 2, t), F32)],
        scratch_shapes=[pltpu.VMEM((2, 128, t), F32)],
        compiler_params=_params(("parallel", "parallel"), VMEM_MID),
    )(qh, kh, vt)


def _attn_bwd(qh, do, kh, v, lse4, dl4, t):
    S = qh.shape[1]
    nb = S // t

    def body(q_ref, do_ref, k_ref, v_ref, lse_ref, dl_ref,
             dq_ref, dk_ref, dv_ref, dfq_ref, dfk_ref, kc_s, vh_s, dk_s, dv_s, dfk_s):
        kj = pl.program_id(1)
        lane = lax.broadcasted_iota(jnp.int32, (t, 128), 1)
        is_a = lane < 64

        @pl.when(kj == 0)
        def _():
            dq_ref[...] = jnp.zeros_like(dq_ref)
            dfq_ref[...] = jnp.zeros_like(dfq_ref)

        vp = v_ref[...]
        zero = jnp.zeros_like(vp)
        kc_s[0] = jnp.where(is_a, k_ref[0], zero.astype(kc_s.dtype))
        kc_s[1] = jnp.where(is_a, zero.astype(kc_s.dtype), k_ref[1])
        vh_s[0] = jnp.where(is_a, vp, zero)
        vh_s[1] = jnp.where(is_a, zero, vp)
        dk_s[...] = jnp.zeros_like(dk_s)
        dv_s[...] = jnp.zeros_like(dv_s)
        dfk_s[...] = jnp.zeros_like(dfk_s)

        def step(qi, masked):
            qs = pl.multiple_of(qi * t, t)
            dob = do_ref[pl.ds(qs, t), :]
            lse = lse_ref[qi]
            dl = dl_ref[qi]
            zq = jnp.zeros_like(dob)
            over_keys = []
            for h in range(2):
                sel = is_a if h == 0 else jnp.logical_not(is_a)
                qb = q_ref[h, pl.ds(qs, t), :]
                st = _mm_nt(k_ref[h], qb) - lse[h:h + 1, :]
                if masked:
                    ri = lax.broadcasted_iota(jnp.int32, (t, t), 0)
                    ci = lax.broadcasted_iota(jnp.int32, (t, t), 1)
                    st = jnp.where(ci >= ri, st, NEG)
                pt = jnp.exp(st)
                dv_s[...] += _mm(pt, jnp.where(sel, dob, zq))
                dpt = _mm_nt(vh_s[h], dob)
                dst = pt * (dpt - dl[h:h + 1, :])
                dfk_s[h] += jnp.sum(dst, axis=1, keepdims=True)
                over_keys.append(jnp.sum(dst, axis=0, keepdims=True))
                dk_s[...] += _mm(dst, jnp.where(sel, qb, jnp.zeros_like(qb)))
                dq_ref[pl.ds(qs, t), :] += _mm_tn(dst, kc_s[h])
            dfq_ref[qi] += jnp.concatenate(over_keys, axis=0)

        step(kj, True)

        def loop_body(qi, carry):
            step(qi, False)
            return carry

        lax.fori_loop(kj + 1, nb, loop_body, 0)
        dk_ref[...] = dk_s[...].astype(dk_ref.dtype)
        dv_ref[...] = dv_s[...].astype(dv_ref.dtype)
        dfk_ref[...] = jnp.where(lax.broadcasted_iota(jnp.int32, (t, 2), 1) == 0, dfk_s[0], dfk_s[1])

        @pl.when(kj == nb - 1)
        def _():
            dq_ref[...] = dq_ref[...] * 0.125

    blk = pl.BlockSpec((t, 128), lambda p, j: (j, p))
    res = pl.BlockSpec((S, 128), lambda p, j: (0, p))
    rows4 = pl.BlockSpec((None, nb, 2, t), lambda p, j: (p, 0, 0, 0))
    cols4 = pl.BlockSpec((None, t, 2), lambda p, j: (p, j, 0))
    return pl.pallas_call(
        body, name="attn_bwd", grid=(PAIRS, nb),
        in_specs=[pl.BlockSpec((2, S, 128), lambda p, j: (p, 0, 0)), res,
                  pl.BlockSpec((2, t, 128), lambda p, j: (p, j, 0)), blk, rows4, rows4],
        out_specs=[res, blk, blk, rows4, cols4],
        out_shape=[jax.ShapeDtypeStruct((S, WIDTH), F32), jax.ShapeDtypeStruct((S, WIDTH), MXU_DTYPE),
                   jax.ShapeDtypeStruct((S, WIDTH), MXU_DTYPE), jax.ShapeDtypeStruct((PAIRS, nb, 2, t), F32),
                   jax.ShapeDtypeStruct((PAIRS, S, 2), F32)],
        scratch_shapes=[pltpu.VMEM((2, t, 128), kh.dtype), pltpu.VMEM((2, t, 128), v.dtype),
                        pltpu.VMEM((t, 128), F32), pltpu.VMEM((t, 128), F32), pltpu.VMEM((2, t, 1), F32)],
        compiler_params=_params(("parallel", "arbitrary"), VMEM_MID),
    )(qh, do, kh, v, lse4, dl4)


def _s5_mats(a_re, a_im, log_dt, b_re, b_im, c_re, c_im, d_skip):
    Lc = CHUNK
    dt = jnp.exp(log_dt)[:, None]
    lr, li = a_re * dt, a_im * dt

    def apow(n):
        n = jnp.asarray(n, F32)[None, :, None]
        mag = jnp.exp(n * lr[:, None, :])
        ang = n * li[:, None, :]
        return mag * jnp.cos(ang), mag * jnp.sin(ang)

    ar, ai = apow([1.0])
    ar, ai = ar[:, 0], ai[:, 0]
    den = a_re * a_re + a_im * a_im
    nr, ni = ar - 1.0, ai
    fr = (nr * a_re + ni * a_im) / den
    fi = (ni * a_re - nr * a_im) / den
    bbr = fr[:, :, None] * b_re - fi[:, :, None] * b_im
    bbi = fr[:, :, None] * b_im + fi[:, :, None] * b_re
    steps = np.arange(Lc, dtype=np.float32)
    pr, pi = apow(steps)
    car = c_re[:, None] * pr[:, :, None, :] - c_im[:, None] * pi[:, :, None, :]
    cai = c_re[:, None] * pi[:, :, None, :] + c_im[:, None] * pr[:, :, None, :]
    kern = (jnp.einsum('glcp,gpd->glcd', car, bbr, precision=HI)
            - jnp.einsum('glcp,gpd->glcd', cai, bbi, precision=HI))
    skip = d_skip.reshape(GROUPS, CG)[:, :, None] * jnp.eye(CG, dtype=F32)[None]
    kern = kern.at[:, 0].add(skip)
    sel = (steps[None, None, :] - steps[None, :, None] == steps[:, None, None]).astype(np.float32)
    tmat = jnp.einsum('tsl,gtcd->gsdlc', sel, kern, precision=HI).reshape(GROUPS, Lc * CG, Lc * CG)
    p1r, p1i = apow(steps + 1.0)
    cr = c_re[:, None] * p1r[:, :, None, :] - c_im[:, None] * p1i[:, :, None, :]
    ci = c_re[:, None] * p1i[:, :, None, :] + c_im[:, None] * p1r[:, :, None, :]
    to_rows = lambda m: m.transpose(0, 3, 1, 2).reshape(GROUPS, STATE, Lc * CG)
    camat = jnp.concatenate([to_rows(cr), -to_rows(ci)], axis=1)
    qr, qi = apow(Lc - 1.0 - steps)
    zr = qr[:, :, None, :] * bbr.transpose(0, 2, 1)[:, None] - qi[:, :, None, :] * bbi.transpose(0, 2, 1)[:, None]
    zi = qr[:, :, None, :] * bbi.transpose(0, 2, 1)[:, None] + qi[:, :, None, :] * bbr.transpose(0, 2, 1)[:, None]
    bzmat = jnp.concatenate([zr, zi], axis=-1).reshape(GROUPS, Lc * CG, 2 * STATE)
    lr_, li_ = apow([float(Lc)])
    al = jnp.concatenate([lr_[:, 0], li_[:, 0]], axis=-1)
    return tmat, camat, bzmat, al


def _s5_scan_powers(a_re, a_im, log_dt, n_steps):
    dt = jnp.exp(log_dt)[:, None]
    lr, li = a_re * dt, a_im * dt
    n = (CHUNK * 2.0 ** np.arange(n_steps)).astype(np.float32)[None, :, None]
    mag = jnp.exp(n * lr[:, None, :])
    pr, pi = mag * jnp.cos(n * li[:, None, :]), mag * jnp.sin(n * li[:, None, :])
    fwd = jnp.stack([jnp.concatenate([pr, pr], -1), jnp.concatenate([-pi, pi], -1)], axis=2)
    bwd = jnp.stack([jnp.concatenate([pr, pr], -1), jnp.concatenate([pi, -pi], -1)], axis=2)
    return fwd, bwd


def _shift_rows(x, sh, down):
    n = x.shape[0]
    ri = lax.broadcasted_iota(jnp.int32, x.shape, 0)
    if down:
        return jnp.where(ri >= sh, pltpu.roll(x, sh, 0), 0.0)
    return jnp.where(ri < n - sh, pltpu.roll(x, n - sh, 0), 0.0)


GPB = 128 // CG


def _group_chunks(ref, nch, g):
    lane = lax.broadcasted_iota(jnp.int32, (nch, 128), 1)
    halves = []
    for half in range(CHUNK * CG // 128):
        acc = jnp.zeros((nch, 128), F32)
        for l8 in range(GPB):
            x = ref[pl.ds(half * GPB + l8, nch, stride=CHUNK), :]
            x = pltpu.roll(x, lax.rem(CG * l8 - CG * g + 128, 128), 1)
            acc = jnp.where((lane >= CG * l8) & (lane < CG * l8 + CG), x, acc)
        halves.append(acc)
    return jnp.concatenate(halves, axis=1)


def _store_group(ref, yc, nch, g):
    lane = lax.broadcasted_iota(jnp.int32, (nch, 128), 1)
    mine = (lane >= CG * g) & (lane < CG * g + CG)
    for l in range(CHUNK):
        half, l8 = divmod(l, GPB)
        x = pltpu.roll(yc[:, half * 128:(half + 1) * 128], lax.rem(CG * g - CG * l8 + 128, 128), 1)
        rows = pl.ds(l, nch, stride=CHUNK)
        ref[rows, :] = jnp.where(mine, x, ref[rows, :])


def _s5_fwd(u, tmat, camat, bzmat, pw):
    S = u.shape[0]
    nch = S // CHUNK
    n_steps = pw.shape[1]

    def body(u_ref, t_ref, ca_ref, bz_ref, pw_ref, y_ref, xp_ref):
        g = pl.program_id(1)

        @pl.when(g == 0)
        def _():
            y_ref[...] = jnp.zeros_like(y_ref)

        uc = _group_chunks(u_ref, nch, g)
        x = _mm32(uc, bz_ref[...])
        for kk in range(n_steps):
            xs = _shift_rows(x, 2 ** kk, True)
            m = pw_ref[kk]
            x = x + m[0:1, :] * xs + m[1:2, :] * pltpu.roll(xs, STATE, 1)
        xp = _shift_rows(x, 1, True)
        xp_ref[...] = xp
        _store_group(y_ref, _mm32(uc, t_ref[...]) + _mm32(xp, ca_ref[...]), nch, g)

    per = lambda a: pl.BlockSpec((None,) + a.shape[1:], lambda b, g: (b * GPB + g,) + (0,) * (a.ndim - 1))
    nat = pl.BlockSpec((S, 128), lambda b, g: (0, b))
    return pl.pallas_call(
        body, name="s5_fwd", grid=(GROUPS // GPB, GPB),
        in_specs=[nat, per(tmat), per(camat), per(bzmat), per(pw)],
        out_specs=[nat, pl.BlockSpec((None, nch, 2 * STATE), lambda b, g: (b * GPB + g, 0, 0))],
        out_shape=[jax.ShapeDtypeStruct((S, GROUPS * CG), F32), jax.ShapeDtypeStruct((GROUPS, nch, 2 * STATE), F32)],
        compiler_params=_params(("parallel", "arbitrary"), VMEM_MID),
    )(u, tmat, camat, bzmat, pw)


def _s5_bwd(u, dy, xp, tmat, camat, bzmat, pwc):
    S = u.shape[0]
    nch = S // CHUNK
    n_steps = pwc.shape[1]

    def body(u_ref, dy_ref, xp_ref, t_ref, ca_ref, bz_ref, pw_ref, du_ref, dt_ref, dca_ref, dbz_ref, dal_ref):
        g = pl.program_id(1)

        @pl.when(g == 0)
        def _():
            du_ref[...] = jnp.zeros_like(du_ref)

        uc = _group_chunks(u_ref, nch, g)
        dyc = _group_chunks(dy_ref, nch, g)
        xpv = xp_ref[...]
        dt_ref[...] = _mm32_tn(uc, dyc)
        dca_ref[...] = _mm32_tn(xpv, dyc)
        dx = _shift_rows(_mm32_nt(dyc, ca_ref[...]), 1, False)
        for kk in range(n_steps):
            xs = _shift_rows(dx, 2 ** kk, False)
            m = pw_ref[kk]
            dx = dx + m[0:1, :] * xs + m[1:2, :] * pltpu.roll(xs, STATE, 1)
        dbz_ref[...] = _mm32_tn(uc, dx)
        dal_ref[0:1, :] = jnp.sum(dx * xpv, axis=0, keepdims=True)
        dal_ref[1:2, :] = jnp.sum(dx * pltpu.roll(xpv, STATE, 1), axis=0, keepdims=True)
        _store_group(du_ref, _mm32_nt(dyc, t_ref[...]) + _mm32_nt(dx, bz_ref[...]), nch, g)

    per = lambda a: pl.BlockSpec((None,) + a.shape[1:], lambda b, g: (b * GPB + g,) + (0,) * (a.ndim - 1))
    nat = pl.BlockSpec((S, 128), lambda b, g: (0, b))
    sds = jax.ShapeDtypeStruct
    mats = [sds(tmat.shape, F32), sds(camat.shape, F32), sds(bzmat.shape, F32), sds((GROUPS, 2, 2 * STATE), F32)]
    return pl.pallas_call(
        body, name="s5_bwd", grid=(GROUPS // GPB, GPB),
        in_specs=[nat, nat, per(xp), per(tmat), per(camat), per(bzmat), per(pwc)],
        out_specs=[nat] + [per(o) for o in mats], out_shape=[sds((S, GROUPS * CG), F32)] + mats,
        compiler_params=_params(("parallel", "arbitrary"), VMEM_BIG),
    )(u, dy, xp, tmat, camat, bzmat, pwc)


GELU_C0 = math.sqrt(2.0 / math.pi)
GELU_C1 = 0.044715


def _mix(o, za, ys, zb, ga, gb, x, tgt, gate, b_glu, g_final, w_glu, w_up_a, w_up_b, w_out, hsel, ts):
    S = o.shape[0]

    def body(o_ref, za_ref, ys_ref, zb_ref, ga_ref, gb_ref, x_ref, t_ref, gate_ref, bglu_ref, gf_ref,
             wglu_ref, wua_ref, wub_ref, wout_ref, hsel_ref,
             dx2_ref, do_ref, dza_ref, dzb_ref, dga_ref, dgb_ref, dys_ref, dl_ref,
             mg_ref, dmo_ref, ya_ref, dua_ref, yb_ref, dub_ref, yg_ref, dgl_ref,
             dbglu_ref, dgate_ref, dgf_ref, loss_ref):
        @pl.when(pl.program_id(0) == 0)
        def _():
            dbglu_ref[...] = jnp.zeros_like(dbglu_ref)
            dgate_ref[...] = jnp.zeros_like(dgate_ref)
            dgf_ref[...] = jnp.zeros_like(dgf_ref)
            loss_ref[...] = jnp.zeros_like(loss_ref)

        ov = o_ref[...]
        za = za_ref[...]
        sza = _sigmoid(za)
        silu_a = za * sza
        ya = ov * silu_a
        ysv = ys_ref[...]
        th = jnp.tanh(GELU_C0 * (ysv + GELU_C1 * ysv * ysv * ysv))
        yg = 0.5 * ysv * (1.0 + th)
        sg = _sigmoid(_mm(yg, wglu_ref[...]) + bglu_ref[...])
        yb1 = yg * sg
        zb = zb_ref[...]
        szb = _sigmoid(zb)
        silu_b = zb * szb
        yb = yb1 * silu_b
        ua = _mm(ya, wua_ref[...])
        ub = _mm(yb, wub_ref[...])
        sa = _sigmoid(ga_ref[...])
        sb = _sigmoid(gb_ref[...])
        merged = sa * ua + sb * ub
        mo = _mm(merged, wout_ref[...])
        gate_v = gate_ref[...]
        x2 = x_ref[...] + gate_v * mo
        r2 = lax.rsqrt(jnp.mean(x2 * x2, axis=-1, keepdims=True) + EPS)
        x2n = x2 * r2
        gf = gf_ref[...]
        diff = x2n * gf - t_ref[...]
        loss_ref[...] += jnp.sum(jnp.sum(diff * diff, axis=-1, keepdims=True), axis=0, keepdims=True) * (0.5 / D_MODEL)
        dy = diff * (1.0 / D_MODEL)
        dgf_ref[...] += jnp.sum(dy * x2n, axis=0, keepdims=True)
        dyg = dy * gf
        dx2 = r2 * (dyg - x2n * jnp.mean(dyg * x2n, axis=-1, keepdims=True))
        dx2_ref[...] = dx2
        dgate_ref[...] += jnp.sum(dx2 * mo, axis=0, keepdims=True)
        dmo = dx2 * gate_v
        dmerged = _mm_nt(dmo, wout_ref[...])
        dua = dmerged * sa
        dub = dmerged * sb
        dga_ref[...] = (dmerged * ua * sa * (1.0 - sa)).astype(dga_ref.dtype)
        dgb_ref[...] = (dmerged * ub * sb * (1.0 - sb)).astype(dgb_ref.dtype)
        dya = _mm_nt(dua, wua_ref[...])
        dyb = _mm_nt(dub, wub_ref[...])
        dov = dya * silu_a
        do_ref[...] = dov.astype(do_ref.dtype)
        dl_ref[...] = _mm32_nt(hsel_ref[...], dov * ov)
        dza_ref[...] = (dya * ov * (sza * (1.0 + za * (1.0 - sza)))).astype(dza_ref.dtype)
        dyb1 = dyb * silu_b
        dzb_ref[...] = (dyb * yb1 * (szb * (1.0 + zb * (1.0 - szb)))).astype(dzb_ref.dtype)
        dgl = dyb1 * yg * sg * (1.0 - sg)
        dbglu_ref[...] += jnp.sum(dgl, axis=0, keepdims=True)
        dyg2 = dyb1 * sg + _mm_nt(dgl, wglu_ref[...])
        dgelu = 0.5 * (1.0 + th) + 0.5 * ysv * (1.0 - th * th) * GELU_C0 * (1.0 + 3.0 * GELU_C1 * ysv * ysv)
        dys_ref[...] = dyg2 * dgelu
        mg_ref[...] = merged.astype(mg_ref.dtype)
        dmo_ref[...] = dmo.astype(dmo_ref.dtype)
        ya_ref[...] = ya.astype(ya_ref.dtype)
        dua_ref[...] = dua.astype(dua_ref.dtype)
        yb_ref[...] = yb.astype(yb_ref.dtype)
        dub_ref[...] = dub.astype(dub_ref.dtype)
        yg_ref[...] = yg.astype(yg_ref.dtype)
        dgl_ref[...] = dgl.astype(dgl_ref.dtype)

    row = lambda n: pl.BlockSpec((ts, n), lambda i: (i, 0))
    full = lambda a: pl.BlockSpec(a.shape, lambda i: (0,) * a.ndim)
    vec = lambda n: pl.BlockSpec((1, n), lambda i: (0, 0))
    sds = jax.ShapeDtypeStruct
    W, Dm = WIDTH, D_MODEL
    return pl.pallas_call(
        body, name="mix", grid=(S // ts,),
        in_specs=[row(W), row(W), row(W), row(W), row(Dm), row(Dm), row(Dm), row(Dm),
                  full(gate), full(b_glu), full(g_final), full(w_glu), full(w_up_a), full(w_up_b), full(w_out), full(hsel)],
        out_specs=[row(Dm), row(W), row(W), row(W), row(Dm), row(Dm), row(W), pl.BlockSpec((HEADS, ts), lambda i: (0, i)),
                   row(Dm), row(Dm), row(W), row(Dm), row(W), row(Dm), row(W), row(W),
                   vec(W), vec(Dm), vec(Dm), vec(1)],
        out_shape=[sds((S, Dm), F32), sds((S, W), MXU_DTYPE), sds((S, W), MXU_DTYPE), sds((S, W), MXU_DTYPE),
                   sds((S, Dm), MXU_DTYPE), sds((S, Dm), MXU_DTYPE), sds((S, W), F32), sds((HEADS, S), F32),
                   sds((S, Dm), MXU_DTYPE), sds((S, Dm), MXU_DTYPE), sds((S, W), MXU_DTYPE), sds((S, Dm), MXU_DTYPE),
                   sds((S, W), MXU_DTYPE), sds((S, Dm), MXU_DTYPE), sds((S, W), MXU_DTYPE), sds((S, W), MXU_DTYPE),
                   sds((1, W), F32), sds((1, Dm), F32), sds((1, Dm), F32), sds((1, 1), F32)],
        compiler_params=_params(("arbitrary",), VMEM_BIG),
    )(o, za, ys, zb, ga, gb, x, tgt, gate, b_glu, g_final, w_glu, w_up_a, w_up_b, w_out, hsel)


def _matmul_tn(name, a, b, ts, exact=False):
    S, M = a.shape
    N = b.shape[1]
    tn = min(N, 512)
    mm = _mm32_tn if exact else _mm_tn

    def body(a_ref, b_ref, o_ref):
        @pl.when(pl.program_id(1) == 0)
        def _():
            o_ref[...] = jnp.zeros_like(o_ref)

        av, bv = a_ref[...], b_ref[...]
        if exact:
            av, bv = av.astype(F32), bv.astype(F32)
        o_ref[...] += mm(av, bv)

    return pl.pallas_call(
        body, name=name, grid=(N // tn, S // ts),
        in_specs=[pl.BlockSpec((ts, M), lambda j, i: (i, 0)), pl.BlockSpec((ts, tn), lambda j, i: (i, j))],
        out_specs=pl.BlockSpec((M, tn), lambda j, i: (0, j)),
        out_shape=jax.ShapeDtypeStruct((M, N), F32),
        compiler_params=_params(("parallel", "arbitrary"), VMEM_MID),
    )(a, b)


def _proj_bwd(dq, dk, dv, dza, du, dzb, dga, dgb, dfl, x, dx2, shift, scale, g_norm, w_main, w_ft, ts):
    S = x.shape[0]

    def body(dq_ref, dk_ref, dv_ref, dza_ref, du_ref, dzb_ref, dga_ref, dgb_ref, dfl_ref, x_ref, dx2_ref,
             sc_ref, gn_ref, w_ref, wft_ref, gx_ref, dsh_ref, dsc_ref, dgn_ref):
        @pl.when(pl.program_id(0) == 0)
        def _():
            dsh_ref[...] = jnp.zeros_like(dsh_ref)
            dsc_ref[...] = jnp.zeros_like(dsc_ref)
            dgn_ref[...] = jnp.zeros_like(dgn_ref)

        def seg(ref, off, n):
            return _mm_nt(ref[...], w_ref[:, off:off + n])

        dh = (seg(dq_ref, M_Q, WIDTH) + seg(dk_ref, M_K, WIDTH) + seg(dv_ref, M_V, WIDTH)
              + seg(dza_ref, M_ZA, WIDTH) + seg(du_ref, M_U, WIDTH) + seg(dzb_ref, M_ZB, WIDTH)
              + seg(dga_ref, M_GA, D_MODEL) + seg(dgb_ref, M_GB, D_MODEL)
              + _mm32(dfl_ref[...], wft_ref[...]))
        xv = x_ref[...]
        r = lax.rsqrt(jnp.mean(xv * xv, axis=-1, keepdims=True) + EPS)
        xn = xv * r
        gn = gn_ref[...]
        s1 = 1.0 + sc_ref[...]
        dsh_ref[...] += jnp.sum(dh, axis=0, keepdims=True)
        dhx = dh * xn
        dsc_ref[...] += jnp.sum(dhx, axis=0, keepdims=True) * gn
        dgn_ref[...] += jnp.sum(dhx, axis=0, keepdims=True) * s1
        dxn = dh * (gn * s1)
        gx_ref[...] = dx2_ref[...] + r * (dxn - xn * jnp.mean(dxn * xn, axis=-1, keepdims=True))

    row = lambda n: pl.BlockSpec((ts, n), lambda i: (i, 0))
    full = lambda a: pl.BlockSpec(a.shape, lambda i: (0,) * a.ndim)
    vec = pl.BlockSpec((1, D_MODEL), lambda i: (0, 0))
    W, Dm = WIDTH, D_MODEL
    del shift
    return pl.pallas_call(
        body, name="proj_bwd", grid=(S // ts,),
        in_specs=[row(W)] * 6 + [row(Dm)] * 2 + [row(HEADS), row(Dm), row(Dm),
                                                 full(scale), full(g_norm), full(w_main), full(w_ft)],
        out_specs=[row(Dm), vec, vec, vec],
        out_shape=[jax.ShapeDtypeStruct((S, Dm), F32)] + [jax.ShapeDtypeStruct((1, Dm), F32)] * 3,
        compiler_params=_params(("arbitrary",), VMEM_BIG),
    )(dq, dk, dv, dza, du, dzb, dga, dgb, dfl, x, dx2, scale, g_norm, w_main, w_ft)


def _adamw(name, planes, w, m, v, tr):
    n, R, C = planes.shape
    bc1 = 1.0 - ADAM_B1 ** ADAM_STEP
    bc2 = 1.0 - ADAM_B2 ** ADAM_STEP

    def body(p_ref, w_ref, m_ref, v_ref, g_ref, d_ref, nm_ref, nv_ref):
        g = p_ref[0].astype(F32)
        for i in range(1, n):
            g = g + p_ref[i].astype(F32)
        g_ref[...] = g
        nm = ADAM_B1 * m_ref[...] + (1.0 - ADAM_B1) * g
        nv = ADAM_B2 * v_ref[...] + (1.0 - ADAM_B2) * (g * g)
        nm_ref[...] = nm
        nv_ref[...] = nv
        d_ref[...] = -ADAM_LR * ((nm / bc1) / (jnp.sqrt(nv / bc2) + ADAM_EPS) + ADAM_WD * w_ref[...])

    blk = pl.BlockSpec((tr, C), lambda i: (i, 0))
    return pl.pallas_call(
        body, name=name, grid=(R // tr,),
        in_specs=[pl.BlockSpec((n, tr, C), lambda i: (0, i, 0)), blk, blk, blk],
        out_specs=[blk] * 4, out_shape=[jax.ShapeDtypeStruct((R, C), F32)] * 4,
        compiler_params=_params(("parallel",), VMEM_MID),
    )(planes, w, m, v)


def _wada_grad(c_all, dmod_cols):
    def body(c_ref, d_ref, o_ref):
        o_ref[0] = _mm32_tn(c_ref[...], d_ref[...])

    return pl.pallas_call(
        body, name="wada_grad",
        out_shape=jax.ShapeDtypeStruct((1, c_all.shape[1], dmod_cols.shape[1]), F32),
        in_specs=[VMEM, VMEM], out_specs=VMEM,
    )(c_all, dmod_cols)


SMALL_ORDER = ("b_ada", "g_norm", "b_f", "a_re", "a_im", "log_dt", "b_re", "b_im", "c_re", "c_im",
               "d_skip", "b_glu", "g_final")
BIG_ORDER = ("w_ada", "w_in", "w_glu", "w_up_a", "w_up_b", "w_out")
ALL_ORDER = ("w_ada", "b_ada", "g_norm", "w_in", "b_f", "a_re", "a_im", "log_dt", "b_re", "b_im", "c_re", "c_im",
             "d_skip", "w_glu", "b_glu", "w_up_a", "w_up_b", "w_out", "g_final")


def _pack_small(parts, rows):
    flat = jnp.concatenate([p.reshape(-1).astype(F32) for p in parts])
    return jnp.pad(flat, (0, rows * 128 - flat.shape[0])).reshape(rows, 128)


def kernel(x, c, w_ada, b_ada, g_norm, w_in, b_f, a_re, a_im, log_dt, b_re, b_im, c_re, c_im, d_skip, w_glu, b_glu, w_up_a, w_up_b, w_out, g_final, loss_target, m_w_ada, m_b_ada, m_g_norm, m_w_in, m_b_f, m_a_re, m_a_im, m_log_dt, m_b_re, m_b_im, m_c_re, m_c_im, m_d_skip, m_w_glu, m_b_glu, m_w_up_a, m_w_up_b, m_w_out, m_g_final, v_w_ada, v_b_ada, v_g_norm, v_w_in, v_b_f, v_a_re, v_a_im, v_log_dt, v_b_re, v_b_im, v_c_re, v_c_im, v_d_skip, v_w_glu, v_b_glu, v_w_up_a, v_w_up_b, v_w_out, v_g_final):
    weights = dict(w_ada=w_ada, b_ada=b_ada, g_norm=g_norm, w_in=w_in, b_f=b_f, a_re=a_re, a_im=a_im, log_dt=log_dt,
                   b_re=b_re, b_im=b_im, c_re=c_re, c_im=c_im, d_skip=d_skip, w_glu=w_glu, b_glu=b_glu,
                   w_up_a=w_up_a, w_up_b=w_up_b, w_out=w_out, g_final=g_final)
    mom_m = dict(w_ada=m_w_ada, b_ada=m_b_ada, g_norm=m_g_norm, w_in=m_w_in, b_f=m_b_f, a_re=m_a_re, a_im=m_a_im,
                 log_dt=m_log_dt, b_re=m_b_re, b_im=m_b_im, c_re=m_c_re, c_im=m_c_im, d_skip=m_d_skip, w_glu=m_w_glu,
                 b_glu=m_b_glu, w_up_a=m_w_up_a, w_up_b=m_w_up_b, w_out=m_w_out, g_final=m_g_final)
    mom_v = dict(w_ada=v_w_ada, b_ada=v_b_ada, g_norm=v_g_norm, w_in=v_w_in, b_f=v_b_f, a_re=v_a_re, a_im=v_a_im,
                 log_dt=v_log_dt, b_re=v_b_re, b_im=v_b_im, c_re=v_c_re, c_im=v_c_im, d_skip=v_d_skip, w_glu=v_w_glu,
                 b_glu=v_b_glu, w_up_a=v_w_up_a, w_up_b=v_w_up_b, w_out=v_w_out, g_final=v_g_final)
    xs = x[0]
    tgt = loss_target[0]
    S = xs.shape[0]
    ts = min(256, S)
    ta = min(512, S)
    tw = min(1024, S)
    nch = S // CHUNK
    n_steps = max(1, int(math.ceil(math.log2(nch))))
    me = _my_index()

    shards = [w.astype(MXU_DTYPE) for w in (w_in[0], w_glu[0], w_up_a[0], w_up_b[0], w_out[0])]
    mod8, c_all, gathered = _comm_in(c, w_ada[0], b_ada.reshape(N_DEV, -1), shards)
    mod = mod8.reshape(1, 3 * D_MODEL)
    shift, scale, gate = mod[:, :D_MODEL], mod[:, D_MODEL:2 * D_MODEL], mod[:, 2 * D_MODEL:]
    w_in_full = gathered[0].transpose(1, 0, 2).reshape(D_MODEL, PROJ_WIDTH)
    w_main = jnp.concatenate([w_in_full[:, :OFF_F], w_in_full[:, OFF_F + HEADS:]], axis=1)
    w_f = w_in_full[:, OFF_F:OFF_F + HEADS].astype(F32)
    w_ft = w_f.T
    w_glu_full = gathered[1].reshape(WIDTH, WIDTH)
    w_up_a_full = gathered[2].transpose(1, 0, 2).reshape(WIDTH, D_MODEL)
    w_up_b_full = gathered[3].transpose(1, 0, 2).reshape(WIDTH, D_MODEL)
    w_out_full = gathered[4].reshape(D_MODEL, D_MODEL)

    q, k, v, za, u, zb, ga, gb, flc, hb = _proj_fwd(xs, shift, scale, g_norm, w_main, w_f, ts)
    fcol = _fgate_fwd(flc, b_f, ta)
    nb = S // ta
    rows4 = lambda r: r.reshape(PAIRS, 2, nb, ta).transpose(0, 2, 1, 3)
    qh, kh, vt = _attn_prep(q, k, v, fcol, ta)
    o, lse4 = _attn_fwd(qh, kh, vt, ta)

    s5_params = (a_re[0], a_im[0], log_dt[0], b_re[0], b_im[0], c_re[0], c_im[0], d_skip[0])
    (tmat, camat, bzmat, al), mats_vjp = jax.vjp(_s5_mats, *s5_params)
    del al
    pw_f, pw_b = _s5_scan_powers(a_re[0], a_im[0], log_dt[0], n_steps)
    ys, xprev = _s5_fwd(u, tmat, camat, bzmat, pw_f)

    hsel = (np.arange(WIDTH)[None, :] // 64 == np.arange(HEADS)[:, None]).astype(np.float32)
    (dx2, do, dza, dzb, dga, dgb, dys, dl_row, merged, dmo, ya, dua, yb, dub, yg, dgl,
     db_glu, dgate, dg_final, loss_part) = _mix(o, za, ys, zb, ga, gb, xs, tgt, gate, b_glu, g_final.reshape(1, -1),
                                                w_glu_full, w_up_a_full, w_up_b_full, w_out_full, jnp.asarray(hsel), ts)

    gw_out = _matmul_tn("dw_out", merged, dmo, tw)
    gw_up_a = _matmul_tn("dw_up_a", ya, dua, tw)
    gw_up_b = _matmul_tn("dw_up_b", yb, dub, tw)
    gw_glu = _matmul_tn("dw_glu", yg, dgl, tw)

    du, d_tmat, d_camat, d_bzmat, dal2 = _s5_bwd(u, dys, xprev, tmat, camat, bzmat, pw_b)
    d_al = jnp.concatenate([dal2[:, 0, :STATE] + dal2[:, 0, STATE:], dal2[:, 1, STATE:] - dal2[:, 1, :STATE]], axis=-1)
    gs5 = mats_vjp((d_tmat, d_camat, d_bzmat, d_al))

    dl4 = rows4(dl_row)
    dq, dk, dv, dfq4, dfk4 = _attn_bwd(qh, do, kh, v, lse4, dl4, ta)
    d_fcol = dfq4.transpose(0, 2, 1, 3).reshape(HEADS, S).T - dfk4.transpose(1, 0, 2).reshape(S, HEADS)
    dfl, db_f = _fgate_bwd(d_fcol, flc, b_f, ta)

    grad_x, dshift, dscale, dg_norm = _proj_bwd(dq, dk, dv, dza, du, dzb, dga, dgb, dfl, xs, dx2,
                                                shift, scale, g_norm, w_main, w_ft, ts)
    segs = [("dw_q", dq), ("dw_k", dk), ("dw_v", dv), ("dw_f", dfl), ("dw_za", dza), ("dw_u", du), ("dw_zb", dzb),
            ("dw_ga", dga), ("dw_gb", dgb)]
    gw_in = jnp.concatenate([_matmul_tn(nm, hb, d, tw, exact=(nm == "dw_f")) for nm, d in segs], axis=1)

    planes = [gw_in.reshape(D_MODEL, N_DEV, -1).transpose(1, 0, 2),
              gw_glu.reshape(N_DEV, -1, WIDTH),
              gw_up_a.reshape(WIDTH, N_DEV, -1).transpose(1, 0, 2),
              gw_up_b.reshape(WIDTH, N_DEV, -1).transpose(1, 0, 2),
              gw_out.reshape(N_DEV, -1, D_MODEL)]
    planes = [p.astype(MXU_DTYPE) for p in planes]
    dmod = jnp.concatenate([dshift, dscale, dgate], axis=1)
    small_parts = [dmod, dg_norm, db_f, gs5[0], gs5[1], gs5[2], gs5[3], gs5[4], gs5[5], gs5[6], gs5[7],
                   db_glu, dg_final, loss_part]
    n_small = sum(int(np.prod(p.shape)) for p in small_parts)
    rows = -(-n_small // (8 * 128)) * 8
    small = _pack_small(small_parts, rows)
    from_sib = _comm_pair(planes)
    core = lax.axis_index("c")
    chip_planes = []
    for name, p, s in zip(("w_in", "w_glu", "w_up_a", "w_up_b", "w_out"), planes, from_sib):
        own = lax.dynamic_index_in_dim(p.reshape((N_CHIP, 2) + p.shape[1:]), core, axis=1, keepdims=False)
        tr = 256 if own.shape[1] % 256 == 0 else own.shape[1]
        chip_planes.append(_pair_sum("pair_sum_" + name, own, s, tr))
    recv, small_all = _comm_out(chip_planes, small)

    grads, deltas, new_m, new_v = {}, {}, {}, {}

    def put(name, res, shape):
        grads[name], deltas[name], new_m[name], new_v[name] = [r.reshape(shape) for r in res]

    names = ("w_in", "w_glu", "w_up_a", "w_up_b", "w_out")
    for name, pr in zip(names, recv):
        w2 = weights[name][0]
        tr = 256 if w2.shape[0] % 256 == 0 else w2.shape[0]
        put(name, _adamw("adamw_" + name, pr, w2, mom_m[name][0], mom_v[name][0], tr), weights[name].shape)
    cols = w_ada.shape[2]
    dmod_all = small_all[:, :24, :].reshape(N_DEV, 3 * D_MODEL)
    dmod_cols = lax.dynamic_slice_in_dim(dmod_all, me * cols, cols, axis=1)
    g_wada = _wada_grad(c_all, dmod_cols)
    put("w_ada", _adamw("adamw_w_ada", g_wada, w_ada[0], m_w_ada[0], v_w_ada[0], 256), w_ada.shape)
    pack = lambda d: _pack_small([d[n] for n in SMALL_ORDER] + [jnp.zeros((1,), F32)], rows)
    res_small = _adamw("adamw_small", small_all, pack(weights), pack(mom_m), pack(mom_v), rows)
    flat = [r.reshape(-1) for r in res_small]
    off = 0
    for name in SMALL_ORDER:
        shape = weights[name].shape
        size = int(np.prod(shape))
        put(name, [f[off:off + size] for f in flat], shape)
        off += size
    loss = flat[0][off]

    return (loss, grad_x[None], *[grads[n] for n in ALL_ORDER], *[deltas[n] for n in ALL_ORDER],
            *[new_m[n] for n in ALL_ORDER], *[new_v[n] for n in ALL_ORDER])
```

```python
import functools
import math

import jax
import jax.numpy as jnp
import numpy as np
from jax import lax
from jax.experimental import pallas as pl
from jax.experimental.pallas import tpu as pltpu

F32 = jnp.float32
MXU_DTYPE = jnp.bfloat16
HI = lax.Precision.HIGHEST

N_DEV = 8
D_MODEL = 1024
WIDTH = 512
HEADS = 8
PAIRS = HEADS // 2
GROUPS = 32
STATE = 64
CG = 16
CHUNK = 16
EPS = 1e-6
NEG = float(np.finfo(np.float32).min)

ADAM_LR = 0.001
ADAM_B1 = 0.9
ADAM_B2 = 0.999
ADAM_EPS = 1e-08
ADAM_WD = 0.01
ADAM_STEP = 10

VMEM_BIG = 56 * 1024 * 1024
VMEM_MID = 40 * 1024 * 1024

OFF_F = 3 * WIDTH
PROJ_WIDTH = 5128
M_Q, M_K, M_V, M_ZA, M_U, M_ZB, M_GA, M_GB = 0, 512, 1024, 1536, 2048, 2560, 3072, 4096


def _mm(a, b):
    return jnp.dot(a.astype(MXU_DTYPE), b.astype(MXU_DTYPE), preferred_element_type=F32)


def _mm_nt(a, b):
    return lax.dot_general(a.astype(MXU_DTYPE), b.astype(MXU_DTYPE), (((1,), (1,)), ((), ())),
                           preferred_element_type=F32)


def _mm_tn(a, b):
    return lax.dot_general(a.astype(MXU_DTYPE), b.astype(MXU_DTYPE), (((0,), (0,)), ((), ())),
                           preferred_element_type=F32)


def _mm32(a, b):
    return jnp.dot(a, b, precision=HI, preferred_element_type=F32)


def _mm32_nt(a, b):
    return lax.dot_general(a, b, (((1,), (1,)), ((), ())), precision=HI, preferred_element_type=F32)


def _mm32_tn(a, b):
    return lax.dot_general(a, b, (((0,), (0,)), ((), ())), precision=HI, preferred_element_type=F32)


def _sigmoid(x):
    return 1.0 / (1.0 + jnp.exp(-x))


def _params(sem=None, vmem=None):
    kw = {}
    if sem is not None:
        kw["dimension_semantics"] = sem
    if vmem is not None:
        kw["vmem_limit_bytes"] = vmem
    return pltpu.CompilerParams(**kw)


def _my_index():
    return 4 * lax.axis_index("x") + 2 * lax.axis_index("y") + lax.axis_index("c")


def _dev(p):
    return (p // 4, (p // 2) % 2, p % 2)


ANY = pl.BlockSpec(memory_space=pl.ANY)
VMEM = pl.BlockSpec(memory_space=pltpu.VMEM)
MESH = pl.DeviceIdType.MESH


def _comm_in(c, w_ada, b_ada8, shards):
    n = len(shards)
    cols = w_ada.shape[1]

    def body(c_ref, wada_ref, bada_ref, *rest):
        srcs = rest[:n]
        mod_ref, call_ref = rest[n], rest[n + 1]
        dsts = rest[n + 2:2 * n + 2]
        modp, wsend, wrecv, wloc, csend, crecv, msend, mrecv = rest[2 * n + 2:]
        me = _my_index()

        x, y, cc = lax.axis_index("x"), lax.axis_index("y"), lax.axis_index("c")
        sib = (x, y, 1 - cc)
        chips = [(1 - x, y), (x, 1 - y), (1 - x, 1 - y)]

        def wcopy(a, k, block, to, src=None):
            ref = dsts[a].at[4 * block[0] + 2 * block[1] + block[2]]
            return pltpu.make_async_remote_copy(src_ref=ref if src is None else src, dst_ref=ref,
                                                send_sem=wsend.at[a, k], recv_sem=wrecv.at[a, k],
                                                device_id=to, device_id_type=MESH)

        def ccopy(src_dev, d, to):
            return pltpu.make_async_remote_copy(src_ref=c_ref, dst_ref=call_ref.at[pl.ds(src_dev, 1)],
                                                send_sem=csend.at[d], recv_sem=crecv.at[src_dev],
                                                device_id=_dev(to), device_id_type=MESH)

        def mcopy(src_dev, d, to):
            return pltpu.make_async_remote_copy(src_ref=modp.at[pl.ds(to, 1)], dst_ref=mod_ref.at[pl.ds(src_dev, 1)],
                                                send_sem=msend.at[d], recv_sem=mrecv.at[src_dev],
                                                device_id=_dev(to), device_id_type=MESH)

        local = [pltpu.make_async_copy(srcs[a], dsts[a].at[me], wloc.at[a]) for a in range(n)]
        for cp in local:
            cp.start()
        peers = [(me + d) % N_DEV for d in range(1, N_DEV)]
        first = []
        for a in range(n):
            first.append(wcopy(a, 0, (x, y, cc), sib, src=srcs[a]))
            first += [wcopy(a, 1 + j, (x, y, cc), (*chip, cc), src=srcs[a]) for j, chip in enumerate(chips)]
        for cp in first:
            cp.start()
        call_ref[pl.ds(me, 1), :] = c_ref[...]
        for d, p in enumerate(peers):
            ccopy(me, d, p).start()
        for d, p in enumerate(peers):
            ccopy(p, d, p).wait_recv()
        modp[...] = _mm32(call_ref[...], wada_ref[...]) + bada_ref[pl.ds(me, 1), :]
        mod_ref[pl.ds(me, 1), :] = modp[pl.ds(me, 1), :]
        for d, p in enumerate(peers):
            mcopy(me, d, p).start()
        for d, p in enumerate(peers):
            mcopy(p, d, p).wait_recv()
        passed = []
        for j, chip in enumerate(chips):
            for a in range(n):
                wcopy(a, 1 + j, (*chip, cc), (x, y, cc)).wait_recv()
                cp = wcopy(a, 4 + j, (*chip, cc), sib)
                cp.start()
                passed.append(cp)
        for a in range(n):
            wcopy(a, 0, sib, (x, y, cc)).wait_recv()
            for j, chip in enumerate(chips):
                wcopy(a, 4 + j, (*chip, 1 - cc), (x, y, cc)).wait_recv()
        for cp in first + passed:
            cp.wait_send()
        for d, p in enumerate(peers):
            ccopy(me, d, p).wait_send()
            mcopy(me, d, p).wait_send()
        for cp in local:
            cp.wait()

    out_shape = ([jax.ShapeDtypeStruct((N_DEV, cols), F32), jax.ShapeDtypeStruct((N_DEV, D_MODEL), F32)]
                 + [jax.ShapeDtypeStruct((N_DEV,) + s.shape, s.dtype) for s in shards])
    res = pl.pallas_call(
        body, name="comm_in", out_shape=out_shape,
        in_specs=[VMEM, VMEM, VMEM] + [ANY] * n,
        out_specs=[VMEM, VMEM] + [ANY] * n,
        scratch_shapes=[pltpu.VMEM((N_DEV, cols), F32),
                        pltpu.SemaphoreType.DMA((n, N_DEV)), pltpu.SemaphoreType.DMA((n, N_DEV)),
                        pltpu.SemaphoreType.DMA((n,)),
                        pltpu.SemaphoreType.DMA((N_DEV,)), pltpu.SemaphoreType.DMA((N_DEV,)),
                        pltpu.SemaphoreType.DMA((N_DEV,)), pltpu.SemaphoreType.DMA((N_DEV,))],
        compiler_params=_params(vmem=VMEM_MID),
    )(c, w_ada, b_ada8, *shards)
    return res[0], res[1], list(res[2:])


N_CHIP = 4


def _comm_pair(planes):
    n = len(planes)

    def body(*rest):
        srcs, dsts = rest[:n], rest[n:2 * n]
        send, recv = rest[2 * n:]
        x, y, cc = lax.axis_index("x"), lax.axis_index("y"), lax.axis_index("c")
        copies = [pltpu.make_async_remote_copy(src_ref=srcs[a].at[2 * ch + 1 - cc], dst_ref=dsts[a].at[ch],
                                               send_sem=send.at[a, ch], recv_sem=recv.at[a, ch],
                                               device_id=(x, y, 1 - cc), device_id_type=MESH)
                  for a in range(n) for ch in range(N_CHIP)]
        for cp in copies:
            cp.start()
        for cp in copies:
            cp.wait()

    out_shape = [jax.ShapeDtypeStruct((N_CHIP,) + p.shape[1:], p.dtype) for p in planes]
    return pl.pallas_call(
        body, name="comm_pair", out_shape=out_shape, in_specs=[ANY] * n, out_specs=[ANY] * n,
        scratch_shapes=[pltpu.SemaphoreType.DMA((n, N_CHIP)), pltpu.SemaphoreType.DMA((n, N_CHIP))],
    )(*planes)


def _pair_sum(name, a, b, tr):
    _, R, C = a.shape

    def body(a_ref, b_ref, o_ref):
        o_ref[...] = (a_ref[...].astype(F32) + b_ref[...].astype(F32)).astype(o_ref.dtype)

    blk = pl.BlockSpec((None, tr, C), lambda i, j: (i, j, 0))
    return pl.pallas_call(
        body, name=name, grid=(N_CHIP, R // tr), in_specs=[blk, blk], out_specs=blk,
        out_shape=jax.ShapeDtypeStruct(a.shape, a.dtype),
        compiler_params=_params(("parallel", "parallel"), VMEM_MID),
    )(a, b)


def _comm_out(chip_planes, small):
    n = len(chip_planes)

    def body(*rest):
        srcs = rest[:n]
        small_ref = rest[n]
        dsts = rest[n + 1:2 * n + 1]
        sall_ref = rest[2 * n + 1]
        wsend, wrecv, wloc, ssend, srecv, sloc = rest[2 * n + 2:]
        me = _my_index()
        x, y, cc = lax.axis_index("x"), lax.axis_index("y"), lax.axis_index("c")
        mine = 2 * x + y
        chips = [(1 - x, y), (x, 1 - y), (1 - x, 1 - y)]

        def wcopy(a, j, sending):
            chip = chips[j]
            there = 2 * chip[0] + chip[1]
            return pltpu.make_async_remote_copy(src_ref=srcs[a].at[there], dst_ref=dsts[a].at[mine if sending else there],
                                                send_sem=wsend.at[a, j], recv_sem=wrecv.at[a, j],
                                                device_id=(*chip, cc), device_id_type=MESH)

        def scopy(src_dev, d, to):
            return pltpu.make_async_remote_copy(src_ref=small_ref, dst_ref=sall_ref.at[src_dev],
                                                send_sem=ssend.at[d], recv_sem=srecv.at[src_dev],
                                                device_id=_dev(to), device_id_type=MESH)

        local = [pltpu.make_async_copy(srcs[a].at[mine], dsts[a].at[mine], wloc.at[a]) for a in range(n)]
        local.append(pltpu.make_async_copy(small_ref, sall_ref.at[me], sloc))
        for cp in local:
            cp.start()
        peers = [(me + d) % N_DEV for d in range(1, N_DEV)]
        for d, p in enumerate(peers):
            scopy(me, d, p).start()
        for j in range(len(chips)):
            for a in range(n):
                wcopy(a, j, True).start()
        for d, p in enumerate(peers):
            scopy(p, d, p).wait_recv()
        for j in range(len(chips)):
            for a in range(n):
                wcopy(a, j, False).wait_recv()
        for d, p in enumerate(peers):
            scopy(me, d, p).wait_send()
        for j in range(len(chips)):
            for a in range(n):
                wcopy(a, j, True).wait_send()
        for cp in local:
            cp.wait()

    out_shape = ([jax.ShapeDtypeStruct(p.shape, p.dtype) for p in chip_planes]
                 + [jax.ShapeDtypeStruct((N_DEV,) + small.shape, small.dtype)])
    res = pl.pallas_call(
        body, name="comm_out", out_shape=out_shape,
        in_specs=[ANY] * (n + 1), out_specs=[ANY] * (n + 1),
        scratch_shapes=[pltpu.SemaphoreType.DMA((n, N_CHIP)), pltpu.SemaphoreType.DMA((n, N_CHIP)),
                        pltpu.SemaphoreType.DMA((n,)),
                        pltpu.SemaphoreType.DMA((N_DEV,)), pltpu.SemaphoreType.DMA((N_DEV,)),
                        pltpu.SemaphoreType.DMA(())],
    )(*chip_planes, small)
    return list(res[:n]), res[n]


def _proj_fwd(x, shift, scale, g_norm, w_main, w_f, ts):
    S = x.shape[0]

    def body(x_ref, sh_ref, sc_ref, gn_ref, w_ref, wf_ref,
             q_ref, k_ref, v_ref, za_ref, u_ref, zb_ref, ga_ref, gb_ref, flc_ref, h_ref):
        xv = x_ref[...]
        r = lax.rsqrt(jnp.mean(xv * xv, axis=-1, keepdims=True) + EPS)
        h = (xv * r) * gn_ref[...] * (1.0 + sc_ref[...]) + sh_ref[...]
        hb = h.astype(MXU_DTYPE)
        h_ref[...] = hb

        def seg(off, n):
            return jnp.dot(hb, w_ref[:, off:off + n], preferred_element_type=F32)

        q_ref[...] = (seg(M_Q, WIDTH) * 0.125).astype(q_ref.dtype)
        k_ref[...] = seg(M_K, WIDTH).astype(k_ref.dtype)
        v_ref[...] = seg(M_V, WIDTH).astype(v_ref.dtype)
        za_ref[...] = seg(M_ZA, WIDTH)
        u_ref[...] = seg(M_U, WIDTH)
        zb_ref[...] = seg(M_ZB, WIDTH)
        ga_ref[...] = seg(M_GA, D_MODEL)
        gb_ref[...] = seg(M_GB, D_MODEL)
        flc_ref[...] = _mm32(h, wf_ref[...])

    row = lambda n: pl.BlockSpec((ts, n), lambda i: (i, 0))
    full = lambda a: pl.BlockSpec(a.shape, lambda i: (0,) * a.ndim)
    sds = jax.ShapeDtypeStruct
    return pl.pallas_call(
        body, name="proj_fwd", grid=(S // ts,),
        in_specs=[row(D_MODEL), full(shift), full(scale), full(g_norm), full(w_main), full(w_f)],
        out_specs=[row(WIDTH)] * 6 + [row(D_MODEL)] * 2 + [row(HEADS), row(D_MODEL)],
        out_shape=[sds((S, WIDTH), MXU_DTYPE)] * 3 + [sds((S, WIDTH), F32)] * 3 + [sds((S, D_MODEL), F32)] * 2
                  + [sds((S, HEADS), F32), sds((S, D_MODEL), MXU_DTYPE)],
        compiler_params=_params(("parallel",), VMEM_BIG),
    )(x, shift, scale, g_norm, w_main, w_f)


def _log_sigmoid(z):
    return jnp.minimum(z, 0.0) - jnp.log(1.0 + jnp.exp(-jnp.abs(z)))


def _fgate_fwd(flc, bf_row, ts):
    S = flc.shape[0]

    def body(flc_ref, bfr_ref, fc_ref, carry_c):
        @pl.when(pl.program_id(0) == 0)
        def _():
            carry_c[...] = jnp.zeros_like(carry_c)

        ri = lax.broadcasted_iota(jnp.int32, (ts, ts), 0)
        ci = lax.broadcasted_iota(jnp.int32, (ts, ts), 1)
        lower = (ci <= ri).astype(F32)
        fc = _mm32(lower, _log_sigmoid(flc_ref[...] + bfr_ref[...])) + carry_c[...]
        fc_ref[...] = fc
        carry_c[...] = fc[ts - 1:ts, :]

    col = pl.BlockSpec((ts, HEADS), lambda i: (i, 0))
    return pl.pallas_call(
        body, name="fgate_fwd", grid=(S // ts,),
        in_specs=[col, pl.BlockSpec((1, HEADS), lambda i: (0, 0))],
        out_specs=col, out_shape=jax.ShapeDtypeStruct((S, HEADS), F32),
        scratch_shapes=[pltpu.VMEM((1, HEADS), F32)],
        compiler_params=_params(("arbitrary",)),
    )(flc, bf_row)


def _fgate_bwd(dfc, flc, bf_row, ts):
    S = flc.shape[0]
    n = S // ts

    def body(df_ref, flc_ref, bfr_ref, dfl_ref, dbf_ref, carry):
        @pl.when(pl.program_id(0) == 0)
        def _():
            carry[...] = jnp.zeros_like(carry)
            dbf_ref[...] = jnp.zeros_like(dbf_ref)

        ri = lax.broadcasted_iota(jnp.int32, (ts, ts), 0)
        ci = lax.broadcasted_iota(jnp.int32, (ts, ts), 1)
        upper = (ci >= ri).astype(F32)
        rc = _mm32(upper, df_ref[...]) + carry[...]
        carry[...] = rc[0:1, :]
        z = flc_ref[...] + bfr_ref[...]
        dfl = rc * _sigmoid(-z)
        dfl_ref[...] = dfl
        dbf_ref[...] += jnp.sum(dfl, axis=0, keepdims=True)

    col = pl.BlockSpec((ts, HEADS), lambda i: (n - 1 - i, 0))
    one = pl.BlockSpec((1, HEADS), lambda i: (0, 0))
    return pl.pallas_call(
        body, name="fgate_bwd", grid=(n,),
        in_specs=[col, col, one], out_specs=[col, one],
        out_shape=[jax.ShapeDtypeStruct((S, HEADS), F32), jax.ShapeDtypeStruct((1, HEADS), F32)],
        scratch_shapes=[pltpu.VMEM((1, HEADS), F32)],
        compiler_params=_params(("arbitrary",)),
    )(dfc, flc, bf_row)


N_EXTRA = 3


def _attn_prep(q, k, v, fcol, t):
    S = q.shape[0]
    nb = S // t

    def body(q_ref, k_ref, v_ref, f_ref, qh_ref, kh_ref, vt_ref):
        lane = lax.broadcasted_iota(jnp.int32, (t, 128), 1)
        f = f_ref[...]
        for p in range(PAIRS):
            qp = q_ref[:, p * 128:(p + 1) * 128]
            kp = k_ref[:, p * 128:(p + 1) * 128]
            vt_ref[p, 0] = v_ref[:, p * 128:(p + 1) * 128].T
            for h in range(2):
                own = (lane < 64) if h == 0 else (lane >= 64)
                base = 64 if h == 0 else 0
                fh = f[:, 2 * p + h:2 * p + h + 1]
                parts = []
                rest = fh
                for _ in range(N_EXTRA):
                    part = rest.astype(qh_ref.dtype)
                    parts.append(part)
                    rest = rest - part.astype(F32)
                one = jnp.ones((t, 1), qh_ref.dtype)
                eq = jnp.zeros((t, 128), qh_ref.dtype)
                ek = jnp.zeros((t, 128), qh_ref.dtype)
                for j in range(N_EXTRA):
                    eq = jnp.where(lane == base + j, parts[j], eq)
                    eq = jnp.where(lane == base + N_EXTRA + j, one, eq)
                    ek = jnp.where(lane == base + j, one, ek)
                    ek = jnp.where(lane == base + N_EXTRA + j, -parts[j], ek)
                qh_ref[2 * p + h] = jnp.where(own, qp, eq)
                kh_ref[2 * p + h] = jnp.where(own, kp, ek)

    row = pl.BlockSpec((t, WIDTH), lambda i: (i, 0))
    heads = pl.BlockSpec((HEADS, t, 128), lambda i: (0, i, 0))
    return pl.pallas_call(
        body, name="attn_prep", grid=(nb,),
        in_specs=[row, row, row, pl.BlockSpec((t, HEADS), lambda i: (i, 0))],
        out_specs=[heads, heads, pl.BlockSpec((PAIRS, 1, 128, t), lambda i: (0, i, 0, 0))],
        out_shape=[jax.ShapeDtypeStruct((HEADS, S, 128), q.dtype), jax.ShapeDtypeStruct((HEADS, S, 128), k.dtype),
                   jax.ShapeDtypeStruct((PAIRS, nb, 128, t), v.dtype)],
        compiler_params=_params(("parallel",), VMEM_MID),
    )(q, k, v, fcol)


def _attn_fwd(qh, kh, vt, t):
    S = qh.shape[1]
    nb = S // t

    def body(q_ref, k_ref, vt_ref, o_ref, lse_ref, acc_s):
        qi = pl.program_id(1)
        acc_s[...] = jnp.zeros_like(acc_s)

        def step(ki, nblk, masked, carry):
            m_old, l_old = carry[:2], carry[2:]
            ks = pl.multiple_of(ki * t, t)
            rows = nblk * t
            sts = [_mm_nt(k_ref[h, pl.ds(ks, rows), :], q_ref[h]) for h in range(2)]
            if masked:
                ri = lax.broadcasted_iota(jnp.int32, (t, t), 0)
                ci = lax.broadcasted_iota(jnp.int32, (t, t), 1)
                sts = [jnp.where(ci >= ri, st, NEG) for st in sts]
            m_new = [jnp.maximum(m_old[h], jnp.max(sts[h], axis=0, keepdims=True)) for h in range(2)]
            alpha = [jnp.exp(m_old[h] - m_new[h]) for h in range(2)]
            pts = [jnp.exp(sts[h] - m_new[h]) for h in range(2)]
            l_new = [alpha[h] * l_old[h] + jnp.sum(pts[h], axis=0, keepdims=True) for h in range(2)]
            for h in range(2):
                pv = _mm(vt_ref[ki], pts[h][:t])
                for b in range(1, nblk):
                    pv = pv + _mm(vt_ref[ki + b], pts[h][b * t:(b + 1) * t])
                acc_s[h] = alpha[h] * acc_s[h] + pv
            return (*m_new, *l_new)

        init = (jnp.full((1, t), -jnp.inf, F32),) * 2 + (jnp.zeros((1, t), F32),) * 2
        carry = lax.fori_loop(0, qi // 2, lambda j, c: step(2 * j, 2, False, c), init)
        carry = lax.cond(qi % 2 == 1, lambda c: step(qi - 1, 1, False, c), lambda c: c, carry)
        m0, m1, l0, l1 = step(qi, 1, True, carry)
        first = lax.broadcasted_iota(jnp.int32, (128, t), 0) < 64
        o_ref[...] = jnp.where(first, acc_s[0] / l0, acc_s[1] / l1).T
        lse_ref[...] = jnp.concatenate([m0 + jnp.log(l0), m1 + jnp.log(l1)], axis=0)

    return pl.pallas_call(
        body, name="attn_fwd", grid=(PAIRS, nb),
        in_specs=[pl.BlockSpec((2, t, 128), lambda p, i: (p, i, 0)), pl.BlockSpec((2, S, 128), lambda p, i: (p, 0, 0)),
                  pl.BlockSpec((None, nb, 128, t), lambda p, i: (p, 0, 0, 0))],
        out_specs=[pl.BlockSpec((t, 128), lambda p, i: (i, p)), pl.BlockSpec((None, None, 2, t), lambda p, i: (p, i, 0, 0))],
        out_shape=[jax.ShapeDtypeStruct((S, WIDTH), F32), jax.ShapeDtypeStruct((PAIRS, nb, 2, t), F32)],
        scratch_shapes=[pltpu.VMEM((2, 128, t), F32)],
        compiler_params=_params(("parallel", "parallel"), VMEM_MID),
    )(qh, kh, vt)


def _attn_bwd(qh, do, kh, v, lse4, dl4, t):
    S = qh.shape[1]
    nb = S // t

    def body(q_ref, do_ref, k_ref, v_ref, lse_ref, dl_ref,
             dq_ref, dk_ref, dv_ref, dfq_ref, dfk_ref, kc_s, vh_s, dk_s, dv_s, dfk_s):
        kj = pl.program_id(1)
        lane = lax.broadcasted_iota(jnp.int32, (t, 128), 1)
        is_a = lane < 64

        @pl.when(kj == 0)
        def _():
            dq_ref[...] = jnp.zeros_like(dq_ref)
            dfq_ref[...] = jnp.zeros_like(dfq_ref)

        vp = v_ref[...]
        zero = jnp.zeros_like(vp)
        kc_s[0] = jnp.where(is_a, k_ref[0], zero.astype(kc_s.dtype))
        kc_s[1] = jnp.where(is_a, zero.astype(kc_s.dtype), k_ref[1])
        vh_s[0] = jnp.where(is_a, vp, zero)
        vh_s[1] = jnp.where(is_a, zero, vp)
        dk_s[...] = jnp.zeros_like(dk_s)
        dv_s[...] = jnp.zeros_like(dv_s)
        dfk_s[...] = jnp.zeros_like(dfk_s)

        def step(qi, masked):
            qs = pl.multiple_of(qi * t, t)
            dob = do_ref[pl.ds(qs, t), :]
            lse = lse_ref[qi]
            dl = dl_ref[qi]
            zq = jnp.zeros_like(dob)
            over_keys = []
            for h in range(2):
                sel = is_a if h == 0 else jnp.logical_not(is_a)
                qb = q_ref[h, pl.ds(qs, t), :]
                st = _mm_nt(k_ref[h], qb) - lse[h:h + 1, :]
                if masked:
                    ri = lax.broadcasted_iota(jnp.int32, (t, t), 0)
                    ci = lax.broadcasted_iota(jnp.int32, (t, t), 1)
                    st = jnp.where(ci >= ri, st, NEG)
                pt = jnp.exp(st)
                dv_s[...] += _mm(pt, jnp.where(sel, dob, zq))
                dpt = _mm_nt(vh_s[h], dob)
                dst = pt * (dpt - dl[h:h + 1, :])
                dfk_s[h] += jnp.sum(dst, axis=1, keepdims=True)
                over_keys.append(jnp.sum(dst, axis=0, keepdims=True))
                dk_s[...] += _mm(dst, jnp.where(sel, qb, jnp.zeros_like(qb)))
                dq_ref[pl.ds(qs, t), :] += _mm_tn(dst, kc_s[h])
            dfq_ref[qi] += jnp.concatenate(over_keys, axis=0)

        step(kj, True)

        def loop_body(qi, carry):
            step(qi, False)
            return carry

        lax.fori_loop(kj + 1, nb, loop_body, 0)
        dk_ref[...] = dk_s[...].astype(dk_ref.dtype)
        dv_ref[...] = dv_s[...].astype(dv_ref.dtype)
        dfk_ref[...] = jnp.where(lax.broadcasted_iota(jnp.int32, (t, 2), 1) == 0, dfk_s[0], dfk_s[1])

        @pl.when(kj == nb - 1)
        def _():
            dq_ref[...] = dq_ref[...] * 0.125

    blk = pl.BlockSpec((t, 128), lambda p, j: (j, p))
    res = pl.BlockSpec((S, 128), lambda p, j: (0, p))
    rows4 = pl.BlockSpec((None, nb, 2, t), lambda p, j: (p, 0, 0, 0))
    cols4 = pl.BlockSpec((None, t, 2), lambda p, j: (p, j, 0))
    return pl.pallas_call(
        body, name="attn_bwd", grid=(PAIRS, nb),
        in_specs=[pl.BlockSpec((2, S, 128), lambda p, j: (p, 0, 0)), res,
                  pl.BlockSpec((2, t, 128), lambda p, j: (p, j, 0)), blk, rows4, rows4],
        out_specs=[res, blk, blk, rows4, cols4],
        out_shape=[jax.ShapeDtypeStruct((S, WIDTH), F32), jax.ShapeDtypeStruct((S, WIDTH), MXU_DTYPE),
                   jax.ShapeDtypeStruct((S, WIDTH), MXU_DTYPE), jax.ShapeDtypeStruct((PAIRS, nb, 2, t), F32),
                   jax.ShapeDtypeStruct((PAIRS, S, 2), F32)],
        scratch_shapes=[pltpu.VMEM((2, t, 128), kh.dtype), pltpu.VMEM((2, t, 128), v.dtype),
                        pltpu.VMEM((t, 128), F32), pltpu.VMEM((t, 128), F32), pltpu.VMEM((2, t, 1), F32)],
        compiler_params=_params(("parallel", "arbitrary"), VMEM_MID),
    )(qh, do, kh, v, lse4, dl4)


def _s5_mats(a_re, a_im, log_dt, b_re, b_im, c_re, c_im, d_skip):
    Lc = CHUNK
    dt = jnp.exp(log_dt)[:, None]
    lr, li = a_re * dt, a_im * dt

    def apow(n):
        n = jnp.asarray(n, F32)[None, :, None]
        mag = jnp.exp(n * lr[:, None, :])
        ang = n * li[:, None, :]
        return mag * jnp.cos(ang), mag * jnp.sin(ang)

    ar, ai = apow([1.0])
    ar, ai = ar[:, 0], ai[:, 0]
    den = a_re * a_re + a_im * a_im
    nr, ni = ar - 1.0, ai
    fr = (nr * a_re + ni * a_im) / den
    fi = (ni * a_re - nr * a_im) / den
    bbr = fr[:, :, None] * b_re - fi[:, :, None] * b_im
    bbi = fr[:, :, None] * b_im + fi[:, :, None] * b_re
    steps = np.arange(Lc, dtype=np.float32)
    pr, pi = apow(steps)
    car = c_re[:, None] * pr[:, :, None, :] - c_im[:, None] * pi[:, :, None, :]
    cai = c_re[:, None] * pi[:, :, None, :] + c_im[:, None] * pr[:, :, None, :]
    kern = (jnp.einsum('glcp,gpd->glcd', car, bbr, precision=HI)
            - jnp.einsum('glcp,gpd->glcd', cai, bbi, precision=HI))
    skip = d_skip.reshape(GROUPS, CG)[:, :, None] * jnp.eye(CG, dtype=F32)[None]
    kern = kern.at[:, 0].add(skip)
    trow = kern.transpose(0, 3, 1, 2).reshape(GROUPS, CG, Lc * CG)
    p1r, p1i = apow(steps + 1.0)
    cr = c_re[:, None] * p1r[:, :, None, :] - c_im[:, None] * p1i[:, :, None, :]
    ci = c_re[:, None] * p1i[:, :, None, :] + c_im[:, None] * p1r[:, :, None, :]
    to_rows = lambda m: m.transpose(0, 3, 1, 2).reshape(GROUPS, STATE, Lc * CG)
    camat = jnp.concatenate([to_rows(cr), -to_rows(ci)], axis=1)
    qr, qi = apow(Lc - 1.0 - steps)
    zr = qr[:, :, None, :] * bbr.transpose(0, 2, 1)[:, None] - qi[:, :, None, :] * bbi.transpose(0, 2, 1)[:, None]
    zi = qr[:, :, None, :] * bbi.transpose(0, 2, 1)[:, None] + qi[:, :, None, :] * bbr.transpose(0, 2, 1)[:, None]
    bzmat = jnp.concatenate([zr, zi], axis=-1).reshape(GROUPS, Lc * CG, 2 * STATE)
    lr_, li_ = apow([float(Lc)])
    al = jnp.concatenate([lr_[:, 0], li_[:, 0]], axis=-1)
    return trow, camat, bzmat, al


def _s5_scan_powers(a_re, a_im, log_dt, n_steps):
    dt = jnp.exp(log_dt)[:, None]
    lr, li = a_re * dt, a_im * dt
    n = (CHUNK * 2.0 ** np.arange(n_steps)).astype(np.float32)[None, :, None]
    mag = jnp.exp(n * lr[:, None, :])
    pr, pi = mag * jnp.cos(n * li[:, None, :]), mag * jnp.sin(n * li[:, None, :])
    fwd = jnp.stack([jnp.concatenate([pr, pr], -1), jnp.concatenate([-pi, pi], -1)], axis=2)
    bwd = jnp.stack([jnp.concatenate([pr, pr], -1), jnp.concatenate([pi, -pi], -1)], axis=2)
    return fwd, bwd


def _shift_rows(x, sh, down):
    n = x.shape[0]
    ri = lax.broadcasted_iota(jnp.int32, x.shape, 0)
    if down:
        return jnp.where(ri >= sh, pltpu.roll(x, sh, 0), 0.0)
    return jnp.where(ri < n - sh, pltpu.roll(x, n - sh, 0), 0.0)


GPB = 128 // CG


def _group_chunks(ref, nch, g):
    lane = lax.broadcasted_iota(jnp.int32, (nch, 128), 1)
    halves = []
    for half in range(CHUNK * CG // 128):
        acc = jnp.zeros((nch, 128), F32)
        for l8 in range(GPB):
            x = ref[pl.ds(half * GPB + l8, nch, stride=CHUNK), :]
            x = pltpu.roll(x, lax.rem(CG * l8 - CG * g + 128, 128), 1)
            acc = jnp.where((lane >= CG * l8) & (lane < CG * l8 + CG), x, acc)
        halves.append(acc)
    return jnp.concatenate(halves, axis=1)


def _store_group(ref, yc, nch, g):
    lane = lax.broadcasted_iota(jnp.int32, (nch, 128), 1)
    mine = (lane >= CG * g) & (lane < CG * g + CG)
    for l in range(CHUNK):
        half, l8 = divmod(l, GPB)
        x = pltpu.roll(yc[:, half * 128:(half + 1) * 128], lax.rem(CG * g - CG * l8 + 128, 128), 1)
        rows = pl.ds(l, nch, stride=CHUNK)
        ref[rows, :] = jnp.where(mine, x, ref[rows, :])


def _toeplitz(trow):
    lane = lax.broadcasted_iota(jnp.int32, (CG, 128), 1)
    x0, x1 = trow[:, :128], trow[:, 128:]
    zero = jnp.zeros_like(x0)
    rows = []
    for s in range(CHUNK):
        sh = (CG * s) % 128
        r0 = pltpu.roll(x0, sh, 1) if sh else x0
        r1 = pltpu.roll(x1, sh, 1) if sh else x1
        if CG * s < 128:
            rows.append(jnp.concatenate([jnp.where(lane >= sh, r0, zero), jnp.where(lane >= sh, r1, r0)], axis=1))
        else:
            rows.append(jnp.concatenate([zero, jnp.where(lane >= sh, r0, zero)], axis=1))
    return jnp.concatenate(rows, axis=0)


def _toeplitz_adjoint(dt):
    lane = lax.broadcasted_iota(jnp.int32, (CG, 128), 1)
    acc0 = jnp.zeros((CG, 128), F32)
    acc1 = jnp.zeros((CG, 128), F32)
    for s in range(CHUNK):
        x0, x1 = dt[CG * s:CG * s + CG, :128], dt[CG * s:CG * s + CG, 128:]
        sh = (CG * s) % 128
        keep = 128 - sh
        r0 = pltpu.roll(x0, keep, 1) if sh else x0
        r1 = pltpu.roll(x1, keep, 1) if sh else x1
        if CG * s < 128:
            acc0 = acc0 + jnp.where(lane < keep, r0, r1)
            acc1 = acc1 + jnp.where(lane < keep, r1, 0.0)
        else:
            acc0 = acc0 + jnp.where(lane < keep, r1, 0.0)
    return jnp.concatenate([acc0, acc1], axis=1)


def _s5_fwd(u, trow, camat, bzmat, pw):
    S = u.shape[0]
    nch = S // CHUNK
    n_steps = pw.shape[1]

    def body(u_ref, t_ref, ca_ref, bz_ref, pw_ref, y_ref, xp_ref):
        g = pl.program_id(1)

        @pl.when(g == 0)
        def _():
            y_ref[...] = jnp.zeros_like(y_ref)

        uc = _group_chunks(u_ref, nch, g)
        x = _mm32(uc, bz_ref[...])
        for kk in range(n_steps):
            xs = _shift_rows(x, 2 ** kk, True)
            m = pw_ref[kk]
            x = x + m[0:1, :] * xs + m[1:2, :] * pltpu.roll(xs, STATE, 1)
        xp = _shift_rows(x, 1, True)
        xp_ref[...] = xp
        _store_group(y_ref, _mm32(uc, _toeplitz(t_ref[...])) + _mm32(xp, ca_ref[...]), nch, g)

    per = lambda a: pl.BlockSpec((None,) + a.shape[1:], lambda b, g: (b * GPB + g,) + (0,) * (a.ndim - 1))
    nat = pl.BlockSpec((S, 128), lambda b, g: (0, b))
    return pl.pallas_call(
        body, name="s5_fwd", grid=(GROUPS // GPB, GPB),
        in_specs=[nat, per(trow), per(camat), per(bzmat), per(pw)],
        out_specs=[nat, pl.BlockSpec((None, nch, 2 * STATE), lambda b, g: (b * GPB + g, 0, 0))],
        out_shape=[jax.ShapeDtypeStruct((S, GROUPS * CG), F32), jax.ShapeDtypeStruct((GROUPS, nch, 2 * STATE), F32)],
        compiler_params=_params(("parallel", "arbitrary"), VMEM_MID),
    )(u, trow, camat, bzmat, pw)


def _s5_bwd(u, dy, xp, trow, camat, bzmat, pwc):
    S = u.shape[0]
    nch = S // CHUNK
    n_steps = pwc.shape[1]

    def body(u_ref, dy_ref, xp_ref, t_ref, ca_ref, bz_ref, pw_ref, du_ref, dt_ref, dca_ref, dbz_ref, dal_ref):
        g = pl.program_id(1)

        @pl.when(g == 0)
        def _():
            du_ref[...] = jnp.zeros_like(du_ref)

        uc = _group_chunks(u_ref, nch, g)
        dyc = _group_chunks(dy_ref, nch, g)
        xpv = xp_ref[...]
        dt_ref[...] = _toeplitz_adjoint(_mm32_tn(uc, dyc))
        dca_ref[...] = _mm32_tn(xpv, dyc)
        dx = _shift_rows(_mm32_nt(dyc, ca_ref[...]), 1, False)
        for kk in range(n_steps):
            xs = _shift_rows(dx, 2 ** kk, False)
            m = pw_ref[kk]
            dx = dx + m[0:1, :] * xs + m[1:2, :] * pltpu.roll(xs, STATE, 1)
        dbz_ref[...] = _mm32_tn(uc, dx)
        dal_ref[0:1, :] = jnp.sum(dx * xpv, axis=0, keepdims=True)
        dal_ref[1:2, :] = jnp.sum(dx * pltpu.roll(xpv, STATE, 1), axis=0, keepdims=True)
        _store_group(du_ref, _mm32_nt(dyc, _toeplitz(t_ref[...])) + _mm32_nt(dx, bz_ref[...]), nch, g)

    per = lambda a: pl.BlockSpec((None,) + a.shape[1:], lambda b, g: (b * GPB + g,) + (0,) * (a.ndim - 1))
    nat = pl.BlockSpec((S, 128), lambda b, g: (0, b))
    sds = jax.ShapeDtypeStruct
    mats = [sds(trow.shape, F32), sds(camat.shape, F32), sds(bzmat.shape, F32), sds((GROUPS, 2, 2 * STATE), F32)]
    return pl.pallas_call(
        body, name="s5_bwd", grid=(GROUPS // GPB, GPB),
        in_specs=[nat, nat, per(xp), per(trow), per(camat), per(bzmat), per(pwc)],
        out_specs=[nat] + [per(o) for o in mats], out_shape=[sds((S, GROUPS * CG), F32)] + mats,
        compiler_params=_params(("parallel", "arbitrary"), VMEM_BIG),
    )(u, dy, xp, trow, camat, bzmat, pwc)


GELU_C0 = math.sqrt(2.0 / math.pi)
GELU_C1 = 0.044715


def _mix(o, za, ys, zb, ga, gb, x, tgt, gate, b_glu, g_final, w_glu, w_up_a, w_up_b, w_out, hsel, ts):
    S = o.shape[0]

    def body(o_ref, za_ref, ys_ref, zb_ref, ga_ref, gb_ref, x_ref, t_ref, gate_ref, bglu_ref, gf_ref,
             wglu_ref, wua_ref, wub_ref, wout_ref, hsel_ref,
             dx2_ref, do_ref, dza_ref, dzb_ref, dga_ref, dgb_ref, dys_ref, dl_ref,
             mg_ref, dmo_ref, ya_ref, dua_ref, yb_ref, dub_ref, yg_ref, dgl_ref,
             dbglu_ref, dgate_ref, dgf_ref, loss_ref):
        @pl.when(pl.program_id(0) == 0)
        def _():
            dbglu_ref[...] = jnp.zeros_like(dbglu_ref)
            dgate_ref[...] = jnp.zeros_like(dgate_ref)
            dgf_ref[...] = jnp.zeros_like(dgf_ref)
            loss_ref[...] = jnp.zeros_like(loss_ref)

        ov = o_ref[...]
        za = za_ref[...]
        sza = _sigmoid(za)
        silu_a = za * sza
        ya = ov * silu_a
        ysv = ys_ref[...]
        th = jnp.tanh(GELU_C0 * (ysv + GELU_C1 * ysv * ysv * ysv))
        yg = 0.5 * ysv * (1.0 + th)
        sg = _sigmoid(_mm(yg, wglu_ref[...]) + bglu_ref[...])
        yb1 = yg * sg
        zb = zb_ref[...]
        szb = _sigmoid(zb)
        silu_b = zb * szb
        yb = yb1 * silu_b
        ua = _mm(ya, wua_ref[...])
        ub = _mm(yb, wub_ref[...])
        sa = _sigmoid(ga_ref[...])
        sb = _sigmoid(gb_ref[...])
        merged = sa * ua + sb * ub
        mo = _mm(merged, wout_ref[...])
        gate_v = gate_ref[...]
        x2 = x_ref[...] + gate_v * mo
        r2 = lax.rsqrt(jnp.mean(x2 * x2, axis=-1, keepdims=True) + EPS)
        x2n = x2 * r2
        gf = gf_ref[...]
        diff = x2n * gf - t_ref[...]
        loss_ref[...] += jnp.sum(jnp.sum(diff * diff, axis=-1, keepdims=True), axis=0, keepdims=True) * (0.5 / D_MODEL)
        dy = diff * (1.0 / D_MODEL)
        dgf_ref[...] += jnp.sum(dy * x2n, axis=0, keepdims=True)
        dyg = dy * gf
        dx2 = r2 * (dyg - x2n * jnp.mean(dyg * x2n, axis=-1, keepdims=True))
        dx2_ref[...] = dx2
        dgate_ref[...] += jnp.sum(dx2 * mo, axis=0, keepdims=True)
        dmo = dx2 * gate_v
        dmerged = _mm_nt(dmo, wout_ref[...])
        dua = dmerged * sa
        dub = dmerged * sb
        dga_ref[...] = (dmerged * ua * sa * (1.0 - sa)).astype(dga_ref.dtype)
        dgb_ref[...] = (dmerged * ub * sb * (1.0 - sb)).astype(dgb_ref.dtype)
        dya = _mm_nt(dua, wua_ref[...])
        dyb = _mm_nt(dub, wub_ref[...])
        dov = dya * silu_a
        do_ref[...] = dov.astype(do_ref.dtype)
        dl_ref[...] = _mm32_nt(hsel_ref[...], dov * ov)
        dza_ref[...] = (dya * ov * (sza * (1.0 + za * (1.0 - sza)))).astype(dza_ref.dtype)
        dyb1 = dyb * silu_b
        dzb_ref[...] = (dyb * yb1 * (szb * (1.0 + zb * (1.0 - szb)))).astype(dzb_ref.dtype)
        dgl = dyb1 * yg * sg * (1.0 - sg)
        dbglu_ref[...] += jnp.sum(dgl, axis=0, keepdims=True)
        dyg2 = dyb1 * sg + _mm_nt(dgl, wglu_ref[...])
        dgelu = 0.5 * (1.0 + th) + 0.5 * ysv * (1.0 - th * th) * GELU_C0 * (1.0 + 3.0 * GELU_C1 * ysv * ysv)
        dys_ref[...] = dyg2 * dgelu
        mg_ref[...] = merged.astype(mg_ref.dtype)
        dmo_ref[...] = dmo.astype(dmo_ref.dtype)
        ya_ref[...] = ya.astype(ya_ref.dtype)
        dua_ref[...] = dua.astype(dua_ref.dtype)
        yb_ref[...] = yb.astype(yb_ref.dtype)
        dub_ref[...] = dub.astype(dub_ref.dtype)
        yg_ref[...] = yg.astype(yg_ref.dtype)
        dgl_ref[...] = dgl.astype(dgl_ref.dtype)

    row = lambda n: pl.BlockSpec((ts, n), lambda i: (i, 0))
    full = lambda a: pl.BlockSpec(a.shape, lambda i: (0,) * a.ndim)
    vec = lambda n: pl.BlockSpec((1, n), lambda i: (0, 0))
    sds = jax.ShapeDtypeStruct
    W, Dm = WIDTH, D_MODEL
    return pl.pallas_call(
        body, name="mix", grid=(S // ts,),
        in_specs=[row(W), row(W), row(W), row(W), row(Dm), row(Dm), row(Dm), row(Dm),
                  full(gate), full(b_glu), full(g_final), full(w_glu), full(w_up_a), full(w_up_b), full(w_out), full(hsel)],
        out_specs=[row(Dm), row(W), row(W), row(W), row(Dm), row(Dm), row(W), pl.BlockSpec((HEADS, ts), lambda i: (0, i)),
                   row(Dm), row(Dm), row(W), row(Dm), row(W), row(Dm), row(W), row(W),
                   vec(W), vec(Dm), vec(Dm), vec(1)],
        out_shape=[sds((S, Dm), F32), sds((S, W), MXU_DTYPE), sds((S, W), MXU_DTYPE), sds((S, W), MXU_DTYPE),
                   sds((S, Dm), MXU_DTYPE), sds((S, Dm), MXU_DTYPE), sds((S, W), F32), sds((HEADS, S), F32),
                   sds((S, Dm), MXU_DTYPE), sds((S, Dm), MXU_DTYPE), sds((S, W), MXU_DTYPE), sds((S, Dm), MXU_DTYPE),
                   sds((S, W), MXU_DTYPE), sds((S, Dm), MXU_DTYPE), sds((S, W), MXU_DTYPE), sds((S, W), MXU_DTYPE),
                   sds((1, W), F32), sds((1, Dm), F32), sds((1, Dm), F32), sds((1, 1), F32)],
        compiler_params=_params(("arbitrary",), VMEM_BIG),
    )(o, za, ys, zb, ga, gb, x, tgt, gate, b_glu, g_final, w_glu, w_up_a, w_up_b, w_out, hsel)


def _matmul_tn(name, a, b, ts, exact=False):
    S, M = a.shape
    N = b.shape[1]
    tn = min(N, 512)
    mm = _mm32_tn if exact else _mm_tn

    def body(a_ref, b_ref, o_ref):
        @pl.when(pl.program_id(1) == 0)
        def _():
            o_ref[...] = jnp.zeros_like(o_ref)

        av, bv = a_ref[...], b_ref[...]
        if exact:
            av, bv = av.astype(F32), bv.astype(F32)
        o_ref[...] += mm(av, bv)

    return pl.pallas_call(
        body, name=name, grid=(N // tn, S // ts),
        in_specs=[pl.BlockSpec((ts, M), lambda j, i: (i, 0)), pl.BlockSpec((ts, tn), lambda j, i: (i, j))],
        out_specs=pl.BlockSpec((M, tn), lambda j, i: (0, j)),
        out_shape=jax.ShapeDtypeStruct((M, N), F32),
        compiler_params=_params(("parallel", "arbitrary"), VMEM_MID),
    )(a, b)


def _proj_bwd(dq, dk, dv, dza, du, dzb, dga, dgb, dfl, x, dx2, shift, scale, g_norm, w_main, w_ft, ts):
    S = x.shape[0]

    def body(dq_ref, dk_ref, dv_ref, dza_ref, du_ref, dzb_ref, dga_ref, dgb_ref, dfl_ref, x_ref, dx2_ref,
             sc_ref, gn_ref, w_ref, wft_ref, gx_ref, dsh_ref, dsc_ref, dgn_ref):
        @pl.when(pl.program_id(0) == 0)
        def _():
            dsh_ref[...] = jnp.zeros_like(dsh_ref)
            dsc_ref[...] = jnp.zeros_like(dsc_ref)
            dgn_ref[...] = jnp.zeros_like(dgn_ref)

        def seg(ref, off, n):
            return _mm_nt(ref[...], w_ref[:, off:off + n])

        dh = (seg(dq_ref, M_Q, WIDTH) + seg(dk_ref, M_K, WIDTH) + seg(dv_ref, M_V, WIDTH)
              + seg(dza_ref, M_ZA, WIDTH) + seg(du_ref, M_U, WIDTH) + seg(dzb_ref, M_ZB, WIDTH)
              + seg(dga_ref, M_GA, D_MODEL) + seg(dgb_ref, M_GB, D_MODEL)
              + _mm32(dfl_ref[...], wft_ref[...]))
        xv = x_ref[...]
        r = lax.rsqrt(jnp.mean(xv * xv, axis=-1, keepdims=True) + EPS)
        xn = xv * r
        gn = gn_ref[...]
        s1 = 1.0 + sc_ref[...]
        dsh_ref[...] += jnp.sum(dh, axis=0, keepdims=True)
        dhx = dh * xn
        dsc_ref[...] += jnp.sum(dhx, axis=0, keepdims=True) * gn
        dgn_ref[...] += jnp.sum(dhx, axis=0, keepdims=True) * s1
        dxn = dh * (gn * s1)
        gx_ref[...] = dx2_ref[...] + r * (dxn - xn * jnp.mean(dxn * xn, axis=-1, keepdims=True))

    row = lambda n: pl.BlockSpec((ts, n), lambda i: (i, 0))
    full = lambda a: pl.BlockSpec(a.shape, lambda i: (0,) * a.ndim)
    vec = pl.BlockSpec((1, D_MODEL), lambda i: (0, 0))
    W, Dm = WIDTH, D_MODEL
    del shift
    return pl.pallas_call(
        body, name="proj_bwd", grid=(S // ts,),
        in_specs=[row(W)] * 6 + [row(Dm)] * 2 + [row(HEADS), row(Dm), row(Dm),
                                                 full(scale), full(g_norm), full(w_main), full(w_ft)],
        out_specs=[row(Dm), vec, vec, vec],
        out_shape=[jax.ShapeDtypeStruct((S, Dm), F32)] + [jax.ShapeDtypeStruct((1, Dm), F32)] * 3,
        compiler_params=_params(("arbitrary",), VMEM_BIG),
    )(dq, dk, dv, dza, du, dzb, dga, dgb, dfl, x, dx2, scale, g_norm, w_main, w_ft)


def _adamw(name, planes, w, m, v, tr):
    n, R, C = planes.shape
    bc1 = 1.0 - ADAM_B1 ** ADAM_STEP
    bc2 = 1.0 - ADAM_B2 ** ADAM_STEP

    def body(p_ref, w_ref, m_ref, v_ref, g_ref, d_ref, nm_ref, nv_ref):
        g = p_ref[0].astype(F32)
        for i in range(1, n):
            g = g + p_ref[i].astype(F32)
        g_ref[...] = g
        nm = ADAM_B1 * m_ref[...] + (1.0 - ADAM_B1) * g
        nv = ADAM_B2 * v_ref[...] + (1.0 - ADAM_B2) * (g * g)
        nm_ref[...] = nm
        nv_ref[...] = nv
        d_ref[...] = -ADAM_LR * ((nm / bc1) / (jnp.sqrt(nv / bc2) + ADAM_EPS) + ADAM_WD * w_ref[...])

    blk = pl.BlockSpec((tr, C), lambda i: (i, 0))
    return pl.pallas_call(
        body, name=name, grid=(R // tr,),
        in_specs=[pl.BlockSpec((n, tr, C), lambda i: (0, i, 0)), blk, blk, blk],
        out_specs=[blk] * 4, out_shape=[jax.ShapeDtypeStruct((R, C), F32)] * 4,
        compiler_params=_params(("parallel",), VMEM_MID),
    )(planes, w, m, v)


def _wada_grad(c_all, dmod_cols):
    def body(c_ref, d_ref, o_ref):
        o_ref[0] = _mm32_tn(c_ref[...], d_ref[...])

    return pl.pallas_call(
        body, name="wada_grad",
        out_shape=jax.ShapeDtypeStruct((1, c_all.shape[1], dmod_cols.shape[1]), F32),
        in_specs=[VMEM, VMEM], out_specs=VMEM,
    )(c_all, dmod_cols)


SMALL_ORDER = ("b_ada", "g_norm", "b_f", "a_re", "a_im", "log_dt", "b_re", "b_im", "c_re", "c_im",
               "d_skip", "b_glu", "g_final")
BIG_ORDER = ("w_ada", "w_in", "w_glu", "w_up_a", "w_up_b", "w_out")
ALL_ORDER = ("w_ada", "b_ada", "g_norm", "w_in", "b_f", "a_re", "a_im", "log_dt", "b_re", "b_im", "c_re", "c_im",
             "d_skip", "w_glu", "b_glu", "w_up_a", "w_up_b", "w_out", "g_final")


def _pack_small(parts, rows):
    flat = jnp.concatenate([p.reshape(-1).astype(F32) for p in parts])
    return jnp.pad(flat, (0, rows * 128 - flat.shape[0])).reshape(rows, 128)


def kernel(x, c, w_ada, b_ada, g_norm, w_in, b_f, a_re, a_im, log_dt, b_re, b_im, c_re, c_im, d_skip, w_glu, b_glu, w_up_a, w_up_b, w_out, g_final, loss_target, m_w_ada, m_b_ada, m_g_norm, m_w_in, m_b_f, m_a_re, m_a_im, m_log_dt, m_b_re, m_b_im, m_c_re, m_c_im, m_d_skip, m_w_glu, m_b_glu, m_w_up_a, m_w_up_b, m_w_out, m_g_final, v_w_ada, v_b_ada, v_g_norm, v_w_in, v_b_f, v_a_re, v_a_im, v_log_dt, v_b_re, v_b_im, v_c_re, v_c_im, v_d_skip, v_w_glu, v_b_glu, v_w_up_a, v_w_up_b, v_w_out, v_g_final):
    weights = dict(w_ada=w_ada, b_ada=b_ada, g_norm=g_norm, w_in=w_in, b_f=b_f, a_re=a_re, a_im=a_im, log_dt=log_dt,
                   b_re=b_re, b_im=b_im, c_re=c_re, c_im=c_im, d_skip=d_skip, w_glu=w_glu, b_glu=b_glu,
                   w_up_a=w_up_a, w_up_b=w_up_b, w_out=w_out, g_final=g_final)
    mom_m = dict(w_ada=m_w_ada, b_ada=m_b_ada, g_norm=m_g_norm, w_in=m_w_in, b_f=m_b_f, a_re=m_a_re, a_im=m_a_im,
                 log_dt=m_log_dt, b_re=m_b_re, b_im=m_b_im, c_re=m_c_re, c_im=m_c_im, d_skip=m_d_skip, w_glu=m_w_glu,
                 b_glu=m_b_glu, w_up_a=m_w_up_a, w_up_b=m_w_up_b, w_out=m_w_out, g_final=m_g_final)
    mom_v = dict(w_ada=v_w_ada, b_ada=v_b_ada, g_norm=v_g_norm, w_in=v_w_in, b_f=v_b_f, a_re=v_a_re, a_im=v_a_im,
                 log_dt=v_log_dt, b_re=v_b_re, b_im=v_b_im, c_re=v_c_re, c_im=v_c_im, d_skip=v_d_skip, w_glu=v_w_glu,
                 b_glu=v_b_glu, w_up_a=v_w_up_a, w_up_b=v_w_up_b, w_out=v_w_out, g_final=v_g_final)
    xs = x[0]
    tgt = loss_target[0]
    S = xs.shape[0]
    ts = min(256, S)
    ta = min(512, S)
    tw = min(1024, S)
    nch = S // CHUNK
    n_steps = max(1, int(math.ceil(math.log2(nch))))
    me = _my_index()

    shards = [w.astype(MXU_DTYPE) for w in (w_in[0], w_glu[0], w_up_a[0], w_up_b[0], w_out[0])]
    mod8, c_all, gathered = _comm_in(c, w_ada[0], b_ada.reshape(N_DEV, -1), shards)
    mod = mod8.reshape(1, 3 * D_MODEL)
    shift, scale, gate = mod[:, :D_MODEL], mod[:, D_MODEL:2 * D_MODEL], mod[:, 2 * D_MODEL:]
    w_in_full = gathered[0].transpose(1, 0, 2).reshape(D_MODEL, PROJ_WIDTH)
    w_main = jnp.concatenate([w_in_full[:, :OFF_F], w_in_full[:, OFF_F + HEADS:]], axis=1)
    w_f = w_in_full[:, OFF_F:OFF_F + HEADS].astype(F32)
    w_ft = w_f.T
    w_glu_full = gathered[1].reshape(WIDTH, WIDTH)
    w_up_a_full = gathered[2].transpose(1, 0, 2).reshape(WIDTH, D_MODEL)
    w_up_b_full = gathered[3].transpose(1, 0, 2).reshape(WIDTH, D_MODEL)
    w_out_full = gathered[4].reshape(D_MODEL, D_MODEL)

    q, k, v, za, u, zb, ga, gb, flc, hb = _proj_fwd(xs, shift, scale, g_norm, w_main, w_f, ts)
    fcol = _fgate_fwd(flc, b_f, ta)
    nb = S // ta
    rows4 = lambda r: r.reshape(PAIRS, 2, nb, ta).transpose(0, 2, 1, 3)
    qh, kh, vt = _attn_prep(q, k, v, fcol, ta)
    o, lse4 = _attn_fwd(qh, kh, vt, ta)

    s5_params = (a_re[0], a_im[0], log_dt[0], b_re[0], b_im[0], c_re[0], c_im[0], d_skip[0])
    (trow, camat, bzmat, al), mats_vjp = jax.vjp(_s5_mats, *s5_params)
    del al
    pw_f, pw_b = _s5_scan_powers(a_re[0], a_im[0], log_dt[0], n_steps)
    ys, xprev = _s5_fwd(u, trow, camat, bzmat, pw_f)

    hsel = (np.arange(WIDTH)[None, :] // 64 == np.arange(HEADS)[:, None]).astype(np.float32)
    (dx2, do, dza, dzb, dga, dgb, dys, dl_row, merged, dmo, ya, dua, yb, dub, yg, dgl,
     db_glu, dgate, dg_final, loss_part) = _mix(o, za, ys, zb, ga, gb, xs, tgt, gate, b_glu, g_final.reshape(1, -1),
                                                w_glu_full, w_up_a_full, w_up_b_full, w_out_full, jnp.asarray(hsel), ts)

    gw_out = _matmul_tn("dw_out", merged, dmo, tw)
    gw_up_a = _matmul_tn("dw_up_a", ya, dua, tw)
    gw_up_b = _matmul_tn("dw_up_b", yb, dub, tw)
    gw_glu = _matmul_tn("dw_glu", yg, dgl, tw)

    du, d_trow, d_camat, d_bzmat, dal2 = _s5_bwd(u, dys, xprev, trow, camat, bzmat, pw_b)
    d_al = jnp.concatenate([dal2[:, 0, :STATE] + dal2[:, 0, STATE:], dal2[:, 1, STATE:] - dal2[:, 1, :STATE]], axis=-1)
    gs5 = mats_vjp((d_trow, d_camat, d_bzmat, d_al))

    dl4 = rows4(dl_row)
    dq, dk, dv, dfq4, dfk4 = _attn_bwd(qh, do, kh, v, lse4, dl4, ta)
    d_fcol = dfq4.transpose(0, 2, 1, 3).reshape(HEADS, S).T - dfk4.transpose(1, 0, 2).reshape(S, HEADS)
    dfl, db_f = _fgate_bwd(d_fcol, flc, b_f, ta)

    grad_x, dshift, dscale, dg_norm = _proj_bwd(dq, dk, dv, dza, du, dzb, dga, dgb, dfl, xs, dx2,
                                                shift, scale, g_norm, w_main, w_ft, ts)
    segs = [("dw_q", dq), ("dw_k", dk), ("dw_v", dv), ("dw_f", dfl), ("dw_za", dza), ("dw_u", du), ("dw_zb", dzb),
            ("dw_ga", dga), ("dw_gb", dgb)]
    gw_in = jnp.concatenate([_matmul_tn(nm, hb, d, tw) for nm, d in segs], axis=1)

    planes = [gw_in.reshape(D_MODEL, N_DEV, -1).transpose(1, 0, 2),
              gw_glu.reshape(N_DEV, -1, WIDTH),
              gw_up_a.reshape(WIDTH, N_DEV, -1).transpose(1, 0, 2),
              gw_up_b.reshape(WIDTH, N_DEV, -1).transpose(1, 0, 2),
              gw_out.reshape(N_DEV, -1, D_MODEL)]
    planes = [p.astype(MXU_DTYPE) for p in planes]
    dmod = jnp.concatenate([dshift, dscale, dgate], axis=1)
    small_parts = [dmod, dg_norm, db_f, gs5[0], gs5[1], gs5[2], gs5[3], gs5[4], gs5[5], gs5[6], gs5[7],
                   db_glu, dg_final, loss_part]
    n_small = sum(int(np.prod(p.shape)) for p in small_parts)
    rows = -(-n_small // (8 * 128)) * 8
    small = _pack_small(small_parts, rows)
    from_sib = _comm_pair(planes)
    core = lax.axis_index("c")
    chip_planes = []
    for name, p, s in zip(("w_in", "w_glu", "w_up_a", "w_up_b", "w_out"), planes, from_sib):
        own = lax.dynamic_index_in_dim(p.reshape((N_CHIP, 2) + p.shape[1:]), core, axis=1, keepdims=False)
        tr = 256 if own.shape[1] % 256 == 0 else own.shape[1]
        chip_planes.append(_pair_sum("pair_sum_" + name, own, s, tr))
    recv, small_all = _comm_out(chip_planes, small)

    grads, deltas, new_m, new_v = {}, {}, {}, {}

    def put(name, res, shape):
        grads[name], deltas[name], new_m[name], new_v[name] = [r.reshape(shape) for r in res]

    names = ("w_in", "w_glu", "w_up_a", "w_up_b", "w_out")
    for name, pr in zip(names, recv):
        w2 = weights[name][0]
        tr = 256 if w2.shape[0] % 256 == 0 else w2.shape[0]
        put(name, _adamw("adamw_" + name, pr, w2, mom_m[name][0], mom_v[name][0], tr), weights[name].shape)
    cols = w_ada.shape[2]
    dmod_all = small_all[:, :24, :].reshape(N_DEV, 3 * D_MODEL)
    dmod_cols = lax.dynamic_slice_in_dim(dmod_all, me * cols, cols, axis=1)
    g_wada = _wada_grad(c_all, dmod_cols)
    put("w_ada", _adamw("adamw_w_ada", g_wada, w_ada[0], m_w_ada[0], v_w_ada[0], 256), w_ada.shape)
    pack = lambda d: _pack_small([d[n] for n in SMALL_ORDER] + [jnp.zeros((1,), F32)], rows)
    res_small = _adamw("adamw_small", small_all, pack(weights), pack(mom_m), pack(mom_v), rows)
    flat = [r.reshape(-1) for r in res_small]
    off = 0
    for name in SMALL_ORDER:
        shape = weights[name].shape
        size = int(np.prod(shape))
        put(name, [f[off:off + size] for f in flat], shape)
        off += size
    loss = flat[0][off]

    return (loss, grad_x[None], *[grads[n] for n in ALL_ORDER], *[deltas[n] for n in ALL_ORDER],
            *[new_m[n] for n in ALL_ORDER], *[new_v[n] for n in ALL_ORDER])
```

```python
import functools
import math

import jax
import jax.numpy as jnp
import numpy as np
from jax import lax
from jax.experimental import pallas as pl
from jax.experimental.pallas import tpu as pltpu

F32 = jnp.float32
MXU_DTYPE = jnp.bfloat16
HI = lax.Precision.HIGHEST

N_DEV = 8
D_MODEL = 1024
WIDTH = 512
HEADS = 8
PAIRS = HEADS // 2
GROUPS = 32
STATE = 64
CG = 16
CHUNK = 16
EPS = 1e-6
NEG = float(np.finfo(np.float32).min)

ADAM_LR = 0.001
ADAM_B1 = 0.9
ADAM_B2 = 0.999
ADAM_EPS = 1e-08
ADAM_WD = 0.01
ADAM_STEP = 10

VMEM_BIG = 56 * 1024 * 1024
VMEM_MID = 40 * 1024 * 1024

OFF_F = 3 * WIDTH
PROJ_WIDTH = 5128
M_Q, M_K, M_V, M_ZA, M_U, M_ZB, M_GA, M_GB = 0, 512, 1024, 1536, 2048, 2560, 3072, 4096


def _mm(a, b):
    return jnp.dot(a.astype(MXU_DTYPE), b.astype(MXU_DTYPE), preferred_element_type=F32)


def _mm_nt(a, b):
    return lax.dot_general(a.astype(MXU_DTYPE), b.astype(MXU_DTYPE), (((1,), (1,)), ((), ())),
                           preferred_element_type=F32)


def _mm_tn(a, b):
    return lax.dot_general(a.astype(MXU_DTYPE), b.astype(MXU_DTYPE), (((0,), (0,)), ((), ())),
                           preferred_element_type=F32)


def _mm32(a, b):
    return jnp.dot(a, b, precision=HI, preferred_element_type=F32)


def _mm32_nt(a, b):
    return lax.dot_general(a, b, (((1,), (1,)), ((), ())), precision=HI, preferred_element_type=F32)


def _mm32_tn(a, b):
    return lax.dot_general(a, b, (((0,), (0,)), ((), ())), precision=HI, preferred_element_type=F32)


def _sigmoid(x):
    return 1.0 / (1.0 + jnp.exp(-x))


def _params(sem=None, vmem=None):
    kw = {}
    if sem is not None:
        kw["dimension_semantics"] = sem
    if vmem is not None:
        kw["vmem_limit_bytes"] = vmem
    return pltpu.CompilerParams(**kw)


def _my_index():
    return 4 * lax.axis_index("x") + 2 * lax.axis_index("y") + lax.axis_index("c")


def _dev(p):
    return (p // 4, (p // 2) % 2, p % 2)


ANY = pl.BlockSpec(memory_space=pl.ANY)
VMEM = pl.BlockSpec(memory_space=pltpu.VMEM)
MESH = pl.DeviceIdType.MESH


def _comm_in(c, w_ada, b_ada8, shards):
    n = len(shards)
    cols = w_ada.shape[1]

    def body(c_ref, wada_ref, bada_ref, *rest):
        srcs = rest[:n]
        mod_ref, call_ref = rest[n], rest[n + 1]
        dsts = rest[n + 2:2 * n + 2]
        modp, wsend, wrecv, wloc, csend, crecv, msend, mrecv = rest[2 * n + 2:]
        me = _my_index()

        x, y, cc = lax.axis_index("x"), lax.axis_index("y"), lax.axis_index("c")
        sib = (x, y, 1 - cc)
        chips = [(1 - x, y), (x, 1 - y), (1 - x, 1 - y)]

        def wcopy(a, k, block, to, src=None):
            ref = dsts[a].at[4 * block[0] + 2 * block[1] + block[2]]
            return pltpu.make_async_remote_copy(src_ref=ref if src is None else src, dst_ref=ref,
                                                send_sem=wsend.at[a, k], recv_sem=wrecv.at[a, k],
                                                device_id=to, device_id_type=MESH)

        def ccopy(src_dev, d, to):
            return pltpu.make_async_remote_copy(src_ref=c_ref, dst_ref=call_ref.at[pl.ds(src_dev, 1)],
                                                send_sem=csend.at[d], recv_sem=crecv.at[src_dev],
                                                device_id=_dev(to), device_id_type=MESH)

        def mcopy(src_dev, d, to):
            return pltpu.make_async_remote_copy(src_ref=modp.at[pl.ds(to, 1)], dst_ref=mod_ref.at[pl.ds(src_dev, 1)],
                                                send_sem=msend.at[d], recv_sem=mrecv.at[src_dev],
                                                device_id=_dev(to), device_id_type=MESH)

        local = [pltpu.make_async_copy(srcs[a], dsts[a].at[me], wloc.at[a]) for a in range(n)]
        for cp in local:
            cp.start()
        peers = [(me + d) % N_DEV for d in range(1, N_DEV)]
        first = []
        for a in range(n):
            first.append(wcopy(a, 0, (x, y, cc), sib, src=srcs[a]))
            first += [wcopy(a, 1 + j, (x, y, cc), (*chip, cc), src=srcs[a]) for j, chip in enumerate(chips)]
        for cp in first:
            cp.start()
        call_ref[pl.ds(me, 1), :] = c_ref[...]
        for d, p in enumerate(peers):
            ccopy(me, d, p).start()
        for d, p in enumerate(peers):
            ccopy(p, d, p).wait_recv()
        modp[...] = _mm32(call_ref[...], wada_ref[...]) + bada_ref[pl.ds(me, 1), :]
        mod_ref[pl.ds(me, 1), :] = modp[pl.ds(me, 1), :]
        for d, p in enumerate(peers):
            mcopy(me, d, p).start()
        for d, p in enumerate(peers):
            mcopy(p, d, p).wait_recv()
        passed = []
        for j, chip in enumerate(chips):
            for a in range(n):
                wcopy(a, 1 + j, (*chip, cc), (x, y, cc)).wait_recv()
                cp = wcopy(a, 4 + j, (*chip, cc), sib)
                cp.start()
                passed.append(cp)
        for a in range(n):
            wcopy(a, 0, sib, (x, y, cc)).wait_recv()
            for j, chip in enumerate(chips):
                wcopy(a, 4 + j, (*chip, 1 - cc), (x, y, cc)).wait_recv()
        for cp in first + passed:
            cp.wait_send()
        for d, p in enumerate(peers):
            ccopy(me, d, p).wait_send()
            mcopy(me, d, p).wait_send()
        for cp in local:
            cp.wait()

    out_shape = ([jax.ShapeDtypeStruct((N_DEV, cols), F32), jax.ShapeDtypeStruct((N_DEV, D_MODEL), F32)]
                 + [jax.ShapeDtypeStruct((N_DEV,) + s.shape, s.dtype) for s in shards])
    res = pl.pallas_call(
        body, name="comm_in", out_shape=out_shape,
        in_specs=[VMEM, VMEM, VMEM] + [ANY] * n,
        out_specs=[VMEM, VMEM] + [ANY] * n,
        scratch_shapes=[pltpu.VMEM((N_DEV, cols), F32),
                        pltpu.SemaphoreType.DMA((n, N_DEV)), pltpu.SemaphoreType.DMA((n, N_DEV)),
                        pltpu.SemaphoreType.DMA((n,)),
                        pltpu.SemaphoreType.DMA((N_DEV,)), pltpu.SemaphoreType.DMA((N_DEV,)),
                        pltpu.SemaphoreType.DMA((N_DEV,)), pltpu.SemaphoreType.DMA((N_DEV,))],
        compiler_params=_params(vmem=VMEM_MID),
    )(c, w_ada, b_ada8, *shards)
    return res[0], res[1], list(res[2:])


N_CHIP = 4


def _comm_pair(planes):
    n = len(planes)

    def body(*rest):
        srcs, dsts = rest[:n], rest[n:2 * n]
        send, recv = rest[2 * n:]
        x, y, cc = lax.axis_index("x"), lax.axis_index("y"), lax.axis_index("c")
        copies = [pltpu.make_async_remote_copy(src_ref=srcs[a].at[2 * ch + 1 - cc], dst_ref=dsts[a].at[ch],
                                               send_sem=send.at[a, ch], recv_sem=recv.at[a, ch],
                                               device_id=(x, y, 1 - cc), device_id_type=MESH)
                  for a in range(n) for ch in range(N_CHIP)]
        for cp in copies:
            cp.start()
        for cp in copies:
            cp.wait()

    out_shape = [jax.ShapeDtypeStruct((N_CHIP,) + p.shape[1:], p.dtype) for p in planes]
    return pl.pallas_call(
        body, name="comm_pair", out_shape=out_shape, in_specs=[ANY] * n, out_specs=[ANY] * n,
        scratch_shapes=[pltpu.SemaphoreType.DMA((n, N_CHIP)), pltpu.SemaphoreType.DMA((n, N_CHIP))],
    )(*planes)


def _pair_sum(name, a, b, tr):
    _, R, C = a.shape

    def body(a_ref, b_ref, o_ref):
        o_ref[...] = (a_ref[...].astype(F32) + b_ref[...].astype(F32)).astype(o_ref.dtype)

    blk = pl.BlockSpec((None, tr, C), lambda i, j: (i, j, 0))
    return pl.pallas_call(
        body, name=name, grid=(N_CHIP, R // tr), in_specs=[blk, blk], out_specs=blk,
        out_shape=jax.ShapeDtypeStruct(a.shape, a.dtype),
        compiler_params=_params(("parallel", "parallel"), VMEM_MID),
    )(a, b)


def _comm_out(chip_planes, small):
    n = len(chip_planes)

    def body(*rest):
        srcs = rest[:n]
        small_ref = rest[n]
        dsts = rest[n + 1:2 * n + 1]
        sall_ref = rest[2 * n + 1]
        wsend, wrecv, wloc, ssend, srecv, sloc = rest[2 * n + 2:]
        me = _my_index()
        x, y, cc = lax.axis_index("x"), lax.axis_index("y"), lax.axis_index("c")
        mine = 2 * x + y
        chips = [(1 - x, y), (x, 1 - y), (1 - x, 1 - y)]

        def wcopy(a, j, sending):
            chip = chips[j]
            there = 2 * chip[0] + chip[1]
            return pltpu.make_async_remote_copy(src_ref=srcs[a].at[there], dst_ref=dsts[a].at[mine if sending else there],
                                                send_sem=wsend.at[a, j], recv_sem=wrecv.at[a, j],
                                                device_id=(*chip, cc), device_id_type=MESH)

        def scopy(src_dev, d, to):
            return pltpu.make_async_remote_copy(src_ref=small_ref, dst_ref=sall_ref.at[src_dev],
                                                send_sem=ssend.at[d], recv_sem=srecv.at[src_dev],
                                                device_id=_dev(to), device_id_type=MESH)

        local = [pltpu.make_async_copy(srcs[a].at[mine], dsts[a].at[mine], wloc.at[a]) for a in range(n)]
        local.append(pltpu.make_async_copy(small_ref, sall_ref.at[me], sloc))
        for cp in local:
            cp.start()
        peers = [(me + d) % N_DEV for d in range(1, N_DEV)]
        for d, p in enumerate(peers):
            scopy(me, d, p).start()
        for j in range(len(chips)):
            for a in range(n):
                wcopy(a, j, True).start()
        for d, p in enumerate(peers):
            scopy(p, d, p).wait_recv()
        for j in range(len(chips)):
            for a in range(n):
                wcopy(a, j, False).wait_recv()
        for d, p in enumerate(peers):
            scopy(me, d, p).wait_send()
        for j in range(len(chips)):
            for a in range(n):
                wcopy(a, j, True).wait_send()
        for cp in local:
            cp.wait()

    out_shape = ([jax.ShapeDtypeStruct(p.shape, p.dtype) for p in chip_planes]
                 + [jax.ShapeDtypeStruct((N_DEV,) + small.shape, small.dtype)])
    res = pl.pallas_call(
        body, name="comm_out", out_shape=out_shape,
        in_specs=[ANY] * (n + 1), out_specs=[ANY] * (n + 1),
        scratch_shapes=[pltpu.SemaphoreType.DMA((n, N_CHIP)), pltpu.SemaphoreType.DMA((n, N_CHIP)),
                        pltpu.SemaphoreType.DMA((n,)),
                        pltpu.SemaphoreType.DMA((N_DEV,)), pltpu.SemaphoreType.DMA((N_DEV,)),
                        pltpu.SemaphoreType.DMA(())],
    )(*chip_planes, small)
    return list(res[:n]), res[n]


def _proj_fwd(x, shift, scale, g_norm, w_main, w_f, ts):
    S = x.shape[0]

    def body(x_ref, sh_ref, sc_ref, gn_ref, w_ref, wf_ref,
             q_ref, k_ref, v_ref, za_ref, u_ref, zb_ref, ga_ref, gb_ref, flc_ref, h_ref):
        xv = x_ref[...]
        r = lax.rsqrt(jnp.mean(xv * xv, axis=-1, keepdims=True) + EPS)
        h = (xv * r) * gn_ref[...] * (1.0 + sc_ref[...]) + sh_ref[...]
        hb = h.astype(MXU_DTYPE)
        h_ref[...] = hb

        def seg(off, n):
            return jnp.dot(hb, w_ref[:, off:off + n], preferred_element_type=F32)

        q_ref[...] = (seg(M_Q, WIDTH) * 0.125).astype(q_ref.dtype)
        k_ref[...] = seg(M_K, WIDTH).astype(k_ref.dtype)
        v_ref[...] = seg(M_V, WIDTH).astype(v_ref.dtype)
        za_ref[...] = seg(M_ZA, WIDTH)
        u_ref[...] = seg(M_U, WIDTH)
        zb_ref[...] = seg(M_ZB, WIDTH)
        ga_ref[...] = seg(M_GA, D_MODEL)
        gb_ref[...] = seg(M_GB, D_MODEL)
        flc_ref[...] = _mm32(h, wf_ref[...])

    row = lambda n: pl.BlockSpec((ts, n), lambda i: (i, 0))
    full = lambda a: pl.BlockSpec(a.shape, lambda i: (0,) * a.ndim)
    sds = jax.ShapeDtypeStruct
    return pl.pallas_call(
        body, name="proj_fwd", grid=(S // ts,),
        in_specs=[row(D_MODEL), full(shift), full(scale), full(g_norm), full(w_main), full(w_f)],
        out_specs=[row(WIDTH)] * 6 + [row(D_MODEL)] * 2 + [row(HEADS), row(D_MODEL)],
        out_shape=[sds((S, WIDTH), MXU_DTYPE)] * 3 + [sds((S, WIDTH), F32)] * 3 + [sds((S, D_MODEL), F32)] * 2
                  + [sds((S, HEADS), F32), sds((S, D_MODEL), MXU_DTYPE)],
        compiler_params=_params(("parallel",), VMEM_BIG),
    )(x, shift, scale, g_norm, w_main, w_f)


def _log_sigmoid(z):
    return jnp.minimum(z, 0.0) - jnp.log(1.0 + jnp.exp(-jnp.abs(z)))


def _fgate_fwd(flc, bf_row, ts):
    S = flc.shape[0]

    def body(flc_ref, bfr_ref, fc_ref, carry_c):
        @pl.when(pl.program_id(0) == 0)
        def _():
            carry_c[...] = jnp.zeros_like(carry_c)

        ri = lax.broadcasted_iota(jnp.int32, (ts, ts), 0)
        ci = lax.broadcasted_iota(jnp.int32, (ts, ts), 1)
        lower = (ci <= ri).astype(F32)
        fc = _mm32(lower, _log_sigmoid(flc_ref[...] + bfr_ref[...])) + carry_c[...]
        fc_ref[...] = fc
        carry_c[...] = fc[ts - 1:ts, :]

    col = pl.BlockSpec((ts, HEADS), lambda i: (i, 0))
    return pl.pallas_call(
        body, name="fgate_fwd", grid=(S // ts,),
        in_specs=[col, pl.BlockSpec((1, HEADS), lambda i: (0, 0))],
        out_specs=col, out_shape=jax.ShapeDtypeStruct((S, HEADS), F32),
        scratch_shapes=[pltpu.VMEM((1, HEADS), F32)],
        compiler_params=_params(("arbitrary",)),
    )(flc, bf_row)


def _fgate_bwd(dfc, flc, bf_row, ts):
    S = flc.shape[0]
    n = S // ts

    def body(df_ref, flc_ref, bfr_ref, dfl_ref, dbf_ref, carry):
        @pl.when(pl.program_id(0) == 0)
        def _():
            carry[...] = jnp.zeros_like(carry)
            dbf_ref[...] = jnp.zeros_like(dbf_ref)

        ri = lax.broadcasted_iota(jnp.int32, (ts, ts), 0)
        ci = lax.broadcasted_iota(jnp.int32, (ts, ts), 1)
        upper = (ci >= ri).astype(F32)
        rc = _mm32(upper, df_ref[...]) + carry[...]
        carry[...] = rc[0:1, :]
        z = flc_ref[...] + bfr_ref[...]
        dfl = rc * _sigmoid(-z)
        dfl_ref[...] = dfl
        dbf_ref[...] += jnp.sum(dfl, axis=0, keepdims=True)

    col = pl.BlockSpec((ts, HEADS), lambda i: (n - 1 - i, 0))
    one = pl.BlockSpec((1, HEADS), lambda i: (0, 0))
    return pl.pallas_call(
        body, name="fgate_bwd", grid=(n,),
        in_specs=[col, col, one], out_specs=[col, one],
        out_shape=[jax.ShapeDtypeStruct((S, HEADS), F32), jax.ShapeDtypeStruct((1, HEADS), F32)],
        scratch_shapes=[pltpu.VMEM((1, HEADS), F32)],
        compiler_params=_params(("arbitrary",)),
    )(dfc, flc, bf_row)


N_EXTRA = 3


def _attn_prep(q, k, v, fcol, t):
    S = q.shape[0]
    nb = S // t

    def body(q_ref, k_ref, v_ref, f_ref, qh_ref, kh_ref, vt_ref):
        lane = lax.broadcasted_iota(jnp.int32, (t, 128), 1)
        f = f_ref[...]
        for p in range(PAIRS):
            qp = q_ref[:, p * 128:(p + 1) * 128]
            kp = k_ref[:, p * 128:(p + 1) * 128]
            vt_ref[p, 0] = v_ref[:, p * 128:(p + 1) * 128].T
            for h in range(2):
                own = (lane < 64) if h == 0 else (lane >= 64)
                base = 64 if h == 0 else 0
                fh = f[:, 2 * p + h:2 * p + h + 1]
                parts = []
                rest = fh
                for _ in range(N_EXTRA):
                    part = rest.astype(qh_ref.dtype)
                    parts.append(part)
                    rest = rest - part.astype(F32)
                one = jnp.ones((t, 1), qh_ref.dtype)
                eq = jnp.zeros((t, 128), qh_ref.dtype)
                ek = jnp.zeros((t, 128), qh_ref.dtype)
                for j in range(N_EXTRA):
                    eq = jnp.where(lane == base + j, parts[j], eq)
                    eq = jnp.where(lane == base + N_EXTRA + j, one, eq)
                    ek = jnp.where(lane == base + j, one, ek)
                    ek = jnp.where(lane == base + N_EXTRA + j, -parts[j], ek)
                qh_ref[2 * p + h] = jnp.where(own, qp, eq)
                kh_ref[2 * p + h] = jnp.where(own, kp, ek)

    row = pl.BlockSpec((t, WIDTH), lambda i: (i, 0))
    heads = pl.BlockSpec((HEADS, t, 128), lambda i: (0, i, 0))
    return pl.pallas_call(
        body, name="attn_prep", grid=(nb,),
        in_specs=[row, row, row, pl.BlockSpec((t, HEADS), lambda i: (i, 0))],
        out_specs=[heads, heads, pl.BlockSpec((PAIRS, 1, 128, t), lambda i: (0, i, 0, 0))],
        out_shape=[jax.ShapeDtypeStruct((HEADS, S, 128), q.dtype), jax.ShapeDtypeStruct((HEADS, S, 128), k.dtype),
                   jax.ShapeDtypeStruct((PAIRS, nb, 128, t), v.dtype)],
        compiler_params=_params(("parallel",), VMEM_MID),
    )(q, k, v, fcol)


def _attn_fwd(qh, kh, vt, t):
    S = qh.shape[1]
    nb = S // t

    def body(q_ref, k_ref, vt_ref, o_ref, lse_ref, acc_s):
        qi = pl.program_id(1)
        acc_s[...] = jnp.zeros_like(acc_s)

        def step(ki, nblk, masked, carry):
            m_old, l_old = carry[:2], carry[2:]
            ks = pl.multiple_of(ki * t, t)
            rows = nblk * t
            sts = [_mm_nt(k_ref[h, pl.ds(ks, rows), :], q_ref[h]) for h in range(2)]
            if masked:
                ri = lax.broadcasted_iota(jnp.int32, (t, t), 0)
                ci = lax.broadcasted_iota(jnp.int32, (t, t), 1)
                sts = [jnp.where(ci >= ri, st, NEG) for st in sts]
            m_new = [jnp.maximum(m_old[h], jnp.max(sts[h], axis=0, keepdims=True)) for h in range(2)]
            alpha = [jnp.exp(m_old[h] - m_new[h]) for h in range(2)]
            pts = [jnp.exp(sts[h] - m_new[h]) for h in range(2)]
            l_new = [alpha[h] * l_old[h] + jnp.sum(pts[h], axis=0, keepdims=True) for h in range(2)]
            for h in range(2):
                pv = _mm(vt_ref[ki], pts[h][:t])
                for b in range(1, nblk):
                    pv = pv + _mm(vt_ref[ki + b], pts[h][b * t:(b + 1) * t])
                acc_s[h] = alpha[h] * acc_s[h] + pv
            return (*m_new, *l_new)

        init = (jnp.full((1, t), -jnp.inf, F32),) * 2 + (jnp.zeros((1, t), F32),) * 2
        carry = lax.fori_loop(0, qi // 2, lambda j, c: step(2 * j, 2, False, c), init)
        carry = lax.cond(qi % 2 == 1, lambda c: step(qi - 1, 1, False, c), lambda c: c, carry)
        m0, m1, l0, l1 = step(qi, 1, True, carry)
        first = lax.broadcasted_iota(jnp.int32, (128, t), 0) < 64
        o_ref[...] = jnp.where(first, acc_s[0] / l0, acc_s[1] / l1).T
        lse_ref[...] = jnp.concatenate([m0 + jnp.log(l0), m1 + jnp.log(l1)], axis=0)

    return pl.pallas_call(
        body, name="attn_fwd", grid=(PAIRS, nb),
        in_specs=[pl.BlockSpec((2, t, 128), lambda p, i: (p, i, 0)), pl.BlockSpec((2, S, 128), lambda p, i: (p, 0, 0)),
                  pl.BlockSpec((None, nb, 128, t), lambda p, i: (p, 0, 0, 0))],
        out_specs=[pl.BlockSpec((t, 128), lambda p, i: (i, p)), pl.BlockSpec((None, None, 2, t), lambda p, i: (p, i, 0, 0))],
        out_shape=[jax.ShapeDtypeStruct((S, WIDTH), F32), jax.ShapeDtypeStruct((PAIRS, nb, 2, t), F32)],
        scratch_shapes=[pltpu.VMEM((2, 128, t), F32)],
        compiler_params=_params(("parallel", "parallel"), VMEM_MID),
    )(qh, kh, vt)


def _attn_bwd(qh, do, kh, v, lse4, dl4, t):
    S = qh.shape[1]
    nb = S // t

    def body(q_ref, do_ref, k_ref, v_ref, lse_ref, dl_ref,
             dq_ref, dk_ref, dv_ref, dfq_ref, dfk_ref, kc_s, vh_s, dk_s, dv_s, dfk_s):
        kj = pl.program_id(1)
        lane = lax.broadcasted_iota(jnp.int32, (t, 128), 1)
        is_a = lane < 64

        @pl.when(kj == 0)
        def _():
            dq_ref[...] = jnp.zeros_like(dq_ref)
            dfq_ref[...] = jnp.zeros_like(dfq_ref)

        vp = v_ref[...]
        zero = jnp.zeros_like(vp)
        kc_s[0] = jnp.where(is_a, k_ref[0], zero.astype(kc_s.dtype))
        kc_s[1] = jnp.where(is_a, zero.astype(kc_s.dtype), k_ref[1])
        vh_s[0] = jnp.where(is_a, vp, zero)
        vh_s[1] = jnp.where(is_a, zero, vp)
        dk_s[...] = jnp.zeros_like(dk_s)
        dv_s[...] = jnp.zeros_like(dv_s)
        dfk_s[...] = jnp.zeros_like(dfk_s)

        def step(qi, masked):
            qs = pl.multiple_of(qi * t, t)
            dob = do_ref[pl.ds(qs, t), :]
            lse = lse_ref[qi]
            dl = dl_ref[qi]
            zq = jnp.zeros_like(dob)
            over_keys = []
            for h in range(2):
                sel = is_a if h == 0 else jnp.logical_not(is_a)
                qb = q_ref[h, pl.ds(qs, t), :]
                st = _mm_nt(k_ref[h], qb) - lse[h:h + 1, :]
                if masked:
                    ri = lax.broadcasted_iota(jnp.int32, (t, t), 0)
                    ci = lax.broadcasted_iota(jnp.int32, (t, t), 1)
                    st = jnp.where(ci >= ri, st, NEG)
                pt = jnp.exp(st)
                dv_s[...] += _mm(pt, jnp.where(sel, dob, zq))
                dpt = _mm_nt(vh_s[h], dob)
                dst = pt * (dpt - dl[h:h + 1, :])
                dfk_s[h] += jnp.sum(dst, axis=1, keepdims=True)
                over_keys.append(jnp.sum(dst, axis=0, keepdims=True))
                dk_s[...] += _mm(dst, jnp.where(sel, qb, jnp.zeros_like(qb)))
                dq_ref[pl.ds(qs, t), :] += _mm_tn(dst, kc_s[h])
            dfq_ref[qi] += jnp.concatenate(over_keys, axis=0)

        step(kj, True)

        def loop_body(qi, carry):
            step(qi, False)
            return carry

        lax.fori_loop(kj + 1, nb, loop_body, 0)
        dk_ref[...] = dk_s[...].astype(dk_ref.dtype)
        dv_ref[...] = dv_s[...].astype(dv_ref.dtype)
        dfk_ref[...] = jnp.where(lax.broadcasted_iota(jnp.int32, (t, 2), 1) == 0, dfk_s[0], dfk_s[1])

        @pl.when(kj == nb - 1)
        def _():
            dq_ref[...] = dq_ref[...] * 0.125

    blk = pl.BlockSpec((t, 128), lambda p, j: (j, p))
    res = pl.BlockSpec((S, 128), lambda p, j: (0, p))
    rows4 = pl.BlockSpec((None, nb, 2, t), lambda p, j: (p, 0, 0, 0))
    cols4 = pl.BlockSpec((None, t, 2), lambda p, j: (p, j, 0))
    return pl.pallas_call(
        body, name="attn_bwd", grid=(PAIRS, nb),
        in_specs=[pl.BlockSpec((2, S, 128), lambda p, j: (p, 0, 0)), res,
                  pl.BlockSpec((2, t, 128), lambda p, j: (p, j, 0)), blk, rows4, rows4],
        out_specs=[res, blk, blk, rows4, cols4],
        out_shape=[jax.ShapeDtypeStruct((S, WIDTH), F32), jax.ShapeDtypeStruct((S, WIDTH), MXU_DTYPE),
                   jax.ShapeDtypeStruct((S, WIDTH), MXU_DTYPE), jax.ShapeDtypeStruct((PAIRS, nb, 2, t), F32),
                   jax.ShapeDtypeStruct((PAIRS, S, 2), F32)],
        scratch_shapes=[pltpu.VMEM((2, t, 128), kh.dtype), pltpu.VMEM((2, t, 128), v.dtype),
                        pltpu.VMEM((t, 128), F32), pltpu.VMEM((t, 128), F32), pltpu.VMEM((2, t, 1), F32)],
        compiler_params=_params(("parallel", "arbitrary"), VMEM_MID),
    )(qh, do, kh, v, lse4, dl4)


def _s5_mats(a_re, a_im, log_dt, b_re, b_im, c_re, c_im, d_skip):
    Lc = CHUNK
    dt = jnp.exp(log_dt)[:, None]
    lr, li = a_re * dt, a_im * dt

    def apow(n):
        n = jnp.asarray(n, F32)[None, :, None]
        mag = jnp.exp(n * lr[:, None, :])
        ang = n * li[:, None, :]
        return mag * jnp.cos(ang), mag * jnp.sin(ang)

    ar, ai = apow([1.0])
    ar, ai = ar[:, 0], ai[:, 0]
    den = a_re * a_re + a_im * a_im
    nr, ni = ar - 1.0, ai
    fr = (nr * a_re + ni * a_im) / den
    fi = (ni * a_re - nr * a_im) / den
    bbr = fr[:, :, None] * b_re - fi[:, :, None] * b_im
    bbi = fr[:, :, None] * b_im + fi[:, :, None] * b_re
    steps = np.arange(Lc, dtype=np.float32)
    pr, pi = apow(steps)
    car = c_re[:, None] * pr[:, :, None, :] - c_im[:, None] * pi[:, :, None, :]
    cai = c_re[:, None] * pi[:, :, None, :] + c_im[:, None] * pr[:, :, None, :]
    kern = (jnp.einsum('glcp,gpd->glcd', car, bbr, precision=HI)
            - jnp.einsum('glcp,gpd->glcd', cai, bbi, precision=HI))
    skip = d_skip.reshape(GROUPS, CG)[:, :, None] * jnp.eye(CG, dtype=F32)[None]
    kern = kern.at[:, 0].add(skip)
    trow = kern.transpose(0, 3, 1, 2).reshape(GROUPS, CG, Lc * CG)
    p1r, p1i = apow(steps + 1.0)
    cr = c_re[:, None] * p1r[:, :, None, :] - c_im[:, None] * p1i[:, :, None, :]
    ci = c_re[:, None] * p1i[:, :, None, :] + c_im[:, None] * p1r[:, :, None, :]
    to_rows = lambda m: m.transpose(0, 3, 1, 2).reshape(GROUPS, STATE, Lc * CG)
    camat = jnp.concatenate([to_rows(cr), -to_rows(ci)], axis=1)
    qr, qi = apow(Lc - 1.0 - steps)
    zr = qr[:, :, None, :] * bbr.transpose(0, 2, 1)[:, None] - qi[:, :, None, :] * bbi.transpose(0, 2, 1)[:, None]
    zi = qr[:, :, None, :] * bbi.transpose(0, 2, 1)[:, None] + qi[:, :, None, :] * bbr.transpose(0, 2, 1)[:, None]
    bzmat = jnp.concatenate([zr, zi], axis=-1).reshape(GROUPS, Lc * CG, 2 * STATE)
    lr_, li_ = apow([float(Lc)])
    al = jnp.concatenate([lr_[:, 0], li_[:, 0]], axis=-1)
    return trow, camat, bzmat, al


def _s5_scan_powers(a_re, a_im, log_dt, n_steps):
    dt = jnp.exp(log_dt)[:, None]
    lr, li = a_re * dt, a_im * dt
    n = (CHUNK * 2.0 ** np.arange(n_steps)).astype(np.float32)[None, :, None]
    mag = jnp.exp(n * lr[:, None, :])
    pr, pi = mag * jnp.cos(n * li[:, None, :]), mag * jnp.sin(n * li[:, None, :])
    fwd = jnp.stack([jnp.concatenate([pr, pr], -1), jnp.concatenate([-pi, pi], -1)], axis=2)
    bwd = jnp.stack([jnp.concatenate([pr, pr], -1), jnp.concatenate([pi, -pi], -1)], axis=2)
    return fwd, bwd


def _shift_rows(x, sh, down):
    n = x.shape[0]
    ri = lax.broadcasted_iota(jnp.int32, x.shape, 0)
    if down:
        return jnp.where(ri >= sh, pltpu.roll(x, sh, 0), 0.0)
    return jnp.where(ri < n - sh, pltpu.roll(x, n - sh, 0), 0.0)


GPB = 128 // CG


def _lane_transpose(arrs):
    lane = lax.broadcasted_iota(jnp.int32, arrs[0].shape, 1)
    arrs = list(arrs)
    k = GPB // 2
    while k >= 1:
        hi = ((lane // CG) & k) != 0
        new = list(arrs)
        for i in range(GPB):
            if i & k:
                continue
            lo_arr, hi_arr = arrs[i], arrs[i + k]
            new[i] = jnp.where(hi, pltpu.roll(hi_arr, CG * k, 1), lo_arr)
            new[i + k] = jnp.where(hi, hi_arr, pltpu.roll(lo_arr, 128 - CG * k, 1))
        arrs = new
        k //= 2
    return arrs


def _gather_block(ref, dst, nch):
    for half in range(CHUNK // GPB):
        outs = _lane_transpose([ref[pl.ds(half * GPB + l8, nch, stride=CHUNK), :] for l8 in range(GPB)])
        for g in range(GPB):
            dst[half, g] = outs[g]


def _scatter_block(src, ref, nch):
    for half in range(CHUNK // GPB):
        outs = _lane_transpose([src[half, g] for g in range(GPB)])
        for l8 in range(GPB):
            ref[pl.ds(half * GPB + l8, nch, stride=CHUNK), :] = outs[l8]


def _toeplitz(trow):
    lane = lax.broadcasted_iota(jnp.int32, (CG, 128), 1)
    x0, x1 = trow[:, :128], trow[:, 128:]
    zero = jnp.zeros_like(x0)
    rows = []
    for s in range(CHUNK):
        sh = (CG * s) % 128
        r0 = pltpu.roll(x0, sh, 1) if sh else x0
        r1 = pltpu.roll(x1, sh, 1) if sh else x1
        if CG * s < 128:
            rows.append(jnp.concatenate([jnp.where(lane >= sh, r0, zero), jnp.where(lane >= sh, r1, r0)], axis=1))
        else:
            rows.append(jnp.concatenate([zero, jnp.where(lane >= sh, r0, zero)], axis=1))
    return jnp.concatenate(rows, axis=0)


def _toeplitz_adjoint(dt):
    lane = lax.broadcasted_iota(jnp.int32, (CG, 128), 1)
    acc0 = jnp.zeros((CG, 128), F32)
    acc1 = jnp.zeros((CG, 128), F32)
    for s in range(CHUNK):
        x0, x1 = dt[CG * s:CG * s + CG, :128], dt[CG * s:CG * s + CG, 128:]
        sh = (CG * s) % 128
        keep = 128 - sh
        r0 = pltpu.roll(x0, keep, 1) if sh else x0
        r1 = pltpu.roll(x1, keep, 1) if sh else x1
        if CG * s < 128:
            acc0 = acc0 + jnp.where(lane < keep, r0, r1)
            acc1 = acc1 + jnp.where(lane < keep, r1, 0.0)
        else:
            acc0 = acc0 + jnp.where(lane < keep, r1, 0.0)
    return jnp.concatenate([acc0, acc1], axis=1)


def _s5_fwd(u, trow, camat, bzmat, pw):
    S = u.shape[0]
    nch = S // CHUNK
    n_steps = pw.shape[1]

    def body(u_ref, t_ref, ca_ref, bz_ref, pw_ref, y_ref, xp_ref, uc_ref, ub_s, yb_s):
        g = pl.program_id(1)

        @pl.when(g == 0)
        def _():
            _gather_block(u_ref, ub_s, nch)

        uc = jnp.concatenate([ub_s[0, g], ub_s[1, g]], axis=1)
        uc_ref[...] = uc
        x = _mm32(uc, bz_ref[...])
        for kk in range(n_steps):
            xs = _shift_rows(x, 2 ** kk, True)
            m = pw_ref[kk]
            x = x + m[0:1, :] * xs + m[1:2, :] * pltpu.roll(xs, STATE, 1)
        xp = _shift_rows(x, 1, True)
        xp_ref[...] = xp
        yc = _mm32(uc, _toeplitz(t_ref[...])) + _mm32(xp, ca_ref[...])
        yb_s[0, g] = yc[:, :128]
        yb_s[1, g] = yc[:, 128:]

        @pl.when(g == GPB - 1)
        def _():
            _scatter_block(yb_s, y_ref, nch)

    per = lambda a: pl.BlockSpec((None,) + a.shape[1:], lambda b, g: (b * GPB + g,) + (0,) * (a.ndim - 1))
    nat = pl.BlockSpec((S, 128), lambda b, g: (0, b))
    return pl.pallas_call(
        body, name="s5_fwd", grid=(GROUPS // GPB, GPB),
        in_specs=[nat, per(trow), per(camat), per(bzmat), per(pw)],
        out_specs=[nat, pl.BlockSpec((None, nch, 2 * STATE), lambda b, g: (b * GPB + g, 0, 0)),
                   pl.BlockSpec((None, nch, CHUNK * CG), lambda b, g: (b * GPB + g, 0, 0))],
        out_shape=[jax.ShapeDtypeStruct((S, GROUPS * CG), F32), jax.ShapeDtypeStruct((GROUPS, nch, 2 * STATE), F32),
                   jax.ShapeDtypeStruct((GROUPS, nch, CHUNK * CG), F32)],
        scratch_shapes=[pltpu.VMEM((CHUNK // GPB, GPB, nch, 128), F32)] * 2,
        compiler_params=_params(("parallel", "arbitrary"), VMEM_BIG),
    )(u, trow, camat, bzmat, pw)


def _s5_bwd(uc, dy, xp, trow, camat, bzmat, pwc):
    S = dy.shape[0]
    nch = S // CHUNK
    n_steps = pwc.shape[1]

    def body(uc_ref, dy_ref, xp_ref, t_ref, ca_ref, bz_ref, pw_ref, du_ref, dt_ref, dca_ref, dbz_ref, dal_ref,
             dyb_s, dub_s):
        g = pl.program_id(1)

        @pl.when(g == 0)
        def _():
            _gather_block(dy_ref, dyb_s, nch)

        uc = uc_ref[...]
        dyc = jnp.concatenate([dyb_s[0, g], dyb_s[1, g]], axis=1)
        xpv = xp_ref[...]
        dt_ref[...] = _toeplitz_adjoint(_mm32_tn(uc, dyc))
        dca_ref[...] = _mm32_tn(xpv, dyc)
        dx = _shift_rows(_mm32_nt(dyc, ca_ref[...]), 1, False)
        for kk in range(n_steps):
            xs = _shift_rows(dx, 2 ** kk, False)
            m = pw_ref[kk]
            dx = dx + m[0:1, :] * xs + m[1:2, :] * pltpu.roll(xs, STATE, 1)
        dbz_ref[...] = _mm32_tn(uc, dx)
        dal_ref[0:1, :] = jnp.sum(dx * xpv, axis=0, keepdims=True)
        dal_ref[1:2, :] = jnp.sum(dx * pltpu.roll(xpv, STATE, 1), axis=0, keepdims=True)
        duc = _mm32_nt(dyc, _toeplitz(t_ref[...])) + _mm32_nt(dx, bz_ref[...])
        dub_s[0, g] = duc[:, :128]
        dub_s[1, g] = duc[:, 128:]

        @pl.when(g == GPB - 1)
        def _():
            _scatter_block(dub_s, du_ref, nch)

    per = lambda a: pl.BlockSpec((None,) + a.shape[1:], lambda b, g: (b * GPB + g,) + (0,) * (a.ndim - 1))
    nat = pl.BlockSpec((S, 128), lambda b, g: (0, b))
    sds = jax.ShapeDtypeStruct
    mats = [sds(trow.shape, F32), sds(camat.shape, F32), sds(bzmat.shape, F32), sds((GROUPS, 2, 2 * STATE), F32)]
    return pl.pallas_call(
        body, name="s5_bwd", grid=(GROUPS // GPB, GPB),
        in_specs=[per(uc), nat, per(xp), per(trow), per(camat), per(bzmat), per(pwc)],
        out_specs=[nat] + [per(o) for o in mats], out_shape=[sds((S, GROUPS * CG), F32)] + mats,
        scratch_shapes=[pltpu.VMEM((CHUNK // GPB, GPB, nch, 128), F32)] * 2,
        compiler_params=_params(("parallel", "arbitrary"), VMEM_BIG),
    )(uc, dy, xp, trow, camat, bzmat, pwc)


GELU_C0 = math.sqrt(2.0 / math.pi)
GELU_C1 = 0.044715


def _mix(o, za, ys, zb, ga, gb, x, tgt, gate, b_glu, g_final, w_glu, w_up_a, w_up_b, w_out, hsel, ts):
    S = o.shape[0]

    def body(o_ref, za_ref, ys_ref, zb_ref, ga_ref, gb_ref, x_ref, t_ref, gate_ref, bglu_ref, gf_ref,
             wglu_ref, wua_ref, wub_ref, wout_ref, hsel_ref,
             dx2_ref, do_ref, dza_ref, dzb_ref, dga_ref, dgb_ref, dys_ref, dl_ref,
             mg_ref, dmo_ref, ya_ref, dua_ref, yb_ref, dub_ref, yg_ref, dgl_ref,
             dbglu_ref, dgate_ref, dgf_ref, loss_ref):
        @pl.when(pl.program_id(0) == 0)
        def _():
            dbglu_ref[...] = jnp.zeros_like(dbglu_ref)
            dgate_ref[...] = jnp.zeros_like(dgate_ref)
            dgf_ref[...] = jnp.zeros_like(dgf_ref)
            loss_ref[...] = jnp.zeros_like(loss_ref)

        ov = o_ref[...]
        za = za_ref[...]
        sza = _sigmoid(za)
        silu_a = za * sza
        ya = ov * silu_a
        ysv = ys_ref[...]
        th = jnp.tanh(GELU_C0 * (ysv + GELU_C1 * ysv * ysv * ysv))
        yg = 0.5 * ysv * (1.0 + th)
        sg = _sigmoid(_mm(yg, wglu_ref[...]) + bglu_ref[...])
        yb1 = yg * sg
        zb = zb_ref[...]
        szb = _sigmoid(zb)
        silu_b = zb * szb
        yb = yb1 * silu_b
        ua = _mm(ya, wua_ref[...])
        ub = _mm(yb, wub_ref[...])
        sa = _sigmoid(ga_ref[...])
        sb = _sigmoid(gb_ref[...])
        merged = sa * ua + sb * ub
        mo = _mm(merged, wout_ref[...])
        gate_v = gate_ref[...]
        x2 = x_ref[...] + gate_v * mo
        r2 = lax.rsqrt(jnp.mean(x2 * x2, axis=-1, keepdims=True) + EPS)
        x2n = x2 * r2
        gf = gf_ref[...]
        diff = x2n * gf - t_ref[...]
        loss_ref[...] += jnp.sum(jnp.sum(diff * diff, axis=-1, keepdims=True), axis=0, keepdims=True) * (0.5 / D_MODEL)
        dy = diff * (1.0 / D_MODEL)
        dgf_ref[...] += jnp.sum(dy * x2n, axis=0, keepdims=True)
        dyg = dy * gf
        dx2 = r2 * (dyg - x2n * jnp.mean(dyg * x2n, axis=-1, keepdims=True))
        dx2_ref[...] = dx2
        dgate_ref[...] += jnp.sum(dx2 * mo, axis=0, keepdims=True)
        dmo = dx2 * gate_v
        dmerged = _mm_nt(dmo, wout_ref[...])
        dua = dmerged * sa
        dub = dmerged * sb
        dga_ref[...] = (dmerged * ua * sa * (1.0 - sa)).astype(dga_ref.dtype)
        dgb_ref[...] = (dmerged * ub * sb * (1.0 - sb)).astype(dgb_ref.dtype)
        dya = _mm_nt(dua, wua_ref[...])
        dyb = _mm_nt(dub, wub_ref[...])
        dov = dya * silu_a
        do_ref[...] = dov.astype(do_ref.dtype)
        dl_ref[...] = _mm32_nt(hsel_ref[...], dov * ov)
        dza_ref[...] = (dya * ov * (sza * (1.0 + za * (1.0 - sza)))).astype(dza_ref.dtype)
        dyb1 = dyb * silu_b
        dzb_ref[...] = (dyb * yb1 * (szb * (1.0 + zb * (1.0 - szb)))).astype(dzb_ref.dtype)
        dgl = dyb1 * yg * sg * (1.0 - sg)
        dbglu_ref[...] += jnp.sum(dgl, axis=0, keepdims=True)
        dyg2 = dyb1 * sg + _mm_nt(dgl, wglu_ref[...])
        dgelu = 0.5 * (1.0 + th) + 0.5 * ysv * (1.0 - th * th) * GELU_C0 * (1.0 + 3.0 * GELU_C1 * ysv * ysv)
        dys_ref[...] = dyg2 * dgelu
        mg_ref[...] = merged.astype(mg_ref.dtype)
        dmo_ref[...] = dmo.astype(dmo_ref.dtype)
        ya_ref[...] = ya.astype(ya_ref.dtype)
        dua_ref[...] = dua.astype(dua_ref.dtype)
        yb_ref[...] = yb.astype(yb_ref.dtype)
        dub_ref[...] = dub.astype(dub_ref.dtype)
        yg_ref[...] = yg.astype(yg_ref.dtype)
        dgl_ref[...] = dgl.astype(dgl_ref.dtype)

    row = lambda n: pl.BlockSpec((ts, n), lambda i: (i, 0))
    full = lambda a: pl.BlockSpec(a.shape, lambda i: (0,) * a.ndim)
    vec = lambda n: pl.BlockSpec((1, n), lambda i: (0, 0))
    sds = jax.ShapeDtypeStruct
    W, Dm = WIDTH, D_MODEL
    return pl.pallas_call(
        body, name="mix", grid=(S // ts,),
        in_specs=[row(W), row(W), row(W), row(W), row(Dm), row(Dm), row(Dm), row(Dm),
                  full(gate), full(b_glu), full(g_final), full(w_glu), full(w_up_a), full(w_up_b), full(w_out), full(hsel)],
        out_specs=[row(Dm), row(W), row(W), row(W), row(Dm), row(Dm), row(W), pl.BlockSpec((HEADS, ts), lambda i: (0, i)),
                   row(Dm), row(Dm), row(W), row(Dm), row(W), row(Dm), row(W), row(W),
                   vec(W), vec(Dm), vec(Dm), vec(1)],
        out_shape=[sds((S, Dm), F32), sds((S, W), MXU_DTYPE), sds((S, W), MXU_DTYPE), sds((S, W), MXU_DTYPE),
                   sds((S, Dm), MXU_DTYPE), sds((S, Dm), MXU_DTYPE), sds((S, W), F32), sds((HEADS, S), F32),
                   sds((S, Dm), MXU_DTYPE), sds((S, Dm), MXU_DTYPE), sds((S, W), MXU_DTYPE), sds((S, Dm), MXU_DTYPE),
                   sds((S, W), MXU_DTYPE), sds((S, Dm), MXU_DTYPE), sds((S, W), MXU_DTYPE), sds((S, W), MXU_DTYPE),
                   sds((1, W), F32), sds((1, Dm), F32), sds((1, Dm), F32), sds((1, 1), F32)],
        compiler_params=_params(("arbitrary",), VMEM_BIG),
    )(o, za, ys, zb, ga, gb, x, tgt, gate, b_glu, g_final, w_glu, w_up_a, w_up_b, w_out, hsel)


def _matmul_tn(name, a, b, ts):
    S, M = a.shape
    N = b.shape[1]
    tn = min(N, 1024)

    def body(a_ref, b_ref, o_ref):
        @pl.when(pl.program_id(1) == 0)
        def _():
            o_ref[...] = jnp.zeros_like(o_ref)

        o_ref[...] += _mm_tn(a_ref[...], b_ref[...])

    return pl.pallas_call(
        body, name=name, grid=(N // tn, S // ts),
        in_specs=[pl.BlockSpec((ts, M), lambda j, i: (i, 0)), pl.BlockSpec((ts, tn), lambda j, i: (i, j))],
        out_specs=pl.BlockSpec((M, tn), lambda j, i: (0, j)),
        out_shape=jax.ShapeDtypeStruct((M, N), F32),
        compiler_params=_params(("parallel", "arbitrary"), VMEM_MID),
    )(a, b)


def _proj_bwd(dq, dk, dv, dza, du, dzb, dga, dgb, dfl, x, dx2, shift, scale, g_norm, w_main, w_ft, ts):
    S = x.shape[0]

    def body(dq_ref, dk_ref, dv_ref, dza_ref, du_ref, dzb_ref, dga_ref, dgb_ref, dfl_ref, x_ref, dx2_ref,
             sc_ref, gn_ref, w_ref, wft_ref, gx_ref, dsh_ref, dsc_ref, dgn_ref):
        @pl.when(pl.program_id(0) == 0)
        def _():
            dsh_ref[...] = jnp.zeros_like(dsh_ref)
            dsc_ref[...] = jnp.zeros_like(dsc_ref)
            dgn_ref[...] = jnp.zeros_like(dgn_ref)

        def seg(ref, off, n):
            return _mm_nt(ref[...], w_ref[:, off:off + n])

        dh = (seg(dq_ref, M_Q, WIDTH) + seg(dk_ref, M_K, WIDTH) + seg(dv_ref, M_V, WIDTH)
              + seg(dza_ref, M_ZA, WIDTH) + seg(du_ref, M_U, WIDTH) + seg(dzb_ref, M_ZB, WIDTH)
              + seg(dga_ref, M_GA, D_MODEL) + seg(dgb_ref, M_GB, D_MODEL)
              + _mm32(dfl_ref[...], wft_ref[...]))
        xv = x_ref[...]
        r = lax.rsqrt(jnp.mean(xv * xv, axis=-1, keepdims=True) + EPS)
        xn = xv * r
        gn = gn_ref[...]
        s1 = 1.0 + sc_ref[...]
        dsh_ref[...] += jnp.sum(dh, axis=0, keepdims=True)
        dhx = dh * xn
        dsc_ref[...] += jnp.sum(dhx, axis=0, keepdims=True) * gn
        dgn_ref[...] += jnp.sum(dhx, axis=0, keepdims=True) * s1
        dxn = dh * (gn * s1)
        gx_ref[...] = dx2_ref[...] + r * (dxn - xn * jnp.mean(dxn * xn, axis=-1, keepdims=True))

    row = lambda n: pl.BlockSpec((ts, n), lambda i: (i, 0))
    full = lambda a: pl.BlockSpec(a.shape, lambda i: (0,) * a.ndim)
    vec = pl.BlockSpec((1, D_MODEL), lambda i: (0, 0))
    W, Dm = WIDTH, D_MODEL
    del shift
    return pl.pallas_call(
        body, name="proj_bwd", grid=(S // ts,),
        in_specs=[row(W)] * 6 + [row(Dm)] * 2 + [row(HEADS), row(Dm), row(Dm),
                                                 full(scale), full(g_norm), full(w_main), full(w_ft)],
        out_specs=[row(Dm), vec, vec, vec],
        out_shape=[jax.ShapeDtypeStruct((S, Dm), F32)] + [jax.ShapeDtypeStruct((1, Dm), F32)] * 3,
        compiler_params=_params(("arbitrary",), VMEM_BIG),
    )(dq, dk, dv, dza, du, dzb, dga, dgb, dfl, x, dx2, scale, g_norm, w_main, w_ft)


def _adamw(name, planes, w, m, v, tr):
    n, R, C = planes.shape
    bc1 = 1.0 - ADAM_B1 ** ADAM_STEP
    bc2 = 1.0 - ADAM_B2 ** ADAM_STEP

    def body(p_ref, w_ref, m_ref, v_ref, g_ref, d_ref, nm_ref, nv_ref):
        g = p_ref[0].astype(F32)
        for i in range(1, n):
            g = g + p_ref[i].astype(F32)
        g_ref[...] = g
        nm = ADAM_B1 * m_ref[...] + (1.0 - ADAM_B1) * g
        nv = ADAM_B2 * v_ref[...] + (1.0 - ADAM_B2) * (g * g)
        nm_ref[...] = nm
        nv_ref[...] = nv
        d_ref[...] = -ADAM_LR * ((nm / bc1) / (jnp.sqrt(nv / bc2) + ADAM_EPS) + ADAM_WD * w_ref[...])

    blk = pl.BlockSpec((tr, C), lambda i: (i, 0))
    return pl.pallas_call(
        body, name=name, grid=(R // tr,),
        in_specs=[pl.BlockSpec((n, tr, C), lambda i: (0, i, 0)), blk, blk, blk],
        out_specs=[blk] * 4, out_shape=[jax.ShapeDtypeStruct((R, C), F32)] * 4,
        compiler_params=_params(("parallel",), VMEM_MID),
    )(planes, w, m, v)


def _wada_grad(c_all, dmod_cols):
    def body(c_ref, d_ref, o_ref):
        o_ref[0] = _mm32_tn(c_ref[...], d_ref[...])

    return pl.pallas_call(
        body, name="wada_grad",
        out_shape=jax.ShapeDtypeStruct((1, c_all.shape[1], dmod_cols.shape[1]), F32),
        in_specs=[VMEM, VMEM], out_specs=VMEM,
    )(c_all, dmod_cols)


SMALL_ORDER = ("b_ada", "g_norm", "b_f", "a_re", "a_im", "log_dt", "b_re", "b_im", "c_re", "c_im",
               "d_skip", "b_glu", "g_final")
BIG_ORDER = ("w_ada", "w_in", "w_glu", "w_up_a", "w_up_b", "w_out")
ALL_ORDER = ("w_ada", "b_ada", "g_norm", "w_in", "b_f", "a_re", "a_im", "log_dt", "b_re", "b_im", "c_re", "c_im",
             "d_skip", "w_glu", "b_glu", "w_up_a", "w_up_b", "w_out", "g_final")


def _pack_small(parts, rows):
    flat = jnp.concatenate([p.reshape(-1).astype(F32) for p in parts])
    return jnp.pad(flat, (0, rows * 128 - flat.shape[0])).reshape(rows, 128)


def kernel(x, c, w_ada, b_ada, g_norm, w_in, b_f, a_re, a_im, log_dt, b_re, b_im, c_re, c_im, d_skip, w_glu, b_glu, w_up_a, w_up_b, w_out, g_final, loss_target, m_w_ada, m_b_ada, m_g_norm, m_w_in, m_b_f, m_a_re, m_a_im, m_log_dt, m_b_re, m_b_im, m_c_re, m_c_im, m_d_skip, m_w_glu, m_b_glu, m_w_up_a, m_w_up_b, m_w_out, m_g_final, v_w_ada, v_b_ada, v_g_norm, v_w_in, v_b_f, v_a_re, v_a_im, v_log_dt, v_b_re, v_b_im, v_c_re, v_c_im, v_d_skip, v_w_glu, v_b_glu, v_w_up_a, v_w_up_b, v_w_out, v_g_final):
    weights = dict(w_ada=w_ada, b_ada=b_ada, g_norm=g_norm, w_in=w_in, b_f=b_f, a_re=a_re, a_im=a_im, log_dt=log_dt,
                   b_re=b_re, b_im=b_im, c_re=c_re, c_im=c_im, d_skip=d_skip, w_glu=w_glu, b_glu=b_glu,
                   w_up_a=w_up_a, w_up_b=w_up_b, w_out=w_out, g_final=g_final)
    mom_m = dict(w_ada=m_w_ada, b_ada=m_b_ada, g_norm=m_g_norm, w_in=m_w_in, b_f=m_b_f, a_re=m_a_re, a_im=m_a_im,
                 log_dt=m_log_dt, b_re=m_b_re, b_im=m_b_im, c_re=m_c_re, c_im=m_c_im, d_skip=m_d_skip, w_glu=m_w_glu,
                 b_glu=m_b_glu, w_up_a=m_w_up_a, w_up_b=m_w_up_b, w_out=m_w_out, g_final=m_g_final)
    mom_v = dict(w_ada=v_w_ada, b_ada=v_b_ada, g_norm=v_g_norm, w_in=v_w_in, b_f=v_b_f, a_re=v_a_re, a_im=v_a_im,
                 log_dt=v_log_dt, b_re=v_b_re, b_im=v_b_im, c_re=v_c_re, c_im=v_c_im, d_skip=v_d_skip, w_glu=v_w_glu,
                 b_glu=v_b_glu, w_up_a=v_w_up_a, w_up_b=v_w_up_b, w_out=v_w_out, g_final=v_g_final)
    xs = x[0]
    tgt = loss_target[0]
    S = xs.shape[0]
    ts = min(256, S)
    ta = min(512, S)
    tw = min(2048, S)
    nch = S // CHUNK
    n_steps = max(1, int(math.ceil(math.log2(nch))))
    me = _my_index()

    shards = [w.astype(MXU_DTYPE) for w in (w_in[0], w_glu[0], w_up_a[0], w_up_b[0], w_out[0])]
    mod8, c_all, gathered = _comm_in(c, w_ada[0], b_ada.reshape(N_DEV, -1), shards)
    mod = mod8.reshape(1, 3 * D_MODEL)
    shift, scale, gate = mod[:, :D_MODEL], mod[:, D_MODEL:2 * D_MODEL], mod[:, 2 * D_MODEL:]
    w_in_full = gathered[0].transpose(1, 0, 2).reshape(D_MODEL, PROJ_WIDTH)
    w_main = jnp.concatenate([w_in_full[:, :OFF_F], w_in_full[:, OFF_F + HEADS:]], axis=1)
    w_f = w_in_full[:, OFF_F:OFF_F + HEADS].astype(F32)
    w_ft = w_f.T
    w_glu_full = gathered[1].reshape(WIDTH, WIDTH)
    w_up_a_full = gathered[2].transpose(1, 0, 2).reshape(WIDTH, D_MODEL)
    w_up_b_full = gathered[3].transpose(1, 0, 2).reshape(WIDTH, D_MODEL)
    w_out_full = gathered[4].reshape(D_MODEL, D_MODEL)

    q, k, v, za, u, zb, ga, gb, flc, hb = _proj_fwd(xs, shift, scale, g_norm, w_main, w_f, ts)
    fcol = _fgate_fwd(flc, b_f, ta)
    nb = S // ta
    rows4 = lambda r: r.reshape(PAIRS, 2, nb, ta).transpose(0, 2, 1, 3)
    qh, kh, vt = _attn_prep(q, k, v, fcol, ta)
    o, lse4 = _attn_fwd(qh, kh, vt, ta)

    s5_params = (a_re[0], a_im[0], log_dt[0], b_re[0], b_im[0], c_re[0], c_im[0], d_skip[0])
    (trow, camat, bzmat, al), mats_vjp = jax.vjp(_s5_mats, *s5_params)
    del al
    pw_f, pw_b = _s5_scan_powers(a_re[0], a_im[0], log_dt[0], n_steps)
    ys, xprev, uc = _s5_fwd(u, trow, camat, bzmat, pw_f)

    hsel = (np.arange(WIDTH)[None, :] // 64 == np.arange(HEADS)[:, None]).astype(np.float32)
    (dx2, do, dza, dzb, dga, dgb, dys, dl_row, merged, dmo, ya, dua, yb, dub, yg, dgl,
     db_glu, dgate, dg_final, loss_part) = _mix(o, za, ys, zb, ga, gb, xs, tgt, gate, b_glu, g_final.reshape(1, -1),
                                                w_glu_full, w_up_a_full, w_up_b_full, w_out_full, jnp.asarray(hsel), ts)

    gw_out = _matmul_tn("dw_out", merged, dmo, tw)
    gw_up_a = _matmul_tn("dw_up_a", ya, dua, tw)
    gw_up_b = _matmul_tn("dw_up_b", yb, dub, tw)
    gw_glu = _matmul_tn("dw_glu", yg, dgl, tw)

    du, d_trow, d_camat, d_bzmat, dal2 = _s5_bwd(uc, dys, xprev, trow, camat, bzmat, pw_b)
    d_al = jnp.concatenate([dal2[:, 0, :STATE] + dal2[:, 0, STATE:], dal2[:, 1, STATE:] - dal2[:, 1, :STATE]], axis=-1)
    gs5 = mats_vjp((d_trow, d_camat, d_bzmat, d_al))

    dl4 = rows4(dl_row)
    dq, dk, dv, dfq4, dfk4 = _attn_bwd(qh, do, kh, v, lse4, dl4, ta)
    d_fcol = dfq4.transpose(0, 2, 1, 3).reshape(HEADS, S).T - dfk4.transpose(1, 0, 2).reshape(S, HEADS)
    dfl, db_f = _fgate_bwd(d_fcol, flc, b_f, ta)

    grad_x, dshift, dscale, dg_norm = _proj_bwd(dq, dk, dv, dza, du, dzb, dga, dgb, dfl, xs, dx2,
                                                shift, scale, g_norm, w_main, w_ft, ts)
    segs = [("dw_q", dq), ("dw_k", dk), ("dw_v", dv), ("dw_f", dfl), ("dw_za", dza), ("dw_u", du), ("dw_zb", dzb),
            ("dw_ga", dga), ("dw_gb", dgb)]
    gw_in = jnp.concatenate([_matmul_tn(nm, hb, d, tw) for nm, d in segs], axis=1)

    planes = [gw_in.reshape(D_MODEL, N_DEV, -1).transpose(1, 0, 2),
              gw_glu.reshape(N_DEV, -1, WIDTH),
              gw_up_a.reshape(WIDTH, N_DEV, -1).transpose(1, 0, 2),
              gw_up_b.reshape(WIDTH, N_DEV, -1).transpose(1, 0, 2),
              gw_out.reshape(N_DEV, -1, D_MODEL)]
    planes = [p.astype(MXU_DTYPE) for p in planes]
    dmod = jnp.concatenate([dshift, dscale, dgate], axis=1)
    small_parts = [dmod, dg_norm, db_f, gs5[0], gs5[1], gs5[2], gs5[3], gs5[4], gs5[5], gs5[6], gs5[7],
                   db_glu, dg_final, loss_part]
    n_small = sum(int(np.prod(p.shape)) for p in small_parts)
    rows = -(-n_small // (8 * 128)) * 8
    small = _pack_small(small_parts, rows)
    from_sib = _comm_pair(planes)
    core = lax.axis_index("c")
    chip_planes = []
    for name, p, s in zip(("w_in", "w_glu", "w_up_a", "w_up_b", "w_out"), planes, from_sib):
        own = lax.dynamic_index_in_dim(p.reshape((N_CHIP, 2) + p.shape[1:]), core, axis=1, keepdims=False)
        tr = 256 if own.shape[1] % 256 == 0 else own.shape[1]
        chip_planes.append(_pair_sum("pair_sum_" + name, own, s, tr))
    recv, small_all = _comm_out(chip_planes, small)

    grads, deltas, new_m, new_v = {}, {}, {}, {}

    def put(name, res, shape):
        grads[name], deltas[name], new_m[name], new_v[name] = [r.reshape(shape) for r in res]

    names = ("w_in", "w_glu", "w_up_a", "w_up_b", "w_out")
    for name, pr in zip(names, recv):
        w2 = weights[name][0]
        tr = 256 if w2.shape[0] % 256 == 0 else w2.shape[0]
        put(name, _adamw("adamw_" + name, pr, w2, mom_m[name][0], mom_v[name][0], tr), weights[name].shape)
    cols = w_ada.shape[2]
    dmod_all = small_all[:, :24, :].reshape(N_DEV, 3 * D_MODEL)
    dmod_cols = lax.dynamic_slice_in_dim(dmod_all, me * cols, cols, axis=1)
    g_wada = _wada_grad(c_all, dmod_cols)
    put("w_ada", _adamw("adamw_w_ada", g_wada, w_ada[0], m_w_ada[0], v_w_ada[0], 256), w_ada.shape)
    pack = lambda d: _pack_small([d[n] for n in SMALL_ORDER] + [jnp.zeros((1,), F32)], rows)
    res_small = _adamw("adamw_small", small_all, pack(weights), pack(mom_m), pack(mom_v), rows)
    flat = [r.reshape(-1) for r in res_small]
    off = 0
    for name in SMALL_ORDER:
        shape = weights[name].shape
        size = int(np.prod(shape))
        put(name, [f[off:off + size] for f in flat], shape)
        off += size
    loss = flat[0][off]

    return (loss, grad_x[None], *[grads[n] for n in ALL_ORDER], *[deltas[n] for n in ALL_ORDER],
            *[new_m[n] for n in ALL_ORDER], *[new_v[n] for n in ALL_ORDER])
```

```python
import functools
import math

import jax
import jax.numpy as jnp
import numpy as np
from jax import lax
from jax.experimental import pallas as pl
from jax.experimental.pallas import tpu as pltpu

F32 = jnp.float32
MXU_DTYPE = jnp.bfloat16
HI = lax.Precision.HIGHEST

N_DEV = 8
D_MODEL = 1024
WIDTH = 512
HEADS = 8
PAIRS = HEADS // 2
GROUPS = 32
STATE = 64
CG = 16
CHUNK = 16
EPS = 1e-6
NEG = float(np.finfo(np.float32).min)

ADAM_LR = 0.001
ADAM_B1 = 0.9
ADAM_B2 = 0.999
ADAM_EPS = 1e-08
ADAM_WD = 0.01
ADAM_STEP = 10

VMEM_BIG = 56 * 1024 * 1024
VMEM_MID = 40 * 1024 * 1024

OFF_F = 3 * WIDTH
PROJ_WIDTH = 5128
M_Q, M_K, M_V, M_ZA, M_U, M_ZB, M_GA, M_GB = 0, 512, 1024, 1536, 2048, 2560, 3072, 4096


def _mm(a, b):
    return jnp.dot(a.astype(MXU_DTYPE), b.astype(MXU_DTYPE), preferred_element_type=F32)


def _mm_nt(a, b):
    return lax.dot_general(a.astype(MXU_DTYPE), b.astype(MXU_DTYPE), (((1,), (1,)), ((), ())),
                           preferred_element_type=F32)


def _mm_tn(a, b):
    return lax.dot_general(a.astype(MXU_DTYPE), b.astype(MXU_DTYPE), (((0,), (0,)), ((), ())),
                           preferred_element_type=F32)


def _mm32(a, b):
    return jnp.dot(a, b, precision=HI, preferred_element_type=F32)


def _mm32_nt(a, b):
    return lax.dot_general(a, b, (((1,), (1,)), ((), ())), precision=HI, preferred_element_type=F32)


def _mm32_tn(a, b):
    return lax.dot_general(a, b, (((0,), (0,)), ((), ())), precision=HI, preferred_element_type=F32)


def _sigmoid(x):
    return 1.0 / (1.0 + jnp.exp(-x))


def _params(sem=None, vmem=None):
    kw = {}
    if sem is not None:
        kw["dimension_semantics"] = sem
    if vmem is not None:
        kw["vmem_limit_bytes"] = vmem
    return pltpu.CompilerParams(**kw)


def _my_index():
    return 4 * lax.axis_index("x") + 2 * lax.axis_index("y") + lax.axis_index("c")


def _dev(p):
    return (p // 4, (p // 2) % 2, p % 2)


ANY = pl.BlockSpec(memory_space=pl.ANY)
VMEM = pl.BlockSpec(memory_space=pltpu.VMEM)
MESH = pl.DeviceIdType.MESH


def _comm_in(c, w_ada, b_ada8, shards):
    n = len(shards)
    cols = w_ada.shape[1]

    def body(c_ref, wada_ref, bada_ref, *rest):
        srcs = rest[:n]
        mod_ref, call_ref = rest[n], rest[n + 1]
        dsts = rest[n + 2:2 * n + 2]
        modp, wsend, wrecv, wloc, csend, crecv, msend, mrecv = rest[2 * n + 2:]
        me = _my_index()

        x, y, cc = lax.axis_index("x"), lax.axis_index("y"), lax.axis_index("c")
        sib = (x, y, 1 - cc)
        chips = [(1 - x, y), (x, 1 - y), (1 - x, 1 - y)]

        def wcopy(a, k, block, to, src=None):
            ref = dsts[a].at[4 * block[0] + 2 * block[1] + block[2]]
            return pltpu.make_async_remote_copy(src_ref=ref if src is None else src, dst_ref=ref,
                                                send_sem=wsend.at[a, k], recv_sem=wrecv.at[a, k],
                                                device_id=to, device_id_type=MESH)

        def ccopy(src_dev, d, to):
            return pltpu.make_async_remote_copy(src_ref=c_ref, dst_ref=call_ref.at[pl.ds(src_dev, 1)],
                                                send_sem=csend.at[d], recv_sem=crecv.at[src_dev],
                                                device_id=_dev(to), device_id_type=MESH)

        def mcopy(src_dev, d, to):
            return pltpu.make_async_remote_copy(src_ref=modp.at[pl.ds(to, 1)], dst_ref=mod_ref.at[pl.ds(src_dev, 1)],
                                                send_sem=msend.at[d], recv_sem=mrecv.at[src_dev],
                                                device_id=_dev(to), device_id_type=MESH)

        local = [pltpu.make_async_copy(srcs[a], dsts[a].at[me], wloc.at[a]) for a in range(n)]
        for cp in local:
            cp.start()
        peers = [(me + d) % N_DEV for d in range(1, N_DEV)]
        first = []
        for a in range(n):
            first.append(wcopy(a, 0, (x, y, cc), sib, src=srcs[a]))
            first += [wcopy(a, 1 + j, (x, y, cc), (*chip, cc), src=srcs[a]) for j, chip in enumerate(chips)]
        for cp in first:
            cp.start()
        call_ref[pl.ds(me, 1), :] = c_ref[...]
        for d, p in enumerate(peers):
            ccopy(me, d, p).start()
        for d, p in enumerate(peers):
            ccopy(p, d, p).wait_recv()
        modp[...] = _mm32(call_ref[...], wada_ref[...]) + bada_ref[pl.ds(me, 1), :]
        mod_ref[pl.ds(me, 1), :] = modp[pl.ds(me, 1), :]
        for d, p in enumerate(peers):
            mcopy(me, d, p).start()
        for d, p in enumerate(peers):
            mcopy(p, d, p).wait_recv()
        passed = []
        for j, chip in enumerate(chips):
            for a in range(n):
                wcopy(a, 1 + j, (*chip, cc), (x, y, cc)).wait_recv()
                cp = wcopy(a, 4 + j, (*chip, cc), sib)
                cp.start()
                passed.append(cp)
        for a in range(n):
            wcopy(a, 0, sib, (x, y, cc)).wait_recv()
            for j, chip in enumerate(chips):
                wcopy(a, 4 + j, (*chip, 1 - cc), (x, y, cc)).wait_recv()
        for cp in first + passed:
            cp.wait_send()
        for d, p in enumerate(peers):
            ccopy(me, d, p).wait_send()
            mcopy(me, d, p).wait_send()
        for cp in local:
            cp.wait()

    out_shape = ([jax.ShapeDtypeStruct((N_DEV, cols), F32), jax.ShapeDtypeStruct((N_DEV, D_MODEL), F32)]
                 + [jax.ShapeDtypeStruct((N_DEV,) + s.shape, s.dtype) for s in shards])
    res = pl.pallas_call(
        body, name="comm_in", out_shape=out_shape,
        in_specs=[VMEM, VMEM, VMEM] + [ANY] * n,
        out_specs=[VMEM, VMEM] + [ANY] * n,
        scratch_shapes=[pltpu.VMEM((N_DEV, cols), F32),
                        pltpu.SemaphoreType.DMA((n, N_DEV)), pltpu.SemaphoreType.DMA((n, N_DEV)),
                        pltpu.SemaphoreType.DMA((n,)),
                        pltpu.SemaphoreType.DMA((N_DEV,)), pltpu.SemaphoreType.DMA((N_DEV,)),
                        pltpu.SemaphoreType.DMA((N_DEV,)), pltpu.SemaphoreType.DMA((N_DEV,))],
        compiler_params=_params(vmem=VMEM_MID),
    )(c, w_ada, b_ada8, *shards)
    return res[0], res[1], list(res[2:])


N_CHIP = 4


def _comm_pair(planes):
    n = len(planes)

    def body(*rest):
        srcs, dsts = rest[:n], rest[n:2 * n]
        send, recv = rest[2 * n:]
        x, y, cc = lax.axis_index("x"), lax.axis_index("y"), lax.axis_index("c")
        copies = [pltpu.make_async_remote_copy(src_ref=srcs[a].at[2 * ch + 1 - cc], dst_ref=dsts[a].at[ch],
                                               send_sem=send.at[a, ch], recv_sem=recv.at[a, ch],
                                               device_id=(x, y, 1 - cc), device_id_type=MESH)
                  for a in range(n) for ch in range(N_CHIP)]
        for cp in copies:
            cp.start()
        for cp in copies:
            cp.wait()

    out_shape = [jax.ShapeDtypeStruct((N_CHIP,) + p.shape[1:], p.dtype) for p in planes]
    return pl.pallas_call(
        body, name="comm_pair", out_shape=out_shape, in_specs=[ANY] * n, out_specs=[ANY] * n,
        scratch_shapes=[pltpu.SemaphoreType.DMA((n, N_CHIP)), pltpu.SemaphoreType.DMA((n, N_CHIP))],
    )(*planes)


def _pair_sum(name, planes, from_sib, core, tr):
    _, R, C = from_sib.shape

    def body(core_ref, a_ref, b_ref, o_ref):
        del core_ref
        o_ref[...] = (a_ref[...].astype(F32) + b_ref[...].astype(F32)).astype(o_ref.dtype)

    blk = pl.BlockSpec((None, tr, C), lambda i, j, c: (i, j, 0))
    grid_spec = pltpu.PrefetchScalarGridSpec(
        num_scalar_prefetch=1, grid=(N_CHIP, R // tr),
        in_specs=[pl.BlockSpec((None, tr, C), lambda i, j, c: (2 * i + c[0], j, 0)), blk], out_specs=blk)
    return pl.pallas_call(
        body, name=name, grid_spec=grid_spec, out_shape=jax.ShapeDtypeStruct(from_sib.shape, from_sib.dtype),
        compiler_params=_params(("parallel", "parallel"), VMEM_MID),
    )(core.reshape(1).astype(jnp.int32), planes, from_sib)


def _comm_out(chip_planes, small):
    n = len(chip_planes)

    def body(*rest):
        srcs = rest[:n]
        small_ref = rest[n]
        dsts = rest[n + 1:2 * n + 1]
        sall_ref = rest[2 * n + 1]
        wsend, wrecv, wloc, ssend, srecv, sloc = rest[2 * n + 2:]
        me = _my_index()
        x, y, cc = lax.axis_index("x"), lax.axis_index("y"), lax.axis_index("c")
        mine = 2 * x + y
        chips = [(1 - x, y), (x, 1 - y), (1 - x, 1 - y)]

        def wcopy(a, j, sending):
            chip = chips[j]
            there = 2 * chip[0] + chip[1]
            return pltpu.make_async_remote_copy(src_ref=srcs[a].at[there], dst_ref=dsts[a].at[mine if sending else there],
                                                send_sem=wsend.at[a, j], recv_sem=wrecv.at[a, j],
                                                device_id=(*chip, cc), device_id_type=MESH)

        def scopy(src_dev, d, to):
            return pltpu.make_async_remote_copy(src_ref=small_ref, dst_ref=sall_ref.at[src_dev],
                                                send_sem=ssend.at[d], recv_sem=srecv.at[src_dev],
                                                device_id=_dev(to), device_id_type=MESH)

        local = [pltpu.make_async_copy(srcs[a].at[mine], dsts[a].at[mine], wloc.at[a]) for a in range(n)]
        local.append(pltpu.make_async_copy(small_ref, sall_ref.at[me], sloc))
        for cp in local:
            cp.start()
        peers = [(me + d) % N_DEV for d in range(1, N_DEV)]
        for d, p in enumerate(peers):
            scopy(me, d, p).start()
        for j in range(len(chips)):
            for a in range(n):
                wcopy(a, j, True).start()
        for d, p in enumerate(peers):
            scopy(p, d, p).wait_recv()
        for j in range(len(chips)):
            for a in range(n):
                wcopy(a, j, False).wait_recv()
        for d, p in enumerate(peers):
            scopy(me, d, p).wait_send()
        for j in range(len(chips)):
            for a in range(n):
                wcopy(a, j, True).wait_send()
        for cp in local:
            cp.wait()

    out_shape = ([jax.ShapeDtypeStruct(p.shape, p.dtype) for p in chip_planes]
                 + [jax.ShapeDtypeStruct((N_DEV,) + small.shape, small.dtype)])
    res = pl.pallas_call(
        body, name="comm_out", out_shape=out_shape,
        in_specs=[ANY] * (n + 1), out_specs=[ANY] * (n + 1),
        scratch_shapes=[pltpu.SemaphoreType.DMA((n, N_CHIP)), pltpu.SemaphoreType.DMA((n, N_CHIP)),
                        pltpu.SemaphoreType.DMA((n,)),
                        pltpu.SemaphoreType.DMA((N_DEV,)), pltpu.SemaphoreType.DMA((N_DEV,)),
                        pltpu.SemaphoreType.DMA(())],
    )(*chip_planes, small)
    return list(res[:n]), res[n]


def _proj_fwd(x, shift, scale, g_norm, w_main, w_f, ts):
    S = x.shape[0]

    def body(x_ref, sh_ref, sc_ref, gn_ref, w_ref, wf_ref,
             q_ref, k_ref, v_ref, za_ref, u_ref, zb_ref, ga_ref, gb_ref, flc_ref, h_ref):
        xv = x_ref[...]
        r = lax.rsqrt(jnp.mean(xv * xv, axis=-1, keepdims=True) + EPS)
        h = (xv * r) * gn_ref[...] * (1.0 + sc_ref[...]) + sh_ref[...]
        hb = h.astype(MXU_DTYPE)
        h_ref[...] = hb

        def seg(off, n):
            return jnp.dot(hb, w_ref[:, off:off + n], preferred_element_type=F32)

        q_ref[...] = (seg(M_Q, WIDTH) * 0.125).astype(q_ref.dtype)
        k_ref[...] = seg(M_K, WIDTH).astype(k_ref.dtype)
        v_ref[...] = seg(M_V, WIDTH).astype(v_ref.dtype)
        za_ref[...] = seg(M_ZA, WIDTH)
        u_ref[...] = seg(M_U, WIDTH)
        zb_ref[...] = seg(M_ZB, WIDTH)
        ga_ref[...] = seg(M_GA, D_MODEL)
        gb_ref[...] = seg(M_GB, D_MODEL)
        flc_ref[...] = _mm32(h, wf_ref[...])

    row = lambda n: pl.BlockSpec((ts, n), lambda i: (i, 0))
    full = lambda a: pl.BlockSpec(a.shape, lambda i: (0,) * a.ndim)
    sds = jax.ShapeDtypeStruct
    return pl.pallas_call(
        body, name="proj_fwd", grid=(S // ts,),
        in_specs=[row(D_MODEL), full(shift), full(scale), full(g_norm), full(w_main), full(w_f)],
        out_specs=[row(WIDTH)] * 6 + [row(D_MODEL)] * 2 + [row(HEADS), row(D_MODEL)],
        out_shape=[sds((S, WIDTH), MXU_DTYPE)] * 3 + [sds((S, WIDTH), F32)] * 3 + [sds((S, D_MODEL), F32)] * 2
                  + [sds((S, HEADS), F32), sds((S, D_MODEL), MXU_DTYPE)],
        compiler_params=_params(("parallel",), VMEM_BIG),
    )(x, shift, scale, g_norm, w_main, w_f)


def _log_sigmoid(z):
    return jnp.minimum(z, 0.0) - jnp.log(1.0 + jnp.exp(-jnp.abs(z)))


def _fgate_bwd(dfc, flc, bf_row, ts):
    S = flc.shape[0]
    n = S // ts

    def body(df_ref, flc_ref, bfr_ref, dfl_ref, dbf_ref, carry):
        @pl.when(pl.program_id(0) == 0)
        def _():
            carry[...] = jnp.zeros_like(carry)
            dbf_ref[...] = jnp.zeros_like(dbf_ref)

        ri = lax.broadcasted_iota(jnp.int32, (ts, ts), 0)
        ci = lax.broadcasted_iota(jnp.int32, (ts, ts), 1)
        upper = (ci >= ri).astype(F32)
        rc = _mm32(upper, df_ref[...]) + carry[...]
        carry[...] = rc[0:1, :]
        z = flc_ref[...] + bfr_ref[...]
        dfl = rc * _sigmoid(-z)
        dfl_ref[...] = dfl
        dbf_ref[...] += jnp.sum(dfl, axis=0, keepdims=True)

    col = pl.BlockSpec((ts, HEADS), lambda i: (n - 1 - i, 0))
    one = pl.BlockSpec((1, HEADS), lambda i: (0, 0))
    return pl.pallas_call(
        body, name="fgate_bwd", grid=(n,),
        in_specs=[col, col, one], out_specs=[col, one],
        out_shape=[jax.ShapeDtypeStruct((S, HEADS), F32), jax.ShapeDtypeStruct((1, HEADS), F32)],
        scratch_shapes=[pltpu.VMEM((1, HEADS), F32)],
        compiler_params=_params(("arbitrary",)),
    )(dfc, flc, bf_row)


N_EXTRA = 3


def _attn_prep(q, k, v, flc, bf_row, t):
    S = q.shape[0]
    nb = S // t

    def body(q_ref, k_ref, v_ref, flc_ref, bfr_ref, qh_ref, kh_ref, vt_ref, carry):
        @pl.when(pl.program_id(0) == 0)
        def _():
            carry[...] = jnp.zeros_like(carry)

        ri = lax.broadcasted_iota(jnp.int32, (t, t), 0)
        ci = lax.broadcasted_iota(jnp.int32, (t, t), 1)
        f = _mm32((ci <= ri).astype(F32), _log_sigmoid(flc_ref[...] + bfr_ref[...])) + carry[...]
        carry[...] = f[t - 1:t, :]
        lane = lax.broadcasted_iota(jnp.int32, (t, 128), 1)
        for p in range(PAIRS):
            qp = q_ref[:, p * 128:(p + 1) * 128]
            kp = k_ref[:, p * 128:(p + 1) * 128]
            vt_ref[p, 0] = v_ref[:, p * 128:(p + 1) * 128].T
            for h in range(2):
                own = (lane < 64) if h == 0 else (lane >= 64)
                base = 64 if h == 0 else 0
                fh = f[:, 2 * p + h:2 * p + h + 1]
                parts = []
                rest = fh
                for _ in range(N_EXTRA):
                    part = rest.astype(qh_ref.dtype)
                    parts.append(part)
                    rest = rest - part.astype(F32)
                one = jnp.ones((t, 1), qh_ref.dtype)
                eq = jnp.zeros((t, 128), qh_ref.dtype)
                ek = jnp.zeros((t, 128), qh_ref.dtype)
                for j in range(N_EXTRA):
                    eq = jnp.where(lane == base + j, parts[j], eq)
                    eq = jnp.where(lane == base + N_EXTRA + j, one, eq)
                    ek = jnp.where(lane == base + j, one, ek)
                    ek = jnp.where(lane == base + N_EXTRA + j, -parts[j], ek)
                qh_ref[2 * p + h] = jnp.where(own, qp, eq)
                kh_ref[2 * p + h] = jnp.where(own, kp, ek)

    row = pl.BlockSpec((t, WIDTH), lambda i: (i, 0))
    heads = pl.BlockSpec((HEADS, t, 128), lambda i: (0, i, 0))
    return pl.pallas_call(
        body, name="attn_prep", grid=(nb,),
        in_specs=[row, row, row, pl.BlockSpec((t, HEADS), lambda i: (i, 0)), pl.BlockSpec((1, HEADS), lambda i: (0, 0))],
        out_specs=[heads, heads, pl.BlockSpec((PAIRS, 1, 128, t), lambda i: (0, i, 0, 0))],
        out_shape=[jax.ShapeDtypeStruct((HEADS, S, 128), q.dtype), jax.ShapeDtypeStruct((HEADS, S, 128), k.dtype),
                   jax.ShapeDtypeStruct((PAIRS, nb, 128, t), v.dtype)],
        scratch_shapes=[pltpu.VMEM((1, HEADS), F32)],
        compiler_params=_params(("arbitrary",), VMEM_MID),
    )(q, k, v, flc, bf_row)


def _attn_fwd(qh, kh, vt, t):
    S = qh.shape[1]
    nb = S // t

    def body(q_ref, k_ref, vt_ref, o_ref, lse_ref, acc_s):
        qi = pl.program_id(1)
        acc_s[...] = jnp.zeros_like(acc_s)

        def step(ki, nblk, masked, carry):
            m_old, l_old = carry[:2], carry[2:]
            ks = pl.multiple_of(ki * t, t)
            rows = nblk * t
            sts = [_mm_nt(k_ref[h, pl.ds(ks, rows), :], q_ref[h]) for h in range(2)]
            if masked:
                ri = lax.broadcasted_iota(jnp.int32, (t, t), 0)
                ci = lax.broadcasted_iota(jnp.int32, (t, t), 1)
                sts = [jnp.where(ci >= ri, st, NEG) for st in sts]
            m_new = [jnp.maximum(m_old[h], jnp.max(sts[h], axis=0, keepdims=True)) for h in range(2)]
            alpha = [jnp.exp(m_old[h] - m_new[h]) for h in range(2)]
            pts = [jnp.exp(sts[h] - m_new[h]) for h in range(2)]
            l_new = [alpha[h] * l_old[h] + jnp.sum(pts[h], axis=0, keepdims=True) for h in range(2)]
            for h in range(2):
                pv = _mm(vt_ref[ki], pts[h][:t])
                for b in range(1, nblk):
                    pv = pv + _mm(vt_ref[ki + b], pts[h][b * t:(b + 1) * t])
                acc_s[h] = alpha[h] * acc_s[h] + pv
            return (*m_new, *l_new)

        init = (jnp.full((1, t), -jnp.inf, F32),) * 2 + (jnp.zeros((1, t), F32),) * 2
        carry = lax.fori_loop(0, qi // 2, lambda j, c: step(2 * j, 2, False, c), init)
        carry = lax.cond(qi % 2 == 1, lambda c: step(qi - 1, 1, False, c), lambda c: c, carry)
        m0, m1, l0, l1 = step(qi, 1, True, carry)
        first = lax.broadcasted_iota(jnp.int32, (128, t), 0) < 64
        o_ref[...] = jnp.where(first, acc_s[0] / l0, acc_s[1] / l1).T
        lse_ref[...] = jnp.concatenate([m0 + jnp.log(l0), m1 + jnp.log(l1)], axis=0)

    return pl.pallas_call(
        body, name="attn_fwd", grid=(PAIRS, nb),
        in_specs=[pl.BlockSpec((2, t, 128), lambda p, i: (p, i, 0)), pl.BlockSpec((2, S, 128), lambda p, i: (p, 0, 0)),
                  pl.BlockSpec((None, nb, 128, t), lambda p, i: (p, 0, 0, 0))],
        out_specs=[pl.BlockSpec((t, 128), lambda p, i: (i, p)), pl.BlockSpec((None, None, 2, t), lambda p, i: (p, i, 0, 0))],
        out_shape=[jax.ShapeDtypeStruct((S, WIDTH), F32), jax.ShapeDtypeStruct((PAIRS, nb, 2, t), F32)],
        scratch_shapes=[pltpu.VMEM((2, 128, t), F32)],
        compiler_params=_params(("parallel", "parallel"), VMEM_MID),
    )(qh, kh, vt)


def _attn_bwd(qh, do, kh, v, lse4, dl4, t):
    S = qh.shape[1]
    nb = S // t

    def body(q_ref, do_ref, k_ref, v_ref, lse_ref, dl_ref,
             dq_ref, dk_ref, dv_ref, dfq_ref, dfk_ref, kc_s, vh_s, dk_s, dv_s, dfk_s):
        kj = pl.program_id(1)
        lane = lax.broadcasted_iota(jnp.int32, (t, 128), 1)
        is_a = lane < 64

        @pl.when(kj == 0)
        def _():
            dq_ref[...] = jnp.zeros_like(dq_ref)
            dfq_ref[...] = jnp.zeros_like(dfq_ref)

        vp = v_ref[...]
        zero = jnp.zeros_like(vp)
        kc_s[0] = jnp.where(is_a, k_ref[0], zero.astype(kc_s.dtype))
        kc_s[1] = jnp.where(is_a, zero.astype(kc_s.dtype), k_ref[1])
        vh_s[0] = jnp.where(is_a, vp, zero)
        vh_s[1] = jnp.where(is_a, zero, vp)
        dk_s[...] = jnp.zeros_like(dk_s)
        dv_s[...] = jnp.zeros_like(dv_s)
        dfk_s[...] = jnp.zeros_like(dfk_s)

        def step(qi, masked):
            qs = pl.multiple_of(qi * t, t)
            dob = do_ref[pl.ds(qs, t), :]
            lse = lse_ref[qi]
            dl = dl_ref[qi]
            zq = jnp.zeros_like(dob)
            over_keys = []
            for h in range(2):
                sel = is_a if h == 0 else jnp.logical_not(is_a)
                qb = q_ref[h, pl.ds(qs, t), :]
                st = _mm_nt(k_ref[h], qb) - lse[h:h + 1, :]
                if masked:
                    ri = lax.broadcasted_iota(jnp.int32, (t, t), 0)
                    ci = lax.broadcasted_iota(jnp.int32, (t, t), 1)
                    st = jnp.where(ci >= ri, st, NEG)
                pt = jnp.exp(st)
                dv_s[...] += _mm(pt, jnp.where(sel, dob, zq))
                dpt = _mm_nt(vh_s[h], dob)
                dst = pt * (dpt - dl[h:h + 1, :])
                dfk_s[h] += jnp.sum(dst, axis=1, keepdims=True)
                over_keys.append(jnp.sum(dst, axis=0, keepdims=True))
                dk_s[...] += _mm(dst, jnp.where(sel, qb, jnp.zeros_like(qb)))
                dq_ref[pl.ds(qs, t), :] += _mm_tn(dst, kc_s[h])
            dfq_ref[qi] += jnp.concatenate(over_keys, axis=0)

        step(kj, True)

        def loop_body(qi, carry):
            step(qi, False)
            return carry

        lax.fori_loop(kj + 1, nb, loop_body, 0)
        dk_ref[...] = dk_s[...].astype(dk_ref.dtype)
        dv_ref[...] = dv_s[...].astype(dv_ref.dtype)
        dfk_ref[...] = jnp.where(lax.broadcasted_iota(jnp.int32, (t, 2), 1) == 0, dfk_s[0], dfk_s[1])

        @pl.when(kj == nb - 1)
        def _():
            dq_ref[...] = dq_ref[...] * 0.125

    blk = pl.BlockSpec((t, 128), lambda p, j: (j, p))
    res = pl.BlockSpec((S, 128), lambda p, j: (0, p))
    rows4 = pl.BlockSpec((None, nb, 2, t), lambda p, j: (p, 0, 0, 0))
    cols4 = pl.BlockSpec((None, t, 2), lambda p, j: (p, j, 0))
    return pl.pallas_call(
        body, name="attn_bwd", grid=(PAIRS, nb),
        in_specs=[pl.BlockSpec((2, S, 128), lambda p, j: (p, 0, 0)), res,
                  pl.BlockSpec((2, t, 128), lambda p, j: (p, j, 0)), blk, rows4, rows4],
        out_specs=[res, blk, blk, rows4, cols4],
        out_shape=[jax.ShapeDtypeStruct((S, WIDTH), F32), jax.ShapeDtypeStruct((S, WIDTH), MXU_DTYPE),
                   jax.ShapeDtypeStruct((S, WIDTH), MXU_DTYPE), jax.ShapeDtypeStruct((PAIRS, nb, 2, t), F32),
                   jax.ShapeDtypeStruct((PAIRS, S, 2), F32)],
        scratch_shapes=[pltpu.VMEM((2, t, 128), kh.dtype), pltpu.VMEM((2, t, 128), v.dtype),
                        pltpu.VMEM((t, 128), F32), pltpu.VMEM((t, 128), F32), pltpu.VMEM((2, t, 1), F32)],
        compiler_params=_params(("parallel", "arbitrary"), VMEM_MID),
    )(qh, do, kh, v, lse4, dl4)


def _s5_mats(a_re, a_im, log_dt, b_re, b_im, c_re, c_im, d_skip):
    Lc = CHUNK
    dt = jnp.exp(log_dt)[:, None]
    lr, li = a_re * dt, a_im * dt

    def apow(n):
        n = jnp.asarray(n, F32)[None, :, None]
        mag = jnp.exp(n * lr[:, None, :])
        ang = n * li[:, None, :]
        return mag * jnp.cos(ang), mag * jnp.sin(ang)

    ar, ai = apow([1.0])
    ar, ai = ar[:, 0], ai[:, 0]
    den = a_re * a_re + a_im * a_im
    nr, ni = ar - 1.0, ai
    fr = (nr * a_re + ni * a_im) / den
    fi = (ni * a_re - nr * a_im) / den
    bbr = fr[:, :, None] * b_re - fi[:, :, None] * b_im
    bbi = fr[:, :, None] * b_im + fi[:, :, None] * b_re
    steps = np.arange(Lc, dtype=np.float32)
    pr, pi = apow(steps)
    car = c_re[:, None] * pr[:, :, None, :] - c_im[:, None] * pi[:, :, None, :]
    cai = c_re[:, None] * pi[:, :, None, :] + c_im[:, None] * pr[:, :, None, :]
    kern = (jnp.einsum('glcp,gpd->glcd', car, bbr, precision=HI)
            - jnp.einsum('glcp,gpd->glcd', cai, bbi, precision=HI))
    skip = d_skip.reshape(GROUPS, CG)[:, :, None] * jnp.eye(CG, dtype=F32)[None]
    kern = kern.at[:, 0].add(skip)
    trow = kern.transpose(0, 3, 1, 2).reshape(GROUPS, CG, Lc * CG)
    p1r, p1i = apow(steps + 1.0)
    cr = c_re[:, None] * p1r[:, :, None, :] - c_im[:, None] * p1i[:, :, None, :]
    ci = c_re[:, None] * p1i[:, :, None, :] + c_im[:, None] * p1r[:, :, None, :]
    to_rows = lambda m: m.transpose(0, 3, 1, 2).reshape(GROUPS, STATE, Lc * CG)
    camat = jnp.concatenate([to_rows(cr), -to_rows(ci)], axis=1)
    qr, qi = apow(Lc - 1.0 - steps)
    zr = qr[:, :, None, :] * bbr.transpose(0, 2, 1)[:, None] - qi[:, :, None, :] * bbi.transpose(0, 2, 1)[:, None]
    zi = qr[:, :, None, :] * bbi.transpose(0, 2, 1)[:, None] + qi[:, :, None, :] * bbr.transpose(0, 2, 1)[:, None]
    bzmat = jnp.concatenate([zr, zi], axis=-1).reshape(GROUPS, Lc * CG, 2 * STATE)
    lr_, li_ = apow([float(Lc)])
    al = jnp.concatenate([lr_[:, 0], li_[:, 0]], axis=-1)
    return trow, camat, bzmat, al


def _s5_scan_powers(a_re, a_im, log_dt, n_steps):
    dt = jnp.exp(log_dt)[:, None]
    lr, li = a_re * dt, a_im * dt
    n = (CHUNK * 2.0 ** np.arange(n_steps)).astype(np.float32)[None, :, None]
    mag = jnp.exp(n * lr[:, None, :])
    pr, pi = mag * jnp.cos(n * li[:, None, :]), mag * jnp.sin(n * li[:, None, :])
    fwd = jnp.stack([jnp.concatenate([pr, pr], -1), jnp.concatenate([-pi, pi], -1)], axis=2)
    bwd = jnp.stack([jnp.concatenate([pr, pr], -1), jnp.concatenate([pi, -pi], -1)], axis=2)
    return fwd, bwd


def _shift_rows(x, sh, down):
    n = x.shape[0]
    ri = lax.broadcasted_iota(jnp.int32, x.shape, 0)
    if down:
        return jnp.where(ri >= sh, pltpu.roll(x, sh, 0), 0.0)
    return jnp.where(ri < n - sh, pltpu.roll(x, n - sh, 0), 0.0)


GPB = 128 // CG


def _lane_transpose(arrs):
    lane = lax.broadcasted_iota(jnp.int32, arrs[0].shape, 1)
    arrs = list(arrs)
    k = GPB // 2
    while k >= 1:
        hi = ((lane // CG) & k) != 0
        new = list(arrs)
        for i in range(GPB):
            if i & k:
                continue
            lo_arr, hi_arr = arrs[i], arrs[i + k]
            new[i] = jnp.where(hi, pltpu.roll(hi_arr, CG * k, 1), lo_arr)
            new[i + k] = jnp.where(hi, hi_arr, pltpu.roll(lo_arr, 128 - CG * k, 1))
        arrs = new
        k //= 2
    return arrs


def _gather_block(ref, dst, nch):
    for half in range(CHUNK // GPB):
        outs = _lane_transpose([ref[pl.ds(half * GPB + l8, nch, stride=CHUNK), :] for l8 in range(GPB)])
        for g in range(GPB):
            dst[half, g] = outs[g]


def _scatter_block(src, ref, nch):
    for half in range(CHUNK // GPB):
        outs = _lane_transpose([src[half, g] for g in range(GPB)])
        for l8 in range(GPB):
            ref[pl.ds(half * GPB + l8, nch, stride=CHUNK), :] = outs[l8]


def _toeplitz(trow):
    lane = lax.broadcasted_iota(jnp.int32, (CG, 128), 1)
    x0, x1 = trow[:, :128], trow[:, 128:]
    zero = jnp.zeros_like(x0)
    rows = []
    for s in range(CHUNK):
        sh = (CG * s) % 128
        r0 = pltpu.roll(x0, sh, 1) if sh else x0
        r1 = pltpu.roll(x1, sh, 1) if sh else x1
        if CG * s < 128:
            rows.append(jnp.concatenate([jnp.where(lane >= sh, r0, zero), jnp.where(lane >= sh, r1, r0)], axis=1))
        else:
            rows.append(jnp.concatenate([zero, jnp.where(lane >= sh, r0, zero)], axis=1))
    return jnp.concatenate(rows, axis=0)


def _toeplitz_adjoint(dt):
    lane = lax.broadcasted_iota(jnp.int32, (CG, 128), 1)
    acc0 = jnp.zeros((CG, 128), F32)
    acc1 = jnp.zeros((CG, 128), F32)
    for s in range(CHUNK):
        x0, x1 = dt[CG * s:CG * s + CG, :128], dt[CG * s:CG * s + CG, 128:]
        sh = (CG * s) % 128
        keep = 128 - sh
        r0 = pltpu.roll(x0, keep, 1) if sh else x0
        r1 = pltpu.roll(x1, keep, 1) if sh else x1
        if CG * s < 128:
            acc0 = acc0 + jnp.where(lane < keep, r0, r1)
            acc1 = acc1 + jnp.where(lane < keep, r1, 0.0)
        else:
            acc0 = acc0 + jnp.where(lane < keep, r1, 0.0)
    return jnp.concatenate([acc0, acc1], axis=1)


def _s5_fwd(u, trow, camat, bzmat, pw):
    S = u.shape[0]
    nch = S // CHUNK
    n_steps = pw.shape[1]

    def body(u_ref, t_ref, ca_ref, bz_ref, pw_ref, y_ref, xp_ref, uc_ref, ub_s, yb_s):
        g = pl.program_id(1)

        @pl.when(g == 0)
        def _():
            _gather_block(u_ref, ub_s, nch)

        uc = jnp.concatenate([ub_s[0, g], ub_s[1, g]], axis=1)
        uc_ref[...] = uc
        x = _mm32(uc, bz_ref[...])
        for kk in range(n_steps):
            xs = _shift_rows(x, 2 ** kk, True)
            m = pw_ref[kk]
            x = x + m[0:1, :] * xs + m[1:2, :] * pltpu.roll(xs, STATE, 1)
        xp = _shift_rows(x, 1, True)
        xp_ref[...] = xp
        yc = _mm32(uc, _toeplitz(t_ref[...])) + _mm32(xp, ca_ref[...])
        yb_s[0, g] = yc[:, :128]
        yb_s[1, g] = yc[:, 128:]

        @pl.when(g == GPB - 1)
        def _():
            _scatter_block(yb_s, y_ref, nch)

    per = lambda a: pl.BlockSpec((None,) + a.shape[1:], lambda b, g: (b * GPB + g,) + (0,) * (a.ndim - 1))
    nat = pl.BlockSpec((S, 128), lambda b, g: (0, b))
    return pl.pallas_call(
        body, name="s5_fwd", grid=(GROUPS // GPB, GPB),
        in_specs=[nat, per(trow), per(camat), per(bzmat), per(pw)],
        out_specs=[nat, pl.BlockSpec((None, nch, 2 * STATE), lambda b, g: (b * GPB + g, 0, 0)),
                   pl.BlockSpec((None, nch, CHUNK * CG), lambda b, g: (b * GPB + g, 0, 0))],
        out_shape=[jax.ShapeDtypeStruct((S, GROUPS * CG), F32), jax.ShapeDtypeStruct((GROUPS, nch, 2 * STATE), F32),
                   jax.ShapeDtypeStruct((GROUPS, nch, CHUNK * CG), F32)],
        scratch_shapes=[pltpu.VMEM((CHUNK // GPB, GPB, nch, 128), F32)] * 2,
        compiler_params=_params(("parallel", "arbitrary"), VMEM_BIG),
    )(u, trow, camat, bzmat, pw)


def _s5_bwd(uc, dy, xp, trow, camat, bzmat, pwc):
    S = dy.shape[0]
    nch = S // CHUNK
    n_steps = pwc.shape[1]

    def body(uc_ref, dy_ref, xp_ref, t_ref, ca_ref, bz_ref, pw_ref, du_ref, dt_ref, dca_ref, dbz_ref, dal_ref,
             dyb_s, dub_s):
        g = pl.program_id(1)

        @pl.when(g == 0)
        def _():
            _gather_block(dy_ref, dyb_s, nch)

        uc = uc_ref[...]
        dyc = jnp.concatenate([dyb_s[0, g], dyb_s[1, g]], axis=1)
        xpv = xp_ref[...]
        dt_ref[...] = _toeplitz_adjoint(_mm32_tn(uc, dyc))
        dca_ref[...] = _mm32_tn(xpv, dyc)
        dx = _shift_rows(_mm32_nt(dyc, ca_ref[...]), 1, False)
        for kk in range(n_steps):
            xs = _shift_rows(dx, 2 ** kk, False)
            m = pw_ref[kk]
            dx = dx + m[0:1, :] * xs + m[1:2, :] * pltpu.roll(xs, STATE, 1)
        dbz_ref[...] = _mm32_tn(uc, dx)
        dal_ref[0:1, :] = jnp.sum(dx * xpv, axis=0, keepdims=True)
        dal_ref[1:2, :] = jnp.sum(dx * pltpu.roll(xpv, STATE, 1), axis=0, keepdims=True)
        duc = _mm32_nt(dyc, _toeplitz(t_ref[...])) + _mm32_nt(dx, bz_ref[...])
        dub_s[0, g] = duc[:, :128]
        dub_s[1, g] = duc[:, 128:]

        @pl.when(g == GPB - 1)
        def _():
            _scatter_block(dub_s, du_ref, nch)

    per = lambda a: pl.BlockSpec((None,) + a.shape[1:], lambda b, g: (b * GPB + g,) + (0,) * (a.ndim - 1))
    nat = pl.BlockSpec((S, 128), lambda b, g: (0, b))
    sds = jax.ShapeDtypeStruct
    mats = [sds(trow.shape, F32), sds(camat.shape, F32), sds(bzmat.shape, F32), sds((GROUPS, 2, 2 * STATE), F32)]
    return pl.pallas_call(
        body, name="s5_bwd", grid=(GROUPS // GPB, GPB),
        in_specs=[per(uc), nat, per(xp), per(trow), per(camat), per(bzmat), per(pwc)],
        out_specs=[nat] + [per(o) for o in mats], out_shape=[sds((S, GROUPS * CG), F32)] + mats,
        scratch_shapes=[pltpu.VMEM((CHUNK // GPB, GPB, nch, 128), F32)] * 2,
        compiler_params=_params(("parallel", "arbitrary"), VMEM_BIG),
    )(uc, dy, xp, trow, camat, bzmat, pwc)


GELU_C0 = math.sqrt(2.0 / math.pi)
GELU_C1 = 0.044715


def _mix(o, za, ys, zb, ga, gb, x, tgt, gate, b_glu, g_final, w_glu, w_up_a, w_up_b, w_out, hsel, ts):
    S = o.shape[0]

    def body(o_ref, za_ref, ys_ref, zb_ref, ga_ref, gb_ref, x_ref, t_ref, gate_ref, bglu_ref, gf_ref,
             wglu_ref, wua_ref, wub_ref, wout_ref, hsel_ref,
             dx2_ref, do_ref, dza_ref, dzb_ref, dga_ref, dgb_ref, dys_ref, dl_ref,
             mg_ref, dmo_ref, ya_ref, dua_ref, yb_ref, dub_ref, yg_ref, dgl_ref,
             dbglu_ref, dgate_ref, dgf_ref, loss_ref):
        @pl.when(pl.program_id(0) == 0)
        def _():
            dbglu_ref[...] = jnp.zeros_like(dbglu_ref)
            dgate_ref[...] = jnp.zeros_like(dgate_ref)
            dgf_ref[...] = jnp.zeros_like(dgf_ref)
            loss_ref[...] = jnp.zeros_like(loss_ref)

        ov = o_ref[...]
        za = za_ref[...]
        sza = _sigmoid(za)
        silu_a = za * sza
        ya = ov * silu_a
        ysv = ys_ref[...]
        th = jnp.tanh(GELU_C0 * (ysv + GELU_C1 * ysv * ysv * ysv))
        yg = 0.5 * ysv * (1.0 + th)
        sg = _sigmoid(_mm(yg, wglu_ref[...]) + bglu_ref[...])
        yb1 = yg * sg
        zb = zb_ref[...]
        szb = _sigmoid(zb)
        silu_b = zb * szb
        yb = yb1 * silu_b
        ua = _mm(ya, wua_ref[...])
        ub = _mm(yb, wub_ref[...])
        sa = _sigmoid(ga_ref[...])
        sb = _sigmoid(gb_ref[...])
        merged = sa * ua + sb * ub
        mo = _mm(merged, wout_ref[...])
        gate_v = gate_ref[...]
        x2 = x_ref[...] + gate_v * mo
        r2 = lax.rsqrt(jnp.mean(x2 * x2, axis=-1, keepdims=True) + EPS)
        x2n = x2 * r2
        gf = gf_ref[...]
        diff = x2n * gf - t_ref[...]
        loss_ref[...] += jnp.sum(jnp.sum(diff * diff, axis=-1, keepdims=True), axis=0, keepdims=True) * (0.5 / D_MODEL)
        dy = diff * (1.0 / D_MODEL)
        dgf_ref[...] += jnp.sum(dy * x2n, axis=0, keepdims=True)
        dyg = dy * gf
        dx2 = r2 * (dyg - x2n * jnp.mean(dyg * x2n, axis=-1, keepdims=True))
        dx2_ref[...] = dx2
        dgate_ref[...] += jnp.sum(dx2 * mo, axis=0, keepdims=True)
        dmo = dx2 * gate_v
        dmerged = _mm_nt(dmo, wout_ref[...])
        dua = dmerged * sa
        dub = dmerged * sb
        dga_ref[...] = (dmerged * ua * sa * (1.0 - sa)).astype(dga_ref.dtype)
        dgb_ref[...] = (dmerged * ub * sb * (1.0 - sb)).astype(dgb_ref.dtype)
        dya = _mm_nt(dua, wua_ref[...])
        dyb = _mm_nt(dub, wub_ref[...])
        dov = dya * silu_a
        do_ref[...] = dov.astype(do_ref.dtype)
        dl_ref[...] = _mm32_nt(hsel_ref[...], dov * ov)
        dza_ref[...] = (dya * ov * (sza * (1.0 + za * (1.0 - sza)))).astype(dza_ref.dtype)
        dyb1 = dyb * silu_b
        dzb_ref[...] = (dyb * yb1 * (szb * (1.0 + zb * (1.0 - szb)))).astype(dzb_ref.dtype)
        dgl = dyb1 * yg * sg * (1.0 - sg)
        dbglu_ref[...] += jnp.sum(dgl, axis=0, keepdims=True)
        dyg2 = dyb1 * sg + _mm_nt(dgl, wglu_ref[...])
        dgelu = 0.5 * (1.0 + th) + 0.5 * ysv * (1.0 - th * th) * GELU_C0 * (1.0 + 3.0 * GELU_C1 * ysv * ysv)
        dys_ref[...] = dyg2 * dgelu
        mg_ref[...] = merged.astype(mg_ref.dtype)
        dmo_ref[...] = dmo.astype(dmo_ref.dtype)
        ya_ref[...] = ya.astype(ya_ref.dtype)
        dua_ref[...] = dua.astype(dua_ref.dtype)
        yb_ref[...] = yb.astype(yb_ref.dtype)
        dub_ref[...] = dub.astype(dub_ref.dtype)
        yg_ref[...] = yg.astype(yg_ref.dtype)
        dgl_ref[...] = dgl.astype(dgl_ref.dtype)

    row = lambda n: pl.BlockSpec((ts, n), lambda i: (i, 0))
    full = lambda a: pl.BlockSpec(a.shape, lambda i: (0,) * a.ndim)
    vec = lambda n: pl.BlockSpec((1, n), lambda i: (0, 0))
    sds = jax.ShapeDtypeStruct
    W, Dm = WIDTH, D_MODEL
    return pl.pallas_call(
        body, name="mix", grid=(S // ts,),
        in_specs=[row(W), row(W), row(W), row(W), row(Dm), row(Dm), row(Dm), row(Dm),
                  full(gate), full(b_glu), full(g_final), full(w_glu), full(w_up_a), full(w_up_b), full(w_out), full(hsel)],
        out_specs=[row(Dm), row(W), row(W), row(W), row(Dm), row(Dm), row(W), pl.BlockSpec((HEADS, ts), lambda i: (0, i)),
                   row(Dm), row(Dm), row(W), row(Dm), row(W), row(Dm), row(W), row(W),
                   vec(W), vec(Dm), vec(Dm), vec(1)],
        out_shape=[sds((S, Dm), F32), sds((S, W), MXU_DTYPE), sds((S, W), MXU_DTYPE), sds((S, W), MXU_DTYPE),
                   sds((S, Dm), MXU_DTYPE), sds((S, Dm), MXU_DTYPE), sds((S, W), F32), sds((HEADS, S), F32),
                   sds((S, Dm), MXU_DTYPE), sds((S, Dm), MXU_DTYPE), sds((S, W), MXU_DTYPE), sds((S, Dm), MXU_DTYPE),
                   sds((S, W), MXU_DTYPE), sds((S, Dm), MXU_DTYPE), sds((S, W), MXU_DTYPE), sds((S, W), MXU_DTYPE),
                   sds((1, W), F32), sds((1, Dm), F32), sds((1, Dm), F32), sds((1, 1), F32)],
        compiler_params=_params(("arbitrary",), VMEM_BIG),
    )(o, za, ys, zb, ga, gb, x, tgt, gate, b_glu, g_final, w_glu, w_up_a, w_up_b, w_out, hsel)


def _matmul_tn(name, a, b, ts):
    S, M = a.shape
    N = b.shape[1]
    tn = min(N, 1024)

    def body(a_ref, b_ref, o_ref):
        @pl.when(pl.program_id(1) == 0)
        def _():
            o_ref[...] = jnp.zeros_like(o_ref)

        o_ref[...] += _mm_tn(a_ref[...], b_ref[...])

    return pl.pallas_call(
        body, name=name, grid=(N // tn, S // ts),
        in_specs=[pl.BlockSpec((ts, M), lambda j, i: (i, 0)), pl.BlockSpec((ts, tn), lambda j, i: (i, j))],
        out_specs=pl.BlockSpec((M, tn), lambda j, i: (0, j)),
        out_shape=jax.ShapeDtypeStruct((M, N), F32),
        compiler_params=_params(("parallel", "arbitrary"), VMEM_MID),
    )(a, b)


def _proj_bwd(dq, dk, dv, dza, du, dzb, dga, dgb, dfl, x, dx2, shift, scale, g_norm, w_main, w_ft, ts):
    S = x.shape[0]

    def body(dq_ref, dk_ref, dv_ref, dza_ref, du_ref, dzb_ref, dga_ref, dgb_ref, dfl_ref, x_ref, dx2_ref,
             sc_ref, gn_ref, w_ref, wft_ref, gx_ref, dsh_ref, dsc_ref, dgn_ref):
        @pl.when(pl.program_id(0) == 0)
        def _():
            dsh_ref[...] = jnp.zeros_like(dsh_ref)
            dsc_ref[...] = jnp.zeros_like(dsc_ref)
            dgn_ref[...] = jnp.zeros_like(dgn_ref)

        def seg(ref, off, n):
            return _mm_nt(ref[...], w_ref[:, off:off + n])

        dh = (seg(dq_ref, M_Q, WIDTH) + seg(dk_ref, M_K, WIDTH) + seg(dv_ref, M_V, WIDTH)
              + seg(dza_ref, M_ZA, WIDTH) + seg(du_ref, M_U, WIDTH) + seg(dzb_ref, M_ZB, WIDTH)
              + seg(dga_ref, M_GA, D_MODEL) + seg(dgb_ref, M_GB, D_MODEL)
              + _mm32(dfl_ref[...], wft_ref[...]))
        xv = x_ref[...]
        r = lax.rsqrt(jnp.mean(xv * xv, axis=-1, keepdims=True) + EPS)
        xn = xv * r
        gn = gn_ref[...]
        s1 = 1.0 + sc_ref[...]
        dsh_ref[...] += jnp.sum(dh, axis=0, keepdims=True)
        dhx = dh * xn
        dsc_ref[...] += jnp.sum(dhx, axis=0, keepdims=True) * gn
        dgn_ref[...] += jnp.sum(dhx, axis=0, keepdims=True) * s1
        dxn = dh * (gn * s1)
        gx_ref[...] = dx2_ref[...] + r * (dxn - xn * jnp.mean(dxn * xn, axis=-1, keepdims=True))

    row = lambda n: pl.BlockSpec((ts, n), lambda i: (i, 0))
    full = lambda a: pl.BlockSpec(a.shape, lambda i: (0,) * a.ndim)
    vec = pl.BlockSpec((1, D_MODEL), lambda i: (0, 0))
    W, Dm = WIDTH, D_MODEL
    del shift
    return pl.pallas_call(
        body, name="proj_bwd", grid=(S // ts,),
        in_specs=[row(W)] * 6 + [row(Dm)] * 2 + [row(HEADS), row(Dm), row(Dm),
                                                 full(scale), full(g_norm), full(w_main), full(w_ft)],
        out_specs=[row(Dm), vec, vec, vec],
        out_shape=[jax.ShapeDtypeStruct((S, Dm), F32)] + [jax.ShapeDtypeStruct((1, Dm), F32)] * 3,
        compiler_params=_params(("arbitrary",), VMEM_BIG),
    )(dq, dk, dv, dza, du, dzb, dga, dgb, dfl, x, dx2, scale, g_norm, w_main, w_ft)


def _adamw(name, planes, w, m, v, tr):
    n, R, C = planes.shape
    bc1 = 1.0 - ADAM_B1 ** ADAM_STEP
    bc2 = 1.0 - ADAM_B2 ** ADAM_STEP

    def body(p_ref, w_ref, m_ref, v_ref, g_ref, d_ref, nm_ref, nv_ref):
        g = p_ref[0].astype(F32)
        for i in range(1, n):
            g = g + p_ref[i].astype(F32)
        g_ref[...] = g
        nm = ADAM_B1 * m_ref[...] + (1.0 - ADAM_B1) * g
        nv = ADAM_B2 * v_ref[...] + (1.0 - ADAM_B2) * (g * g)
        nm_ref[...] = nm
        nv_ref[...] = nv
        d_ref[...] = -ADAM_LR * ((nm / bc1) / (jnp.sqrt(nv / bc2) + ADAM_EPS) + ADAM_WD * w_ref[...])

    blk = pl.BlockSpec((tr, C), lambda i: (i, 0))
    return pl.pallas_call(
        body, name=name, grid=(R // tr,),
        in_specs=[pl.BlockSpec((n, tr, C), lambda i: (0, i, 0)), blk, blk, blk],
        out_specs=[blk] * 4, out_shape=[jax.ShapeDtypeStruct((R, C), F32)] * 4,
        compiler_params=_params(("parallel",), VMEM_MID),
    )(planes, w, m, v)


def _wada_grad(c_all, dmod_cols):
    def body(c_ref, d_ref, o_ref):
        o_ref[0] = _mm32_tn(c_ref[...], d_ref[...])

    return pl.pallas_call(
        body, name="wada_grad",
        out_shape=jax.ShapeDtypeStruct((1, c_all.shape[1], dmod_cols.shape[1]), F32),
        in_specs=[VMEM, VMEM], out_specs=VMEM,
    )(c_all, dmod_cols)


SMALL_ORDER = ("b_ada", "g_norm", "b_f", "a_re", "a_im", "log_dt", "b_re", "b_im", "c_re", "c_im",
               "d_skip", "b_glu", "g_final")
BIG_ORDER = ("w_ada", "w_in", "w_glu", "w_up_a", "w_up_b", "w_out")
ALL_ORDER = ("w_ada", "b_ada", "g_norm", "w_in", "b_f", "a_re", "a_im", "log_dt", "b_re", "b_im", "c_re", "c_im",
             "d_skip", "w_glu", "b_glu", "w_up_a", "w_up_b", "w_out", "g_final")


def _pack_small(parts, rows):
    flat = jnp.concatenate([p.reshape(-1).astype(F32) for p in parts])
    return jnp.pad(flat, (0, rows * 128 - flat.shape[0])).reshape(rows, 128)


def kernel(x, c, w_ada, b_ada, g_norm, w_in, b_f, a_re, a_im, log_dt, b_re, b_im, c_re, c_im, d_skip, w_glu, b_glu, w_up_a, w_up_b, w_out, g_final, loss_target, m_w_ada, m_b_ada, m_g_norm, m_w_in, m_b_f, m_a_re, m_a_im, m_log_dt, m_b_re, m_b_im, m_c_re, m_c_im, m_d_skip, m_w_glu, m_b_glu, m_w_up_a, m_w_up_b, m_w_out, m_g_final, v_w_ada, v_b_ada, v_g_norm, v_w_in, v_b_f, v_a_re, v_a_im, v_log_dt, v_b_re, v_b_im, v_c_re, v_c_im, v_d_skip, v_w_glu, v_b_glu, v_w_up_a, v_w_up_b, v_w_out, v_g_final):
    weights = dict(w_ada=w_ada, b_ada=b_ada, g_norm=g_norm, w_in=w_in, b_f=b_f, a_re=a_re, a_im=a_im, log_dt=log_dt,
                   b_re=b_re, b_im=b_im, c_re=c_re, c_im=c_im, d_skip=d_skip, w_glu=w_glu, b_glu=b_glu,
                   w_up_a=w_up_a, w_up_b=w_up_b, w_out=w_out, g_final=g_final)
    mom_m = dict(w_ada=m_w_ada, b_ada=m_b_ada, g_norm=m_g_norm, w_in=m_w_in, b_f=m_b_f, a_re=m_a_re, a_im=m_a_im,
                 log_dt=m_log_dt, b_re=m_b_re, b_im=m_b_im, c_re=m_c_re, c_im=m_c_im, d_skip=m_d_skip, w_glu=m_w_glu,
                 b_glu=m_b_glu, w_up_a=m_w_up_a, w_up_b=m_w_up_b, w_out=m_w_out, g_final=m_g_final)
    mom_v = dict(w_ada=v_w_ada, b_ada=v_b_ada, g_norm=v_g_norm, w_in=v_w_in, b_f=v_b_f, a_re=v_a_re, a_im=v_a_im,
                 log_dt=v_log_dt, b_re=v_b_re, b_im=v_b_im, c_re=v_c_re, c_im=v_c_im, d_skip=v_d_skip, w_glu=v_w_glu,
                 b_glu=v_b_glu, w_up_a=v_w_up_a, w_up_b=v_w_up_b, w_out=v_w_out, g_final=v_g_final)
    xs = x[0]
    tgt = loss_target[0]
    S = xs.shape[0]
    ts = min(256, S)
    ta = min(512, S)
    tw = min(2048, S)
    nch = S // CHUNK
    n_steps = max(1, int(math.ceil(math.log2(nch))))
    me = _my_index()

    shards = [w.astype(MXU_DTYPE) for w in (w_in[0], w_glu[0], w_up_a[0], w_up_b[0], w_out[0])]
    mod8, c_all, gathered = _comm_in(c, w_ada[0], b_ada.reshape(N_DEV, -1), shards)
    mod = mod8.reshape(1, 3 * D_MODEL)
    shift, scale, gate = mod[:, :D_MODEL], mod[:, D_MODEL:2 * D_MODEL], mod[:, 2 * D_MODEL:]
    w_in_full = gathered[0].transpose(1, 0, 2).reshape(D_MODEL, PROJ_WIDTH)
    w_main = jnp.concatenate([w_in_full[:, :OFF_F], w_in_full[:, OFF_F + HEADS:]], axis=1)
    w_f = w_in_full[:, OFF_F:OFF_F + HEADS].astype(F32)
    w_ft = w_f.T
    w_glu_full = gathered[1].reshape(WIDTH, WIDTH)
    w_up_a_full = gathered[2].transpose(1, 0, 2).reshape(WIDTH, D_MODEL)
    w_up_b_full = gathered[3].transpose(1, 0, 2).reshape(WIDTH, D_MODEL)
    w_out_full = gathered[4].reshape(D_MODEL, D_MODEL)

    q, k, v, za, u, zb, ga, gb, flc, hb = _proj_fwd(xs, shift, scale, g_norm, w_main, w_f, ts)
    nb = S // ta
    rows4 = lambda r: r.reshape(PAIRS, 2, nb, ta).transpose(0, 2, 1, 3)
    qh, kh, vt = _attn_prep(q, k, v, flc, b_f, ta)
    o, lse4 = _attn_fwd(qh, kh, vt, ta)

    s5_params = (a_re[0], a_im[0], log_dt[0], b_re[0], b_im[0], c_re[0], c_im[0], d_skip[0])
    (trow, camat, bzmat, al), mats_vjp = jax.vjp(_s5_mats, *s5_params)
    del al
    pw_f, pw_b = _s5_scan_powers(a_re[0], a_im[0], log_dt[0], n_steps)
    ys, xprev, uc = _s5_fwd(u, trow, camat, bzmat, pw_f)

    hsel = (np.arange(WIDTH)[None, :] // 64 == np.arange(HEADS)[:, None]).astype(np.float32)
    (dx2, do, dza, dzb, dga, dgb, dys, dl_row, merged, dmo, ya, dua, yb, dub, yg, dgl,
     db_glu, dgate, dg_final, loss_part) = _mix(o, za, ys, zb, ga, gb, xs, tgt, gate, b_glu, g_final.reshape(1, -1),
                                                w_glu_full, w_up_a_full, w_up_b_full, w_out_full, jnp.asarray(hsel), ts)

    gw_out = _matmul_tn("dw_out", merged, dmo, tw)
    gw_up_a = _matmul_tn("dw_up_a", ya, dua, tw)
    gw_up_b = _matmul_tn("dw_up_b", yb, dub, tw)
    gw_glu = _matmul_tn("dw_glu", yg, dgl, tw)

    du, d_trow, d_camat, d_bzmat, dal2 = _s5_bwd(uc, dys, xprev, trow, camat, bzmat, pw_b)
    d_al = jnp.concatenate([dal2[:, 0, :STATE] + dal2[:, 0, STATE:], dal2[:, 1, STATE:] - dal2[:, 1, :STATE]], axis=-1)
    gs5 = mats_vjp((d_trow, d_camat, d_bzmat, d_al))

    dl4 = rows4(dl_row)
    dq, dk, dv, dfq4, dfk4 = _attn_bwd(qh, do, kh, v, lse4, dl4, ta)
    d_fcol = dfq4.transpose(0, 2, 1, 3).reshape(HEADS, S).T - dfk4.transpose(1, 0, 2).reshape(S, HEADS)
    dfl, db_f = _fgate_bwd(d_fcol, flc, b_f, ta)

    grad_x, dshift, dscale, dg_norm = _proj_bwd(dq, dk, dv, dza, du, dzb, dga, dgb, dfl, xs, dx2,
                                                shift, scale, g_norm, w_main, w_ft, ts)
    segs = [("dw_q", dq), ("dw_k", dk), ("dw_v", dv), ("dw_f", dfl), ("dw_za", dza), ("dw_u", du), ("dw_zb", dzb),
            ("dw_ga", dga), ("dw_gb", dgb)]
    gw_in = jnp.concatenate([_matmul_tn(nm, hb, d, tw) for nm, d in segs], axis=1)

    planes = [gw_in.reshape(D_MODEL, N_DEV, -1).transpose(1, 0, 2),
              gw_glu.reshape(N_DEV, -1, WIDTH),
              gw_up_a.reshape(WIDTH, N_DEV, -1).transpose(1, 0, 2),
              gw_up_b.reshape(WIDTH, N_DEV, -1).transpose(1, 0, 2),
              gw_out.reshape(N_DEV, -1, D_MODEL)]
    planes = [p.astype(MXU_DTYPE) for p in planes]
    dmod = jnp.concatenate([dshift, dscale, dgate], axis=1)
    small_parts = [dmod, dg_norm, db_f, gs5[0], gs5[1], gs5[2], gs5[3], gs5[4], gs5[5], gs5[6], gs5[7],
                   db_glu, dg_final, loss_part]
    n_small = sum(int(np.prod(p.shape)) for p in small_parts)
    rows = -(-n_small // (8 * 128)) * 8
    small = _pack_small(small_parts, rows)
    from_sib = _comm_pair(planes)
    core = lax.axis_index("c")
    chip_planes = []
    for name, p, s in zip(("w_in", "w_glu", "w_up_a", "w_up_b", "w_out"), planes, from_sib):
        tr = 256 if s.shape[1] % 256 == 0 else s.shape[1]
        chip_planes.append(_pair_sum("pair_sum_" + name, p, s, core, tr))
    recv, small_all = _comm_out(chip_planes, small)

    grads, deltas, new_m, new_v = {}, {}, {}, {}

    def put(name, res, shape):
        grads[name], deltas[name], new_m[name], new_v[name] = [r.reshape(shape) for r in res]

    names = ("w_in", "w_glu", "w_up_a", "w_up_b", "w_out")
    for name, pr in zip(names, recv):
        w2 = weights[name][0]
        tr = 256 if w2.shape[0] % 256 == 0 else w2.shape[0]
        put(name, _adamw("adamw_" + name, pr, w2, mom_m[name][0], mom_v[name][0], tr), weights[name].shape)
    cols = w_ada.shape[2]
    dmod_all = small_all[:, :24, :].reshape(N_DEV, 3 * D_MODEL)
    dmod_cols = lax.dynamic_slice_in_dim(dmod_all, me * cols, cols, axis=1)
    g_wada = _wada_grad(c_all, dmod_cols)
    put("w_ada", _adamw("adamw_w_ada", g_wada, w_ada[0], m_w_ada[0], v_w_ada[0], 256), w_ada.shape)
    pack = lambda d: _pack_small([d[n] for n in SMALL_ORDER] + [jnp.zeros((1,), F32)], rows)
    res_small = _adamw("adamw_small", small_all, pack(weights), pack(mom_m), pack(mom_v), rows)
    flat = [r.reshape(-1) for r in res_small]
    off = 0
    for name in SMALL_ORDER:
        shape = weights[name].shape
        size = int(np.prod(shape))
        put(name, [f[off:off + size] for f in flat], shape)
        off += size
    loss = flat[0][off]

    return (loss, grad_x[None], *[grads[n] for n in ALL_ORDER], *[deltas[n] for n in ALL_ORDER],
            *[new_m[n] for n in ALL_ORDER], *[new_v[n] for n in ALL_ORDER])
```

```python
import functools
import math

import jax
import jax.numpy as jnp
import numpy as np
from jax import lax
from jax.experimental import pallas as pl
from jax.experimental.pallas import tpu as pltpu

F32 = jnp.float32
MXU_DTYPE = jnp.bfloat16
HI = lax.Precision.HIGHEST

N_DEV = 8
D_MODEL = 1024
WIDTH = 512
HEADS = 8
PAIRS = HEADS // 2
GROUPS = 32
STATE = 64
CG = 16
CHUNK = 16
EPS = 1e-6
NEG = float(np.finfo(np.float32).min)

ADAM_LR = 0.001
ADAM_B1 = 0.9
ADAM_B2 = 0.999
ADAM_EPS = 1e-08
ADAM_WD = 0.01
ADAM_STEP = 10

VMEM_BIG = 56 * 1024 * 1024
VMEM_MID = 40 * 1024 * 1024

OFF_F = 3 * WIDTH
PROJ_WIDTH = 5128
M_Q, M_K, M_V, M_ZA, M_U, M_ZB, M_GA, M_GB = 0, 512, 1024, 1536, 2048, 2560, 3072, 4096


def _mm(a, b):
    return jnp.dot(a.astype(MXU_DTYPE), b.astype(MXU_DTYPE), preferred_element_type=F32)


def _mm_nt(a, b):
    return lax.dot_general(a.astype(MXU_DTYPE), b.astype(MXU_DTYPE), (((1,), (1,)), ((), ())),
                           preferred_element_type=F32)


def _mm_tn(a, b):
    return lax.dot_general(a.astype(MXU_DTYPE), b.astype(MXU_DTYPE), (((0,), (0,)), ((), ())),
                           preferred_element_type=F32)


def _mm32(a, b):
    return jnp.dot(a, b, precision=HI, preferred_element_type=F32)


def _mm32_nt(a, b):
    return lax.dot_general(a, b, (((1,), (1,)), ((), ())), precision=HI, preferred_element_type=F32)


def _mm32_tn(a, b):
    return lax.dot_general(a, b, (((0,), (0,)), ((), ())), precision=HI, preferred_element_type=F32)


S5_PRECISION = lax.Precision.HIGH


def _s5mm(a, b):
    return jnp.dot(a, b, precision=S5_PRECISION, preferred_element_type=F32)


def _s5mm_nt(a, b):
    return lax.dot_general(a, b, (((1,), (1,)), ((), ())), precision=S5_PRECISION, preferred_element_type=F32)


def _s5mm_tn(a, b):
    return lax.dot_general(a, b, (((0,), (0,)), ((), ())), precision=S5_PRECISION, preferred_element_type=F32)


def _sigmoid(x):
    return 1.0 / (1.0 + jnp.exp(-x))


def _params(sem=None, vmem=None):
    kw = {}
    if sem is not None:
        kw["dimension_semantics"] = sem
    if vmem is not None:
        kw["vmem_limit_bytes"] = vmem
    return pltpu.CompilerParams(**kw)


def _my_index():
    return 4 * lax.axis_index("x") + 2 * lax.axis_index("y") + lax.axis_index("c")


def _dev(p):
    return (p // 4, (p // 2) % 2, p % 2)


ANY = pl.BlockSpec(memory_space=pl.ANY)
VMEM = pl.BlockSpec(memory_space=pltpu.VMEM)
MESH = pl.DeviceIdType.MESH


def _comm_in(c, w_ada, b_ada8, shards):
    n = len(shards)
    cols = w_ada.shape[1]

    def body(c_ref, wada_ref, bada_ref, *rest):
        srcs = rest[:n]
        mod_ref, call_ref = rest[n], rest[n + 1]
        dsts = rest[n + 2:2 * n + 2]
        modp, wsend, wrecv, wloc, csend, crecv, msend, mrecv = rest[2 * n + 2:]
        me = _my_index()

        x, y, cc = lax.axis_index("x"), lax.axis_index("y"), lax.axis_index("c")
        sib = (x, y, 1 - cc)
        chips = [(1 - x, y), (x, 1 - y), (1 - x, 1 - y)]

        def wcopy(a, k, block, to, src=None):
            ref = dsts[a].at[4 * block[0] + 2 * block[1] + block[2]]
            return pltpu.make_async_remote_copy(src_ref=ref if src is None else src, dst_ref=ref,
                                                send_sem=wsend.at[a, k], recv_sem=wrecv.at[a, k],
                                                device_id=to, device_id_type=MESH)

        def ccopy(src_dev, d, to):
            return pltpu.make_async_remote_copy(src_ref=c_ref, dst_ref=call_ref.at[pl.ds(src_dev, 1)],
                                                send_sem=csend.at[d], recv_sem=crecv.at[src_dev],
                                                device_id=_dev(to), device_id_type=MESH)

        def mcopy(src_dev, d, to):
            return pltpu.make_async_remote_copy(src_ref=modp.at[pl.ds(to, 1)], dst_ref=mod_ref.at[pl.ds(src_dev, 1)],
                                                send_sem=msend.at[d], recv_sem=mrecv.at[src_dev],
                                                device_id=_dev(to), device_id_type=MESH)

        local = [pltpu.make_async_copy(srcs[a], dsts[a].at[me], wloc.at[a]) for a in range(n)]
        for cp in local:
            cp.start()
        peers = [(me + d) % N_DEV for d in range(1, N_DEV)]
        first = []
        for a in range(n):
            first.append(wcopy(a, 0, (x, y, cc), sib, src=srcs[a]))
            first += [wcopy(a, 1 + j, (x, y, cc), (*chip, cc), src=srcs[a]) for j, chip in enumerate(chips)]
        for cp in first:
            cp.start()
        call_ref[pl.ds(me, 1), :] = c_ref[...]
        for d, p in enumerate(peers):
            ccopy(me, d, p).start()
        for d, p in enumerate(peers):
            ccopy(p, d, p).wait_recv()
        modp[...] = _mm32(call_ref[...], wada_ref[...]) + bada_ref[pl.ds(me, 1), :]
        mod_ref[pl.ds(me, 1), :] = modp[pl.ds(me, 1), :]
        for d, p in enumerate(peers):
            mcopy(me, d, p).start()
        for d, p in enumerate(peers):
            mcopy(p, d, p).wait_recv()
        passed = []
        for j, chip in enumerate(chips):
            for a in range(n):
                wcopy(a, 1 + j, (*chip, cc), (x, y, cc)).wait_recv()
                cp = wcopy(a, 4 + j, (*chip, cc), sib)
                cp.start()
                passed.append(cp)
        for a in range(n):
            wcopy(a, 0, sib, (x, y, cc)).wait_recv()
            for j, chip in enumerate(chips):
                wcopy(a, 4 + j, (*chip, 1 - cc), (x, y, cc)).wait_recv()
        for cp in first + passed:
            cp.wait_send()
        for d, p in enumerate(peers):
            ccopy(me, d, p).wait_send()
            mcopy(me, d, p).wait_send()
        for cp in local:
            cp.wait()

    out_shape = ([jax.ShapeDtypeStruct((N_DEV, cols), F32), jax.ShapeDtypeStruct((N_DEV, D_MODEL), F32)]
                 + [jax.ShapeDtypeStruct((N_DEV,) + s.shape, s.dtype) for s in shards])
    res = pl.pallas_call(
        body, name="comm_in", out_shape=out_shape,
        in_specs=[VMEM, VMEM, VMEM] + [ANY] * n,
        out_specs=[VMEM, VMEM] + [ANY] * n,
        scratch_shapes=[pltpu.VMEM((N_DEV, cols), F32),
                        pltpu.SemaphoreType.DMA((n, N_DEV)), pltpu.SemaphoreType.DMA((n, N_DEV)),
                        pltpu.SemaphoreType.DMA((n,)),
                        pltpu.SemaphoreType.DMA((N_DEV,)), pltpu.SemaphoreType.DMA((N_DEV,)),
                        pltpu.SemaphoreType.DMA((N_DEV,)), pltpu.SemaphoreType.DMA((N_DEV,))],
        compiler_params=_params(vmem=VMEM_MID),
    )(c, w_ada, b_ada8, *shards)
    return res[0], res[1], list(res[2:])


N_CHIP = 4


def _comm_pair(planes):
    n = len(planes)

    def body(*rest):
        srcs, dsts = rest[:n], rest[n:2 * n]
        send, recv = rest[2 * n:]
        x, y, cc = lax.axis_index("x"), lax.axis_index("y"), lax.axis_index("c")
        copies = [pltpu.make_async_remote_copy(src_ref=srcs[a].at[2 * ch + 1 - cc], dst_ref=dsts[a].at[ch],
                                               send_sem=send.at[a, ch], recv_sem=recv.at[a, ch],
                                               device_id=(x, y, 1 - cc), device_id_type=MESH)
                  for a in range(n) for ch in range(N_CHIP)]
        for cp in copies:
            cp.start()
        for cp in copies:
            cp.wait()

    out_shape = [jax.ShapeDtypeStruct((N_CHIP,) + p.shape[1:], p.dtype) for p in planes]
    return pl.pallas_call(
        body, name="comm_pair", out_shape=out_shape, in_specs=[ANY] * n, out_specs=[ANY] * n,
        scratch_shapes=[pltpu.SemaphoreType.DMA((n, N_CHIP)), pltpu.SemaphoreType.DMA((n, N_CHIP))],
    )(*planes)


def _pair_sum(name, planes, from_sib, core, tr):
    _, R, C = from_sib.shape

    def body(core_ref, a_ref, b_ref, o_ref):
        del core_ref
        o_ref[...] = (a_ref[...].astype(F32) + b_ref[...].astype(F32)).astype(o_ref.dtype)

    blk = pl.BlockSpec((None, tr, C), lambda i, j, c: (i, j, 0))
    grid_spec = pltpu.PrefetchScalarGridSpec(
        num_scalar_prefetch=1, grid=(N_CHIP, R // tr),
        in_specs=[pl.BlockSpec((None, tr, C), lambda i, j, c: (2 * i + c[0], j, 0)), blk], out_specs=blk)
    return pl.pallas_call(
        body, name=name, grid_spec=grid_spec, out_shape=jax.ShapeDtypeStruct(from_sib.shape, from_sib.dtype),
        compiler_params=_params(("parallel", "parallel"), VMEM_MID),
    )(core.reshape(1).astype(jnp.int32), planes, from_sib)


def _comm_out(chip_planes, small):
    n = len(chip_planes)

    def body(*rest):
        srcs = rest[:n]
        small_ref = rest[n]
        dsts = rest[n + 1:2 * n + 1]
        sall_ref = rest[2 * n + 1]
        wsend, wrecv, wloc, ssend, srecv, sloc = rest[2 * n + 2:]
        me = _my_index()
        x, y, cc = lax.axis_index("x"), lax.axis_index("y"), lax.axis_index("c")
        mine = 2 * x + y
        chips = [(1 - x, y), (x, 1 - y), (1 - x, 1 - y)]

        def wcopy(a, j, sending):
            chip = chips[j]
            there = 2 * chip[0] + chip[1]
            return pltpu.make_async_remote_copy(src_ref=srcs[a].at[there], dst_ref=dsts[a].at[mine if sending else there],
                                                send_sem=wsend.at[a, j], recv_sem=wrecv.at[a, j],
                                                device_id=(*chip, cc), device_id_type=MESH)

        def scopy(src_dev, d, to):
            return pltpu.make_async_remote_copy(src_ref=small_ref, dst_ref=sall_ref.at[src_dev],
                                                send_sem=ssend.at[d], recv_sem=srecv.at[src_dev],
                                                device_id=_dev(to), device_id_type=MESH)

        local = [pltpu.make_async_copy(srcs[a].at[mine], dsts[a].at[mine], wloc.at[a]) for a in range(n)]
        local.append(pltpu.make_async_copy(small_ref, sall_ref.at[me], sloc))
        for cp in local:
            cp.start()
        peers = [(me + d) % N_DEV for d in range(1, N_DEV)]
        for d, p in enumerate(peers):
            scopy(me, d, p).start()
        for j in range(len(chips)):
            for a in range(n):
                wcopy(a, j, True).start()
        for d, p in enumerate(peers):
            scopy(p, d, p).wait_recv()
        for j in range(len(chips)):
            for a in range(n):
                wcopy(a, j, False).wait_recv()
        for d, p in enumerate(peers):
            scopy(me, d, p).wait_send()
        for j in range(len(chips)):
            for a in range(n):
                wcopy(a, j, True).wait_send()
        for cp in local:
            cp.wait()

    out_shape = ([jax.ShapeDtypeStruct(p.shape, p.dtype) for p in chip_planes]
                 + [jax.ShapeDtypeStruct((N_DEV,) + small.shape, small.dtype)])
    res = pl.pallas_call(
        body, name="comm_out", out_shape=out_shape,
        in_specs=[ANY] * (n + 1), out_specs=[ANY] * (n + 1),
        scratch_shapes=[pltpu.SemaphoreType.DMA((n, N_CHIP)), pltpu.SemaphoreType.DMA((n, N_CHIP)),
                        pltpu.SemaphoreType.DMA((n,)),
                        pltpu.SemaphoreType.DMA((N_DEV,)), pltpu.SemaphoreType.DMA((N_DEV,)),
                        pltpu.SemaphoreType.DMA(())],
    )(*chip_planes, small)
    return list(res[:n]), res[n]


def _proj_fwd(x, shift, scale, g_norm, w_main, w_f, ts):
    S = x.shape[0]

    def body(x_ref, sh_ref, sc_ref, gn_ref, w_ref, wf_ref,
             q_ref, k_ref, v_ref, za_ref, u_ref, zb_ref, ga_ref, gb_ref, flc_ref, h_ref):
        xv = x_ref[...]
        r = lax.rsqrt(jnp.mean(xv * xv, axis=-1, keepdims=True) + EPS)
        h = (xv * r) * gn_ref[...] * (1.0 + sc_ref[...]) + sh_ref[...]
        hb = h.astype(MXU_DTYPE)
        h_ref[...] = hb

        def seg(off, n):
            return jnp.dot(hb, w_ref[:, off:off + n], preferred_element_type=F32)

        q_ref[...] = (seg(M_Q, WIDTH) * 0.125).astype(q_ref.dtype)
        k_ref[...] = seg(M_K, WIDTH).astype(k_ref.dtype)
        v_ref[...] = seg(M_V, WIDTH).astype(v_ref.dtype)
        za_ref[...] = seg(M_ZA, WIDTH)
        u_ref[...] = seg(M_U, WIDTH)
        zb_ref[...] = seg(M_ZB, WIDTH)
        ga_ref[...] = seg(M_GA, D_MODEL)
        gb_ref[...] = seg(M_GB, D_MODEL)
        flc_ref[...] = _mm32(h, wf_ref[...])

    row = lambda n: pl.BlockSpec((ts, n), lambda i: (i, 0))
    full = lambda a: pl.BlockSpec(a.shape, lambda i: (0,) * a.ndim)
    sds = jax.ShapeDtypeStruct
    return pl.pallas_call(
        body, name="proj_fwd", grid=(S // ts,),
        in_specs=[row(D_MODEL), full(shift), full(scale), full(g_norm), full(w_main), full(w_f)],
        out_specs=[row(WIDTH)] * 6 + [row(D_MODEL)] * 2 + [row(HEADS), row(D_MODEL)],
        out_shape=[sds((S, WIDTH), MXU_DTYPE)] * 3 + [sds((S, WIDTH), F32)] * 3 + [sds((S, D_MODEL), F32)] * 2
                  + [sds((S, HEADS), F32), sds((S, D_MODEL), MXU_DTYPE)],
        compiler_params=_params(("parallel",), VMEM_BIG),
    )(x, shift, scale, g_norm, w_main, w_f)


def _log_sigmoid(z):
    return jnp.minimum(z, 0.0) - jnp.log(1.0 + jnp.exp(-jnp.abs(z)))


def _fgate_bwd(dfc, flc, bf_row, ts):
    S = flc.shape[0]
    n = S // ts

    def body(df_ref, flc_ref, bfr_ref, dfl_ref, dbf_ref, carry):
        @pl.when(pl.program_id(0) == 0)
        def _():
            carry[...] = jnp.zeros_like(carry)
            dbf_ref[...] = jnp.zeros_like(dbf_ref)

        ri = lax.broadcasted_iota(jnp.int32, (ts, ts), 0)
        ci = lax.broadcasted_iota(jnp.int32, (ts, ts), 1)
        upper = (ci >= ri).astype(F32)
        rc = _mm32(upper, df_ref[...]) + carry[...]
        carry[...] = rc[0:1, :]
        z = flc_ref[...] + bfr_ref[...]
        dfl = rc * _sigmoid(-z)
        dfl_ref[...] = dfl
        dbf_ref[...] += jnp.sum(dfl, axis=0, keepdims=True)

    col = pl.BlockSpec((ts, HEADS), lambda i: (n - 1 - i, 0))
    one = pl.BlockSpec((1, HEADS), lambda i: (0, 0))
    return pl.pallas_call(
        body, name="fgate_bwd", grid=(n,),
        in_specs=[col, col, one], out_specs=[col, one],
        out_shape=[jax.ShapeDtypeStruct((S, HEADS), F32), jax.ShapeDtypeStruct((1, HEADS), F32)],
        scratch_shapes=[pltpu.VMEM((1, HEADS), F32)],
        compiler_params=_params(("arbitrary",)),
    )(dfc, flc, bf_row)


N_EXTRA = 3


def _attn_prep(q, k, v, flc, bf_row, t):
    S = q.shape[0]
    nb = S // t

    def body(q_ref, k_ref, v_ref, flc_ref, bfr_ref, qh_ref, kh_ref, vt_ref, carry):
        @pl.when(pl.program_id(0) == 0)
        def _():
            carry[...] = jnp.zeros_like(carry)

        ri = lax.broadcasted_iota(jnp.int32, (t, t), 0)
        ci = lax.broadcasted_iota(jnp.int32, (t, t), 1)
        f = _mm32((ci <= ri).astype(F32), _log_sigmoid(flc_ref[...] + bfr_ref[...])) + carry[...]
        carry[...] = f[t - 1:t, :]
        lane = lax.broadcasted_iota(jnp.int32, (t, 128), 1)
        for p in range(PAIRS):
            qp = q_ref[:, p * 128:(p + 1) * 128]
            kp = k_ref[:, p * 128:(p + 1) * 128]
            vt_ref[p, 0] = v_ref[:, p * 128:(p + 1) * 128].T
            for h in range(2):
                own = (lane < 64) if h == 0 else (lane >= 64)
                base = 64 if h == 0 else 0
                fh = f[:, 2 * p + h:2 * p + h + 1]
                parts = []
                rest = fh
                for _ in range(N_EXTRA):
                    part = rest.astype(qh_ref.dtype)
                    parts.append(part)
                    rest = rest - part.astype(F32)
                one = jnp.ones((t, 1), qh_ref.dtype)
                eq = jnp.zeros((t, 128), qh_ref.dtype)
                ek = jnp.zeros((t, 128), qh_ref.dtype)
                for j in range(N_EXTRA):
                    eq = jnp.where(lane == base + j, parts[j], eq)
                    eq = jnp.where(lane == base + N_EXTRA + j, one, eq)
                    ek = jnp.where(lane == base + j, one, ek)
                    ek = jnp.where(lane == base + N_EXTRA + j, -parts[j], ek)
                qh_ref[2 * p + h] = jnp.where(own, qp, eq)
                kh_ref[2 * p + h] = jnp.where(own, kp, ek)

    row = pl.BlockSpec((t, WIDTH), lambda i: (i, 0))
    heads = pl.BlockSpec((HEADS, t, 128), lambda i: (0, i, 0))
    return pl.pallas_call(
        body, name="attn_prep", grid=(nb,),
        in_specs=[row, row, row, pl.BlockSpec((t, HEADS), lambda i: (i, 0)), pl.BlockSpec((1, HEADS), lambda i: (0, 0))],
        out_specs=[heads, heads, pl.BlockSpec((PAIRS, 1, 128, t), lambda i: (0, i, 0, 0))],
        out_shape=[jax.ShapeDtypeStruct((HEADS, S, 128), q.dtype), jax.ShapeDtypeStruct((HEADS, S, 128), k.dtype),
                   jax.ShapeDtypeStruct((PAIRS, nb, 128, t), v.dtype)],
        scratch_shapes=[pltpu.VMEM((1, HEADS), F32)],
        compiler_params=_params(("arbitrary",), VMEM_MID),
    )(q, k, v, flc, bf_row)


def _attn_fwd(qh, kh, vt, t):
    S = qh.shape[1]
    nb = S // t

    def body(q_ref, k_ref, vt_ref, o_ref, lse_ref, acc_s):
        qi = pl.program_id(1)
        acc_s[...] = jnp.zeros_like(acc_s)

        def step(ki, nblk, masked, carry):
            m_old, l_old = carry[:2], carry[2:]
            ks = pl.multiple_of(ki * t, t)
            rows = nblk * t
            sts = [_mm_nt(k_ref[h, pl.ds(ks, rows), :], q_ref[h]) for h in range(2)]
            if masked:
                ri = lax.broadcasted_iota(jnp.int32, (t, t), 0)
                ci = lax.broadcasted_iota(jnp.int32, (t, t), 1)
                sts = [jnp.where(ci >= ri, st, NEG) for st in sts]
            m_new = [jnp.maximum(m_old[h], jnp.max(sts[h], axis=0, keepdims=True)) for h in range(2)]
            alpha = [jnp.exp(m_old[h] - m_new[h]) for h in range(2)]
            pts = [jnp.exp(sts[h] - m_new[h]) for h in range(2)]
            l_new = [alpha[h] * l_old[h] + jnp.sum(pts[h], axis=0, keepdims=True) for h in range(2)]
            for h in range(2):
                pv = _mm(vt_ref[ki], pts[h][:t])
                for b in range(1, nblk):
                    pv = pv + _mm(vt_ref[ki + b], pts[h][b * t:(b + 1) * t])
                acc_s[h] = alpha[h] * acc_s[h] + pv
            return (*m_new, *l_new)

        init = (jnp.full((1, t), -jnp.inf, F32),) * 2 + (jnp.zeros((1, t), F32),) * 2
        carry = lax.fori_loop(0, qi // 2, lambda j, c: step(2 * j, 2, False, c), init)
        carry = lax.cond(qi % 2 == 1, lambda c: step(qi - 1, 1, False, c), lambda c: c, carry)
        m0, m1, l0, l1 = step(qi, 1, True, carry)
        first = lax.broadcasted_iota(jnp.int32, (128, t), 0) < 64
        o_ref[...] = jnp.where(first, acc_s[0] / l0, acc_s[1] / l1).T
        lse_ref[...] = jnp.concatenate([m0 + jnp.log(l0), m1 + jnp.log(l1)], axis=0)

    return pl.pallas_call(
        body, name="attn_fwd", grid=(PAIRS, nb),
        in_specs=[pl.BlockSpec((2, t, 128), lambda p, i: (p, i, 0)), pl.BlockSpec((2, S, 128), lambda p, i: (p, 0, 0)),
                  pl.BlockSpec((None, nb, 128, t), lambda p, i: (p, 0, 0, 0))],
        out_specs=[pl.BlockSpec((t, 128), lambda p, i: (i, p)), pl.BlockSpec((None, None, 2, t), lambda p, i: (p, i, 0, 0))],
        out_shape=[jax.ShapeDtypeStruct((S, WIDTH), F32), jax.ShapeDtypeStruct((PAIRS, nb, 2, t), F32)],
        scratch_shapes=[pltpu.VMEM((2, 128, t), F32)],
        compiler_params=_params(("parallel", "parallel"), VMEM_MID),
    )(qh, kh, vt)


def _attn_bwd(qh, do, kh, v, lse4, dl4, t):
    S = qh.shape[1]
    nb = S // t

    def body(q_ref, do_ref, k_ref, v_ref, lse_ref, dl_ref,
             dq_ref, dk_ref, dv_ref, dfq_ref, dfk_ref, kc_s, vh_s, dk_s, dv_s, dfk_s):
        kj = pl.program_id(1)
        lane = lax.broadcasted_iota(jnp.int32, (t, 128), 1)
        is_a = lane < 64

        @pl.when(kj == 0)
        def _():
            dq_ref[...] = jnp.zeros_like(dq_ref)
            dfq_ref[...] = jnp.zeros_like(dfq_ref)

        vp = v_ref[...]
        zero = jnp.zeros_like(vp)
        kc_s[0] = jnp.where(is_a, k_ref[0], zero.astype(kc_s.dtype))
        kc_s[1] = jnp.where(is_a, zero.astype(kc_s.dtype), k_ref[1])
        vh_s[0] = jnp.where(is_a, vp, zero)
        vh_s[1] = jnp.where(is_a, zero, vp)
        dk_s[...] = jnp.zeros_like(dk_s)
        dv_s[...] = jnp.zeros_like(dv_s)
        dfk_s[...] = jnp.zeros_like(dfk_s)

        def step(qi, masked):
            qs = pl.multiple_of(qi * t, t)
            dob = do_ref[pl.ds(qs, t), :]
            lse = lse_ref[qi]
            dl = dl_ref[qi]
            zq = jnp.zeros_like(dob)
            over_keys = []
            for h in range(2):
                sel = is_a if h == 0 else jnp.logical_not(is_a)
                qb = q_ref[h, pl.ds(qs, t), :]
                st = _mm_nt(k_ref[h], qb) - lse[h:h + 1, :]
                if masked:
                    ri = lax.broadcasted_iota(jnp.int32, (t, t), 0)
                    ci = lax.broadcasted_iota(jnp.int32, (t, t), 1)
                    st = jnp.where(ci >= ri, st, NEG)
                pt = jnp.exp(st)
                dv_s[...] += _mm(pt, jnp.where(sel, dob, zq))
                dpt = _mm_nt(vh_s[h], dob)
                dst = pt * (dpt - dl[h:h + 1, :])
                dfk_s[h] += jnp.sum(dst, axis=1, keepdims=True)
                over_keys.append(jnp.sum(dst, axis=0, keepdims=True))
                dk_s[...] += _mm(dst, jnp.where(sel, qb, jnp.zeros_like(qb)))
                dq_ref[pl.ds(qs, t), :] += _mm_tn(dst, kc_s[h])
            dfq_ref[qi] += jnp.concatenate(over_keys, axis=0)

        step(kj, True)

        def loop_body(qi, carry):
            step(qi, False)
            return carry

        lax.fori_loop(kj + 1, nb, loop_body, 0)
        dk_ref[...] = dk_s[...].astype(dk_ref.dtype)
        dv_ref[...] = dv_s[...].astype(dv_ref.dtype)
        dfk_ref[...] = jnp.where(lax.broadcasted_iota(jnp.int32, (t, 2), 1) == 0, dfk_s[0], dfk_s[1])

        @pl.when(kj == nb - 1)
        def _():
            dq_ref[...] = dq_ref[...] * 0.125

    blk = pl.BlockSpec((t, 128), lambda p, j: (j, p))
    res = pl.BlockSpec((S, 128), lambda p, j: (0, p))
    rows4 = pl.BlockSpec((None, nb, 2, t), lambda p, j: (p, 0, 0, 0))
    cols4 = pl.BlockSpec((None, t, 2), lambda p, j: (p, j, 0))
    return pl.pallas_call(
        body, name="attn_bwd", grid=(PAIRS, nb),
        in_specs=[pl.BlockSpec((2, S, 128), lambda p, j: (p, 0, 0)), res,
                  pl.BlockSpec((2, t, 128), lambda p, j: (p, j, 0)), blk, rows4, rows4],
        out_specs=[res, blk, blk, rows4, cols4],
        out_shape=[jax.ShapeDtypeStruct((S, WIDTH), F32), jax.ShapeDtypeStruct((S, WIDTH), MXU_DTYPE),
                   jax.ShapeDtypeStruct((S, WIDTH), MXU_DTYPE), jax.ShapeDtypeStruct((PAIRS, nb, 2, t), F32),
                   jax.ShapeDtypeStruct((PAIRS, S, 2), F32)],
        scratch_shapes=[pltpu.VMEM((2, t, 128), kh.dtype), pltpu.VMEM((2, t, 128), v.dtype),
                        pltpu.VMEM((t, 128), F32), pltpu.VMEM((t, 128), F32), pltpu.VMEM((2, t, 1), F32)],
        compiler_params=_params(("parallel", "arbitrary"), VMEM_MID),
    )(qh, do, kh, v, lse4, dl4)


def _s5_mats(a_re, a_im, log_dt, b_re, b_im, c_re, c_im, d_skip):
    Lc = CHUNK
    dt = jnp.exp(log_dt)[:, None]
    lr, li = a_re * dt, a_im * dt

    def apow(n):
        n = jnp.asarray(n, F32)[None, :, None]
        mag = jnp.exp(n * lr[:, None, :])
        ang = n * li[:, None, :]
        return mag * jnp.cos(ang), mag * jnp.sin(ang)

    ar, ai = apow([1.0])
    ar, ai = ar[:, 0], ai[:, 0]
    den = a_re * a_re + a_im * a_im
    nr, ni = ar - 1.0, ai
    fr = (nr * a_re + ni * a_im) / den
    fi = (ni * a_re - nr * a_im) / den
    bbr = fr[:, :, None] * b_re - fi[:, :, None] * b_im
    bbi = fr[:, :, None] * b_im + fi[:, :, None] * b_re
    steps = np.arange(Lc, dtype=np.float32)
    pr, pi = apow(steps)
    car = c_re[:, None] * pr[:, :, None, :] - c_im[:, None] * pi[:, :, None, :]
    cai = c_re[:, None] * pi[:, :, None, :] + c_im[:, None] * pr[:, :, None, :]
    kern = (jnp.einsum('glcp,gpd->glcd', car, bbr, precision=HI)
            - jnp.einsum('glcp,gpd->glcd', cai, bbi, precision=HI))
    skip = d_skip.reshape(GROUPS, CG)[:, :, None] * jnp.eye(CG, dtype=F32)[None]
    kern = kern.at[:, 0].add(skip)
    trow = kern.transpose(0, 3, 1, 2).reshape(GROUPS, CG, Lc * CG)
    p1r, p1i = apow(steps + 1.0)
    cr = c_re[:, None] * p1r[:, :, None, :] - c_im[:, None] * p1i[:, :, None, :]
    ci = c_re[:, None] * p1i[:, :, None, :] + c_im[:, None] * p1r[:, :, None, :]
    to_rows = lambda m: m.transpose(0, 3, 1, 2).reshape(GROUPS, STATE, Lc * CG)
    camat = jnp.concatenate([to_rows(cr), -to_rows(ci)], axis=1)
    qr, qi = apow(Lc - 1.0 - steps)
    zr = qr[:, :, None, :] * bbr.transpose(0, 2, 1)[:, None] - qi[:, :, None, :] * bbi.transpose(0, 2, 1)[:, None]
    zi = qr[:, :, None, :] * bbi.transpose(0, 2, 1)[:, None] + qi[:, :, None, :] * bbr.transpose(0, 2, 1)[:, None]
    bzmat = jnp.concatenate([zr, zi], axis=-1).reshape(GROUPS, Lc * CG, 2 * STATE)
    lr_, li_ = apow([float(Lc)])
    al = jnp.concatenate([lr_[:, 0], li_[:, 0]], axis=-1)
    return trow, camat, bzmat, al


def _s5_scan_powers(a_re, a_im, log_dt, n_steps):
    dt = jnp.exp(log_dt)[:, None]
    lr, li = a_re * dt, a_im * dt
    n = (CHUNK * 2.0 ** np.arange(n_steps)).astype(np.float32)[None, :, None]
    mag = jnp.exp(n * lr[:, None, :])
    pr, pi = mag * jnp.cos(n * li[:, None, :]), mag * jnp.sin(n * li[:, None, :])
    fwd = jnp.stack([jnp.concatenate([pr, pr], -1), jnp.concatenate([-pi, pi], -1)], axis=2)
    bwd = jnp.stack([jnp.concatenate([pr, pr], -1), jnp.concatenate([pi, -pi], -1)], axis=2)
    return fwd, bwd


def _shift_rows(x, sh, down):
    n = x.shape[0]
    ri = lax.broadcasted_iota(jnp.int32, x.shape, 0)
    if down:
        return jnp.where(ri >= sh, pltpu.roll(x, sh, 0), 0.0)
    return jnp.where(ri < n - sh, pltpu.roll(x, n - sh, 0), 0.0)


GPB = 128 // CG


def _lane_transpose(arrs):
    lane = lax.broadcasted_iota(jnp.int32, arrs[0].shape, 1)
    arrs = list(arrs)
    k = GPB // 2
    while k >= 1:
        hi = ((lane // CG) & k) != 0
        new = list(arrs)
        for i in range(GPB):
            if i & k:
                continue
            lo_arr, hi_arr = arrs[i], arrs[i + k]
            new[i] = jnp.where(hi, pltpu.roll(hi_arr, CG * k, 1), lo_arr)
            new[i + k] = jnp.where(hi, hi_arr, pltpu.roll(lo_arr, 128 - CG * k, 1))
        arrs = new
        k //= 2
    return arrs


def _gather_block(ref, dst, nch):
    for half in range(CHUNK // GPB):
        outs = _lane_transpose([ref[pl.ds(half * GPB + l8, nch, stride=CHUNK), :] for l8 in range(GPB)])
        for g in range(GPB):
            dst[half, g] = outs[g]


def _scatter_block(src, ref, nch):
    for half in range(CHUNK // GPB):
        outs = _lane_transpose([src[half, g] for g in range(GPB)])
        for l8 in range(GPB):
            ref[pl.ds(half * GPB + l8, nch, stride=CHUNK), :] = outs[l8]


def _toeplitz(trow):
    lane = lax.broadcasted_iota(jnp.int32, (CG, 128), 1)
    x0, x1 = trow[:, :128], trow[:, 128:]
    zero = jnp.zeros_like(x0)
    rows = []
    for s in range(CHUNK):
        sh = (CG * s) % 128
        r0 = pltpu.roll(x0, sh, 1) if sh else x0
        r1 = pltpu.roll(x1, sh, 1) if sh else x1
        if CG * s < 128:
            rows.append(jnp.concatenate([jnp.where(lane >= sh, r0, zero), jnp.where(lane >= sh, r1, r0)], axis=1))
        else:
            rows.append(jnp.concatenate([zero, jnp.where(lane >= sh, r0, zero)], axis=1))
    return jnp.concatenate(rows, axis=0)


def _toeplitz_adjoint(dt):
    lane = lax.broadcasted_iota(jnp.int32, (CG, 128), 1)
    acc0 = jnp.zeros((CG, 128), F32)
    acc1 = jnp.zeros((CG, 128), F32)
    for s in range(CHUNK):
        x0, x1 = dt[CG * s:CG * s + CG, :128], dt[CG * s:CG * s + CG, 128:]
        sh = (CG * s) % 128
        keep = 128 - sh
        r0 = pltpu.roll(x0, keep, 1) if sh else x0
        r1 = pltpu.roll(x1, keep, 1) if sh else x1
        if CG * s < 128:
            acc0 = acc0 + jnp.where(lane < keep, r0, r1)
            acc1 = acc1 + jnp.where(lane < keep, r1, 0.0)
        else:
            acc0 = acc0 + jnp.where(lane < keep, r1, 0.0)
    return jnp.concatenate([acc0, acc1], axis=1)


def _s5_fwd(u, trow, camat, bzmat, pw):
    S = u.shape[0]
    nch = S // CHUNK
    n_steps = pw.shape[1]

    def body(u_ref, t_ref, ca_ref, bz_ref, pw_ref, y_ref, xp_ref, uc_ref, ub_s, yb_s):
        g = pl.program_id(1)

        @pl.when(g == 0)
        def _():
            _gather_block(u_ref, ub_s, nch)

        uc = jnp.concatenate([ub_s[0, g], ub_s[1, g]], axis=1)
        uc_ref[...] = uc
        x = _s5mm(uc, bz_ref[...])
        for kk in range(n_steps):
            xs = _shift_rows(x, 2 ** kk, True)
            m = pw_ref[kk]
            x = x + m[0:1, :] * xs + m[1:2, :] * pltpu.roll(xs, STATE, 1)
        xp = _shift_rows(x, 1, True)
        xp_ref[...] = xp
        yc = _s5mm(uc, _toeplitz(t_ref[...])) + _s5mm(xp, ca_ref[...])
        yb_s[0, g] = yc[:, :128]
        yb_s[1, g] = yc[:, 128:]

        @pl.when(g == GPB - 1)
        def _():
            _scatter_block(yb_s, y_ref, nch)

    per = lambda a: pl.BlockSpec((None,) + a.shape[1:], lambda b, g: (b * GPB + g,) + (0,) * (a.ndim - 1))
    nat = pl.BlockSpec((S, 128), lambda b, g: (0, b))
    return pl.pallas_call(
        body, name="s5_fwd", grid=(GROUPS // GPB, GPB),
        in_specs=[nat, per(trow), per(camat), per(bzmat), per(pw)],
        out_specs=[nat, pl.BlockSpec((None, nch, 2 * STATE), lambda b, g: (b * GPB + g, 0, 0)),
                   pl.BlockSpec((None, nch, CHUNK * CG), lambda b, g: (b * GPB + g, 0, 0))],
        out_shape=[jax.ShapeDtypeStruct((S, GROUPS * CG), F32), jax.ShapeDtypeStruct((GROUPS, nch, 2 * STATE), F32),
                   jax.ShapeDtypeStruct((GROUPS, nch, CHUNK * CG), F32)],
        scratch_shapes=[pltpu.VMEM((CHUNK // GPB, GPB, nch, 128), F32)] * 2,
        compiler_params=_params(("parallel", "arbitrary"), VMEM_BIG),
    )(u, trow, camat, bzmat, pw)


def _s5_bwd(uc, dy, xp, trow, camat, bzmat, pwc):
    S = dy.shape[0]
    nch = S // CHUNK
    n_steps = pwc.shape[1]

    def body(uc_ref, dy_ref, xp_ref, t_ref, ca_ref, bz_ref, pw_ref, du_ref, dt_ref, dca_ref, dbz_ref, dal_ref,
             dyb_s, dub_s):
        g = pl.program_id(1)

        @pl.when(g == 0)
        def _():
            _gather_block(dy_ref, dyb_s, nch)

        uc = uc_ref[...]
        dyc = jnp.concatenate([dyb_s[0, g], dyb_s[1, g]], axis=1)
        xpv = xp_ref[...]
        dt_ref[...] = _toeplitz_adjoint(_s5mm_tn(uc, dyc))
        dca_ref[...] = _s5mm_tn(xpv, dyc)
        dx = _shift_rows(_s5mm_nt(dyc, ca_ref[...]), 1, False)
        for kk in range(n_steps):
            xs = _shift_rows(dx, 2 ** kk, False)
            m = pw_ref[kk]
            dx = dx + m[0:1, :] * xs + m[1:2, :] * pltpu.roll(xs, STATE, 1)
        dbz_ref[...] = _s5mm_tn(uc, dx)
        dal_ref[0:1, :] = jnp.sum(dx * xpv, axis=0, keepdims=True)
        dal_ref[1:2, :] = jnp.sum(dx * pltpu.roll(xpv, STATE, 1), axis=0, keepdims=True)
        duc = _s5mm_nt(dyc, _toeplitz(t_ref[...])) + _s5mm_nt(dx, bz_ref[...])
        dub_s[0, g] = duc[:, :128]
        dub_s[1, g] = duc[:, 128:]

        @pl.when(g == GPB - 1)
        def _():
            _scatter_block(dub_s, du_ref, nch)

    per = lambda a: pl.BlockSpec((None,) + a.shape[1:], lambda b, g: (b * GPB + g,) + (0,) * (a.ndim - 1))
    nat = pl.BlockSpec((S, 128), lambda b, g: (0, b))
    sds = jax.ShapeDtypeStruct
    mats = [sds(trow.shape, F32), sds(camat.shape, F32), sds(bzmat.shape, F32), sds((GROUPS, 2, 2 * STATE), F32)]
    return pl.pallas_call(
        body, name="s5_bwd", grid=(GROUPS // GPB, GPB),
        in_specs=[per(uc), nat, per(xp), per(trow), per(camat), per(bzmat), per(pwc)],
        out_specs=[nat] + [per(o) for o in mats], out_shape=[sds((S, GROUPS * CG), F32)] + mats,
        scratch_shapes=[pltpu.VMEM((CHUNK // GPB, GPB, nch, 128), F32)] * 2,
        compiler_params=_params(("parallel", "arbitrary"), VMEM_BIG),
    )(uc, dy, xp, trow, camat, bzmat, pwc)


GELU_C0 = math.sqrt(2.0 / math.pi)
GELU_C1 = 0.044715


def _mix(o, za, ys, zb, ga, gb, x, tgt, gate, b_glu, g_final, w_glu, w_up_a, w_up_b, w_out, hsel, ts):
    S = o.shape[0]

    def body(o_ref, za_ref, ys_ref, zb_ref, ga_ref, gb_ref, x_ref, t_ref, gate_ref, bglu_ref, gf_ref,
             wglu_ref, wua_ref, wub_ref, wout_ref, hsel_ref,
             dx2_ref, do_ref, dza_ref, dzb_ref, dga_ref, dgb_ref, dys_ref, dl_ref,
             mg_ref, dmo_ref, ya_ref, dua_ref, yb_ref, dub_ref, yg_ref, dgl_ref,
             dbglu_ref, dgate_ref, dgf_ref, loss_ref):
        @pl.when(pl.program_id(0) == 0)
        def _():
            dbglu_ref[...] = jnp.zeros_like(dbglu_ref)
            dgate_ref[...] = jnp.zeros_like(dgate_ref)
            dgf_ref[...] = jnp.zeros_like(dgf_ref)
            loss_ref[...] = jnp.zeros_like(loss_ref)

        ov = o_ref[...]
        za = za_ref[...]
        sza = _sigmoid(za)
        silu_a = za * sza
        ya = ov * silu_a
        ya_b = ya.astype(ya_ref.dtype)
        ya_ref[...] = ya_b
        ysv = ys_ref[...]
        th = jnp.tanh(GELU_C0 * (ysv + GELU_C1 * ysv * ysv * ysv))
        yg = 0.5 * ysv * (1.0 + th)
        yg_b = yg.astype(yg_ref.dtype)
        yg_ref[...] = yg_b
        sg = _sigmoid(_mm(yg_b, wglu_ref[...]) + bglu_ref[...])
        yb1 = yg * sg
        zb = zb_ref[...]
        szb = _sigmoid(zb)
        silu_b = zb * szb
        yb_b = (yb1 * silu_b).astype(yb_ref.dtype)
        yb_ref[...] = yb_b
        ua = _mm(ya_b, wua_ref[...])
        ub = _mm(yb_b, wub_ref[...])
        sa = _sigmoid(ga_ref[...])
        sb = _sigmoid(gb_ref[...])
        merged_b = (sa * ua + sb * ub).astype(mg_ref.dtype)
        mg_ref[...] = merged_b
        mo = _mm(merged_b, wout_ref[...])
        gate_v = gate_ref[...]
        x2 = x_ref[...] + gate_v * mo
        r2 = lax.rsqrt(jnp.mean(x2 * x2, axis=-1, keepdims=True) + EPS)
        x2n = x2 * r2
        gf = gf_ref[...]
        diff = x2n * gf - t_ref[...]
        loss_ref[...] += jnp.sum(jnp.sum(diff * diff, axis=-1, keepdims=True), axis=0, keepdims=True) * (0.5 / D_MODEL)
        dy = diff * (1.0 / D_MODEL)
        dgf_ref[...] += jnp.sum(dy * x2n, axis=0, keepdims=True)
        dyg = dy * gf
        dx2 = r2 * (dyg - x2n * jnp.mean(dyg * x2n, axis=-1, keepdims=True))
        dx2_ref[...] = dx2
        dgate_ref[...] += jnp.sum(dx2 * mo, axis=0, keepdims=True)
        dmo_b = (dx2 * gate_v).astype(dmo_ref.dtype)
        dmo_ref[...] = dmo_b
        dmerged = _mm_nt(dmo_b, wout_ref[...])
        dua_b = (dmerged * sa).astype(dua_ref.dtype)
        dub_b = (dmerged * sb).astype(dub_ref.dtype)
        dua_ref[...] = dua_b
        dub_ref[...] = dub_b
        dga_ref[...] = (dmerged * ua * sa * (1.0 - sa)).astype(dga_ref.dtype)
        dgb_ref[...] = (dmerged * ub * sb * (1.0 - sb)).astype(dgb_ref.dtype)
        dya = _mm_nt(dua_b, wua_ref[...])
        dyb = _mm_nt(dub_b, wub_ref[...])
        dov = dya * silu_a
        do_ref[...] = dov.astype(do_ref.dtype)
        dl_ref[...] = _mm32_nt(hsel_ref[...], dov * ov)
        dza_ref[...] = (dya * ov * (sza * (1.0 + za * (1.0 - sza)))).astype(dza_ref.dtype)
        dyb1 = dyb * silu_b
        dzb_ref[...] = (dyb * yb1 * (szb * (1.0 + zb * (1.0 - szb)))).astype(dzb_ref.dtype)
        dgl = dyb1 * yg * sg * (1.0 - sg)
        dbglu_ref[...] += jnp.sum(dgl, axis=0, keepdims=True)
        dgl_b = dgl.astype(dgl_ref.dtype)
        dgl_ref[...] = dgl_b
        dyg2 = dyb1 * sg + _mm_nt(dgl_b, wglu_ref[...])
        dgelu = 0.5 * (1.0 + th) + 0.5 * ysv * (1.0 - th * th) * GELU_C0 * (1.0 + 3.0 * GELU_C1 * ysv * ysv)
        dys_ref[...] = dyg2 * dgelu

    row = lambda n: pl.BlockSpec((ts, n), lambda i: (i, 0))
    full = lambda a: pl.BlockSpec(a.shape, lambda i: (0,) * a.ndim)
    vec = lambda n: pl.BlockSpec((1, n), lambda i: (0, 0))
    sds = jax.ShapeDtypeStruct
    W, Dm = WIDTH, D_MODEL
    return pl.pallas_call(
        body, name="mix", grid=(S // ts,),
        in_specs=[row(W), row(W), row(W), row(W), row(Dm), row(Dm), row(Dm), row(Dm),
                  full(gate), full(b_glu), full(g_final), full(w_glu), full(w_up_a), full(w_up_b), full(w_out), full(hsel)],
        out_specs=[row(Dm), row(W), row(W), row(W), row(Dm), row(Dm), row(W), pl.BlockSpec((HEADS, ts), lambda i: (0, i)),
                   row(Dm), row(Dm), row(W), row(Dm), row(W), row(Dm), row(W), row(W),
                   vec(W), vec(Dm), vec(Dm), vec(1)],
        out_shape=[sds((S, Dm), F32), sds((S, W), MXU_DTYPE), sds((S, W), MXU_DTYPE), sds((S, W), MXU_DTYPE),
                   sds((S, Dm), MXU_DTYPE), sds((S, Dm), MXU_DTYPE), sds((S, W), F32), sds((HEADS, S), F32),
                   sds((S, Dm), MXU_DTYPE), sds((S, Dm), MXU_DTYPE), sds((S, W), MXU_DTYPE), sds((S, Dm), MXU_DTYPE),
                   sds((S, W), MXU_DTYPE), sds((S, Dm), MXU_DTYPE), sds((S, W), MXU_DTYPE), sds((S, W), MXU_DTYPE),
                   sds((1, W), F32), sds((1, Dm), F32), sds((1, Dm), F32), sds((1, 1), F32)],
        compiler_params=_params(("arbitrary",), VMEM_BIG),
    )(o, za, ys, zb, ga, gb, x, tgt, gate, b_glu, g_final, w_glu, w_up_a, w_up_b, w_out, hsel)


def _matmul_tn(name, a, b, ts):
    S, M = a.shape
    N = b.shape[1]
    tn = min(N, 1024)

    def body(a_ref, b_ref, o_ref):
        @pl.when(pl.program_id(1) == 0)
        def _():
            o_ref[...] = jnp.zeros_like(o_ref)

        o_ref[...] += _mm_tn(a_ref[...], b_ref[...])

    return pl.pallas_call(
        body, name=name, grid=(N // tn, S // ts),
        in_specs=[pl.BlockSpec((ts, M), lambda j, i: (i, 0)), pl.BlockSpec((ts, tn), lambda j, i: (i, j))],
        out_specs=pl.BlockSpec((M, tn), lambda j, i: (0, j)),
        out_shape=jax.ShapeDtypeStruct((M, N), F32),
        compiler_params=_params(("parallel", "arbitrary"), VMEM_MID),
    )(a, b)


def _proj_bwd(dq, dk, dv, dza, du, dzb, dga, dgb, dfl, x, dx2, shift, scale, g_norm, w_main, w_ft, ts):
    S = x.shape[0]

    def body(dq_ref, dk_ref, dv_ref, dza_ref, du_ref, dzb_ref, dga_ref, dgb_ref, dfl_ref, x_ref, dx2_ref,
             sc_ref, gn_ref, w_ref, wft_ref, gx_ref, dsh_ref, dsc_ref, dgn_ref):
        @pl.when(pl.program_id(0) == 0)
        def _():
            dsh_ref[...] = jnp.zeros_like(dsh_ref)
            dsc_ref[...] = jnp.zeros_like(dsc_ref)
            dgn_ref[...] = jnp.zeros_like(dgn_ref)

        def seg(ref, off, n):
            return _mm_nt(ref[...], w_ref[:, off:off + n])

        dh = (seg(dq_ref, M_Q, WIDTH) + seg(dk_ref, M_K, WIDTH) + seg(dv_ref, M_V, WIDTH)
              + seg(dza_ref, M_ZA, WIDTH) + seg(du_ref, M_U, WIDTH) + seg(dzb_ref, M_ZB, WIDTH)
              + seg(dga_ref, M_GA, D_MODEL) + seg(dgb_ref, M_GB, D_MODEL)
              + _mm32(dfl_ref[...], wft_ref[...]))
        xv = x_ref[...]
        r = lax.rsqrt(jnp.mean(xv * xv, axis=-1, keepdims=True) + EPS)
        xn = xv * r
        gn = gn_ref[...]
        s1 = 1.0 + sc_ref[...]
        dsh_ref[...] += jnp.sum(dh, axis=0, keepdims=True)
        dhx = dh * xn
        dsc_ref[...] += jnp.sum(dhx, axis=0, keepdims=True) * gn
        dgn_ref[...] += jnp.sum(dhx, axis=0, keepdims=True) * s1
        dxn = dh * (gn * s1)
        gx_ref[...] = dx2_ref[...] + r * (dxn - xn * jnp.mean(dxn * xn, axis=-1, keepdims=True))

    row = lambda n: pl.BlockSpec((ts, n), lambda i: (i, 0))
    full = lambda a: pl.BlockSpec(a.shape, lambda i: (0,) * a.ndim)
    vec = pl.BlockSpec((1, D_MODEL), lambda i: (0, 0))
    W, Dm = WIDTH, D_MODEL
    del shift
    return pl.pallas_call(
        body, name="proj_bwd", grid=(S // ts,),
        in_specs=[row(W)] * 6 + [row(Dm)] * 2 + [row(HEADS), row(Dm), row(Dm),
                                                 full(scale), full(g_norm), full(w_main), full(w_ft)],
        out_specs=[row(Dm), vec, vec, vec],
        out_shape=[jax.ShapeDtypeStruct((S, Dm), F32)] + [jax.ShapeDtypeStruct((1, Dm), F32)] * 3,
        compiler_params=_params(("arbitrary",), VMEM_BIG),
    )(dq, dk, dv, dza, du, dzb, dga, dgb, dfl, x, dx2, scale, g_norm, w_main, w_ft)


def _adamw(name, planes, w, m, v, tr):
    n, R, C = planes.shape
    bc1 = 1.0 - ADAM_B1 ** ADAM_STEP
    bc2 = 1.0 - ADAM_B2 ** ADAM_STEP

    def body(p_ref, w_ref, m_ref, v_ref, g_ref, d_ref, nm_ref, nv_ref):
        g = p_ref[0].astype(F32)
        for i in range(1, n):
            g = g + p_ref[i].astype(F32)
        g_ref[...] = g
        nm = ADAM_B1 * m_ref[...] + (1.0 - ADAM_B1) * g
        nv = ADAM_B2 * v_ref[...] + (1.0 - ADAM_B2) * (g * g)
        nm_ref[...] = nm
        nv_ref[...] = nv
        d_ref[...] = -ADAM_LR * ((nm / bc1) / (jnp.sqrt(nv / bc2) + ADAM_EPS) + ADAM_WD * w_ref[...])

    blk = pl.BlockSpec((tr, C), lambda i: (i, 0))
    return pl.pallas_call(
        body, name=name, grid=(R // tr,),
        in_specs=[pl.BlockSpec((n, tr, C), lambda i: (0, i, 0)), blk, blk, blk],
        out_specs=[blk] * 4, out_shape=[jax.ShapeDtypeStruct((R, C), F32)] * 4,
        compiler_params=_params(("parallel",), VMEM_MID),
    )(planes, w, m, v)


def _wada_grad(c_all, dmod_cols):
    def body(c_ref, d_ref, o_ref):
        o_ref[0] = _mm32_tn(c_ref[...], d_ref[...])

    return pl.pallas_call(
        body, name="wada_grad",
        out_shape=jax.ShapeDtypeStruct((1, c_all.shape[1], dmod_cols.shape[1]), F32),
        in_specs=[VMEM, VMEM], out_specs=VMEM,
    )(c_all, dmod_cols)


SMALL_ORDER = ("b_ada", "g_norm", "b_f", "a_re", "a_im", "log_dt", "b_re", "b_im", "c_re", "c_im",
               "d_skip", "b_glu", "g_final")
BIG_ORDER = ("w_ada", "w_in", "w_glu", "w_up_a", "w_up_b", "w_out")
ALL_ORDER = ("w_ada", "b_ada", "g_norm", "w_in", "b_f", "a_re", "a_im", "log_dt", "b_re", "b_im", "c_re", "c_im",
             "d_skip", "w_glu", "b_glu", "w_up_a", "w_up_b", "w_out", "g_final")


def _pack_small(parts, rows):
    flat = jnp.concatenate([p.reshape(-1).astype(F32) for p in parts])
    return jnp.pad(flat, (0, rows * 128 - flat.shape[0])).reshape(rows, 128)


def kernel(x, c, w_ada, b_ada, g_norm, w_in, b_f, a_re, a_im, log_dt, b_re, b_im, c_re, c_im, d_skip, w_glu, b_glu, w_up_a, w_up_b, w_out, g_final, loss_target, m_w_ada, m_b_ada, m_g_norm, m_w_in, m_b_f, m_a_re, m_a_im, m_log_dt, m_b_re, m_b_im, m_c_re, m_c_im, m_d_skip, m_w_glu, m_b_glu, m_w_up_a, m_w_up_b, m_w_out, m_g_final, v_w_ada, v_b_ada, v_g_norm, v_w_in, v_b_f, v_a_re, v_a_im, v_log_dt, v_b_re, v_b_im, v_c_re, v_c_im, v_d_skip, v_w_glu, v_b_glu, v_w_up_a, v_w_up_b, v_w_out, v_g_final):
    weights = dict(w_ada=w_ada, b_ada=b_ada, g_norm=g_norm, w_in=w_in, b_f=b_f, a_re=a_re, a_im=a_im, log_dt=log_dt,
                   b_re=b_re, b_im=b_im, c_re=c_re, c_im=c_im, d_skip=d_skip, w_glu=w_glu, b_glu=b_glu,
                   w_up_a=w_up_a, w_up_b=w_up_b, w_out=w_out, g_final=g_final)
    mom_m = dict(w_ada=m_w_ada, b_ada=m_b_ada, g_norm=m_g_norm, w_in=m_w_in, b_f=m_b_f, a_re=m_a_re, a_im=m_a_im,
                 log_dt=m_log_dt, b_re=m_b_re, b_im=m_b_im, c_re=m_c_re, c_im=m_c_im, d_skip=m_d_skip, w_glu=m_w_glu,
                 b_glu=m_b_glu, w_up_a=m_w_up_a, w_up_b=m_w_up_b, w_out=m_w_out, g_final=m_g_final)
    mom_v = dict(w_ada=v_w_ada, b_ada=v_b_ada, g_norm=v_g_norm, w_in=v_w_in, b_f=v_b_f, a_re=v_a_re, a_im=v_a_im,
                 log_dt=v_log_dt, b_re=v_b_re, b_im=v_b_im, c_re=v_c_re, c_im=v_c_im, d_skip=v_d_skip, w_glu=v_w_glu,
                 b_glu=v_b_glu, w_up_a=v_w_up_a, w_up_b=v_w_up_b, w_out=v_w_out, g_final=v_g_final)
    xs = x[0]
    tgt = loss_target[0]
    S = xs.shape[0]
    ts = min(256, S)
    ta = min(512, S)
    tw = min(2048, S)
    nch = S // CHUNK
    n_steps = max(1, int(math.ceil(math.log2(nch))))
    me = _my_index()

    shards = [w.astype(MXU_DTYPE) for w in (w_in[0], w_glu[0], w_up_a[0], w_up_b[0], w_out[0])]
    mod8, c_all, gathered = _comm_in(c, w_ada[0], b_ada.reshape(N_DEV, -1), shards)
    mod = mod8.reshape(1, 3 * D_MODEL)
    shift, scale, gate = mod[:, :D_MODEL], mod[:, D_MODEL:2 * D_MODEL], mod[:, 2 * D_MODEL:]
    w_in_full = gathered[0].transpose(1, 0, 2).reshape(D_MODEL, PROJ_WIDTH)
    w_main = jnp.concatenate([w_in_full[:, :OFF_F], w_in_full[:, OFF_F + HEADS:]], axis=1)
    w_f = w_in_full[:, OFF_F:OFF_F + HEADS].astype(F32)
    w_ft = w_f.T
    w_glu_full = gathered[1].reshape(WIDTH, WIDTH)
    w_up_a_full = gathered[2].transpose(1, 0, 2).reshape(WIDTH, D_MODEL)
    w_up_b_full = gathered[3].transpose(1, 0, 2).reshape(WIDTH, D_MODEL)
    w_out_full = gathered[4].reshape(D_MODEL, D_MODEL)

    q, k, v, za, u, zb, ga, gb, flc, hb = _proj_fwd(xs, shift, scale, g_norm, w_main, w_f, ts)
    nb = S // ta
    rows4 = lambda r: r.reshape(PAIRS, 2, nb, ta).transpose(0, 2, 1, 3)
    qh, kh, vt = _attn_prep(q, k, v, flc, b_f, ta)
    o, lse4 = _attn_fwd(qh, kh, vt, ta)

    s5_params = (a_re[0], a_im[0], log_dt[0], b_re[0], b_im[0], c_re[0], c_im[0], d_skip[0])
    (trow, camat, bzmat, al), mats_vjp = jax.vjp(_s5_mats, *s5_params)
    del al
    pw_f, pw_b = _s5_scan_powers(a_re[0], a_im[0], log_dt[0], n_steps)
    ys, xprev, uc = _s5_fwd(u, trow, camat, bzmat, pw_f)

    hsel = (np.arange(WIDTH)[None, :] // 64 == np.arange(HEADS)[:, None]).astype(np.float32)
    (dx2, do, dza, dzb, dga, dgb, dys, dl_row, merged, dmo, ya, dua, yb, dub, yg, dgl,
     db_glu, dgate, dg_final, loss_part) = _mix(o, za, ys, zb, ga, gb, xs, tgt, gate, b_glu, g_final.reshape(1, -1),
                                                w_glu_full, w_up_a_full, w_up_b_full, w_out_full, jnp.asarray(hsel), ts)

    gw_out = _matmul_tn("dw_out", merged, dmo, tw)
    gw_up_a = _matmul_tn("dw_up_a", ya, dua, tw)
    gw_up_b = _matmul_tn("dw_up_b", yb, dub, tw)
    gw_glu = _matmul_tn("dw_glu", yg, dgl, tw)

    du, d_trow, d_camat, d_bzmat, dal2 = _s5_bwd(uc, dys, xprev, trow, camat, bzmat, pw_b)
    d_al = jnp.concatenate([dal2[:, 0, :STATE] + dal2[:, 0, STATE:], dal2[:, 1, STATE:] - dal2[:, 1, :STATE]], axis=-1)
    gs5 = mats_vjp((d_trow, d_camat, d_bzmat, d_al))

    dl4 = rows4(dl_row)
    dq, dk, dv, dfq4, dfk4 = _attn_bwd(qh, do, kh, v, lse4, dl4, ta)
    d_fcol = dfq4.transpose(0, 2, 1, 3).reshape(HEADS, S).T - dfk4.transpose(1, 0, 2).reshape(S, HEADS)
    dfl, db_f = _fgate_bwd(d_fcol, flc, b_f, ta)

    grad_x, dshift, dscale, dg_norm = _proj_bwd(dq, dk, dv, dza, du, dzb, dga, dgb, dfl, xs, dx2,
                                                shift, scale, g_norm, w_main, w_ft, ts)
    segs = [("dw_q", dq), ("dw_k", dk), ("dw_v", dv), ("dw_f", dfl), ("dw_za", dza), ("dw_u", du), ("dw_zb", dzb),
            ("dw_ga", dga), ("dw_gb", dgb)]
    gw_in = jnp.concatenate([_matmul_tn(nm, hb, d, tw) for nm, d in segs], axis=1)

    planes = [gw_in.reshape(D_MODEL, N_DEV, -1).transpose(1, 0, 2),
              gw_glu.reshape(N_DEV, -1, WIDTH),
              gw_up_a.reshape(WIDTH, N_DEV, -1).transpose(1, 0, 2),
              gw_up_b.reshape(WIDTH, N_DEV, -1).transpose(1, 0, 2),
              gw_out.reshape(N_DEV, -1, D_MODEL)]
    planes = [p.astype(MXU_DTYPE) for p in planes]
    dmod = jnp.concatenate([dshift, dscale, dgate], axis=1)
    small_parts = [dmod, dg_norm, db_f, gs5[0], gs5[1], gs5[2], gs5[3], gs5[4], gs5[5], gs5[6], gs5[7],
                   db_glu, dg_final, loss_part]
    n_small = sum(int(np.prod(p.shape)) for p in small_parts)
    rows = -(-n_small // (8 * 128)) * 8
    small = _pack_small(small_parts, rows)
    from_sib = _comm_pair(planes)
    core = lax.axis_index("c")
    chip_planes = []
    for name, p, s in zip(("w_in", "w_glu", "w_up_a", "w_up_b", "w_out"), planes, from_sib):
        tr = 256 if s.shape[1] % 256 == 0 else s.shape[1]
        chip_planes.append(_pair_sum("pair_sum_" + name, p, s, core, tr))
    recv, small_all = _comm_out(chip_planes, small)

    grads, deltas, new_m, new_v = {}, {}, {}, {}

    def put(name, res, shape):
        grads[name], deltas[name], new_m[name], new_v[name] = [r.reshape(shape) for r in res]

    names = ("w_in", "w_glu", "w_up_a", "w_up_b", "w_out")
    for name, pr in zip(names, recv):
        w2 = weights[name][0]
        tr = 256 if w2.shape[0] % 256 == 0 else w2.shape[0]
        put(name, _adamw("adamw_" + name, pr, w2, mom_m[name][0], mom_v[name][0], tr), weights[name].shape)
    cols = w_ada.shape[2]
    dmod_all = small_all[:, :24, :].reshape(N_DEV, 3 * D_MODEL)
    dmod_cols = lax.dynamic_slice_in_dim(dmod_all, me * cols, cols, axis=1)
    g_wada = _wada_grad(c_all, dmod_cols)
    put("w_ada", _adamw("adamw_w_ada", g_wada, w_ada[0], m_w_ada[0], v_w_ada[0], 256), w_ada.shape)
    pack = lambda d: _pack_small([d[n] for n in SMALL_ORDER] + [jnp.zeros((1,), F32)], rows)
    res_small = _adamw("adamw_small", small_all, pack(weights), pack(mom_m), pack(mom_v), rows)
    flat = [r.reshape(-1) for r in res_small]
    off = 0
    for name in SMALL_ORDER:
        shape = weights[name].shape
        size = int(np.prod(shape))
        put(name, [f[off:off + size] for f in flat], shape)
        off += size
    loss = flat[0][off]

    return (loss, grad_x[None], *[grads[n] for n in ALL_ORDER], *[deltas[n] for n in ALL_ORDER],
            *[new_m[n] for n in ALL_ORDER], *[new_v[n] for n in ALL_ORDER])
```

```python
import functools
import math

import jax
import jax.numpy as jnp
import numpy as np
from jax import lax
from jax.experimental import pallas as pl
from jax.experimental.pallas import tpu as pltpu

F32 = jnp.float32
MXU_DTYPE = jnp.bfloat16
HI = lax.Precision.HIGHEST

N_DEV = 8
D_MODEL = 1024
WIDTH = 512
HEADS = 8
PAIRS = HEADS // 2
GROUPS = 32
STATE = 64
CG = 16
CHUNK = 16
EPS = 1e-6
NEG = float(np.finfo(np.float32).min)

ADAM_LR = 0.001
ADAM_B1 = 0.9
ADAM_B2 = 0.999
ADAM_EPS = 1e-08
ADAM_WD = 0.01
ADAM_STEP = 10

VMEM_BIG = 56 * 1024 * 1024
VMEM_MID = 40 * 1024 * 1024

OFF_F = 3 * WIDTH
PROJ_WIDTH = 5128
M_Q, M_K, M_V, M_ZA, M_U, M_ZB, M_GA, M_GB = 0, 512, 1024, 1536, 2048, 2560, 3072, 4096


def _mm(a, b):
    return jnp.dot(a.astype(MXU_DTYPE), b.astype(MXU_DTYPE), preferred_element_type=F32)


def _mm_nt(a, b):
    return lax.dot_general(a.astype(MXU_DTYPE), b.astype(MXU_DTYPE), (((1,), (1,)), ((), ())),
                           preferred_element_type=F32)


def _mm_tn(a, b):
    return lax.dot_general(a.astype(MXU_DTYPE), b.astype(MXU_DTYPE), (((0,), (0,)), ((), ())),
                           preferred_element_type=F32)


def _mm32(a, b):
    return jnp.dot(a, b, precision=HI, preferred_element_type=F32)


def _mm32_nt(a, b):
    return lax.dot_general(a, b, (((1,), (1,)), ((), ())), precision=HI, preferred_element_type=F32)


def _mm32_tn(a, b):
    return lax.dot_general(a, b, (((0,), (0,)), ((), ())), precision=HI, preferred_element_type=F32)


S5_PRECISION = lax.Precision.HIGH


def _s5mm(a, b):
    return jnp.dot(a, b, precision=S5_PRECISION, preferred_element_type=F32)


def _s5mm_nt(a, b):
    return lax.dot_general(a, b, (((1,), (1,)), ((), ())), precision=S5_PRECISION, preferred_element_type=F32)


def _s5mm_tn(a, b):
    return lax.dot_general(a, b, (((0,), (0,)), ((), ())), precision=S5_PRECISION, preferred_element_type=F32)


def _sigmoid(x):
    return 1.0 / (1.0 + jnp.exp(-x))


def _params(sem=None, vmem=None):
    kw = {}
    if sem is not None:
        kw["dimension_semantics"] = sem
    if vmem is not None:
        kw["vmem_limit_bytes"] = vmem
    return pltpu.CompilerParams(**kw)


def _my_index():
    return 4 * lax.axis_index("x") + 2 * lax.axis_index("y") + lax.axis_index("c")


def _dev(p):
    return (p // 4, (p // 2) % 2, p % 2)


ANY = pl.BlockSpec(memory_space=pl.ANY)
VMEM = pl.BlockSpec(memory_space=pltpu.VMEM)
MESH = pl.DeviceIdType.MESH


def _comm_in(c, w_ada, b_ada8, shards):
    n = len(shards)
    cols = w_ada.shape[1]

    def body(c_ref, wada_ref, bada_ref, *rest):
        srcs = rest[:n]
        mod_ref, call_ref = rest[n], rest[n + 1]
        dsts = rest[n + 2:2 * n + 2]
        modp, wsend, wrecv, wloc, csend, crecv, msend, mrecv = rest[2 * n + 2:]
        me = _my_index()

        x, y, cc = lax.axis_index("x"), lax.axis_index("y"), lax.axis_index("c")
        sib = (x, y, 1 - cc)
        chips = [(1 - x, y), (x, 1 - y), (1 - x, 1 - y)]

        def wcopy(a, k, block, to, src=None):
            ref = dsts[a].at[4 * block[0] + 2 * block[1] + block[2]]
            return pltpu.make_async_remote_copy(src_ref=ref if src is None else src, dst_ref=ref,
                                                send_sem=wsend.at[a, k], recv_sem=wrecv.at[a, k],
                                                device_id=to, device_id_type=MESH)

        def ccopy(src_dev, d, to):
            return pltpu.make_async_remote_copy(src_ref=c_ref, dst_ref=call_ref.at[pl.ds(src_dev, 1)],
                                                send_sem=csend.at[d], recv_sem=crecv.at[src_dev],
                                                device_id=_dev(to), device_id_type=MESH)

        def mcopy(src_dev, d, to):
            return pltpu.make_async_remote_copy(src_ref=modp.at[pl.ds(to, 1)], dst_ref=mod_ref.at[pl.ds(src_dev, 1)],
                                                send_sem=msend.at[d], recv_sem=mrecv.at[src_dev],
                                                device_id=_dev(to), device_id_type=MESH)

        local = [pltpu.make_async_copy(srcs[a], dsts[a].at[me], wloc.at[a]) for a in range(n)]
        for cp in local:
            cp.start()
        peers = [(me + d) % N_DEV for d in range(1, N_DEV)]
        first = []
        for a in range(n):
            first.append(wcopy(a, 0, (x, y, cc), sib, src=srcs[a]))
            first += [wcopy(a, 1 + j, (x, y, cc), (*chip, cc), src=srcs[a]) for j, chip in enumerate(chips)]
        for cp in first:
            cp.start()
        call_ref[pl.ds(me, 1), :] = c_ref[...]
        for d, p in enumerate(peers):
            ccopy(me, d, p).start()
        for d, p in enumerate(peers):
            ccopy(p, d, p).wait_recv()
        modp[...] = _mm32(call_ref[...], wada_ref[...]) + bada_ref[pl.ds(me, 1), :]
        mod_ref[pl.ds(me, 1), :] = modp[pl.ds(me, 1), :]
        for d, p in enumerate(peers):
            mcopy(me, d, p).start()
        for d, p in enumerate(peers):
            mcopy(p, d, p).wait_recv()
        passed = []
        for j, chip in enumerate(chips):
            for a in range(n):
                wcopy(a, 1 + j, (*chip, cc), (x, y, cc)).wait_recv()
                cp = wcopy(a, 4 + j, (*chip, cc), sib)
                cp.start()
                passed.append(cp)
        for a in range(n):
            wcopy(a, 0, sib, (x, y, cc)).wait_recv()
            for j, chip in enumerate(chips):
                wcopy(a, 4 + j, (*chip, 1 - cc), (x, y, cc)).wait_recv()
        for cp in first + passed:
            cp.wait_send()
        for d, p in enumerate(peers):
            ccopy(me, d, p).wait_send()
            mcopy(me, d, p).wait_send()
        for cp in local:
            cp.wait()

    out_shape = ([jax.ShapeDtypeStruct((N_DEV, cols), F32), jax.ShapeDtypeStruct((N_DEV, D_MODEL), F32)]
                 + [jax.ShapeDtypeStruct((N_DEV,) + s.shape, s.dtype) for s in shards])
    res = pl.pallas_call(
        body, name="comm_in", out_shape=out_shape,
        in_specs=[VMEM, VMEM, VMEM] + [ANY] * n,
        out_specs=[VMEM, VMEM] + [ANY] * n,
        scratch_shapes=[pltpu.VMEM((N_DEV, cols), F32),
                        pltpu.SemaphoreType.DMA((n, N_DEV)), pltpu.SemaphoreType.DMA((n, N_DEV)),
                        pltpu.SemaphoreType.DMA((n,)),
                        pltpu.SemaphoreType.DMA((N_DEV,)), pltpu.SemaphoreType.DMA((N_DEV,)),
                        pltpu.SemaphoreType.DMA((N_DEV,)), pltpu.SemaphoreType.DMA((N_DEV,))],
        compiler_params=_params(vmem=VMEM_MID),
    )(c, w_ada, b_ada8, *shards)
    return res[0], res[1], list(res[2:])


N_CHIP = 4


def _comm_pair(planes):
    n = len(planes)

    def body(*rest):
        srcs, dsts = rest[:n], rest[n:2 * n]
        send, recv = rest[2 * n:]
        x, y, cc = lax.axis_index("x"), lax.axis_index("y"), lax.axis_index("c")
        copies = [pltpu.make_async_remote_copy(src_ref=srcs[a].at[2 * ch + 1 - cc], dst_ref=dsts[a].at[ch],
                                               send_sem=send.at[a, ch], recv_sem=recv.at[a, ch],
                                               device_id=(x, y, 1 - cc), device_id_type=MESH)
                  for a in range(n) for ch in range(N_CHIP)]
        for cp in copies:
            cp.start()
        for cp in copies:
            cp.wait()

    out_shape = [jax.ShapeDtypeStruct((N_CHIP,) + p.shape[1:], p.dtype) for p in planes]
    return pl.pallas_call(
        body, name="comm_pair", out_shape=out_shape, in_specs=[ANY] * n, out_specs=[ANY] * n,
        scratch_shapes=[pltpu.SemaphoreType.DMA((n, N_CHIP)), pltpu.SemaphoreType.DMA((n, N_CHIP))],
    )(*planes)


def _pair_sum(name, planes, from_sib, core, tr):
    _, R, C = from_sib.shape

    def body(core_ref, a_ref, b_ref, o_ref):
        del core_ref
        o_ref[...] = (a_ref[...].astype(F32) + b_ref[...].astype(F32)).astype(o_ref.dtype)

    blk = pl.BlockSpec((None, tr, C), lambda i, j, c: (i, j, 0))
    grid_spec = pltpu.PrefetchScalarGridSpec(
        num_scalar_prefetch=1, grid=(N_CHIP, R // tr),
        in_specs=[pl.BlockSpec((None, tr, C), lambda i, j, c: (2 * i + c[0], j, 0)), blk], out_specs=blk)
    return pl.pallas_call(
        body, name=name, grid_spec=grid_spec, out_shape=jax.ShapeDtypeStruct(from_sib.shape, from_sib.dtype),
        compiler_params=_params(("parallel", "parallel"), VMEM_MID),
    )(core.reshape(1).astype(jnp.int32), planes, from_sib)


def _comm_out(chip_planes, small):
    n = len(chip_planes)

    def body(*rest):
        srcs = rest[:n]
        small_ref = rest[n]
        dsts = rest[n + 1:2 * n + 1]
        sall_ref = rest[2 * n + 1]
        wsend, wrecv, wloc, ssend, srecv, sloc = rest[2 * n + 2:]
        me = _my_index()
        x, y, cc = lax.axis_index("x"), lax.axis_index("y"), lax.axis_index("c")
        mine = 2 * x + y
        chips = [(1 - x, y), (x, 1 - y), (1 - x, 1 - y)]

        def wcopy(a, j, sending):
            chip = chips[j]
            there = 2 * chip[0] + chip[1]
            return pltpu.make_async_remote_copy(src_ref=srcs[a].at[there], dst_ref=dsts[a].at[mine if sending else there],
                                                send_sem=wsend.at[a, j], recv_sem=wrecv.at[a, j],
                                                device_id=(*chip, cc), device_id_type=MESH)

        def scopy(src_dev, d, to):
            return pltpu.make_async_remote_copy(src_ref=small_ref, dst_ref=sall_ref.at[src_dev],
                                                send_sem=ssend.at[d], recv_sem=srecv.at[src_dev],
                                                device_id=_dev(to), device_id_type=MESH)

        local = [pltpu.make_async_copy(srcs[a].at[mine], dsts[a].at[mine], wloc.at[a]) for a in range(n)]
        local.append(pltpu.make_async_copy(small_ref, sall_ref.at[me], sloc))
        for cp in local:
            cp.start()
        peers = [(me + d) % N_DEV for d in range(1, N_DEV)]
        for d, p in enumerate(peers):
            scopy(me, d, p).start()
        for j in range(len(chips)):
            for a in range(n):
                wcopy(a, j, True).start()
        for d, p in enumerate(peers):
            scopy(p, d, p).wait_recv()
        for j in range(len(chips)):
            for a in range(n):
                wcopy(a, j, False).wait_recv()
        for d, p in enumerate(peers):
            scopy(me, d, p).wait_send()
        for j in range(len(chips)):
            for a in range(n):
                wcopy(a, j, True).wait_send()
        for cp in local:
            cp.wait()

    out_shape = ([jax.ShapeDtypeStruct(p.shape, p.dtype) for p in chip_planes]
                 + [jax.ShapeDtypeStruct((N_DEV,) + small.shape, small.dtype)])
    res = pl.pallas_call(
        body, name="comm_out", out_shape=out_shape,
        in_specs=[ANY] * (n + 1), out_specs=[ANY] * (n + 1),
        scratch_shapes=[pltpu.SemaphoreType.DMA((n, N_CHIP)), pltpu.SemaphoreType.DMA((n, N_CHIP)),
                        pltpu.SemaphoreType.DMA((n,)),
                        pltpu.SemaphoreType.DMA((N_DEV,)), pltpu.SemaphoreType.DMA((N_DEV,)),
                        pltpu.SemaphoreType.DMA(())],
    )(*chip_planes, small)
    return list(res[:n]), res[n]


def _proj_fwd(x, shift, scale, g_norm, w_main, w_f, ts):
    S = x.shape[0]

    def body(x_ref, sh_ref, sc_ref, gn_ref, w_ref, wf_ref,
             q_ref, k_ref, v_ref, za_ref, u_ref, zb_ref, ga_ref, gb_ref, flc_ref, h_ref):
        xv = x_ref[...]
        r = lax.rsqrt(jnp.mean(xv * xv, axis=-1, keepdims=True) + EPS)
        h = (xv * r) * gn_ref[...] * (1.0 + sc_ref[...]) + sh_ref[...]
        hb = h.astype(MXU_DTYPE)
        h_ref[...] = hb

        def seg(off, n):
            return jnp.dot(hb, w_ref[:, off:off + n], preferred_element_type=F32)

        q_ref[...] = (seg(M_Q, WIDTH) * 0.125).astype(q_ref.dtype)
        k_ref[...] = seg(M_K, WIDTH).astype(k_ref.dtype)
        v_ref[...] = seg(M_V, WIDTH).astype(v_ref.dtype)
        za_ref[...] = seg(M_ZA, WIDTH)
        u_ref[...] = seg(M_U, WIDTH)
        zb_ref[...] = seg(M_ZB, WIDTH)
        ga_ref[...] = seg(M_GA, D_MODEL)
        gb_ref[...] = seg(M_GB, D_MODEL)
        flc_ref[...] = _mm32(h, wf_ref[...])

    row = lambda n: pl.BlockSpec((ts, n), lambda i: (i, 0))
    full = lambda a: pl.BlockSpec(a.shape, lambda i: (0,) * a.ndim)
    sds = jax.ShapeDtypeStruct
    return pl.pallas_call(
        body, name="proj_fwd", grid=(S // ts,),
        in_specs=[row(D_MODEL), full(shift), full(scale), full(g_norm), full(w_main), full(w_f)],
        out_specs=[row(WIDTH)] * 6 + [row(D_MODEL)] * 2 + [row(HEADS), row(D_MODEL)],
        out_shape=[sds((S, WIDTH), MXU_DTYPE)] * 3 + [sds((S, WIDTH), F32)] * 3 + [sds((S, D_MODEL), F32)] * 2
                  + [sds((S, HEADS), F32), sds((S, D_MODEL), MXU_DTYPE)],
        compiler_params=_params(("parallel",), VMEM_BIG),
    )(x, shift, scale, g_norm, w_main, w_f)


def _log_sigmoid(z):
    return jnp.minimum(z, 0.0) - jnp.log(1.0 + jnp.exp(-jnp.abs(z)))


def _fgate_bwd(dfc, flc, bf_row, ts):
    S = flc.shape[0]
    n = S // ts

    def body(df_ref, flc_ref, bfr_ref, dfl_ref, dbf_ref, carry):
        @pl.when(pl.program_id(0) == 0)
        def _():
            carry[...] = jnp.zeros_like(carry)
            dbf_ref[...] = jnp.zeros_like(dbf_ref)

        ri = lax.broadcasted_iota(jnp.int32, (ts, ts), 0)
        ci = lax.broadcasted_iota(jnp.int32, (ts, ts), 1)
        upper = (ci >= ri).astype(F32)
        rc = _mm32(upper, df_ref[...]) + carry[...]
        carry[...] = rc[0:1, :]
        z = flc_ref[...] + bfr_ref[...]
        dfl = rc * _sigmoid(-z)
        dfl_ref[...] = dfl
        dbf_ref[...] += jnp.sum(dfl, axis=0, keepdims=True)

    col = pl.BlockSpec((ts, HEADS), lambda i: (n - 1 - i, 0))
    one = pl.BlockSpec((1, HEADS), lambda i: (0, 0))
    return pl.pallas_call(
        body, name="fgate_bwd", grid=(n,),
        in_specs=[col, col, one], out_specs=[col, one],
        out_shape=[jax.ShapeDtypeStruct((S, HEADS), F32), jax.ShapeDtypeStruct((1, HEADS), F32)],
        scratch_shapes=[pltpu.VMEM((1, HEADS), F32)],
        compiler_params=_params(("arbitrary",)),
    )(dfc, flc, bf_row)


N_EXTRA = 3


def _attn_prep(q, k, v, flc, bf_row, t):
    S = q.shape[0]
    nb = S // t

    def body(q_ref, k_ref, v_ref, flc_ref, bfr_ref, qh_ref, kh_ref, vt_ref, qt_ref, carry):
        @pl.when(pl.program_id(0) == 0)
        def _():
            carry[...] = jnp.zeros_like(carry)

        ri = lax.broadcasted_iota(jnp.int32, (t, t), 0)
        ci = lax.broadcasted_iota(jnp.int32, (t, t), 1)
        f = _mm32((ci <= ri).astype(F32), _log_sigmoid(flc_ref[...] + bfr_ref[...])) + carry[...]
        carry[...] = f[t - 1:t, :]
        lane = lax.broadcasted_iota(jnp.int32, (t, 128), 1)
        for p in range(PAIRS):
            qp = q_ref[:, p * 128:(p + 1) * 128]
            kp = k_ref[:, p * 128:(p + 1) * 128]
            vt_ref[p, 0] = v_ref[:, p * 128:(p + 1) * 128].T
            qt_ref[p, 0] = qp.T
            for h in range(2):
                own = (lane < 64) if h == 0 else (lane >= 64)
                base = 64 if h == 0 else 0
                fh = f[:, 2 * p + h:2 * p + h + 1]
                parts = []
                rest = fh
                for _ in range(N_EXTRA):
                    part = rest.astype(qh_ref.dtype)
                    parts.append(part)
                    rest = rest - part.astype(F32)
                one = jnp.ones((t, 1), qh_ref.dtype)
                eq = jnp.zeros((t, 128), qh_ref.dtype)
                ek = jnp.zeros((t, 128), qh_ref.dtype)
                for j in range(N_EXTRA):
                    eq = jnp.where(lane == base + j, parts[j], eq)
                    eq = jnp.where(lane == base + N_EXTRA + j, one, eq)
                    ek = jnp.where(lane == base + j, one, ek)
                    ek = jnp.where(lane == base + N_EXTRA + j, -parts[j], ek)
                qh_ref[2 * p + h] = jnp.where(own, qp, eq)
                kh_ref[2 * p + h] = jnp.where(own, kp, ek)

    row = pl.BlockSpec((t, WIDTH), lambda i: (i, 0))
    heads = pl.BlockSpec((HEADS, t, 128), lambda i: (0, i, 0))
    return pl.pallas_call(
        body, name="attn_prep", grid=(nb,),
        in_specs=[row, row, row, pl.BlockSpec((t, HEADS), lambda i: (i, 0)), pl.BlockSpec((1, HEADS), lambda i: (0, 0))],
        out_specs=[heads, heads] + [pl.BlockSpec((PAIRS, 1, 128, t), lambda i: (0, i, 0, 0))] * 2,
        out_shape=[jax.ShapeDtypeStruct((HEADS, S, 128), q.dtype), jax.ShapeDtypeStruct((HEADS, S, 128), k.dtype),
                   jax.ShapeDtypeStruct((PAIRS, nb, 128, t), v.dtype), jax.ShapeDtypeStruct((PAIRS, nb, 128, t), q.dtype)],
        scratch_shapes=[pltpu.VMEM((1, HEADS), F32)],
        compiler_params=_params(("arbitrary",), VMEM_MID),
    )(q, k, v, flc, bf_row)


def _attn_fwd(qh, kh, vt, t):
    S = qh.shape[1]
    nb = S // t

    def body(q_ref, k_ref, vt_ref, o_ref, lse_ref, acc_s):
        qi = pl.program_id(1)
        acc_s[...] = jnp.zeros_like(acc_s)

        def step(ki, nblk, masked, carry):
            m_old, l_old = carry[:2], carry[2:]
            ks = pl.multiple_of(ki * t, t)
            rows = nblk * t
            sts = [_mm_nt(k_ref[h, pl.ds(ks, rows), :], q_ref[h]) for h in range(2)]
            if masked:
                ri = lax.broadcasted_iota(jnp.int32, (t, t), 0)
                ci = lax.broadcasted_iota(jnp.int32, (t, t), 1)
                sts = [jnp.where(ci >= ri, st, NEG) for st in sts]
            m_new = [jnp.maximum(m_old[h], jnp.max(sts[h], axis=0, keepdims=True)) for h in range(2)]
            alpha = [jnp.exp(m_old[h] - m_new[h]) for h in range(2)]
            pts = [jnp.exp(sts[h] - m_new[h]) for h in range(2)]
            l_new = [alpha[h] * l_old[h] + jnp.sum(pts[h], axis=0, keepdims=True) for h in range(2)]
            for h in range(2):
                pv = _mm(vt_ref[ki], pts[h][:t])
                for b in range(1, nblk):
                    pv = pv + _mm(vt_ref[ki + b], pts[h][b * t:(b + 1) * t])
                acc_s[h] = alpha[h] * acc_s[h] + pv
            return (*m_new, *l_new)

        init = (jnp.full((1, t), -jnp.inf, F32),) * 2 + (jnp.zeros((1, t), F32),) * 2
        carry = lax.fori_loop(0, qi // 2, lambda j, c: step(2 * j, 2, False, c), init)
        carry = lax.cond(qi % 2 == 1, lambda c: step(qi - 1, 1, False, c), lambda c: c, carry)
        m0, m1, l0, l1 = step(qi, 1, True, carry)
        first = lax.broadcasted_iota(jnp.int32, (128, t), 0) < 64
        o_ref[...] = jnp.where(first, acc_s[0] / l0, acc_s[1] / l1).T
        lse_ref[...] = jnp.concatenate([m0 + jnp.log(l0), m1 + jnp.log(l1)], axis=0)

    return pl.pallas_call(
        body, name="attn_fwd", grid=(PAIRS, nb),
        in_specs=[pl.BlockSpec((2, t, 128), lambda p, i: (p, i, 0)), pl.BlockSpec((2, S, 128), lambda p, i: (p, 0, 0)),
                  pl.BlockSpec((None, nb, 128, t), lambda p, i: (p, 0, 0, 0))],
        out_specs=[pl.BlockSpec((t, 128), lambda p, i: (i, p)), pl.BlockSpec((None, None, 2, t), lambda p, i: (p, i, 0, 0))],
        out_shape=[jax.ShapeDtypeStruct((S, WIDTH), F32), jax.ShapeDtypeStruct((PAIRS, nb, 2, t), F32)],
        scratch_shapes=[pltpu.VMEM((2, 128, t), F32)],
        compiler_params=_params(("parallel", "parallel"), VMEM_MID),
    )(qh, kh, vt)


def _attn_bwd(qh, qt, dot, kh, v, lse4, dl4, t):
    S = qh.shape[1]
    nb = S // t

    def body(q_ref, qt_ref, dot_ref, k_ref, v_ref, lse_ref, dl_ref,
             dq_ref, dk_ref, dv_ref, dfq_ref, dfk_ref, dqt_s, kct_s, vh_s, dkt_s, dvt_s, dfk_s):
        kj = pl.program_id(1)
        is_a = lax.broadcasted_iota(jnp.int32, (t, 128), 1) < 64
        top = lax.broadcasted_iota(jnp.int32, (128, t), 0) < 64

        @pl.when(kj == 0)
        def _():
            dqt_s[...] = jnp.zeros_like(dqt_s)
            dfq_ref[...] = jnp.zeros_like(dfq_ref)

        vp = v_ref[...]
        zero = jnp.zeros_like(vp)
        vh_s[0] = jnp.where(is_a, vp, zero)
        vh_s[1] = jnp.where(is_a, zero, vp)
        kct_s[0] = jnp.where(is_a, k_ref[0], zero.astype(kct_s.dtype)).T
        kct_s[1] = jnp.where(is_a, zero.astype(kct_s.dtype), k_ref[1]).T
        dkt_s[...] = jnp.zeros_like(dkt_s)
        dvt_s[...] = jnp.zeros_like(dvt_s)
        dfk_s[...] = jnp.zeros_like(dfk_s)

        def step(qi, masked):
            qs = pl.multiple_of(qi * t, t)
            qtb, dotb, lse, dl = qt_ref[qi], dot_ref[qi], lse_ref[qi], dl_ref[qi]
            zt = jnp.zeros_like(qtb)
            sts = [_mm_nt(k_ref[h], q_ref[h, pl.ds(qs, t), :]) - lse[h:h + 1, :] for h in range(2)]
            dpts = [_mm(vh_s[h], dotb) for h in range(2)]
            if masked:
                ri = lax.broadcasted_iota(jnp.int32, (t, t), 0)
                ci = lax.broadcasted_iota(jnp.int32, (t, t), 1)
                sts = [jnp.where(ci >= ri, st, NEG) for st in sts]
            pts = [jnp.exp(st) for st in sts]
            dsts = [pts[h] * (dpts[h] - dl[h:h + 1, :]) for h in range(2)]
            dfq_ref[qi] += jnp.concatenate([jnp.sum(dst, axis=0, keepdims=True) for dst in dsts], axis=0)
            for h in range(2):
                rows = top if h == 0 else jnp.logical_not(top)
                dfk_s[h] += jnp.sum(dsts[h], axis=1, keepdims=True)
                dsb = dsts[h].astype(MXU_DTYPE)
                dvt_s[...] += _mm_nt(jnp.where(rows, dotb, zt.astype(dotb.dtype)), pts[h])
                dkt_s[...] += _mm_nt(jnp.where(rows, qtb, zt), dsb)
                dqt_s[qi] += _mm(kct_s[h], dsb)

        step(kj, True)

        def loop_body(qi, carry):
            step(qi, False)
            return carry

        lax.fori_loop(kj + 1, nb, loop_body, 0)
        dk_ref[...] = dkt_s[...].T.astype(dk_ref.dtype)
        dv_ref[...] = dvt_s[...].T.astype(dv_ref.dtype)
        dfk_ref[...] = jnp.where(lax.broadcasted_iota(jnp.int32, (t, 2), 1) == 0, dfk_s[0], dfk_s[1])

        @pl.when(kj == nb - 1)
        def _():
            for i in range(nb):
                dq_ref[i * t:(i + 1) * t, :] = dqt_s[i].T * 0.125

    blk = pl.BlockSpec((t, 128), lambda p, j: (j, p))
    res = pl.BlockSpec((S, 128), lambda p, j: (0, p))
    rows4 = pl.BlockSpec((None, nb, 2, t), lambda p, j: (p, 0, 0, 0))
    cols4 = pl.BlockSpec((None, t, 2), lambda p, j: (p, j, 0))
    tr4 = pl.BlockSpec((None, nb, 128, t), lambda p, j: (p, 0, 0, 0))
    return pl.pallas_call(
        body, name="attn_bwd", grid=(PAIRS, nb),
        in_specs=[pl.BlockSpec((2, S, 128), lambda p, j: (p, 0, 0)), tr4, tr4,
                  pl.BlockSpec((2, t, 128), lambda p, j: (p, j, 0)), blk, rows4, rows4],
        out_specs=[res, blk, blk, rows4, cols4],
        out_shape=[jax.ShapeDtypeStruct((S, WIDTH), F32), jax.ShapeDtypeStruct((S, WIDTH), MXU_DTYPE),
                   jax.ShapeDtypeStruct((S, WIDTH), MXU_DTYPE), jax.ShapeDtypeStruct((PAIRS, nb, 2, t), F32),
                   jax.ShapeDtypeStruct((PAIRS, S, 2), F32)],
        scratch_shapes=[pltpu.VMEM((nb, 128, t), F32), pltpu.VMEM((2, 128, t), kh.dtype),
                        pltpu.VMEM((2, t, 128), v.dtype), pltpu.VMEM((128, t), F32), pltpu.VMEM((128, t), F32),
                        pltpu.VMEM((2, t, 1), F32)],
        compiler_params=_params(("parallel", "arbitrary"), VMEM_BIG),
    )(qh, qt, dot, kh, v, lse4, dl4)


def _s5_mats(a_re, a_im, log_dt, b_re, b_im, c_re, c_im, d_skip):
    Lc = CHUNK
    dt = jnp.exp(log_dt)[:, None]
    lr, li = a_re * dt, a_im * dt

    def apow(n):
        n = jnp.asarray(n, F32)[None, :, None]
        mag = jnp.exp(n * lr[:, None, :])
        ang = n * li[:, None, :]
        return mag * jnp.cos(ang), mag * jnp.sin(ang)

    ar, ai = apow([1.0])
    ar, ai = ar[:, 0], ai[:, 0]
    den = a_re * a_re + a_im * a_im
    nr, ni = ar - 1.0, ai
    fr = (nr * a_re + ni * a_im) / den
    fi = (ni * a_re - nr * a_im) / den
    bbr = fr[:, :, None] * b_re - fi[:, :, None] * b_im
    bbi = fr[:, :, None] * b_im + fi[:, :, None] * b_re
    steps = np.arange(Lc, dtype=np.float32)
    pr, pi = apow(steps)
    car = c_re[:, None] * pr[:, :, None, :] - c_im[:, None] * pi[:, :, None, :]
    cai = c_re[:, None] * pi[:, :, None, :] + c_im[:, None] * pr[:, :, None, :]
    kern = (jnp.einsum('glcp,gpd->glcd', car, bbr, precision=HI)
            - jnp.einsum('glcp,gpd->glcd', cai, bbi, precision=HI))
    skip = d_skip.reshape(GROUPS, CG)[:, :, None] * jnp.eye(CG, dtype=F32)[None]
    kern = kern.at[:, 0].add(skip)
    trow = kern.transpose(0, 3, 1, 2).reshape(GROUPS, CG, Lc * CG)
    p1r, p1i = apow(steps + 1.0)
    cr = c_re[:, None] * p1r[:, :, None, :] - c_im[:, None] * p1i[:, :, None, :]
    ci = c_re[:, None] * p1i[:, :, None, :] + c_im[:, None] * p1r[:, :, None, :]
    to_rows = lambda m: m.transpose(0, 3, 1, 2).reshape(GROUPS, STATE, Lc * CG)
    camat = jnp.concatenate([to_rows(cr), -to_rows(ci)], axis=1)
    qr, qi = apow(Lc - 1.0 - steps)
    zr = qr[:, :, None, :] * bbr.transpose(0, 2, 1)[:, None] - qi[:, :, None, :] * bbi.transpose(0, 2, 1)[:, None]
    zi = qr[:, :, None, :] * bbi.transpose(0, 2, 1)[:, None] + qi[:, :, None, :] * bbr.transpose(0, 2, 1)[:, None]
    bzmat = jnp.concatenate([zr, zi], axis=-1).reshape(GROUPS, Lc * CG, 2 * STATE)
    lr_, li_ = apow([float(Lc)])
    al = jnp.concatenate([lr_[:, 0], li_[:, 0]], axis=-1)
    return trow, camat, bzmat, al


def _s5_scan_powers(a_re, a_im, log_dt, n_steps):
    dt = jnp.exp(log_dt)[:, None]
    lr, li = a_re * dt, a_im * dt
    n = (CHUNK * 2.0 ** np.arange(n_steps)).astype(np.float32)[None, :, None]
    mag = jnp.exp(n * lr[:, None, :])
    pr, pi = mag * jnp.cos(n * li[:, None, :]), mag * jnp.sin(n * li[:, None, :])
    fwd = jnp.stack([jnp.concatenate([pr, pr], -1), jnp.concatenate([-pi, pi], -1)], axis=2)
    bwd = jnp.stack([jnp.concatenate([pr, pr], -1), jnp.concatenate([pi, -pi], -1)], axis=2)
    return fwd, bwd


def _shift_rows(x, sh, down):
    n = x.shape[0]
    ri = lax.broadcasted_iota(jnp.int32, x.shape, 0)
    if down:
        return jnp.where(ri >= sh, pltpu.roll(x, sh, 0), 0.0)
    return jnp.where(ri < n - sh, pltpu.roll(x, n - sh, 0), 0.0)


GPB = 128 // CG


def _lane_transpose(arrs):
    lane = lax.broadcasted_iota(jnp.int32, arrs[0].shape, 1)
    arrs = list(arrs)
    k = GPB // 2
    while k >= 1:
        hi = ((lane // CG) & k) != 0
        new = list(arrs)
        for i in range(GPB):
            if i & k:
                continue
            lo_arr, hi_arr = arrs[i], arrs[i + k]
            new[i] = jnp.where(hi, pltpu.roll(hi_arr, CG * k, 1), lo_arr)
            new[i + k] = jnp.where(hi, hi_arr, pltpu.roll(lo_arr, 128 - CG * k, 1))
        arrs = new
        k //= 2
    return arrs


def _gather_block(ref, dst, nch):
    for half in range(CHUNK // GPB):
        outs = _lane_transpose([ref[pl.ds(half * GPB + l8, nch, stride=CHUNK), :] for l8 in range(GPB)])
        for g in range(GPB):
            dst[half, g] = outs[g]


def _scatter_block(src, ref, nch):
    for half in range(CHUNK // GPB):
        outs = _lane_transpose([src[half, g] for g in range(GPB)])
        for l8 in range(GPB):
            ref[pl.ds(half * GPB + l8, nch, stride=CHUNK), :] = outs[l8]


def _toeplitz(trow):
    lane = lax.broadcasted_iota(jnp.int32, (CG, 128), 1)
    x0, x1 = trow[:, :128], trow[:, 128:]
    zero = jnp.zeros_like(x0)
    rows = []
    for s in range(CHUNK):
        sh = (CG * s) % 128
        r0 = pltpu.roll(x0, sh, 1) if sh else x0
        r1 = pltpu.roll(x1, sh, 1) if sh else x1
        if CG * s < 128:
            rows.append(jnp.concatenate([jnp.where(lane >= sh, r0, zero), jnp.where(lane >= sh, r1, r0)], axis=1))
        else:
            rows.append(jnp.concatenate([zero, jnp.where(lane >= sh, r0, zero)], axis=1))
    return jnp.concatenate(rows, axis=0)


def _toeplitz_adjoint(dt):
    lane = lax.broadcasted_iota(jnp.int32, (CG, 128), 1)
    acc0 = jnp.zeros((CG, 128), F32)
    acc1 = jnp.zeros((CG, 128), F32)
    for s in range(CHUNK):
        x0, x1 = dt[CG * s:CG * s + CG, :128], dt[CG * s:CG * s + CG, 128:]
        sh = (CG * s) % 128
        keep = 128 - sh
        r0 = pltpu.roll(x0, keep, 1) if sh else x0
        r1 = pltpu.roll(x1, keep, 1) if sh else x1
        if CG * s < 128:
            acc0 = acc0 + jnp.where(lane < keep, r0, r1)
            acc1 = acc1 + jnp.where(lane < keep, r1, 0.0)
        else:
            acc0 = acc0 + jnp.where(lane < keep, r1, 0.0)
    return jnp.concatenate([acc0, acc1], axis=1)


def _s5_fwd(u, trow, camat, bzmat, pw):
    S = u.shape[0]
    nch = S // CHUNK
    n_steps = pw.shape[1]

    def body(u_ref, t_ref, ca_ref, bz_ref, pw_ref, y_ref, xp_ref, uc_ref, ub_s, yb_s):
        g = pl.program_id(1)

        @pl.when(g == 0)
        def _():
            _gather_block(u_ref, ub_s, nch)

        uc = jnp.concatenate([ub_s[0, g], ub_s[1, g]], axis=1)
        uc_ref[...] = uc
        x = _s5mm(uc, bz_ref[...])
        for kk in range(n_steps):
            xs = _shift_rows(x, 2 ** kk, True)
            m = pw_ref[kk]
            x = x + m[0:1, :] * xs + m[1:2, :] * pltpu.roll(xs, STATE, 1)
        xp = _shift_rows(x, 1, True)
        xp_ref[...] = xp
        yc = _s5mm(uc, _toeplitz(t_ref[...])) + _s5mm(xp, ca_ref[...])
        yb_s[0, g] = yc[:, :128]
        yb_s[1, g] = yc[:, 128:]

        @pl.when(g == GPB - 1)
        def _():
            _scatter_block(yb_s, y_ref, nch)

    per = lambda a: pl.BlockSpec((None,) + a.shape[1:], lambda b, g: (b * GPB + g,) + (0,) * (a.ndim - 1))
    nat = pl.BlockSpec((S, 128), lambda b, g: (0, b))
    return pl.pallas_call(
        body, name="s5_fwd", grid=(GROUPS // GPB, GPB),
        in_specs=[nat, per(trow), per(camat), per(bzmat), per(pw)],
        out_specs=[nat, pl.BlockSpec((None, nch, 2 * STATE), lambda b, g: (b * GPB + g, 0, 0)),
                   pl.BlockSpec((None, nch, CHUNK * CG), lambda b, g: (b * GPB + g, 0, 0))],
        out_shape=[jax.ShapeDtypeStruct((S, GROUPS * CG), F32), jax.ShapeDtypeStruct((GROUPS, nch, 2 * STATE), F32),
                   jax.ShapeDtypeStruct((GROUPS, nch, CHUNK * CG), F32)],
        scratch_shapes=[pltpu.VMEM((CHUNK // GPB, GPB, nch, 128), F32)] * 2,
        compiler_params=_params(("parallel", "arbitrary"), VMEM_BIG),
    )(u, trow, camat, bzmat, pw)


def _s5_bwd(uc, dy, xp, trow, camat, bzmat, pwc):
    S = dy.shape[0]
    nch = S // CHUNK
    n_steps = pwc.shape[1]

    def body(uc_ref, dy_ref, xp_ref, t_ref, ca_ref, bz_ref, pw_ref, du_ref, dt_ref, dca_ref, dbz_ref, dal_ref,
             dyb_s, dub_s):
        g = pl.program_id(1)

        @pl.when(g == 0)
        def _():
            _gather_block(dy_ref, dyb_s, nch)

        uc = uc_ref[...]
        dyc = jnp.concatenate([dyb_s[0, g], dyb_s[1, g]], axis=1)
        xpv = xp_ref[...]
        dt_ref[...] = _toeplitz_adjoint(_s5mm_tn(uc, dyc))
        dca_ref[...] = _s5mm_tn(xpv, dyc)
        dx = _shift_rows(_s5mm_nt(dyc, ca_ref[...]), 1, False)
        for kk in range(n_steps):
            xs = _shift_rows(dx, 2 ** kk, False)
            m = pw_ref[kk]
            dx = dx + m[0:1, :] * xs + m[1:2, :] * pltpu.roll(xs, STATE, 1)
        dbz_ref[...] = _s5mm_tn(uc, dx)
        dal_ref[0:1, :] = jnp.sum(dx * xpv, axis=0, keepdims=True)
        dal_ref[1:2, :] = jnp.sum(dx * pltpu.roll(xpv, STATE, 1), axis=0, keepdims=True)
        duc = _s5mm_nt(dyc, _toeplitz(t_ref[...])) + _s5mm_nt(dx, bz_ref[...])
        dub_s[0, g] = duc[:, :128]
        dub_s[1, g] = duc[:, 128:]

        @pl.when(g == GPB - 1)
        def _():
            _scatter_block(dub_s, du_ref, nch)

    per = lambda a: pl.BlockSpec((None,) + a.shape[1:], lambda b, g: (b * GPB + g,) + (0,) * (a.ndim - 1))
    nat = pl.BlockSpec((S, 128), lambda b, g: (0, b))
    sds = jax.ShapeDtypeStruct
    mats = [sds(trow.shape, F32), sds(camat.shape, F32), sds(bzmat.shape, F32), sds((GROUPS, 2, 2 * STATE), F32)]
    return pl.pallas_call(
        body, name="s5_bwd", grid=(GROUPS // GPB, GPB),
        in_specs=[per(uc), nat, per(xp), per(trow), per(camat), per(bzmat), per(pwc)],
        out_specs=[nat] + [per(o) for o in mats], out_shape=[sds((S, GROUPS * CG), F32)] + mats,
        scratch_shapes=[pltpu.VMEM((CHUNK // GPB, GPB, nch, 128), F32)] * 2,
        compiler_params=_params(("parallel", "arbitrary"), VMEM_BIG),
    )(uc, dy, xp, trow, camat, bzmat, pwc)


GELU_C0 = math.sqrt(2.0 / math.pi)
GELU_C1 = 0.044715


def _mix(o, za, ys, zb, ga, gb, x, tgt, gate, b_glu, g_final, w_glu, w_up_a, w_up_b, w_out, hsel, ts, ta):
    S = o.shape[0]
    tpb = ta // ts

    def body(o_ref, za_ref, ys_ref, zb_ref, ga_ref, gb_ref, x_ref, t_ref, gate_ref, bglu_ref, gf_ref,
             wglu_ref, wua_ref, wub_ref, wout_ref, hsel_ref,
             dx2_ref, do_ref, dza_ref, dzb_ref, dga_ref, dgb_ref, dys_ref, dl_ref,
             mg_ref, dmo_ref, ya_ref, dua_ref, yb_ref, dub_ref, yg_ref, dgl_ref,
             dbglu_ref, dgate_ref, dgf_ref, loss_ref):
        @pl.when(pl.program_id(0) == 0)
        def _():
            dbglu_ref[...] = jnp.zeros_like(dbglu_ref)
            dgate_ref[...] = jnp.zeros_like(dgate_ref)
            dgf_ref[...] = jnp.zeros_like(dgf_ref)
            loss_ref[...] = jnp.zeros_like(loss_ref)

        ov = o_ref[...]
        za = za_ref[...]
        sza = _sigmoid(za)
        silu_a = za * sza
        ya = ov * silu_a
        ya_b = ya.astype(ya_ref.dtype)
        ya_ref[...] = ya_b
        ysv = ys_ref[...]
        th = jnp.tanh(GELU_C0 * (ysv + GELU_C1 * ysv * ysv * ysv))
        yg = 0.5 * ysv * (1.0 + th)
        yg_b = yg.astype(yg_ref.dtype)
        yg_ref[...] = yg_b
        sg = _sigmoid(_mm(yg_b, wglu_ref[...]) + bglu_ref[...])
        yb1 = yg * sg
        zb = zb_ref[...]
        szb = _sigmoid(zb)
        silu_b = zb * szb
        yb_b = (yb1 * silu_b).astype(yb_ref.dtype)
        yb_ref[...] = yb_b
        ua = _mm(ya_b, wua_ref[...])
        ub = _mm(yb_b, wub_ref[...])
        sa = _sigmoid(ga_ref[...])
        sb = _sigmoid(gb_ref[...])
        merged_b = (sa * ua + sb * ub).astype(mg_ref.dtype)
        mg_ref[...] = merged_b
        mo = _mm(merged_b, wout_ref[...])
        gate_v = gate_ref[...]
        x2 = x_ref[...] + gate_v * mo
        r2 = lax.rsqrt(jnp.mean(x2 * x2, axis=-1, keepdims=True) + EPS)
        x2n = x2 * r2
        gf = gf_ref[...]
        diff = x2n * gf - t_ref[...]
        loss_ref[...] += jnp.sum(jnp.sum(diff * diff, axis=-1, keepdims=True), axis=0, keepdims=True) * (0.5 / D_MODEL)
        dy = diff * (1.0 / D_MODEL)
        dgf_ref[...] += jnp.sum(dy * x2n, axis=0, keepdims=True)
        dyg = dy * gf
        dx2 = r2 * (dyg - x2n * jnp.mean(dyg * x2n, axis=-1, keepdims=True))
        dx2_ref[...] = dx2
        dgate_ref[...] += jnp.sum(dx2 * mo, axis=0, keepdims=True)
        dmo_b = (dx2 * gate_v).astype(dmo_ref.dtype)
        dmo_ref[...] = dmo_b
        dmerged = _mm_nt(dmo_b, wout_ref[...])
        dua_b = (dmerged * sa).astype(dua_ref.dtype)
        dub_b = (dmerged * sb).astype(dub_ref.dtype)
        dua_ref[...] = dua_b
        dub_ref[...] = dub_b
        dga_ref[...] = (dmerged * ua * sa * (1.0 - sa)).astype(dga_ref.dtype)
        dgb_ref[...] = (dmerged * ub * sb * (1.0 - sb)).astype(dgb_ref.dtype)
        dya = _mm_nt(dua_b, wua_ref[...])
        dyb = _mm_nt(dub_b, wub_ref[...])
        dov = dya * silu_a
        dob = dov.astype(do_ref.dtype)
        for p in range(PAIRS):
            do_ref[p] = dob[:, p * 128:(p + 1) * 128].T
        dl_ref[...] = _mm32_nt(hsel_ref[...], dov * ov)
        dza_ref[...] = (dya * ov * (sza * (1.0 + za * (1.0 - sza)))).astype(dza_ref.dtype)
        dyb1 = dyb * silu_b
        dzb_ref[...] = (dyb * yb1 * (szb * (1.0 + zb * (1.0 - szb)))).astype(dzb_ref.dtype)
        dgl = dyb1 * yg * sg * (1.0 - sg)
        dbglu_ref[...] += jnp.sum(dgl, axis=0, keepdims=True)
        dgl_b = dgl.astype(dgl_ref.dtype)
        dgl_ref[...] = dgl_b
        dyg2 = dyb1 * sg + _mm_nt(dgl_b, wglu_ref[...])
        dgelu = 0.5 * (1.0 + th) + 0.5 * ysv * (1.0 - th * th) * GELU_C0 * (1.0 + 3.0 * GELU_C1 * ysv * ysv)
        dys_ref[...] = dyg2 * dgelu

    row = lambda n: pl.BlockSpec((ts, n), lambda i: (i, 0))
    full = lambda a: pl.BlockSpec(a.shape, lambda i: (0,) * a.ndim)
    vec = lambda n: pl.BlockSpec((1, n), lambda i: (0, 0))
    sds = jax.ShapeDtypeStruct
    W, Dm = WIDTH, D_MODEL
    return pl.pallas_call(
        body, name="mix", grid=(S // ts,),
        in_specs=[row(W), row(W), row(W), row(W), row(Dm), row(Dm), row(Dm), row(Dm),
                  full(gate), full(b_glu), full(g_final), full(w_glu), full(w_up_a), full(w_up_b), full(w_out), full(hsel)],
        out_specs=[row(Dm), pl.BlockSpec((PAIRS, None, 128, ts), lambda i: (0, i // tpb, 0, i % tpb)), row(W), row(W),
                   row(Dm), row(Dm), row(W), pl.BlockSpec((HEADS, ts), lambda i: (0, i)),
                   row(Dm), row(Dm), row(W), row(Dm), row(W), row(Dm), row(W), row(W),
                   vec(W), vec(Dm), vec(Dm), vec(1)],
        out_shape=[sds((S, Dm), F32), sds((PAIRS, S // ta, 128, ta), MXU_DTYPE), sds((S, W), MXU_DTYPE), sds((S, W), MXU_DTYPE),
                   sds((S, Dm), MXU_DTYPE), sds((S, Dm), MXU_DTYPE), sds((S, W), F32), sds((HEADS, S), F32),
                   sds((S, Dm), MXU_DTYPE), sds((S, Dm), MXU_DTYPE), sds((S, W), MXU_DTYPE), sds((S, Dm), MXU_DTYPE),
                   sds((S, W), MXU_DTYPE), sds((S, Dm), MXU_DTYPE), sds((S, W), MXU_DTYPE), sds((S, W), MXU_DTYPE),
                   sds((1, W), F32), sds((1, Dm), F32), sds((1, Dm), F32), sds((1, 1), F32)],
        compiler_params=_params(("arbitrary",), VMEM_BIG),
    )(o, za, ys, zb, ga, gb, x, tgt, gate, b_glu, g_final, w_glu, w_up_a, w_up_b, w_out, hsel)


def _matmul_tn(name, a, b, ts):
    S, M = a.shape
    N = b.shape[1]
    tn = min(N, 1024)

    def body(a_ref, b_ref, o_ref):
        @pl.when(pl.program_id(1) == 0)
        def _():
            o_ref[...] = jnp.zeros_like(o_ref)

        o_ref[...] += _mm_tn(a_ref[...], b_ref[...])

    return pl.pallas_call(
        body, name=name, grid=(N // tn, S // ts),
        in_specs=[pl.BlockSpec((ts, M), lambda j, i: (i, 0)), pl.BlockSpec((ts, tn), lambda j, i: (i, j))],
        out_specs=pl.BlockSpec((M, tn), lambda j, i: (0, j)),
        out_shape=jax.ShapeDtypeStruct((M, N), F32),
        compiler_params=_params(("parallel", "arbitrary"), VMEM_MID),
    )(a, b)


def _proj_bwd(dq, dk, dv, dza, du, dzb, dga, dgb, dfl, x, dx2, shift, scale, g_norm, w_main, w_ft, ts):
    S = x.shape[0]

    def body(dq_ref, dk_ref, dv_ref, dza_ref, du_ref, dzb_ref, dga_ref, dgb_ref, dfl_ref, x_ref, dx2_ref,
             sc_ref, gn_ref, w_ref, wft_ref, gx_ref, dsh_ref, dsc_ref, dgn_ref):
        @pl.when(pl.program_id(0) == 0)
        def _():
            dsh_ref[...] = jnp.zeros_like(dsh_ref)
            dsc_ref[...] = jnp.zeros_like(dsc_ref)
            dgn_ref[...] = jnp.zeros_like(dgn_ref)

        def seg(ref, off, n):
            return _mm_nt(ref[...], w_ref[:, off:off + n])

        dh = (seg(dq_ref, M_Q, WIDTH) + seg(dk_ref, M_K, WIDTH) + seg(dv_ref, M_V, WIDTH)
              + seg(dza_ref, M_ZA, WIDTH) + seg(du_ref, M_U, WIDTH) + seg(dzb_ref, M_ZB, WIDTH)
              + seg(dga_ref, M_GA, D_MODEL) + seg(dgb_ref, M_GB, D_MODEL)
              + _mm32(dfl_ref[...], wft_ref[...]))
        xv = x_ref[...]
        r = lax.rsqrt(jnp.mean(xv * xv, axis=-1, keepdims=True) + EPS)
        xn = xv * r
        gn = gn_ref[...]
        s1 = 1.0 + sc_ref[...]
        dsh_ref[...] += jnp.sum(dh, axis=0, keepdims=True)
        dhx = dh * xn
        dsc_ref[...] += jnp.sum(dhx, axis=0, keepdims=True) * gn
        dgn_ref[...] += jnp.sum(dhx, axis=0, keepdims=True) * s1
        dxn = dh * (gn * s1)
        gx_ref[...] = dx2_ref[...] + r * (dxn - xn * jnp.mean(dxn * xn, axis=-1, keepdims=True))

    row = lambda n: pl.BlockSpec((ts, n), lambda i: (i, 0))
    full = lambda a: pl.BlockSpec(a.shape, lambda i: (0,) * a.ndim)
    vec = pl.BlockSpec((1, D_MODEL), lambda i: (0, 0))
    W, Dm = WIDTH, D_MODEL
    del shift
    return pl.pallas_call(
        body, name="proj_bwd", grid=(S // ts,),
        in_specs=[row(W)] * 6 + [row(Dm)] * 2 + [row(HEADS), row(Dm), row(Dm),
                                                 full(scale), full(g_norm), full(w_main), full(w_ft)],
        out_specs=[row(Dm), vec, vec, vec],
        out_shape=[jax.ShapeDtypeStruct((S, Dm), F32)] + [jax.ShapeDtypeStruct((1, Dm), F32)] * 3,
        compiler_params=_params(("arbitrary",), VMEM_BIG),
    )(dq, dk, dv, dza, du, dzb, dga, dgb, dfl, x, dx2, scale, g_norm, w_main, w_ft)


def _adamw(name, planes, w, m, v, tr):
    n, R, C = planes.shape
    bc1 = 1.0 - ADAM_B1 ** ADAM_STEP
    bc2 = 1.0 - ADAM_B2 ** ADAM_STEP

    def body(p_ref, w_ref, m_ref, v_ref, g_ref, d_ref, nm_ref, nv_ref):
        g = p_ref[0].astype(F32)
        for i in range(1, n):
            g = g + p_ref[i].astype(F32)
        g_ref[...] = g
        nm = ADAM_B1 * m_ref[...] + (1.0 - ADAM_B1) * g
        nv = ADAM_B2 * v_ref[...] + (1.0 - ADAM_B2) * (g * g)
        nm_ref[...] = nm
        nv_ref[...] = nv
        d_ref[...] = -ADAM_LR * ((nm / bc1) / (jnp.sqrt(nv / bc2) + ADAM_EPS) + ADAM_WD * w_ref[...])

    blk = pl.BlockSpec((tr, C), lambda i: (i, 0))
    return pl.pallas_call(
        body, name=name, grid=(R // tr,),
        in_specs=[pl.BlockSpec((n, tr, C), lambda i: (0, i, 0)), blk, blk, blk],
        out_specs=[blk] * 4, out_shape=[jax.ShapeDtypeStruct((R, C), F32)] * 4,
        compiler_params=_params(("parallel",), VMEM_MID),
    )(planes, w, m, v)


def _wada_grad(c_all, dmod_cols):
    def body(c_ref, d_ref, o_ref):
        o_ref[0] = _mm32_tn(c_ref[...], d_ref[...])

    return pl.pallas_call(
        body, name="wada_grad",
        out_shape=jax.ShapeDtypeStruct((1, c_all.shape[1], dmod_cols.shape[1]), F32),
        in_specs=[VMEM, VMEM], out_specs=VMEM,
    )(c_all, dmod_cols)


SMALL_ORDER = ("b_ada", "g_norm", "b_f", "a_re", "a_im", "log_dt", "b_re", "b_im", "c_re", "c_im",
               "d_skip", "b_glu", "g_final")
BIG_ORDER = ("w_ada", "w_in", "w_glu", "w_up_a", "w_up_b", "w_out")
ALL_ORDER = ("w_ada", "b_ada", "g_norm", "w_in", "b_f", "a_re", "a_im", "log_dt", "b_re", "b_im", "c_re", "c_im",
             "d_skip", "w_glu", "b_glu", "w_up_a", "w_up_b", "w_out", "g_final")


def _pack_small(parts, rows):
    flat = jnp.concatenate([p.reshape(-1).astype(F32) for p in parts])
    return jnp.pad(flat, (0, rows * 128 - flat.shape[0])).reshape(rows, 128)


def kernel(x, c, w_ada, b_ada, g_norm, w_in, b_f, a_re, a_im, log_dt, b_re, b_im, c_re, c_im, d_skip, w_glu, b_glu, w_up_a, w_up_b, w_out, g_final, loss_target, m_w_ada, m_b_ada, m_g_norm, m_w_in, m_b_f, m_a_re, m_a_im, m_log_dt, m_b_re, m_b_im, m_c_re, m_c_im, m_d_skip, m_w_glu, m_b_glu, m_w_up_a, m_w_up_b, m_w_out, m_g_final, v_w_ada, v_b_ada, v_g_norm, v_w_in, v_b_f, v_a_re, v_a_im, v_log_dt, v_b_re, v_b_im, v_c_re, v_c_im, v_d_skip, v_w_glu, v_b_glu, v_w_up_a, v_w_up_b, v_w_out, v_g_final):
    weights = dict(w_ada=w_ada, b_ada=b_ada, g_norm=g_norm, w_in=w_in, b_f=b_f, a_re=a_re, a_im=a_im, log_dt=log_dt,
                   b_re=b_re, b_im=b_im, c_re=c_re, c_im=c_im, d_skip=d_skip, w_glu=w_glu, b_glu=b_glu,
                   w_up_a=w_up_a, w_up_b=w_up_b, w_out=w_out, g_final=g_final)
    mom_m = dict(w_ada=m_w_ada, b_ada=m_b_ada, g_norm=m_g_norm, w_in=m_w_in, b_f=m_b_f, a_re=m_a_re, a_im=m_a_im,
                 log_dt=m_log_dt, b_re=m_b_re, b_im=m_b_im, c_re=m_c_re, c_im=m_c_im, d_skip=m_d_skip, w_glu=m_w_glu,
                 b_glu=m_b_glu, w_up_a=m_w_up_a, w_up_b=m_w_up_b, w_out=m_w_out, g_final=m_g_final)
    mom_v = dict(w_ada=v_w_ada, b_ada=v_b_ada, g_norm=v_g_norm, w_in=v_w_in, b_f=v_b_f, a_re=v_a_re, a_im=v_a_im,
                 log_dt=v_log_dt, b_re=v_b_re, b_im=v_b_im, c_re=v_c_re, c_im=v_c_im, d_skip=v_d_skip, w_glu=v_w_glu,
                 b_glu=v_b_glu, w_up_a=v_w_up_a, w_up_b=v_w_up_b, w_out=v_w_out, g_final=v_g_final)
    xs = x[0]
    tgt = loss_target[0]
    S = xs.shape[0]
    ts = min(256, S)
    ta = min(512, S)
    tw = min(2048, S)
    nch = S // CHUNK
    n_steps = max(1, int(math.ceil(math.log2(nch))))
    me = _my_index()

    shards = [w.astype(MXU_DTYPE) for w in (w_in[0], w_glu[0], w_up_a[0], w_up_b[0], w_out[0])]
    mod8, c_all, gathered = _comm_in(c, w_ada[0], b_ada.reshape(N_DEV, -1), shards)
    mod = mod8.reshape(1, 3 * D_MODEL)
    shift, scale, gate = mod[:, :D_MODEL], mod[:, D_MODEL:2 * D_MODEL], mod[:, 2 * D_MODEL:]
    w_in_full = gathered[0].transpose(1, 0, 2).reshape(D_MODEL, PROJ_WIDTH)
    w_main = jnp.concatenate([w_in_full[:, :OFF_F], w_in_full[:, OFF_F + HEADS:]], axis=1)
    w_f = w_in_full[:, OFF_F:OFF_F + HEADS].astype(F32)
    w_ft = w_f.T
    w_glu_full = gathered[1].reshape(WIDTH, WIDTH)
    w_up_a_full = gathered[2].transpose(1, 0, 2).reshape(WIDTH, D_MODEL)
    w_up_b_full = gathered[3].transpose(1, 0, 2).reshape(WIDTH, D_MODEL)
    w_out_full = gathered[4].reshape(D_MODEL, D_MODEL)

    q, k, v, za, u, zb, ga, gb, flc, hb = _proj_fwd(xs, shift, scale, g_norm, w_main, w_f, ts)
    nb = S // ta
    rows4 = lambda r: r.reshape(PAIRS, 2, nb, ta).transpose(0, 2, 1, 3)
    qh, kh, vt, qt = _attn_prep(q, k, v, flc, b_f, ta)
    o, lse4 = _attn_fwd(qh, kh, vt, ta)

    s5_params = (a_re[0], a_im[0], log_dt[0], b_re[0], b_im[0], c_re[0], c_im[0], d_skip[0])
    (trow, camat, bzmat, al), mats_vjp = jax.vjp(_s5_mats, *s5_params)
    del al
    pw_f, pw_b = _s5_scan_powers(a_re[0], a_im[0], log_dt[0], n_steps)
    ys, xprev, uc = _s5_fwd(u, trow, camat, bzmat, pw_f)

    hsel = (np.arange(WIDTH)[None, :] // 64 == np.arange(HEADS)[:, None]).astype(np.float32)
    (dx2, do, dza, dzb, dga, dgb, dys, dl_row, merged, dmo, ya, dua, yb, dub, yg, dgl,
     db_glu, dgate, dg_final, loss_part) = _mix(o, za, ys, zb, ga, gb, xs, tgt, gate, b_glu, g_final.reshape(1, -1),
                                                w_glu_full, w_up_a_full, w_up_b_full, w_out_full, jnp.asarray(hsel), ts, ta)

    gw_out = _matmul_tn("dw_out", merged, dmo, tw)
    gw_up_a = _matmul_tn("dw_up_a", ya, dua, tw)
    gw_up_b = _matmul_tn("dw_up_b", yb, dub, tw)
    gw_glu = _matmul_tn("dw_glu", yg, dgl, tw)

    du, d_trow, d_camat, d_bzmat, dal2 = _s5_bwd(uc, dys, xprev, trow, camat, bzmat, pw_b)
    d_al = jnp.concatenate([dal2[:, 0, :STATE] + dal2[:, 0, STATE:], dal2[:, 1, STATE:] - dal2[:, 1, :STATE]], axis=-1)
    gs5 = mats_vjp((d_trow, d_camat, d_bzmat, d_al))

    dl4 = rows4(dl_row)
    dq, dk, dv, dfq4, dfk4 = _attn_bwd(qh, qt, do, kh, v, lse4, dl4, ta)
    d_fcol = dfq4.transpose(0, 2, 1, 3).reshape(HEADS, S).T - dfk4.transpose(1, 0, 2).reshape(S, HEADS)
    dfl, db_f = _fgate_bwd(d_fcol, flc, b_f, ta)

    grad_x, dshift, dscale, dg_norm = _proj_bwd(dq, dk, dv, dza, du, dzb, dga, dgb, dfl, xs, dx2,
                                                shift, scale, g_norm, w_main, w_ft, ts)
    segs = [("dw_q", dq), ("dw_k", dk), ("dw_v", dv), ("dw_f", dfl), ("dw_za", dza), ("dw_u", du), ("dw_zb", dzb),
            ("dw_ga", dga), ("dw_gb", dgb)]
    gw_in = jnp.concatenate([_matmul_tn(nm, hb, d, tw) for nm, d in segs], axis=1)

    planes = [gw_in.reshape(D_MODEL, N_DEV, -1).transpose(1, 0, 2),
              gw_glu.reshape(N_DEV, -1, WIDTH),
              gw_up_a.reshape(WIDTH, N_DEV, -1).transpose(1, 0, 2),
              gw_up_b.reshape(WIDTH, N_DEV, -1).transpose(1, 0, 2),
              gw_out.reshape(N_DEV, -1, D_MODEL)]
    planes = [p.astype(MXU_DTYPE) for p in planes]
    dmod = jnp.concatenate([dshift, dscale, dgate], axis=1)
    small_parts = [dmod, dg_norm, db_f, gs5[0], gs5[1], gs5[2], gs5[3], gs5[4], gs5[5], gs5[6], gs5[7],
                   db_glu, dg_final, loss_part]
    n_small = sum(int(np.prod(p.shape)) for p in small_parts)
    rows = -(-n_small // (8 * 128)) * 8
    small = _pack_small(small_parts, rows)
    from_sib = _comm_pair(planes)
    core = lax.axis_index("c")
    chip_planes = []
    for name, p, s in zip(("w_in", "w_glu", "w_up_a", "w_up_b", "w_out"), planes, from_sib):
        tr = 256 if s.shape[1] % 256 == 0 else s.shape[1]
        chip_planes.append(_pair_sum("pair_sum_" + name, p, s, core, tr))
    recv, small_all = _comm_out(chip_planes, small)

    grads, deltas, new_m, new_v = {}, {}, {}, {}

    def put(name, res, shape):
        grads[name], deltas[name], new_m[name], new_v[name] = [r.reshape(shape) for r in res]

    names = ("w_in", "w_glu", "w_up_a", "w_up_b", "w_out")
    for name, pr in zip(names, recv):
        w2 = weights[name][0]
        tr = 256 if w2.shape[0] % 256 == 0 else w2.shape[0]
        put(name, _adamw("adamw_" + name, pr, w2, mom_m[name][0], mom_v[name][0], tr), weights[name].shape)
    cols = w_ada.shape[2]
    dmod_all = small_all[:, :24, :].reshape(N_DEV, 3 * D_MODEL)
    dmod_cols = lax.dynamic_slice_in_dim(dmod_all, me * cols, cols, axis=1)
    g_wada = _wada_grad(c_all, dmod_cols)
    put("w_ada", _adamw("adamw_w_ada", g_wada, w_ada[0], m_w_ada[0], v_w_ada[0], 256), w_ada.shape)
    pack = lambda d: _pack_small([d[n] for n in SMALL_ORDER] + [jnp.zeros((1,), F32)], rows)
    res_small = _adamw("adamw_small", small_all, pack(weights), pack(mom_m), pack(mom_v), rows)
    flat = [r.reshape(-1) for r in res_small]
    off = 0
    for name in SMALL_ORDER:
        shape = weights[name].shape
        size = int(np.prod(shape))
        put(name, [f[off:off + size] for f in flat], shape)
        off += size
    loss = flat[0][off]

    return (loss, grad_x[None], *[grads[n] for n in ALL_ORDER], *[deltas[n] for n in ALL_ORDER],
            *[new_m[n] for n in ALL_ORDER], *[new_v[n] for n in ALL_ORDER])
```

```python
import math

import jax
import jax.numpy as jnp
import numpy as np
from jax import lax
from jax.experimental import pallas as pl
from jax.experimental.pallas import tpu as pltpu

F32 = jnp.float32
MXU_DTYPE = jnp.bfloat16
HI = lax.Precision.HIGHEST

N_DEV = 8
D_MODEL = 1024
WIDTH = 512
HEADS = 8
PAIRS = HEADS // 2
GROUPS = 32
STATE = 64
CG = 16
CHUNK = 16
EPS = 1e-6
NEG = float(np.finfo(np.float32).min)

ADAM_LR = 0.001
ADAM_B1 = 0.9
ADAM_B2 = 0.999
ADAM_EPS = 1e-08
ADAM_WD = 0.01
ADAM_STEP = 10

VMEM_BIG = 56 * 1024 * 1024
VMEM_MID = 40 * 1024 * 1024

OFF_F = 3 * WIDTH
PROJ_WIDTH = 5128
M_Q, M_K, M_V, M_ZA, M_U, M_ZB, M_GA, M_GB = 0, 512, 1024, 1536, 2048, 2560, 3072, 4096


def _mm(a, b):
    return jnp.dot(a.astype(MXU_DTYPE), b.astype(MXU_DTYPE), preferred_element_type=F32)


def _mm_nt(a, b):
    return lax.dot_general(a.astype(MXU_DTYPE), b.astype(MXU_DTYPE), (((1,), (1,)), ((), ())),
                           preferred_element_type=F32)


def _mm_tn(a, b):
    return lax.dot_general(a.astype(MXU_DTYPE), b.astype(MXU_DTYPE), (((0,), (0,)), ((), ())),
                           preferred_element_type=F32)


def _mm32(a, b):
    return jnp.dot(a, b, precision=HI, preferred_element_type=F32)


def _mm32_nt(a, b):
    return lax.dot_general(a, b, (((1,), (1,)), ((), ())), precision=HI, preferred_element_type=F32)


def _mm32_tn(a, b):
    return lax.dot_general(a, b, (((0,), (0,)), ((), ())), precision=HI, preferred_element_type=F32)


S5_PRECISION = lax.Precision.HIGH


def _s5mm(a, b):
    return jnp.dot(a, b, precision=S5_PRECISION, preferred_element_type=F32)


def _s5mm_nt(a, b):
    return lax.dot_general(a, b, (((1,), (1,)), ((), ())), precision=S5_PRECISION, preferred_element_type=F32)


def _s5mm_tn(a, b):
    return lax.dot_general(a, b, (((0,), (0,)), ((), ())), precision=S5_PRECISION, preferred_element_type=F32)


def _sigmoid(x):
    return 1.0 / (1.0 + jnp.exp(-x))


def _params(sem=None, vmem=None):
    kw = {}
    if sem is not None:
        kw["dimension_semantics"] = sem
    if vmem is not None:
        kw["vmem_limit_bytes"] = vmem
    return pltpu.CompilerParams(**kw)


def _my_index():
    return 4 * lax.axis_index("x") + 2 * lax.axis_index("y") + lax.axis_index("c")


def _dev(p):
    return (p // 4, (p // 2) % 2, p % 2)


ANY = pl.BlockSpec(memory_space=pl.ANY)
VMEM = pl.BlockSpec(memory_space=pltpu.VMEM)
MESH = pl.DeviceIdType.MESH


def _comm_in(c, w_ada, b_ada8, shards):
    n = len(shards)
    cols = w_ada.shape[1]

    def body(c_ref, wada_ref, bada_ref, *rest):
        srcs = rest[:n]
        mod_ref, call_ref = rest[n], rest[n + 1]
        dsts = rest[n + 2:2 * n + 2]
        modp, wsend, wrecv, wloc, csend, crecv, msend, mrecv = rest[2 * n + 2:]
        me = _my_index()

        x, y, cc = lax.axis_index("x"), lax.axis_index("y"), lax.axis_index("c")
        here, sib = (x, y, cc), (x, y, 1 - cc)
        xn, yn, dg = (1 - x, y, cc), (x, 1 - y, cc), (1 - x, 1 - y, cc)
        half = srcs[0].shape[0] // 2
        parts = [(0, pl.ds(0, half)), (0, pl.ds(half, half))] + [(a, None) for a in range(1, n)]
        via_y = [i % 2 == 0 for i in range(len(parts))]

        def wcopy(i, k, block, to, own=False):
            a, rs = parts[i]
            dst = dsts[a].at[4 * block[0] + 2 * block[1] + block[2]]
            src = srcs[a] if own else dst
            if rs is not None:
                src, dst = src.at[rs], dst.at[rs]
            return pltpu.make_async_remote_copy(src_ref=src, dst_ref=dst,
                                                send_sem=wsend.at[i, k], recv_sem=wrecv.at[i, k],
                                                device_id=to, device_id_type=MESH)

        def ccopy(src_dev, d, to):
            return pltpu.make_async_remote_copy(src_ref=c_ref, dst_ref=call_ref.at[pl.ds(src_dev, 1)],
                                                send_sem=csend.at[d], recv_sem=crecv.at[src_dev],
                                                device_id=_dev(to), device_id_type=MESH)

        def mcopy(src_dev, d, to):
            return pltpu.make_async_remote_copy(src_ref=modp.at[pl.ds(to, 1)], dst_ref=mod_ref.at[pl.ds(src_dev, 1)],
                                                send_sem=msend.at[d], recv_sem=mrecv.at[src_dev],
                                                device_id=_dev(to), device_id_type=MESH)

        local = [pltpu.make_async_copy(srcs[a], dsts[a].at[me], wloc.at[a]) for a in range(n)]
        for cp in local:
            cp.start()
        peers = [(me + d) % N_DEV for d in range(1, N_DEV)]
        sends = []
        for i in range(len(parts)):
            sends += [wcopy(i, 0, here, sib, own=True), wcopy(i, 1, here, xn, own=True), wcopy(i, 2, here, yn, own=True)]
        for cp in sends:
            cp.start()
        call_ref[pl.ds(me, 1), :] = c_ref[...]
        for d, p in enumerate(peers):
            ccopy(me, d, p).start()
        for d, p in enumerate(peers):
            ccopy(p, d, p).wait_recv()
        modp[...] = _mm32(call_ref[...], wada_ref[...]) + bada_ref[pl.ds(me, 1), :]
        mod_ref[pl.ds(me, 1), :] = modp[pl.ds(me, 1), :]
        for d, p in enumerate(peers):
            mcopy(me, d, p).start()
        for d, p in enumerate(peers):
            mcopy(p, d, p).wait_recv()
        def after(i, k, block, nxt):
            wcopy(i, k, block, here).wait_recv()
            for kk, to in nxt:
                cp = wcopy(i, kk, block, to)
                cp.start()
                sends.append(cp)

        for i in range(len(parts)):
            after(i, 1, xn, [(3, sib)] + ([(5, yn)] if via_y[i] else []))
        for i in range(len(parts)):
            after(i, 2, yn, [(4, sib)] + ([] if via_y[i] else [(6, xn)]))
        for i in range(len(parts)):
            after(i, 5 if via_y[i] else 6, dg, [(7, sib)])
        for i in range(len(parts)):
            wcopy(i, 0, sib, here).wait_recv()
            for k, block in ((3, xn), (4, yn), (7, dg)):
                wcopy(i, k, (block[0], block[1], 1 - cc), here).wait_recv()
        for cp in sends:
            cp.wait_send()
        for d, p in enumerate(peers):
            ccopy(me, d, p).wait_send()
            mcopy(me, d, p).wait_send()
        for cp in local:
            cp.wait()

    out_shape = ([jax.ShapeDtypeStruct((N_DEV, cols), F32), jax.ShapeDtypeStruct((N_DEV, D_MODEL), F32)]
                 + [jax.ShapeDtypeStruct((N_DEV,) + s.shape, s.dtype) for s in shards])
    res = pl.pallas_call(
        body, name="comm_in", out_shape=out_shape,
        in_specs=[VMEM, VMEM, VMEM] + [ANY] * n,
        out_specs=[VMEM, VMEM] + [ANY] * n,
        scratch_shapes=[pltpu.VMEM((N_DEV, cols), F32),
                        pltpu.SemaphoreType.DMA((n + 1, N_DEV)), pltpu.SemaphoreType.DMA((n + 1, N_DEV)),
                        pltpu.SemaphoreType.DMA((n,)),
                        pltpu.SemaphoreType.DMA((N_DEV,)), pltpu.SemaphoreType.DMA((N_DEV,)),
                        pltpu.SemaphoreType.DMA((N_DEV,)), pltpu.SemaphoreType.DMA((N_DEV,))],
        compiler_params=_params(vmem=VMEM_MID),
    )(c, w_ada, b_ada8, *shards)
    return res[0], res[1], list(res[2:])


N_CHIP = 4


def _comm_pair(planes):
    n = len(planes)

    def body(*rest):
        srcs, dsts = rest[:n], rest[n:2 * n]
        send, recv = rest[2 * n:]
        x, y, cc = lax.axis_index("x"), lax.axis_index("y"), lax.axis_index("c")
        copies = [pltpu.make_async_remote_copy(src_ref=srcs[a].at[2 * ch + 1 - cc], dst_ref=dsts[a].at[ch],
                                               send_sem=send.at[a, ch], recv_sem=recv.at[a, ch],
                                               device_id=(x, y, 1 - cc), device_id_type=MESH)
                  for a in range(n) for ch in range(N_CHIP)]
        for cp in copies:
            cp.start()
        for cp in copies:
            cp.wait()

    out_shape = [jax.ShapeDtypeStruct((N_CHIP,) + p.shape[1:], p.dtype) for p in planes]
    return pl.pallas_call(
        body, name="comm_pair", out_shape=out_shape, in_specs=[ANY] * n, out_specs=[ANY] * n,
        scratch_shapes=[pltpu.SemaphoreType.DMA((n, N_CHIP)), pltpu.SemaphoreType.DMA((n, N_CHIP))],
    )(*planes)


def _pair_sum(name, planes, from_sib, core, tr):
    _, R, C = from_sib.shape

    def body(core_ref, a_ref, b_ref, o_ref):
        del core_ref
        o_ref[...] = (a_ref[...].astype(F32) + b_ref[...].astype(F32)).astype(o_ref.dtype)

    blk = pl.BlockSpec((None, tr, C), lambda i, j, c: (i, j, 0))
    grid_spec = pltpu.PrefetchScalarGridSpec(
        num_scalar_prefetch=1, grid=(N_CHIP, R // tr),
        in_specs=[pl.BlockSpec((None, tr, C), lambda i, j, c: (2 * i + c[0], j, 0)), blk], out_specs=blk)
    return pl.pallas_call(
        body, name=name, grid_spec=grid_spec, out_shape=jax.ShapeDtypeStruct(from_sib.shape, from_sib.dtype),
        compiler_params=_params(("parallel", "parallel"), VMEM_MID),
    )(core.reshape(1).astype(jnp.int32), planes, from_sib)


def _comm_out(chip_planes, small):
    n = len(chip_planes)
    rows = small.shape[0]
    cut = (rows // 16) * 8

    def body(*rest):
        srcs = rest[:n]
        small_ref = rest[n]
        dsts = rest[n + 1:2 * n + 1]
        sall_ref = rest[2 * n + 1]
        wsend, wrecv, wloc, ssend, srecv, sloc = rest[2 * n + 2:]
        me = _my_index()
        x, y, cc = lax.axis_index("x"), lax.axis_index("y"), lax.axis_index("c")
        mine = 2 * x + y
        chips = [(1 - x, y), (x, 1 - y), (1 - x, 1 - y)]
        here, sib = (x, y, cc), (x, y, 1 - cc)
        xn, yn, dg = (1 - x, y, cc), (x, 1 - y, cc), (1 - x, 1 - y, cc)
        sparts = [pl.ds(0, cut), pl.ds(cut, rows - cut)]
        via_y = [True, False]

        def wcopy(a, j, sending):
            chip = chips[j]
            there = 2 * chip[0] + chip[1]
            return pltpu.make_async_remote_copy(src_ref=srcs[a].at[there], dst_ref=dsts[a].at[mine if sending else there],
                                                send_sem=wsend.at[a, j], recv_sem=wrecv.at[a, j],
                                                device_id=(*chip, cc), device_id_type=MESH)

        def scopy(i, k, block, to, own=False):
            dst = sall_ref.at[4 * block[0] + 2 * block[1] + block[2]].at[sparts[i]]
            src = small_ref.at[sparts[i]] if own else dst
            return pltpu.make_async_remote_copy(src_ref=src, dst_ref=dst, send_sem=ssend.at[i, k], recv_sem=srecv.at[i, k],
                                                device_id=to, device_id_type=MESH)

        local = [pltpu.make_async_copy(srcs[a].at[mine], dsts[a].at[mine], wloc.at[a]) for a in range(n)]
        local.append(pltpu.make_async_copy(small_ref, sall_ref.at[me], sloc))
        for cp in local:
            cp.start()
        sends = []
        for i in range(len(sparts)):
            sends += [scopy(i, 0, here, sib, own=True), scopy(i, 1, here, xn, own=True), scopy(i, 2, here, yn, own=True)]
        for j in range(len(chips)):
            sends += [wcopy(a, j, True) for a in range(n)]
        for cp in sends:
            cp.start()

        def after(i, k, block, nxt):
            scopy(i, k, block, here).wait_recv()
            for kk, to in nxt:
                cp = scopy(i, kk, block, to)
                cp.start()
                sends.append(cp)

        for i in range(len(sparts)):
            after(i, 1, xn, [(3, sib)] + ([(5, yn)] if via_y[i] else []))
        for i in range(len(sparts)):
            after(i, 2, yn, [(4, sib)] + ([] if via_y[i] else [(6, xn)]))
        for i in range(len(sparts)):
            after(i, 5 if via_y[i] else 6, dg, [(7, sib)])
        for i in range(len(sparts)):
            scopy(i, 0, sib, here).wait_recv()
            for k, block in ((3, xn), (4, yn), (7, dg)):
                scopy(i, k, (block[0], block[1], 1 - cc), here).wait_recv()
        for j in range(len(chips)):
            for a in range(n):
                wcopy(a, j, False).wait_recv()
        for cp in sends:
            cp.wait_send()
        for cp in local:
            cp.wait()

    out_shape = ([jax.ShapeDtypeStruct(p.shape, p.dtype) for p in chip_planes]
                 + [jax.ShapeDtypeStruct((N_DEV,) + small.shape, small.dtype)])
    res = pl.pallas_call(
        body, name="comm_out", out_shape=out_shape,
        in_specs=[ANY] * (n + 1), out_specs=[ANY] * (n + 1),
        scratch_shapes=[pltpu.SemaphoreType.DMA((n, N_CHIP)), pltpu.SemaphoreType.DMA((n, N_CHIP)),
                        pltpu.SemaphoreType.DMA((n,)),
                        pltpu.SemaphoreType.DMA((2, N_DEV)), pltpu.SemaphoreType.DMA((2, N_DEV)),
                        pltpu.SemaphoreType.DMA(())],
    )(*chip_planes, small)
    return list(res[:n]), res[n]


def _proj_fwd(x, shift, scale, g_norm, w_main, w_f, ts):
    S = x.shape[0]

    def body(x_ref, sh_ref, sc_ref, gn_ref, w_ref, wf_ref,
             q_ref, k_ref, v_ref, za_ref, u_ref, zb_ref, ga_ref, gb_ref, flc_ref, h_ref):
        xv = x_ref[...]
        r = lax.rsqrt(jnp.mean(xv * xv, axis=-1, keepdims=True) + EPS)
        h = (xv * r) * gn_ref[...] * (1.0 + sc_ref[...]) + sh_ref[...]
        hb = h.astype(MXU_DTYPE)
        h_ref[...] = hb

        def seg(off, n):
            return jnp.dot(hb, w_ref[:, off:off + n], preferred_element_type=F32)

        q_ref[...] = (seg(M_Q, WIDTH) * 0.125).astype(q_ref.dtype)
        k_ref[...] = seg(M_K, WIDTH).astype(k_ref.dtype)
        v_ref[...] = seg(M_V, WIDTH).astype(v_ref.dtype)
        za_ref[...] = seg(M_ZA, WIDTH)
        u_ref[...] = seg(M_U, WIDTH)
        zb_ref[...] = seg(M_ZB, WIDTH)
        ga_ref[...] = seg(M_GA, D_MODEL)
        gb_ref[...] = seg(M_GB, D_MODEL)
        flc_ref[...] = _mm32(h, wf_ref[...])

    row = lambda n: pl.BlockSpec((ts, n), lambda i: (i, 0))
    full = lambda a: pl.BlockSpec(a.shape, lambda i: (0,) * a.ndim)
    sds = jax.ShapeDtypeStruct
    return pl.pallas_call(
        body, name="proj_fwd", grid=(S // ts,),
        in_specs=[row(D_MODEL), full(shift), full(scale), full(g_norm), full(w_main), full(w_f)],
        out_specs=[row(WIDTH)] * 6 + [row(D_MODEL)] * 2 + [row(HEADS), row(D_MODEL)],
        out_shape=[sds((S, WIDTH), MXU_DTYPE)] * 3 + [sds((S, WIDTH), F32)] * 3 + [sds((S, D_MODEL), F32)] * 2
                  + [sds((S, HEADS), F32), sds((S, D_MODEL), MXU_DTYPE)],
        compiler_params=_params(("parallel",), VMEM_BIG),
    )(x, shift, scale, g_norm, w_main, w_f)


def _log_sigmoid(z):
    return jnp.minimum(z, 0.0) - jnp.log(1.0 + jnp.exp(-jnp.abs(z)))


def _fgate_bwd(dfc, flc, bf_row, ts):
    S = flc.shape[0]
    n = S // ts

    def body(df_ref, flc_ref, bfr_ref, dfl_ref, dbf_ref, carry):
        @pl.when(pl.program_id(0) == 0)
        def _():
            carry[...] = jnp.zeros_like(carry)
            dbf_ref[...] = jnp.zeros_like(dbf_ref)

        ri = lax.broadcasted_iota(jnp.int32, (ts, ts), 0)
        ci = lax.broadcasted_iota(jnp.int32, (ts, ts), 1)
        upper = (ci >= ri).astype(F32)
        rc = _mm32(upper, df_ref[...]) + carry[...]
        carry[...] = rc[0:1, :]
        z = flc_ref[...] + bfr_ref[...]
        dfl = rc * _sigmoid(-z)
        dfl_ref[...] = dfl
        dbf_ref[...] += jnp.sum(dfl, axis=0, keepdims=True)

    col = pl.BlockSpec((ts, HEADS), lambda i: (n - 1 - i, 0))
    one = pl.BlockSpec((1, HEADS), lambda i: (0, 0))
    return pl.pallas_call(
        body, name="fgate_bwd", grid=(n,),
        in_specs=[col, col, one], out_specs=[col, one],
        out_shape=[jax.ShapeDtypeStruct((S, HEADS), F32), jax.ShapeDtypeStruct((1, HEADS), F32)],
        scratch_shapes=[pltpu.VMEM((1, HEADS), F32)],
        compiler_params=_params(("arbitrary",)),
    )(dfc, flc, bf_row)


N_EXTRA = 3


def _attn_prep(q, k, v, flc, bf_row, t):
    S = q.shape[0]
    nb = S // t

    def body(q_ref, k_ref, v_ref, flc_ref, bfr_ref, qh_ref, kh_ref, vt_ref, carry):
        @pl.when(pl.program_id(0) == 0)
        def _():
            carry[...] = jnp.zeros_like(carry)

        ri = lax.broadcasted_iota(jnp.int32, (t, t), 0)
        ci = lax.broadcasted_iota(jnp.int32, (t, t), 1)
        f = _mm32((ci <= ri).astype(F32), _log_sigmoid(flc_ref[...] + bfr_ref[...])) + carry[...]
        carry[...] = f[t - 1:t, :]
        lane = lax.broadcasted_iota(jnp.int32, (t, 128), 1)
        for p in range(PAIRS):
            qp = q_ref[:, p * 128:(p + 1) * 128]
            kp = k_ref[:, p * 128:(p + 1) * 128]
            vt_ref[p, 0] = v_ref[:, p * 128:(p + 1) * 128].T
            for h in range(2):
                own = (lane < 64) if h == 0 else (lane >= 64)
                base = 64 if h == 0 else 0
                fh = f[:, 2 * p + h:2 * p + h + 1]
                parts = []
                rest = fh
                for _ in range(N_EXTRA):
                    part = rest.astype(qh_ref.dtype)
                    parts.append(part)
                    rest = rest - part.astype(F32)
                one = jnp.ones((t, 1), qh_ref.dtype)
                eq = jnp.zeros((t, 128), qh_ref.dtype)
                ek = jnp.zeros((t, 128), qh_ref.dtype)
                for j in range(N_EXTRA):
                    eq = jnp.where(lane == base + j, parts[j], eq)
                    eq = jnp.where(lane == base + N_EXTRA + j, one, eq)
                    ek = jnp.where(lane == base + j, one, ek)
                    ek = jnp.where(lane == base + N_EXTRA + j, -parts[j], ek)
                qh_ref[2 * p + h] = jnp.where(own, qp, eq)
                kh_ref[2 * p + h] = jnp.where(own, kp, ek)

    row = pl.BlockSpec((t, WIDTH), lambda i: (i, 0))
    heads = pl.BlockSpec((HEADS, t, 128), lambda i: (0, i, 0))
    return pl.pallas_call(
        body, name="attn_prep", grid=(nb,),
        in_specs=[row, row, row, pl.BlockSpec((t, HEADS), lambda i: (i, 0)), pl.BlockSpec((1, HEADS), lambda i: (0, 0))],
        out_specs=[heads, heads, pl.BlockSpec((PAIRS, 1, 128, t), lambda i: (0, i, 0, 0))],
        out_shape=[jax.ShapeDtypeStruct((HEADS, S, 128), q.dtype), jax.ShapeDtypeStruct((HEADS, S, 128), k.dtype),
                   jax.ShapeDtypeStruct((PAIRS, nb, 128, t), v.dtype)],
        scratch_shapes=[pltpu.VMEM((1, HEADS), F32)],
        compiler_params=_params(("arbitrary",), VMEM_MID),
    )(q, k, v, flc, bf_row)


def _attn_fwd(qh, kh, vt, t):
    S = qh.shape[1]
    nb = S // t

    def body(q_ref, k_ref, vt_ref, o_ref, lse_ref, acc_s):
        qi = pl.program_id(1)
        acc_s[...] = jnp.zeros_like(acc_s)

        def step(ki, nblk, masked, carry):
            m_old, l_old = carry[:2], carry[2:]
            ks = pl.multiple_of(ki * t, t)
            rows = nblk * t
            sts = [_mm_nt(k_ref[h, pl.ds(ks, rows), :], q_ref[h]) for h in range(2)]
            if masked:
                ri = lax.broadcasted_iota(jnp.int32, (t, t), 0)
                ci = lax.broadcasted_iota(jnp.int32, (t, t), 1)
                sts = [jnp.where(ci >= ri, st, NEG) for st in sts]
            m_new = [jnp.maximum(m_old[h], jnp.max(sts[h], axis=0, keepdims=True)) for h in range(2)]
            alpha = [jnp.exp(m_old[h] - m_new[h]) for h in range(2)]
            pts = [jnp.exp(sts[h] - m_new[h]) for h in range(2)]
            l_new = [alpha[h] * l_old[h] + jnp.sum(pts[h], axis=0, keepdims=True) for h in range(2)]
            for h in range(2):
                pv = _mm(vt_ref[ki], pts[h][:t])
                for b in range(1, nblk):
                    pv = pv + _mm(vt_ref[ki + b], pts[h][b * t:(b + 1) * t])
                acc_s[h] = alpha[h] * acc_s[h] + pv
            return (*m_new, *l_new)

        init = (jnp.full((1, t), -jnp.inf, F32),) * 2 + (jnp.zeros((1, t), F32),) * 2
        carry = lax.fori_loop(0, qi // 2, lambda j, c: step(2 * j, 2, False, c), init)
        carry = lax.cond(qi % 2 == 1, lambda c: step(qi - 1, 1, False, c), lambda c: c, carry)
        m0, m1, l0, l1 = step(qi, 1, True, carry)
        first = lax.broadcasted_iota(jnp.int32, (128, t), 0) < 64
        o_ref[...] = jnp.where(first, acc_s[0] / l0, acc_s[1] / l1).T
        lse_ref[...] = jnp.concatenate([m0 + jnp.log(l0), m1 + jnp.log(l1)], axis=0)

    return pl.pallas_call(
        body, name="attn_fwd", grid=(PAIRS, nb),
        in_specs=[pl.BlockSpec((2, t, 128), lambda p, i: (p, i, 0)), pl.BlockSpec((2, S, 128), lambda p, i: (p, 0, 0)),
                  pl.BlockSpec((None, nb, 128, t), lambda p, i: (p, 0, 0, 0))],
        out_specs=[pl.BlockSpec((t, 128), lambda p, i: (i, p)), pl.BlockSpec((None, None, 2, t), lambda p, i: (p, i, 0, 0))],
        out_shape=[jax.ShapeDtypeStruct((S, WIDTH), F32), jax.ShapeDtypeStruct((PAIRS, nb, 2, t), F32)],
        scratch_shapes=[pltpu.VMEM((2, 128, t), F32)],
        compiler_params=_params(("parallel", "parallel"), VMEM_MID),
    )(qh, kh, vt)


def _attn_bwd(qh, do, kh, v, lse4, dl4, t):
    S = qh.shape[1]
    nb = S // t

    def body(q_ref, do_ref, k_ref, v_ref, lse_ref, dl_ref,
             dq_ref, dk_ref, dv_ref, dfq_ref, dfk_ref, kc_s, vh_s, dk_s, dv_s, dfk_s):
        kj = pl.program_id(1)
        lane = lax.broadcasted_iota(jnp.int32, (t, 128), 1)
        is_a = lane < 64

        @pl.when(kj == 0)
        def _():
            dq_ref[...] = jnp.zeros_like(dq_ref)
            dfq_ref[...] = jnp.zeros_like(dfq_ref)

        vp = v_ref[...]
        zero = jnp.zeros_like(vp)
        kc_s[0] = jnp.where(is_a, k_ref[0], zero.astype(kc_s.dtype))
        kc_s[1] = jnp.where(is_a, zero.astype(kc_s.dtype), k_ref[1])
        vh_s[0] = jnp.where(is_a, vp, zero)
        vh_s[1] = jnp.where(is_a, zero, vp)
        dk_s[...] = jnp.zeros_like(dk_s)
        dv_s[...] = jnp.zeros_like(dv_s)
        dfk_s[...] = jnp.zeros_like(dfk_s)

        def step(qi, masked):
            qs = pl.multiple_of(qi * t, t)
            dob = do_ref[pl.ds(qs, t), :]
            lse = lse_ref[qi]
            dl = dl_ref[qi]
            zq = jnp.zeros_like(dob)
            over_keys = []
            for h in range(2):
                sel = is_a if h == 0 else jnp.logical_not(is_a)
                qb = q_ref[h, pl.ds(qs, t), :]
                st = _mm_nt(k_ref[h], qb) - lse[h:h + 1, :]
                if masked:
                    ri = lax.broadcasted_iota(jnp.int32, (t, t), 0)
                    ci = lax.broadcasted_iota(jnp.int32, (t, t), 1)
                    st = jnp.where(ci >= ri, st, NEG)
                pt = jnp.exp(st)
                dv_s[...] += _mm(pt, jnp.where(sel, dob, zq))
                dpt = _mm_nt(vh_s[h], dob)
                dst = pt * (dpt - dl[h:h + 1, :])
                dfk_s[h] += jnp.sum(dst, axis=1, keepdims=True)
                over_keys.append(jnp.sum(dst, axis=0, keepdims=True))
                dk_s[...] += _mm(dst, jnp.where(sel, qb, jnp.zeros_like(qb)))
                dq_ref[pl.ds(qs, t), :] += _mm_tn(dst, kc_s[h])
            dfq_ref[qi] += jnp.concatenate(over_keys, axis=0)

        step(kj, True)

        def loop_body(qi, carry):
            step(qi, False)
            return carry

        lax.fori_loop(kj + 1, nb, loop_body, 0)
        dk_ref[...] = dk_s[...].astype(dk_ref.dtype)
        dv_ref[...] = dv_s[...].astype(dv_ref.dtype)
        dfk_ref[...] = jnp.where(lax.broadcasted_iota(jnp.int32, (t, 2), 1) == 0, dfk_s[0], dfk_s[1])

        @pl.when(kj == nb - 1)
        def _():
            dq_ref[...] = dq_ref[...] * 0.125

    blk = pl.BlockSpec((t, 128), lambda p, j: (j, p))
    res = pl.BlockSpec((S, 128), lambda p, j: (0, p))
    rows4 = pl.BlockSpec((None, nb, 2, t), lambda p, j: (p, 0, 0, 0))
    cols4 = pl.BlockSpec((None, t, 2), lambda p, j: (p, j, 0))
    return pl.pallas_call(
        body, name="attn_bwd", grid=(PAIRS, nb),
        in_specs=[pl.BlockSpec((2, S, 128), lambda p, j: (p, 0, 0)), res,
                  pl.BlockSpec((2, t, 128), lambda p, j: (p, j, 0)), blk, rows4, rows4],
        out_specs=[res, blk, blk, rows4, cols4],
        out_shape=[jax.ShapeDtypeStruct((S, WIDTH), F32), jax.ShapeDtypeStruct((S, WIDTH), MXU_DTYPE),
                   jax.ShapeDtypeStruct((S, WIDTH), MXU_DTYPE), jax.ShapeDtypeStruct((PAIRS, nb, 2, t), F32),
                   jax.ShapeDtypeStruct((PAIRS, S, 2), F32)],
        scratch_shapes=[pltpu.VMEM((2, t, 128), kh.dtype), pltpu.VMEM((2, t, 128), v.dtype),
                        pltpu.VMEM((t, 128), F32), pltpu.VMEM((t, 128), F32), pltpu.VMEM((2, t, 1), F32)],
        compiler_params=_params(("parallel", "arbitrary"), VMEM_MID),
    )(qh, do, kh, v, lse4, dl4)


def _s5_mats(a_re, a_im, log_dt, b_re, b_im, c_re, c_im, d_skip):
    Lc = CHUNK
    dt = jnp.exp(log_dt)[:, None]
    lr, li = a_re * dt, a_im * dt

    def apow(n):
        n = jnp.asarray(n, F32)[None, :, None]
        mag = jnp.exp(n * lr[:, None, :])
        ang = n * li[:, None, :]
        return mag * jnp.cos(ang), mag * jnp.sin(ang)

    ar, ai = apow([1.0])
    ar, ai = ar[:, 0], ai[:, 0]
    den = a_re * a_re + a_im * a_im
    nr, ni = ar - 1.0, ai
    fr = (nr * a_re + ni * a_im) / den
    fi = (ni * a_re - nr * a_im) / den
    bbr = fr[:, :, None] * b_re - fi[:, :, None] * b_im
    bbi = fr[:, :, None] * b_im + fi[:, :, None] * b_re
    steps = np.arange(Lc, dtype=np.float32)
    pr, pi = apow(steps)
    car = c_re[:, None] * pr[:, :, None, :] - c_im[:, None] * pi[:, :, None, :]
    cai = c_re[:, None] * pi[:, :, None, :] + c_im[:, None] * pr[:, :, None, :]
    kern = (jnp.einsum('glcp,gpd->glcd', car, bbr, precision=HI)
            - jnp.einsum('glcp,gpd->glcd', cai, bbi, precision=HI))
    skip = d_skip.reshape(GROUPS, CG)[:, :, None] * jnp.eye(CG, dtype=F32)[None]
    kern = kern.at[:, 0].add(skip)
    trow = kern.transpose(0, 3, 1, 2).reshape(GROUPS, CG, Lc * CG)
    p1r, p1i = apow(steps + 1.0)
    cr = c_re[:, None] * p1r[:, :, None, :] - c_im[:, None] * p1i[:, :, None, :]
    ci = c_re[:, None] * p1i[:, :, None, :] + c_im[:, None] * p1r[:, :, None, :]
    to_rows = lambda m: m.transpose(0, 3, 1, 2).reshape(GROUPS, STATE, Lc * CG)
    camat = jnp.concatenate([to_rows(cr), -to_rows(ci)], axis=1)
    qr, qi = apow(Lc - 1.0 - steps)
    zr = qr[:, :, None, :] * bbr.transpose(0, 2, 1)[:, None] - qi[:, :, None, :] * bbi.transpose(0, 2, 1)[:, None]
    zi = qr[:, :, None, :] * bbi.transpose(0, 2, 1)[:, None] + qi[:, :, None, :] * bbr.transpose(0, 2, 1)[:, None]
    bzmat = jnp.concatenate([zr, zi], axis=-1).reshape(GROUPS, Lc * CG, 2 * STATE)
    lr_, li_ = apow([float(Lc)])
    al = jnp.concatenate([lr_[:, 0], li_[:, 0]], axis=-1)
    return trow, camat, bzmat, al


def _s5_scan_powers(a_re, a_im, log_dt, n_steps):
    dt = jnp.exp(log_dt)[:, None]
    lr, li = a_re * dt, a_im * dt
    n = (CHUNK * 2.0 ** np.arange(n_steps)).astype(np.float32)[None, :, None]
    mag = jnp.exp(n * lr[:, None, :])
    pr, pi = mag * jnp.cos(n * li[:, None, :]), mag * jnp.sin(n * li[:, None, :])
    fwd = jnp.stack([jnp.concatenate([pr, pr], -1), jnp.concatenate([-pi, pi], -1)], axis=2)
    bwd = jnp.stack([jnp.concatenate([pr, pr], -1), jnp.concatenate([pi, -pi], -1)], axis=2)
    return fwd, bwd


def _shift_rows(x, sh, down):
    n = x.shape[0]
    ri = lax.broadcasted_iota(jnp.int32, x.shape, 0)
    if down:
        return jnp.where(ri >= sh, pltpu.roll(x, sh, 0), 0.0)
    return jnp.where(ri < n - sh, pltpu.roll(x, n - sh, 0), 0.0)


GPB = 128 // CG


def _lane_transpose(arrs):
    lane = lax.broadcasted_iota(jnp.int32, arrs[0].shape, 1)
    arrs = list(arrs)
    k = GPB // 2
    while k >= 1:
        hi = ((lane // CG) & k) != 0
        new = list(arrs)
        for i in range(GPB):
            if i & k:
                continue
            lo_arr, hi_arr = arrs[i], arrs[i + k]
            new[i] = jnp.where(hi, pltpu.roll(hi_arr, CG * k, 1), lo_arr)
            new[i + k] = jnp.where(hi, hi_arr, pltpu.roll(lo_arr, 128 - CG * k, 1))
        arrs = new
        k //= 2
    return arrs


def _gather_block(ref, dst, nch):
    for half in range(CHUNK // GPB):
        outs = _lane_transpose([ref[pl.ds(half * GPB + l8, nch, stride=CHUNK), :] for l8 in range(GPB)])
        for g in range(GPB):
            dst[half, g] = outs[g]


def _scatter_block(src, ref, nch):
    for half in range(CHUNK // GPB):
        outs = _lane_transpose([src[half, g] for g in range(GPB)])
        for l8 in range(GPB):
            ref[pl.ds(half * GPB + l8, nch, stride=CHUNK), :] = outs[l8]


def _toeplitz(trow):
    lane = lax.broadcasted_iota(jnp.int32, (CG, 128), 1)
    x0, x1 = trow[:, :128], trow[:, 128:]
    zero = jnp.zeros_like(x0)
    rows = []
    for s in range(CHUNK):
        sh = (CG * s) % 128
        r0 = pltpu.roll(x0, sh, 1) if sh else x0
        r1 = pltpu.roll(x1, sh, 1) if sh else x1
        if CG * s < 128:
            rows.append(jnp.concatenate([jnp.where(lane >= sh, r0, zero), jnp.where(lane >= sh, r1, r0)], axis=1))
        else:
            rows.append(jnp.concatenate([zero, jnp.where(lane >= sh, r0, zero)], axis=1))
    return jnp.concatenate(rows, axis=0)


def _toeplitz_adjoint(dt):
    lane = lax.broadcasted_iota(jnp.int32, (CG, 128), 1)
    acc0 = jnp.zeros((CG, 128), F32)
    acc1 = jnp.zeros((CG, 128), F32)
    for s in range(CHUNK):
        x0, x1 = dt[CG * s:CG * s + CG, :128], dt[CG * s:CG * s + CG, 128:]
        sh = (CG * s) % 128
        keep = 128 - sh
        r0 = pltpu.roll(x0, keep, 1) if sh else x0
        r1 = pltpu.roll(x1, keep, 1) if sh else x1
        if CG * s < 128:
            acc0 = acc0 + jnp.where(lane < keep, r0, r1)
            acc1 = acc1 + jnp.where(lane < keep, r1, 0.0)
        else:
            acc0 = acc0 + jnp.where(lane < keep, r1, 0.0)
    return jnp.concatenate([acc0, acc1], axis=1)


def _s5_fwd(u, trow, camat, bzmat, pw):
    S = u.shape[0]
    nch = S // CHUNK
    n_steps = pw.shape[1]

    def body(u_ref, t_ref, ca_ref, bz_ref, pw_ref, y_ref, xp_ref, uc_ref, ub_s, yb_s):
        g = pl.program_id(1)

        @pl.when(g == 0)
        def _():
            _gather_block(u_ref, ub_s, nch)

        uc = jnp.concatenate([ub_s[0, g], ub_s[1, g]], axis=1)
        uc_ref[...] = uc
        x = _s5mm(uc, bz_ref[...])
        for kk in range(n_steps):
            xs = _shift_rows(x, 2 ** kk, True)
            m = pw_ref[kk]
            x = x + m[0:1, :] * xs + m[1:2, :] * pltpu.roll(xs, STATE, 1)
        xp = _shift_rows(x, 1, True)
        xp_ref[...] = xp
        yc = _s5mm(uc, _toeplitz(t_ref[...])) + _s5mm(xp, ca_ref[...])
        yb_s[0, g] = yc[:, :128]
        yb_s[1, g] = yc[:, 128:]

        @pl.when(g == GPB - 1)
        def _():
            _scatter_block(yb_s, y_ref, nch)

    per = lambda a: pl.BlockSpec((None,) + a.shape[1:], lambda b, g: (b * GPB + g,) + (0,) * (a.ndim - 1))
    nat = pl.BlockSpec((S, 128), lambda b, g: (0, b))
    return pl.pallas_call(
        body, name="s5_fwd", grid=(GROUPS // GPB, GPB),
        in_specs=[nat, per(trow), per(camat), per(bzmat), per(pw)],
        out_specs=[nat, pl.BlockSpec((None, nch, 2 * STATE), lambda b, g: (b * GPB + g, 0, 0)),
                   pl.BlockSpec((None, nch, CHUNK * CG), lambda b, g: (b * GPB + g, 0, 0))],
        out_shape=[jax.ShapeDtypeStruct((S, GROUPS * CG), F32), jax.ShapeDtypeStruct((GROUPS, nch, 2 * STATE), F32),
                   jax.ShapeDtypeStruct((GROUPS, nch, CHUNK * CG), F32)],
        scratch_shapes=[pltpu.VMEM((CHUNK // GPB, GPB, nch, 128), F32)] * 2,
        compiler_params=_params(("parallel", "arbitrary"), VMEM_BIG),
    )(u, trow, camat, bzmat, pw)


def _s5_bwd(uc, dy, xp, trow, camat, bzmat, pwc):
    S = dy.shape[0]
    nch = S // CHUNK
    n_steps = pwc.shape[1]

    def body(uc_ref, dy_ref, xp_ref, t_ref, ca_ref, bz_ref, pw_ref, du_ref, dt_ref, dca_ref, dbz_ref, dal_ref,
             dyb_s, dub_s):
        g = pl.program_id(1)

        @pl.when(g == 0)
        def _():
            _gather_block(dy_ref, dyb_s, nch)

        uc = uc_ref[...]
        dyc = jnp.concatenate([dyb_s[0, g], dyb_s[1, g]], axis=1)
        xpv = xp_ref[...]
        dt_ref[...] = _toeplitz_adjoint(_s5mm_tn(uc, dyc))
        dca_ref[...] = _s5mm_tn(xpv, dyc)
        dx = _shift_rows(_s5mm_nt(dyc, ca_ref[...]), 1, False)
        for kk in range(n_steps):
            xs = _shift_rows(dx, 2 ** kk, False)
            m = pw_ref[kk]
            dx = dx + m[0:1, :] * xs + m[1:2, :] * pltpu.roll(xs, STATE, 1)
        dbz_ref[...] = _s5mm_tn(uc, dx)
        dal_ref[0:1, :] = jnp.sum(dx * xpv, axis=0, keepdims=True)
        dal_ref[1:2, :] = jnp.sum(dx * pltpu.roll(xpv, STATE, 1), axis=0, keepdims=True)
        duc = _s5mm_nt(dyc, _toeplitz(t_ref[...])) + _s5mm_nt(dx, bz_ref[...])
        dub_s[0, g] = duc[:, :128]
        dub_s[1, g] = duc[:, 128:]

        @pl.when(g == GPB - 1)
        def _():
            _scatter_block(dub_s, du_ref, nch)

    per = lambda a: pl.BlockSpec((None,) + a.shape[1:], lambda b, g: (b * GPB + g,) + (0,) * (a.ndim - 1))
    nat = pl.BlockSpec((S, 128), lambda b, g: (0, b))
    sds = jax.ShapeDtypeStruct
    mats = [sds(trow.shape, F32), sds(camat.shape, F32), sds(bzmat.shape, F32), sds((GROUPS, 2, 2 * STATE), F32)]
    return pl.pallas_call(
        body, name="s5_bwd", grid=(GROUPS // GPB, GPB),
        in_specs=[per(uc), nat, per(xp), per(trow), per(camat), per(bzmat), per(pwc)],
        out_specs=[nat] + [per(o) for o in mats], out_shape=[sds((S, GROUPS * CG), F32)] + mats,
        scratch_shapes=[pltpu.VMEM((CHUNK // GPB, GPB, nch, 128), F32)] * 2,
        compiler_params=_params(("parallel", "arbitrary"), VMEM_BIG),
    )(uc, dy, xp, trow, camat, bzmat, pwc)


GELU_C0 = math.sqrt(2.0 / math.pi)
GELU_C1 = 0.044715


def _mix(o, za, ys, zb, ga, gb, x, tgt, gate, b_glu, g_final, w_glu, w_up_a, w_up_b, w_out, hsel, ts):
    S = o.shape[0]

    def body(o_ref, za_ref, ys_ref, zb_ref, ga_ref, gb_ref, x_ref, t_ref, gate_ref, bglu_ref, gf_ref,
             wglu_ref, wua_ref, wub_ref, wout_ref, hsel_ref,
             dx2_ref, do_ref, dza_ref, dzb_ref, dga_ref, dgb_ref, dys_ref, dl_ref,
             mg_ref, dmo_ref, ya_ref, dua_ref, yb_ref, dub_ref, yg_ref, dgl_ref,
             dbglu_ref, dgate_ref, dgf_ref, loss_ref):
        @pl.when(pl.program_id(0) == 0)
        def _():
            dbglu_ref[...] = jnp.zeros_like(dbglu_ref)
            dgate_ref[...] = jnp.zeros_like(dgate_ref)
            dgf_ref[...] = jnp.zeros_like(dgf_ref)
            loss_ref[...] = jnp.zeros_like(loss_ref)

        ov = o_ref[...]
        za = za_ref[...]
        sza = _sigmoid(za)
        silu_a = za * sza
        ya = ov * silu_a
        ya_b = ya.astype(ya_ref.dtype)
        ya_ref[...] = ya_b
        ysv = ys_ref[...]
        th = jnp.tanh(GELU_C0 * (ysv + GELU_C1 * ysv * ysv * ysv))
        yg = 0.5 * ysv * (1.0 + th)
        yg_b = yg.astype(yg_ref.dtype)
        yg_ref[...] = yg_b
        sg = _sigmoid(_mm(yg_b, wglu_ref[...]) + bglu_ref[...])
        yb1 = yg * sg
        zb = zb_ref[...]
        szb = _sigmoid(zb)
        silu_b = zb * szb
        yb_b = (yb1 * silu_b).astype(yb_ref.dtype)
        yb_ref[...] = yb_b
        ua = _mm(ya_b, wua_ref[...])
        ub = _mm(yb_b, wub_ref[...])
        sa = _sigmoid(ga_ref[...])
        sb = _sigmoid(gb_ref[...])
        merged_b = (sa * ua + sb * ub).astype(mg_ref.dtype)
        mg_ref[...] = merged_b
        mo = _mm(merged_b, wout_ref[...])
        gate_v = gate_ref[...]
        x2 = x_ref[...] + gate_v * mo
        r2 = lax.rsqrt(jnp.mean(x2 * x2, axis=-1, keepdims=True) + EPS)
        x2n = x2 * r2
        gf = gf_ref[...]
        diff = x2n * gf - t_ref[...]
        loss_ref[...] += jnp.sum(jnp.sum(diff * diff, axis=-1, keepdims=True), axis=0, keepdims=True) * (0.5 / D_MODEL)
        dy = diff * (1.0 / D_MODEL)
        dgf_ref[...] += jnp.sum(dy * x2n, axis=0, keepdims=True)
        dyg = dy * gf
        dx2 = r2 * (dyg - x2n * jnp.mean(dyg * x2n, axis=-1, keepdims=True))
        dx2_ref[...] = dx2
        dgate_ref[...] += jnp.sum(dx2 * mo, axis=0, keepdims=True)
        dmo_b = (dx2 * gate_v).astype(dmo_ref.dtype)
        dmo_ref[...] = dmo_b
        dmerged = _mm_nt(dmo_b, wout_ref[...])
        dua_b = (dmerged * sa).astype(dua_ref.dtype)
        dub_b = (dmerged * sb).astype(dub_ref.dtype)
        dua_ref[...] = dua_b
        dub_ref[...] = dub_b
        dga_ref[...] = (dmerged * ua * sa * (1.0 - sa)).astype(dga_ref.dtype)
        dgb_ref[...] = (dmerged * ub * sb * (1.0 - sb)).astype(dgb_ref.dtype)
        dya = _mm_nt(dua_b, wua_ref[...])
        dyb = _mm_nt(dub_b, wub_ref[...])
        dov = dya * silu_a
        do_ref[...] = dov.astype(do_ref.dtype)
        dl_ref[...] = _mm32_nt(hsel_ref[...], dov * ov)
        dza_ref[...] = (dya * ov * (sza * (1.0 + za * (1.0 - sza)))).astype(dza_ref.dtype)
        dyb1 = dyb * silu_b
        dzb_ref[...] = (dyb * yb1 * (szb * (1.0 + zb * (1.0 - szb)))).astype(dzb_ref.dtype)
        dgl = dyb1 * yg * sg * (1.0 - sg)
        dbglu_ref[...] += jnp.sum(dgl, axis=0, keepdims=True)
        dgl_b = dgl.astype(dgl_ref.dtype)
        dgl_ref[...] = dgl_b
        dyg2 = dyb1 * sg + _mm_nt(dgl_b, wglu_ref[...])
        dgelu = 0.5 * (1.0 + th) + 0.5 * ysv * (1.0 - th * th) * GELU_C0 * (1.0 + 3.0 * GELU_C1 * ysv * ysv)
        dys_ref[...] = dyg2 * dgelu

    row = lambda n: pl.BlockSpec((ts, n), lambda i: (i, 0))
    full = lambda a: pl.BlockSpec(a.shape, lambda i: (0,) * a.ndim)
    vec = lambda n: pl.BlockSpec((1, n), lambda i: (0, 0))
    sds = jax.ShapeDtypeStruct
    W, Dm = WIDTH, D_MODEL
    return pl.pallas_call(
        body, name="mix", grid=(S // ts,),
        in_specs=[row(W), row(W), row(W), row(W), row(Dm), row(Dm), row(Dm), row(Dm),
                  full(gate), full(b_glu), full(g_final), full(w_glu), full(w_up_a), full(w_up_b), full(w_out), full(hsel)],
        out_specs=[row(Dm), row(W), row(W), row(W), row(Dm), row(Dm), row(W), pl.BlockSpec((HEADS, ts), lambda i: (0, i)),
                   row(Dm), row(Dm), row(W), row(Dm), row(W), row(Dm), row(W), row(W),
                   vec(W), vec(Dm), vec(Dm), vec(1)],
        out_shape=[sds((S, Dm), F32), sds((S, W), MXU_DTYPE), sds((S, W), MXU_DTYPE), sds((S, W), MXU_DTYPE),
                   sds((S, Dm), MXU_DTYPE), sds((S, Dm), MXU_DTYPE), sds((S, W), F32), sds((HEADS, S), F32),
                   sds((S, Dm), MXU_DTYPE), sds((S, Dm), MXU_DTYPE), sds((S, W), MXU_DTYPE), sds((S, Dm), MXU_DTYPE),
                   sds((S, W), MXU_DTYPE), sds((S, Dm), MXU_DTYPE), sds((S, W), MXU_DTYPE), sds((S, W), MXU_DTYPE),
                   sds((1, W), F32), sds((1, Dm), F32), sds((1, Dm), F32), sds((1, 1), F32)],
        compiler_params=_params(("arbitrary",), VMEM_BIG),
    )(o, za, ys, zb, ga, gb, x, tgt, gate, b_glu, g_final, w_glu, w_up_a, w_up_b, w_out, hsel)


def _matmul_tn(name, a, b, ts):
    S, M = a.shape
    N = b.shape[1]
    tn = min(N, 1024)

    def body(a_ref, b_ref, o_ref):
        @pl.when(pl.program_id(1) == 0)
        def _():
            o_ref[...] = jnp.zeros_like(o_ref)

        o_ref[...] += _mm_tn(a_ref[...], b_ref[...])

    return pl.pallas_call(
        body, name=name, grid=(N // tn, S // ts),
        in_specs=[pl.BlockSpec((ts, M), lambda j, i: (i, 0)), pl.BlockSpec((ts, tn), lambda j, i: (i, j))],
        out_specs=pl.BlockSpec((M, tn), lambda j, i: (0, j)),
        out_shape=jax.ShapeDtypeStruct((M, N), F32),
        compiler_params=_params(("parallel", "arbitrary"), VMEM_MID),
    )(a, b)


def _proj_bwd(dq, dk, dv, dza, du, dzb, dga, dgb, dfl, x, dx2, shift, scale, g_norm, w_main, w_ft, ts):
    S = x.shape[0]

    def body(dq_ref, dk_ref, dv_ref, dza_ref, du_ref, dzb_ref, dga_ref, dgb_ref, dfl_ref, x_ref, dx2_ref,
             sc_ref, gn_ref, w_ref, wft_ref, gx_ref, dsh_ref, dsc_ref, dgn_ref):
        @pl.when(pl.program_id(0) == 0)
        def _():
            dsh_ref[...] = jnp.zeros_like(dsh_ref)
            dsc_ref[...] = jnp.zeros_like(dsc_ref)
            dgn_ref[...] = jnp.zeros_like(dgn_ref)

        def seg(ref, off, n):
            return _mm_nt(ref[...], w_ref[:, off:off + n])

        dh = (seg(dq_ref, M_Q, WIDTH) + seg(dk_ref, M_K, WIDTH) + seg(dv_ref, M_V, WIDTH)
              + seg(dza_ref, M_ZA, WIDTH) + seg(du_ref, M_U, WIDTH) + seg(dzb_ref, M_ZB, WIDTH)
              + seg(dga_ref, M_GA, D_MODEL) + seg(dgb_ref, M_GB, D_MODEL)
              + _mm32(dfl_ref[...], wft_ref[...]))
        xv = x_ref[...]
        r = lax.rsqrt(jnp.mean(xv * xv, axis=-1, keepdims=True) + EPS)
        xn = xv * r
        gn = gn_ref[...]
        s1 = 1.0 + sc_ref[...]
        dsh_ref[...] += jnp.sum(dh, axis=0, keepdims=True)
        dhx = dh * xn
        dsc_ref[...] += jnp.sum(dhx, axis=0, keepdims=True) * gn
        dgn_ref[...] += jnp.sum(dhx, axis=0, keepdims=True) * s1
        dxn = dh * (gn * s1)
        gx_ref[...] = dx2_ref[...] + r * (dxn - xn * jnp.mean(dxn * xn, axis=-1, keepdims=True))

    row = lambda n: pl.BlockSpec((ts, n), lambda i: (i, 0))
    full = lambda a: pl.BlockSpec(a.shape, lambda i: (0,) * a.ndim)
    vec = pl.BlockSpec((1, D_MODEL), lambda i: (0, 0))
    W, Dm = WIDTH, D_MODEL
    del shift
    return pl.pallas_call(
        body, name="proj_bwd", grid=(S // ts,),
        in_specs=[row(W)] * 6 + [row(Dm)] * 2 + [row(HEADS), row(Dm), row(Dm),
                                                 full(scale), full(g_norm), full(w_main), full(w_ft)],
        out_specs=[row(Dm), vec, vec, vec],
        out_shape=[jax.ShapeDtypeStruct((S, Dm), F32)] + [jax.ShapeDtypeStruct((1, Dm), F32)] * 3,
        compiler_params=_params(("arbitrary",), VMEM_BIG),
    )(dq, dk, dv, dza, du, dzb, dga, dgb, dfl, x, dx2, scale, g_norm, w_main, w_ft)


def _adamw(name, planes, w, m, v, tr):
    n, R, C = planes.shape
    bc1 = 1.0 - ADAM_B1 ** ADAM_STEP
    bc2 = 1.0 - ADAM_B2 ** ADAM_STEP

    def body(p_ref, w_ref, m_ref, v_ref, g_ref, d_ref, nm_ref, nv_ref):
        g = p_ref[0].astype(F32)
        for i in range(1, n):
            g = g + p_ref[i].astype(F32)
        g_ref[...] = g
        nm = ADAM_B1 * m_ref[...] + (1.0 - ADAM_B1) * g
        nv = ADAM_B2 * v_ref[...] + (1.0 - ADAM_B2) * (g * g)
        nm_ref[...] = nm
        nv_ref[...] = nv
        d_ref[...] = -ADAM_LR * ((nm / bc1) / (jnp.sqrt(nv / bc2) + ADAM_EPS) + ADAM_WD * w_ref[...])

    blk = pl.BlockSpec((tr, C), lambda i: (i, 0))
    return pl.pallas_call(
        body, name=name, grid=(R // tr,),
        in_specs=[pl.BlockSpec((n, tr, C), lambda i: (0, i, 0)), blk, blk, blk],
        out_specs=[blk] * 4, out_shape=[jax.ShapeDtypeStruct((R, C), F32)] * 4,
        compiler_params=_params(("parallel",), VMEM_MID),
    )(planes, w, m, v)


def _wada_grad(c_all, dmod_cols):
    def body(c_ref, d_ref, o_ref):
        o_ref[0] = _mm32_tn(c_ref[...], d_ref[...])

    return pl.pallas_call(
        body, name="wada_grad",
        out_shape=jax.ShapeDtypeStruct((1, c_all.shape[1], dmod_cols.shape[1]), F32),
        in_specs=[VMEM, VMEM], out_specs=VMEM,
    )(c_all, dmod_cols)


SMALL_ORDER = ("b_ada", "g_norm", "b_f", "a_re", "a_im", "log_dt", "b_re", "b_im", "c_re", "c_im",
               "d_skip", "b_glu", "g_final")
BIG_ORDER = ("w_ada", "w_in", "w_glu", "w_up_a", "w_up_b", "w_out")
ALL_ORDER = ("w_ada", "b_ada", "g_norm", "w_in", "b_f", "a_re", "a_im", "log_dt", "b_re", "b_im", "c_re", "c_im",
             "d_skip", "w_glu", "b_glu", "w_up_a", "w_up_b", "w_out", "g_final")


def _pack_small(parts, rows):
    flat = jnp.concatenate([p.reshape(-1).astype(F32) for p in parts])
    return jnp.pad(flat, (0, rows * 128 - flat.shape[0])).reshape(rows, 128)


def kernel(x, c, w_ada, b_ada, g_norm, w_in, b_f, a_re, a_im, log_dt, b_re, b_im, c_re, c_im, d_skip, w_glu, b_glu, w_up_a, w_up_b, w_out, g_final, loss_target, m_w_ada, m_b_ada, m_g_norm, m_w_in, m_b_f, m_a_re, m_a_im, m_log_dt, m_b_re, m_b_im, m_c_re, m_c_im, m_d_skip, m_w_glu, m_b_glu, m_w_up_a, m_w_up_b, m_w_out, m_g_final, v_w_ada, v_b_ada, v_g_norm, v_w_in, v_b_f, v_a_re, v_a_im, v_log_dt, v_b_re, v_b_im, v_c_re, v_c_im, v_d_skip, v_w_glu, v_b_glu, v_w_up_a, v_w_up_b, v_w_out, v_g_final):
    weights = dict(w_ada=w_ada, b_ada=b_ada, g_norm=g_norm, w_in=w_in, b_f=b_f, a_re=a_re, a_im=a_im, log_dt=log_dt,
                   b_re=b_re, b_im=b_im, c_re=c_re, c_im=c_im, d_skip=d_skip, w_glu=w_glu, b_glu=b_glu,
                   w_up_a=w_up_a, w_up_b=w_up_b, w_out=w_out, g_final=g_final)
    mom_m = dict(w_ada=m_w_ada, b_ada=m_b_ada, g_norm=m_g_norm, w_in=m_w_in, b_f=m_b_f, a_re=m_a_re, a_im=m_a_im,
                 log_dt=m_log_dt, b_re=m_b_re, b_im=m_b_im, c_re=m_c_re, c_im=m_c_im, d_skip=m_d_skip, w_glu=m_w_glu,
                 b_glu=m_b_glu, w_up_a=m_w_up_a, w_up_b=m_w_up_b, w_out=m_w_out, g_final=m_g_final)
    mom_v = dict(w_ada=v_w_ada, b_ada=v_b_ada, g_norm=v_g_norm, w_in=v_w_in, b_f=v_b_f, a_re=v_a_re, a_im=v_a_im,
                 log_dt=v_log_dt, b_re=v_b_re, b_im=v_b_im, c_re=v_c_re, c_im=v_c_im, d_skip=v_d_skip, w_glu=v_w_glu,
                 b_glu=v_b_glu, w_up_a=v_w_up_a, w_up_b=v_w_up_b, w_out=v_w_out, g_final=v_g_final)
    xs = x[0]
    tgt = loss_target[0]
    S = xs.shape[0]
    ts = min(256, S)
    ta = min(512, S)
    tw = min(2048, S)
    nch = S // CHUNK
    n_steps = max(1, int(math.ceil(math.log2(nch))))
    me = _my_index()

    shards = [w.astype(MXU_DTYPE) for w in (w_in[0], w_glu[0], w_up_a[0], w_up_b[0], w_out[0])]
    mod8, c_all, gathered = _comm_in(c, w_ada[0], b_ada.reshape(N_DEV, -1), shards)
    mod = mod8.reshape(1, 3 * D_MODEL)
    shift, scale, gate = mod[:, :D_MODEL], mod[:, D_MODEL:2 * D_MODEL], mod[:, 2 * D_MODEL:]
    w_in_full = gathered[0].transpose(1, 0, 2).reshape(D_MODEL, PROJ_WIDTH)
    w_main = jnp.concatenate([w_in_full[:, :OFF_F], w_in_full[:, OFF_F + HEADS:]], axis=1)
    w_f = w_in_full[:, OFF_F:OFF_F + HEADS].astype(F32)
    w_ft = w_f.T
    w_glu_full = gathered[1].reshape(WIDTH, WIDTH)
    w_up_a_full = gathered[2].transpose(1, 0, 2).reshape(WIDTH, D_MODEL)
    w_up_b_full = gathered[3].transpose(1, 0, 2).reshape(WIDTH, D_MODEL)
    w_out_full = gathered[4].reshape(D_MODEL, D_MODEL)

    q, k, v, za, u, zb, ga, gb, flc, hb = _proj_fwd(xs, shift, scale, g_norm, w_main, w_f, ts)
    nb = S // ta
    rows4 = lambda r: r.reshape(PAIRS, 2, nb, ta).transpose(0, 2, 1, 3)
    qh, kh, vt = _attn_prep(q, k, v, flc, b_f, ta)
    o, lse4 = _attn_fwd(qh, kh, vt, ta)

    s5_params = (a_re[0], a_im[0], log_dt[0], b_re[0], b_im[0], c_re[0], c_im[0], d_skip[0])
    (trow, camat, bzmat, al), mats_vjp = jax.vjp(_s5_mats, *s5_params)
    del al
    pw_f, pw_b = _s5_scan_powers(a_re[0], a_im[0], log_dt[0], n_steps)
    ys, xprev, uc = _s5_fwd(u, trow, camat, bzmat, pw_f)

    hsel = (np.arange(WIDTH)[None, :] // 64 == np.arange(HEADS)[:, None]).astype(np.float32)
    (dx2, do, dza, dzb, dga, dgb, dys, dl_row, merged, dmo, ya, dua, yb, dub, yg, dgl,
     db_glu, dgate, dg_final, loss_part) = _mix(o, za, ys, zb, ga, gb, xs, tgt, gate, b_glu, g_final.reshape(1, -1),
                                                w_glu_full, w_up_a_full, w_up_b_full, w_out_full, jnp.asarray(hsel), ts)

    gw_out = _matmul_tn("dw_out", merged, dmo, tw)
    gw_up_a = _matmul_tn("dw_up_a", ya, dua, tw)
    gw_up_b = _matmul_tn("dw_up_b", yb, dub, tw)
    gw_glu = _matmul_tn("dw_glu", yg, dgl, tw)

    du, d_trow, d_camat, d_bzmat, dal2 = _s5_bwd(uc, dys, xprev, trow, camat, bzmat, pw_b)
    d_al = jnp.concatenate([dal2[:, 0, :STATE] + dal2[:, 0, STATE:], dal2[:, 1, STATE:] - dal2[:, 1, :STATE]], axis=-1)
    gs5 = mats_vjp((d_trow, d_camat, d_bzmat, d_al))

    dl4 = rows4(dl_row)
    dq, dk, dv, dfq4, dfk4 = _attn_bwd(qh, do, kh, v, lse4, dl4, ta)
    d_fcol = dfq4.transpose(0, 2, 1, 3).reshape(HEADS, S).T - dfk4.transpose(1, 0, 2).reshape(S, HEADS)
    dfl, db_f = _fgate_bwd(d_fcol, flc, b_f, ta)

    grad_x, dshift, dscale, dg_norm = _proj_bwd(dq, dk, dv, dza, du, dzb, dga, dgb, dfl, xs, dx2,
                                                shift, scale, g_norm, w_main, w_ft, ts)
    segs = [("dw_q", dq), ("dw_k", dk), ("dw_v", dv), ("dw_f", dfl), ("dw_za", dza), ("dw_u", du), ("dw_zb", dzb),
            ("dw_ga", dga), ("dw_gb", dgb)]
    gw_in = jnp.concatenate([_matmul_tn(nm, hb, d, tw) for nm, d in segs], axis=1)

    planes = [gw_in.reshape(D_MODEL, N_DEV, -1).transpose(1, 0, 2),
              gw_glu.reshape(N_DEV, -1, WIDTH),
              gw_up_a.reshape(WIDTH, N_DEV, -1).transpose(1, 0, 2),
              gw_up_b.reshape(WIDTH, N_DEV, -1).transpose(1, 0, 2),
              gw_out.reshape(N_DEV, -1, D_MODEL)]
    planes = [p.astype(MXU_DTYPE) for p in planes]
    dmod = jnp.concatenate([dshift, dscale, dgate], axis=1)
    small_parts = [dmod, dg_norm, db_f, gs5[0], gs5[1], gs5[2], gs5[3], gs5[4], gs5[5], gs5[6], gs5[7],
                   db_glu, dg_final, loss_part]
    n_small = sum(int(np.prod(p.shape)) for p in small_parts)
    rows = -(-n_small // (8 * 128)) * 8
    small = _pack_small(small_parts, rows)
    from_sib = _comm_pair(planes)
    core = lax.axis_index("c")
    chip_planes = []
    for name, p, s in zip(("w_in", "w_glu", "w_up_a", "w_up_b", "w_out"), planes, from_sib):
        tr = 256 if s.shape[1] % 256 == 0 else s.shape[1]
        chip_planes.append(_pair_sum("pair_sum_" + name, p, s, core, tr))
    recv, small_all = _comm_out(chip_planes, small)

    grads, deltas, new_m, new_v = {}, {}, {}, {}

    def put(name, res, shape):
        grads[name], deltas[name], new_m[name], new_v[name] = [r.reshape(shape) for r in res]

    names = ("w_in", "w_glu", "w_up_a", "w_up_b", "w_out")
    for name, pr in zip(names, recv):
        w2 = weights[name][0]
        tr = 256 if w2.shape[0] % 256 == 0 else w2.shape[0]
        put(name, _adamw("adamw_" + name, pr, w2, mom_m[name][0], mom_v[name][0], tr), weights[name].shape)
    cols = w_ada.shape[2]
    dmod_all = small_all[:, :24, :].reshape(N_DEV, 3 * D_MODEL)
    dmod_cols = lax.dynamic_slice_in_dim(dmod_all, me * cols, cols, axis=1)
    g_wada = _wada_grad(c_all, dmod_cols)
    put("w_ada", _adamw("adamw_w_ada", g_wada, w_ada[0], m_w_ada[0], v_w_ada[0], 256), w_ada.shape)
    pack = lambda d: _pack_small([d[n] for n in SMALL_ORDER] + [jnp.zeros((1,), F32)], rows)
    res_small = _adamw("adamw_small", small_all, pack(weights), pack(mom_m), pack(mom_v), rows)
    flat = [r.reshape(-1) for r in res_small]
    off = 0
    for name in SMALL_ORDER:
        shape = weights[name].shape
        size = int(np.prod(shape))
        put(name, [f[off:off + size] for f in flat], shape)
        off += size
    loss = flat[0][off]

    return (loss, grad_x[None], *[grads[n] for n in ALL_ORDER], *[deltas[n] for n in ALL_ORDER],
            *[new_m[n] for n in ALL_ORDER], *[new_v[n] for n in ALL_ORDER])
```

```python
import math

import jax
import jax.numpy as jnp
import numpy as np
from jax import lax
from jax.experimental import pallas as pl
from jax.experimental.pallas import tpu as pltpu

F32 = jnp.float32
MXU_DTYPE = jnp.bfloat16
HI = lax.Precision.HIGHEST

N_DEV = 8
D_MODEL = 1024
WIDTH = 512
HEADS = 8
PAIRS = HEADS // 2
GROUPS = 32
STATE = 64
CG = 16
CHUNK = 16
EPS = 1e-6
NEG = float(np.finfo(np.float32).min)

ADAM_LR = 0.001
ADAM_B1 = 0.9
ADAM_B2 = 0.999
ADAM_EPS = 1e-08
ADAM_WD = 0.01
ADAM_STEP = 10

VMEM_BIG = 56 * 1024 * 1024
VMEM_MID = 40 * 1024 * 1024

OFF_F = 3 * WIDTH
PROJ_WIDTH = 5128
M_Q, M_K, M_V, M_ZA, M_U, M_ZB, M_GA, M_GB = 0, 512, 1024, 1536, 2048, 2560, 3072, 4096


def _mm(a, b):
    return jnp.dot(a.astype(MXU_DTYPE), b.astype(MXU_DTYPE), preferred_element_type=F32)


def _mm_nt(a, b):
    return lax.dot_general(a.astype(MXU_DTYPE), b.astype(MXU_DTYPE), (((1,), (1,)), ((), ())),
                           preferred_element_type=F32)


def _mm_tn(a, b):
    return lax.dot_general(a.astype(MXU_DTYPE), b.astype(MXU_DTYPE), (((0,), (0,)), ((), ())),
                           preferred_element_type=F32)


def _mm32(a, b):
    return jnp.dot(a, b, precision=HI, preferred_element_type=F32)


def _mm32_nt(a, b):
    return lax.dot_general(a, b, (((1,), (1,)), ((), ())), precision=HI, preferred_element_type=F32)


def _mm32_tn(a, b):
    return lax.dot_general(a, b, (((0,), (0,)), ((), ())), precision=HI, preferred_element_type=F32)


S5_PRECISION = lax.Precision.HIGH


def _s5mm(a, b):
    return jnp.dot(a, b, precision=S5_PRECISION, preferred_element_type=F32)


def _s5mm_nt(a, b):
    return lax.dot_general(a, b, (((1,), (1,)), ((), ())), precision=S5_PRECISION, preferred_element_type=F32)


def _s5mm_tn(a, b):
    return lax.dot_general(a, b, (((0,), (0,)), ((), ())), precision=S5_PRECISION, preferred_element_type=F32)


def _sigmoid(x):
    return 1.0 / (1.0 + jnp.exp(-x))


def _params(sem=None, vmem=None):
    kw = {}
    if sem is not None:
        kw["dimension_semantics"] = sem
    if vmem is not None:
        kw["vmem_limit_bytes"] = vmem
    return pltpu.CompilerParams(**kw)


def _my_index():
    return 4 * lax.axis_index("x") + 2 * lax.axis_index("y") + lax.axis_index("c")


def _dev(p):
    return (p // 4, (p // 2) % 2, p % 2)


ANY = pl.BlockSpec(memory_space=pl.ANY)
VMEM = pl.BlockSpec(memory_space=pltpu.VMEM)
MESH = pl.DeviceIdType.MESH


def _comm_in(c, w_ada, b_ada8, shards):
    n = len(shards)
    cols = w_ada.shape[1]

    def body(c_ref, wada_ref, bada_ref, *rest):
        srcs = rest[:n]
        mod_ref, call_ref = rest[n], rest[n + 1]
        dsts = rest[n + 2:2 * n + 2]
        modp, wsend, wrecv, wloc, csend, crecv, msend, mrecv = rest[2 * n + 2:]
        me = _my_index()

        x, y, cc = lax.axis_index("x"), lax.axis_index("y"), lax.axis_index("c")
        here, sib = (x, y, cc), (x, y, 1 - cc)
        xn, yn, dg = (1 - x, y, cc), (x, 1 - y, cc), (1 - x, 1 - y, cc)
        half = srcs[0].shape[0] // 2
        parts = [(0, pl.ds(0, half)), (0, pl.ds(half, half))] + [(a, None) for a in range(1, n)]
        via_y = [i % 2 == 0 for i in range(len(parts))]

        def wcopy(i, k, block, to, own=False):
            a, rs = parts[i]
            dst = dsts[a].at[4 * block[0] + 2 * block[1] + block[2]]
            src = srcs[a] if own else dst
            if rs is not None:
                src, dst = src.at[rs], dst.at[rs]
            return pltpu.make_async_remote_copy(src_ref=src, dst_ref=dst,
                                                send_sem=wsend.at[i, k], recv_sem=wrecv.at[i, k],
                                                device_id=to, device_id_type=MESH)

        def ccopy(src_dev, d, to):
            return pltpu.make_async_remote_copy(src_ref=c_ref, dst_ref=call_ref.at[pl.ds(src_dev, 1)],
                                                send_sem=csend.at[d], recv_sem=crecv.at[src_dev],
                                                device_id=_dev(to), device_id_type=MESH)

        def mcopy(src_dev, d, to):
            return pltpu.make_async_remote_copy(src_ref=modp.at[pl.ds(to, 1)], dst_ref=mod_ref.at[pl.ds(src_dev, 1)],
                                                send_sem=msend.at[d], recv_sem=mrecv.at[src_dev],
                                                device_id=_dev(to), device_id_type=MESH)

        local = [pltpu.make_async_copy(srcs[a], dsts[a].at[me], wloc.at[a]) for a in range(n)]
        for cp in local:
            cp.start()
        peers = [(me + d) % N_DEV for d in range(1, N_DEV)]
        sends = []
        for i in range(len(parts)):
            sends += [wcopy(i, 0, here, sib, own=True), wcopy(i, 1, here, xn, own=True), wcopy(i, 2, here, yn, own=True)]
        for cp in sends:
            cp.start()
        call_ref[pl.ds(me, 1), :] = c_ref[...]
        for d, p in enumerate(peers):
            ccopy(me, d, p).start()
        for d, p in enumerate(peers):
            ccopy(p, d, p).wait_recv()
        modp[...] = _mm32(call_ref[...], wada_ref[...]) + bada_ref[pl.ds(me, 1), :]
        mod_ref[pl.ds(me, 1), :] = modp[pl.ds(me, 1), :]
        for d, p in enumerate(peers):
            mcopy(me, d, p).start()
        for d, p in enumerate(peers):
            mcopy(p, d, p).wait_recv()
        def after(i, k, block, nxt):
            wcopy(i, k, block, here).wait_recv()
            for kk, to in nxt:
                cp = wcopy(i, kk, block, to)
                cp.start()
                sends.append(cp)

        for i in range(len(parts)):
            after(i, 1, xn, [(3, sib)] + ([(5, yn)] if via_y[i] else []))
        for i in range(len(parts)):
            after(i, 2, yn, [(4, sib)] + ([] if via_y[i] else [(6, xn)]))
        for i in range(len(parts)):
            after(i, 5 if via_y[i] else 6, dg, [(7, sib)])
        for i in range(len(parts)):
            wcopy(i, 0, sib, here).wait_recv()
            for k, block in ((3, xn), (4, yn), (7, dg)):
                wcopy(i, k, (block[0], block[1], 1 - cc), here).wait_recv()
        for cp in sends:
            cp.wait_send()
        for d, p in enumerate(peers):
            ccopy(me, d, p).wait_send()
            mcopy(me, d, p).wait_send()
        for cp in local:
            cp.wait()

    out_shape = ([jax.ShapeDtypeStruct((N_DEV, cols), F32), jax.ShapeDtypeStruct((N_DEV, D_MODEL), F32)]
                 + [jax.ShapeDtypeStruct((N_DEV,) + s.shape, s.dtype) for s in shards])
    res = pl.pallas_call(
        body, name="comm_in", out_shape=out_shape,
        in_specs=[VMEM, VMEM, VMEM] + [ANY] * n,
        out_specs=[VMEM, VMEM] + [ANY] * n,
        scratch_shapes=[pltpu.VMEM((N_DEV, cols), F32),
                        pltpu.SemaphoreType.DMA((n + 1, N_DEV)), pltpu.SemaphoreType.DMA((n + 1, N_DEV)),
                        pltpu.SemaphoreType.DMA((n,)),
                        pltpu.SemaphoreType.DMA((N_DEV,)), pltpu.SemaphoreType.DMA((N_DEV,)),
                        pltpu.SemaphoreType.DMA((N_DEV,)), pltpu.SemaphoreType.DMA((N_DEV,))],
        compiler_params=_params(vmem=VMEM_MID),
    )(c, w_ada, b_ada8, *shards)
    return res[0], res[1], list(res[2:])


def _direct_copy(srcs, dsts, send, recv, scatter, a, d, receiving):
    me = _my_index()
    p = (me + d) % N_DEV
    slot = p if receiving else me
    return pltpu.make_async_remote_copy(src_ref=srcs[a].at[p] if scatter else srcs[a], dst_ref=dsts[a].at[slot],
                                        send_sem=send.at[a, d], recv_sem=recv.at[a, slot],
                                        device_id=_dev(p), device_id_type=MESH)


def _hosted_exchange(first, last, srcs, dsts, send, recv, loc, scatter):
    me = _my_index()
    n = len(srcs)
    local = lambda a: pltpu.make_async_copy(srcs[a].at[me] if scatter else srcs[a], dsts[a].at[me], loc.at[a])
    pairs = [(a, d) for d in range(1, N_DEV) for a in range(n)]

    @pl.when(first)
    def _():
        for a in range(n):
            local(a).start()
        for a, d in pairs:
            _direct_copy(srcs, dsts, send, recv, scatter, a, d, False).start()

    @pl.when(last)
    def _():
        for a, d in pairs:
            _direct_copy(srcs, dsts, send, recv, scatter, a, d, True).wait_recv()
            _direct_copy(srcs, dsts, send, recv, scatter, a, d, False).wait_send()
        for a in range(n):
            local(a).wait()


def _exchange_scratch(n):
    return [pltpu.SemaphoreType.DMA((n, N_DEV)), pltpu.SemaphoreType.DMA((n, N_DEV)), pltpu.SemaphoreType.DMA((n,))]


N_CHIP = 4


def _comm_pair(planes):
    n = len(planes)

    def body(*rest):
        srcs, dsts = rest[:n], rest[n:2 * n]
        send, recv = rest[2 * n:]
        x, y, cc = lax.axis_index("x"), lax.axis_index("y"), lax.axis_index("c")
        copies = [pltpu.make_async_remote_copy(src_ref=srcs[a].at[2 * ch + 1 - cc], dst_ref=dsts[a].at[ch],
                                               send_sem=send.at[a, ch], recv_sem=recv.at[a, ch],
                                               device_id=(x, y, 1 - cc), device_id_type=MESH)
                  for a in range(n) for ch in range(N_CHIP)]
        for cp in copies:
            cp.start()
        for cp in copies:
            cp.wait()

    out_shape = [jax.ShapeDtypeStruct((N_CHIP,) + p.shape[1:], p.dtype) for p in planes]
    return pl.pallas_call(
        body, name="comm_pair", out_shape=out_shape, in_specs=[ANY] * n, out_specs=[ANY] * n,
        scratch_shapes=[pltpu.SemaphoreType.DMA((n, N_CHIP)), pltpu.SemaphoreType.DMA((n, N_CHIP))],
    )(*planes)


def _pair_sum(name, planes, from_sib, core, tr):
    _, R, C = from_sib.shape

    def body(core_ref, a_ref, b_ref, o_ref):
        del core_ref
        o_ref[...] = (a_ref[...].astype(F32) + b_ref[...].astype(F32)).astype(o_ref.dtype)

    blk = pl.BlockSpec((None, tr, C), lambda i, j, c: (i, j, 0))
    grid_spec = pltpu.PrefetchScalarGridSpec(
        num_scalar_prefetch=1, grid=(N_CHIP, R // tr),
        in_specs=[pl.BlockSpec((None, tr, C), lambda i, j, c: (2 * i + c[0], j, 0)), blk], out_specs=blk)
    return pl.pallas_call(
        body, name=name, grid_spec=grid_spec, out_shape=jax.ShapeDtypeStruct(from_sib.shape, from_sib.dtype),
        compiler_params=_params(("parallel", "parallel"), VMEM_MID),
    )(core.reshape(1).astype(jnp.int32), planes, from_sib)


def _comm_out(chip_planes, small):
    n = len(chip_planes)
    rows = small.shape[0]
    cut = (rows // 16) * 8

    def body(*rest):
        srcs = rest[:n]
        small_ref = rest[n]
        dsts = rest[n + 1:2 * n + 1]
        sall_ref = rest[2 * n + 1]
        wsend, wrecv, wloc, ssend, srecv, sloc = rest[2 * n + 2:]
        me = _my_index()
        x, y, cc = lax.axis_index("x"), lax.axis_index("y"), lax.axis_index("c")
        mine = 2 * x + y
        chips = [(1 - x, y), (x, 1 - y), (1 - x, 1 - y)]
        here, sib = (x, y, cc), (x, y, 1 - cc)
        xn, yn, dg = (1 - x, y, cc), (x, 1 - y, cc), (1 - x, 1 - y, cc)
        sparts = [pl.ds(0, cut), pl.ds(cut, rows - cut)]
        via_y = [True, False]

        def wcopy(a, j, sending):
            chip = chips[j]
            there = 2 * chip[0] + chip[1]
            return pltpu.make_async_remote_copy(src_ref=srcs[a].at[there], dst_ref=dsts[a].at[mine if sending else there],
                                                send_sem=wsend.at[a, j], recv_sem=wrecv.at[a, j],
                                                device_id=(*chip, cc), device_id_type=MESH)

        def scopy(i, k, block, to, own=False):
            dst = sall_ref.at[4 * block[0] + 2 * block[1] + block[2]].at[sparts[i]]
            src = small_ref.at[sparts[i]] if own else dst
            return pltpu.make_async_remote_copy(src_ref=src, dst_ref=dst, send_sem=ssend.at[i, k], recv_sem=srecv.at[i, k],
                                                device_id=to, device_id_type=MESH)

        local = [pltpu.make_async_copy(srcs[a].at[mine], dsts[a].at[mine], wloc.at[a]) for a in range(n)]
        local.append(pltpu.make_async_copy(small_ref, sall_ref.at[me], sloc))
        for cp in local:
            cp.start()
        sends = []
        for i in range(len(sparts)):
            sends += [scopy(i, 0, here, sib, own=True), scopy(i, 1, here, xn, own=True), scopy(i, 2, here, yn, own=True)]
        for j in range(len(chips)):
            sends += [wcopy(a, j, True) for a in range(n)]
        for cp in sends:
            cp.start()

        def after(i, k, block, nxt):
            scopy(i, k, block, here).wait_recv()
            for kk, to in nxt:
                cp = scopy(i, kk, block, to)
                cp.start()
                sends.append(cp)

        for i in range(len(sparts)):
            after(i, 1, xn, [(3, sib)] + ([(5, yn)] if via_y[i] else []))
        for i in range(len(sparts)):
            after(i, 2, yn, [(4, sib)] + ([] if via_y[i] else [(6, xn)]))
        for i in range(len(sparts)):
            after(i, 5 if via_y[i] else 6, dg, [(7, sib)])
        for i in range(len(sparts)):
            scopy(i, 0, sib, here).wait_recv()
            for k, block in ((3, xn), (4, yn), (7, dg)):
                scopy(i, k, (block[0], block[1], 1 - cc), here).wait_recv()
        for j in range(len(chips)):
            for a in range(n):
                wcopy(a, j, False).wait_recv()
        for cp in sends:
            cp.wait_send()
        for cp in local:
            cp.wait()

    out_shape = ([jax.ShapeDtypeStruct(p.shape, p.dtype) for p in chip_planes]
                 + [jax.ShapeDtypeStruct((N_DEV,) + small.shape, small.dtype)])
    res = pl.pallas_call(
        body, name="comm_out", out_shape=out_shape,
        in_specs=[ANY] * (n + 1), out_specs=[ANY] * (n + 1),
        scratch_shapes=[pltpu.SemaphoreType.DMA((n, N_CHIP)), pltpu.SemaphoreType.DMA((n, N_CHIP)),
                        pltpu.SemaphoreType.DMA((n,)),
                        pltpu.SemaphoreType.DMA((2, N_DEV)), pltpu.SemaphoreType.DMA((2, N_DEV)),
                        pltpu.SemaphoreType.DMA(())],
    )(*chip_planes, small)
    return list(res[:n]), res[n]


def _proj_fwd(x, shift, scale, g_norm, w_main, w_f, late, ts):
    S = x.shape[0]
    nl = len(late)

    def body(x_ref, sh_ref, sc_ref, gn_ref, w_ref, wf_ref, *rest):
        (q_ref, k_ref, v_ref, za_ref, u_ref, zb_ref, ga_ref, gb_ref, flc_ref, h_ref) = rest[nl:nl + 10]
        i = pl.program_id(0)
        _hosted_exchange(i == 0, i == pl.num_programs(0) - 1, rest[:nl], rest[nl + 10:2 * nl + 10],
                         *rest[2 * nl + 10:], scatter=False)
        xv = x_ref[...]
        r = lax.rsqrt(jnp.mean(xv * xv, axis=-1, keepdims=True) + EPS)
        h = (xv * r) * gn_ref[...] * (1.0 + sc_ref[...]) + sh_ref[...]
        hb = h.astype(MXU_DTYPE)
        h_ref[...] = hb

        def seg(off, n):
            return jnp.dot(hb, w_ref[:, off:off + n], preferred_element_type=F32)

        q_ref[...] = (seg(M_Q, WIDTH) * 0.125).astype(q_ref.dtype)
        k_ref[...] = seg(M_K, WIDTH).astype(k_ref.dtype)
        v_ref[...] = seg(M_V, WIDTH).astype(v_ref.dtype)
        za_ref[...] = seg(M_ZA, WIDTH)
        u_ref[...] = seg(M_U, WIDTH)
        zb_ref[...] = seg(M_ZB, WIDTH)
        ga_ref[...] = seg(M_GA, D_MODEL)
        gb_ref[...] = seg(M_GB, D_MODEL)
        flc_ref[...] = _mm32(h, wf_ref[...])

    row = lambda n: pl.BlockSpec((ts, n), lambda i: (i, 0))
    full = lambda a: pl.BlockSpec(a.shape, lambda i: (0,) * a.ndim)
    sds = jax.ShapeDtypeStruct
    return pl.pallas_call(
        body, name="proj_fwd", grid=(S // ts,),
        in_specs=[row(D_MODEL), full(shift), full(scale), full(g_norm), full(w_main), full(w_f)] + [ANY] * nl,
        out_specs=[row(WIDTH)] * 6 + [row(D_MODEL)] * 2 + [row(HEADS), row(D_MODEL)] + [ANY] * nl,
        out_shape=[sds((S, WIDTH), MXU_DTYPE)] * 3 + [sds((S, WIDTH), F32)] * 3 + [sds((S, D_MODEL), F32)] * 2
                  + [sds((S, HEADS), F32), sds((S, D_MODEL), MXU_DTYPE)]
                  + [sds((N_DEV,) + w.shape, w.dtype) for w in late],
        scratch_shapes=_exchange_scratch(nl),
        compiler_params=_params(("arbitrary",), VMEM_BIG),
    )(x, shift, scale, g_norm, w_main, w_f, *late)


def _log_sigmoid(z):
    return jnp.minimum(z, 0.0) - jnp.log(1.0 + jnp.exp(-jnp.abs(z)))


def _fgate_bwd(dfc, flc, bf_row, ts):
    S = flc.shape[0]
    n = S // ts

    def body(df_ref, flc_ref, bfr_ref, dfl_ref, dbf_ref, carry):
        @pl.when(pl.program_id(0) == 0)
        def _():
            carry[...] = jnp.zeros_like(carry)
            dbf_ref[...] = jnp.zeros_like(dbf_ref)

        ri = lax.broadcasted_iota(jnp.int32, (ts, ts), 0)
        ci = lax.broadcasted_iota(jnp.int32, (ts, ts), 1)
        upper = (ci >= ri).astype(F32)
        rc = _mm32(upper, df_ref[...]) + carry[...]
        carry[...] = rc[0:1, :]
        z = flc_ref[...] + bfr_ref[...]
        dfl = rc * _sigmoid(-z)
        dfl_ref[...] = dfl
        dbf_ref[...] += jnp.sum(dfl, axis=0, keepdims=True)

    col = pl.BlockSpec((ts, HEADS), lambda i: (n - 1 - i, 0))
    one = pl.BlockSpec((1, HEADS), lambda i: (0, 0))
    return pl.pallas_call(
        body, name="fgate_bwd", grid=(n,),
        in_specs=[col, col, one], out_specs=[col, one],
        out_shape=[jax.ShapeDtypeStruct((S, HEADS), F32), jax.ShapeDtypeStruct((1, HEADS), F32)],
        scratch_shapes=[pltpu.VMEM((1, HEADS), F32)],
        compiler_params=_params(("arbitrary",)),
    )(dfc, flc, bf_row)


N_EXTRA = 3


def _attn_prep(q, k, v, flc, bf_row, t):
    S = q.shape[0]
    nb = S // t

    def body(q_ref, k_ref, v_ref, flc_ref, bfr_ref, qh_ref, kh_ref, vt_ref, carry):
        @pl.when(pl.program_id(0) == 0)
        def _():
            carry[...] = jnp.zeros_like(carry)

        ri = lax.broadcasted_iota(jnp.int32, (t, t), 0)
        ci = lax.broadcasted_iota(jnp.int32, (t, t), 1)
        f = _mm32((ci <= ri).astype(F32), _log_sigmoid(flc_ref[...] + bfr_ref[...])) + carry[...]
        carry[...] = f[t - 1:t, :]
        lane = lax.broadcasted_iota(jnp.int32, (t, 128), 1)
        for p in range(PAIRS):
            qp = q_ref[:, p * 128:(p + 1) * 128]
            kp = k_ref[:, p * 128:(p + 1) * 128]
            vt_ref[p, 0] = v_ref[:, p * 128:(p + 1) * 128].T
            for h in range(2):
                own = (lane < 64) if h == 0 else (lane >= 64)
                base = 64 if h == 0 else 0
                fh = f[:, 2 * p + h:2 * p + h + 1]
                parts = []
                rest = fh
                for _ in range(N_EXTRA):
                    part = rest.astype(qh_ref.dtype)
                    parts.append(part)
                    rest = rest - part.astype(F32)
                one = jnp.ones((t, 1), qh_ref.dtype)
                eq = jnp.zeros((t, 128), qh_ref.dtype)
                ek = jnp.zeros((t, 128), qh_ref.dtype)
                for j in range(N_EXTRA):
                    eq = jnp.where(lane == base + j, parts[j], eq)
                    eq = jnp.where(lane == base + N_EXTRA + j, one, eq)
                    ek = jnp.where(lane == base + j, one, ek)
                    ek = jnp.where(lane == base + N_EXTRA + j, -parts[j], ek)
                qh_ref[2 * p + h] = jnp.where(own, qp, eq)
                kh_ref[2 * p + h] = jnp.where(own, kp, ek)

    row = pl.BlockSpec((t, WIDTH), lambda i: (i, 0))
    heads = pl.BlockSpec((HEADS, t, 128), lambda i: (0, i, 0))
    return pl.pallas_call(
        body, name="attn_prep", grid=(nb,),
        in_specs=[row, row, row, pl.BlockSpec((t, HEADS), lambda i: (i, 0)), pl.BlockSpec((1, HEADS), lambda i: (0, 0))],
        out_specs=[heads, heads, pl.BlockSpec((PAIRS, 1, 128, t), lambda i: (0, i, 0, 0))],
        out_shape=[jax.ShapeDtypeStruct((HEADS, S, 128), q.dtype), jax.ShapeDtypeStruct((HEADS, S, 128), k.dtype),
                   jax.ShapeDtypeStruct((PAIRS, nb, 128, t), v.dtype)],
        scratch_shapes=[pltpu.VMEM((1, HEADS), F32)],
        compiler_params=_params(("arbitrary",), VMEM_MID),
    )(q, k, v, flc, bf_row)


def _attn_fwd(qh, kh, vt, t):
    S = qh.shape[1]
    nb = S // t

    def body(q_ref, k_ref, vt_ref, o_ref, lse_ref, acc_s):
        qi = pl.program_id(1)
        acc_s[...] = jnp.zeros_like(acc_s)

        def step(ki, nblk, masked, carry):
            m_old, l_old = carry[:2], carry[2:]
            ks = pl.multiple_of(ki * t, t)
            rows = nblk * t
            sts = [_mm_nt(k_ref[h, pl.ds(ks, rows), :], q_ref[h]) for h in range(2)]
            if masked:
                ri = lax.broadcasted_iota(jnp.int32, (t, t), 0)
                ci = lax.broadcasted_iota(jnp.int32, (t, t), 1)
                sts = [jnp.where(ci >= ri, st, NEG) for st in sts]
            m_new = [jnp.maximum(m_old[h], jnp.max(sts[h], axis=0, keepdims=True)) for h in range(2)]
            alpha = [jnp.exp(m_old[h] - m_new[h]) for h in range(2)]
            pts = [jnp.exp(sts[h] - m_new[h]) for h in range(2)]
            l_new = [alpha[h] * l_old[h] + jnp.sum(pts[h], axis=0, keepdims=True) for h in range(2)]
            for h in range(2):
                pv = _mm(vt_ref[ki], pts[h][:t])
                for b in range(1, nblk):
                    pv = pv + _mm(vt_ref[ki + b], pts[h][b * t:(b + 1) * t])
                acc_s[h] = alpha[h] * acc_s[h] + pv
            return (*m_new, *l_new)

        init = (jnp.full((1, t), -jnp.inf, F32),) * 2 + (jnp.zeros((1, t), F32),) * 2
        carry = lax.fori_loop(0, qi // 2, lambda j, c: step(2 * j, 2, False, c), init)
        carry = lax.cond(qi % 2 == 1, lambda c: step(qi - 1, 1, False, c), lambda c: c, carry)
        m0, m1, l0, l1 = step(qi, 1, True, carry)
        first = lax.broadcasted_iota(jnp.int32, (128, t), 0) < 64
        o_ref[...] = jnp.where(first, acc_s[0] / l0, acc_s[1] / l1).T
        lse_ref[...] = jnp.concatenate([m0 + jnp.log(l0), m1 + jnp.log(l1)], axis=0)

    return pl.pallas_call(
        body, name="attn_fwd", grid=(PAIRS, nb),
        in_specs=[pl.BlockSpec((2, t, 128), lambda p, i: (p, i, 0)), pl.BlockSpec((2, S, 128), lambda p, i: (p, 0, 0)),
                  pl.BlockSpec((None, nb, 128, t), lambda p, i: (p, 0, 0, 0))],
        out_specs=[pl.BlockSpec((t, 128), lambda p, i: (i, p)), pl.BlockSpec((None, None, 2, t), lambda p, i: (p, i, 0, 0))],
        out_shape=[jax.ShapeDtypeStruct((S, WIDTH), F32), jax.ShapeDtypeStruct((PAIRS, nb, 2, t), F32)],
        scratch_shapes=[pltpu.VMEM((2, 128, t), F32)],
        compiler_params=_params(("parallel", "parallel"), VMEM_MID),
    )(qh, kh, vt)


def _attn_bwd(qh, do, kh, v, lse4, dl4, early, t):
    S = qh.shape[1]
    nb = S // t
    ne = len(early)

    def body(q_ref, do_ref, k_ref, v_ref, lse_ref, dl_ref, *rest):
        dq_ref, dk_ref, dv_ref, dfq_ref, dfk_ref = rest[ne:ne + 5]
        kc_s, vh_s, dk_s, dv_s, dfk_s = rest[2 * ne + 5:2 * ne + 10]
        kj = pl.program_id(1)
        _hosted_exchange((pl.program_id(0) == 0) & (kj == 0), (pl.program_id(0) == PAIRS - 1) & (kj == nb - 1),
                         rest[:ne], rest[ne + 5:2 * ne + 5], *rest[2 * ne + 10:], scatter=True)
        lane = lax.broadcasted_iota(jnp.int32, (t, 128), 1)
        is_a = lane < 64

        @pl.when(kj == 0)
        def _():
            dq_ref[...] = jnp.zeros_like(dq_ref)
            dfq_ref[...] = jnp.zeros_like(dfq_ref)

        vp = v_ref[...]
        zero = jnp.zeros_like(vp)
        kc_s[0] = jnp.where(is_a, k_ref[0], zero.astype(kc_s.dtype))
        kc_s[1] = jnp.where(is_a, zero.astype(kc_s.dtype), k_ref[1])
        vh_s[0] = jnp.where(is_a, vp, zero)
        vh_s[1] = jnp.where(is_a, zero, vp)
        dk_s[...] = jnp.zeros_like(dk_s)
        dv_s[...] = jnp.zeros_like(dv_s)
        dfk_s[...] = jnp.zeros_like(dfk_s)

        def step(qi, masked):
            qs = pl.multiple_of(qi * t, t)
            dob = do_ref[pl.ds(qs, t), :]
            lse = lse_ref[qi]
            dl = dl_ref[qi]
            zq = jnp.zeros_like(dob)
            over_keys = []
            for h in range(2):
                sel = is_a if h == 0 else jnp.logical_not(is_a)
                qb = q_ref[h, pl.ds(qs, t), :]
                st = _mm_nt(k_ref[h], qb) - lse[h:h + 1, :]
                if masked:
                    ri = lax.broadcasted_iota(jnp.int32, (t, t), 0)
                    ci = lax.broadcasted_iota(jnp.int32, (t, t), 1)
                    st = jnp.where(ci >= ri, st, NEG)
                pt = jnp.exp(st)
                dv_s[...] += _mm(pt, jnp.where(sel, dob, zq))
                dpt = _mm_nt(vh_s[h], dob)
                dst = pt * (dpt - dl[h:h + 1, :])
                dfk_s[h] += jnp.sum(dst, axis=1, keepdims=True)
                over_keys.append(jnp.sum(dst, axis=0, keepdims=True))
                dk_s[...] += _mm(dst, jnp.where(sel, qb, jnp.zeros_like(qb)))
                dq_ref[pl.ds(qs, t), :] += _mm_tn(dst, kc_s[h])
            dfq_ref[qi] += jnp.concatenate(over_keys, axis=0)

        step(kj, True)

        def loop_body(qi, carry):
            step(qi, False)
            return carry

        lax.fori_loop(kj + 1, nb, loop_body, 0)
        dk_ref[...] = dk_s[...].astype(dk_ref.dtype)
        dv_ref[...] = dv_s[...].astype(dv_ref.dtype)
        dfk_ref[...] = jnp.where(lax.broadcasted_iota(jnp.int32, (t, 2), 1) == 0, dfk_s[0], dfk_s[1])

        @pl.when(kj == nb - 1)
        def _():
            dq_ref[...] = dq_ref[...] * 0.125

    blk = pl.BlockSpec((t, 128), lambda p, j: (j, p))
    res = pl.BlockSpec((S, 128), lambda p, j: (0, p))
    rows4 = pl.BlockSpec((None, nb, 2, t), lambda p, j: (p, 0, 0, 0))
    cols4 = pl.BlockSpec((None, t, 2), lambda p, j: (p, j, 0))
    return pl.pallas_call(
        body, name="attn_bwd", grid=(PAIRS, nb),
        in_specs=[pl.BlockSpec((2, S, 128), lambda p, j: (p, 0, 0)), res,
                  pl.BlockSpec((2, t, 128), lambda p, j: (p, j, 0)), blk, rows4, rows4] + [ANY] * ne,
        out_specs=[res, blk, blk, rows4, cols4] + [ANY] * ne,
        out_shape=[jax.ShapeDtypeStruct((S, WIDTH), F32), jax.ShapeDtypeStruct((S, WIDTH), MXU_DTYPE),
                   jax.ShapeDtypeStruct((S, WIDTH), MXU_DTYPE), jax.ShapeDtypeStruct((PAIRS, nb, 2, t), F32),
                   jax.ShapeDtypeStruct((PAIRS, S, 2), F32)] + [jax.ShapeDtypeStruct(e.shape, e.dtype) for e in early],
        scratch_shapes=[pltpu.VMEM((2, t, 128), kh.dtype), pltpu.VMEM((2, t, 128), v.dtype),
                        pltpu.VMEM((t, 128), F32), pltpu.VMEM((t, 128), F32), pltpu.VMEM((2, t, 1), F32)]
                       + _exchange_scratch(ne),
        compiler_params=_params(("arbitrary", "arbitrary"), VMEM_MID),
    )(qh, do, kh, v, lse4, dl4, *early)


def _s5_mats(a_re, a_im, log_dt, b_re, b_im, c_re, c_im, d_skip):
    Lc = CHUNK
    dt = jnp.exp(log_dt)[:, None]
    lr, li = a_re * dt, a_im * dt

    def apow(n):
        n = jnp.asarray(n, F32)[None, :, None]
        mag = jnp.exp(n * lr[:, None, :])
        ang = n * li[:, None, :]
        return mag * jnp.cos(ang), mag * jnp.sin(ang)

    ar, ai = apow([1.0])
    ar, ai = ar[:, 0], ai[:, 0]
    den = a_re * a_re + a_im * a_im
    nr, ni = ar - 1.0, ai
    fr = (nr * a_re + ni * a_im) / den
    fi = (ni * a_re - nr * a_im) / den
    bbr = fr[:, :, None] * b_re - fi[:, :, None] * b_im
    bbi = fr[:, :, None] * b_im + fi[:, :, None] * b_re
    steps = np.arange(Lc, dtype=np.float32)
    pr, pi = apow(steps)
    car = c_re[:, None] * pr[:, :, None, :] - c_im[:, None] * pi[:, :, None, :]
    cai = c_re[:, None] * pi[:, :, None, :] + c_im[:, None] * pr[:, :, None, :]
    kern = (jnp.einsum('glcp,gpd->glcd', car, bbr, precision=HI)
            - jnp.einsum('glcp,gpd->glcd', cai, bbi, precision=HI))
    skip = d_skip.reshape(GROUPS, CG)[:, :, None] * jnp.eye(CG, dtype=F32)[None]
    kern = kern.at[:, 0].add(skip)
    trow = kern.transpose(0, 3, 1, 2).reshape(GROUPS, CG, Lc * CG)
    p1r, p1i = apow(steps + 1.0)
    cr = c_re[:, None] * p1r[:, :, None, :] - c_im[:, None] * p1i[:, :, None, :]
    ci = c_re[:, None] * p1i[:, :, None, :] + c_im[:, None] * p1r[:, :, None, :]
    to_rows = lambda m: m.transpose(0, 3, 1, 2).reshape(GROUPS, STATE, Lc * CG)
    camat = jnp.concatenate([to_rows(cr), -to_rows(ci)], axis=1)
    qr, qi = apow(Lc - 1.0 - steps)
    zr = qr[:, :, None, :] * bbr.transpose(0, 2, 1)[:, None] - qi[:, :, None, :] * bbi.transpose(0, 2, 1)[:, None]
    zi = qr[:, :, None, :] * bbi.transpose(0, 2, 1)[:, None] + qi[:, :, None, :] * bbr.transpose(0, 2, 1)[:, None]
    bzmat = jnp.concatenate([zr, zi], axis=-1).reshape(GROUPS, Lc * CG, 2 * STATE)
    lr_, li_ = apow([float(Lc)])
    al = jnp.concatenate([lr_[:, 0], li_[:, 0]], axis=-1)
    return trow, camat, bzmat, al


def _s5_scan_powers(a_re, a_im, log_dt, n_steps):
    dt = jnp.exp(log_dt)[:, None]
    lr, li = a_re * dt, a_im * dt
    n = (CHUNK * 2.0 ** np.arange(n_steps)).astype(np.float32)[None, :, None]
    mag = jnp.exp(n * lr[:, None, :])
    pr, pi = mag * jnp.cos(n * li[:, None, :]), mag * jnp.sin(n * li[:, None, :])
    fwd = jnp.stack([jnp.concatenate([pr, pr], -1), jnp.concatenate([-pi, pi], -1)], axis=2)
    bwd = jnp.stack([jnp.concatenate([pr, pr], -1), jnp.concatenate([pi, -pi], -1)], axis=2)
    return fwd, bwd


def _shift_rows(x, sh, down):
    n = x.shape[0]
    ri = lax.broadcasted_iota(jnp.int32, x.shape, 0)
    if down:
        return jnp.where(ri >= sh, pltpu.roll(x, sh, 0), 0.0)
    return jnp.where(ri < n - sh, pltpu.roll(x, n - sh, 0), 0.0)


GPB = 128 // CG


def _lane_transpose(arrs):
    lane = lax.broadcasted_iota(jnp.int32, arrs[0].shape, 1)
    arrs = list(arrs)
    k = GPB // 2
    while k >= 1:
        hi = ((lane // CG) & k) != 0
        new = list(arrs)
        for i in range(GPB):
            if i & k:
                continue
            lo_arr, hi_arr = arrs[i], arrs[i + k]
            new[i] = jnp.where(hi, pltpu.roll(hi_arr, CG * k, 1), lo_arr)
            new[i + k] = jnp.where(hi, hi_arr, pltpu.roll(lo_arr, 128 - CG * k, 1))
        arrs = new
        k //= 2
    return arrs


def _gather_block(ref, dst, nch):
    for half in range(CHUNK // GPB):
        outs = _lane_transpose([ref[pl.ds(half * GPB + l8, nch, stride=CHUNK), :] for l8 in range(GPB)])
        for g in range(GPB):
            dst[half, g] = outs[g]


def _scatter_block(src, ref, nch):
    for half in range(CHUNK // GPB):
        outs = _lane_transpose([src[half, g] for g in range(GPB)])
        for l8 in range(GPB):
            ref[pl.ds(half * GPB + l8, nch, stride=CHUNK), :] = outs[l8]


def _toeplitz(trow):
    lane = lax.broadcasted_iota(jnp.int32, (CG, 128), 1)
    x0, x1 = trow[:, :128], trow[:, 128:]
    zero = jnp.zeros_like(x0)
    rows = []
    for s in range(CHUNK):
        sh = (CG * s) % 128
        r0 = pltpu.roll(x0, sh, 1) if sh else x0
        r1 = pltpu.roll(x1, sh, 1) if sh else x1
        if CG * s < 128:
            rows.append(jnp.concatenate([jnp.where(lane >= sh, r0, zero), jnp.where(lane >= sh, r1, r0)], axis=1))
        else:
            rows.append(jnp.concatenate([zero, jnp.where(lane >= sh, r0, zero)], axis=1))
    return jnp.concatenate(rows, axis=0)


def _toeplitz_adjoint(dt):
    lane = lax.broadcasted_iota(jnp.int32, (CG, 128), 1)
    acc0 = jnp.zeros((CG, 128), F32)
    acc1 = jnp.zeros((CG, 128), F32)
    for s in range(CHUNK):
        x0, x1 = dt[CG * s:CG * s + CG, :128], dt[CG * s:CG * s + CG, 128:]
        sh = (CG * s) % 128
        keep = 128 - sh
        r0 = pltpu.roll(x0, keep, 1) if sh else x0
        r1 = pltpu.roll(x1, keep, 1) if sh else x1
        if CG * s < 128:
            acc0 = acc0 + jnp.where(lane < keep, r0, r1)
            acc1 = acc1 + jnp.where(lane < keep, r1, 0.0)
        else:
            acc0 = acc0 + jnp.where(lane < keep, r1, 0.0)
    return jnp.concatenate([acc0, acc1], axis=1)


def _s5_fwd(u, trow, camat, bzmat, pw):
    S = u.shape[0]
    nch = S // CHUNK
    n_steps = pw.shape[1]

    def body(u_ref, t_ref, ca_ref, bz_ref, pw_ref, y_ref, xp_ref, uc_ref, ub_s, yb_s):
        g = pl.program_id(1)

        @pl.when(g == 0)
        def _():
            _gather_block(u_ref, ub_s, nch)

        uc = jnp.concatenate([ub_s[0, g], ub_s[1, g]], axis=1)
        uc_ref[...] = uc
        x = _s5mm(uc, bz_ref[...])
        for kk in range(n_steps):
            xs = _shift_rows(x, 2 ** kk, True)
            m = pw_ref[kk]
            x = x + m[0:1, :] * xs + m[1:2, :] * pltpu.roll(xs, STATE, 1)
        xp = _shift_rows(x, 1, True)
        xp_ref[...] = xp
        yc = _s5mm(uc, _toeplitz(t_ref[...])) + _s5mm(xp, ca_ref[...])
        yb_s[0, g] = yc[:, :128]
        yb_s[1, g] = yc[:, 128:]

        @pl.when(g == GPB - 1)
        def _():
            _scatter_block(yb_s, y_ref, nch)

    per = lambda a: pl.BlockSpec((None,) + a.shape[1:], lambda b, g: (b * GPB + g,) + (0,) * (a.ndim - 1))
    nat = pl.BlockSpec((S, 128), lambda b, g: (0, b))
    return pl.pallas_call(
        body, name="s5_fwd", grid=(GROUPS // GPB, GPB),
        in_specs=[nat, per(trow), per(camat), per(bzmat), per(pw)],
        out_specs=[nat, pl.BlockSpec((None, nch, 2 * STATE), lambda b, g: (b * GPB + g, 0, 0)),
                   pl.BlockSpec((None, nch, CHUNK * CG), lambda b, g: (b * GPB + g, 0, 0))],
        out_shape=[jax.ShapeDtypeStruct((S, GROUPS * CG), F32), jax.ShapeDtypeStruct((GROUPS, nch, 2 * STATE), F32),
                   jax.ShapeDtypeStruct((GROUPS, nch, CHUNK * CG), F32)],
        scratch_shapes=[pltpu.VMEM((CHUNK // GPB, GPB, nch, 128), F32)] * 2,
        compiler_params=_params(("parallel", "arbitrary"), VMEM_BIG),
    )(u, trow, camat, bzmat, pw)


def _s5_bwd(uc, dy, xp, trow, camat, bzmat, pwc):
    S = dy.shape[0]
    nch = S // CHUNK
    n_steps = pwc.shape[1]

    def body(uc_ref, dy_ref, xp_ref, t_ref, ca_ref, bz_ref, pw_ref, du_ref, dt_ref, dca_ref, dbz_ref, dal_ref,
             dyb_s, dub_s):
        g = pl.program_id(1)

        @pl.when(g == 0)
        def _():
            _gather_block(dy_ref, dyb_s, nch)

        uc = uc_ref[...]
        dyc = jnp.concatenate([dyb_s[0, g], dyb_s[1, g]], axis=1)
        xpv = xp_ref[...]
        dt_ref[...] = _toeplitz_adjoint(_s5mm_tn(uc, dyc))
        dca_ref[...] = _s5mm_tn(xpv, dyc)
        dx = _shift_rows(_s5mm_nt(dyc, ca_ref[...]), 1, False)
        for kk in range(n_steps):
            xs = _shift_rows(dx, 2 ** kk, False)
            m = pw_ref[kk]
            dx = dx + m[0:1, :] * xs + m[1:2, :] * pltpu.roll(xs, STATE, 1)
        dbz_ref[...] = _s5mm_tn(uc, dx)
        dal_ref[0:1, :] = jnp.sum(dx * xpv, axis=0, keepdims=True)
        dal_ref[1:2, :] = jnp.sum(dx * pltpu.roll(xpv, STATE, 1), axis=0, keepdims=True)
        duc = _s5mm_nt(dyc, _toeplitz(t_ref[...])) + _s5mm_nt(dx, bz_ref[...])
        dub_s[0, g] = duc[:, :128]
        dub_s[1, g] = duc[:, 128:]

        @pl.when(g == GPB - 1)
        def _():
            _scatter_block(dub_s, du_ref, nch)

    per = lambda a: pl.BlockSpec((None,) + a.shape[1:], lambda b, g: (b * GPB + g,) + (0,) * (a.ndim - 1))
    nat = pl.BlockSpec((S, 128), lambda b, g: (0, b))
    sds = jax.ShapeDtypeStruct
    mats = [sds(trow.shape, F32), sds(camat.shape, F32), sds(bzmat.shape, F32), sds((GROUPS, 2, 2 * STATE), F32)]
    return pl.pallas_call(
        body, name="s5_bwd", grid=(GROUPS // GPB, GPB),
        in_specs=[per(uc), nat, per(xp), per(trow), per(camat), per(bzmat), per(pwc)],
        out_specs=[nat] + [per(o) for o in mats], out_shape=[sds((S, GROUPS * CG), F32)] + mats,
        scratch_shapes=[pltpu.VMEM((CHUNK // GPB, GPB, nch, 128), F32)] * 2,
        compiler_params=_params(("parallel", "arbitrary"), VMEM_BIG),
    )(uc, dy, xp, trow, camat, bzmat, pwc)


GELU_C0 = math.sqrt(2.0 / math.pi)
GELU_C1 = 0.044715


def _mix(o, za, ys, zb, ga, gb, x, tgt, gate, b_glu, g_final, w_glu, w_up_a, w_up_b, w_out, hsel, ts):
    S = o.shape[0]

    def body(o_ref, za_ref, ys_ref, zb_ref, ga_ref, gb_ref, x_ref, t_ref, gate_ref, bglu_ref, gf_ref,
             wglu_ref, wua_ref, wub_ref, wout_ref, hsel_ref,
             dx2_ref, do_ref, dza_ref, dzb_ref, dga_ref, dgb_ref, dys_ref, dl_ref,
             mg_ref, dmo_ref, ya_ref, dua_ref, yb_ref, dub_ref, yg_ref, dgl_ref,
             dbglu_ref, dgate_ref, dgf_ref, loss_ref):
        @pl.when(pl.program_id(0) == 0)
        def _():
            dbglu_ref[...] = jnp.zeros_like(dbglu_ref)
            dgate_ref[...] = jnp.zeros_like(dgate_ref)
            dgf_ref[...] = jnp.zeros_like(dgf_ref)
            loss_ref[...] = jnp.zeros_like(loss_ref)

        ov = o_ref[...]
        za = za_ref[...]
        sza = _sigmoid(za)
        silu_a = za * sza
        ya = ov * silu_a
        ya_b = ya.astype(ya_ref.dtype)
        ya_ref[...] = ya_b
        ysv = ys_ref[...]
        th = jnp.tanh(GELU_C0 * (ysv + GELU_C1 * ysv * ysv * ysv))
        yg = 0.5 * ysv * (1.0 + th)
        yg_b = yg.astype(yg_ref.dtype)
        yg_ref[...] = yg_b
        sg = _sigmoid(_mm(yg_b, wglu_ref[...]) + bglu_ref[...])
        yb1 = yg * sg
        zb = zb_ref[...]
        szb = _sigmoid(zb)
        silu_b = zb * szb
        yb_b = (yb1 * silu_b).astype(yb_ref.dtype)
        yb_ref[...] = yb_b
        ua = _mm(ya_b, wua_ref[...])
        ub = _mm(yb_b, wub_ref[...])
        sa = _sigmoid(ga_ref[...])
        sb = _sigmoid(gb_ref[...])
        merged_b = (sa * ua + sb * ub).astype(mg_ref.dtype)
        mg_ref[...] = merged_b
        mo = _mm(merged_b, wout_ref[...])
        gate_v = gate_ref[...]
        x2 = x_ref[...] + gate_v * mo
        r2 = lax.rsqrt(jnp.mean(x2 * x2, axis=-1, keepdims=True) + EPS)
        x2n = x2 * r2
        gf = gf_ref[...]
        diff = x2n * gf - t_ref[...]
        loss_ref[...] += jnp.sum(jnp.sum(diff * diff, axis=-1, keepdims=True), axis=0, keepdims=True) * (0.5 / D_MODEL)
        dy = diff * (1.0 / D_MODEL)
        dgf_ref[...] += jnp.sum(dy * x2n, axis=0, keepdims=True)
        dyg = dy * gf
        dx2 = r2 * (dyg - x2n * jnp.mean(dyg * x2n, axis=-1, keepdims=True))
        dx2_ref[...] = dx2
        dgate_ref[...] += jnp.sum(dx2 * mo, axis=0, keepdims=True)
        dmo_b = (dx2 * gate_v).astype(dmo_ref.dtype)
        dmo_ref[...] = dmo_b
        dmerged = _mm_nt(dmo_b, wout_ref[...])
        dua_b = (dmerged * sa).astype(dua_ref.dtype)
        dub_b = (dmerged * sb).astype(dub_ref.dtype)
        dua_ref[...] = dua_b
        dub_ref[...] = dub_b
        dga_ref[...] = (dmerged * ua * sa * (1.0 - sa)).astype(dga_ref.dtype)
        dgb_ref[...] = (dmerged * ub * sb * (1.0 - sb)).astype(dgb_ref.dtype)
        dya = _mm_nt(dua_b, wua_ref[...])
        dyb = _mm_nt(dub_b, wub_ref[...])
        dov = dya * silu_a
        do_ref[...] = dov.astype(do_ref.dtype)
        dl_ref[...] = _mm32_nt(hsel_ref[...], dov * ov)
        dza_ref[...] = (dya * ov * (sza * (1.0 + za * (1.0 - sza)))).astype(dza_ref.dtype)
        dyb1 = dyb * silu_b
        dzb_ref[...] = (dyb * yb1 * (szb * (1.0 + zb * (1.0 - szb)))).astype(dzb_ref.dtype)
        dgl = dyb1 * yg * sg * (1.0 - sg)
        dbglu_ref[...] += jnp.sum(dgl, axis=0, keepdims=True)
        dgl_b = dgl.astype(dgl_ref.dtype)
        dgl_ref[...] = dgl_b
        dyg2 = dyb1 * sg + _mm_nt(dgl_b, wglu_ref[...])
        dgelu = 0.5 * (1.0 + th) + 0.5 * ysv * (1.0 - th * th) * GELU_C0 * (1.0 + 3.0 * GELU_C1 * ysv * ysv)
        dys_ref[...] = dyg2 * dgelu

    row = lambda n: pl.BlockSpec((ts, n), lambda i: (i, 0))
    full = lambda a: pl.BlockSpec(a.shape, lambda i: (0,) * a.ndim)
    vec = lambda n: pl.BlockSpec((1, n), lambda i: (0, 0))
    sds = jax.ShapeDtypeStruct
    W, Dm = WIDTH, D_MODEL
    return pl.pallas_call(
        body, name="mix", grid=(S // ts,),
        in_specs=[row(W), row(W), row(W), row(W), row(Dm), row(Dm), row(Dm), row(Dm),
                  full(gate), full(b_glu), full(g_final), full(w_glu), full(w_up_a), full(w_up_b), full(w_out), full(hsel)],
        out_specs=[row(Dm), row(W), row(W), row(W), row(Dm), row(Dm), row(W), pl.BlockSpec((HEADS, ts), lambda i: (0, i)),
                   row(Dm), row(Dm), row(W), row(Dm), row(W), row(Dm), row(W), row(W),
                   vec(W), vec(Dm), vec(Dm), vec(1)],
        out_shape=[sds((S, Dm), F32), sds((S, W), MXU_DTYPE), sds((S, W), MXU_DTYPE), sds((S, W), MXU_DTYPE),
                   sds((S, Dm), MXU_DTYPE), sds((S, Dm), MXU_DTYPE), sds((S, W), F32), sds((HEADS, S), F32),
                   sds((S, Dm), MXU_DTYPE), sds((S, Dm), MXU_DTYPE), sds((S, W), MXU_DTYPE), sds((S, Dm), MXU_DTYPE),
                   sds((S, W), MXU_DTYPE), sds((S, Dm), MXU_DTYPE), sds((S, W), MXU_DTYPE), sds((S, W), MXU_DTYPE),
                   sds((1, W), F32), sds((1, Dm), F32), sds((1, Dm), F32), sds((1, 1), F32)],
        compiler_params=_params(("arbitrary",), VMEM_BIG),
    )(o, za, ys, zb, ga, gb, x, tgt, gate, b_glu, g_final, w_glu, w_up_a, w_up_b, w_out, hsel)


def _matmul_tn(name, a, b, ts):
    S, M = a.shape
    N = b.shape[1]
    tn = min(N, 1024)

    def body(a_ref, b_ref, o_ref):
        @pl.when(pl.program_id(1) == 0)
        def _():
            o_ref[...] = jnp.zeros_like(o_ref)

        o_ref[...] += _mm_tn(a_ref[...], b_ref[...])

    return pl.pallas_call(
        body, name=name, grid=(N // tn, S // ts),
        in_specs=[pl.BlockSpec((ts, M), lambda j, i: (i, 0)), pl.BlockSpec((ts, tn), lambda j, i: (i, j))],
        out_specs=pl.BlockSpec((M, tn), lambda j, i: (0, j)),
        out_shape=jax.ShapeDtypeStruct((M, N), F32),
        compiler_params=_params(("parallel", "arbitrary"), VMEM_MID),
    )(a, b)


def _proj_bwd(dq, dk, dv, dza, du, dzb, dga, dgb, dfl, x, dx2, shift, scale, g_norm, w_main, w_ft, ts):
    S = x.shape[0]

    def body(dq_ref, dk_ref, dv_ref, dza_ref, du_ref, dzb_ref, dga_ref, dgb_ref, dfl_ref, x_ref, dx2_ref,
             sc_ref, gn_ref, w_ref, wft_ref, gx_ref, dsh_ref, dsc_ref, dgn_ref):
        @pl.when(pl.program_id(0) == 0)
        def _():
            dsh_ref[...] = jnp.zeros_like(dsh_ref)
            dsc_ref[...] = jnp.zeros_like(dsc_ref)
            dgn_ref[...] = jnp.zeros_like(dgn_ref)

        def seg(ref, off, n):
            return _mm_nt(ref[...], w_ref[:, off:off + n])

        dh = (seg(dq_ref, M_Q, WIDTH) + seg(dk_ref, M_K, WIDTH) + seg(dv_ref, M_V, WIDTH)
              + seg(dza_ref, M_ZA, WIDTH) + seg(du_ref, M_U, WIDTH) + seg(dzb_ref, M_ZB, WIDTH)
              + seg(dga_ref, M_GA, D_MODEL) + seg(dgb_ref, M_GB, D_MODEL)
              + _mm32(dfl_ref[...], wft_ref[...]))
        xv = x_ref[...]
        r = lax.rsqrt(jnp.mean(xv * xv, axis=-1, keepdims=True) + EPS)
        xn = xv * r
        gn = gn_ref[...]
        s1 = 1.0 + sc_ref[...]
        dsh_ref[...] += jnp.sum(dh, axis=0, keepdims=True)
        dhx = dh * xn
        dsc_ref[...] += jnp.sum(dhx, axis=0, keepdims=True) * gn
        dgn_ref[...] += jnp.sum(dhx, axis=0, keepdims=True) * s1
        dxn = dh * (gn * s1)
        gx_ref[...] = dx2_ref[...] + r * (dxn - xn * jnp.mean(dxn * xn, axis=-1, keepdims=True))

    row = lambda n: pl.BlockSpec((ts, n), lambda i: (i, 0))
    full = lambda a: pl.BlockSpec(a.shape, lambda i: (0,) * a.ndim)
    vec = pl.BlockSpec((1, D_MODEL), lambda i: (0, 0))
    W, Dm = WIDTH, D_MODEL
    del shift
    return pl.pallas_call(
        body, name="proj_bwd", grid=(S // ts,),
        in_specs=[row(W)] * 6 + [row(Dm)] * 2 + [row(HEADS), row(Dm), row(Dm),
                                                 full(scale), full(g_norm), full(w_main), full(w_ft)],
        out_specs=[row(Dm), vec, vec, vec],
        out_shape=[jax.ShapeDtypeStruct((S, Dm), F32)] + [jax.ShapeDtypeStruct((1, Dm), F32)] * 3,
        compiler_params=_params(("arbitrary",), VMEM_BIG),
    )(dq, dk, dv, dza, du, dzb, dga, dgb, dfl, x, dx2, scale, g_norm, w_main, w_ft)


def _adamw(name, planes, w, m, v, tr):
    n, R, C = planes.shape
    bc1 = 1.0 - ADAM_B1 ** ADAM_STEP
    bc2 = 1.0 - ADAM_B2 ** ADAM_STEP

    def body(p_ref, w_ref, m_ref, v_ref, g_ref, d_ref, nm_ref, nv_ref):
        g = p_ref[0].astype(F32)
        for i in range(1, n):
            g = g + p_ref[i].astype(F32)
        g_ref[...] = g
        nm = ADAM_B1 * m_ref[...] + (1.0 - ADAM_B1) * g
        nv = ADAM_B2 * v_ref[...] + (1.0 - ADAM_B2) * (g * g)
        nm_ref[...] = nm
        nv_ref[...] = nv
        d_ref[...] = -ADAM_LR * ((nm / bc1) / (jnp.sqrt(nv / bc2) + ADAM_EPS) + ADAM_WD * w_ref[...])

    blk = pl.BlockSpec((tr, C), lambda i: (i, 0))
    return pl.pallas_call(
        body, name=name, grid=(R // tr,),
        in_specs=[pl.BlockSpec((n, tr, C), lambda i: (0, i, 0)), blk, blk, blk],
        out_specs=[blk] * 4, out_shape=[jax.ShapeDtypeStruct((R, C), F32)] * 4,
        compiler_params=_params(("parallel",), VMEM_MID),
    )(planes, w, m, v)


def _wada_grad(c_all, dmod_cols):
    def body(c_ref, d_ref, o_ref):
        o_ref[0] = _mm32_tn(c_ref[...], d_ref[...])

    return pl.pallas_call(
        body, name="wada_grad",
        out_shape=jax.ShapeDtypeStruct((1, c_all.shape[1], dmod_cols.shape[1]), F32),
        in_specs=[VMEM, VMEM], out_specs=VMEM,
    )(c_all, dmod_cols)


SMALL_ORDER = ("b_ada", "g_norm", "b_f", "a_re", "a_im", "log_dt", "b_re", "b_im", "c_re", "c_im",
               "d_skip", "b_glu", "g_final")
BIG_ORDER = ("w_ada", "w_in", "w_glu", "w_up_a", "w_up_b", "w_out")
ALL_ORDER = ("w_ada", "b_ada", "g_norm", "w_in", "b_f", "a_re", "a_im", "log_dt", "b_re", "b_im", "c_re", "c_im",
             "d_skip", "w_glu", "b_glu", "w_up_a", "w_up_b", "w_out", "g_final")


def _pack_small(parts, rows):
    flat = jnp.concatenate([p.reshape(-1).astype(F32) for p in parts])
    return jnp.pad(flat, (0, rows * 128 - flat.shape[0])).reshape(rows, 128)


def kernel(x, c, w_ada, b_ada, g_norm, w_in, b_f, a_re, a_im, log_dt, b_re, b_im, c_re, c_im, d_skip, w_glu, b_glu, w_up_a, w_up_b, w_out, g_final, loss_target, m_w_ada, m_b_ada, m_g_norm, m_w_in, m_b_f, m_a_re, m_a_im, m_log_dt, m_b_re, m_b_im, m_c_re, m_c_im, m_d_skip, m_w_glu, m_b_glu, m_w_up_a, m_w_up_b, m_w_out, m_g_final, v_w_ada, v_b_ada, v_g_norm, v_w_in, v_b_f, v_a_re, v_a_im, v_log_dt, v_b_re, v_b_im, v_c_re, v_c_im, v_d_skip, v_w_glu, v_b_glu, v_w_up_a, v_w_up_b, v_w_out, v_g_final):
    weights = dict(w_ada=w_ada, b_ada=b_ada, g_norm=g_norm, w_in=w_in, b_f=b_f, a_re=a_re, a_im=a_im, log_dt=log_dt,
                   b_re=b_re, b_im=b_im, c_re=c_re, c_im=c_im, d_skip=d_skip, w_glu=w_glu, b_glu=b_glu,
                   w_up_a=w_up_a, w_up_b=w_up_b, w_out=w_out, g_final=g_final)
    mom_m = dict(w_ada=m_w_ada, b_ada=m_b_ada, g_norm=m_g_norm, w_in=m_w_in, b_f=m_b_f, a_re=m_a_re, a_im=m_a_im,
                 log_dt=m_log_dt, b_re=m_b_re, b_im=m_b_im, c_re=m_c_re, c_im=m_c_im, d_skip=m_d_skip, w_glu=m_w_glu,
                 b_glu=m_b_glu, w_up_a=m_w_up_a, w_up_b=m_w_up_b, w_out=m_w_out, g_final=m_g_final)
    mom_v = dict(w_ada=v_w_ada, b_ada=v_b_ada, g_norm=v_g_norm, w_in=v_w_in, b_f=v_b_f, a_re=v_a_re, a_im=v_a_im,
                 log_dt=v_log_dt, b_re=v_b_re, b_im=v_b_im, c_re=v_c_re, c_im=v_c_im, d_skip=v_d_skip, w_glu=v_w_glu,
                 b_glu=v_b_glu, w_up_a=v_w_up_a, w_up_b=v_w_up_b, w_out=v_w_out, g_final=v_g_final)
    xs = x[0]
    tgt = loss_target[0]
    S = xs.shape[0]
    ts = min(256, S)
    ta = min(512, S)
    tw = min(2048, S)
    nch = S // CHUNK
    n_steps = max(1, int(math.ceil(math.log2(nch))))
    me = _my_index()

    shards = [w.astype(MXU_DTYPE) for w in (w_in[0], w_glu[0], w_up_a[0], w_up_b[0], w_out[0])]
    mod8, c_all, gathered = _comm_in(c, w_ada[0], b_ada.reshape(N_DEV, -1), shards[:1])
    mod = mod8.reshape(1, 3 * D_MODEL)
    shift, scale, gate = mod[:, :D_MODEL], mod[:, D_MODEL:2 * D_MODEL], mod[:, 2 * D_MODEL:]
    w_in_full = gathered[0].transpose(1, 0, 2).reshape(D_MODEL, PROJ_WIDTH)
    w_main = jnp.concatenate([w_in_full[:, :OFF_F], w_in_full[:, OFF_F + HEADS:]], axis=1)
    w_f = w_in_full[:, OFF_F:OFF_F + HEADS].astype(F32)
    w_ft = w_f.T

    q, k, v, za, u, zb, ga, gb, flc, hb, *late = _proj_fwd(xs, shift, scale, g_norm, w_main, w_f, shards[1:], ts)
    w_glu_full = late[0].reshape(WIDTH, WIDTH)
    w_up_a_full = late[1].transpose(1, 0, 2).reshape(WIDTH, D_MODEL)
    w_up_b_full = late[2].transpose(1, 0, 2).reshape(WIDTH, D_MODEL)
    w_out_full = late[3].reshape(D_MODEL, D_MODEL)
    nb = S // ta
    rows4 = lambda r: r.reshape(PAIRS, 2, nb, ta).transpose(0, 2, 1, 3)
    qh, kh, vt = _attn_prep(q, k, v, flc, b_f, ta)
    o, lse4 = _attn_fwd(qh, kh, vt, ta)

    s5_params = (a_re[0], a_im[0], log_dt[0], b_re[0], b_im[0], c_re[0], c_im[0], d_skip[0])
    (trow, camat, bzmat, al), mats_vjp = jax.vjp(_s5_mats, *s5_params)
    del al
    pw_f, pw_b = _s5_scan_powers(a_re[0], a_im[0], log_dt[0], n_steps)
    ys, xprev, uc = _s5_fwd(u, trow, camat, bzmat, pw_f)

    hsel = (np.arange(WIDTH)[None, :] // 64 == np.arange(HEADS)[:, None]).astype(np.float32)
    (dx2, do, dza, dzb, dga, dgb, dys, dl_row, merged, dmo, ya, dua, yb, dub, yg, dgl,
     db_glu, dgate, dg_final, loss_part) = _mix(o, za, ys, zb, ga, gb, xs, tgt, gate, b_glu, g_final.reshape(1, -1),
                                                w_glu_full, w_up_a_full, w_up_b_full, w_out_full, jnp.asarray(hsel), ts)

    gw_out = _matmul_tn("dw_out", merged, dmo, tw)
    gw_up_a = _matmul_tn("dw_up_a", ya, dua, tw)
    gw_up_b = _matmul_tn("dw_up_b", yb, dub, tw)
    gw_glu = _matmul_tn("dw_glu", yg, dgl, tw)

    du, d_trow, d_camat, d_bzmat, dal2 = _s5_bwd(uc, dys, xprev, trow, camat, bzmat, pw_b)
    d_al = jnp.concatenate([dal2[:, 0, :STATE] + dal2[:, 0, STATE:], dal2[:, 1, STATE:] - dal2[:, 1, :STATE]], axis=-1)
    gs5 = mats_vjp((d_trow, d_camat, d_bzmat, d_al))

    dl4 = rows4(dl_row)
    early = [p.astype(MXU_DTYPE) for p in (gw_glu.reshape(N_DEV, -1, WIDTH),
                                           gw_up_a.reshape(WIDTH, N_DEV, -1).transpose(1, 0, 2),
                                           gw_up_b.reshape(WIDTH, N_DEV, -1).transpose(1, 0, 2),
                                           gw_out.reshape(N_DEV, -1, D_MODEL))]
    dq, dk, dv, dfq4, dfk4, *early_recv = _attn_bwd(qh, do, kh, v, lse4, dl4, early, ta)
    d_fcol = dfq4.transpose(0, 2, 1, 3).reshape(HEADS, S).T - dfk4.transpose(1, 0, 2).reshape(S, HEADS)
    dfl, db_f = _fgate_bwd(d_fcol, flc, b_f, ta)

    grad_x, dshift, dscale, dg_norm = _proj_bwd(dq, dk, dv, dza, du, dzb, dga, dgb, dfl, xs, dx2,
                                                shift, scale, g_norm, w_main, w_ft, ts)
    segs = [("dw_q", dq), ("dw_k", dk), ("dw_v", dv), ("dw_f", dfl), ("dw_za", dza), ("dw_u", du), ("dw_zb", dzb),
            ("dw_ga", dga), ("dw_gb", dgb)]
    gw_in = jnp.concatenate([_matmul_tn(nm, hb, d, tw) for nm, d in segs], axis=1)

    planes = [gw_in.reshape(D_MODEL, N_DEV, -1).transpose(1, 0, 2).astype(MXU_DTYPE)]
    dmod = jnp.concatenate([dshift, dscale, dgate], axis=1)
    small_parts = [dmod, dg_norm, db_f, gs5[0], gs5[1], gs5[2], gs5[3], gs5[4], gs5[5], gs5[6], gs5[7],
                   db_glu, dg_final, loss_part]
    n_small = sum(int(np.prod(p.shape)) for p in small_parts)
    rows = -(-n_small // (8 * 128)) * 8
    small = _pack_small(small_parts, rows)
    from_sib = _comm_pair(planes)
    core = lax.axis_index("c")
    chip_planes = []
    for name, p, s in zip(("w_in",), planes, from_sib):
        tr = 256 if s.shape[1] % 256 == 0 else s.shape[1]
        chip_planes.append(_pair_sum("pair_sum_" + name, p, s, core, tr))
    recv, small_all = _comm_out(chip_planes, small)

    grads, deltas, new_m, new_v = {}, {}, {}, {}

    def put(name, res, shape):
        grads[name], deltas[name], new_m[name], new_v[name] = [r.reshape(shape) for r in res]

    names = ("w_in", "w_glu", "w_up_a", "w_up_b", "w_out")
    for name, pr in zip(names, recv + early_recv):
        w2 = weights[name][0]
        tr = 256 if w2.shape[0] % 256 == 0 else w2.shape[0]
        put(name, _adamw("adamw_" + name, pr, w2, mom_m[name][0], mom_v[name][0], tr), weights[name].shape)
    cols = w_ada.shape[2]
    dmod_all = small_all[:, :24, :].reshape(N_DEV, 3 * D_MODEL)
    dmod_cols = lax.dynamic_slice_in_dim(dmod_all, me * cols, cols, axis=1)
    g_wada = _wada_grad(c_all, dmod_cols)
    put("w_ada", _adamw("adamw_w_ada", g_wada, w_ada[0], m_w_ada[0], v_w_ada[0], 256), w_ada.shape)
    pack = lambda d: _pack_small([d[n] for n in SMALL_ORDER] + [jnp.zeros((1,), F32)], rows)
    res_small = _adamw("adamw_small", small_all, pack(weights), pack(mom_m), pack(mom_v), rows)
    flat = [r.reshape(-1) for r in res_small]
    off = 0
    for name in SMALL_ORDER:
        shape = weights[name].shape
        size = int(np.prod(shape))
        put(name, [f[off:off + size] for f in flat], shape)
        off += size
    loss = flat[0][off]

    return (loss, grad_x[None], *[grads[n] for n in ALL_ORDER], *[deltas[n] for n in ALL_ORDER],
            *[new_m[n] for n in ALL_ORDER], *[new_v[n] for n in ALL_ORDER])
```

```python
import math

import jax
import jax.numpy as jnp
import numpy as np
from jax import lax
from jax.experimental import pallas as pl
from jax.experimental.pallas import tpu as pltpu

F32 = jnp.float32
MXU_DTYPE = jnp.bfloat16
HI = lax.Precision.HIGHEST

N_DEV = 8
D_MODEL = 1024
WIDTH = 512
HEADS = 8
PAIRS = HEADS // 2
GROUPS = 32
STATE = 64
CG = 16
CHUNK = 16
EPS = 1e-6
NEG = float(np.finfo(np.float32).min)

ADAM_LR = 0.001
ADAM_B1 = 0.9
ADAM_B2 = 0.999
ADAM_EPS = 1e-08
ADAM_WD = 0.01
ADAM_STEP = 10

VMEM_BIG = 56 * 1024 * 1024
VMEM_MID = 40 * 1024 * 1024

OFF_F = 3 * WIDTH
PROJ_WIDTH = 5128
M_Q, M_K, M_V, M_ZA, M_U, M_ZB, M_GA, M_GB = 0, 512, 1024, 1536, 2048, 2560, 3072, 4096


def _mm(a, b):
    return jnp.dot(a.astype(MXU_DTYPE), b.astype(MXU_DTYPE), preferred_element_type=F32)


def _mm_nt(a, b):
    return lax.dot_general(a.astype(MXU_DTYPE), b.astype(MXU_DTYPE), (((1,), (1,)), ((), ())),
                           preferred_element_type=F32)


def _mm_tn(a, b):
    return lax.dot_general(a.astype(MXU_DTYPE), b.astype(MXU_DTYPE), (((0,), (0,)), ((), ())),
                           preferred_element_type=F32)


def _mm32(a, b):
    return jnp.dot(a, b, precision=HI, preferred_element_type=F32)


def _mm32_nt(a, b):
    return lax.dot_general(a, b, (((1,), (1,)), ((), ())), precision=HI, preferred_element_type=F32)


def _mm32_tn(a, b):
    return lax.dot_general(a, b, (((0,), (0,)), ((), ())), precision=HI, preferred_element_type=F32)


S5_PRECISION = lax.Precision.HIGH


def _s5mm(a, b):
    return jnp.dot(a, b, precision=S5_PRECISION, preferred_element_type=F32)


def _s5mm_nt(a, b):
    return lax.dot_general(a, b, (((1,), (1,)), ((), ())), precision=S5_PRECISION, preferred_element_type=F32)


def _s5mm_tn(a, b):
    return lax.dot_general(a, b, (((0,), (0,)), ((), ())), precision=S5_PRECISION, preferred_element_type=F32)


def _sigmoid(x):
    return 1.0 / (1.0 + jnp.exp(-x))


def _params(sem=None, vmem=None):
    kw = {}
    if sem is not None:
        kw["dimension_semantics"] = sem
    if vmem is not None:
        kw["vmem_limit_bytes"] = vmem
    return pltpu.CompilerParams(**kw)


def _my_index():
    return 4 * lax.axis_index("x") + 2 * lax.axis_index("y") + lax.axis_index("c")


def _dev(p):
    return (p // 4, (p // 2) % 2, p % 2)


ANY = pl.BlockSpec(memory_space=pl.ANY)
VMEM = pl.BlockSpec(memory_space=pltpu.VMEM)
MESH = pl.DeviceIdType.MESH


def _comm_in(c, w_ada, b_ada8, shards):
    n = len(shards)
    cols = w_ada.shape[1]

    def body(c_ref, wada_ref, bada_ref, *rest):
        srcs = rest[:n]
        mod_ref, call_ref = rest[n], rest[n + 1]
        dsts = rest[n + 2:2 * n + 2]
        modp, wsend, wrecv, wloc, csend, crecv, msend, mrecv = rest[2 * n + 2:]
        me = _my_index()

        x, y, cc = lax.axis_index("x"), lax.axis_index("y"), lax.axis_index("c")
        here, sib = (x, y, cc), (x, y, 1 - cc)
        xn, yn, dg = (1 - x, y, cc), (x, 1 - y, cc), (1 - x, 1 - y, cc)
        half = srcs[0].shape[0] // 2
        parts = [(0, pl.ds(0, half)), (0, pl.ds(half, half))] + [(a, None) for a in range(1, n)]
        via_y = [i % 2 == 0 for i in range(len(parts))]

        def wcopy(i, k, block, to, own=False):
            a, rs = parts[i]
            dst = dsts[a].at[4 * block[0] + 2 * block[1] + block[2]]
            src = srcs[a] if own else dst
            if rs is not None:
                src, dst = src.at[rs], dst.at[rs]
            return pltpu.make_async_remote_copy(src_ref=src, dst_ref=dst,
                                                send_sem=wsend.at[i, k], recv_sem=wrecv.at[i, k],
                                                device_id=to, device_id_type=MESH)

        def ccopy(src_dev, d, to):
            return pltpu.make_async_remote_copy(src_ref=c_ref, dst_ref=call_ref.at[pl.ds(src_dev, 1)],
                                                send_sem=csend.at[d], recv_sem=crecv.at[src_dev],
                                                device_id=_dev(to), device_id_type=MESH)

        def mcopy(src_dev, d, to):
            return pltpu.make_async_remote_copy(src_ref=modp.at[pl.ds(to, 1)], dst_ref=mod_ref.at[pl.ds(src_dev, 1)],
                                                send_sem=msend.at[d], recv_sem=mrecv.at[src_dev],
                                                device_id=_dev(to), device_id_type=MESH)

        local = [pltpu.make_async_copy(srcs[a], dsts[a].at[me], wloc.at[a]) for a in range(n)]
        for cp in local:
            cp.start()
        peers = [(me + d) % N_DEV for d in range(1, N_DEV)]
        sends = []
        for i in range(len(parts)):
            sends += [wcopy(i, 0, here, sib, own=True), wcopy(i, 1, here, xn, own=True), wcopy(i, 2, here, yn, own=True)]
        for cp in sends:
            cp.start()
        call_ref[pl.ds(me, 1), :] = c_ref[...]
        for d, p in enumerate(peers):
            ccopy(me, d, p).start()
        for d, p in enumerate(peers):
            ccopy(p, d, p).wait_recv()
        modp[...] = _mm32(call_ref[...], wada_ref[...]) + bada_ref[pl.ds(me, 1), :]
        mod_ref[pl.ds(me, 1), :] = modp[pl.ds(me, 1), :]
        for d, p in enumerate(peers):
            mcopy(me, d, p).start()
        for d, p in enumerate(peers):
            mcopy(p, d, p).wait_recv()
        def after(i, k, block, nxt):
            wcopy(i, k, block, here).wait_recv()
            for kk, to in nxt:
                cp = wcopy(i, kk, block, to)
                cp.start()
                sends.append(cp)

        for i in range(len(parts)):
            after(i, 1, xn, [(3, sib)] + ([(5, yn)] if via_y[i] else []))
        for i in range(len(parts)):
            after(i, 2, yn, [(4, sib)] + ([] if via_y[i] else [(6, xn)]))
        for i in range(len(parts)):
            after(i, 5 if via_y[i] else 6, dg, [(7, sib)])
        for i in range(len(parts)):
            wcopy(i, 0, sib, here).wait_recv()
            for k, block in ((3, xn), (4, yn), (7, dg)):
                wcopy(i, k, (block[0], block[1], 1 - cc), here).wait_recv()
        for cp in sends:
            cp.wait_send()
        for d, p in enumerate(peers):
            ccopy(me, d, p).wait_send()
            mcopy(me, d, p).wait_send()
        for cp in local:
            cp.wait()

    out_shape = ([jax.ShapeDtypeStruct((N_DEV, cols), F32), jax.ShapeDtypeStruct((N_DEV, D_MODEL), F32)]
                 + [jax.ShapeDtypeStruct((N_DEV,) + s.shape, s.dtype) for s in shards])
    res = pl.pallas_call(
        body, name="comm_in", out_shape=out_shape,
        in_specs=[VMEM, VMEM, VMEM] + [ANY] * n,
        out_specs=[VMEM, VMEM] + [ANY] * n,
        scratch_shapes=[pltpu.VMEM((N_DEV, cols), F32),
                        pltpu.SemaphoreType.DMA((n + 1, N_DEV)), pltpu.SemaphoreType.DMA((n + 1, N_DEV)),
                        pltpu.SemaphoreType.DMA((n,)),
                        pltpu.SemaphoreType.DMA((N_DEV,)), pltpu.SemaphoreType.DMA((N_DEV,)),
                        pltpu.SemaphoreType.DMA((N_DEV,)), pltpu.SemaphoreType.DMA((N_DEV,))],
        compiler_params=_params(vmem=VMEM_MID),
    )(c, w_ada, b_ada8, *shards)
    return res[0], res[1], list(res[2:])


def _direct_copy(srcs, dsts, send, recv, scatter, a, d, receiving):
    me = _my_index()
    p = (me + d) % N_DEV
    slot = p if receiving else me
    return pltpu.make_async_remote_copy(src_ref=srcs[a].at[p] if scatter else srcs[a], dst_ref=dsts[a].at[slot],
                                        send_sem=send.at[a, d], recv_sem=recv.at[a, slot],
                                        device_id=_dev(p), device_id_type=MESH)


def _hosted_exchange(first, last, srcs, dsts, send, recv, loc, scatter):
    me = _my_index()
    n = len(srcs)
    local = lambda a: pltpu.make_async_copy(srcs[a].at[me] if scatter else srcs[a], dsts[a].at[me], loc.at[a])
    pairs = [(a, d) for d in range(1, N_DEV) for a in range(n)]

    @pl.when(first)
    def _():
        for a in range(n):
            local(a).start()
        for a, d in pairs:
            _direct_copy(srcs, dsts, send, recv, scatter, a, d, False).start()

    @pl.when(last)
    def _():
        for a, d in pairs:
            _direct_copy(srcs, dsts, send, recv, scatter, a, d, True).wait_recv()
            _direct_copy(srcs, dsts, send, recv, scatter, a, d, False).wait_send()
        for a in range(n):
            local(a).wait()


def _exchange_scratch(n):
    return [pltpu.SemaphoreType.DMA((n, N_DEV)), pltpu.SemaphoreType.DMA((n, N_DEV)), pltpu.SemaphoreType.DMA((n,))]


def _hosted_chip_exchange(first, last, srcs, dsts, send, recv, loc):
    x, y, cc = lax.axis_index("x"), lax.axis_index("y"), lax.axis_index("c")
    mine = 2 * x + y
    chips = [(1 - x, y), (x, 1 - y), (1 - x, 1 - y)]
    n = len(srcs)

    def copy(a, j, sending):
        chip = chips[j]
        there = 2 * chip[0] + chip[1]
        return pltpu.make_async_remote_copy(src_ref=srcs[a].at[there], dst_ref=dsts[a].at[mine if sending else there],
                                            send_sem=send.at[a, j], recv_sem=recv.at[a, j],
                                            device_id=(*chip, cc), device_id_type=MESH)

    local = lambda a: pltpu.make_async_copy(srcs[a].at[mine], dsts[a].at[mine], loc.at[a])

    @pl.when(first)
    def _():
        for a in range(n):
            local(a).start()
            for j in range(len(chips)):
                copy(a, j, True).start()

    @pl.when(last)
    def _():
        for a in range(n):
            for j in range(len(chips)):
                copy(a, j, False).wait_recv()
                copy(a, j, True).wait_send()
            local(a).wait()


N_CHIP = 4


def _comm_pair(planes):
    n = len(planes)

    def body(*rest):
        srcs, dsts = rest[:n], rest[n:2 * n]
        send, recv = rest[2 * n:]
        x, y, cc = lax.axis_index("x"), lax.axis_index("y"), lax.axis_index("c")
        copies = [pltpu.make_async_remote_copy(src_ref=srcs[a].at[2 * ch + 1 - cc], dst_ref=dsts[a].at[ch],
                                               send_sem=send.at[a, ch], recv_sem=recv.at[a, ch],
                                               device_id=(x, y, 1 - cc), device_id_type=MESH)
                  for a in range(n) for ch in range(N_CHIP)]
        for cp in copies:
            cp.start()
        for cp in copies:
            cp.wait()

    out_shape = [jax.ShapeDtypeStruct((N_CHIP,) + p.shape[1:], p.dtype) for p in planes]
    return pl.pallas_call(
        body, name="comm_pair", out_shape=out_shape, in_specs=[ANY] * n, out_specs=[ANY] * n,
        scratch_shapes=[pltpu.SemaphoreType.DMA((n, N_CHIP)), pltpu.SemaphoreType.DMA((n, N_CHIP))],
    )(*planes)


def _pair_sum(name, planes, from_sib, core, tr):
    _, R, C = from_sib.shape

    def body(core_ref, a_ref, b_ref, o_ref):
        del core_ref
        o_ref[...] = (a_ref[...].astype(F32) + b_ref[...].astype(F32)).astype(o_ref.dtype)

    blk = pl.BlockSpec((None, tr, C), lambda i, j, c: (i, j, 0))
    grid_spec = pltpu.PrefetchScalarGridSpec(
        num_scalar_prefetch=1, grid=(N_CHIP, R // tr),
        in_specs=[pl.BlockSpec((None, tr, C), lambda i, j, c: (2 * i + c[0], j, 0)), blk], out_specs=blk)
    return pl.pallas_call(
        body, name=name, grid_spec=grid_spec, out_shape=jax.ShapeDtypeStruct(from_sib.shape, from_sib.dtype),
        compiler_params=_params(("parallel", "parallel"), VMEM_MID),
    )(core.reshape(1).astype(jnp.int32), planes, from_sib)


def _comm_small(small):
    rows = small.shape[0]
    cut = (rows // 16) * 8

    def body(small_ref, sall_ref, ssend, srecv, sloc):
        me = _my_index()
        x, y, cc = lax.axis_index("x"), lax.axis_index("y"), lax.axis_index("c")
        here, sib = (x, y, cc), (x, y, 1 - cc)
        xn, yn, dg = (1 - x, y, cc), (x, 1 - y, cc), (1 - x, 1 - y, cc)
        sparts = [pl.ds(0, cut), pl.ds(cut, rows - cut)]
        via_y = [True, False]

        def scopy(i, k, block, to, own=False):
            dst = sall_ref.at[4 * block[0] + 2 * block[1] + block[2]].at[sparts[i]]
            src = small_ref.at[sparts[i]] if own else dst
            return pltpu.make_async_remote_copy(src_ref=src, dst_ref=dst, send_sem=ssend.at[i, k], recv_sem=srecv.at[i, k],
                                                device_id=to, device_id_type=MESH)

        local = pltpu.make_async_copy(small_ref, sall_ref.at[me], sloc)
        local.start()
        sends = []
        for i in range(len(sparts)):
            sends += [scopy(i, 0, here, sib, own=True), scopy(i, 1, here, xn, own=True), scopy(i, 2, here, yn, own=True)]
        for cp in sends:
            cp.start()

        def after(i, k, block, nxt):
            scopy(i, k, block, here).wait_recv()
            for kk, to in nxt:
                cp = scopy(i, kk, block, to)
                cp.start()
                sends.append(cp)

        for i in range(len(sparts)):
            after(i, 1, xn, [(3, sib)] + ([(5, yn)] if via_y[i] else []))
        for i in range(len(sparts)):
            after(i, 2, yn, [(4, sib)] + ([] if via_y[i] else [(6, xn)]))
        for i in range(len(sparts)):
            after(i, 5 if via_y[i] else 6, dg, [(7, sib)])
        for i in range(len(sparts)):
            scopy(i, 0, sib, here).wait_recv()
            for k, block in ((3, xn), (4, yn), (7, dg)):
                scopy(i, k, (block[0], block[1], 1 - cc), here).wait_recv()
        for cp in sends:
            cp.wait_send()
        local.wait()

    return pl.pallas_call(
        body, name="comm_small", out_shape=jax.ShapeDtypeStruct((N_DEV,) + small.shape, small.dtype),
        in_specs=[ANY], out_specs=ANY,
        scratch_shapes=[pltpu.SemaphoreType.DMA((2, N_DEV)), pltpu.SemaphoreType.DMA((2, N_DEV)),
                        pltpu.SemaphoreType.DMA(())],
    )(small)


def _proj_fwd(x, shift, scale, g_norm, w_main, w_f, late, ts):
    S = x.shape[0]
    nl = len(late)

    def body(x_ref, sh_ref, sc_ref, gn_ref, w_ref, wf_ref, *rest):
        (q_ref, k_ref, v_ref, za_ref, u_ref, zb_ref, ga_ref, gb_ref, flc_ref, h_ref) = rest[nl:nl + 10]
        i = pl.program_id(0)
        _hosted_exchange(i == 0, i == pl.num_programs(0) - 1, rest[:nl], rest[nl + 10:2 * nl + 10],
                         *rest[2 * nl + 10:], scatter=False)
        xv = x_ref[...]
        r = lax.rsqrt(jnp.mean(xv * xv, axis=-1, keepdims=True) + EPS)
        h = (xv * r) * gn_ref[...] * (1.0 + sc_ref[...]) + sh_ref[...]
        hb = h.astype(MXU_DTYPE)
        h_ref[...] = hb

        def seg(off, n):
            return jnp.dot(hb, w_ref[:, off:off + n], preferred_element_type=F32)

        q_ref[...] = (seg(M_Q, WIDTH) * 0.125).astype(q_ref.dtype)
        k_ref[...] = seg(M_K, WIDTH).astype(k_ref.dtype)
        v_ref[...] = seg(M_V, WIDTH).astype(v_ref.dtype)
        za_ref[...] = seg(M_ZA, WIDTH)
        u_ref[...] = seg(M_U, WIDTH)
        zb_ref[...] = seg(M_ZB, WIDTH)
        ga_ref[...] = seg(M_GA, D_MODEL)
        gb_ref[...] = seg(M_GB, D_MODEL)
        flc_ref[...] = _mm32(h, wf_ref[...])

    row = lambda n: pl.BlockSpec((ts, n), lambda i: (i, 0))
    full = lambda a: pl.BlockSpec(a.shape, lambda i: (0,) * a.ndim)
    sds = jax.ShapeDtypeStruct
    return pl.pallas_call(
        body, name="proj_fwd", grid=(S // ts,),
        in_specs=[row(D_MODEL), full(shift), full(scale), full(g_norm), full(w_main), full(w_f)] + [ANY] * nl,
        out_specs=[row(WIDTH)] * 6 + [row(D_MODEL)] * 2 + [row(HEADS), row(D_MODEL)] + [ANY] * nl,
        out_shape=[sds((S, WIDTH), MXU_DTYPE)] * 3 + [sds((S, WIDTH), F32)] * 3 + [sds((S, D_MODEL), F32)] * 2
                  + [sds((S, HEADS), F32), sds((S, D_MODEL), MXU_DTYPE)]
                  + [sds((N_DEV,) + w.shape, w.dtype) for w in late],
        scratch_shapes=_exchange_scratch(nl),
        compiler_params=_params(("arbitrary",), VMEM_BIG),
    )(x, shift, scale, g_norm, w_main, w_f, *late)


def _log_sigmoid(z):
    return jnp.minimum(z, 0.0) - jnp.log(1.0 + jnp.exp(-jnp.abs(z)))


def _fgate_bwd(dfc, flc, bf_row, ts):
    S = flc.shape[0]
    n = S // ts

    def body(df_ref, flc_ref, bfr_ref, dfl_ref, dbf_ref, carry):
        @pl.when(pl.program_id(0) == 0)
        def _():
            carry[...] = jnp.zeros_like(carry)
            dbf_ref[...] = jnp.zeros_like(dbf_ref)

        ri = lax.broadcasted_iota(jnp.int32, (ts, ts), 0)
        ci = lax.broadcasted_iota(jnp.int32, (ts, ts), 1)
        upper = (ci >= ri).astype(F32)
        rc = _mm32(upper, df_ref[...]) + carry[...]
        carry[...] = rc[0:1, :]
        z = flc_ref[...] + bfr_ref[...]
        dfl = rc * _sigmoid(-z)
        dfl_ref[...] = dfl
        dbf_ref[...] += jnp.sum(dfl, axis=0, keepdims=True)

    col = pl.BlockSpec((ts, HEADS), lambda i: (n - 1 - i, 0))
    one = pl.BlockSpec((1, HEADS), lambda i: (0, 0))
    return pl.pallas_call(
        body, name="fgate_bwd", grid=(n,),
        in_specs=[col, col, one], out_specs=[col, one],
        out_shape=[jax.ShapeDtypeStruct((S, HEADS), F32), jax.ShapeDtypeStruct((1, HEADS), F32)],
        scratch_shapes=[pltpu.VMEM((1, HEADS), F32)],
        compiler_params=_params(("arbitrary",)),
    )(dfc, flc, bf_row)


N_EXTRA = 3


def _attn_prep(q, k, v, flc, bf_row, t):
    S = q.shape[0]
    nb = S // t

    def body(q_ref, k_ref, v_ref, flc_ref, bfr_ref, qh_ref, kh_ref, vt_ref, carry):
        @pl.when(pl.program_id(0) == 0)
        def _():
            carry[...] = jnp.zeros_like(carry)

        ri = lax.broadcasted_iota(jnp.int32, (t, t), 0)
        ci = lax.broadcasted_iota(jnp.int32, (t, t), 1)
        f = _mm32((ci <= ri).astype(F32), _log_sigmoid(flc_ref[...] + bfr_ref[...])) + carry[...]
        carry[...] = f[t - 1:t, :]
        lane = lax.broadcasted_iota(jnp.int32, (t, 128), 1)
        for p in range(PAIRS):
            qp = q_ref[:, p * 128:(p + 1) * 128]
            kp = k_ref[:, p * 128:(p + 1) * 128]
            vt_ref[p, 0] = v_ref[:, p * 128:(p + 1) * 128].T
            for h in range(2):
                own = (lane < 64) if h == 0 else (lane >= 64)
                base = 64 if h == 0 else 0
                fh = f[:, 2 * p + h:2 * p + h + 1]
                parts = []
                rest = fh
                for _ in range(N_EXTRA):
                    part = rest.astype(qh_ref.dtype)
                    parts.append(part)
                    rest = rest - part.astype(F32)
                one = jnp.ones((t, 1), qh_ref.dtype)
                eq = jnp.zeros((t, 128), qh_ref.dtype)
                ek = jnp.zeros((t, 128), qh_ref.dtype)
                for j in range(N_EXTRA):
                    eq = jnp.where(lane == base + j, parts[j], eq)
                    eq = jnp.where(lane == base + N_EXTRA + j, one, eq)
                    ek = jnp.where(lane == base + j, one, ek)
                    ek = jnp.where(lane == base + N_EXTRA + j, -parts[j], ek)
                qh_ref[2 * p + h] = jnp.where(own, qp, eq)
                kh_ref[2 * p + h] = jnp.where(own, kp, ek)

    row = pl.BlockSpec((t, WIDTH), lambda i: (i, 0))
    heads = pl.BlockSpec((HEADS, t, 128), lambda i: (0, i, 0))
    return pl.pallas_call(
        body, name="attn_prep", grid=(nb,),
        in_specs=[row, row, row, pl.BlockSpec((t, HEADS), lambda i: (i, 0)), pl.BlockSpec((1, HEADS), lambda i: (0, 0))],
        out_specs=[heads, heads, pl.BlockSpec((PAIRS, 1, 128, t), lambda i: (0, i, 0, 0))],
        out_shape=[jax.ShapeDtypeStruct((HEADS, S, 128), q.dtype), jax.ShapeDtypeStruct((HEADS, S, 128), k.dtype),
                   jax.ShapeDtypeStruct((PAIRS, nb, 128, t), v.dtype)],
        scratch_shapes=[pltpu.VMEM((1, HEADS), F32)],
        compiler_params=_params(("arbitrary",), VMEM_MID),
    )(q, k, v, flc, bf_row)


def _attn_fwd(qh, kh, vt, t):
    S = qh.shape[1]
    nb = S // t

    def body(q_ref, k_ref, vt_ref, o_ref, lse_ref, acc_s):
        qi = pl.program_id(1)
        acc_s[...] = jnp.zeros_like(acc_s)

        def step(ki, nblk, masked, carry):
            m_old, l_old = carry[:2], carry[2:]
            ks = pl.multiple_of(ki * t, t)
            rows = nblk * t
            sts = [_mm_nt(k_ref[h, pl.ds(ks, rows), :], q_ref[h]) for h in range(2)]
            if masked:
                ri = lax.broadcasted_iota(jnp.int32, (t, t), 0)
                ci = lax.broadcasted_iota(jnp.int32, (t, t), 1)
                sts = [jnp.where(ci >= ri, st, NEG) for st in sts]
            m_new = [jnp.maximum(m_old[h], jnp.max(sts[h], axis=0, keepdims=True)) for h in range(2)]
            alpha = [jnp.exp(m_old[h] - m_new[h]) for h in range(2)]
            pts = [jnp.exp(sts[h] - m_new[h]) for h in range(2)]
            l_new = [alpha[h] * l_old[h] + jnp.sum(pts[h], axis=0, keepdims=True) for h in range(2)]
            for h in range(2):
                pv = _mm(vt_ref[ki], pts[h][:t])
                for b in range(1, nblk):
                    pv = pv + _mm(vt_ref[ki + b], pts[h][b * t:(b + 1) * t])
                acc_s[h] = alpha[h] * acc_s[h] + pv
            return (*m_new, *l_new)

        init = (jnp.full((1, t), -jnp.inf, F32),) * 2 + (jnp.zeros((1, t), F32),) * 2
        carry = lax.fori_loop(0, qi // 2, lambda j, c: step(2 * j, 2, False, c), init)
        carry = lax.cond(qi % 2 == 1, lambda c: step(qi - 1, 1, False, c), lambda c: c, carry)
        m0, m1, l0, l1 = step(qi, 1, True, carry)
        first = lax.broadcasted_iota(jnp.int32, (128, t), 0) < 64
        o_ref[...] = jnp.where(first, acc_s[0] / l0, acc_s[1] / l1).T
        lse_ref[...] = jnp.concatenate([m0 + jnp.log(l0), m1 + jnp.log(l1)], axis=0)

    return pl.pallas_call(
        body, name="attn_fwd", grid=(PAIRS, nb),
        in_specs=[pl.BlockSpec((2, t, 128), lambda p, i: (p, i, 0)), pl.BlockSpec((2, S, 128), lambda p, i: (p, 0, 0)),
                  pl.BlockSpec((None, nb, 128, t), lambda p, i: (p, 0, 0, 0))],
        out_specs=[pl.BlockSpec((t, 128), lambda p, i: (i, p)), pl.BlockSpec((None, None, 2, t), lambda p, i: (p, i, 0, 0))],
        out_shape=[jax.ShapeDtypeStruct((S, WIDTH), F32), jax.ShapeDtypeStruct((PAIRS, nb, 2, t), F32)],
        scratch_shapes=[pltpu.VMEM((2, 128, t), F32)],
        compiler_params=_params(("parallel", "parallel"), VMEM_MID),
    )(qh, kh, vt)


def _attn_bwd(qh, do, kh, v, lse4, dl4, early, t):
    S = qh.shape[1]
    nb = S // t
    ne = len(early)

    def body(q_ref, do_ref, k_ref, v_ref, lse_ref, dl_ref, *rest):
        dq_ref, dk_ref, dv_ref, dfq_ref, dfk_ref = rest[ne:ne + 5]
        kc_s, vh_s, dk_s, dv_s, dfk_s = rest[2 * ne + 5:2 * ne + 10]
        kj = pl.program_id(1)
        _hosted_exchange((pl.program_id(0) == 0) & (kj == 0), (pl.program_id(0) == PAIRS - 1) & (kj == nb - 1),
                         rest[:ne], rest[ne + 5:2 * ne + 5], *rest[2 * ne + 10:], scatter=True)
        lane = lax.broadcasted_iota(jnp.int32, (t, 128), 1)
        is_a = lane < 64

        @pl.when(kj == 0)
        def _():
            dq_ref[...] = jnp.zeros_like(dq_ref)
            dfq_ref[...] = jnp.zeros_like(dfq_ref)

        vp = v_ref[...]
        zero = jnp.zeros_like(vp)
        kc_s[0] = jnp.where(is_a, k_ref[0], zero.astype(kc_s.dtype))
        kc_s[1] = jnp.where(is_a, zero.astype(kc_s.dtype), k_ref[1])
        vh_s[0] = jnp.where(is_a, vp, zero)
        vh_s[1] = jnp.where(is_a, zero, vp)
        dk_s[...] = jnp.zeros_like(dk_s)
        dv_s[...] = jnp.zeros_like(dv_s)
        dfk_s[...] = jnp.zeros_like(dfk_s)

        def step(qi, masked):
            qs = pl.multiple_of(qi * t, t)
            dob = do_ref[pl.ds(qs, t), :]
            lse = lse_ref[qi]
            dl = dl_ref[qi]
            zq = jnp.zeros_like(dob)
            over_keys = []
            for h in range(2):
                sel = is_a if h == 0 else jnp.logical_not(is_a)
                qb = q_ref[h, pl.ds(qs, t), :]
                st = _mm_nt(k_ref[h], qb) - lse[h:h + 1, :]
                if masked:
                    ri = lax.broadcasted_iota(jnp.int32, (t, t), 0)
                    ci = lax.broadcasted_iota(jnp.int32, (t, t), 1)
                    st = jnp.where(ci >= ri, st, NEG)
                pt = jnp.exp(st)
                dv_s[...] += _mm(pt, jnp.where(sel, dob, zq))
                dpt = _mm_nt(vh_s[h], dob)
                dst = pt * (dpt - dl[h:h + 1, :])
                dfk_s[h] += jnp.sum(dst, axis=1, keepdims=True)
                over_keys.append(jnp.sum(dst, axis=0, keepdims=True))
                dk_s[...] += _mm(dst, jnp.where(sel, qb, jnp.zeros_like(qb)))
                dq_ref[pl.ds(qs, t), :] += _mm_tn(dst, kc_s[h])
            dfq_ref[qi] += jnp.concatenate(over_keys, axis=0)

        step(kj, True)

        def loop_body(qi, carry):
            step(qi, False)
            return carry

        lax.fori_loop(kj + 1, nb, loop_body, 0)
        dk_ref[...] = dk_s[...].astype(dk_ref.dtype)
        dv_ref[...] = dv_s[...].astype(dv_ref.dtype)
        dfk_ref[...] = jnp.where(lax.broadcasted_iota(jnp.int32, (t, 2), 1) == 0, dfk_s[0], dfk_s[1])

        @pl.when(kj == nb - 1)
        def _():
            dq_ref[...] = dq_ref[...] * 0.125

    blk = pl.BlockSpec((t, 128), lambda p, j: (j, p))
    res = pl.BlockSpec((S, 128), lambda p, j: (0, p))
    rows4 = pl.BlockSpec((None, nb, 2, t), lambda p, j: (p, 0, 0, 0))
    cols4 = pl.BlockSpec((None, t, 2), lambda p, j: (p, j, 0))
    return pl.pallas_call(
        body, name="attn_bwd", grid=(PAIRS, nb),
        in_specs=[pl.BlockSpec((2, S, 128), lambda p, j: (p, 0, 0)), res,
                  pl.BlockSpec((2, t, 128), lambda p, j: (p, j, 0)), blk, rows4, rows4] + [ANY] * ne,
        out_specs=[res, blk, blk, rows4, cols4] + [ANY] * ne,
        out_shape=[jax.ShapeDtypeStruct((S, WIDTH), F32), jax.ShapeDtypeStruct((S, WIDTH), MXU_DTYPE),
                   jax.ShapeDtypeStruct((S, WIDTH), MXU_DTYPE), jax.ShapeDtypeStruct((PAIRS, nb, 2, t), F32),
                   jax.ShapeDtypeStruct((PAIRS, S, 2), F32)] + [jax.ShapeDtypeStruct(e.shape, e.dtype) for e in early],
        scratch_shapes=[pltpu.VMEM((2, t, 128), kh.dtype), pltpu.VMEM((2, t, 128), v.dtype),
                        pltpu.VMEM((t, 128), F32), pltpu.VMEM((t, 128), F32), pltpu.VMEM((2, t, 1), F32)]
                       + _exchange_scratch(ne),
        compiler_params=_params(("arbitrary", "arbitrary"), VMEM_MID),
    )(qh, do, kh, v, lse4, dl4, *early)


def _s5_mats(a_re, a_im, log_dt, b_re, b_im, c_re, c_im, d_skip):
    Lc = CHUNK
    dt = jnp.exp(log_dt)[:, None]
    lr, li = a_re * dt, a_im * dt

    def apow(n):
        n = jnp.asarray(n, F32)[None, :, None]
        mag = jnp.exp(n * lr[:, None, :])
        ang = n * li[:, None, :]
        return mag * jnp.cos(ang), mag * jnp.sin(ang)

    ar, ai = apow([1.0])
    ar, ai = ar[:, 0], ai[:, 0]
    den = a_re * a_re + a_im * a_im
    nr, ni = ar - 1.0, ai
    fr = (nr * a_re + ni * a_im) / den
    fi = (ni * a_re - nr * a_im) / den
    bbr = fr[:, :, None] * b_re - fi[:, :, None] * b_im
    bbi = fr[:, :, None] * b_im + fi[:, :, None] * b_re
    steps = np.arange(Lc, dtype=np.float32)
    pr, pi = apow(steps)
    car = c_re[:, None] * pr[:, :, None, :] - c_im[:, None] * pi[:, :, None, :]
    cai = c_re[:, None] * pi[:, :, None, :] + c_im[:, None] * pr[:, :, None, :]
    kern = (jnp.einsum('glcp,gpd->glcd', car, bbr, precision=HI)
            - jnp.einsum('glcp,gpd->glcd', cai, bbi, precision=HI))
    skip = d_skip.reshape(GROUPS, CG)[:, :, None] * jnp.eye(CG, dtype=F32)[None]
    kern = kern.at[:, 0].add(skip)
    trow = kern.transpose(0, 3, 1, 2).reshape(GROUPS, CG, Lc * CG)
    p1r, p1i = apow(steps + 1.0)
    cr = c_re[:, None] * p1r[:, :, None, :] - c_im[:, None] * p1i[:, :, None, :]
    ci = c_re[:, None] * p1i[:, :, None, :] + c_im[:, None] * p1r[:, :, None, :]
    to_rows = lambda m: m.transpose(0, 3, 1, 2).reshape(GROUPS, STATE, Lc * CG)
    camat = jnp.concatenate([to_rows(cr), -to_rows(ci)], axis=1)
    qr, qi = apow(Lc - 1.0 - steps)
    zr = qr[:, :, None, :] * bbr.transpose(0, 2, 1)[:, None] - qi[:, :, None, :] * bbi.transpose(0, 2, 1)[:, None]
    zi = qr[:, :, None, :] * bbi.transpose(0, 2, 1)[:, None] + qi[:, :, None, :] * bbr.transpose(0, 2, 1)[:, None]
    bzmat = jnp.concatenate([zr, zi], axis=-1).reshape(GROUPS, Lc * CG, 2 * STATE)
    lr_, li_ = apow([float(Lc)])
    al = jnp.concatenate([lr_[:, 0], li_[:, 0]], axis=-1)
    return trow, camat, bzmat, al


def _s5_scan_powers(a_re, a_im, log_dt, n_steps):
    dt = jnp.exp(log_dt)[:, None]
    lr, li = a_re * dt, a_im * dt
    n = (CHUNK * 2.0 ** np.arange(n_steps)).astype(np.float32)[None, :, None]
    mag = jnp.exp(n * lr[:, None, :])
    pr, pi = mag * jnp.cos(n * li[:, None, :]), mag * jnp.sin(n * li[:, None, :])
    fwd = jnp.stack([jnp.concatenate([pr, pr], -1), jnp.concatenate([-pi, pi], -1)], axis=2)
    bwd = jnp.stack([jnp.concatenate([pr, pr], -1), jnp.concatenate([pi, -pi], -1)], axis=2)
    return fwd, bwd


def _shift_rows(x, sh, down):
    n = x.shape[0]
    ri = lax.broadcasted_iota(jnp.int32, x.shape, 0)
    if down:
        return jnp.where(ri >= sh, pltpu.roll(x, sh, 0), 0.0)
    return jnp.where(ri < n - sh, pltpu.roll(x, n - sh, 0), 0.0)


GPB = 128 // CG


def _lane_transpose(arrs):
    lane = lax.broadcasted_iota(jnp.int32, arrs[0].shape, 1)
    arrs = list(arrs)
    k = GPB // 2
    while k >= 1:
        hi = ((lane // CG) & k) != 0
        new = list(arrs)
        for i in range(GPB):
            if i & k:
                continue
            lo_arr, hi_arr = arrs[i], arrs[i + k]
            new[i] = jnp.where(hi, pltpu.roll(hi_arr, CG * k, 1), lo_arr)
            new[i + k] = jnp.where(hi, hi_arr, pltpu.roll(lo_arr, 128 - CG * k, 1))
        arrs = new
        k //= 2
    return arrs


def _gather_block(ref, dst, nch):
    for half in range(CHUNK // GPB):
        outs = _lane_transpose([ref[pl.ds(half * GPB + l8, nch, stride=CHUNK), :] for l8 in range(GPB)])
        for g in range(GPB):
            dst[half, g] = outs[g]


def _scatter_block(src, ref, nch):
    for half in range(CHUNK // GPB):
        outs = _lane_transpose([src[half, g] for g in range(GPB)])
        for l8 in range(GPB):
            ref[pl.ds(half * GPB + l8, nch, stride=CHUNK), :] = outs[l8]


def _toeplitz(trow):
    lane = lax.broadcasted_iota(jnp.int32, (CG, 128), 1)
    x0, x1 = trow[:, :128], trow[:, 128:]
    zero = jnp.zeros_like(x0)
    rows = []
    for s in range(CHUNK):
        sh = (CG * s) % 128
        r0 = pltpu.roll(x0, sh, 1) if sh else x0
        r1 = pltpu.roll(x1, sh, 1) if sh else x1
        if CG * s < 128:
            rows.append(jnp.concatenate([jnp.where(lane >= sh, r0, zero), jnp.where(lane >= sh, r1, r0)], axis=1))
        else:
            rows.append(jnp.concatenate([zero, jnp.where(lane >= sh, r0, zero)], axis=1))
    return jnp.concatenate(rows, axis=0)


def _toeplitz_adjoint(dt):
    lane = lax.broadcasted_iota(jnp.int32, (CG, 128), 1)
    acc0 = jnp.zeros((CG, 128), F32)
    acc1 = jnp.zeros((CG, 128), F32)
    for s in range(CHUNK):
        x0, x1 = dt[CG * s:CG * s + CG, :128], dt[CG * s:CG * s + CG, 128:]
        sh = (CG * s) % 128
        keep = 128 - sh
        r0 = pltpu.roll(x0, keep, 1) if sh else x0
        r1 = pltpu.roll(x1, keep, 1) if sh else x1
        if CG * s < 128:
            acc0 = acc0 + jnp.where(lane < keep, r0, r1)
            acc1 = acc1 + jnp.where(lane < keep, r1, 0.0)
        else:
            acc0 = acc0 + jnp.where(lane < keep, r1, 0.0)
    return jnp.concatenate([acc0, acc1], axis=1)


def _s5_fwd(u, trow, camat, bzmat, pw):
    S = u.shape[0]
    nch = S // CHUNK
    n_steps = pw.shape[1]

    def body(u_ref, t_ref, ca_ref, bz_ref, pw_ref, y_ref, xp_ref, uc_ref, ub_s, yb_s):
        g = pl.program_id(1)

        @pl.when(g == 0)
        def _():
            _gather_block(u_ref, ub_s, nch)

        uc = jnp.concatenate([ub_s[0, g], ub_s[1, g]], axis=1)
        uc_ref[...] = uc
        x = _s5mm(uc, bz_ref[...])
        for kk in range(n_steps):
            xs = _shift_rows(x, 2 ** kk, True)
            m = pw_ref[kk]
            x = x + m[0:1, :] * xs + m[1:2, :] * pltpu.roll(xs, STATE, 1)
        xp = _shift_rows(x, 1, True)
        xp_ref[...] = xp
        yc = _s5mm(uc, _toeplitz(t_ref[...])) + _s5mm(xp, ca_ref[...])
        yb_s[0, g] = yc[:, :128]
        yb_s[1, g] = yc[:, 128:]

        @pl.when(g == GPB - 1)
        def _():
            _scatter_block(yb_s, y_ref, nch)

    per = lambda a: pl.BlockSpec((None,) + a.shape[1:], lambda b, g: (b * GPB + g,) + (0,) * (a.ndim - 1))
    nat = pl.BlockSpec((S, 128), lambda b, g: (0, b))
    return pl.pallas_call(
        body, name="s5_fwd", grid=(GROUPS // GPB, GPB),
        in_specs=[nat, per(trow), per(camat), per(bzmat), per(pw)],
        out_specs=[nat, pl.BlockSpec((None, nch, 2 * STATE), lambda b, g: (b * GPB + g, 0, 0)),
                   pl.BlockSpec((None, nch, CHUNK * CG), lambda b, g: (b * GPB + g, 0, 0))],
        out_shape=[jax.ShapeDtypeStruct((S, GROUPS * CG), F32), jax.ShapeDtypeStruct((GROUPS, nch, 2 * STATE), F32),
                   jax.ShapeDtypeStruct((GROUPS, nch, CHUNK * CG), F32)],
        scratch_shapes=[pltpu.VMEM((CHUNK // GPB, GPB, nch, 128), F32)] * 2,
        compiler_params=_params(("parallel", "arbitrary"), VMEM_BIG),
    )(u, trow, camat, bzmat, pw)


def _s5_bwd(uc, dy, xp, trow, camat, bzmat, pwc):
    S = dy.shape[0]
    nch = S // CHUNK
    n_steps = pwc.shape[1]

    def body(uc_ref, dy_ref, xp_ref, t_ref, ca_ref, bz_ref, pw_ref, du_ref, dt_ref, dca_ref, dbz_ref, dal_ref,
             dyb_s, dub_s):
        g = pl.program_id(1)

        @pl.when(g == 0)
        def _():
            _gather_block(dy_ref, dyb_s, nch)

        uc = uc_ref[...]
        dyc = jnp.concatenate([dyb_s[0, g], dyb_s[1, g]], axis=1)
        xpv = xp_ref[...]
        dt_ref[...] = _toeplitz_adjoint(_s5mm_tn(uc, dyc))
        dca_ref[...] = _s5mm_tn(xpv, dyc)
        dx = _shift_rows(_s5mm_nt(dyc, ca_ref[...]), 1, False)
        for kk in range(n_steps):
            xs = _shift_rows(dx, 2 ** kk, False)
            m = pw_ref[kk]
            dx = dx + m[0:1, :] * xs + m[1:2, :] * pltpu.roll(xs, STATE, 1)
        dbz_ref[...] = _s5mm_tn(uc, dx)
        dal_ref[0:1, :] = jnp.sum(dx * xpv, axis=0, keepdims=True)
        dal_ref[1:2, :] = jnp.sum(dx * pltpu.roll(xpv, STATE, 1), axis=0, keepdims=True)
        duc = _s5mm_nt(dyc, _toeplitz(t_ref[...])) + _s5mm_nt(dx, bz_ref[...])
        dub_s[0, g] = duc[:, :128]
        dub_s[1, g] = duc[:, 128:]

        @pl.when(g == GPB - 1)
        def _():
            _scatter_block(dub_s, du_ref, nch)

    per = lambda a: pl.BlockSpec((None,) + a.shape[1:], lambda b, g: (b * GPB + g,) + (0,) * (a.ndim - 1))
    nat = pl.BlockSpec((S, 128), lambda b, g: (0, b))
    sds = jax.ShapeDtypeStruct
    mats = [sds(trow.shape, F32), sds(camat.shape, F32), sds(bzmat.shape, F32), sds((GROUPS, 2, 2 * STATE), F32)]
    return pl.pallas_call(
        body, name="s5_bwd", grid=(GROUPS // GPB, GPB),
        in_specs=[per(uc), nat, per(xp), per(trow), per(camat), per(bzmat), per(pwc)],
        out_specs=[nat] + [per(o) for o in mats], out_shape=[sds((S, GROUPS * CG), F32)] + mats,
        scratch_shapes=[pltpu.VMEM((CHUNK // GPB, GPB, nch, 128), F32)] * 2,
        compiler_params=_params(("parallel", "arbitrary"), VMEM_BIG),
    )(uc, dy, xp, trow, camat, bzmat, pwc)


GELU_C0 = math.sqrt(2.0 / math.pi)
GELU_C1 = 0.044715


def _mix(o, za, ys, zb, ga, gb, x, tgt, gate, b_glu, g_final, w_glu, w_up_a, w_up_b, w_out, hsel, ts):
    S = o.shape[0]

    def body(o_ref, za_ref, ys_ref, zb_ref, ga_ref, gb_ref, x_ref, t_ref, gate_ref, bglu_ref, gf_ref,
             wglu_ref, wua_ref, wub_ref, wout_ref, hsel_ref,
             dx2_ref, do_ref, dza_ref, dzb_ref, dga_ref, dgb_ref, dys_ref, dl_ref,
             mg_ref, dmo_ref, ya_ref, dua_ref, yb_ref, dub_ref, yg_ref, dgl_ref,
             dbglu_ref, dgate_ref, dgf_ref, loss_ref):
        @pl.when(pl.program_id(0) == 0)
        def _():
            dbglu_ref[...] = jnp.zeros_like(dbglu_ref)
            dgate_ref[...] = jnp.zeros_like(dgate_ref)
            dgf_ref[...] = jnp.zeros_like(dgf_ref)
            loss_ref[...] = jnp.zeros_like(loss_ref)

        ov = o_ref[...]
        za = za_ref[...]
        sza = _sigmoid(za)
        silu_a = za * sza
        ya = ov * silu_a
        ya_b = ya.astype(ya_ref.dtype)
        ya_ref[...] = ya_b
        ysv = ys_ref[...]
        th = jnp.tanh(GELU_C0 * (ysv + GELU_C1 * ysv * ysv * ysv))
        yg = 0.5 * ysv * (1.0 + th)
        yg_b = yg.astype(yg_ref.dtype)
        yg_ref[...] = yg_b
        sg = _sigmoid(_mm(yg_b, wglu_ref[...]) + bglu_ref[...])
        yb1 = yg * sg
        zb = zb_ref[...]
        szb = _sigmoid(zb)
        silu_b = zb * szb
        yb_b = (yb1 * silu_b).astype(yb_ref.dtype)
        yb_ref[...] = yb_b
        ua = _mm(ya_b, wua_ref[...])
        ub = _mm(yb_b, wub_ref[...])
        sa = _sigmoid(ga_ref[...])
        sb = _sigmoid(gb_ref[...])
        merged_b = (sa * ua + sb * ub).astype(mg_ref.dtype)
        mg_ref[...] = merged_b
        mo = _mm(merged_b, wout_ref[...])
        gate_v = gate_ref[...]
        x2 = x_ref[...] + gate_v * mo
        r2 = lax.rsqrt(jnp.mean(x2 * x2, axis=-1, keepdims=True) + EPS)
        x2n = x2 * r2
        gf = gf_ref[...]
        diff = x2n * gf - t_ref[...]
        loss_ref[...] += jnp.sum(jnp.sum(diff * diff, axis=-1, keepdims=True), axis=0, keepdims=True) * (0.5 / D_MODEL)
        dy = diff * (1.0 / D_MODEL)
        dgf_ref[...] += jnp.sum(dy * x2n, axis=0, keepdims=True)
        dyg = dy * gf
        dx2 = r2 * (dyg - x2n * jnp.mean(dyg * x2n, axis=-1, keepdims=True))
        dx2_ref[...] = dx2
        dgate_ref[...] += jnp.sum(dx2 * mo, axis=0, keepdims=True)
        dmo_b = (dx2 * gate_v).astype(dmo_ref.dtype)
        dmo_ref[...] = dmo_b
        dmerged = _mm_nt(dmo_b, wout_ref[...])
        dua_b = (dmerged * sa).astype(dua_ref.dtype)
        dub_b = (dmerged * sb).astype(dub_ref.dtype)
        dua_ref[...] = dua_b
        dub_ref[...] = dub_b
        dga_ref[...] = (dmerged * ua * sa * (1.0 - sa)).astype(dga_ref.dtype)
        dgb_ref[...] = (dmerged * ub * sb * (1.0 - sb)).astype(dgb_ref.dtype)
        dya = _mm_nt(dua_b, wua_ref[...])
        dyb = _mm_nt(dub_b, wub_ref[...])
        dov = dya * silu_a
        do_ref[...] = dov.astype(do_ref.dtype)
        dl_ref[...] = _mm32_nt(hsel_ref[...], dov * ov)
        dza_ref[...] = (dya * ov * (sza * (1.0 + za * (1.0 - sza)))).astype(dza_ref.dtype)
        dyb1 = dyb * silu_b
        dzb_ref[...] = (dyb * yb1 * (szb * (1.0 + zb * (1.0 - szb)))).astype(dzb_ref.dtype)
        dgl = dyb1 * yg * sg * (1.0 - sg)
        dbglu_ref[...] += jnp.sum(dgl, axis=0, keepdims=True)
        dgl_b = dgl.astype(dgl_ref.dtype)
        dgl_ref[...] = dgl_b
        dyg2 = dyb1 * sg + _mm_nt(dgl_b, wglu_ref[...])
        dgelu = 0.5 * (1.0 + th) + 0.5 * ysv * (1.0 - th * th) * GELU_C0 * (1.0 + 3.0 * GELU_C1 * ysv * ysv)
        dys_ref[...] = dyg2 * dgelu

    row = lambda n: pl.BlockSpec((ts, n), lambda i: (i, 0))
    full = lambda a: pl.BlockSpec(a.shape, lambda i: (0,) * a.ndim)
    vec = lambda n: pl.BlockSpec((1, n), lambda i: (0, 0))
    sds = jax.ShapeDtypeStruct
    W, Dm = WIDTH, D_MODEL
    return pl.pallas_call(
        body, name="mix", grid=(S // ts,),
        in_specs=[row(W), row(W), row(W), row(W), row(Dm), row(Dm), row(Dm), row(Dm),
                  full(gate), full(b_glu), full(g_final), full(w_glu), full(w_up_a), full(w_up_b), full(w_out), full(hsel)],
        out_specs=[row(Dm), row(W), row(W), row(W), row(Dm), row(Dm), row(W), pl.BlockSpec((HEADS, ts), lambda i: (0, i)),
                   row(Dm), row(Dm), row(W), row(Dm), row(W), row(Dm), row(W), row(W),
                   vec(W), vec(Dm), vec(Dm), vec(1)],
        out_shape=[sds((S, Dm), F32), sds((S, W), MXU_DTYPE), sds((S, W), MXU_DTYPE), sds((S, W), MXU_DTYPE),
                   sds((S, Dm), MXU_DTYPE), sds((S, Dm), MXU_DTYPE), sds((S, W), F32), sds((HEADS, S), F32),
                   sds((S, Dm), MXU_DTYPE), sds((S, Dm), MXU_DTYPE), sds((S, W), MXU_DTYPE), sds((S, Dm), MXU_DTYPE),
                   sds((S, W), MXU_DTYPE), sds((S, Dm), MXU_DTYPE), sds((S, W), MXU_DTYPE), sds((S, W), MXU_DTYPE),
                   sds((1, W), F32), sds((1, Dm), F32), sds((1, Dm), F32), sds((1, 1), F32)],
        compiler_params=_params(("arbitrary",), VMEM_BIG),
    )(o, za, ys, zb, ga, gb, x, tgt, gate, b_glu, g_final, w_glu, w_up_a, w_up_b, w_out, hsel)


def _matmul_tn(name, a, b, ts):
    S, M = a.shape
    N = b.shape[1]
    tn = min(N, 1024)

    def body(a_ref, b_ref, o_ref):
        @pl.when(pl.program_id(1) == 0)
        def _():
            o_ref[...] = jnp.zeros_like(o_ref)

        o_ref[...] += _mm_tn(a_ref[...], b_ref[...])

    return pl.pallas_call(
        body, name=name, grid=(N // tn, S // ts),
        in_specs=[pl.BlockSpec((ts, M), lambda j, i: (i, 0)), pl.BlockSpec((ts, tn), lambda j, i: (i, j))],
        out_specs=pl.BlockSpec((M, tn), lambda j, i: (0, j)),
        out_shape=jax.ShapeDtypeStruct((M, N), F32),
        compiler_params=_params(("parallel", "arbitrary"), VMEM_MID),
    )(a, b)


def _proj_bwd(dq, dk, dv, dza, du, dzb, dga, dgb, dfl, x, dx2, shift, scale, g_norm, w_main, w_ft, chip_planes, ts):
    S = x.shape[0]
    nc = len(chip_planes)

    def body(dq_ref, dk_ref, dv_ref, dza_ref, du_ref, dzb_ref, dga_ref, dgb_ref, dfl_ref, x_ref, dx2_ref,
             sc_ref, gn_ref, w_ref, wft_ref, *rest):
        gx_ref, dsh_ref, dsc_ref, dgn_ref = rest[nc:nc + 4]
        i = pl.program_id(0)
        _hosted_chip_exchange(i == 0, i == pl.num_programs(0) - 1, rest[:nc], rest[nc + 4:2 * nc + 4],
                              *rest[2 * nc + 4:])

        @pl.when(pl.program_id(0) == 0)
        def _():
            dsh_ref[...] = jnp.zeros_like(dsh_ref)
            dsc_ref[...] = jnp.zeros_like(dsc_ref)
            dgn_ref[...] = jnp.zeros_like(dgn_ref)

        def seg(ref, off, n):
            return _mm_nt(ref[...], w_ref[:, off:off + n])

        dh = (seg(dq_ref, M_Q, WIDTH) + seg(dk_ref, M_K, WIDTH) + seg(dv_ref, M_V, WIDTH)
              + seg(dza_ref, M_ZA, WIDTH) + seg(du_ref, M_U, WIDTH) + seg(dzb_ref, M_ZB, WIDTH)
              + seg(dga_ref, M_GA, D_MODEL) + seg(dgb_ref, M_GB, D_MODEL)
              + _mm32(dfl_ref[...], wft_ref[...]))
        xv = x_ref[...]
        r = lax.rsqrt(jnp.mean(xv * xv, axis=-1, keepdims=True) + EPS)
        xn = xv * r
        gn = gn_ref[...]
        s1 = 1.0 + sc_ref[...]
        dsh_ref[...] += jnp.sum(dh, axis=0, keepdims=True)
        dhx = dh * xn
        dsc_ref[...] += jnp.sum(dhx, axis=0, keepdims=True) * gn
        dgn_ref[...] += jnp.sum(dhx, axis=0, keepdims=True) * s1
        dxn = dh * (gn * s1)
        gx_ref[...] = dx2_ref[...] + r * (dxn - xn * jnp.mean(dxn * xn, axis=-1, keepdims=True))

    row = lambda n: pl.BlockSpec((ts, n), lambda i: (i, 0))
    full = lambda a: pl.BlockSpec(a.shape, lambda i: (0,) * a.ndim)
    vec = pl.BlockSpec((1, D_MODEL), lambda i: (0, 0))
    W, Dm = WIDTH, D_MODEL
    del shift
    return pl.pallas_call(
        body, name="proj_bwd", grid=(S // ts,),
        in_specs=[row(W)] * 6 + [row(Dm)] * 2 + [row(HEADS), row(Dm), row(Dm),
                                                 full(scale), full(g_norm), full(w_main), full(w_ft)] + [ANY] * nc,
        out_specs=[row(Dm), vec, vec, vec] + [ANY] * nc,
        out_shape=[jax.ShapeDtypeStruct((S, Dm), F32)] + [jax.ShapeDtypeStruct((1, Dm), F32)] * 3
                  + [jax.ShapeDtypeStruct(p.shape, p.dtype) for p in chip_planes],
        scratch_shapes=[pltpu.SemaphoreType.DMA((nc, N_CHIP)), pltpu.SemaphoreType.DMA((nc, N_CHIP)),
                        pltpu.SemaphoreType.DMA((nc,))],
        compiler_params=_params(("arbitrary",), VMEM_BIG),
    )(dq, dk, dv, dza, du, dzb, dga, dgb, dfl, x, dx2, scale, g_norm, w_main, w_ft, *chip_planes)


def _adamw(name, planes, w, m, v, tr):
    n, R, C = planes.shape
    bc1 = 1.0 - ADAM_B1 ** ADAM_STEP
    bc2 = 1.0 - ADAM_B2 ** ADAM_STEP

    def body(p_ref, w_ref, m_ref, v_ref, g_ref, d_ref, nm_ref, nv_ref):
        g = p_ref[0].astype(F32)
        for i in range(1, n):
            g = g + p_ref[i].astype(F32)
        g_ref[...] = g
        nm = ADAM_B1 * m_ref[...] + (1.0 - ADAM_B1) * g
        nv = ADAM_B2 * v_ref[...] + (1.0 - ADAM_B2) * (g * g)
        nm_ref[...] = nm
        nv_ref[...] = nv
        d_ref[...] = -ADAM_LR * ((nm / bc1) / (jnp.sqrt(nv / bc2) + ADAM_EPS) + ADAM_WD * w_ref[...])

    blk = pl.BlockSpec((tr, C), lambda i: (i, 0))
    return pl.pallas_call(
        body, name=name, grid=(R // tr,),
        in_specs=[pl.BlockSpec((n, tr, C), lambda i: (0, i, 0)), blk, blk, blk],
        out_specs=[blk] * 4, out_shape=[jax.ShapeDtypeStruct((R, C), F32)] * 4,
        compiler_params=_params(("parallel",), VMEM_MID),
    )(planes, w, m, v)


def _wada_grad(c_all, dmod_cols):
    def body(c_ref, d_ref, o_ref):
        o_ref[0] = _mm32_tn(c_ref[...], d_ref[...])

    return pl.pallas_call(
        body, name="wada_grad",
        out_shape=jax.ShapeDtypeStruct((1, c_all.shape[1], dmod_cols.shape[1]), F32),
        in_specs=[VMEM, VMEM], out_specs=VMEM,
    )(c_all, dmod_cols)


SMALL_ORDER = ("b_ada", "g_norm", "b_f", "a_re", "a_im", "log_dt", "b_re", "b_im", "c_re", "c_im",
               "d_skip", "b_glu", "g_final")
BIG_ORDER = ("w_ada", "w_in", "w_glu", "w_up_a", "w_up_b", "w_out")
ALL_ORDER = ("w_ada", "b_ada", "g_norm", "w_in", "b_f", "a_re", "a_im", "log_dt", "b_re", "b_im", "c_re", "c_im",
             "d_skip", "w_glu", "b_glu", "w_up_a", "w_up_b", "w_out", "g_final")


def _pack_small(parts, rows):
    flat = jnp.concatenate([p.reshape(-1).astype(F32) for p in parts])
    return jnp.pad(flat, (0, rows * 128 - flat.shape[0])).reshape(rows, 128)


def kernel(x, c, w_ada, b_ada, g_norm, w_in, b_f, a_re, a_im, log_dt, b_re, b_im, c_re, c_im, d_skip, w_glu, b_glu, w_up_a, w_up_b, w_out, g_final, loss_target, m_w_ada, m_b_ada, m_g_norm, m_w_in, m_b_f, m_a_re, m_a_im, m_log_dt, m_b_re, m_b_im, m_c_re, m_c_im, m_d_skip, m_w_glu, m_b_glu, m_w_up_a, m_w_up_b, m_w_out, m_g_final, v_w_ada, v_b_ada, v_g_norm, v_w_in, v_b_f, v_a_re, v_a_im, v_log_dt, v_b_re, v_b_im, v_c_re, v_c_im, v_d_skip, v_w_glu, v_b_glu, v_w_up_a, v_w_up_b, v_w_out, v_g_final):
    weights = dict(w_ada=w_ada, b_ada=b_ada, g_norm=g_norm, w_in=w_in, b_f=b_f, a_re=a_re, a_im=a_im, log_dt=log_dt,
                   b_re=b_re, b_im=b_im, c_re=c_re, c_im=c_im, d_skip=d_skip, w_glu=w_glu, b_glu=b_glu,
                   w_up_a=w_up_a, w_up_b=w_up_b, w_out=w_out, g_final=g_final)
    mom_m = dict(w_ada=m_w_ada, b_ada=m_b_ada, g_norm=m_g_norm, w_in=m_w_in, b_f=m_b_f, a_re=m_a_re, a_im=m_a_im,
                 log_dt=m_log_dt, b_re=m_b_re, b_im=m_b_im, c_re=m_c_re, c_im=m_c_im, d_skip=m_d_skip, w_glu=m_w_glu,
                 b_glu=m_b_glu, w_up_a=m_w_up_a, w_up_b=m_w_up_b, w_out=m_w_out, g_final=m_g_final)
    mom_v = dict(w_ada=v_w_ada, b_ada=v_b_ada, g_norm=v_g_norm, w_in=v_w_in, b_f=v_b_f, a_re=v_a_re, a_im=v_a_im,
                 log_dt=v_log_dt, b_re=v_b_re, b_im=v_b_im, c_re=v_c_re, c_im=v_c_im, d_skip=v_d_skip, w_glu=v_w_glu,
                 b_glu=v_b_glu, w_up_a=v_w_up_a, w_up_b=v_w_up_b, w_out=v_w_out, g_final=v_g_final)
    xs = x[0]
    tgt = loss_target[0]
    S = xs.shape[0]
    ts = min(256, S)
    ta = min(512, S)
    tw = min(2048, S)
    nch = S // CHUNK
    n_steps = max(1, int(math.ceil(math.log2(nch))))
    me = _my_index()

    shards = [w.astype(MXU_DTYPE) for w in (w_in[0], w_glu[0], w_up_a[0], w_up_b[0], w_out[0])]
    mod8, c_all, gathered = _comm_in(c, w_ada[0], b_ada.reshape(N_DEV, -1), shards[:1])
    mod = mod8.reshape(1, 3 * D_MODEL)
    shift, scale, gate = mod[:, :D_MODEL], mod[:, D_MODEL:2 * D_MODEL], mod[:, 2 * D_MODEL:]
    w_in_full = gathered[0].transpose(1, 0, 2).reshape(D_MODEL, PROJ_WIDTH)
    w_main = jnp.concatenate([w_in_full[:, :OFF_F], w_in_full[:, OFF_F + HEADS:]], axis=1)
    w_f = w_in_full[:, OFF_F:OFF_F + HEADS].astype(F32)
    w_ft = w_f.T

    q, k, v, za, u, zb, ga, gb, flc, hb, *late = _proj_fwd(xs, shift, scale, g_norm, w_main, w_f, shards[1:], ts)
    w_glu_full = late[0].reshape(WIDTH, WIDTH)
    w_up_a_full = late[1].transpose(1, 0, 2).reshape(WIDTH, D_MODEL)
    w_up_b_full = late[2].transpose(1, 0, 2).reshape(WIDTH, D_MODEL)
    w_out_full = late[3].reshape(D_MODEL, D_MODEL)
    nb = S // ta
    rows4 = lambda r: r.reshape(PAIRS, 2, nb, ta).transpose(0, 2, 1, 3)
    qh, kh, vt = _attn_prep(q, k, v, flc, b_f, ta)
    o, lse4 = _attn_fwd(qh, kh, vt, ta)

    s5_params = (a_re[0], a_im[0], log_dt[0], b_re[0], b_im[0], c_re[0], c_im[0], d_skip[0])
    (trow, camat, bzmat, al), mats_vjp = jax.vjp(_s5_mats, *s5_params)
    del al
    pw_f, pw_b = _s5_scan_powers(a_re[0], a_im[0], log_dt[0], n_steps)
    ys, xprev, uc = _s5_fwd(u, trow, camat, bzmat, pw_f)

    hsel = (np.arange(WIDTH)[None, :] // 64 == np.arange(HEADS)[:, None]).astype(np.float32)
    (dx2, do, dza, dzb, dga, dgb, dys, dl_row, merged, dmo, ya, dua, yb, dub, yg, dgl,
     db_glu, dgate, dg_final, loss_part) = _mix(o, za, ys, zb, ga, gb, xs, tgt, gate, b_glu, g_final.reshape(1, -1),
                                                w_glu_full, w_up_a_full, w_up_b_full, w_out_full, jnp.asarray(hsel), ts)

    gw_out = _matmul_tn("dw_out", merged, dmo, tw)
    gw_up_a = _matmul_tn("dw_up_a", ya, dua, tw)
    gw_up_b = _matmul_tn("dw_up_b", yb, dub, tw)
    gw_glu = _matmul_tn("dw_glu", yg, dgl, tw)

    du, d_trow, d_camat, d_bzmat, dal2 = _s5_bwd(uc, dys, xprev, trow, camat, bzmat, pw_b)
    d_al = jnp.concatenate([dal2[:, 0, :STATE] + dal2[:, 0, STATE:], dal2[:, 1, STATE:] - dal2[:, 1, :STATE]], axis=-1)
    gs5 = mats_vjp((d_trow, d_camat, d_bzmat, d_al))

    dl4 = rows4(dl_row)
    early = [p.astype(MXU_DTYPE) for p in (gw_glu.reshape(N_DEV, -1, WIDTH),
                                           gw_up_a.reshape(WIDTH, N_DEV, -1).transpose(1, 0, 2),
                                           gw_up_b.reshape(WIDTH, N_DEV, -1).transpose(1, 0, 2),
                                           gw_out.reshape(N_DEV, -1, D_MODEL))]
    dq, dk, dv, dfq4, dfk4, *early_recv = _attn_bwd(qh, do, kh, v, lse4, dl4, early, ta)
    d_fcol = dfq4.transpose(0, 2, 1, 3).reshape(HEADS, S).T - dfk4.transpose(1, 0, 2).reshape(S, HEADS)
    dfl, db_f = _fgate_bwd(d_fcol, flc, b_f, ta)

    segs = [("dw_q", dq), ("dw_k", dk), ("dw_v", dv), ("dw_f", dfl), ("dw_za", dza), ("dw_u", du), ("dw_zb", dzb),
            ("dw_ga", dga), ("dw_gb", dgb)]
    gw_in = jnp.concatenate([_matmul_tn(nm, hb, d, tw) for nm, d in segs], axis=1)

    planes = [gw_in.reshape(D_MODEL, N_DEV, -1).transpose(1, 0, 2).astype(MXU_DTYPE)]
    from_sib = _comm_pair(planes)
    core = lax.axis_index("c")
    chip_planes = []
    for name, p, s in zip(("w_in",), planes, from_sib):
        tr = 256 if s.shape[1] % 256 == 0 else s.shape[1]
        chip_planes.append(_pair_sum("pair_sum_" + name, p, s, core, tr))
    grad_x, dshift, dscale, dg_norm, *recv = _proj_bwd(dq, dk, dv, dza, du, dzb, dga, dgb, dfl, xs, dx2,
                                                       shift, scale, g_norm, w_main, w_ft, chip_planes, ts)

    dmod = jnp.concatenate([dshift, dscale, dgate], axis=1)
    small_parts = [dmod, dg_norm, db_f, gs5[0], gs5[1], gs5[2], gs5[3], gs5[4], gs5[5], gs5[6], gs5[7],
                   db_glu, dg_final, loss_part]
    n_small = sum(int(np.prod(p.shape)) for p in small_parts)
    rows = -(-n_small // (8 * 128)) * 8
    small_all = _comm_small(_pack_small(small_parts, rows))

    grads, deltas, new_m, new_v = {}, {}, {}, {}

    def put(name, res, shape):
        grads[name], deltas[name], new_m[name], new_v[name] = [r.reshape(shape) for r in res]

    names = ("w_in", "w_glu", "w_up_a", "w_up_b", "w_out")
    for name, pr in zip(names, recv + early_recv):
        w2 = weights[name][0]
        tr = 256 if w2.shape[0] % 256 == 0 else w2.shape[0]
        put(name, _adamw("adamw_" + name, pr, w2, mom_m[name][0], mom_v[name][0], tr), weights[name].shape)
    cols = w_ada.shape[2]
    dmod_all = small_all[:, :24, :].reshape(N_DEV, 3 * D_MODEL)
    dmod_cols = lax.dynamic_slice_in_dim(dmod_all, me * cols, cols, axis=1)
    g_wada = _wada_grad(c_all, dmod_cols)
    put("w_ada", _adamw("adamw_w_ada", g_wada, w_ada[0], m_w_ada[0], v_w_ada[0], 256), w_ada.shape)
    pack = lambda d: _pack_small([d[n] for n in SMALL_ORDER] + [jnp.zeros((1,), F32)], rows)
    res_small = _adamw("adamw_small", small_all, pack(weights), pack(mom_m), pack(mom_v), rows)
    flat = [r.reshape(-1) for r in res_small]
    off = 0
    for name in SMALL_ORDER:
        shape = weights[name].shape
        size = int(np.prod(shape))
        put(name, [f[off:off + size] for f in flat], shape)
        off += size
    loss = flat[0][off]

    return (loss, grad_x[None], *[grads[n] for n in ALL_ORDER], *[deltas[n] for n in ALL_ORDER],
            *[new_m[n] for n in ALL_ORDER], *[new_v[n] for n in ALL_ORDER])
```

```python
import math

import jax
import jax.numpy as jnp
import numpy as np
from jax import lax
from jax.experimental import pallas as pl
from jax.experimental.pallas import tpu as pltpu

F32 = jnp.float32
MXU_DTYPE = jnp.bfloat16
HI = lax.Precision.HIGHEST

N_DEV = 8
D_MODEL = 1024
WIDTH = 512
HEADS = 8
PAIRS = HEADS // 2
GROUPS = 32
STATE = 64
CG = 16
CHUNK = 16
EPS = 1e-6
NEG = float(np.finfo(np.float32).min)

ADAM_LR = 0.001
ADAM_B1 = 0.9
ADAM_B2 = 0.999
ADAM_EPS = 1e-08
ADAM_WD = 0.01
ADAM_STEP = 10

VMEM_BIG = 56 * 1024 * 1024
VMEM_MID = 40 * 1024 * 1024

OFF_F = 3 * WIDTH
PROJ_WIDTH = 5128
M_Q, M_K, M_V, M_ZA, M_U, M_ZB, M_GA, M_GB = 0, 512, 1024, 1536, 2048, 2560, 3072, 4096


def _mm(a, b):
    return jnp.dot(a.astype(MXU_DTYPE), b.astype(MXU_DTYPE), preferred_element_type=F32)


def _mm_nt(a, b):
    return lax.dot_general(a.astype(MXU_DTYPE), b.astype(MXU_DTYPE), (((1,), (1,)), ((), ())),
                           preferred_element_type=F32)


def _mm_tn(a, b):
    return lax.dot_general(a.astype(MXU_DTYPE), b.astype(MXU_DTYPE), (((0,), (0,)), ((), ())),
                           preferred_element_type=F32)


def _mm32(a, b):
    return jnp.dot(a, b, precision=HI, preferred_element_type=F32)


def _mm32_nt(a, b):
    return lax.dot_general(a, b, (((1,), (1,)), ((), ())), precision=HI, preferred_element_type=F32)


def _mm32_tn(a, b):
    return lax.dot_general(a, b, (((0,), (0,)), ((), ())), precision=HI, preferred_element_type=F32)


S5_PRECISION = lax.Precision.HIGH


def _s5mm(a, b):
    return jnp.dot(a, b, precision=S5_PRECISION, preferred_element_type=F32)


def _s5mm_nt(a, b):
    return lax.dot_general(a, b, (((1,), (1,)), ((), ())), precision=S5_PRECISION, preferred_element_type=F32)


def _s5mm_tn(a, b):
    return lax.dot_general(a, b, (((0,), (0,)), ((), ())), precision=S5_PRECISION, preferred_element_type=F32)


def _sigmoid(x):
    return 1.0 / (1.0 + jnp.exp(-x))


def _params(sem=None, vmem=None):
    kw = {}
    if sem is not None:
        kw["dimension_semantics"] = sem
    if vmem is not None:
        kw["vmem_limit_bytes"] = vmem
    return pltpu.CompilerParams(**kw)


def _my_index():
    return 4 * lax.axis_index("x") + 2 * lax.axis_index("y") + lax.axis_index("c")


def _dev(p):
    return (p // 4, (p // 2) % 2, p % 2)


ANY = pl.BlockSpec(memory_space=pl.ANY)
VMEM = pl.BlockSpec(memory_space=pltpu.VMEM)
MESH = pl.DeviceIdType.MESH


def _comm_in(c, w_ada, b_ada8, shards):
    n = len(shards)
    cols = w_ada.shape[1]

    def body(c_ref, wada_ref, bada_ref, *rest):
        srcs = rest[:n]
        mod_ref, call_ref = rest[n], rest[n + 1]
        dsts = rest[n + 2:2 * n + 2]
        modp, wsend, wrecv, wloc, csend, crecv, msend, mrecv = rest[2 * n + 2:]
        me = _my_index()

        x, y, cc = lax.axis_index("x"), lax.axis_index("y"), lax.axis_index("c")
        here, sib = (x, y, cc), (x, y, 1 - cc)
        xn, yn, dg = (1 - x, y, cc), (x, 1 - y, cc), (1 - x, 1 - y, cc)
        half = srcs[0].shape[0] // 2
        parts = [(0, pl.ds(0, half)), (0, pl.ds(half, half))] + [(a, None) for a in range(1, n)]
        via_y = [i % 2 == 0 for i in range(len(parts))]

        def wcopy(i, k, block, to, own=False):
            a, rs = parts[i]
            dst = dsts[a].at[4 * block[0] + 2 * block[1] + block[2]]
            src = srcs[a] if own else dst
            if rs is not None:
                src, dst = src.at[rs], dst.at[rs]
            return pltpu.make_async_remote_copy(src_ref=src, dst_ref=dst,
                                                send_sem=wsend.at[i, k], recv_sem=wrecv.at[i, k],
                                                device_id=to, device_id_type=MESH)

        def ccopy(src_dev, d, to):
            return pltpu.make_async_remote_copy(src_ref=c_ref, dst_ref=call_ref.at[pl.ds(src_dev, 1)],
                                                send_sem=csend.at[d], recv_sem=crecv.at[src_dev],
                                                device_id=_dev(to), device_id_type=MESH)

        def mcopy(src_dev, d, to):
            return pltpu.make_async_remote_copy(src_ref=modp.at[pl.ds(to, 1)], dst_ref=mod_ref.at[pl.ds(src_dev, 1)],
                                                send_sem=msend.at[d], recv_sem=mrecv.at[src_dev],
                                                device_id=_dev(to), device_id_type=MESH)

        local = [pltpu.make_async_copy(srcs[a], dsts[a].at[me], wloc.at[a]) for a in range(n)]
        for cp in local:
            cp.start()
        peers = [(me + d) % N_DEV for d in range(1, N_DEV)]
        sends = []
        for i in range(len(parts)):
            sends += [wcopy(i, 0, here, sib, own=True), wcopy(i, 1, here, xn, own=True), wcopy(i, 2, here, yn, own=True)]
        for cp in sends:
            cp.start()
        call_ref[pl.ds(me, 1), :] = c_ref[...]
        for d, p in enumerate(peers):
            ccopy(me, d, p).start()
        for d, p in enumerate(peers):
            ccopy(p, d, p).wait_recv()
        modp[...] = _mm32(call_ref[...], wada_ref[...]) + bada_ref[pl.ds(me, 1), :]
        mod_ref[pl.ds(me, 1), :] = modp[pl.ds(me, 1), :]
        for d, p in enumerate(peers):
            mcopy(me, d, p).start()
        for d, p in enumerate(peers):
            mcopy(p, d, p).wait_recv()
        def after(i, k, block, nxt):
            wcopy(i, k, block, here).wait_recv()
            for kk, to in nxt:
                cp = wcopy(i, kk, block, to)
                cp.start()
                sends.append(cp)

        for i in range(len(parts)):
            after(i, 1, xn, [(3, sib)] + ([(5, yn)] if via_y[i] else []))
        for i in range(len(parts)):
            after(i, 2, yn, [(4, sib)] + ([] if via_y[i] else [(6, xn)]))
        for i in range(len(parts)):
            after(i, 5 if via_y[i] else 6, dg, [(7, sib)])
        for i in range(len(parts)):
            wcopy(i, 0, sib, here).wait_recv()
            for k, block in ((3, xn), (4, yn), (7, dg)):
                wcopy(i, k, (block[0], block[1], 1 - cc), here).wait_recv()
        for cp in sends:
            cp.wait_send()
        for d, p in enumerate(peers):
            ccopy(me, d, p).wait_send()
            mcopy(me, d, p).wait_send()
        for cp in local:
            cp.wait()

    out_shape = ([jax.ShapeDtypeStruct((N_DEV, cols), F32), jax.ShapeDtypeStruct((N_DEV, D_MODEL), F32)]
                 + [jax.ShapeDtypeStruct((N_DEV,) + s.shape, s.dtype) for s in shards])
    res = pl.pallas_call(
        body, name="comm_in", out_shape=out_shape,
        in_specs=[VMEM, VMEM, VMEM] + [ANY] * n,
        out_specs=[VMEM, VMEM] + [ANY] * n,
        scratch_shapes=[pltpu.VMEM((N_DEV, cols), F32),
                        pltpu.SemaphoreType.DMA((n + 1, N_DEV)), pltpu.SemaphoreType.DMA((n + 1, N_DEV)),
                        pltpu.SemaphoreType.DMA((n,)),
                        pltpu.SemaphoreType.DMA((N_DEV,)), pltpu.SemaphoreType.DMA((N_DEV,)),
                        pltpu.SemaphoreType.DMA((N_DEV,)), pltpu.SemaphoreType.DMA((N_DEV,))],
        compiler_params=_params(vmem=VMEM_MID),
    )(c, w_ada, b_ada8, *shards)
    return res[0], res[1], list(res[2:])


def _direct_copy(srcs, dsts, send, recv, scatter, a, d, receiving):
    me = _my_index()
    p = (me + d) % N_DEV
    slot = p if receiving else me
    return pltpu.make_async_remote_copy(src_ref=srcs[a].at[p] if scatter else srcs[a], dst_ref=dsts[a].at[slot],
                                        send_sem=send.at[a, d], recv_sem=recv.at[a, slot],
                                        device_id=_dev(p), device_id_type=MESH)


def _hosted_exchange(first, last, srcs, dsts, send, recv, loc, scatter):
    me = _my_index()
    n = len(srcs)
    local = lambda a: pltpu.make_async_copy(srcs[a].at[me] if scatter else srcs[a], dsts[a].at[me], loc.at[a])
    pairs = [(a, d) for d in range(1, N_DEV) for a in range(n)]

    @pl.when(first)
    def _():
        for a in range(n):
            local(a).start()
        for a, d in pairs:
            _direct_copy(srcs, dsts, send, recv, scatter, a, d, False).start()

    @pl.when(last)
    def _():
        for a, d in pairs:
            _direct_copy(srcs, dsts, send, recv, scatter, a, d, True).wait_recv()
            _direct_copy(srcs, dsts, send, recv, scatter, a, d, False).wait_send()
        for a in range(n):
            local(a).wait()


def _exchange_scratch(n):
    return [pltpu.SemaphoreType.DMA((n, N_DEV)), pltpu.SemaphoreType.DMA((n, N_DEV)), pltpu.SemaphoreType.DMA((n,))]


def _hosted_chip_exchange(first, last, srcs, dsts, send, recv, loc):
    x, y, cc = lax.axis_index("x"), lax.axis_index("y"), lax.axis_index("c")
    mine = 2 * x + y
    chips = [(1 - x, y), (x, 1 - y), (1 - x, 1 - y)]
    n = len(srcs)

    def copy(a, j, sending):
        chip = chips[j]
        there = 2 * chip[0] + chip[1]
        return pltpu.make_async_remote_copy(src_ref=srcs[a].at[there], dst_ref=dsts[a].at[mine if sending else there],
                                            send_sem=send.at[a, j], recv_sem=recv.at[a, j],
                                            device_id=(*chip, cc), device_id_type=MESH)

    local = lambda a: pltpu.make_async_copy(srcs[a].at[mine], dsts[a].at[mine], loc.at[a])

    @pl.when(first)
    def _():
        for a in range(n):
            local(a).start()
            for j in range(len(chips)):
                copy(a, j, True).start()

    @pl.when(last)
    def _():
        for a in range(n):
            for j in range(len(chips)):
                copy(a, j, False).wait_recv()
                copy(a, j, True).wait_send()
            local(a).wait()


N_CHIP = 4


def _comm_pair(planes):
    n = len(planes)

    def body(*rest):
        srcs, dsts = rest[:n], rest[n:2 * n]
        send, recv = rest[2 * n:]
        x, y, cc = lax.axis_index("x"), lax.axis_index("y"), lax.axis_index("c")
        copies = [pltpu.make_async_remote_copy(src_ref=srcs[a].at[2 * ch + 1 - cc], dst_ref=dsts[a].at[ch],
                                               send_sem=send.at[a, ch], recv_sem=recv.at[a, ch],
                                               device_id=(x, y, 1 - cc), device_id_type=MESH)
                  for a in range(n) for ch in range(N_CHIP)]
        for cp in copies:
            cp.start()
        for cp in copies:
            cp.wait()

    out_shape = [jax.ShapeDtypeStruct((N_CHIP,) + p.shape[1:], p.dtype) for p in planes]
    return pl.pallas_call(
        body, name="comm_pair", out_shape=out_shape, in_specs=[ANY] * n, out_specs=[ANY] * n,
        scratch_shapes=[pltpu.SemaphoreType.DMA((n, N_CHIP)), pltpu.SemaphoreType.DMA((n, N_CHIP))],
    )(*planes)


def _pair_sum(name, planes, from_sib, core, tr):
    _, R, C = from_sib.shape

    def body(core_ref, a_ref, b_ref, o_ref):
        del core_ref
        o_ref[...] = (a_ref[...].astype(F32) + b_ref[...].astype(F32)).astype(o_ref.dtype)

    blk = pl.BlockSpec((None, tr, C), lambda i, j, c: (i, j, 0))
    grid_spec = pltpu.PrefetchScalarGridSpec(
        num_scalar_prefetch=1, grid=(N_CHIP, R // tr),
        in_specs=[pl.BlockSpec((None, tr, C), lambda i, j, c: (2 * i + c[0], j, 0)), blk], out_specs=blk)
    return pl.pallas_call(
        body, name=name, grid_spec=grid_spec, out_shape=jax.ShapeDtypeStruct(from_sib.shape, from_sib.dtype),
        compiler_params=_params(("parallel", "parallel"), VMEM_MID),
    )(core.reshape(1).astype(jnp.int32), planes, from_sib)


def _comm_small(small):
    rows = small.shape[0]
    cut = (rows // 16) * 8

    def body(small_ref, sall_ref, ssend, srecv, sloc):
        me = _my_index()
        x, y, cc = lax.axis_index("x"), lax.axis_index("y"), lax.axis_index("c")
        here, sib = (x, y, cc), (x, y, 1 - cc)
        xn, yn, dg = (1 - x, y, cc), (x, 1 - y, cc), (1 - x, 1 - y, cc)
        sparts = [pl.ds(0, cut), pl.ds(cut, rows - cut)]
        via_y = [True, False]

        def scopy(i, k, block, to, own=False):
            dst = sall_ref.at[4 * block[0] + 2 * block[1] + block[2]].at[sparts[i]]
            src = small_ref.at[sparts[i]] if own else dst
            return pltpu.make_async_remote_copy(src_ref=src, dst_ref=dst, send_sem=ssend.at[i, k], recv_sem=srecv.at[i, k],
                                                device_id=to, device_id_type=MESH)

        local = pltpu.make_async_copy(small_ref, sall_ref.at[me], sloc)
        local.start()
        sends = []
        for i in range(len(sparts)):
            sends += [scopy(i, 0, here, sib, own=True), scopy(i, 1, here, xn, own=True), scopy(i, 2, here, yn, own=True)]
        for cp in sends:
            cp.start()

        def after(i, k, block, nxt):
            scopy(i, k, block, here).wait_recv()
            for kk, to in nxt:
                cp = scopy(i, kk, block, to)
                cp.start()
                sends.append(cp)

        for i in range(len(sparts)):
            after(i, 1, xn, [(3, sib)] + ([(5, yn)] if via_y[i] else []))
        for i in range(len(sparts)):
            after(i, 2, yn, [(4, sib)] + ([] if via_y[i] else [(6, xn)]))
        for i in range(len(sparts)):
            after(i, 5 if via_y[i] else 6, dg, [(7, sib)])
        for i in range(len(sparts)):
            scopy(i, 0, sib, here).wait_recv()
            for k, block in ((3, xn), (4, yn), (7, dg)):
                scopy(i, k, (block[0], block[1], 1 - cc), here).wait_recv()
        for cp in sends:
            cp.wait_send()
        local.wait()

    return pl.pallas_call(
        body, name="comm_small", out_shape=jax.ShapeDtypeStruct((N_DEV,) + small.shape, small.dtype),
        in_specs=[ANY], out_specs=ANY,
        scratch_shapes=[pltpu.SemaphoreType.DMA((2, N_DEV)), pltpu.SemaphoreType.DMA((2, N_DEV)),
                        pltpu.SemaphoreType.DMA(())],
    )(small)


def _proj_fwd(x, shift, scale, g_norm, w_main, w_f, late, ts):
    S = x.shape[0]
    nl = len(late)

    def body(x_ref, sh_ref, sc_ref, gn_ref, w_ref, wf_ref, *rest):
        (q_ref, k_ref, v_ref, za_ref, u_ref, zb_ref, ga_ref, gb_ref, flc_ref, h_ref) = rest[nl:nl + 10]
        i = pl.program_id(0)
        _hosted_exchange(i == 0, i == pl.num_programs(0) - 1, rest[:nl], rest[nl + 10:2 * nl + 10],
                         *rest[2 * nl + 10:], scatter=False)
        xv = x_ref[...]
        r = lax.rsqrt(jnp.mean(xv * xv, axis=-1, keepdims=True) + EPS)
        h = (xv * r) * gn_ref[...] * (1.0 + sc_ref[...]) + sh_ref[...]
        hb = h.astype(MXU_DTYPE)
        h_ref[...] = hb

        def seg(off, n):
            return jnp.dot(hb, w_ref[:, off:off + n], preferred_element_type=F32)

        q_ref[...] = (seg(M_Q, WIDTH) * 0.125).astype(q_ref.dtype)
        k_ref[...] = seg(M_K, WIDTH).astype(k_ref.dtype)
        v_ref[...] = seg(M_V, WIDTH).astype(v_ref.dtype)
        za_ref[...] = seg(M_ZA, WIDTH)
        u_ref[...] = seg(M_U, WIDTH)
        zb_ref[...] = seg(M_ZB, WIDTH)
        ga_ref[...] = seg(M_GA, D_MODEL)
        gb_ref[...] = seg(M_GB, D_MODEL)
        flc_ref[...] = _mm32(h, wf_ref[...])

    row = lambda n: pl.BlockSpec((ts, n), lambda i: (i, 0))
    full = lambda a: pl.BlockSpec(a.shape, lambda i: (0,) * a.ndim)
    sds = jax.ShapeDtypeStruct
    return pl.pallas_call(
        body, name="proj_fwd", grid=(S // ts,),
        in_specs=[row(D_MODEL), full(shift), full(scale), full(g_norm), full(w_main), full(w_f)] + [ANY] * nl,
        out_specs=[row(WIDTH)] * 6 + [row(D_MODEL)] * 2 + [row(HEADS), row(D_MODEL)] + [ANY] * nl,
        out_shape=[sds((S, WIDTH), MXU_DTYPE)] * 3 + [sds((S, WIDTH), F32)] * 3 + [sds((S, D_MODEL), F32)] * 2
                  + [sds((S, HEADS), F32), sds((S, D_MODEL), MXU_DTYPE)]
                  + [sds((N_DEV,) + w.shape, w.dtype) for w in late],
        scratch_shapes=_exchange_scratch(nl),
        compiler_params=_params(("arbitrary",), VMEM_BIG),
    )(x, shift, scale, g_norm, w_main, w_f, *late)


def _log_sigmoid(z):
    return jnp.minimum(z, 0.0) - jnp.log(1.0 + jnp.exp(-jnp.abs(z)))


def _fgate_bwd(dfc, flc, bf_row, ts):
    S = flc.shape[0]
    n = S // ts

    def body(df_ref, flc_ref, bfr_ref, dfl_ref, dbf_ref, carry):
        @pl.when(pl.program_id(0) == 0)
        def _():
            carry[...] = jnp.zeros_like(carry)
            dbf_ref[...] = jnp.zeros_like(dbf_ref)

        ri = lax.broadcasted_iota(jnp.int32, (ts, ts), 0)
        ci = lax.broadcasted_iota(jnp.int32, (ts, ts), 1)
        upper = (ci >= ri).astype(F32)
        rc = _mm32(upper, df_ref[...]) + carry[...]
        carry[...] = rc[0:1, :]
        z = flc_ref[...] + bfr_ref[...]
        dfl = rc * _sigmoid(-z)
        dfl_ref[...] = dfl
        dbf_ref[...] += jnp.sum(dfl, axis=0, keepdims=True)

    col = pl.BlockSpec((ts, HEADS), lambda i: (n - 1 - i, 0))
    one = pl.BlockSpec((1, HEADS), lambda i: (0, 0))
    return pl.pallas_call(
        body, name="fgate_bwd", grid=(n,),
        in_specs=[col, col, one], out_specs=[col, one],
        out_shape=[jax.ShapeDtypeStruct((S, HEADS), F32), jax.ShapeDtypeStruct((1, HEADS), F32)],
        scratch_shapes=[pltpu.VMEM((1, HEADS), F32)],
        compiler_params=_params(("arbitrary",)),
    )(dfc, flc, bf_row)


N_EXTRA = 3


def _attn_prep(q, k, v, flc, bf_row, t):
    S = q.shape[0]
    nb = S // t

    def body(q_ref, k_ref, v_ref, flc_ref, bfr_ref, qh_ref, kh_ref, vt_ref, carry):
        @pl.when(pl.program_id(0) == 0)
        def _():
            carry[...] = jnp.zeros_like(carry)

        ri = lax.broadcasted_iota(jnp.int32, (t, t), 0)
        ci = lax.broadcasted_iota(jnp.int32, (t, t), 1)
        f = _mm32((ci <= ri).astype(F32), _log_sigmoid(flc_ref[...] + bfr_ref[...])) + carry[...]
        carry[...] = f[t - 1:t, :]
        lane = lax.broadcasted_iota(jnp.int32, (t, 128), 1)
        for p in range(PAIRS):
            qp = q_ref[:, p * 128:(p + 1) * 128]
            kp = k_ref[:, p * 128:(p + 1) * 128]
            vt_ref[p, 0] = v_ref[:, p * 128:(p + 1) * 128].T
            for h in range(2):
                own = (lane < 64) if h == 0 else (lane >= 64)
                base = 64 if h == 0 else 0
                fh = f[:, 2 * p + h:2 * p + h + 1]
                parts = []
                rest = fh
                for _ in range(N_EXTRA):
                    part = rest.astype(qh_ref.dtype)
                    parts.append(part)
                    rest = rest - part.astype(F32)
                one = jnp.ones((t, 1), qh_ref.dtype)
                eq = jnp.zeros((t, 128), qh_ref.dtype)
                ek = jnp.zeros((t, 128), qh_ref.dtype)
                for j in range(N_EXTRA):
                    eq = jnp.where(lane == base + j, parts[j], eq)
                    eq = jnp.where(lane == base + N_EXTRA + j, one, eq)
                    ek = jnp.where(lane == base + j, one, ek)
                    ek = jnp.where(lane == base + N_EXTRA + j, -parts[j], ek)
                qh_ref[2 * p + h] = jnp.where(own, qp, eq)
                kh_ref[2 * p + h] = jnp.where(own, kp, ek)

    row = pl.BlockSpec((t, WIDTH), lambda i: (i, 0))
    heads = pl.BlockSpec((HEADS, t, 128), lambda i: (0, i, 0))
    return pl.pallas_call(
        body, name="attn_prep", grid=(nb,),
        in_specs=[row, row, row, pl.BlockSpec((t, HEADS), lambda i: (i, 0)), pl.BlockSpec((1, HEADS), lambda i: (0, 0))],
        out_specs=[heads, heads, pl.BlockSpec((PAIRS, 1, 128, t), lambda i: (0, i, 0, 0))],
        out_shape=[jax.ShapeDtypeStruct((HEADS, S, 128), q.dtype), jax.ShapeDtypeStruct((HEADS, S, 128), k.dtype),
                   jax.ShapeDtypeStruct((PAIRS, nb, 128, t), v.dtype)],
        scratch_shapes=[pltpu.VMEM((1, HEADS), F32)],
        compiler_params=_params(("arbitrary",), VMEM_MID),
    )(q, k, v, flc, bf_row)


def _attn_fwd(qh, kh, vt, t):
    S = qh.shape[1]
    nb = S // t

    def body(q_ref, k_ref, vt_ref, o_ref, lse_ref, acc_s):
        qi = pl.program_id(1)
        acc_s[...] = jnp.zeros_like(acc_s)

        def step(ki, nblk, masked, carry):
            m_old, l_old = carry[:2], carry[2:]
            ks = pl.multiple_of(ki * t, t)
            rows = nblk * t
            sts = [_mm_nt(k_ref[h, pl.ds(ks, rows), :], q_ref[h]) for h in range(2)]
            if masked:
                ri = lax.broadcasted_iota(jnp.int32, (t, t), 0)
                ci = lax.broadcasted_iota(jnp.int32, (t, t), 1)
                sts = [jnp.where(ci >= ri, st, NEG) for st in sts]
            m_new = [jnp.maximum(m_old[h], jnp.max(sts[h], axis=0, keepdims=True)) for h in range(2)]
            alpha = [jnp.exp(m_old[h] - m_new[h]) for h in range(2)]
            pts = [jnp.exp(sts[h] - m_new[h]) for h in range(2)]
            l_new = [alpha[h] * l_old[h] + jnp.sum(pts[h], axis=0, keepdims=True) for h in range(2)]
            for h in range(2):
                pv = _mm(vt_ref[ki], pts[h][:t])
                for b in range(1, nblk):
                    pv = pv + _mm(vt_ref[ki + b], pts[h][b * t:(b + 1) * t])
                acc_s[h] = alpha[h] * acc_s[h] + pv
            return (*m_new, *l_new)

        init = (jnp.full((1, t), -jnp.inf, F32),) * 2 + (jnp.zeros((1, t), F32),) * 2
        carry = lax.fori_loop(0, qi // 2, lambda j, c: step(2 * j, 2, False, c), init)
        carry = lax.cond(qi % 2 == 1, lambda c: step(qi - 1, 1, False, c), lambda c: c, carry)
        m0, m1, l0, l1 = step(qi, 1, True, carry)
        first = lax.broadcasted_iota(jnp.int32, (128, t), 0) < 64
        o_ref[...] = jnp.where(first, acc_s[0] / l0, acc_s[1] / l1).T
        lse_ref[...] = jnp.concatenate([m0 + jnp.log(l0), m1 + jnp.log(l1)], axis=0)

    return pl.pallas_call(
        body, name="attn_fwd", grid=(PAIRS, nb),
        in_specs=[pl.BlockSpec((2, t, 128), lambda p, i: (p, i, 0)), pl.BlockSpec((2, S, 128), lambda p, i: (p, 0, 0)),
                  pl.BlockSpec((None, nb, 128, t), lambda p, i: (p, 0, 0, 0))],
        out_specs=[pl.BlockSpec((t, 128), lambda p, i: (i, p)), pl.BlockSpec((None, None, 2, t), lambda p, i: (p, i, 0, 0))],
        out_shape=[jax.ShapeDtypeStruct((S, WIDTH), F32), jax.ShapeDtypeStruct((PAIRS, nb, 2, t), F32)],
        scratch_shapes=[pltpu.VMEM((2, 128, t), F32)],
        compiler_params=_params(("parallel", "parallel"), VMEM_MID),
    )(qh, kh, vt)


def _attn_bwd(qh, do, kh, v, lse4, dl4, early, t):
    S = qh.shape[1]
    nb = S // t
    ne = len(early)

    def body(q_ref, do_ref, k_ref, v_ref, lse_ref, dl_ref, *rest):
        dq_ref, dk_ref, dv_ref, dfq_ref, dfk_ref = rest[ne:ne + 5]
        kc_s, vh_s, dk_s, dv_s, dfk_s = rest[2 * ne + 5:2 * ne + 10]
        kj = pl.program_id(1)
        _hosted_exchange((pl.program_id(0) == 0) & (kj == 0), (pl.program_id(0) == PAIRS - 1) & (kj == nb - 1),
                         rest[:ne], rest[ne + 5:2 * ne + 5], *rest[2 * ne + 10:], scatter=True)
        lane = lax.broadcasted_iota(jnp.int32, (t, 128), 1)
        is_a = lane < 64

        @pl.when(kj == 0)
        def _():
            dq_ref[...] = jnp.zeros_like(dq_ref)
            dfq_ref[...] = jnp.zeros_like(dfq_ref)

        vp = v_ref[...]
        zero = jnp.zeros_like(vp)
        kc_s[0] = jnp.where(is_a, k_ref[0], zero.astype(kc_s.dtype))
        kc_s[1] = jnp.where(is_a, zero.astype(kc_s.dtype), k_ref[1])
        vh_s[0] = jnp.where(is_a, vp, zero)
        vh_s[1] = jnp.where(is_a, zero, vp)
        dk_s[...] = jnp.zeros_like(dk_s)
        dv_s[...] = jnp.zeros_like(dv_s)
        dfk_s[...] = jnp.zeros_like(dfk_s)

        def step(qi, masked):
            qs = pl.multiple_of(qi * t, t)
            dob = do_ref[pl.ds(qs, t), :]
            lse = lse_ref[qi]
            dl = dl_ref[qi]
            zq = jnp.zeros_like(dob)
            over_keys = []
            for h in range(2):
                sel = is_a if h == 0 else jnp.logical_not(is_a)
                qb = q_ref[h, pl.ds(qs, t), :]
                st = _mm_nt(k_ref[h], qb) - lse[h:h + 1, :]
                if masked:
                    ri = lax.broadcasted_iota(jnp.int32, (t, t), 0)
                    ci = lax.broadcasted_iota(jnp.int32, (t, t), 1)
                    st = jnp.where(ci >= ri, st, NEG)
                pt = jnp.exp(st)
                dv_s[...] += _mm(pt, jnp.where(sel, dob, zq))
                dpt = _mm_nt(vh_s[h], dob)
                dst = pt * (dpt - dl[h:h + 1, :])
                dfk_s[h] += jnp.sum(dst, axis=1, keepdims=True)
                over_keys.append(jnp.sum(dst, axis=0, keepdims=True))
                dk_s[...] += _mm(dst, jnp.where(sel, qb, jnp.zeros_like(qb)))
                dq_ref[pl.ds(qs, t), :] += _mm_tn(dst, kc_s[h])
            dfq_ref[qi] += jnp.concatenate(over_keys, axis=0)

        step(kj, True)

        def loop_body(qi, carry):
            step(qi, False)
            return carry

        lax.fori_loop(kj + 1, nb, loop_body, 0)
        dk_ref[...] = dk_s[...].astype(dk_ref.dtype)
        dv_ref[...] = dv_s[...].astype(dv_ref.dtype)
        dfk_ref[...] = jnp.where(lax.broadcasted_iota(jnp.int32, (t, 2), 1) == 0, dfk_s[0], dfk_s[1])

        @pl.when(kj == nb - 1)
        def _():
            dq_ref[...] = dq_ref[...] * 0.125

    blk = pl.BlockSpec((t, 128), lambda p, j: (j, p))
    res = pl.BlockSpec((S, 128), lambda p, j: (0, p))
    rows4 = pl.BlockSpec((None, nb, 2, t), lambda p, j: (p, 0, 0, 0))
    cols4 = pl.BlockSpec((None, t, 2), lambda p, j: (p, j, 0))
    return pl.pallas_call(
        body, name="attn_bwd", grid=(PAIRS, nb),
        in_specs=[pl.BlockSpec((2, S, 128), lambda p, j: (p, 0, 0)), res,
                  pl.BlockSpec((2, t, 128), lambda p, j: (p, j, 0)), blk, rows4, rows4] + [ANY] * ne,
        out_specs=[res, blk, blk, rows4, cols4] + [ANY] * ne,
        out_shape=[jax.ShapeDtypeStruct((S, WIDTH), F32), jax.ShapeDtypeStruct((S, WIDTH), MXU_DTYPE),
                   jax.ShapeDtypeStruct((S, WIDTH), MXU_DTYPE), jax.ShapeDtypeStruct((PAIRS, nb, 2, t), F32),
                   jax.ShapeDtypeStruct((PAIRS, S, 2), F32)] + [jax.ShapeDtypeStruct(e.shape, e.dtype) for e in early],
        scratch_shapes=[pltpu.VMEM((2, t, 128), kh.dtype), pltpu.VMEM((2, t, 128), v.dtype),
                        pltpu.VMEM((t, 128), F32), pltpu.VMEM((t, 128), F32), pltpu.VMEM((2, t, 1), F32)]
                       + _exchange_scratch(ne),
        compiler_params=_params(("arbitrary", "arbitrary"), VMEM_MID),
    )(qh, do, kh, v, lse4, dl4, *early)


def _s5_mats(a_re, a_im, log_dt, b_re, b_im, c_re, c_im, d_skip):
    Lc = CHUNK
    dt = jnp.exp(log_dt)[:, None]
    lr, li = a_re * dt, a_im * dt

    def apow(n):
        n = jnp.asarray(n, F32)[None, :, None]
        mag = jnp.exp(n * lr[:, None, :])
        ang = n * li[:, None, :]
        return mag * jnp.cos(ang), mag * jnp.sin(ang)

    ar, ai = apow([1.0])
    ar, ai = ar[:, 0], ai[:, 0]
    den = a_re * a_re + a_im * a_im
    nr, ni = ar - 1.0, ai
    fr = (nr * a_re + ni * a_im) / den
    fi = (ni * a_re - nr * a_im) / den
    bbr = fr[:, :, None] * b_re - fi[:, :, None] * b_im
    bbi = fr[:, :, None] * b_im + fi[:, :, None] * b_re
    steps = np.arange(Lc, dtype=np.float32)
    pr, pi = apow(steps)
    car = c_re[:, None] * pr[:, :, None, :] - c_im[:, None] * pi[:, :, None, :]
    cai = c_re[:, None] * pi[:, :, None, :] + c_im[:, None] * pr[:, :, None, :]
    kern = (jnp.einsum('glcp,gpd->glcd', car, bbr, precision=HI)
            - jnp.einsum('glcp,gpd->glcd', cai, bbi, precision=HI))
    skip = d_skip.reshape(GROUPS, CG)[:, :, None] * jnp.eye(CG, dtype=F32)[None]
    kern = kern.at[:, 0].add(skip)
    trow = kern.transpose(0, 3, 1, 2).reshape(GROUPS, CG, Lc * CG)
    p1r, p1i = apow(steps + 1.0)
    cr = c_re[:, None] * p1r[:, :, None, :] - c_im[:, None] * p1i[:, :, None, :]
    ci = c_re[:, None] * p1i[:, :, None, :] + c_im[:, None] * p1r[:, :, None, :]
    to_rows = lambda m: m.transpose(0, 3, 1, 2).reshape(GROUPS, STATE, Lc * CG)
    camat = jnp.concatenate([to_rows(cr), -to_rows(ci)], axis=1)
    qr, qi = apow(Lc - 1.0 - steps)
    zr = qr[:, :, None, :] * bbr.transpose(0, 2, 1)[:, None] - qi[:, :, None, :] * bbi.transpose(0, 2, 1)[:, None]
    zi = qr[:, :, None, :] * bbi.transpose(0, 2, 1)[:, None] + qi[:, :, None, :] * bbr.transpose(0, 2, 1)[:, None]
    bzmat = jnp.concatenate([zr, zi], axis=-1).reshape(GROUPS, Lc * CG, 2 * STATE)
    lr_, li_ = apow([float(Lc)])
    al = jnp.concatenate([lr_[:, 0], li_[:, 0]], axis=-1)
    return trow, camat, bzmat, al


def _s5_scan_powers(a_re, a_im, log_dt, n_steps):
    dt = jnp.exp(log_dt)[:, None]
    lr, li = a_re * dt, a_im * dt
    n = (CHUNK * 2.0 ** np.arange(n_steps)).astype(np.float32)[None, :, None]
    mag = jnp.exp(n * lr[:, None, :])
    pr, pi = mag * jnp.cos(n * li[:, None, :]), mag * jnp.sin(n * li[:, None, :])
    fwd = jnp.stack([jnp.concatenate([pr, pr], -1), jnp.concatenate([-pi, pi], -1)], axis=2)
    bwd = jnp.stack([jnp.concatenate([pr, pr], -1), jnp.concatenate([pi, -pi], -1)], axis=2)
    return fwd, bwd


def _shift_rows(x, sh, down):
    n = x.shape[0]
    ri = lax.broadcasted_iota(jnp.int32, x.shape, 0)
    if down:
        return jnp.where(ri >= sh, pltpu.roll(x, sh, 0), 0.0)
    return jnp.where(ri < n - sh, pltpu.roll(x, n - sh, 0), 0.0)


GPB = 128 // CG


def _lane_transpose(arrs):
    lane = lax.broadcasted_iota(jnp.int32, arrs[0].shape, 1)
    arrs = list(arrs)
    k = GPB // 2
    while k >= 1:
        hi = ((lane // CG) & k) != 0
        new = list(arrs)
        for i in range(GPB):
            if i & k:
                continue
            lo_arr, hi_arr = arrs[i], arrs[i + k]
            new[i] = jnp.where(hi, pltpu.roll(hi_arr, CG * k, 1), lo_arr)
            new[i + k] = jnp.where(hi, hi_arr, pltpu.roll(lo_arr, 128 - CG * k, 1))
        arrs = new
        k //= 2
    return arrs


def _gather_block(ref, dst, nch):
    for half in range(CHUNK // GPB):
        outs = _lane_transpose([ref[pl.ds(half * GPB + l8, nch, stride=CHUNK), :] for l8 in range(GPB)])
        for g in range(GPB):
            dst[half, g] = outs[g]


def _scatter_block(src, ref, nch):
    for half in range(CHUNK // GPB):
        outs = _lane_transpose([src[half, g] for g in range(GPB)])
        for l8 in range(GPB):
            ref[pl.ds(half * GPB + l8, nch, stride=CHUNK), :] = outs[l8]


def _toeplitz(trow):
    lane = lax.broadcasted_iota(jnp.int32, (CG, 128), 1)
    x0, x1 = trow[:, :128], trow[:, 128:]
    zero = jnp.zeros_like(x0)
    rows = []
    for s in range(CHUNK):
        sh = (CG * s) % 128
        r0 = pltpu.roll(x0, sh, 1) if sh else x0
        r1 = pltpu.roll(x1, sh, 1) if sh else x1
        if CG * s < 128:
            rows.append(jnp.concatenate([jnp.where(lane >= sh, r0, zero), jnp.where(lane >= sh, r1, r0)], axis=1))
        else:
            rows.append(jnp.concatenate([zero, jnp.where(lane >= sh, r0, zero)], axis=1))
    return jnp.concatenate(rows, axis=0)


def _toeplitz_adjoint(dt):
    lane = lax.broadcasted_iota(jnp.int32, (CG, 128), 1)
    acc0 = jnp.zeros((CG, 128), F32)
    acc1 = jnp.zeros((CG, 128), F32)
    for s in range(CHUNK):
        x0, x1 = dt[CG * s:CG * s + CG, :128], dt[CG * s:CG * s + CG, 128:]
        sh = (CG * s) % 128
        keep = 128 - sh
        r0 = pltpu.roll(x0, keep, 1) if sh else x0
        r1 = pltpu.roll(x1, keep, 1) if sh else x1
        if CG * s < 128:
            acc0 = acc0 + jnp.where(lane < keep, r0, r1)
            acc1 = acc1 + jnp.where(lane < keep, r1, 0.0)
        else:
            acc0 = acc0 + jnp.where(lane < keep, r1, 0.0)
    return jnp.concatenate([acc0, acc1], axis=1)


def _s5_fwd(u, trow, camat, bzmat, pw):
    S = u.shape[0]
    nch = S // CHUNK
    n_steps = pw.shape[1]

    def body(u_ref, t_ref, ca_ref, bz_ref, pw_ref, y_ref, xp_ref, uc_ref, ub_s, yb_s):
        g = pl.program_id(1)

        @pl.when(g == 0)
        def _():
            _gather_block(u_ref, ub_s, nch)

        uc = jnp.concatenate([ub_s[0, g], ub_s[1, g]], axis=1)
        uc_ref[...] = uc
        x = _s5mm(uc, bz_ref[...])
        for kk in range(n_steps):
            xs = _shift_rows(x, 2 ** kk, True)
            m = pw_ref[kk]
            x = x + m[0:1, :] * xs + m[1:2, :] * pltpu.roll(xs, STATE, 1)
        xp = _shift_rows(x, 1, True)
        xp_ref[...] = xp
        yc = _s5mm(uc, _toeplitz(t_ref[...])) + _s5mm(xp, ca_ref[...])
        yb_s[0, g] = yc[:, :128]
        yb_s[1, g] = yc[:, 128:]

        @pl.when(g == GPB - 1)
        def _():
            _scatter_block(yb_s, y_ref, nch)

    per = lambda a: pl.BlockSpec((None,) + a.shape[1:], lambda b, g: (b * GPB + g,) + (0,) * (a.ndim - 1))
    nat = pl.BlockSpec((S, 128), lambda b, g: (0, b))
    return pl.pallas_call(
        body, name="s5_fwd", grid=(GROUPS // GPB, GPB),
        in_specs=[nat, per(trow), per(camat), per(bzmat), per(pw)],
        out_specs=[nat, pl.BlockSpec((None, nch, 2 * STATE), lambda b, g: (b * GPB + g, 0, 0)),
                   pl.BlockSpec((None, nch, CHUNK * CG), lambda b, g: (b * GPB + g, 0, 0))],
        out_shape=[jax.ShapeDtypeStruct((S, GROUPS * CG), F32), jax.ShapeDtypeStruct((GROUPS, nch, 2 * STATE), F32),
                   jax.ShapeDtypeStruct((GROUPS, nch, CHUNK * CG), F32)],
        scratch_shapes=[pltpu.VMEM((CHUNK // GPB, GPB, nch, 128), F32)] * 2,
        compiler_params=_params(("parallel", "arbitrary"), VMEM_BIG),
    )(u, trow, camat, bzmat, pw)


def _s5_bwd(uc, dy, xp, trow, camat, bzmat, pwc):
    S = dy.shape[0]
    nch = S // CHUNK
    n_steps = pwc.shape[1]

    def body(uc_ref, dy_ref, xp_ref, t_ref, ca_ref, bz_ref, pw_ref, du_ref, dt_ref, dca_ref, dbz_ref, dal_ref,
             dyb_s, dub_s):
        g = pl.program_id(1)

        @pl.when(g == 0)
        def _():
            _gather_block(dy_ref, dyb_s, nch)

        uc = uc_ref[...]
        dyc = jnp.concatenate([dyb_s[0, g], dyb_s[1, g]], axis=1)
        xpv = xp_ref[...]
        dt_ref[...] = _toeplitz_adjoint(_s5mm_tn(uc, dyc))
        dca_ref[...] = _s5mm_tn(xpv, dyc)
        dx = _shift_rows(_s5mm_nt(dyc, ca_ref[...]), 1, False)
        for kk in range(n_steps):
            xs = _shift_rows(dx, 2 ** kk, False)
            m = pw_ref[kk]
            dx = dx + m[0:1, :] * xs + m[1:2, :] * pltpu.roll(xs, STATE, 1)
        dbz_ref[...] = _s5mm_tn(uc, dx)
        dal_ref[0:1, :] = jnp.sum(dx * xpv, axis=0, keepdims=True)
        dal_ref[1:2, :] = jnp.sum(dx * pltpu.roll(xpv, STATE, 1), axis=0, keepdims=True)
        duc = _s5mm_nt(dyc, _toeplitz(t_ref[...])) + _s5mm_nt(dx, bz_ref[...])
        dub_s[0, g] = duc[:, :128]
        dub_s[1, g] = duc[:, 128:]

        @pl.when(g == GPB - 1)
        def _():
            _scatter_block(dub_s, du_ref, nch)

    per = lambda a: pl.BlockSpec((None,) + a.shape[1:], lambda b, g: (b * GPB + g,) + (0,) * (a.ndim - 1))
    nat = pl.BlockSpec((S, 128), lambda b, g: (0, b))
    sds = jax.ShapeDtypeStruct
    mats = [sds(trow.shape, F32), sds(camat.shape, F32), sds(bzmat.shape, F32), sds((GROUPS, 2, 2 * STATE), F32)]
    return pl.pallas_call(
        body, name="s5_bwd", grid=(GROUPS // GPB, GPB),
        in_specs=[per(uc), nat, per(xp), per(trow), per(camat), per(bzmat), per(pwc)],
        out_specs=[nat] + [per(o) for o in mats], out_shape=[sds((S, GROUPS * CG), F32)] + mats,
        scratch_shapes=[pltpu.VMEM((CHUNK // GPB, GPB, nch, 128), F32)] * 2,
        compiler_params=_params(("parallel", "arbitrary"), VMEM_BIG),
    )(uc, dy, xp, trow, camat, bzmat, pwc)


GELU_C0 = math.sqrt(2.0 / math.pi)
GELU_C1 = 0.044715


def _mix(o, za, ys, zb, ga, gb, x, tgt, gate, b_glu, g_final, w_glu, w_up_a, w_up_b, w_out, hsel, ts):
    S = o.shape[0]

    def body(o_ref, za_ref, ys_ref, zb_ref, ga_ref, gb_ref, x_ref, t_ref, gate_ref, bglu_ref, gf_ref,
             wglu_ref, wua_ref, wub_ref, wout_ref, hsel_ref,
             dx2_ref, do_ref, dza_ref, dzb_ref, dga_ref, dgb_ref, dys_ref, dl_ref,
             mg_ref, dmo_ref, ya_ref, dua_ref, yb_ref, dub_ref, yg_ref, dgl_ref,
             dbglu_ref, dgate_ref, dgf_ref, loss_ref):
        @pl.when(pl.program_id(0) == 0)
        def _():
            dbglu_ref[...] = jnp.zeros_like(dbglu_ref)
            dgate_ref[...] = jnp.zeros_like(dgate_ref)
            dgf_ref[...] = jnp.zeros_like(dgf_ref)
            loss_ref[...] = jnp.zeros_like(loss_ref)

        ov = o_ref[...]
        za = za_ref[...]
        sza = _sigmoid(za)
        silu_a = za * sza
        ya = ov * silu_a
        ya_b = ya.astype(ya_ref.dtype)
        ya_ref[...] = ya_b
        ysv = ys_ref[...]
        th = jnp.tanh(GELU_C0 * (ysv + GELU_C1 * ysv * ysv * ysv))
        yg = 0.5 * ysv * (1.0 + th)
        yg_b = yg.astype(yg_ref.dtype)
        yg_ref[...] = yg_b
        sg = _sigmoid(_mm(yg_b, wglu_ref[...]) + bglu_ref[...])
        yb1 = yg * sg
        zb = zb_ref[...]
        szb = _sigmoid(zb)
        silu_b = zb * szb
        yb_b = (yb1 * silu_b).astype(yb_ref.dtype)
        yb_ref[...] = yb_b
        ua = _mm(ya_b, wua_ref[...])
        ub = _mm(yb_b, wub_ref[...])
        sa = _sigmoid(ga_ref[...])
        sb = _sigmoid(gb_ref[...])
        merged_b = (sa * ua + sb * ub).astype(mg_ref.dtype)
        mg_ref[...] = merged_b
        mo = _mm(merged_b, wout_ref[...])
        gate_v = gate_ref[...]
        x2 = x_ref[...] + gate_v * mo
        r2 = lax.rsqrt(jnp.mean(x2 * x2, axis=-1, keepdims=True) + EPS)
        x2n = x2 * r2
        gf = gf_ref[...]
        diff = x2n * gf - t_ref[...]
        loss_ref[...] += jnp.sum(jnp.sum(diff * diff, axis=-1, keepdims=True), axis=0, keepdims=True) * (0.5 / D_MODEL)
        dy = diff * (1.0 / D_MODEL)
        dgf_ref[...] += jnp.sum(dy * x2n, axis=0, keepdims=True)
        dyg = dy * gf
        dx2 = r2 * (dyg - x2n * jnp.mean(dyg * x2n, axis=-1, keepdims=True))
        dx2_ref[...] = dx2
        dgate_ref[...] += jnp.sum(dx2 * mo, axis=0, keepdims=True)
        dmo_b = (dx2 * gate_v).astype(dmo_ref.dtype)
        dmo_ref[...] = dmo_b
        dmerged = _mm_nt(dmo_b, wout_ref[...])
        dua_b = (dmerged * sa).astype(dua_ref.dtype)
        dub_b = (dmerged * sb).astype(dub_ref.dtype)
        dua_ref[...] = dua_b
        dub_ref[...] = dub_b
        dga_ref[...] = (dmerged * ua * sa * (1.0 - sa)).astype(dga_ref.dtype)
        dgb_ref[...] = (dmerged * ub * sb * (1.0 - sb)).astype(dgb_ref.dtype)
        dya = _mm_nt(dua_b, wua_ref[...])
        dyb = _mm_nt(dub_b, wub_ref[...])
        dov = dya * silu_a
        do_ref[...] = dov.astype(do_ref.dtype)
        dl_ref[...] = _mm32_nt(hsel_ref[...], dov * ov)
        dza_ref[...] = (dya * ov * (sza * (1.0 + za * (1.0 - sza)))).astype(dza_ref.dtype)
        dyb1 = dyb * silu_b
        dzb_ref[...] = (dyb * yb1 * (szb * (1.0 + zb * (1.0 - szb)))).astype(dzb_ref.dtype)
        dgl = dyb1 * yg * sg * (1.0 - sg)
        dbglu_ref[...] += jnp.sum(dgl, axis=0, keepdims=True)
        dgl_b = dgl.astype(dgl_ref.dtype)
        dgl_ref[...] = dgl_b
        dyg2 = dyb1 * sg + _mm_nt(dgl_b, wglu_ref[...])
        dgelu = 0.5 * (1.0 + th) + 0.5 * ysv * (1.0 - th * th) * GELU_C0 * (1.0 + 3.0 * GELU_C1 * ysv * ysv)
        dys_ref[...] = dyg2 * dgelu

    row = lambda n: pl.BlockSpec((ts, n), lambda i: (i, 0))
    full = lambda a: pl.BlockSpec(a.shape, lambda i: (0,) * a.ndim)
    vec = lambda n: pl.BlockSpec((1, n), lambda i: (0, 0))
    sds = jax.ShapeDtypeStruct
    W, Dm = WIDTH, D_MODEL
    return pl.pallas_call(
        body, name="mix", grid=(S // ts,),
        in_specs=[row(W), row(W), row(W), row(W), row(Dm), row(Dm), row(Dm), row(Dm),
                  full(gate), full(b_glu), full(g_final), full(w_glu), full(w_up_a), full(w_up_b), full(w_out), full(hsel)],
        out_specs=[row(Dm), row(W), row(W), row(W), row(Dm), row(Dm), row(W), pl.BlockSpec((HEADS, ts), lambda i: (0, i)),
                   row(Dm), row(Dm), row(W), row(Dm), row(W), row(Dm), row(W), row(W),
                   vec(W), vec(Dm), vec(Dm), vec(1)],
        out_shape=[sds((S, Dm), F32), sds((S, W), MXU_DTYPE), sds((S, W), MXU_DTYPE), sds((S, W), MXU_DTYPE),
                   sds((S, Dm), MXU_DTYPE), sds((S, Dm), MXU_DTYPE), sds((S, W), F32), sds((HEADS, S), F32),
                   sds((S, Dm), MXU_DTYPE), sds((S, Dm), MXU_DTYPE), sds((S, W), MXU_DTYPE), sds((S, Dm), MXU_DTYPE),
                   sds((S, W), MXU_DTYPE), sds((S, Dm), MXU_DTYPE), sds((S, W), MXU_DTYPE), sds((S, W), MXU_DTYPE),
                   sds((1, W), F32), sds((1, Dm), F32), sds((1, Dm), F32), sds((1, 1), F32)],
        compiler_params=_params(("arbitrary",), VMEM_BIG),
    )(o, za, ys, zb, ga, gb, x, tgt, gate, b_glu, g_final, w_glu, w_up_a, w_up_b, w_out, hsel)


def _matmul_tn(name, a, b, ts):
    S, M = a.shape
    N = b.shape[1]
    tn = min(N, 1024)

    def body(a_ref, b_ref, o_ref):
        @pl.when(pl.program_id(1) == 0)
        def _():
            o_ref[...] = jnp.zeros_like(o_ref)

        o_ref[...] += _mm_tn(a_ref[...], b_ref[...])

    return pl.pallas_call(
        body, name=name, grid=(N // tn, S // ts),
        in_specs=[pl.BlockSpec((ts, M), lambda j, i: (i, 0)), pl.BlockSpec((ts, tn), lambda j, i: (i, j))],
        out_specs=pl.BlockSpec((M, tn), lambda j, i: (0, j)),
        out_shape=jax.ShapeDtypeStruct((M, N), F32),
        compiler_params=_params(("parallel", "arbitrary"), VMEM_MID),
    )(a, b)


def _proj_bwd(dq, dk, dv, dza, du, dzb, dga, dgb, dfl, x, dx2, shift, scale, g_norm, w_main, w_ft, chip_planes,
              gathers, ts):
    S = x.shape[0]
    nc, ng = len(chip_planes), len(gathers)
    nx = nc + ng

    def body(dq_ref, dk_ref, dv_ref, dza_ref, du_ref, dzb_ref, dga_ref, dgb_ref, dfl_ref, x_ref, dx2_ref,
             sc_ref, gn_ref, w_ref, wft_ref, *rest):
        gx_ref, dsh_ref, dsc_ref, dgn_ref = rest[nx:nx + 4]
        outs, sems = rest[nx + 4:2 * nx + 4], rest[2 * nx + 4:]
        i = pl.program_id(0)
        first, last = i == 0, i == pl.num_programs(0) - 1
        _hosted_chip_exchange(first, last, rest[:nc], outs[:nc], *sems[:3])
        _hosted_exchange(first, last, rest[nc:nx], outs[nc:], *sems[3:], scatter=False)

        @pl.when(pl.program_id(0) == 0)
        def _():
            dsh_ref[...] = jnp.zeros_like(dsh_ref)
            dsc_ref[...] = jnp.zeros_like(dsc_ref)
            dgn_ref[...] = jnp.zeros_like(dgn_ref)

        def seg(ref, off, n):
            return _mm_nt(ref[...], w_ref[:, off:off + n])

        dh = (seg(dq_ref, M_Q, WIDTH) + seg(dk_ref, M_K, WIDTH) + seg(dv_ref, M_V, WIDTH)
              + seg(dza_ref, M_ZA, WIDTH) + seg(du_ref, M_U, WIDTH) + seg(dzb_ref, M_ZB, WIDTH)
              + seg(dga_ref, M_GA, D_MODEL) + seg(dgb_ref, M_GB, D_MODEL)
              + _mm32(dfl_ref[...], wft_ref[...]))
        xv = x_ref[...]
        r = lax.rsqrt(jnp.mean(xv * xv, axis=-1, keepdims=True) + EPS)
        xn = xv * r
        gn = gn_ref[...]
        s1 = 1.0 + sc_ref[...]
        dsh_ref[...] += jnp.sum(dh, axis=0, keepdims=True)
        dhx = dh * xn
        dsc_ref[...] += jnp.sum(dhx, axis=0, keepdims=True) * gn
        dgn_ref[...] += jnp.sum(dhx, axis=0, keepdims=True) * s1
        dxn = dh * (gn * s1)
        gx_ref[...] = dx2_ref[...] + r * (dxn - xn * jnp.mean(dxn * xn, axis=-1, keepdims=True))

    row = lambda n: pl.BlockSpec((ts, n), lambda i: (i, 0))
    full = lambda a: pl.BlockSpec(a.shape, lambda i: (0,) * a.ndim)
    vec = pl.BlockSpec((1, D_MODEL), lambda i: (0, 0))
    W, Dm = WIDTH, D_MODEL
    del shift
    return pl.pallas_call(
        body, name="proj_bwd", grid=(S // ts,),
        in_specs=[row(W)] * 6 + [row(Dm)] * 2 + [row(HEADS), row(Dm), row(Dm),
                                                 full(scale), full(g_norm), full(w_main), full(w_ft)] + [ANY] * nx,
        out_specs=[row(Dm), vec, vec, vec] + [ANY] * nx,
        out_shape=[jax.ShapeDtypeStruct((S, Dm), F32)] + [jax.ShapeDtypeStruct((1, Dm), F32)] * 3
                  + [jax.ShapeDtypeStruct(p.shape, p.dtype) for p in chip_planes]
                  + [jax.ShapeDtypeStruct((N_DEV,) + g.shape, g.dtype) for g in gathers],
        scratch_shapes=[pltpu.SemaphoreType.DMA((nc, N_CHIP)), pltpu.SemaphoreType.DMA((nc, N_CHIP)),
                        pltpu.SemaphoreType.DMA((nc,))] + _exchange_scratch(ng),
        compiler_params=_params(("arbitrary",), VMEM_BIG),
    )(dq, dk, dv, dza, du, dzb, dga, dgb, dfl, x, dx2, scale, g_norm, w_main, w_ft, *chip_planes, *gathers)


def _adamw(name, planes, w, m, v, tr):
    n, R, C = planes.shape
    bc1 = 1.0 - ADAM_B1 ** ADAM_STEP
    bc2 = 1.0 - ADAM_B2 ** ADAM_STEP

    def body(p_ref, w_ref, m_ref, v_ref, g_ref, d_ref, nm_ref, nv_ref):
        g = p_ref[0].astype(F32)
        for i in range(1, n):
            g = g + p_ref[i].astype(F32)
        g_ref[...] = g
        nm = ADAM_B1 * m_ref[...] + (1.0 - ADAM_B1) * g
        nv = ADAM_B2 * v_ref[...] + (1.0 - ADAM_B2) * (g * g)
        nm_ref[...] = nm
        nv_ref[...] = nv
        d_ref[...] = -ADAM_LR * ((nm / bc1) / (jnp.sqrt(nv / bc2) + ADAM_EPS) + ADAM_WD * w_ref[...])

    blk = pl.BlockSpec((tr, C), lambda i: (i, 0))
    return pl.pallas_call(
        body, name=name, grid=(R // tr,),
        in_specs=[pl.BlockSpec((n, tr, C), lambda i: (0, i, 0)), blk, blk, blk],
        out_specs=[blk] * 4, out_shape=[jax.ShapeDtypeStruct((R, C), F32)] * 4,
        compiler_params=_params(("parallel",), VMEM_MID),
    )(planes, w, m, v)


def _wada_grad(c_all, dmod_cols):
    def body(c_ref, d_ref, o_ref):
        o_ref[0] = _mm32_tn(c_ref[...], d_ref[...])

    return pl.pallas_call(
        body, name="wada_grad",
        out_shape=jax.ShapeDtypeStruct((1, c_all.shape[1], dmod_cols.shape[1]), F32),
        in_specs=[VMEM, VMEM], out_specs=VMEM,
    )(c_all, dmod_cols)


SMALL_ORDER = ("b_ada", "g_norm", "b_f", "a_re", "a_im", "log_dt", "b_re", "b_im", "c_re", "c_im",
               "d_skip", "b_glu", "g_final")
BIG_ORDER = ("w_ada", "w_in", "w_glu", "w_up_a", "w_up_b", "w_out")
ALL_ORDER = ("w_ada", "b_ada", "g_norm", "w_in", "b_f", "a_re", "a_im", "log_dt", "b_re", "b_im", "c_re", "c_im",
             "d_skip", "w_glu", "b_glu", "w_up_a", "w_up_b", "w_out", "g_final")


def _pack_small(parts, rows):
    flat = jnp.concatenate([p.reshape(-1).astype(F32) for p in parts])
    return jnp.pad(flat, (0, rows * 128 - flat.shape[0])).reshape(rows, 128)


def kernel(x, c, w_ada, b_ada, g_norm, w_in, b_f, a_re, a_im, log_dt, b_re, b_im, c_re, c_im, d_skip, w_glu, b_glu, w_up_a, w_up_b, w_out, g_final, loss_target, m_w_ada, m_b_ada, m_g_norm, m_w_in, m_b_f, m_a_re, m_a_im, m_log_dt, m_b_re, m_b_im, m_c_re, m_c_im, m_d_skip, m_w_glu, m_b_glu, m_w_up_a, m_w_up_b, m_w_out, m_g_final, v_w_ada, v_b_ada, v_g_norm, v_w_in, v_b_f, v_a_re, v_a_im, v_log_dt, v_b_re, v_b_im, v_c_re, v_c_im, v_d_skip, v_w_glu, v_b_glu, v_w_up_a, v_w_up_b, v_w_out, v_g_final):
    weights = dict(w_ada=w_ada, b_ada=b_ada, g_norm=g_norm, w_in=w_in, b_f=b_f, a_re=a_re, a_im=a_im, log_dt=log_dt,
                   b_re=b_re, b_im=b_im, c_re=c_re, c_im=c_im, d_skip=d_skip, w_glu=w_glu, b_glu=b_glu,
                   w_up_a=w_up_a, w_up_b=w_up_b, w_out=w_out, g_final=g_final)
    mom_m = dict(w_ada=m_w_ada, b_ada=m_b_ada, g_norm=m_g_norm, w_in=m_w_in, b_f=m_b_f, a_re=m_a_re, a_im=m_a_im,
                 log_dt=m_log_dt, b_re=m_b_re, b_im=m_b_im, c_re=m_c_re, c_im=m_c_im, d_skip=m_d_skip, w_glu=m_w_glu,
                 b_glu=m_b_glu, w_up_a=m_w_up_a, w_up_b=m_w_up_b, w_out=m_w_out, g_final=m_g_final)
    mom_v = dict(w_ada=v_w_ada, b_ada=v_b_ada, g_norm=v_g_norm, w_in=v_w_in, b_f=v_b_f, a_re=v_a_re, a_im=v_a_im,
                 log_dt=v_log_dt, b_re=v_b_re, b_im=v_b_im, c_re=v_c_re, c_im=v_c_im, d_skip=v_d_skip, w_glu=v_w_glu,
                 b_glu=v_b_glu, w_up_a=v_w_up_a, w_up_b=v_w_up_b, w_out=v_w_out, g_final=v_g_final)
    xs = x[0]
    tgt = loss_target[0]
    S = xs.shape[0]
    ts = min(256, S)
    ta = min(512, S)
    tw = min(2048, S)
    nch = S // CHUNK
    n_steps = max(1, int(math.ceil(math.log2(nch))))
    me = _my_index()

    shards = [w.astype(MXU_DTYPE) for w in (w_in[0], w_glu[0], w_up_a[0], w_up_b[0], w_out[0])]
    mod8, c_all, gathered = _comm_in(c, w_ada[0], b_ada.reshape(N_DEV, -1), shards[:1])
    mod = mod8.reshape(1, 3 * D_MODEL)
    shift, scale, gate = mod[:, :D_MODEL], mod[:, D_MODEL:2 * D_MODEL], mod[:, 2 * D_MODEL:]
    w_in_full = gathered[0].transpose(1, 0, 2).reshape(D_MODEL, PROJ_WIDTH)
    w_main = jnp.concatenate([w_in_full[:, :OFF_F], w_in_full[:, OFF_F + HEADS:]], axis=1)
    w_f = w_in_full[:, OFF_F:OFF_F + HEADS].astype(F32)
    w_ft = w_f.T

    q, k, v, za, u, zb, ga, gb, flc, hb, *late = _proj_fwd(xs, shift, scale, g_norm, w_main, w_f, shards[1:], ts)
    w_glu_full = late[0].reshape(WIDTH, WIDTH)
    w_up_a_full = late[1].transpose(1, 0, 2).reshape(WIDTH, D_MODEL)
    w_up_b_full = late[2].transpose(1, 0, 2).reshape(WIDTH, D_MODEL)
    w_out_full = late[3].reshape(D_MODEL, D_MODEL)
    nb = S // ta
    rows4 = lambda r: r.reshape(PAIRS, 2, nb, ta).transpose(0, 2, 1, 3)
    qh, kh, vt = _attn_prep(q, k, v, flc, b_f, ta)
    o, lse4 = _attn_fwd(qh, kh, vt, ta)

    s5_params = (a_re[0], a_im[0], log_dt[0], b_re[0], b_im[0], c_re[0], c_im[0], d_skip[0])
    (trow, camat, bzmat, al), mats_vjp = jax.vjp(_s5_mats, *s5_params)
    del al
    pw_f, pw_b = _s5_scan_powers(a_re[0], a_im[0], log_dt[0], n_steps)
    ys, xprev, uc = _s5_fwd(u, trow, camat, bzmat, pw_f)

    hsel = (np.arange(WIDTH)[None, :] // 64 == np.arange(HEADS)[:, None]).astype(np.float32)
    (dx2, do, dza, dzb, dga, dgb, dys, dl_row, merged, dmo, ya, dua, yb, dub, yg, dgl,
     db_glu, dgate, dg_final, loss_part) = _mix(o, za, ys, zb, ga, gb, xs, tgt, gate, b_glu, g_final.reshape(1, -1),
                                                w_glu_full, w_up_a_full, w_up_b_full, w_out_full, jnp.asarray(hsel), ts)

    gw_out = _matmul_tn("dw_out", merged, dmo, tw)
    gw_up_a = _matmul_tn("dw_up_a", ya, dua, tw)
    gw_up_b = _matmul_tn("dw_up_b", yb, dub, tw)
    gw_glu = _matmul_tn("dw_glu", yg, dgl, tw)

    du, d_trow, d_camat, d_bzmat, dal2 = _s5_bwd(uc, dys, xprev, trow, camat, bzmat, pw_b)
    d_al = jnp.concatenate([dal2[:, 0, :STATE] + dal2[:, 0, STATE:], dal2[:, 1, STATE:] - dal2[:, 1, :STATE]], axis=-1)
    gs5 = mats_vjp((d_trow, d_camat, d_bzmat, d_al))

    dl4 = rows4(dl_row)
    early = [p.astype(MXU_DTYPE) for p in (gw_glu.reshape(N_DEV, -1, WIDTH),
                                           gw_up_a.reshape(WIDTH, N_DEV, -1).transpose(1, 0, 2),
                                           gw_up_b.reshape(WIDTH, N_DEV, -1).transpose(1, 0, 2),
                                           gw_out.reshape(N_DEV, -1, D_MODEL))]
    dq, dk, dv, dfq4, dfk4, *early_recv = _attn_bwd(qh, do, kh, v, lse4, dl4, early, ta)
    d_fcol = dfq4.transpose(0, 2, 1, 3).reshape(HEADS, S).T - dfk4.transpose(1, 0, 2).reshape(S, HEADS)
    dfl, db_f = _fgate_bwd(d_fcol, flc, b_f, ta)

    segs = [("dw_q", dq), ("dw_k", dk), ("dw_v", dv), ("dw_f", dfl), ("dw_za", dza), ("dw_u", du), ("dw_zb", dzb),
            ("dw_ga", dga), ("dw_gb", dgb)]
    gw_in = jnp.concatenate([_matmul_tn(nm, hb, d, tw) for nm, d in segs], axis=1)

    planes = [gw_in.reshape(D_MODEL, N_DEV, -1).transpose(1, 0, 2).astype(MXU_DTYPE)]
    from_sib = _comm_pair(planes)
    core = lax.axis_index("c")
    chip_planes = []
    for name, p, s in zip(("w_in",), planes, from_sib):
        tr = 256 if s.shape[1] % 256 == 0 else s.shape[1]
        chip_planes.append(_pair_sum("pair_sum_" + name, p, s, core, tr))
    ready_parts = [db_f, gs5[0], gs5[1], gs5[2], gs5[3], gs5[4], gs5[5], gs5[6], gs5[7], db_glu, dg_final, loss_part]
    n_ready = sum(int(np.prod(p.shape)) for p in ready_parts)
    rows_ready = -(-n_ready // (8 * 128)) * 8
    grad_x, dshift, dscale, dg_norm, recv, ready_all = _proj_bwd(
        dq, dk, dv, dza, du, dzb, dga, dgb, dfl, xs, dx2, shift, scale, g_norm, w_main, w_ft, chip_planes,
        [_pack_small(ready_parts, rows_ready)], ts)
    recv = [recv]

    dmod = jnp.concatenate([dshift, dscale, dgate], axis=1)
    rows_late = 4 * D_MODEL // 128
    late_all = _comm_small(_pack_small([dmod, dg_norm], rows_late))
    small_all = jnp.concatenate([late_all, ready_all], axis=1)
    rows = rows_late + rows_ready

    grads, deltas, new_m, new_v = {}, {}, {}, {}

    def put(name, res, shape):
        grads[name], deltas[name], new_m[name], new_v[name] = [r.reshape(shape) for r in res]

    names = ("w_in", "w_glu", "w_up_a", "w_up_b", "w_out")
    for name, pr in zip(names, recv + early_recv):
        w2 = weights[name][0]
        tr = 256 if w2.shape[0] % 256 == 0 else w2.shape[0]
        put(name, _adamw("adamw_" + name, pr, w2, mom_m[name][0], mom_v[name][0], tr), weights[name].shape)
    cols = w_ada.shape[2]
    dmod_all = small_all[:, :24, :].reshape(N_DEV, 3 * D_MODEL)
    dmod_cols = lax.dynamic_slice_in_dim(dmod_all, me * cols, cols, axis=1)
    g_wada = _wada_grad(c_all, dmod_cols)
    put("w_ada", _adamw("adamw_w_ada", g_wada, w_ada[0], m_w_ada[0], v_w_ada[0], 256), w_ada.shape)
    pack = lambda d: _pack_small([d[n] for n in SMALL_ORDER] + [jnp.zeros((1,), F32)], rows)
    res_small = _adamw("adamw_small", small_all, pack(weights), pack(mom_m), pack(mom_v), rows)
    flat = [r.reshape(-1) for r in res_small]
    off = 0
    for name in SMALL_ORDER:
        shape = weights[name].shape
        size = int(np.prod(shape))
        put(name, [f[off:off + size] for f in flat], shape)
        off += size
    loss = flat[0][off]

    return (loss, grad_x[None], *[grads[n] for n in ALL_ORDER], *[deltas[n] for n in ALL_ORDER],
            *[new_m[n] for n in ALL_ORDER], *[new_v[n] for n in ALL_ORDER])
```

```python
import math

import jax
import jax.numpy as jnp
import numpy as np
from jax import lax
from jax.experimental import pallas as pl
from jax.experimental.pallas import tpu as pltpu

F32 = jnp.float32
MXU_DTYPE = jnp.bfloat16
HI = lax.Precision.HIGHEST

N_DEV = 8
D_MODEL = 1024
WIDTH = 512
HEADS = 8
PAIRS = HEADS // 2
GROUPS = 32
STATE = 64
CG = 16
CHUNK = 16
EPS = 1e-6
NEG = float(np.finfo(np.float32).min)

ADAM_LR = 0.001
ADAM_B1 = 0.9
ADAM_B2 = 0.999
ADAM_EPS = 1e-08
ADAM_WD = 0.01
ADAM_STEP = 10

VMEM_BIG = 56 * 1024 * 1024
VMEM_MID = 40 * 1024 * 1024

OFF_F = 3 * WIDTH
PROJ_WIDTH = 5128
M_Q, M_K, M_V, M_ZA, M_U, M_ZB, M_GA, M_GB = 0, 512, 1024, 1536, 2048, 2560, 3072, 4096


def _mm(a, b):
    return jnp.dot(a.astype(MXU_DTYPE), b.astype(MXU_DTYPE), preferred_element_type=F32)


def _mm_nt(a, b):
    return lax.dot_general(a.astype(MXU_DTYPE), b.astype(MXU_DTYPE), (((1,), (1,)), ((), ())),
                           preferred_element_type=F32)


def _mm_tn(a, b):
    return lax.dot_general(a.astype(MXU_DTYPE), b.astype(MXU_DTYPE), (((0,), (0,)), ((), ())),
                           preferred_element_type=F32)


def _mm32(a, b):
    return jnp.dot(a, b, precision=HI, preferred_element_type=F32)


def _mm32_nt(a, b):
    return lax.dot_general(a, b, (((1,), (1,)), ((), ())), precision=HI, preferred_element_type=F32)


def _mm32_tn(a, b):
    return lax.dot_general(a, b, (((0,), (0,)), ((), ())), precision=HI, preferred_element_type=F32)


S5_PRECISION = lax.Precision.HIGH


def _s5mm(a, b):
    return jnp.dot(a, b, precision=S5_PRECISION, preferred_element_type=F32)


def _s5mm_nt(a, b):
    return lax.dot_general(a, b, (((1,), (1,)), ((), ())), precision=S5_PRECISION, preferred_element_type=F32)


def _s5mm_tn(a, b):
    return lax.dot_general(a, b, (((0,), (0,)), ((), ())), precision=S5_PRECISION, preferred_element_type=F32)


def _sigmoid(x):
    return 1.0 / (1.0 + jnp.exp(-x))


def _params(sem=None, vmem=None):
    kw = {}
    if sem is not None:
        kw["dimension_semantics"] = sem
    if vmem is not None:
        kw["vmem_limit_bytes"] = vmem
    return pltpu.CompilerParams(**kw)


def _my_index():
    return 4 * lax.axis_index("x") + 2 * lax.axis_index("y") + lax.axis_index("c")


def _dev(p):
    return (p // 4, (p // 2) % 2, p % 2)


ANY = pl.BlockSpec(memory_space=pl.ANY)
VMEM = pl.BlockSpec(memory_space=pltpu.VMEM)
MESH = pl.DeviceIdType.MESH


def _comm_in(c, w_ada, b_ada8, shards):
    n = len(shards)
    cols = w_ada.shape[1]

    def body(c_ref, wada_ref, bada_ref, *rest):
        srcs = rest[:n]
        mod_ref, call_ref = rest[n], rest[n + 1]
        dsts = rest[n + 2:2 * n + 2]
        modp, wsend, wrecv, wloc, csend, crecv, msend, mrecv = rest[2 * n + 2:]
        me = _my_index()

        x, y, cc = lax.axis_index("x"), lax.axis_index("y"), lax.axis_index("c")
        here, sib = (x, y, cc), (x, y, 1 - cc)
        xn, yn, dg = (1 - x, y, cc), (x, 1 - y, cc), (1 - x, 1 - y, cc)
        half = srcs[0].shape[0] // 2
        parts = [(0, pl.ds(0, half)), (0, pl.ds(half, half))] + [(a, None) for a in range(1, n)]
        via_y = [i % 2 == 0 for i in range(len(parts))]

        def wcopy(i, k, block, to, own=False):
            a, rs = parts[i]
            dst = dsts[a].at[4 * block[0] + 2 * block[1] + block[2]]
            src = srcs[a] if own else dst
            if rs is not None:
                src, dst = src.at[rs], dst.at[rs]
            return pltpu.make_async_remote_copy(src_ref=src, dst_ref=dst,
                                                send_sem=wsend.at[i, k], recv_sem=wrecv.at[i, k],
                                                device_id=to, device_id_type=MESH)

        def ccopy(src_dev, d, to):
            return pltpu.make_async_remote_copy(src_ref=c_ref, dst_ref=call_ref.at[pl.ds(src_dev, 1)],
                                                send_sem=csend.at[d], recv_sem=crecv.at[src_dev],
                                                device_id=_dev(to), device_id_type=MESH)

        def mcopy(src_dev, d, to):
            return pltpu.make_async_remote_copy(src_ref=modp.at[pl.ds(to, 1)], dst_ref=mod_ref.at[pl.ds(src_dev, 1)],
                                                send_sem=msend.at[d], recv_sem=mrecv.at[src_dev],
                                                device_id=_dev(to), device_id_type=MESH)

        local = [pltpu.make_async_copy(srcs[a], dsts[a].at[me], wloc.at[a]) for a in range(n)]
        for cp in local:
            cp.start()
        peers = [(me + d) % N_DEV for d in range(1, N_DEV)]
        sends = []
        for i in range(len(parts)):
            sends += [wcopy(i, 0, here, sib, own=True), wcopy(i, 1, here, xn, own=True), wcopy(i, 2, here, yn, own=True)]
        for cp in sends:
            cp.start()
        call_ref[pl.ds(me, 1), :] = c_ref[...]
        for d, p in enumerate(peers):
            ccopy(me, d, p).start()
        for d, p in enumerate(peers):
            ccopy(p, d, p).wait_recv()
        modp[...] = _mm32(call_ref[...], wada_ref[...]) + bada_ref[pl.ds(me, 1), :]
        mod_ref[pl.ds(me, 1), :] = modp[pl.ds(me, 1), :]
        for d, p in enumerate(peers):
            mcopy(me, d, p).start()
        for d, p in enumerate(peers):
            mcopy(p, d, p).wait_recv()
        def after(i, k, block, nxt):
            wcopy(i, k, block, here).wait_recv()
            for kk, to in nxt:
                cp = wcopy(i, kk, block, to)
                cp.start()
                sends.append(cp)

        for i in range(len(parts)):
            after(i, 1, xn, [(3, sib)] + ([(5, yn)] if via_y[i] else []))
        for i in range(len(parts)):
            after(i, 2, yn, [(4, sib)] + ([] if via_y[i] else [(6, xn)]))
        for i in range(len(parts)):
            after(i, 5 if via_y[i] else 6, dg, [(7, sib)])
        for i in range(len(parts)):
            wcopy(i, 0, sib, here).wait_recv()
            for k, block in ((3, xn), (4, yn), (7, dg)):
                wcopy(i, k, (block[0], block[1], 1 - cc), here).wait_recv()
        for cp in sends:
            cp.wait_send()
        for d, p in enumerate(peers):
            ccopy(me, d, p).wait_send()
            mcopy(me, d, p).wait_send()
        for cp in local:
            cp.wait()

    out_shape = ([jax.ShapeDtypeStruct((N_DEV, cols), F32), jax.ShapeDtypeStruct((N_DEV, D_MODEL), F32)]
                 + [jax.ShapeDtypeStruct((N_DEV,) + s.shape, s.dtype) for s in shards])
    res = pl.pallas_call(
        body, name="comm_in", out_shape=out_shape,
        in_specs=[VMEM, VMEM, VMEM] + [ANY] * n,
        out_specs=[VMEM, VMEM] + [ANY] * n,
        scratch_shapes=[pltpu.VMEM((N_DEV, cols), F32),
                        pltpu.SemaphoreType.DMA((n + 1, N_DEV)), pltpu.SemaphoreType.DMA((n + 1, N_DEV)),
                        pltpu.SemaphoreType.DMA((n,)),
                        pltpu.SemaphoreType.DMA((N_DEV,)), pltpu.SemaphoreType.DMA((N_DEV,)),
                        pltpu.SemaphoreType.DMA((N_DEV,)), pltpu.SemaphoreType.DMA((N_DEV,))],
        compiler_params=_params(vmem=VMEM_MID),
    )(c, w_ada, b_ada8, *shards)
    return res[0], res[1], list(res[2:])


def _direct_copy(srcs, dsts, send, recv, scatter, a, d, receiving):
    me = _my_index()
    p = (me + d) % N_DEV
    slot = p if receiving else me
    return pltpu.make_async_remote_copy(src_ref=srcs[a].at[p] if scatter else srcs[a], dst_ref=dsts[a].at[slot],
                                        send_sem=send.at[a, d], recv_sem=recv.at[a, slot],
                                        device_id=_dev(p), device_id_type=MESH)


def _hosted_exchange(first, last, srcs, dsts, send, recv, loc, scatter):
    me = _my_index()
    n = len(srcs)
    local = lambda a: pltpu.make_async_copy(srcs[a].at[me] if scatter else srcs[a], dsts[a].at[me], loc.at[a])
    pairs = [(a, d) for d in range(1, N_DEV) for a in range(n)]

    @pl.when(first)
    def _():
        for a in range(n):
            local(a).start()
        for a, d in pairs:
            _direct_copy(srcs, dsts, send, recv, scatter, a, d, False).start()

    @pl.when(last)
    def _():
        for a, d in pairs:
            _direct_copy(srcs, dsts, send, recv, scatter, a, d, True).wait_recv()
            _direct_copy(srcs, dsts, send, recv, scatter, a, d, False).wait_send()
        for a in range(n):
            local(a).wait()


def _exchange_scratch(n):
    return [pltpu.SemaphoreType.DMA((n, N_DEV)), pltpu.SemaphoreType.DMA((n, N_DEV)), pltpu.SemaphoreType.DMA((n,))]


def _hosted_chip_exchange(first, last, srcs, dsts, send, recv, loc):
    x, y, cc = lax.axis_index("x"), lax.axis_index("y"), lax.axis_index("c")
    mine = 2 * x + y
    chips = [(1 - x, y), (x, 1 - y), (1 - x, 1 - y)]
    n = len(srcs)

    def copy(a, j, sending):
        chip = chips[j]
        there = 2 * chip[0] + chip[1]
        return pltpu.make_async_remote_copy(src_ref=srcs[a].at[there], dst_ref=dsts[a].at[mine if sending else there],
                                            send_sem=send.at[a, j], recv_sem=recv.at[a, j],
                                            device_id=(*chip, cc), device_id_type=MESH)

    local = lambda a: pltpu.make_async_copy(srcs[a].at[mine], dsts[a].at[mine], loc.at[a])

    @pl.when(first)
    def _():
        for a in range(n):
            local(a).start()
            for j in range(len(chips)):
                copy(a, j, True).start()

    @pl.when(last)
    def _():
        for a in range(n):
            for j in range(len(chips)):
                copy(a, j, False).wait_recv()
                copy(a, j, True).wait_send()
            local(a).wait()


N_CHIP = 4


def _comm_pair(planes):
    n = len(planes)

    def body(*rest):
        srcs, dsts = rest[:n], rest[n:2 * n]
        send, recv = rest[2 * n:]
        x, y, cc = lax.axis_index("x"), lax.axis_index("y"), lax.axis_index("c")
        copies = [pltpu.make_async_remote_copy(src_ref=srcs[a].at[2 * ch + 1 - cc], dst_ref=dsts[a].at[ch],
                                               send_sem=send.at[a, ch], recv_sem=recv.at[a, ch],
                                               device_id=(x, y, 1 - cc), device_id_type=MESH)
                  for a in range(n) for ch in range(N_CHIP)]
        for cp in copies:
            cp.start()
        for cp in copies:
            cp.wait()

    out_shape = [jax.ShapeDtypeStruct((N_CHIP,) + p.shape[1:], p.dtype) for p in planes]
    return pl.pallas_call(
        body, name="comm_pair", out_shape=out_shape, in_specs=[ANY] * n, out_specs=[ANY] * n,
        scratch_shapes=[pltpu.SemaphoreType.DMA((n, N_CHIP)), pltpu.SemaphoreType.DMA((n, N_CHIP))],
    )(*planes)


def _pair_sum(name, planes, from_sib, core, tr):
    _, R, C = from_sib.shape

    def body(core_ref, a_ref, b_ref, o_ref):
        del core_ref
        o_ref[...] = (a_ref[...].astype(F32) + b_ref[...].astype(F32)).astype(o_ref.dtype)

    blk = pl.BlockSpec((None, tr, C), lambda i, j, c: (i, j, 0))
    grid_spec = pltpu.PrefetchScalarGridSpec(
        num_scalar_prefetch=1, grid=(N_CHIP, R // tr),
        in_specs=[pl.BlockSpec((None, tr, C), lambda i, j, c: (2 * i + c[0], j, 0)), blk], out_specs=blk)
    return pl.pallas_call(
        body, name=name, grid_spec=grid_spec, out_shape=jax.ShapeDtypeStruct(from_sib.shape, from_sib.dtype),
        compiler_params=_params(("parallel", "parallel"), VMEM_MID),
    )(core.reshape(1).astype(jnp.int32), planes, from_sib)


def _comm_small(small):
    rows = small.shape[0]
    cut = (rows // 16) * 8

    def body(small_ref, sall_ref, ssend, srecv, sloc):
        me = _my_index()
        x, y, cc = lax.axis_index("x"), lax.axis_index("y"), lax.axis_index("c")
        here, sib = (x, y, cc), (x, y, 1 - cc)
        xn, yn, dg = (1 - x, y, cc), (x, 1 - y, cc), (1 - x, 1 - y, cc)
        sparts = [pl.ds(0, cut), pl.ds(cut, rows - cut)]
        via_y = [True, False]

        def scopy(i, k, block, to, own=False):
            dst = sall_ref.at[4 * block[0] + 2 * block[1] + block[2]].at[sparts[i]]
            src = small_ref.at[sparts[i]] if own else dst
            return pltpu.make_async_remote_copy(src_ref=src, dst_ref=dst, send_sem=ssend.at[i, k], recv_sem=srecv.at[i, k],
                                                device_id=to, device_id_type=MESH)

        local = pltpu.make_async_copy(small_ref, sall_ref.at[me], sloc)
        local.start()
        sends = []
        for i in range(len(sparts)):
            sends += [scopy(i, 0, here, sib, own=True), scopy(i, 1, here, xn, own=True), scopy(i, 2, here, yn, own=True)]
        for cp in sends:
            cp.start()

        def after(i, k, block, nxt):
            scopy(i, k, block, here).wait_recv()
            for kk, to in nxt:
                cp = scopy(i, kk, block, to)
                cp.start()
                sends.append(cp)

        for i in range(len(sparts)):
            after(i, 1, xn, [(3, sib)] + ([(5, yn)] if via_y[i] else []))
        for i in range(len(sparts)):
            after(i, 2, yn, [(4, sib)] + ([] if via_y[i] else [(6, xn)]))
        for i in range(len(sparts)):
            after(i, 5 if via_y[i] else 6, dg, [(7, sib)])
        for i in range(len(sparts)):
            scopy(i, 0, sib, here).wait_recv()
            for k, block in ((3, xn), (4, yn), (7, dg)):
                scopy(i, k, (block[0], block[1], 1 - cc), here).wait_recv()
        for cp in sends:
            cp.wait_send()
        local.wait()

    return pl.pallas_call(
        body, name="comm_small", out_shape=jax.ShapeDtypeStruct((N_DEV,) + small.shape, small.dtype),
        in_specs=[ANY], out_specs=ANY,
        scratch_shapes=[pltpu.SemaphoreType.DMA((2, N_DEV)), pltpu.SemaphoreType.DMA((2, N_DEV)),
                        pltpu.SemaphoreType.DMA(())],
    )(small)


def _proj_fwd(x, shift, scale, g_norm, w_main, w_f, late, ts):
    S = x.shape[0]
    nl = len(late)

    def body(x_ref, sh_ref, sc_ref, gn_ref, w_ref, wf_ref, *rest):
        (q_ref, k_ref, v_ref, za_ref, u_ref, zb_ref, ga_ref, gb_ref, flc_ref, h_ref) = rest[nl:nl + 10]
        i = pl.program_id(0)
        _hosted_exchange(i == 0, i == pl.num_programs(0) - 1, rest[:nl], rest[nl + 10:2 * nl + 10],
                         *rest[2 * nl + 10:], scatter=False)
        xv = x_ref[...]
        r = lax.rsqrt(jnp.mean(xv * xv, axis=-1, keepdims=True) + EPS)
        h = (xv * r) * gn_ref[...] * (1.0 + sc_ref[...]) + sh_ref[...]
        hb = h.astype(MXU_DTYPE)
        h_ref[...] = hb

        def seg(off, n):
            return jnp.dot(hb, w_ref[:, off:off + n], preferred_element_type=F32)

        q_ref[...] = (seg(M_Q, WIDTH) * 0.125).astype(q_ref.dtype)
        k_ref[...] = seg(M_K, WIDTH).astype(k_ref.dtype)
        v_ref[...] = seg(M_V, WIDTH).astype(v_ref.dtype)
        za_ref[...] = seg(M_ZA, WIDTH)
        u_ref[...] = seg(M_U, WIDTH)
        zb_ref[...] = seg(M_ZB, WIDTH)
        ga_ref[...] = seg(M_GA, D_MODEL)
        gb_ref[...] = seg(M_GB, D_MODEL)
        flc_ref[...] = _mm32(h, wf_ref[...])

    row = lambda n: pl.BlockSpec((ts, n), lambda i: (i, 0))
    full = lambda a: pl.BlockSpec(a.shape, lambda i: (0,) * a.ndim)
    sds = jax.ShapeDtypeStruct
    return pl.pallas_call(
        body, name="proj_fwd", grid=(S // ts,),
        in_specs=[row(D_MODEL), full(shift), full(scale), full(g_norm), full(w_main), full(w_f)] + [ANY] * nl,
        out_specs=[row(WIDTH)] * 6 + [row(D_MODEL)] * 2 + [row(HEADS), row(D_MODEL)] + [ANY] * nl,
        out_shape=[sds((S, WIDTH), MXU_DTYPE)] * 3 + [sds((S, WIDTH), F32)] * 3 + [sds((S, D_MODEL), F32)] * 2
                  + [sds((S, HEADS), F32), sds((S, D_MODEL), MXU_DTYPE)]
                  + [sds((N_DEV,) + w.shape, w.dtype) for w in late],
        scratch_shapes=_exchange_scratch(nl),
        compiler_params=_params(("arbitrary",), VMEM_BIG),
    )(x, shift, scale, g_norm, w_main, w_f, *late)


def _log_sigmoid(z):
    return jnp.minimum(z, 0.0) - jnp.log(1.0 + jnp.exp(-jnp.abs(z)))


def _fgate_bwd(dfc, flc, bf_row, ts):
    S = flc.shape[0]
    n = S // ts

    def body(df_ref, flc_ref, bfr_ref, dfl_ref, dbf_ref, carry):
        @pl.when(pl.program_id(0) == 0)
        def _():
            carry[...] = jnp.zeros_like(carry)
            dbf_ref[...] = jnp.zeros_like(dbf_ref)

        ri = lax.broadcasted_iota(jnp.int32, (ts, ts), 0)
        ci = lax.broadcasted_iota(jnp.int32, (ts, ts), 1)
        upper = (ci >= ri).astype(F32)
        rc = _mm32(upper, df_ref[...]) + carry[...]
        carry[...] = rc[0:1, :]
        z = flc_ref[...] + bfr_ref[...]
        dfl = rc * _sigmoid(-z)
        dfl_ref[...] = dfl
        dbf_ref[...] += jnp.sum(dfl, axis=0, keepdims=True)

    col = pl.BlockSpec((ts, HEADS), lambda i: (n - 1 - i, 0))
    one = pl.BlockSpec((1, HEADS), lambda i: (0, 0))
    return pl.pallas_call(
        body, name="fgate_bwd", grid=(n,),
        in_specs=[col, col, one], out_specs=[col, one],
        out_shape=[jax.ShapeDtypeStruct((S, HEADS), F32), jax.ShapeDtypeStruct((1, HEADS), F32)],
        scratch_shapes=[pltpu.VMEM((1, HEADS), F32)],
        compiler_params=_params(("arbitrary",)),
    )(dfc, flc, bf_row)


N_EXTRA = 3


def _attn_prep(q, k, v, flc, bf_row, t):
    S = q.shape[0]
    nb = S // t

    def body(q_ref, k_ref, v_ref, flc_ref, bfr_ref, qh_ref, kh_ref, vt_ref, carry):
        @pl.when(pl.program_id(0) == 0)
        def _():
            carry[...] = jnp.zeros_like(carry)

        ri = lax.broadcasted_iota(jnp.int32, (t, t), 0)
        ci = lax.broadcasted_iota(jnp.int32, (t, t), 1)
        f = _mm32((ci <= ri).astype(F32), _log_sigmoid(flc_ref[...] + bfr_ref[...])) + carry[...]
        carry[...] = f[t - 1:t, :]
        lane = lax.broadcasted_iota(jnp.int32, (t, 128), 1)
        for p in range(PAIRS):
            qp = q_ref[:, p * 128:(p + 1) * 128]
            kp = k_ref[:, p * 128:(p + 1) * 128]
            vt_ref[p, 0] = v_ref[:, p * 128:(p + 1) * 128].T
            for h in range(2):
                own = (lane < 64) if h == 0 else (lane >= 64)
                base = 64 if h == 0 else 0
                fh = f[:, 2 * p + h:2 * p + h + 1]
                parts = []
                rest = fh
                for _ in range(N_EXTRA):
                    part = rest.astype(qh_ref.dtype)
                    parts.append(part)
                    rest = rest - part.astype(F32)
                one = jnp.ones((t, 1), qh_ref.dtype)
                eq = jnp.zeros((t, 128), qh_ref.dtype)
                ek = jnp.zeros((t, 128), qh_ref.dtype)
                for j in range(N_EXTRA):
                    eq = jnp.where(lane == base + j, parts[j], eq)
                    eq = jnp.where(lane == base + N_EXTRA + j, one, eq)
                    ek = jnp.where(lane == base + j, one, ek)
                    ek = jnp.where(lane == base + N_EXTRA + j, -parts[j], ek)
                qh_ref[2 * p + h] = jnp.where(own, qp, eq)
                kh_ref[2 * p + h] = jnp.where(own, kp, ek)

    row = pl.BlockSpec((t, WIDTH), lambda i: (i, 0))
    heads = pl.BlockSpec((HEADS, t, 128), lambda i: (0, i, 0))
    return pl.pallas_call(
        body, name="attn_prep", grid=(nb,),
        in_specs=[row, row, row, pl.BlockSpec((t, HEADS), lambda i: (i, 0)), pl.BlockSpec((1, HEADS), lambda i: (0, 0))],
        out_specs=[heads, heads, pl.BlockSpec((PAIRS, 1, 128, t), lambda i: (0, i, 0, 0))],
        out_shape=[jax.ShapeDtypeStruct((HEADS, S, 128), q.dtype), jax.ShapeDtypeStruct((HEADS, S, 128), k.dtype),
                   jax.ShapeDtypeStruct((PAIRS, nb, 128, t), v.dtype)],
        scratch_shapes=[pltpu.VMEM((1, HEADS), F32)],
        compiler_params=_params(("arbitrary",), VMEM_MID),
    )(q, k, v, flc, bf_row)


def _attn_fwd(qh, kh, vt, t):
    S = qh.shape[1]
    nb = S // t

    def body(q_ref, k_ref, vt_ref, o_ref, lse_ref, acc_s):
        qi = pl.program_id(1)
        acc_s[...] = jnp.zeros_like(acc_s)

        def step(ki, nblk, masked, carry):
            m_old, l_old = carry[:2], carry[2:]
            ks = pl.multiple_of(ki * t, t)
            rows = nblk * t
            sts = [_mm_nt(k_ref[h, pl.ds(ks, rows), :], q_ref[h]) for h in range(2)]
            if masked:
                ri = lax.broadcasted_iota(jnp.int32, (rows, t), 0)
                ci = lax.broadcasted_iota(jnp.int32, (rows, t), 1)
                sts = [jnp.where(ci >= ri - (nblk - 1) * t, st, NEG) for st in sts]
            m_new = [jnp.maximum(m_old[h], jnp.max(sts[h], axis=0, keepdims=True)) for h in range(2)]
            alpha = [jnp.exp(m_old[h] - m_new[h]) for h in range(2)]
            pts = [jnp.exp(sts[h] - m_new[h]) for h in range(2)]
            l_new = [alpha[h] * l_old[h] + jnp.sum(pts[h], axis=0, keepdims=True) for h in range(2)]
            for h in range(2):
                pv = _mm(vt_ref[ki], pts[h][:t])
                for b in range(1, nblk):
                    pv = pv + _mm(vt_ref[ki + b], pts[h][b * t:(b + 1) * t])
                acc_s[h] = alpha[h] * acc_s[h] + pv
            return (*m_new, *l_new)

        init = (jnp.full((1, t), -jnp.inf, F32),) * 2 + (jnp.zeros((1, t), F32),) * 2
        carry = lax.fori_loop(0, qi // 2, lambda j, c: step(2 * j, 2, False, c), init)
        m0, m1, l0, l1 = lax.cond(qi % 2 == 1, lambda c: step(qi - 1, 2, True, c), lambda c: step(qi, 1, True, c), carry)
        first = lax.broadcasted_iota(jnp.int32, (128, t), 0) < 64
        o_ref[...] = jnp.where(first, acc_s[0] / l0, acc_s[1] / l1).T
        lse_ref[...] = jnp.concatenate([m0 + jnp.log(l0), m1 + jnp.log(l1)], axis=0)

    return pl.pallas_call(
        body, name="attn_fwd", grid=(PAIRS, nb),
        in_specs=[pl.BlockSpec((2, t, 128), lambda p, i: (p, i, 0)), pl.BlockSpec((2, S, 128), lambda p, i: (p, 0, 0)),
                  pl.BlockSpec((None, nb, 128, t), lambda p, i: (p, 0, 0, 0))],
        out_specs=[pl.BlockSpec((t, 128), lambda p, i: (i, p)), pl.BlockSpec((None, None, 2, t), lambda p, i: (p, i, 0, 0))],
        out_shape=[jax.ShapeDtypeStruct((S, WIDTH), F32), jax.ShapeDtypeStruct((PAIRS, nb, 2, t), F32)],
        scratch_shapes=[pltpu.VMEM((2, 128, t), F32)],
        compiler_params=_params(("parallel", "parallel"), VMEM_MID),
    )(qh, kh, vt)


def _attn_bwd(qh, do, kh, v, lse4, dl4, early, t):
    S = qh.shape[1]
    nb = S // t
    ne = len(early)

    def body(q_ref, do_ref, k_ref, v_ref, lse_ref, dl_ref, *rest):
        dq_ref, dk_ref, dv_ref, dfq_ref, dfk_ref = rest[ne:ne + 5]
        kc_s, vh_s, dk_s, dv_s, dfk_s = rest[2 * ne + 5:2 * ne + 10]
        kj = pl.program_id(1)
        _hosted_exchange((pl.program_id(0) == 0) & (kj == 0), (pl.program_id(0) == PAIRS - 1) & (kj == nb - 1),
                         rest[:ne], rest[ne + 5:2 * ne + 5], *rest[2 * ne + 10:], scatter=True)
        lane = lax.broadcasted_iota(jnp.int32, (t, 128), 1)
        is_a = lane < 64

        @pl.when(kj == 0)
        def _():
            dq_ref[...] = jnp.zeros_like(dq_ref)
            dfq_ref[...] = jnp.zeros_like(dfq_ref)

        vp = v_ref[...]
        zero = jnp.zeros_like(vp)
        kc_s[0] = jnp.where(is_a, k_ref[0], zero.astype(kc_s.dtype))
        kc_s[1] = jnp.where(is_a, zero.astype(kc_s.dtype), k_ref[1])
        vh_s[0] = jnp.where(is_a, vp, zero)
        vh_s[1] = jnp.where(is_a, zero, vp)
        dk_s[...] = jnp.zeros_like(dk_s)
        dv_s[...] = jnp.zeros_like(dv_s)
        dfk_s[...] = jnp.zeros_like(dfk_s)

        def step(qi, masked):
            qs = pl.multiple_of(qi * t, t)
            dob = do_ref[pl.ds(qs, t), :]
            lse = lse_ref[qi]
            dl = dl_ref[qi]
            zq = jnp.zeros_like(dob)
            over_keys = []
            for h in range(2):
                sel = is_a if h == 0 else jnp.logical_not(is_a)
                qb = q_ref[h, pl.ds(qs, t), :]
                st = _mm_nt(k_ref[h], qb) - lse[h:h + 1, :]
                if masked:
                    ri = lax.broadcasted_iota(jnp.int32, (t, t), 0)
                    ci = lax.broadcasted_iota(jnp.int32, (t, t), 1)
                    st = jnp.where(ci >= ri, st, NEG)
                pt = jnp.exp(st)
                dv_s[...] += _mm(pt, jnp.where(sel, dob, zq))
                dpt = _mm_nt(vh_s[h], dob)
                dst = pt * (dpt - dl[h:h + 1, :])
                dfk_s[h] += jnp.sum(dst, axis=1, keepdims=True)
                over_keys.append(jnp.sum(dst, axis=0, keepdims=True))
                dk_s[...] += _mm(dst, jnp.where(sel, qb, jnp.zeros_like(qb)))
                dq_ref[pl.ds(qs, t), :] += _mm_tn(dst, kc_s[h])
            dfq_ref[qi] += jnp.concatenate(over_keys, axis=0)

        step(kj, True)

        def loop_body(qi, carry):
            step(qi, False)
            return carry

        lax.fori_loop(kj + 1, nb, loop_body, 0)
        dk_ref[...] = dk_s[...].astype(dk_ref.dtype)
        dv_ref[...] = dv_s[...].astype(dv_ref.dtype)
        dfk_ref[...] = jnp.where(lax.broadcasted_iota(jnp.int32, (t, 2), 1) == 0, dfk_s[0], dfk_s[1])

        @pl.when(kj == nb - 1)
        def _():
            dq_ref[...] = dq_ref[...] * 0.125

    blk = pl.BlockSpec((t, 128), lambda p, j: (j, p))
    res = pl.BlockSpec((S, 128), lambda p, j: (0, p))
    rows4 = pl.BlockSpec((None, nb, 2, t), lambda p, j: (p, 0, 0, 0))
    cols4 = pl.BlockSpec((None, t, 2), lambda p, j: (p, j, 0))
    return pl.pallas_call(
        body, name="attn_bwd", grid=(PAIRS, nb),
        in_specs=[pl.BlockSpec((2, S, 128), lambda p, j: (p, 0, 0)), res,
                  pl.BlockSpec((2, t, 128), lambda p, j: (p, j, 0)), blk, rows4, rows4] + [ANY] * ne,
        out_specs=[res, blk, blk, rows4, cols4] + [ANY] * ne,
        out_shape=[jax.ShapeDtypeStruct((S, WIDTH), F32), jax.ShapeDtypeStruct((S, WIDTH), MXU_DTYPE),
                   jax.ShapeDtypeStruct((S, WIDTH), MXU_DTYPE), jax.ShapeDtypeStruct((PAIRS, nb, 2, t), F32),
                   jax.ShapeDtypeStruct((PAIRS, S, 2), F32)] + [jax.ShapeDtypeStruct(e.shape, e.dtype) for e in early],
        scratch_shapes=[pltpu.VMEM((2, t, 128), kh.dtype), pltpu.VMEM((2, t, 128), v.dtype),
                        pltpu.VMEM((t, 128), F32), pltpu.VMEM((t, 128), F32), pltpu.VMEM((2, t, 1), F32)]
                       + _exchange_scratch(ne),
        compiler_params=_params(("arbitrary", "arbitrary"), VMEM_MID),
    )(qh, do, kh, v, lse4, dl4, *early)


def _s5_mats(a_re, a_im, log_dt, b_re, b_im, c_re, c_im, d_skip):
    Lc = CHUNK
    dt = jnp.exp(log_dt)[:, None]
    lr, li = a_re * dt, a_im * dt

    def apow(n):
        n = jnp.asarray(n, F32)[None, :, None]
        mag = jnp.exp(n * lr[:, None, :])
        ang = n * li[:, None, :]
        return mag * jnp.cos(ang), mag * jnp.sin(ang)

    ar, ai = apow([1.0])
    ar, ai = ar[:, 0], ai[:, 0]
    den = a_re * a_re + a_im * a_im
    nr, ni = ar - 1.0, ai
    fr = (nr * a_re + ni * a_im) / den
    fi = (ni * a_re - nr * a_im) / den
    bbr = fr[:, :, None] * b_re - fi[:, :, None] * b_im
    bbi = fr[:, :, None] * b_im + fi[:, :, None] * b_re
    steps = np.arange(Lc, dtype=np.float32)
    pr, pi = apow(steps)
    car = c_re[:, None] * pr[:, :, None, :] - c_im[:, None] * pi[:, :, None, :]
    cai = c_re[:, None] * pi[:, :, None, :] + c_im[:, None] * pr[:, :, None, :]
    kern = (jnp.einsum('glcp,gpd->glcd', car, bbr, precision=HI)
            - jnp.einsum('glcp,gpd->glcd', cai, bbi, precision=HI))
    skip = d_skip.reshape(GROUPS, CG)[:, :, None] * jnp.eye(CG, dtype=F32)[None]
    kern = kern.at[:, 0].add(skip)
    trow = kern.transpose(0, 3, 1, 2).reshape(GROUPS, CG, Lc * CG)
    p1r, p1i = apow(steps + 1.0)
    cr = c_re[:, None] * p1r[:, :, None, :] - c_im[:, None] * p1i[:, :, None, :]
    ci = c_re[:, None] * p1i[:, :, None, :] + c_im[:, None] * p1r[:, :, None, :]
    to_rows = lambda m: m.transpose(0, 3, 1, 2).reshape(GROUPS, STATE, Lc * CG)
    camat = jnp.concatenate([to_rows(cr), -to_rows(ci)], axis=1)
    qr, qi = apow(Lc - 1.0 - steps)
    zr = qr[:, :, None, :] * bbr.transpose(0, 2, 1)[:, None] - qi[:, :, None, :] * bbi.transpose(0, 2, 1)[:, None]
    zi = qr[:, :, None, :] * bbi.transpose(0, 2, 1)[:, None] + qi[:, :, None, :] * bbr.transpose(0, 2, 1)[:, None]
    bzmat = jnp.concatenate([zr, zi], axis=-1).reshape(GROUPS, Lc * CG, 2 * STATE)
    lr_, li_ = apow([float(Lc)])
    al = jnp.concatenate([lr_[:, 0], li_[:, 0]], axis=-1)
    return trow, camat, bzmat, al


def _s5_scan_powers(a_re, a_im, log_dt, n_steps):
    dt = jnp.exp(log_dt)[:, None]
    lr, li = a_re * dt, a_im * dt
    n = (CHUNK * 2.0 ** np.arange(n_steps)).astype(np.float32)[None, :, None]
    mag = jnp.exp(n * lr[:, None, :])
    pr, pi = mag * jnp.cos(n * li[:, None, :]), mag * jnp.sin(n * li[:, None, :])
    fwd = jnp.stack([jnp.concatenate([pr, pr], -1), jnp.concatenate([-pi, pi], -1)], axis=2)
    bwd = jnp.stack([jnp.concatenate([pr, pr], -1), jnp.concatenate([pi, -pi], -1)], axis=2)
    return fwd, bwd


def _shift_rows(x, sh, down):
    n = x.shape[0]
    ri = lax.broadcasted_iota(jnp.int32, x.shape, 0)
    if down:
        return jnp.where(ri >= sh, pltpu.roll(x, sh, 0), 0.0)
    return jnp.where(ri < n - sh, pltpu.roll(x, n - sh, 0), 0.0)


GPB = 128 // CG


def _lane_transpose(arrs):
    lane = lax.broadcasted_iota(jnp.int32, arrs[0].shape, 1)
    arrs = list(arrs)
    k = GPB // 2
    while k >= 1:
        hi = ((lane // CG) & k) != 0
        new = list(arrs)
        for i in range(GPB):
            if i & k:
                continue
            lo_arr, hi_arr = arrs[i], arrs[i + k]
            new[i] = jnp.where(hi, pltpu.roll(hi_arr, CG * k, 1), lo_arr)
            new[i + k] = jnp.where(hi, hi_arr, pltpu.roll(lo_arr, 128 - CG * k, 1))
        arrs = new
        k //= 2
    return arrs


def _gather_block(ref, dst, nch):
    for half in range(CHUNK // GPB):
        outs = _lane_transpose([ref[pl.ds(half * GPB + l8, nch, stride=CHUNK), :] for l8 in range(GPB)])
        for g in range(GPB):
            dst[half, g] = outs[g]


def _scatter_block(src, ref, nch):
    for half in range(CHUNK // GPB):
        outs = _lane_transpose([src[half, g] for g in range(GPB)])
        for l8 in range(GPB):
            ref[pl.ds(half * GPB + l8, nch, stride=CHUNK), :] = outs[l8]


def _toeplitz(trow):
    lane = lax.broadcasted_iota(jnp.int32, (CG, 128), 1)
    x0, x1 = trow[:, :128], trow[:, 128:]
    zero = jnp.zeros_like(x0)
    rows = []
    for s in range(CHUNK):
        sh = (CG * s) % 128
        r0 = pltpu.roll(x0, sh, 1) if sh else x0
        r1 = pltpu.roll(x1, sh, 1) if sh else x1
        if CG * s < 128:
            rows.append(jnp.concatenate([jnp.where(lane >= sh, r0, zero), jnp.where(lane >= sh, r1, r0)], axis=1))
        else:
            rows.append(jnp.concatenate([zero, jnp.where(lane >= sh, r0, zero)], axis=1))
    return jnp.concatenate(rows, axis=0)


def _toeplitz_adjoint(dt):
    lane = lax.broadcasted_iota(jnp.int32, (CG, 128), 1)
    acc0 = jnp.zeros((CG, 128), F32)
    acc1 = jnp.zeros((CG, 128), F32)
    for s in range(CHUNK):
        x0, x1 = dt[CG * s:CG * s + CG, :128], dt[CG * s:CG * s + CG, 128:]
        sh = (CG * s) % 128
        keep = 128 - sh
        r0 = pltpu.roll(x0, keep, 1) if sh else x0
        r1 = pltpu.roll(x1, keep, 1) if sh else x1
        if CG * s < 128:
            acc0 = acc0 + jnp.where(lane < keep, r0, r1)
            acc1 = acc1 + jnp.where(lane < keep, r1, 0.0)
        else:
            acc0 = acc0 + jnp.where(lane < keep, r1, 0.0)
    return jnp.concatenate([acc0, acc1], axis=1)


def _s5_fwd(u, trow, camat, bzmat, pw):
    S = u.shape[0]
    nch = S // CHUNK
    n_steps = pw.shape[1]

    def body(u_ref, t_ref, ca_ref, bz_ref, pw_ref, y_ref, xp_ref, uc_ref, ub_s, yb_s):
        g = pl.program_id(1)

        @pl.when(g == 0)
        def _():
            _gather_block(u_ref, ub_s, nch)

        uc = jnp.concatenate([ub_s[0, g], ub_s[1, g]], axis=1)
        uc_ref[...] = uc
        x = _s5mm(uc, bz_ref[...])
        for kk in range(n_steps):
            xs = _shift_rows(x, 2 ** kk, True)
            m = pw_ref[kk]
            x = x + m[0:1, :] * xs + m[1:2, :] * pltpu.roll(xs, STATE, 1)
        xp = _shift_rows(x, 1, True)
        xp_ref[...] = xp
        yc = _s5mm(uc, _toeplitz(t_ref[...])) + _s5mm(xp, ca_ref[...])
        yb_s[0, g] = yc[:, :128]
        yb_s[1, g] = yc[:, 128:]

        @pl.when(g == GPB - 1)
        def _():
            _scatter_block(yb_s, y_ref, nch)

    per = lambda a: pl.BlockSpec((None,) + a.shape[1:], lambda b, g: (b * GPB + g,) + (0,) * (a.ndim - 1))
    nat = pl.BlockSpec((S, 128), lambda b, g: (0, b))
    return pl.pallas_call(
        body, name="s5_fwd", grid=(GROUPS // GPB, GPB),
        in_specs=[nat, per(trow), per(camat), per(bzmat), per(pw)],
        out_specs=[nat, pl.BlockSpec((None, nch, 2 * STATE), lambda b, g: (b * GPB + g, 0, 0)),
                   pl.BlockSpec((None, nch, CHUNK * CG), lambda b, g: (b * GPB + g, 0, 0))],
        out_shape=[jax.ShapeDtypeStruct((S, GROUPS * CG), F32), jax.ShapeDtypeStruct((GROUPS, nch, 2 * STATE), F32),
                   jax.ShapeDtypeStruct((GROUPS, nch, CHUNK * CG), F32)],
        scratch_shapes=[pltpu.VMEM((CHUNK // GPB, GPB, nch, 128), F32)] * 2,
        compiler_params=_params(("parallel", "arbitrary"), VMEM_BIG),
    )(u, trow, camat, bzmat, pw)


def _s5_bwd(uc, dy, xp, trow, camat, bzmat, pwc):
    S = dy.shape[0]
    nch = S // CHUNK
    n_steps = pwc.shape[1]

    def body(uc_ref, dy_ref, xp_ref, t_ref, ca_ref, bz_ref, pw_ref, du_ref, dt_ref, dca_ref, dbz_ref, dal_ref,
             dyb_s, dub_s):
        g = pl.program_id(1)

        @pl.when(g == 0)
        def _():
            _gather_block(dy_ref, dyb_s, nch)

        uc = uc_ref[...]
        dyc = jnp.concatenate([dyb_s[0, g], dyb_s[1, g]], axis=1)
        xpv = xp_ref[...]
        dt_ref[...] = _toeplitz_adjoint(_s5mm_tn(uc, dyc))
        dca_ref[...] = _s5mm_tn(xpv, dyc)
        dx = _shift_rows(_s5mm_nt(dyc, ca_ref[...]), 1, False)
        for kk in range(n_steps):
            xs = _shift_rows(dx, 2 ** kk, False)
            m = pw_ref[kk]
            dx = dx + m[0:1, :] * xs + m[1:2, :] * pltpu.roll(xs, STATE, 1)
        dbz_ref[...] = _s5mm_tn(uc, dx)
        dal_ref[0:1, :] = jnp.sum(dx * xpv, axis=0, keepdims=True)
        dal_ref[1:2, :] = jnp.sum(dx * pltpu.roll(xpv, STATE, 1), axis=0, keepdims=True)
        duc = _s5mm_nt(dyc, _toeplitz(t_ref[...])) + _s5mm_nt(dx, bz_ref[...])
        dub_s[0, g] = duc[:, :128]
        dub_s[1, g] = duc[:, 128:]

        @pl.when(g == GPB - 1)
        def _():
            _scatter_block(dub_s, du_ref, nch)

    per = lambda a: pl.BlockSpec((None,) + a.shape[1:], lambda b, g: (b * GPB + g,) + (0,) * (a.ndim - 1))
    nat = pl.BlockSpec((S, 128), lambda b, g: (0, b))
    sds = jax.ShapeDtypeStruct
    mats = [sds(trow.shape, F32), sds(camat.shape, F32), sds(bzmat.shape, F32), sds((GROUPS, 2, 2 * STATE), F32)]
    return pl.pallas_call(
        body, name="s5_bwd", grid=(GROUPS // GPB, GPB),
        in_specs=[per(uc), nat, per(xp), per(trow), per(camat), per(bzmat), per(pwc)],
        out_specs=[nat] + [per(o) for o in mats], out_shape=[sds((S, GROUPS * CG), F32)] + mats,
        scratch_shapes=[pltpu.VMEM((CHUNK // GPB, GPB, nch, 128), F32)] * 2,
        compiler_params=_params(("parallel", "arbitrary"), VMEM_BIG),
    )(uc, dy, xp, trow, camat, bzmat, pwc)


GELU_C0 = math.sqrt(2.0 / math.pi)
GELU_C1 = 0.044715


def _mix(o, za, ys, zb, ga, gb, x, tgt, gate, b_glu, g_final, w_glu, w_up_a, w_up_b, w_out, hsel, ts):
    S = o.shape[0]

    def body(o_ref, za_ref, ys_ref, zb_ref, ga_ref, gb_ref, x_ref, t_ref, gate_ref, bglu_ref, gf_ref,
             wglu_ref, wua_ref, wub_ref, wout_ref, hsel_ref,
             dx2_ref, do_ref, dza_ref, dzb_ref, dga_ref, dgb_ref, dys_ref, dl_ref,
             mg_ref, dmo_ref, ya_ref, dua_ref, yb_ref, dub_ref, yg_ref, dgl_ref,
             dbglu_ref, dgate_ref, dgf_ref, loss_ref):
        @pl.when(pl.program_id(0) == 0)
        def _():
            dbglu_ref[...] = jnp.zeros_like(dbglu_ref)
            dgate_ref[...] = jnp.zeros_like(dgate_ref)
            dgf_ref[...] = jnp.zeros_like(dgf_ref)
            loss_ref[...] = jnp.zeros_like(loss_ref)

        ov = o_ref[...]
        za = za_ref[...]
        sza = _sigmoid(za)
        silu_a = za * sza
        ya = ov * silu_a
        ya_b = ya.astype(ya_ref.dtype)
        ya_ref[...] = ya_b
        ysv = ys_ref[...]
        th = jnp.tanh(GELU_C0 * (ysv + GELU_C1 * ysv * ysv * ysv))
        yg = 0.5 * ysv * (1.0 + th)
        yg_b = yg.astype(yg_ref.dtype)
        yg_ref[...] = yg_b
        sg = _sigmoid(_mm(yg_b, wglu_ref[...]) + bglu_ref[...])
        yb1 = yg * sg
        zb = zb_ref[...]
        szb = _sigmoid(zb)
        silu_b = zb * szb
        yb_b = (yb1 * silu_b).astype(yb_ref.dtype)
        yb_ref[...] = yb_b
        ua = _mm(ya_b, wua_ref[...])
        ub = _mm(yb_b, wub_ref[...])
        sa = _sigmoid(ga_ref[...])
        sb = _sigmoid(gb_ref[...])
        merged_b = (sa * ua + sb * ub).astype(mg_ref.dtype)
        mg_ref[...] = merged_b
        mo = _mm(merged_b, wout_ref[...])
        gate_v = gate_ref[...]
        x2 = x_ref[...] + gate_v * mo
        r2 = lax.rsqrt(jnp.mean(x2 * x2, axis=-1, keepdims=True) + EPS)
        x2n = x2 * r2
        gf = gf_ref[...]
        diff = x2n * gf - t_ref[...]
        loss_ref[...] += jnp.sum(jnp.sum(diff * diff, axis=-1, keepdims=True), axis=0, keepdims=True) * (0.5 / D_MODEL)
        dgf_ref[...] += jnp.sum(diff * x2n, axis=0, keepdims=True) * (1.0 / D_MODEL)
        dyg = diff * (gf * (1.0 / D_MODEL))
        dx2 = r2 * (dyg - x2n * jnp.mean(dyg * x2n, axis=-1, keepdims=True))
        dx2_ref[...] = dx2
        dgate_ref[...] += jnp.sum(dx2 * mo, axis=0, keepdims=True)
        dmo_b = (dx2 * gate_v).astype(dmo_ref.dtype)
        dmo_ref[...] = dmo_b
        dmerged = _mm_nt(dmo_b, wout_ref[...])
        dua = dmerged * sa
        dub = dmerged * sb
        dua_b = dua.astype(dua_ref.dtype)
        dub_b = dub.astype(dub_ref.dtype)
        dua_ref[...] = dua_b
        dub_ref[...] = dub_b
        dga_ref[...] = (dua * ua * (1.0 - sa)).astype(dga_ref.dtype)
        dgb_ref[...] = (dub * ub * (1.0 - sb)).astype(dgb_ref.dtype)
        dya = _mm_nt(dua_b, wua_ref[...])
        dyb = _mm_nt(dub_b, wub_ref[...])
        dov = dya * silu_a
        do_ref[...] = dov.astype(do_ref.dtype)
        dl_ref[...] = _mm32_nt(hsel_ref[...], dov * ov)
        dza_ref[...] = (dya * ov * (sza + silu_a * (1.0 - sza))).astype(dza_ref.dtype)
        dyb1 = dyb * silu_b
        dzb_ref[...] = (dyb * yb1 * (szb + silu_b * (1.0 - szb))).astype(dzb_ref.dtype)
        dgl = dyb1 * yg * sg * (1.0 - sg)
        dbglu_ref[...] += jnp.sum(dgl, axis=0, keepdims=True)
        dgl_b = dgl.astype(dgl_ref.dtype)
        dgl_ref[...] = dgl_b
        dyg2 = dyb1 * sg + _mm_nt(dgl_b, wglu_ref[...])
        dgelu = 0.5 * (1.0 + th) + 0.5 * ysv * (1.0 - th * th) * GELU_C0 * (1.0 + 3.0 * GELU_C1 * ysv * ysv)
        dys_ref[...] = dyg2 * dgelu

    row = lambda n: pl.BlockSpec((ts, n), lambda i: (i, 0))
    full = lambda a: pl.BlockSpec(a.shape, lambda i: (0,) * a.ndim)
    vec = lambda n: pl.BlockSpec((1, n), lambda i: (0, 0))
    sds = jax.ShapeDtypeStruct
    W, Dm = WIDTH, D_MODEL
    return pl.pallas_call(
        body, name="mix", grid=(S // ts,),
        in_specs=[row(W), row(W), row(W), row(W), row(Dm), row(Dm), row(Dm), row(Dm),
                  full(gate), full(b_glu), full(g_final), full(w_glu), full(w_up_a), full(w_up_b), full(w_out), full(hsel)],
        out_specs=[row(Dm), row(W), row(W), row(W), row(Dm), row(Dm), row(W), pl.BlockSpec((HEADS, ts), lambda i: (0, i)),
                   row(Dm), row(Dm), row(W), row(Dm), row(W), row(Dm), row(W), row(W),
                   vec(W), vec(Dm), vec(Dm), vec(1)],
        out_shape=[sds((S, Dm), F32), sds((S, W), MXU_DTYPE), sds((S, W), MXU_DTYPE), sds((S, W), MXU_DTYPE),
                   sds((S, Dm), MXU_DTYPE), sds((S, Dm), MXU_DTYPE), sds((S, W), F32), sds((HEADS, S), F32),
                   sds((S, Dm), MXU_DTYPE), sds((S, Dm), MXU_DTYPE), sds((S, W), MXU_DTYPE), sds((S, Dm), MXU_DTYPE),
                   sds((S, W), MXU_DTYPE), sds((S, Dm), MXU_DTYPE), sds((S, W), MXU_DTYPE), sds((S, W), MXU_DTYPE),
                   sds((1, W), F32), sds((1, Dm), F32), sds((1, Dm), F32), sds((1, 1), F32)],
        compiler_params=_params(("arbitrary",), VMEM_BIG),
    )(o, za, ys, zb, ga, gb, x, tgt, gate, b_glu, g_final, w_glu, w_up_a, w_up_b, w_out, hsel)


def _matmul_tn(name, a, b, ts):
    S, M = a.shape
    N = b.shape[1]
    tn = min(N, 1024)

    def body(a_ref, b_ref, o_ref):
        @pl.when(pl.program_id(1) == 0)
        def _():
            o_ref[...] = jnp.zeros_like(o_ref)

        o_ref[...] += _mm_tn(a_ref[...], b_ref[...])

    return pl.pallas_call(
        body, name=name, grid=(N // tn, S // ts),
        in_specs=[pl.BlockSpec((ts, M), lambda j, i: (i, 0)), pl.BlockSpec((ts, tn), lambda j, i: (i, j))],
        out_specs=pl.BlockSpec((M, tn), lambda j, i: (0, j)),
        out_shape=jax.ShapeDtypeStruct((M, N), F32),
        compiler_params=_params(("parallel", "arbitrary"), VMEM_MID),
    )(a, b)


def _proj_bwd(dq, dk, dv, dza, du, dzb, dga, dgb, dfl, x, dx2, shift, scale, g_norm, w_main, w_ft, chip_planes,
              gathers, ts):
    S = x.shape[0]
    nc, ng = len(chip_planes), len(gathers)
    nx = nc + ng

    def body(dq_ref, dk_ref, dv_ref, dza_ref, du_ref, dzb_ref, dga_ref, dgb_ref, dfl_ref, x_ref, dx2_ref,
             sc_ref, gn_ref, w_ref, wft_ref, *rest):
        gx_ref, dsh_ref, dsc_ref, dgn_ref = rest[nx:nx + 4]
        outs, sems = rest[nx + 4:2 * nx + 4], rest[2 * nx + 4:]
        i = pl.program_id(0)
        first, last = i == 0, i == pl.num_programs(0) - 1
        _hosted_chip_exchange(first, last, rest[:nc], outs[:nc], *sems[:3])
        _hosted_exchange(first, last, rest[nc:nx], outs[nc:], *sems[3:], scatter=False)

        @pl.when(pl.program_id(0) == 0)
        def _():
            dsh_ref[...] = jnp.zeros_like(dsh_ref)
            dsc_ref[...] = jnp.zeros_like(dsc_ref)
            dgn_ref[...] = jnp.zeros_like(dgn_ref)

        def seg(ref, off, n):
            return _mm_nt(ref[...], w_ref[:, off:off + n])

        dh = (seg(dq_ref, M_Q, WIDTH) + seg(dk_ref, M_K, WIDTH) + seg(dv_ref, M_V, WIDTH)
              + seg(dza_ref, M_ZA, WIDTH) + seg(du_ref, M_U, WIDTH) + seg(dzb_ref, M_ZB, WIDTH)
              + seg(dga_ref, M_GA, D_MODEL) + seg(dgb_ref, M_GB, D_MODEL)
              + _mm32(dfl_ref[...], wft_ref[...]))
        xv = x_ref[...]
        r = lax.rsqrt(jnp.mean(xv * xv, axis=-1, keepdims=True) + EPS)
        xn = xv * r
        gn = gn_ref[...]
        s1 = 1.0 + sc_ref[...]
        dsh_ref[...] += jnp.sum(dh, axis=0, keepdims=True)
        dhx = dh * xn
        dsc_ref[...] += jnp.sum(dhx, axis=0, keepdims=True) * gn
        dgn_ref[...] += jnp.sum(dhx, axis=0, keepdims=True) * s1
        dxn = dh * (gn * s1)
        gx_ref[...] = dx2_ref[...] + r * (dxn - xn * jnp.mean(dxn * xn, axis=-1, keepdims=True))

    row = lambda n: pl.BlockSpec((ts, n), lambda i: (i, 0))
    full = lambda a: pl.BlockSpec(a.shape, lambda i: (0,) * a.ndim)
    vec = pl.BlockSpec((1, D_MODEL), lambda i: (0, 0))
    W, Dm = WIDTH, D_MODEL
    del shift
    return pl.pallas_call(
        body, name="proj_bwd", grid=(S // ts,),
        in_specs=[row(W)] * 6 + [row(Dm)] * 2 + [row(HEADS), row(Dm), row(Dm),
                                                 full(scale), full(g_norm), full(w_main), full(w_ft)] + [ANY] * nx,
        out_specs=[row(Dm), vec, vec, vec] + [ANY] * nx,
        out_shape=[jax.ShapeDtypeStruct((S, Dm), F32)] + [jax.ShapeDtypeStruct((1, Dm), F32)] * 3
                  + [jax.ShapeDtypeStruct(p.shape, p.dtype) for p in chip_planes]
                  + [jax.ShapeDtypeStruct((N_DEV,) + g.shape, g.dtype) for g in gathers],
        scratch_shapes=[pltpu.SemaphoreType.DMA((nc, N_CHIP)), pltpu.SemaphoreType.DMA((nc, N_CHIP)),
                        pltpu.SemaphoreType.DMA((nc,))] + _exchange_scratch(ng),
        compiler_params=_params(("arbitrary",), VMEM_BIG),
    )(dq, dk, dv, dza, du, dzb, dga, dgb, dfl, x, dx2, scale, g_norm, w_main, w_ft, *chip_planes, *gathers)


def _adamw(name, planes, w, m, v, tr):
    n, R, C = planes.shape
    bc1 = 1.0 - ADAM_B1 ** ADAM_STEP
    bc2 = 1.0 - ADAM_B2 ** ADAM_STEP

    def body(p_ref, w_ref, m_ref, v_ref, g_ref, d_ref, nm_ref, nv_ref):
        g = p_ref[0].astype(F32)
        for i in range(1, n):
            g = g + p_ref[i].astype(F32)
        g_ref[...] = g
        nm = ADAM_B1 * m_ref[...] + (1.0 - ADAM_B1) * g
        nv = ADAM_B2 * v_ref[...] + (1.0 - ADAM_B2) * (g * g)
        nm_ref[...] = nm
        nv_ref[...] = nv
        d_ref[...] = -ADAM_LR * ((nm / bc1) / (jnp.sqrt(nv / bc2) + ADAM_EPS) + ADAM_WD * w_ref[...])

    blk = pl.BlockSpec((tr, C), lambda i: (i, 0))
    return pl.pallas_call(
        body, name=name, grid=(R // tr,),
        in_specs=[pl.BlockSpec((n, tr, C), lambda i: (0, i, 0)), blk, blk, blk],
        out_specs=[blk] * 4, out_shape=[jax.ShapeDtypeStruct((R, C), F32)] * 4,
        compiler_params=_params(("parallel",), VMEM_MID),
    )(planes, w, m, v)


def _wada_grad(c_all, dmod_cols):
    def body(c_ref, d_ref, o_ref):
        o_ref[0] = _mm32_tn(c_ref[...], d_ref[...])

    return pl.pallas_call(
        body, name="wada_grad",
        out_shape=jax.ShapeDtypeStruct((1, c_all.shape[1], dmod_cols.shape[1]), F32),
        in_specs=[VMEM, VMEM], out_specs=VMEM,
    )(c_all, dmod_cols)


SMALL_ORDER = ("b_ada", "g_norm", "b_f", "a_re", "a_im", "log_dt", "b_re", "b_im", "c_re", "c_im",
               "d_skip", "b_glu", "g_final")
BIG_ORDER = ("w_ada", "w_in", "w_glu", "w_up_a", "w_up_b", "w_out")
ALL_ORDER = ("w_ada", "b_ada", "g_norm", "w_in", "b_f", "a_re", "a_im", "log_dt", "b_re", "b_im", "c_re", "c_im",
             "d_skip", "w_glu", "b_glu", "w_up_a", "w_up_b", "w_out", "g_final")


def _pack_small(parts, rows):
    flat = jnp.concatenate([p.reshape(-1).astype(F32) for p in parts])
    return jnp.pad(flat, (0, rows * 128 - flat.shape[0])).reshape(rows, 128)


def kernel(x, c, w_ada, b_ada, g_norm, w_in, b_f, a_re, a_im, log_dt, b_re, b_im, c_re, c_im, d_skip, w_glu, b_glu, w_up_a, w_up_b, w_out, g_final, loss_target, m_w_ada, m_b_ada, m_g_norm, m_w_in, m_b_f, m_a_re, m_a_im, m_log_dt, m_b_re, m_b_im, m_c_re, m_c_im, m_d_skip, m_w_glu, m_b_glu, m_w_up_a, m_w_up_b, m_w_out, m_g_final, v_w_ada, v_b_ada, v_g_norm, v_w_in, v_b_f, v_a_re, v_a_im, v_log_dt, v_b_re, v_b_im, v_c_re, v_c_im, v_d_skip, v_w_glu, v_b_glu, v_w_up_a, v_w_up_b, v_w_out, v_g_final):
    weights = dict(w_ada=w_ada, b_ada=b_ada, g_norm=g_norm, w_in=w_in, b_f=b_f, a_re=a_re, a_im=a_im, log_dt=log_dt,
                   b_re=b_re, b_im=b_im, c_re=c_re, c_im=c_im, d_skip=d_skip, w_glu=w_glu, b_glu=b_glu,
                   w_up_a=w_up_a, w_up_b=w_up_b, w_out=w_out, g_final=g_final)
    mom_m = dict(w_ada=m_w_ada, b_ada=m_b_ada, g_norm=m_g_norm, w_in=m_w_in, b_f=m_b_f, a_re=m_a_re, a_im=m_a_im,
                 log_dt=m_log_dt, b_re=m_b_re, b_im=m_b_im, c_re=m_c_re, c_im=m_c_im, d_skip=m_d_skip, w_glu=m_w_glu,
                 b_glu=m_b_glu, w_up_a=m_w_up_a, w_up_b=m_w_up_b, w_out=m_w_out, g_final=m_g_final)
    mom_v = dict(w_ada=v_w_ada, b_ada=v_b_ada, g_norm=v_g_norm, w_in=v_w_in, b_f=v_b_f, a_re=v_a_re, a_im=v_a_im,
                 log_dt=v_log_dt, b_re=v_b_re, b_im=v_b_im, c_re=v_c_re, c_im=v_c_im, d_skip=v_d_skip, w_glu=v_w_glu,
                 b_glu=v_b_glu, w_up_a=v_w_up_a, w_up_b=v_w_up_b, w_out=v_w_out, g_final=v_g_final)
    xs = x[0]
    tgt = loss_target[0]
    S = xs.shape[0]
    ts = min(256, S)
    ta = min(512, S)
    tw = min(2048, S)
    nch = S // CHUNK
    n_steps = max(1, int(math.ceil(math.log2(nch))))
    me = _my_index()

    shards = [w.astype(MXU_DTYPE) for w in (w_in[0], w_glu[0], w_up_a[0], w_up_b[0], w_out[0])]
    mod8, c_all, gathered = _comm_in(c, w_ada[0], b_ada.reshape(N_DEV, -1), shards[:1])
    mod = mod8.reshape(1, 3 * D_MODEL)
    shift, scale, gate = mod[:, :D_MODEL], mod[:, D_MODEL:2 * D_MODEL], mod[:, 2 * D_MODEL:]
    w_in_full = gathered[0].transpose(1, 0, 2).reshape(D_MODEL, PROJ_WIDTH)
    w_main = jnp.concatenate([w_in_full[:, :OFF_F], w_in_full[:, OFF_F + HEADS:]], axis=1)
    w_f = w_in_full[:, OFF_F:OFF_F + HEADS].astype(F32)
    w_ft = w_f.T

    q, k, v, za, u, zb, ga, gb, flc, hb, *late = _proj_fwd(xs, shift, scale, g_norm, w_main, w_f, shards[1:], ts)
    w_glu_full = late[0].reshape(WIDTH, WIDTH)
    w_up_a_full = late[1].transpose(1, 0, 2).reshape(WIDTH, D_MODEL)
    w_up_b_full = late[2].transpose(1, 0, 2).reshape(WIDTH, D_MODEL)
    w_out_full = late[3].reshape(D_MODEL, D_MODEL)
    nb = S // ta
    rows4 = lambda r: r.reshape(PAIRS, 2, nb, ta).transpose(0, 2, 1, 3)
    qh, kh, vt = _attn_prep(q, k, v, flc, b_f, ta)
    o, lse4 = _attn_fwd(qh, kh, vt, ta)

    s5_params = (a_re[0], a_im[0], log_dt[0], b_re[0], b_im[0], c_re[0], c_im[0], d_skip[0])
    (trow, camat, bzmat, al), mats_vjp = jax.vjp(_s5_mats, *s5_params)
    del al
    pw_f, pw_b = _s5_scan_powers(a_re[0], a_im[0], log_dt[0], n_steps)
    ys, xprev, uc = _s5_fwd(u, trow, camat, bzmat, pw_f)

    hsel = (np.arange(WIDTH)[None, :] // 64 == np.arange(HEADS)[:, None]).astype(np.float32)
    (dx2, do, dza, dzb, dga, dgb, dys, dl_row, merged, dmo, ya, dua, yb, dub, yg, dgl,
     db_glu, dgate, dg_final, loss_part) = _mix(o, za, ys, zb, ga, gb, xs, tgt, gate, b_glu, g_final.reshape(1, -1),
                                                w_glu_full, w_up_a_full, w_up_b_full, w_out_full, jnp.asarray(hsel), ts)

    gw_out = _matmul_tn("dw_out", merged, dmo, tw)
    gw_up_a = _matmul_tn("dw_up_a", ya, dua, tw)
    gw_up_b = _matmul_tn("dw_up_b", yb, dub, tw)
    gw_glu = _matmul_tn("dw_glu", yg, dgl, tw)

    du, d_trow, d_camat, d_bzmat, dal2 = _s5_bwd(uc, dys, xprev, trow, camat, bzmat, pw_b)
    d_al = jnp.concatenate([dal2[:, 0, :STATE] + dal2[:, 0, STATE:], dal2[:, 1, STATE:] - dal2[:, 1, :STATE]], axis=-1)
    gs5 = mats_vjp((d_trow, d_camat, d_bzmat, d_al))

    dl4 = rows4(dl_row)
    early = [p.astype(MXU_DTYPE) for p in (gw_glu.reshape(N_DEV, -1, WIDTH),
                                           gw_up_a.reshape(WIDTH, N_DEV, -1).transpose(1, 0, 2),
                                           gw_up_b.reshape(WIDTH, N_DEV, -1).transpose(1, 0, 2),
                                           gw_out.reshape(N_DEV, -1, D_MODEL))]
    dq, dk, dv, dfq4, dfk4, *early_recv = _attn_bwd(qh, do, kh, v, lse4, dl4, early, ta)
    d_fcol = dfq4.transpose(0, 2, 1, 3).reshape(HEADS, S).T - dfk4.transpose(1, 0, 2).reshape(S, HEADS)
    dfl, db_f = _fgate_bwd(d_fcol, flc, b_f, ta)

    segs = [("dw_q", dq), ("dw_k", dk), ("dw_v", dv), ("dw_f", dfl), ("dw_za", dza), ("dw_u", du), ("dw_zb", dzb),
            ("dw_ga", dga), ("dw_gb", dgb)]
    gw_in = jnp.concatenate([_matmul_tn(nm, hb, d, tw) for nm, d in segs], axis=1)

    planes = [gw_in.reshape(D_MODEL, N_DEV, -1).transpose(1, 0, 2).astype(MXU_DTYPE)]
    from_sib = _comm_pair(planes)
    core = lax.axis_index("c")
    chip_planes = []
    for name, p, s in zip(("w_in",), planes, from_sib):
        tr = 256 if s.shape[1] % 256 == 0 else s.shape[1]
        chip_planes.append(_pair_sum("pair_sum_" + name, p, s, core, tr))
    ready_parts = [db_f, gs5[0], gs5[1], gs5[2], gs5[3], gs5[4], gs5[5], gs5[6], gs5[7], db_glu, dg_final, loss_part]
    n_ready = sum(int(np.prod(p.shape)) for p in ready_parts)
    rows_ready = -(-n_ready // (8 * 128)) * 8
    grad_x, dshift, dscale, dg_norm, recv, ready_all = _proj_bwd(
        dq, dk, dv, dza, du, dzb, dga, dgb, dfl, xs, dx2, shift, scale, g_norm, w_main, w_ft, chip_planes,
        [_pack_small(ready_parts, rows_ready)], ts)
    recv = [recv]

    dmod = jnp.concatenate([dshift, dscale, dgate], axis=1)
    rows_late = 4 * D_MODEL // 128
    late_all = _comm_small(_pack_small([dmod, dg_norm], rows_late))
    small_all = jnp.concatenate([late_all, ready_all], axis=1)
    rows = rows_late + rows_ready

    grads, deltas, new_m, new_v = {}, {}, {}, {}

    def put(name, res, shape):
        grads[name], deltas[name], new_m[name], new_v[name] = [r.reshape(shape) for r in res]

    names = ("w_in", "w_glu", "w_up_a", "w_up_b", "w_out")
    for name, pr in zip(names, recv + early_recv):
        w2 = weights[name][0]
        tr = 256 if w2.shape[0] % 256 == 0 else w2.shape[0]
        put(name, _adamw("adamw_" + name, pr, w2, mom_m[name][0], mom_v[name][0], tr), weights[name].shape)
    cols = w_ada.shape[2]
    dmod_all = small_all[:, :24, :].reshape(N_DEV, 3 * D_MODEL)
    dmod_cols = lax.dynamic_slice_in_dim(dmod_all, me * cols, cols, axis=1)
    g_wada = _wada_grad(c_all, dmod_cols)
    put("w_ada", _adamw("adamw_w_ada", g_wada, w_ada[0], m_w_ada[0], v_w_ada[0], 256), w_ada.shape)
    pack = lambda d: _pack_small([d[n] for n in SMALL_ORDER] + [jnp.zeros((1,), F32)], rows)
    res_small = _adamw("adamw_small", small_all, pack(weights), pack(mom_m), pack(mom_v), rows)
    flat = [r.reshape(-1) for r in res_small]
    off = 0
    for name in SMALL_ORDER:
        shape = weights[name].shape
        size = int(np.prod(shape))
        put(name, [f[off:off + size] for f in flat], shape)
        off += size
    loss = flat[0][off]

    return (loss, grad_x[None], *[grads[n] for n in ALL_ORDER], *[deltas[n] for n in ALL_ORDER],
            *[new_m[n] for n in ALL_ORDER], *[new_v[n] for n in ALL_ORDER])
```

```python
import math

import jax
import jax.numpy as jnp
import numpy as np
from jax import lax
from jax.experimental import pallas as pl
from jax.experimental.pallas import tpu as pltpu

F32 = jnp.float32
MXU_DTYPE = jnp.bfloat16
HI = lax.Precision.HIGHEST

N_DEV = 8
D_MODEL = 1024
WIDTH = 512
HEADS = 8
PAIRS = HEADS // 2
GROUPS = 32
STATE = 64
CG = 16
CHUNK = 16
EPS = 1e-6
NEG = float(np.finfo(np.float32).min)

ADAM_LR = 0.001
ADAM_B1 = 0.9
ADAM_B2 = 0.999
ADAM_EPS = 1e-08
ADAM_WD = 0.01
ADAM_STEP = 10

VMEM_BIG = 56 * 1024 * 1024
VMEM_MID = 40 * 1024 * 1024

OFF_F = 3 * WIDTH
PROJ_WIDTH = 5128
M_Q, M_K, M_V, M_ZA, M_U, M_ZB, M_GA, M_GB = 0, 512, 1024, 1536, 2048, 2560, 3072, 4096


def _mm(a, b):
    return jnp.dot(a.astype(MXU_DTYPE), b.astype(MXU_DTYPE), preferred_element_type=F32)


def _mm_nt(a, b):
    return lax.dot_general(a.astype(MXU_DTYPE), b.astype(MXU_DTYPE), (((1,), (1,)), ((), ())),
                           preferred_element_type=F32)


def _mm_tn(a, b):
    return lax.dot_general(a.astype(MXU_DTYPE), b.astype(MXU_DTYPE), (((0,), (0,)), ((), ())),
                           preferred_element_type=F32)


def _mm32(a, b):
    return jnp.dot(a, b, precision=HI, preferred_element_type=F32)


def _mm32_nt(a, b):
    return lax.dot_general(a, b, (((1,), (1,)), ((), ())), precision=HI, preferred_element_type=F32)


def _mm32_tn(a, b):
    return lax.dot_general(a, b, (((0,), (0,)), ((), ())), precision=HI, preferred_element_type=F32)


S5_PRECISION = lax.Precision.HIGH


def _s5mm(a, b):
    return jnp.dot(a, b, precision=S5_PRECISION, preferred_element_type=F32)


def _s5mm_nt(a, b):
    return lax.dot_general(a, b, (((1,), (1,)), ((), ())), precision=S5_PRECISION, preferred_element_type=F32)


def _s5mm_tn(a, b):
    return lax.dot_general(a, b, (((0,), (0,)), ((), ())), precision=S5_PRECISION, preferred_element_type=F32)


def _sigmoid(x):
    return 1.0 / (1.0 + jnp.exp(-x))


def _params(sem=None, vmem=None):
    kw = {}
    if sem is not None:
        kw["dimension_semantics"] = sem
    if vmem is not None:
        kw["vmem_limit_bytes"] = vmem
    return pltpu.CompilerParams(**kw)


def _my_index():
    return 4 * lax.axis_index("x") + 2 * lax.axis_index("y") + lax.axis_index("c")


def _dev(p):
    return (p // 4, (p // 2) % 2, p % 2)


ANY = pl.BlockSpec(memory_space=pl.ANY)
VMEM = pl.BlockSpec(memory_space=pltpu.VMEM)
MESH = pl.DeviceIdType.MESH


def _comm_in(c, w_ada, b_ada8, shards):
    n = len(shards)
    cols = w_ada.shape[1]

    def body(c_ref, wada_ref, bada_ref, *rest):
        srcs = rest[:n]
        mod_ref, call_ref = rest[n], rest[n + 1]
        dsts = rest[n + 2:2 * n + 2]
        modp, wsend, wrecv, wloc, csend, crecv, msend, mrecv = rest[2 * n + 2:]
        me = _my_index()

        x, y, cc = lax.axis_index("x"), lax.axis_index("y"), lax.axis_index("c")
        here, sib = (x, y, cc), (x, y, 1 - cc)
        xn, yn, dg = (1 - x, y, cc), (x, 1 - y, cc), (1 - x, 1 - y, cc)
        half = srcs[0].shape[0] // 2
        parts = [(0, pl.ds(0, half)), (0, pl.ds(half, half))] + [(a, None) for a in range(1, n)]
        via_y = [i % 2 == 0 for i in range(len(parts))]

        def wcopy(i, k, block, to, own=False):
            a, rs = parts[i]
            dst = dsts[a].at[4 * block[0] + 2 * block[1] + block[2]]
            src = srcs[a] if own else dst
            if rs is not None:
                src, dst = src.at[rs], dst.at[rs]
            return pltpu.make_async_remote_copy(src_ref=src, dst_ref=dst,
                                                send_sem=wsend.at[i, k], recv_sem=wrecv.at[i, k],
                                                device_id=to, device_id_type=MESH)

        def ccopy(src_dev, d, to):
            return pltpu.make_async_remote_copy(src_ref=c_ref, dst_ref=call_ref.at[pl.ds(src_dev, 1)],
                                                send_sem=csend.at[d], recv_sem=crecv.at[src_dev],
                                                device_id=_dev(to), device_id_type=MESH)

        def mcopy(src_dev, d, to):
            return pltpu.make_async_remote_copy(src_ref=modp.at[pl.ds(to, 1)], dst_ref=mod_ref.at[pl.ds(src_dev, 1)],
                                                send_sem=msend.at[d], recv_sem=mrecv.at[src_dev],
                                                device_id=_dev(to), device_id_type=MESH)

        local = [pltpu.make_async_copy(srcs[a], dsts[a].at[me], wloc.at[a]) for a in range(n)]
        for cp in local:
            cp.start()
        peers = [(me + d) % N_DEV for d in range(1, N_DEV)]
        sends = []
        for i in range(len(parts)):
            sends += [wcopy(i, 0, here, sib, own=True), wcopy(i, 1, here, xn, own=True), wcopy(i, 2, here, yn, own=True)]
        for cp in sends:
            cp.start()
        call_ref[pl.ds(me, 1), :] = c_ref[...]
        for d, p in enumerate(peers):
            ccopy(me, d, p).start()
        for d, p in enumerate(peers):
            ccopy(p, d, p).wait_recv()
        modp[...] = _mm32(call_ref[...], wada_ref[...]) + bada_ref[pl.ds(me, 1), :]
        mod_ref[pl.ds(me, 1), :] = modp[pl.ds(me, 1), :]
        for d, p in enumerate(peers):
            mcopy(me, d, p).start()
        for d, p in enumerate(peers):
            mcopy(p, d, p).wait_recv()
        def after(i, k, block, nxt):
            wcopy(i, k, block, here).wait_recv()
            for kk, to in nxt:
                cp = wcopy(i, kk, block, to)
                cp.start()
                sends.append(cp)

        for i in range(len(parts)):
            after(i, 1, xn, [(3, sib)] + ([(5, yn)] if via_y[i] else []))
        for i in range(len(parts)):
            after(i, 2, yn, [(4, sib)] + ([] if via_y[i] else [(6, xn)]))
        for i in range(len(parts)):
            after(i, 5 if via_y[i] else 6, dg, [(7, sib)])
        for i in range(len(parts)):
            wcopy(i, 0, sib, here).wait_recv()
            for k, block in ((3, xn), (4, yn), (7, dg)):
                wcopy(i, k, (block[0], block[1], 1 - cc), here).wait_recv()
        for cp in sends:
            cp.wait_send()
        for d, p in enumerate(peers):
            ccopy(me, d, p).wait_send()
            mcopy(me, d, p).wait_send()
        for cp in local:
            cp.wait()

    out_shape = ([jax.ShapeDtypeStruct((N_DEV, cols), F32), jax.ShapeDtypeStruct((N_DEV, D_MODEL), F32)]
                 + [jax.ShapeDtypeStruct((N_DEV,) + s.shape, s.dtype) for s in shards])
    res = pl.pallas_call(
        body, name="comm_in", out_shape=out_shape,
        in_specs=[VMEM, VMEM, VMEM] + [ANY] * n,
        out_specs=[VMEM, VMEM] + [ANY] * n,
        scratch_shapes=[pltpu.VMEM((N_DEV, cols), F32),
                        pltpu.SemaphoreType.DMA((n + 1, N_DEV)), pltpu.SemaphoreType.DMA((n + 1, N_DEV)),
                        pltpu.SemaphoreType.DMA((n,)),
                        pltpu.SemaphoreType.DMA((N_DEV,)), pltpu.SemaphoreType.DMA((N_DEV,)),
                        pltpu.SemaphoreType.DMA((N_DEV,)), pltpu.SemaphoreType.DMA((N_DEV,))],
        compiler_params=_params(vmem=VMEM_MID),
    )(c, w_ada, b_ada8, *shards)
    return res[0], res[1], list(res[2:])


def _direct_copy(srcs, dsts, send, recv, scatter, a, d, receiving):
    me = _my_index()
    p = (me + d) % N_DEV
    slot = p if receiving else me
    return pltpu.make_async_remote_copy(src_ref=srcs[a].at[p] if scatter else srcs[a], dst_ref=dsts[a].at[slot],
                                        send_sem=send.at[a, d], recv_sem=recv.at[a, slot],
                                        device_id=_dev(p), device_id_type=MESH)


def _hosted_exchange(first, last, srcs, dsts, send, recv, loc, scatter):
    me = _my_index()
    n = len(srcs)
    local = lambda a: pltpu.make_async_copy(srcs[a].at[me] if scatter else srcs[a], dsts[a].at[me], loc.at[a])
    pairs = [(a, d) for d in range(1, N_DEV) for a in range(n)]

    @pl.when(first)
    def _():
        for a in range(n):
            local(a).start()
        for a, d in pairs:
            _direct_copy(srcs, dsts, send, recv, scatter, a, d, False).start()

    @pl.when(last)
    def _():
        for a, d in pairs:
            _direct_copy(srcs, dsts, send, recv, scatter, a, d, True).wait_recv()
            _direct_copy(srcs, dsts, send, recv, scatter, a, d, False).wait_send()
        for a in range(n):
            local(a).wait()


def _exchange_scratch(n):
    return [pltpu.SemaphoreType.DMA((n, N_DEV)), pltpu.SemaphoreType.DMA((n, N_DEV)), pltpu.SemaphoreType.DMA((n,))]


def _hosted_chip_exchange(first, last, srcs, dsts, send, recv, loc):
    x, y, cc = lax.axis_index("x"), lax.axis_index("y"), lax.axis_index("c")
    mine = 2 * x + y
    chips = [(1 - x, y), (x, 1 - y), (1 - x, 1 - y)]
    n = len(srcs)

    def copy(a, j, sending):
        chip = chips[j]
        there = 2 * chip[0] + chip[1]
        return pltpu.make_async_remote_copy(src_ref=srcs[a].at[there], dst_ref=dsts[a].at[mine if sending else there],
                                            send_sem=send.at[a, j], recv_sem=recv.at[a, j],
                                            device_id=(*chip, cc), device_id_type=MESH)

    local = lambda a: pltpu.make_async_copy(srcs[a].at[mine], dsts[a].at[mine], loc.at[a])

    @pl.when(first)
    def _():
        for a in range(n):
            local(a).start()
            for j in range(len(chips)):
                copy(a, j, True).start()

    @pl.when(last)
    def _():
        for a in range(n):
            for j in range(len(chips)):
                copy(a, j, False).wait_recv()
                copy(a, j, True).wait_send()
            local(a).wait()


N_CHIP = 4


def _comm_pair(planes):
    n = len(planes)

    def body(*rest):
        srcs, dsts = rest[:n], rest[n:2 * n]
        send, recv = rest[2 * n:]
        x, y, cc = lax.axis_index("x"), lax.axis_index("y"), lax.axis_index("c")
        copies = [pltpu.make_async_remote_copy(src_ref=srcs[a].at[2 * ch + 1 - cc], dst_ref=dsts[a].at[ch],
                                               send_sem=send.at[a, ch], recv_sem=recv.at[a, ch],
                                               device_id=(x, y, 1 - cc), device_id_type=MESH)
                  for a in range(n) for ch in range(N_CHIP)]
        for cp in copies:
            cp.start()
        for cp in copies:
            cp.wait()

    out_shape = [jax.ShapeDtypeStruct((N_CHIP,) + p.shape[1:], p.dtype) for p in planes]
    return pl.pallas_call(
        body, name="comm_pair", out_shape=out_shape, in_specs=[ANY] * n, out_specs=[ANY] * n,
        scratch_shapes=[pltpu.SemaphoreType.DMA((n, N_CHIP)), pltpu.SemaphoreType.DMA((n, N_CHIP))],
    )(*planes)


def _pair_sum(name, planes, from_sib, core, tr):
    _, R, C = from_sib.shape

    def body(core_ref, a_ref, b_ref, o_ref):
        del core_ref
        o_ref[...] = (a_ref[...].astype(F32) + b_ref[...].astype(F32)).astype(o_ref.dtype)

    blk = pl.BlockSpec((None, tr, C), lambda i, j, c: (i, j, 0))
    grid_spec = pltpu.PrefetchScalarGridSpec(
        num_scalar_prefetch=1, grid=(N_CHIP, R // tr),
        in_specs=[pl.BlockSpec((None, tr, C), lambda i, j, c: (2 * i + c[0], j, 0)), blk], out_specs=blk)
    return pl.pallas_call(
        body, name=name, grid_spec=grid_spec, out_shape=jax.ShapeDtypeStruct(from_sib.shape, from_sib.dtype),
        compiler_params=_params(("parallel", "parallel"), VMEM_MID),
    )(core.reshape(1).astype(jnp.int32), planes, from_sib)


def _comm_small(small):
    rows = small.shape[0]
    cut = (rows // 16) * 8

    def body(small_ref, sall_ref, ssend, srecv, sloc):
        me = _my_index()
        x, y, cc = lax.axis_index("x"), lax.axis_index("y"), lax.axis_index("c")
        here, sib = (x, y, cc), (x, y, 1 - cc)
        xn, yn, dg = (1 - x, y, cc), (x, 1 - y, cc), (1 - x, 1 - y, cc)
        sparts = [pl.ds(0, cut), pl.ds(cut, rows - cut)]
        via_y = [True, False]

        def scopy(i, k, block, to, own=False):
            dst = sall_ref.at[4 * block[0] + 2 * block[1] + block[2]].at[sparts[i]]
            src = small_ref.at[sparts[i]] if own else dst
            return pltpu.make_async_remote_copy(src_ref=src, dst_ref=dst, send_sem=ssend.at[i, k], recv_sem=srecv.at[i, k],
                                                device_id=to, device_id_type=MESH)

        local = pltpu.make_async_copy(small_ref, sall_ref.at[me], sloc)
        local.start()
        sends = []
        for i in range(len(sparts)):
            sends += [scopy(i, 0, here, sib, own=True), scopy(i, 1, here, xn, own=True), scopy(i, 2, here, yn, own=True)]
        for cp in sends:
            cp.start()

        def after(i, k, block, nxt):
            scopy(i, k, block, here).wait_recv()
            for kk, to in nxt:
                cp = scopy(i, kk, block, to)
                cp.start()
                sends.append(cp)

        for i in range(len(sparts)):
            after(i, 1, xn, [(3, sib)] + ([(5, yn)] if via_y[i] else []))
        for i in range(len(sparts)):
            after(i, 2, yn, [(4, sib)] + ([] if via_y[i] else [(6, xn)]))
        for i in range(len(sparts)):
            after(i, 5 if via_y[i] else 6, dg, [(7, sib)])
        for i in range(len(sparts)):
            scopy(i, 0, sib, here).wait_recv()
            for k, block in ((3, xn), (4, yn), (7, dg)):
                scopy(i, k, (block[0], block[1], 1 - cc), here).wait_recv()
        for cp in sends:
            cp.wait_send()
        local.wait()

    return pl.pallas_call(
        body, name="comm_small", out_shape=jax.ShapeDtypeStruct((N_DEV,) + small.shape, small.dtype),
        in_specs=[ANY], out_specs=ANY,
        scratch_shapes=[pltpu.SemaphoreType.DMA((2, N_DEV)), pltpu.SemaphoreType.DMA((2, N_DEV)),
                        pltpu.SemaphoreType.DMA(())],
    )(small)


def _proj_fwd(x, shift, scale, g_norm, w_main, w_f, late, ts):
    S = x.shape[0]
    nl = len(late)

    def body(x_ref, sh_ref, sc_ref, gn_ref, w_ref, wf_ref, *rest):
        (q_ref, k_ref, v_ref, za_ref, u_ref, zb_ref, ga_ref, gb_ref, flc_ref, h_ref) = rest[nl:nl + 10]
        i = pl.program_id(0)
        _hosted_exchange(i == 0, i == pl.num_programs(0) - 1, rest[:nl], rest[nl + 10:2 * nl + 10],
                         *rest[2 * nl + 10:], scatter=False)
        xv = x_ref[...]
        r = lax.rsqrt(jnp.mean(xv * xv, axis=-1, keepdims=True) + EPS)
        h = (xv * r) * gn_ref[...] * (1.0 + sc_ref[...]) + sh_ref[...]
        hb = h.astype(MXU_DTYPE)
        h_ref[...] = hb

        def seg(off, n):
            return jnp.dot(hb, w_ref[:, off:off + n], preferred_element_type=F32)

        q_ref[...] = (seg(M_Q, WIDTH) * 0.125).astype(q_ref.dtype)
        k_ref[...] = seg(M_K, WIDTH).astype(k_ref.dtype)
        v_ref[...] = seg(M_V, WIDTH).astype(v_ref.dtype)
        za_ref[...] = seg(M_ZA, WIDTH)
        u_ref[...] = seg(M_U, WIDTH)
        zb_ref[...] = seg(M_ZB, WIDTH)
        ga_ref[...] = seg(M_GA, D_MODEL)
        gb_ref[...] = seg(M_GB, D_MODEL)
        flc_ref[...] = _mm32(h, wf_ref[...])

    row = lambda n: pl.BlockSpec((ts, n), lambda i: (i, 0))
    full = lambda a: pl.BlockSpec(a.shape, lambda i: (0,) * a.ndim)
    sds = jax.ShapeDtypeStruct
    return pl.pallas_call(
        body, name="proj_fwd", grid=(S // ts,),
        in_specs=[row(D_MODEL), full(shift), full(scale), full(g_norm), full(w_main), full(w_f)] + [ANY] * nl,
        out_specs=[row(WIDTH)] * 6 + [row(D_MODEL)] * 2 + [row(HEADS), row(D_MODEL)] + [ANY] * nl,
        out_shape=[sds((S, WIDTH), MXU_DTYPE)] * 3 + [sds((S, WIDTH), F32)] * 3 + [sds((S, D_MODEL), F32)] * 2
                  + [sds((S, HEADS), F32), sds((S, D_MODEL), MXU_DTYPE)]
                  + [sds((N_DEV,) + w.shape, w.dtype) for w in late],
        scratch_shapes=_exchange_scratch(nl),
        compiler_params=_params(("arbitrary",), VMEM_BIG),
    )(x, shift, scale, g_norm, w_main, w_f, *late)


def _log_sigmoid(z):
    return jnp.minimum(z, 0.0) - jnp.log(1.0 + jnp.exp(-jnp.abs(z)))


def _fgate_bwd(dfc, flc, bf_row, ts):
    S = flc.shape[0]
    n = S // ts

    def body(df_ref, flc_ref, bfr_ref, dfl_ref, dbf_ref, carry):
        @pl.when(pl.program_id(0) == 0)
        def _():
            carry[...] = jnp.zeros_like(carry)
            dbf_ref[...] = jnp.zeros_like(dbf_ref)

        ri = lax.broadcasted_iota(jnp.int32, (ts, ts), 0)
        ci = lax.broadcasted_iota(jnp.int32, (ts, ts), 1)
        upper = (ci >= ri).astype(F32)
        rc = _mm32(upper, df_ref[...]) + carry[...]
        carry[...] = rc[0:1, :]
        z = flc_ref[...] + bfr_ref[...]
        dfl = rc * _sigmoid(-z)
        dfl_ref[...] = dfl
        dbf_ref[...] += jnp.sum(dfl, axis=0, keepdims=True)

    col = pl.BlockSpec((ts, HEADS), lambda i: (n - 1 - i, 0))
    one = pl.BlockSpec((1, HEADS), lambda i: (0, 0))
    return pl.pallas_call(
        body, name="fgate_bwd", grid=(n,),
        in_specs=[col, col, one], out_specs=[col, one],
        out_shape=[jax.ShapeDtypeStruct((S, HEADS), F32), jax.ShapeDtypeStruct((1, HEADS), F32)],
        scratch_shapes=[pltpu.VMEM((1, HEADS), F32)],
        compiler_params=_params(("arbitrary",)),
    )(dfc, flc, bf_row)


N_EXTRA = 3


def _attn_prep(q, k, v, flc, bf_row, t):
    S = q.shape[0]
    nb = S // t

    def body(q_ref, k_ref, v_ref, flc_ref, bfr_ref, qh_ref, kh_ref, vt_ref, carry):
        @pl.when(pl.program_id(0) == 0)
        def _():
            carry[...] = jnp.zeros_like(carry)

        ri = lax.broadcasted_iota(jnp.int32, (t, t), 0)
        ci = lax.broadcasted_iota(jnp.int32, (t, t), 1)
        f = _mm32((ci <= ri).astype(F32), _log_sigmoid(flc_ref[...] + bfr_ref[...])) + carry[...]
        carry[...] = f[t - 1:t, :]
        lane = lax.broadcasted_iota(jnp.int32, (t, 128), 1)
        for p in range(PAIRS):
            qp = q_ref[:, p * 128:(p + 1) * 128]
            kp = k_ref[:, p * 128:(p + 1) * 128]
            vt_ref[p, 0] = v_ref[:, p * 128:(p + 1) * 128].T
            for h in range(2):
                own = (lane < 64) if h == 0 else (lane >= 64)
                base = 64 if h == 0 else 0
                fh = f[:, 2 * p + h:2 * p + h + 1]
                parts = []
                rest = fh
                for _ in range(N_EXTRA):
                    part = rest.astype(qh_ref.dtype)
                    parts.append(part)
                    rest = rest - part.astype(F32)
                one = jnp.ones((t, 1), qh_ref.dtype)
                eq = jnp.zeros((t, 128), qh_ref.dtype)
                ek = jnp.zeros((t, 128), qh_ref.dtype)
                for j in range(N_EXTRA):
                    eq = jnp.where(lane == base + j, parts[j], eq)
                    eq = jnp.where(lane == base + N_EXTRA + j, one, eq)
                    ek = jnp.where(lane == base + j, one, ek)
                    ek = jnp.where(lane == base + N_EXTRA + j, -parts[j], ek)
                qh_ref[2 * p + h] = jnp.where(own, qp, eq)
                kh_ref[2 * p + h] = jnp.where(own, kp, ek)

    row = pl.BlockSpec((t, WIDTH), lambda i: (i, 0))
    heads = pl.BlockSpec((HEADS, t, 128), lambda i: (0, i, 0))
    return pl.pallas_call(
        body, name="attn_prep", grid=(nb,),
        in_specs=[row, row, row, pl.BlockSpec((t, HEADS), lambda i: (i, 0)), pl.BlockSpec((1, HEADS), lambda i: (0, 0))],
        out_specs=[heads, heads, pl.BlockSpec((PAIRS, 1, 128, t), lambda i: (0, i, 0, 0))],
        out_shape=[jax.ShapeDtypeStruct((HEADS, S, 128), q.dtype), jax.ShapeDtypeStruct((HEADS, S, 128), k.dtype),
                   jax.ShapeDtypeStruct((PAIRS, nb, 128, t), v.dtype)],
        scratch_shapes=[pltpu.VMEM((1, HEADS), F32)],
        compiler_params=_params(("arbitrary",), VMEM_MID),
    )(q, k, v, flc, bf_row)


def _attn_fwd(qh, kh, vt, t):
    S = qh.shape[1]
    nb = S // t

    def body(q_ref, k_ref, vt_ref, o_ref, lse_ref, acc_s):
        qi = pl.program_id(1)
        acc_s[...] = jnp.zeros_like(acc_s)

        def step(ki, nblk, masked, carry):
            m_old, l_old = carry[:2], carry[2:]
            ks = pl.multiple_of(ki * t, t)
            rows = nblk * t
            sts = [_mm_nt(k_ref[h, pl.ds(ks, rows), :], q_ref[h]) for h in range(2)]
            if masked:
                ri = lax.broadcasted_iota(jnp.int32, (rows, t), 0)
                ci = lax.broadcasted_iota(jnp.int32, (rows, t), 1)
                sts = [jnp.where(ci >= ri - (nblk - 1) * t, st, NEG) for st in sts]
            m_new = [jnp.maximum(m_old[h], jnp.max(sts[h], axis=0, keepdims=True)) for h in range(2)]
            alpha = [jnp.exp(m_old[h] - m_new[h]) for h in range(2)]
            pts = [jnp.exp(sts[h] - m_new[h]) for h in range(2)]
            l_new = [alpha[h] * l_old[h] + jnp.sum(pts[h], axis=0, keepdims=True) for h in range(2)]
            for h in range(2):
                pv = _mm(vt_ref[ki], pts[h][:t])
                for b in range(1, nblk):
                    pv = pv + _mm(vt_ref[ki + b], pts[h][b * t:(b + 1) * t])
                acc_s[h] = alpha[h] * acc_s[h] + pv
            return (*m_new, *l_new)

        init = (jnp.full((1, t), -jnp.inf, F32),) * 2 + (jnp.zeros((1, t), F32),) * 2
        carry = lax.fori_loop(0, qi // 2, lambda j, c: step(2 * j, 2, False, c), init)
        m0, m1, l0, l1 = lax.cond(qi % 2 == 1, lambda c: step(qi - 1, 2, True, c), lambda c: step(qi, 1, True, c), carry)
        first = lax.broadcasted_iota(jnp.int32, (128, t), 0) < 64
        o_ref[...] = jnp.where(first, acc_s[0] / l0, acc_s[1] / l1).T
        lse_ref[...] = jnp.concatenate([m0 + jnp.log(l0), m1 + jnp.log(l1)], axis=0)

    return pl.pallas_call(
        body, name="attn_fwd", grid=(PAIRS, nb),
        in_specs=[pl.BlockSpec((2, t, 128), lambda p, i: (p, i, 0)), pl.BlockSpec((2, S, 128), lambda p, i: (p, 0, 0)),
                  pl.BlockSpec((None, nb, 128, t), lambda p, i: (p, 0, 0, 0))],
        out_specs=[pl.BlockSpec((t, 128), lambda p, i: (i, p)), pl.BlockSpec((None, None, 2, t), lambda p, i: (p, i, 0, 0))],
        out_shape=[jax.ShapeDtypeStruct((S, WIDTH), F32), jax.ShapeDtypeStruct((PAIRS, nb, 2, t), F32)],
        scratch_shapes=[pltpu.VMEM((2, 128, t), F32)],
        compiler_params=_params(("parallel", "parallel"), VMEM_MID),
    )(qh, kh, vt)


def _attn_bwd(qh, do, kh, v, lse4, dl4, early, t):
    S = qh.shape[1]
    nb = S // t
    ne = len(early)

    def body(q_ref, do_ref, k_ref, v_ref, lse_ref, dl_ref, *rest):
        dq_ref, dk_ref, dv_ref, dfq_ref, dfk_ref = rest[ne:ne + 5]
        kc_s, vh_s, dk_s, dv_s, dfk_s = rest[2 * ne + 5:2 * ne + 10]
        kj = pl.program_id(1)
        _hosted_exchange((pl.program_id(0) == 0) & (kj == 0), (pl.program_id(0) == PAIRS - 1) & (kj == nb - 1),
                         rest[:ne], rest[ne + 5:2 * ne + 5], *rest[2 * ne + 10:], scatter=True)
        lane = lax.broadcasted_iota(jnp.int32, (t, 128), 1)
        is_a = lane < 64

        @pl.when(kj == 0)
        def _():
            dq_ref[...] = jnp.zeros_like(dq_ref)
            dfq_ref[...] = jnp.zeros_like(dfq_ref)

        vp = v_ref[...]
        zero = jnp.zeros_like(vp)
        kc_s[0] = jnp.where(is_a, k_ref[0], zero.astype(kc_s.dtype))
        kc_s[1] = jnp.where(is_a, zero.astype(kc_s.dtype), k_ref[1])
        vh_s[0] = jnp.where(is_a, vp, zero)
        vh_s[1] = jnp.where(is_a, zero, vp)
        dk_s[...] = jnp.zeros_like(dk_s)
        dv_s[...] = jnp.zeros_like(dv_s)
        dfk_s[...] = jnp.zeros_like(dfk_s)

        def step(qi, masked):
            qs = pl.multiple_of(qi * t, t)
            dob = do_ref[pl.ds(qs, t), :]
            lse = lse_ref[qi]
            dl = dl_ref[qi]
            zq = jnp.zeros_like(dob)
            over_keys = []
            for h in range(2):
                sel = is_a if h == 0 else jnp.logical_not(is_a)
                qb = q_ref[h, pl.ds(qs, t), :]
                st = _mm_nt(k_ref[h], qb) - lse[h:h + 1, :]
                if masked:
                    ri = lax.broadcasted_iota(jnp.int32, (t, t), 0)
                    ci = lax.broadcasted_iota(jnp.int32, (t, t), 1)
                    st = jnp.where(ci >= ri, st, NEG)
                pt = jnp.exp(st)
                dv_s[...] += _mm(pt, jnp.where(sel, dob, zq))
                dpt = _mm_nt(vh_s[h], dob)
                dst = pt * (dpt - dl[h:h + 1, :])
                dfk_s[h] += jnp.sum(dst, axis=1, keepdims=True)
                over_keys.append(jnp.sum(dst, axis=0, keepdims=True))
                dk_s[...] += _mm(dst, jnp.where(sel, qb, jnp.zeros_like(qb)))
                dq_ref[pl.ds(qs, t), :] += _mm_tn(dst, kc_s[h])
            dfq_ref[qi] += jnp.concatenate(over_keys, axis=0)

        step(kj, True)

        def loop_body(qi, carry):
            step(qi, False)
            return carry

        lax.fori_loop(kj + 1, nb, loop_body, 0)
        dk_ref[...] = dk_s[...].astype(dk_ref.dtype)
        dv_ref[...] = dv_s[...].astype(dv_ref.dtype)
        dfk_ref[...] = jnp.where(lax.broadcasted_iota(jnp.int32, (t, 2), 1) == 0, dfk_s[0], dfk_s[1])

        @pl.when(kj == nb - 1)
        def _():
            dq_ref[...] = dq_ref[...] * 0.125

    blk = pl.BlockSpec((t, 128), lambda p, j: (j, p))
    res = pl.BlockSpec((S, 128), lambda p, j: (0, p))
    rows4 = pl.BlockSpec((None, nb, 2, t), lambda p, j: (p, 0, 0, 0))
    cols4 = pl.BlockSpec((None, t, 2), lambda p, j: (p, j, 0))
    return pl.pallas_call(
        body, name="attn_bwd", grid=(PAIRS, nb),
        in_specs=[pl.BlockSpec((2, S, 128), lambda p, j: (p, 0, 0)), res,
                  pl.BlockSpec((2, t, 128), lambda p, j: (p, j, 0)), blk, rows4, rows4] + [ANY] * ne,
        out_specs=[res, blk, blk, rows4, cols4] + [ANY] * ne,
        out_shape=[jax.ShapeDtypeStruct((S, WIDTH), F32), jax.ShapeDtypeStruct((S, WIDTH), MXU_DTYPE),
                   jax.ShapeDtypeStruct((S, WIDTH), MXU_DTYPE), jax.ShapeDtypeStruct((PAIRS, nb, 2, t), F32),
                   jax.ShapeDtypeStruct((PAIRS, S, 2), F32)] + [jax.ShapeDtypeStruct(e.shape, e.dtype) for e in early],
        scratch_shapes=[pltpu.VMEM((2, t, 128), kh.dtype), pltpu.VMEM((2, t, 128), v.dtype),
                        pltpu.VMEM((t, 128), F32), pltpu.VMEM((t, 128), F32), pltpu.VMEM((2, t, 1), F32)]
                       + _exchange_scratch(ne),
        compiler_params=_params(("arbitrary", "arbitrary"), VMEM_MID),
    )(qh, do, kh, v, lse4, dl4, *early)


def _s5_mats(a_re, a_im, log_dt, b_re, b_im, c_re, c_im, d_skip):
    Lc = CHUNK
    dt = jnp.exp(log_dt)[:, None]
    lr, li = a_re * dt, a_im * dt

    def apow(n):
        n = jnp.asarray(n, F32)[None, :, None]
        mag = jnp.exp(n * lr[:, None, :])
        ang = n * li[:, None, :]
        return mag * jnp.cos(ang), mag * jnp.sin(ang)

    ar, ai = apow([1.0])
    ar, ai = ar[:, 0], ai[:, 0]
    den = a_re * a_re + a_im * a_im
    nr, ni = ar - 1.0, ai
    fr = (nr * a_re + ni * a_im) / den
    fi = (ni * a_re - nr * a_im) / den
    bbr = fr[:, :, None] * b_re - fi[:, :, None] * b_im
    bbi = fr[:, :, None] * b_im + fi[:, :, None] * b_re
    steps = np.arange(Lc, dtype=np.float32)
    pr, pi = apow(steps)
    car = c_re[:, None] * pr[:, :, None, :] - c_im[:, None] * pi[:, :, None, :]
    cai = c_re[:, None] * pi[:, :, None, :] + c_im[:, None] * pr[:, :, None, :]
    kern = (jnp.einsum('glcp,gpd->glcd', car, bbr, precision=HI)
            - jnp.einsum('glcp,gpd->glcd', cai, bbi, precision=HI))
    skip = d_skip.reshape(GROUPS, CG)[:, :, None] * jnp.eye(CG, dtype=F32)[None]
    kern = kern.at[:, 0].add(skip)
    trow = kern.transpose(0, 3, 1, 2).reshape(GROUPS, CG, Lc * CG)
    p1r, p1i = apow(steps + 1.0)
    cr = c_re[:, None] * p1r[:, :, None, :] - c_im[:, None] * p1i[:, :, None, :]
    ci = c_re[:, None] * p1i[:, :, None, :] + c_im[:, None] * p1r[:, :, None, :]
    to_rows = lambda m: m.transpose(0, 3, 1, 2).reshape(GROUPS, STATE, Lc * CG)
    camat = jnp.concatenate([to_rows(cr), -to_rows(ci)], axis=1)
    qr, qi = apow(Lc - 1.0 - steps)
    zr = qr[:, :, None, :] * bbr.transpose(0, 2, 1)[:, None] - qi[:, :, None, :] * bbi.transpose(0, 2, 1)[:, None]
    zi = qr[:, :, None, :] * bbi.transpose(0, 2, 1)[:, None] + qi[:, :, None, :] * bbr.transpose(0, 2, 1)[:, None]
    bzmat = jnp.concatenate([zr, zi], axis=-1).reshape(GROUPS, Lc * CG, 2 * STATE)
    lr_, li_ = apow([float(Lc)])
    al = jnp.concatenate([lr_[:, 0], li_[:, 0]], axis=-1)
    return trow, camat, bzmat, al


def _s5_scan_powers(a_re, a_im, log_dt, n_steps):
    dt = jnp.exp(log_dt)[:, None]
    lr, li = a_re * dt, a_im * dt
    n = (CHUNK * 2.0 ** np.arange(n_steps)).astype(np.float32)[None, :, None]
    mag = jnp.exp(n * lr[:, None, :])
    pr, pi = mag * jnp.cos(n * li[:, None, :]), mag * jnp.sin(n * li[:, None, :])
    fwd = jnp.stack([jnp.concatenate([pr, pr], -1), jnp.concatenate([-pi, pi], -1)], axis=2)
    bwd = jnp.stack([jnp.concatenate([pr, pr], -1), jnp.concatenate([pi, -pi], -1)], axis=2)
    return fwd, bwd


def _shift_rows(x, sh, down):
    n = x.shape[0]
    ri = lax.broadcasted_iota(jnp.int32, x.shape, 0)
    if down:
        return jnp.where(ri >= sh, pltpu.roll(x, sh, 0), 0.0)
    return jnp.where(ri < n - sh, pltpu.roll(x, n - sh, 0), 0.0)


GPB = 128 // CG


def _lane_transpose(arrs):
    lane = lax.broadcasted_iota(jnp.int32, arrs[0].shape, 1)
    arrs = list(arrs)
    k = GPB // 2
    while k >= 1:
        hi = ((lane // CG) & k) != 0
        new = list(arrs)
        for i in range(GPB):
            if i & k:
                continue
            lo_arr, hi_arr = arrs[i], arrs[i + k]
            new[i] = jnp.where(hi, pltpu.roll(hi_arr, CG * k, 1), lo_arr)
            new[i + k] = jnp.where(hi, hi_arr, pltpu.roll(lo_arr, 128 - CG * k, 1))
        arrs = new
        k //= 2
    return arrs


def _gather_block(ref, dst, nch):
    for half in range(CHUNK // GPB):
        outs = _lane_transpose([ref[pl.ds(half * GPB + l8, nch, stride=CHUNK), :] for l8 in range(GPB)])
        for g in range(GPB):
            dst[half, g] = outs[g]


def _scatter_block(src, ref, nch):
    for half in range(CHUNK // GPB):
        outs = _lane_transpose([src[half, g] for g in range(GPB)])
        for l8 in range(GPB):
            ref[pl.ds(half * GPB + l8, nch, stride=CHUNK), :] = outs[l8]


def _toeplitz(trow):
    lane = lax.broadcasted_iota(jnp.int32, (CG, 128), 1)
    x0, x1 = trow[:, :128], trow[:, 128:]
    zero = jnp.zeros_like(x0)
    rows = []
    for s in range(CHUNK):
        sh = (CG * s) % 128
        r0 = pltpu.roll(x0, sh, 1) if sh else x0
        r1 = pltpu.roll(x1, sh, 1) if sh else x1
        if CG * s < 128:
            rows.append(jnp.concatenate([jnp.where(lane >= sh, r0, zero), jnp.where(lane >= sh, r1, r0)], axis=1))
        else:
            rows.append(jnp.concatenate([zero, jnp.where(lane >= sh, r0, zero)], axis=1))
    return jnp.concatenate(rows, axis=0)


def _toeplitz_adjoint(dt):
    lane = lax.broadcasted_iota(jnp.int32, (CG, 128), 1)
    acc0 = jnp.zeros((CG, 128), F32)
    acc1 = jnp.zeros((CG, 128), F32)
    for s in range(CHUNK):
        x0, x1 = dt[CG * s:CG * s + CG, :128], dt[CG * s:CG * s + CG, 128:]
        sh = (CG * s) % 128
        keep = 128 - sh
        r0 = pltpu.roll(x0, keep, 1) if sh else x0
        r1 = pltpu.roll(x1, keep, 1) if sh else x1
        if CG * s < 128:
            acc0 = acc0 + jnp.where(lane < keep, r0, r1)
            acc1 = acc1 + jnp.where(lane < keep, r1, 0.0)
        else:
            acc0 = acc0 + jnp.where(lane < keep, r1, 0.0)
    return jnp.concatenate([acc0, acc1], axis=1)


def _s5_fwd(u, trow, camat, bzmat, pw):
    S = u.shape[0]
    nch = S // CHUNK
    n_steps = pw.shape[1]

    def body(u_ref, t_ref, ca_ref, bz_ref, pw_ref, y_ref, xp_ref, uc_ref, ub_s, yb_s):
        g = pl.program_id(1)

        @pl.when(g == 0)
        def _():
            _gather_block(u_ref, ub_s, nch)

        uc = jnp.concatenate([ub_s[0, g], ub_s[1, g]], axis=1)
        uc_ref[...] = uc
        x = _s5mm(uc, bz_ref[...])
        for kk in range(n_steps):
            xs = _shift_rows(x, 2 ** kk, True)
            m = pw_ref[kk]
            x = x + m[0:1, :] * xs + m[1:2, :] * pltpu.roll(xs, STATE, 1)
        xp = _shift_rows(x, 1, True)
        xp_ref[...] = xp
        yc = _s5mm(uc, _toeplitz(t_ref[...])) + _s5mm(xp, ca_ref[...])
        yb_s[0, g] = yc[:, :128]
        yb_s[1, g] = yc[:, 128:]

        @pl.when(g == GPB - 1)
        def _():
            _scatter_block(yb_s, y_ref, nch)

    per = lambda a: pl.BlockSpec((None,) + a.shape[1:], lambda b, g: (b * GPB + g,) + (0,) * (a.ndim - 1))
    nat = pl.BlockSpec((S, 128), lambda b, g: (0, b))
    return pl.pallas_call(
        body, name="s5_fwd", grid=(GROUPS // GPB, GPB),
        in_specs=[nat, per(trow), per(camat), per(bzmat), per(pw)],
        out_specs=[nat, pl.BlockSpec((None, nch, 2 * STATE), lambda b, g: (b * GPB + g, 0, 0)),
                   pl.BlockSpec((None, nch, CHUNK * CG), lambda b, g: (b * GPB + g, 0, 0))],
        out_shape=[jax.ShapeDtypeStruct((S, GROUPS * CG), F32), jax.ShapeDtypeStruct((GROUPS, nch, 2 * STATE), F32),
                   jax.ShapeDtypeStruct((GROUPS, nch, CHUNK * CG), F32)],
        scratch_shapes=[pltpu.VMEM((CHUNK // GPB, GPB, nch, 128), F32)] * 2,
        compiler_params=_params(("parallel", "arbitrary"), VMEM_BIG),
    )(u, trow, camat, bzmat, pw)


def _s5_bwd(uc, dy, xp, trow, camat, bzmat, pwc):
    S = dy.shape[0]
    nch = S // CHUNK
    n_steps = pwc.shape[1]

    def body(uc_ref, dy_ref, xp_ref, t_ref, ca_ref, bz_ref, pw_ref, du_ref, dt_ref, dca_ref, dbz_ref, dal_ref,
             dyb_s, dub_s):
        g = pl.program_id(1)

        @pl.when(g == 0)
        def _():
            _gather_block(dy_ref, dyb_s, nch)

        uc = uc_ref[...]
        dyc = jnp.concatenate([dyb_s[0, g], dyb_s[1, g]], axis=1)
        xpv = xp_ref[...]
        dt_ref[...] = _toeplitz_adjoint(_s5mm_tn(uc, dyc))
        dca_ref[...] = _s5mm_tn(xpv, dyc)
        dx = _shift_rows(_s5mm_nt(dyc, ca_ref[...]), 1, False)
        for kk in range(n_steps):
            xs = _shift_rows(dx, 2 ** kk, False)
            m = pw_ref[kk]
            dx = dx + m[0:1, :] * xs + m[1:2, :] * pltpu.roll(xs, STATE, 1)
        dbz_ref[...] = _s5mm_tn(uc, dx)
        dal_ref[0:1, :] = jnp.sum(dx * xpv, axis=0, keepdims=True)
        dal_ref[1:2, :] = jnp.sum(dx * pltpu.roll(xpv, STATE, 1), axis=0, keepdims=True)
        duc = _s5mm_nt(dyc, _toeplitz(t_ref[...])) + _s5mm_nt(dx, bz_ref[...])
        dub_s[0, g] = duc[:, :128]
        dub_s[1, g] = duc[:, 128:]

        @pl.when(g == GPB - 1)
        def _():
            _scatter_block(dub_s, du_ref, nch)

    per = lambda a: pl.BlockSpec((None,) + a.shape[1:], lambda b, g: (b * GPB + g,) + (0,) * (a.ndim - 1))
    nat = pl.BlockSpec((S, 128), lambda b, g: (0, b))
    sds = jax.ShapeDtypeStruct
    mats = [sds(trow.shape, F32), sds(camat.shape, F32), sds(bzmat.shape, F32), sds((GROUPS, 2, 2 * STATE), F32)]
    return pl.pallas_call(
        body, name="s5_bwd", grid=(GROUPS // GPB, GPB),
        in_specs=[per(uc), nat, per(xp), per(trow), per(camat), per(bzmat), per(pwc)],
        out_specs=[nat] + [per(o) for o in mats], out_shape=[sds((S, GROUPS * CG), F32)] + mats,
        scratch_shapes=[pltpu.VMEM((CHUNK // GPB, GPB, nch, 128), F32)] * 2,
        compiler_params=_params(("parallel", "arbitrary"), VMEM_BIG),
    )(uc, dy, xp, trow, camat, bzmat, pwc)


GELU_C0 = math.sqrt(2.0 / math.pi)
GELU_C1 = 0.044715


def _mix(o, za, ys, zb, ga, gb, x, tgt, gate, b_glu, g_final, w_glu, w_up_a, w_up_b, w_out, hsel, ts):
    S = o.shape[0]

    def body(o_ref, za_ref, ys_ref, zb_ref, ga_ref, gb_ref, x_ref, t_ref, gate_ref, bglu_ref, gf_ref,
             wglu_ref, wua_ref, wub_ref, wout_ref, hsel_ref,
             dx2_ref, do_ref, dza_ref, dzb_ref, dga_ref, dgb_ref, dys_ref, dl_ref,
             mg_ref, dmo_ref, ya_ref, dua_ref, yb_ref, dub_ref, yg_ref, dgl_ref,
             dbglu_ref, dgate_ref, dgf_ref, loss_ref):
        @pl.when(pl.program_id(0) == 0)
        def _():
            dbglu_ref[...] = jnp.zeros_like(dbglu_ref)
            dgate_ref[...] = jnp.zeros_like(dgate_ref)
            dgf_ref[...] = jnp.zeros_like(dgf_ref)
            loss_ref[...] = jnp.zeros_like(loss_ref)

        ov = o_ref[...]
        za = za_ref[...]
        sza = _sigmoid(za)
        silu_a = za * sza
        ya = ov * silu_a
        ya_b = ya.astype(ya_ref.dtype)
        ya_ref[...] = ya_b
        ysv = ys_ref[...]
        th = jnp.tanh(GELU_C0 * (ysv + GELU_C1 * ysv * ysv * ysv))
        yg = 0.5 * ysv * (1.0 + th)
        yg_b = yg.astype(yg_ref.dtype)
        yg_ref[...] = yg_b
        sg = _sigmoid(_mm(yg_b, wglu_ref[...]) + bglu_ref[...])
        yb1 = yg * sg
        zb = zb_ref[...]
        szb = _sigmoid(zb)
        silu_b = zb * szb
        yb_b = (yb1 * silu_b).astype(yb_ref.dtype)
        yb_ref[...] = yb_b
        ua = _mm(ya_b, wua_ref[...])
        ub = _mm(yb_b, wub_ref[...])
        sa = _sigmoid(ga_ref[...])
        sb = _sigmoid(gb_ref[...])
        merged_b = (sa * ua + sb * ub).astype(mg_ref.dtype)
        mg_ref[...] = merged_b
        mo = _mm(merged_b, wout_ref[...])
        gate_v = gate_ref[...]
        x2 = x_ref[...] + gate_v * mo
        r2 = lax.rsqrt(jnp.mean(x2 * x2, axis=-1, keepdims=True) + EPS)
        x2n = x2 * r2
        gf = gf_ref[...]
        diff = x2n * gf - t_ref[...]
        loss_ref[...] += jnp.sum(jnp.sum(diff * diff, axis=-1, keepdims=True), axis=0, keepdims=True) * (0.5 / D_MODEL)
        dgf_ref[...] += jnp.sum(diff * x2n, axis=0, keepdims=True) * (1.0 / D_MODEL)
        dyg = diff * (gf * (1.0 / D_MODEL))
        dx2 = r2 * (dyg - x2n * jnp.mean(dyg * x2n, axis=-1, keepdims=True))
        dx2_ref[...] = dx2
        dgate_ref[...] += jnp.sum(dx2 * mo, axis=0, keepdims=True)
        dmo_b = (dx2 * gate_v).astype(dmo_ref.dtype)
        dmo_ref[...] = dmo_b
        dmerged = _mm_nt(dmo_b, wout_ref[...])
        dua = dmerged * sa
        dub = dmerged * sb
        dua_b = dua.astype(dua_ref.dtype)
        dub_b = dub.astype(dub_ref.dtype)
        dua_ref[...] = dua_b
        dub_ref[...] = dub_b
        dga_ref[...] = (dua * ua * (1.0 - sa)).astype(dga_ref.dtype)
        dgb_ref[...] = (dub * ub * (1.0 - sb)).astype(dgb_ref.dtype)
        dya = _mm_nt(dua_b, wua_ref[...])
        dyb = _mm_nt(dub_b, wub_ref[...])
        dov = dya * silu_a
        do_ref[...] = dov.astype(do_ref.dtype)
        dl_ref[...] = _mm32_nt(hsel_ref[...], dov * ov)
        dza_ref[...] = (dya * ov * (sza + silu_a * (1.0 - sza))).astype(dza_ref.dtype)
        dyb1 = dyb * silu_b
        dzb_ref[...] = (dyb * yb1 * (szb + silu_b * (1.0 - szb))).astype(dzb_ref.dtype)
        dgl = dyb1 * yg * sg * (1.0 - sg)
        dbglu_ref[...] += jnp.sum(dgl, axis=0, keepdims=True)
        dgl_b = dgl.astype(dgl_ref.dtype)
        dgl_ref[...] = dgl_b
        dyg2 = dyb1 * sg + _mm_nt(dgl_b, wglu_ref[...])
        dgelu = 0.5 * (1.0 + th) + 0.5 * ysv * (1.0 - th * th) * GELU_C0 * (1.0 + 3.0 * GELU_C1 * ysv * ysv)
        dys_ref[...] = dyg2 * dgelu

    row = lambda n: pl.BlockSpec((ts, n), lambda i: (i, 0))
    full = lambda a: pl.BlockSpec(a.shape, lambda i: (0,) * a.ndim)
    vec = lambda n: pl.BlockSpec((1, n), lambda i: (0, 0))
    sds = jax.ShapeDtypeStruct
    W, Dm = WIDTH, D_MODEL
    return pl.pallas_call(
        body, name="mix", grid=(S // ts,),
        in_specs=[row(W), row(W), row(W), row(W), row(Dm), row(Dm), row(Dm), row(Dm),
                  full(gate), full(b_glu), full(g_final), full(w_glu), full(w_up_a), full(w_up_b), full(w_out), full(hsel)],
        out_specs=[row(Dm), row(W), row(W), row(W), row(Dm), row(Dm), row(W), pl.BlockSpec((HEADS, ts), lambda i: (0, i)),
                   row(Dm), row(Dm), row(W), row(Dm), row(W), row(Dm), row(W), row(W),
                   vec(W), vec(Dm), vec(Dm), vec(1)],
        out_shape=[sds((S, Dm), F32), sds((S, W), MXU_DTYPE), sds((S, W), MXU_DTYPE), sds((S, W), MXU_DTYPE),
                   sds((S, Dm), MXU_DTYPE), sds((S, Dm), MXU_DTYPE), sds((S, W), F32), sds((HEADS, S), F32),
                   sds((S, Dm), MXU_DTYPE), sds((S, Dm), MXU_DTYPE), sds((S, W), MXU_DTYPE), sds((S, Dm), MXU_DTYPE),
                   sds((S, W), MXU_DTYPE), sds((S, Dm), MXU_DTYPE), sds((S, W), MXU_DTYPE), sds((S, W), MXU_DTYPE),
                   sds((1, W), F32), sds((1, Dm), F32), sds((1, Dm), F32), sds((1, 1), F32)],
        compiler_params=_params(("arbitrary",), VMEM_BIG),
    )(o, za, ys, zb, ga, gb, x, tgt, gate, b_glu, g_final, w_glu, w_up_a, w_up_b, w_out, hsel)


def _matmul_tn(name, a, b, ts):
    S, M = a.shape
    N = b.shape[1]
    tn = min(N, 1024)

    def body(a_ref, b_ref, o_ref):
        @pl.when(pl.program_id(1) == 0)
        def _():
            o_ref[...] = jnp.zeros_like(o_ref)

        o_ref[...] += _mm_tn(a_ref[...], b_ref[...])

    return pl.pallas_call(
        body, name=name, grid=(N // tn, S // ts),
        in_specs=[pl.BlockSpec((ts, M), lambda j, i: (i, 0)), pl.BlockSpec((ts, tn), lambda j, i: (i, j))],
        out_specs=pl.BlockSpec((M, tn), lambda j, i: (0, j)),
        out_shape=jax.ShapeDtypeStruct((M, N), F32),
        compiler_params=_params(("parallel", "arbitrary"), VMEM_MID),
    )(a, b)


def _matmul_tn_multi(name, a, bs, ts):
    S, M = a.shape
    nb = len(bs)

    def body(a_ref, *refs):
        @pl.when(pl.program_id(0) == 0)
        def _():
            for o_ref in refs[nb:]:
                o_ref[...] = jnp.zeros_like(o_ref)

        av = a_ref[...]
        for b_ref, o_ref in zip(refs[:nb], refs[nb:]):
            o_ref[...] += _mm_tn(av, b_ref[...])

    return pl.pallas_call(
        body, name=name, grid=(S // ts,),
        in_specs=[pl.BlockSpec((ts, M), lambda i: (i, 0))] + [pl.BlockSpec((ts, b.shape[1]), lambda i: (i, 0)) for b in bs],
        out_specs=[pl.BlockSpec((M, b.shape[1]), lambda i: (0, 0)) for b in bs],
        out_shape=[jax.ShapeDtypeStruct((M, b.shape[1]), F32) for b in bs],
        compiler_params=_params(("arbitrary",), VMEM_BIG),
    )(a, *bs)


def _proj_bwd(dq, dk, dv, dza, du, dzb, dga, dgb, dfl, x, dx2, shift, scale, g_norm, w_main, w_ft, chip_planes,
              gathers, ts):
    S = x.shape[0]
    nc, ng = len(chip_planes), len(gathers)
    nx = nc + ng

    def body(dq_ref, dk_ref, dv_ref, dza_ref, du_ref, dzb_ref, dga_ref, dgb_ref, dfl_ref, x_ref, dx2_ref,
             sc_ref, gn_ref, w_ref, wft_ref, *rest):
        gx_ref, dsh_ref, dsc_ref, dgn_ref = rest[nx:nx + 4]
        outs, sems = rest[nx + 4:2 * nx + 4], rest[2 * nx + 4:]
        i = pl.program_id(0)
        first, last = i == 0, i == pl.num_programs(0) - 1
        _hosted_chip_exchange(first, last, rest[:nc], outs[:nc], *sems[:3])
        _hosted_exchange(first, last, rest[nc:nx], outs[nc:], *sems[3:], scatter=False)

        @pl.when(pl.program_id(0) == 0)
        def _():
            dsh_ref[...] = jnp.zeros_like(dsh_ref)
            dsc_ref[...] = jnp.zeros_like(dsc_ref)
            dgn_ref[...] = jnp.zeros_like(dgn_ref)

        def seg(ref, off, n):
            return _mm_nt(ref[...], w_ref[:, off:off + n])

        dh = (seg(dq_ref, M_Q, WIDTH) + seg(dk_ref, M_K, WIDTH) + seg(dv_ref, M_V, WIDTH)
              + seg(dza_ref, M_ZA, WIDTH) + seg(du_ref, M_U, WIDTH) + seg(dzb_ref, M_ZB, WIDTH)
              + seg(dga_ref, M_GA, D_MODEL) + seg(dgb_ref, M_GB, D_MODEL)
              + _mm32(dfl_ref[...], wft_ref[...]))
        xv = x_ref[...]
        r = lax.rsqrt(jnp.mean(xv * xv, axis=-1, keepdims=True) + EPS)
        xn = xv * r
        gn = gn_ref[...]
        s1 = 1.0 + sc_ref[...]
        dsh_ref[...] += jnp.sum(dh, axis=0, keepdims=True)
        dhx = dh * xn
        dsc_ref[...] += jnp.sum(dhx, axis=0, keepdims=True) * gn
        dgn_ref[...] += jnp.sum(dhx, axis=0, keepdims=True) * s1
        dxn = dh * (gn * s1)
        gx_ref[...] = dx2_ref[...] + r * (dxn - xn * jnp.mean(dxn * xn, axis=-1, keepdims=True))

    row = lambda n: pl.BlockSpec((ts, n), lambda i: (i, 0))
    full = lambda a: pl.BlockSpec(a.shape, lambda i: (0,) * a.ndim)
    vec = pl.BlockSpec((1, D_MODEL), lambda i: (0, 0))
    W, Dm = WIDTH, D_MODEL
    del shift
    return pl.pallas_call(
        body, name="proj_bwd", grid=(S // ts,),
        in_specs=[row(W)] * 6 + [row(Dm)] * 2 + [row(HEADS), row(Dm), row(Dm),
                                                 full(scale), full(g_norm), full(w_main), full(w_ft)] + [ANY] * nx,
        out_specs=[row(Dm), vec, vec, vec] + [ANY] * nx,
        out_shape=[jax.ShapeDtypeStruct((S, Dm), F32)] + [jax.ShapeDtypeStruct((1, Dm), F32)] * 3
                  + [jax.ShapeDtypeStruct(p.shape, p.dtype) for p in chip_planes]
                  + [jax.ShapeDtypeStruct((N_DEV,) + g.shape, g.dtype) for g in gathers],
        scratch_shapes=[pltpu.SemaphoreType.DMA((nc, N_CHIP)), pltpu.SemaphoreType.DMA((nc, N_CHIP)),
                        pltpu.SemaphoreType.DMA((nc,))] + _exchange_scratch(ng),
        compiler_params=_params(("arbitrary",), VMEM_BIG),
    )(dq, dk, dv, dza, du, dzb, dga, dgb, dfl, x, dx2, scale, g_norm, w_main, w_ft, *chip_planes, *gathers)


def _adamw(name, planes, w, m, v, tr):
    n, R, C = planes.shape
    bc1 = 1.0 - ADAM_B1 ** ADAM_STEP
    bc2 = 1.0 - ADAM_B2 ** ADAM_STEP

    def body(p_ref, w_ref, m_ref, v_ref, g_ref, d_ref, nm_ref, nv_ref):
        g = p_ref[0].astype(F32)
        for i in range(1, n):
            g = g + p_ref[i].astype(F32)
        g_ref[...] = g
        nm = ADAM_B1 * m_ref[...] + (1.0 - ADAM_B1) * g
        nv = ADAM_B2 * v_ref[...] + (1.0 - ADAM_B2) * (g * g)
        nm_ref[...] = nm
        nv_ref[...] = nv
        d_ref[...] = -ADAM_LR * ((nm / bc1) / (jnp.sqrt(nv / bc2) + ADAM_EPS) + ADAM_WD * w_ref[...])

    blk = pl.BlockSpec((tr, C), lambda i: (i, 0))
    return pl.pallas_call(
        body, name=name, grid=(R // tr,),
        in_specs=[pl.BlockSpec((n, tr, C), lambda i: (0, i, 0)), blk, blk, blk],
        out_specs=[blk] * 4, out_shape=[jax.ShapeDtypeStruct((R, C), F32)] * 4,
        compiler_params=_params(("parallel",), VMEM_MID),
    )(planes, w, m, v)


def _wada_grad(c_all, dmod_cols):
    def body(c_ref, d_ref, o_ref):
        o_ref[0] = _mm32_tn(c_ref[...], d_ref[...])

    return pl.pallas_call(
        body, name="wada_grad",
        out_shape=jax.ShapeDtypeStruct((1, c_all.shape[1], dmod_cols.shape[1]), F32),
        in_specs=[VMEM, VMEM], out_specs=VMEM,
    )(c_all, dmod_cols)


SMALL_ORDER = ("b_ada", "g_norm", "b_f", "a_re", "a_im", "log_dt", "b_re", "b_im", "c_re", "c_im",
               "d_skip", "b_glu", "g_final")
BIG_ORDER = ("w_ada", "w_in", "w_glu", "w_up_a", "w_up_b", "w_out")
ALL_ORDER = ("w_ada", "b_ada", "g_norm", "w_in", "b_f", "a_re", "a_im", "log_dt", "b_re", "b_im", "c_re", "c_im",
             "d_skip", "w_glu", "b_glu", "w_up_a", "w_up_b", "w_out", "g_final")


def _pack_small(parts, rows):
    flat = jnp.concatenate([p.reshape(-1).astype(F32) for p in parts])
    return jnp.pad(flat, (0, rows * 128 - flat.shape[0])).reshape(rows, 128)


def kernel(x, c, w_ada, b_ada, g_norm, w_in, b_f, a_re, a_im, log_dt, b_re, b_im, c_re, c_im, d_skip, w_glu, b_glu, w_up_a, w_up_b, w_out, g_final, loss_target, m_w_ada, m_b_ada, m_g_norm, m_w_in, m_b_f, m_a_re, m_a_im, m_log_dt, m_b_re, m_b_im, m_c_re, m_c_im, m_d_skip, m_w_glu, m_b_glu, m_w_up_a, m_w_up_b, m_w_out, m_g_final, v_w_ada, v_b_ada, v_g_norm, v_w_in, v_b_f, v_a_re, v_a_im, v_log_dt, v_b_re, v_b_im, v_c_re, v_c_im, v_d_skip, v_w_glu, v_b_glu, v_w_up_a, v_w_up_b, v_w_out, v_g_final):
    weights = dict(w_ada=w_ada, b_ada=b_ada, g_norm=g_norm, w_in=w_in, b_f=b_f, a_re=a_re, a_im=a_im, log_dt=log_dt,
                   b_re=b_re, b_im=b_im, c_re=c_re, c_im=c_im, d_skip=d_skip, w_glu=w_glu, b_glu=b_glu,
                   w_up_a=w_up_a, w_up_b=w_up_b, w_out=w_out, g_final=g_final)
    mom_m = dict(w_ada=m_w_ada, b_ada=m_b_ada, g_norm=m_g_norm, w_in=m_w_in, b_f=m_b_f, a_re=m_a_re, a_im=m_a_im,
                 log_dt=m_log_dt, b_re=m_b_re, b_im=m_b_im, c_re=m_c_re, c_im=m_c_im, d_skip=m_d_skip, w_glu=m_w_glu,
                 b_glu=m_b_glu, w_up_a=m_w_up_a, w_up_b=m_w_up_b, w_out=m_w_out, g_final=m_g_final)
    mom_v = dict(w_ada=v_w_ada, b_ada=v_b_ada, g_norm=v_g_norm, w_in=v_w_in, b_f=v_b_f, a_re=v_a_re, a_im=v_a_im,
                 log_dt=v_log_dt, b_re=v_b_re, b_im=v_b_im, c_re=v_c_re, c_im=v_c_im, d_skip=v_d_skip, w_glu=v_w_glu,
                 b_glu=v_b_glu, w_up_a=v_w_up_a, w_up_b=v_w_up_b, w_out=v_w_out, g_final=v_g_final)
    xs = x[0]
    tgt = loss_target[0]
    S = xs.shape[0]
    ts = min(256, S)
    ta = min(512, S)
    tw = min(2048, S)
    nch = S // CHUNK
    n_steps = max(1, int(math.ceil(math.log2(nch))))
    me = _my_index()

    shards = [w.astype(MXU_DTYPE) for w in (w_in[0], w_glu[0], w_up_a[0], w_up_b[0], w_out[0])]
    mod8, c_all, gathered = _comm_in(c, w_ada[0], b_ada.reshape(N_DEV, -1), shards[:1])
    mod = mod8.reshape(1, 3 * D_MODEL)
    shift, scale, gate = mod[:, :D_MODEL], mod[:, D_MODEL:2 * D_MODEL], mod[:, 2 * D_MODEL:]
    w_in_full = gathered[0].transpose(1, 0, 2).reshape(D_MODEL, PROJ_WIDTH)
    w_main = jnp.concatenate([w_in_full[:, :OFF_F], w_in_full[:, OFF_F + HEADS:]], axis=1)
    w_f = w_in_full[:, OFF_F:OFF_F + HEADS].astype(F32)
    w_ft = w_f.T

    q, k, v, za, u, zb, ga, gb, flc, hb, *late = _proj_fwd(xs, shift, scale, g_norm, w_main, w_f, shards[1:], ts)
    w_glu_full = late[0].reshape(WIDTH, WIDTH)
    w_up_a_full = late[1].transpose(1, 0, 2).reshape(WIDTH, D_MODEL)
    w_up_b_full = late[2].transpose(1, 0, 2).reshape(WIDTH, D_MODEL)
    w_out_full = late[3].reshape(D_MODEL, D_MODEL)
    nb = S // ta
    rows4 = lambda r: r.reshape(PAIRS, 2, nb, ta).transpose(0, 2, 1, 3)
    qh, kh, vt = _attn_prep(q, k, v, flc, b_f, ta)
    o, lse4 = _attn_fwd(qh, kh, vt, ta)

    s5_params = (a_re[0], a_im[0], log_dt[0], b_re[0], b_im[0], c_re[0], c_im[0], d_skip[0])
    (trow, camat, bzmat, al), mats_vjp = jax.vjp(_s5_mats, *s5_params)
    del al
    pw_f, pw_b = _s5_scan_powers(a_re[0], a_im[0], log_dt[0], n_steps)
    ys, xprev, uc = _s5_fwd(u, trow, camat, bzmat, pw_f)

    hsel = (np.arange(WIDTH)[None, :] // 64 == np.arange(HEADS)[:, None]).astype(np.float32)
    (dx2, do, dza, dzb, dga, dgb, dys, dl_row, merged, dmo, ya, dua, yb, dub, yg, dgl,
     db_glu, dgate, dg_final, loss_part) = _mix(o, za, ys, zb, ga, gb, xs, tgt, gate, b_glu, g_final.reshape(1, -1),
                                                w_glu_full, w_up_a_full, w_up_b_full, w_out_full, jnp.asarray(hsel), ts)

    gw_out = _matmul_tn("dw_out", merged, dmo, tw)
    gw_up_a = _matmul_tn("dw_up_a", ya, dua, tw)
    gw_up_b = _matmul_tn("dw_up_b", yb, dub, tw)
    gw_glu = _matmul_tn("dw_glu", yg, dgl, tw)

    du, d_trow, d_camat, d_bzmat, dal2 = _s5_bwd(uc, dys, xprev, trow, camat, bzmat, pw_b)
    d_al = jnp.concatenate([dal2[:, 0, :STATE] + dal2[:, 0, STATE:], dal2[:, 1, STATE:] - dal2[:, 1, :STATE]], axis=-1)
    gs5 = mats_vjp((d_trow, d_camat, d_bzmat, d_al))

    dl4 = rows4(dl_row)
    early = [p.astype(MXU_DTYPE) for p in (gw_glu.reshape(N_DEV, -1, WIDTH),
                                           gw_up_a.reshape(WIDTH, N_DEV, -1).transpose(1, 0, 2),
                                           gw_up_b.reshape(WIDTH, N_DEV, -1).transpose(1, 0, 2),
                                           gw_out.reshape(N_DEV, -1, D_MODEL))]
    dq, dk, dv, dfq4, dfk4, *early_recv = _attn_bwd(qh, do, kh, v, lse4, dl4, early, ta)
    d_fcol = dfq4.transpose(0, 2, 1, 3).reshape(HEADS, S).T - dfk4.transpose(1, 0, 2).reshape(S, HEADS)
    dfl, db_f = _fgate_bwd(d_fcol, flc, b_f, ta)

    gw_in = jnp.concatenate(_matmul_tn_multi("dw_in_narrow", hb, [dq, dk, dv, dfl, dza, du, dzb], tw // 2)
                            + _matmul_tn_multi("dw_in_wide", hb, [dga, dgb], tw // 2), axis=1)

    planes = [gw_in.reshape(D_MODEL, N_DEV, -1).transpose(1, 0, 2).astype(MXU_DTYPE)]
    from_sib = _comm_pair(planes)
    core = lax.axis_index("c")
    chip_planes = []
    for name, p, s in zip(("w_in",), planes, from_sib):
        tr = 256 if s.shape[1] % 256 == 0 else s.shape[1]
        chip_planes.append(_pair_sum("pair_sum_" + name, p, s, core, tr))
    ready_parts = [db_f, gs5[0], gs5[1], gs5[2], gs5[3], gs5[4], gs5[5], gs5[6], gs5[7], db_glu, dg_final, loss_part]
    n_ready = sum(int(np.prod(p.shape)) for p in ready_parts)
    rows_ready = -(-n_ready // (8 * 128)) * 8
    grad_x, dshift, dscale, dg_norm, recv, ready_all = _proj_bwd(
        dq, dk, dv, dza, du, dzb, dga, dgb, dfl, xs, dx2, shift, scale, g_norm, w_main, w_ft, chip_planes,
        [_pack_small(ready_parts, rows_ready)], ts)
    recv = [recv]

    dmod = jnp.concatenate([dshift, dscale, dgate], axis=1)
    rows_late = 4 * D_MODEL // 128
    late_all = _comm_small(_pack_small([dmod, dg_norm], rows_late))
    small_all = jnp.concatenate([late_all, ready_all], axis=1)
    rows = rows_late + rows_ready

    grads, deltas, new_m, new_v = {}, {}, {}, {}

    def put(name, res, shape):
        grads[name], deltas[name], new_m[name], new_v[name] = [r.reshape(shape) for r in res]

    names = ("w_in", "w_glu", "w_up_a", "w_up_b", "w_out")
    for name, pr in zip(names, recv + early_recv):
        w2 = weights[name][0]
        tr = 256 if w2.shape[0] % 256 == 0 else w2.shape[0]
        put(name, _adamw("adamw_" + name, pr, w2, mom_m[name][0], mom_v[name][0], tr), weights[name].shape)
    cols = w_ada.shape[2]
    dmod_all = small_all[:, :24, :].reshape(N_DEV, 3 * D_MODEL)
    dmod_cols = lax.dynamic_slice_in_dim(dmod_all, me * cols, cols, axis=1)
    g_wada = _wada_grad(c_all, dmod_cols)
    put("w_ada", _adamw("adamw_w_ada", g_wada, w_ada[0], m_w_ada[0], v_w_ada[0], 256), w_ada.shape)
    pack = lambda d: _pack_small([d[n] for n in SMALL_ORDER] + [jnp.zeros((1,), F32)], rows)
    res_small = _adamw("adamw_small", small_all, pack(weights), pack(mom_m), pack(mom_v), rows)
    flat = [r.reshape(-1) for r in res_small]
    off = 0
    for name in SMALL_ORDER:
        shape = weights[name].shape
        size = int(np.prod(shape))
        put(name, [f[off:off + size] for f in flat], shape)
        off += size
    loss = flat[0][off]

    return (loss, grad_x[None], *[grads[n] for n in ALL_ORDER], *[deltas[n] for n in ALL_ORDER],
            *[new_m[n] for n in ALL_ORDER], *[new_v[n] for n in ALL_ORDER])
```

```python
import math

import jax
import jax.numpy as jnp
import numpy as np
from jax import lax
from jax.experimental import pallas as pl
from jax.experimental.pallas import tpu as pltpu

F32 = jnp.float32
MXU_DTYPE = jnp.bfloat16
HI = lax.Precision.HIGHEST

N_DEV = 8
D_MODEL = 1024
WIDTH = 512
HEADS = 8
PAIRS = HEADS // 2
GROUPS = 32
STATE = 64
CG = 16
CHUNK = 16
EPS = 1e-6
NEG = float(np.finfo(np.float32).min)

ADAM_LR = 0.001
ADAM_B1 = 0.9
ADAM_B2 = 0.999
ADAM_EPS = 1e-08
ADAM_WD = 0.01
ADAM_STEP = 10

VMEM_BIG = 56 * 1024 * 1024
VMEM_MID = 40 * 1024 * 1024

OFF_F = 3 * WIDTH
PROJ_WIDTH = 5128
M_Q, M_K, M_V, M_ZA, M_U, M_ZB, M_GA, M_GB = 0, 512, 1024, 1536, 2048, 2560, 3072, 4096


def _mm(a, b):
    return jnp.dot(a.astype(MXU_DTYPE), b.astype(MXU_DTYPE), preferred_element_type=F32)


def _mm_nt(a, b):
    return lax.dot_general(a.astype(MXU_DTYPE), b.astype(MXU_DTYPE), (((1,), (1,)), ((), ())),
                           preferred_element_type=F32)


def _mm_tn(a, b):
    return lax.dot_general(a.astype(MXU_DTYPE), b.astype(MXU_DTYPE), (((0,), (0,)), ((), ())),
                           preferred_element_type=F32)


def _mm32(a, b):
    return jnp.dot(a, b, precision=HI, preferred_element_type=F32)


def _mm32_nt(a, b):
    return lax.dot_general(a, b, (((1,), (1,)), ((), ())), precision=HI, preferred_element_type=F32)


def _mm32_tn(a, b):
    return lax.dot_general(a, b, (((0,), (0,)), ((), ())), precision=HI, preferred_element_type=F32)


S5_PRECISION = lax.Precision.HIGH


def _s5mm(a, b):
    return jnp.dot(a, b, precision=S5_PRECISION, preferred_element_type=F32)


def _s5mm_nt(a, b):
    return lax.dot_general(a, b, (((1,), (1,)), ((), ())), precision=S5_PRECISION, preferred_element_type=F32)


def _s5mm_tn(a, b):
    return lax.dot_general(a, b, (((0,), (0,)), ((), ())), precision=S5_PRECISION, preferred_element_type=F32)


def _sigmoid(x):
    return 1.0 / (1.0 + jnp.exp(-x))


def _params(sem=None, vmem=None):
    kw = {}
    if sem is not None:
        kw["dimension_semantics"] = sem
    if vmem is not None:
        kw["vmem_limit_bytes"] = vmem
    return pltpu.CompilerParams(**kw)


def _my_index():
    return 4 * lax.axis_index("x") + 2 * lax.axis_index("y") + lax.axis_index("c")


def _dev(p):
    return (p // 4, (p // 2) % 2, p % 2)


ANY = pl.BlockSpec(memory_space=pl.ANY)
VMEM = pl.BlockSpec(memory_space=pltpu.VMEM)
MESH = pl.DeviceIdType.MESH


def _comm_in(c, w_ada, b_ada8, shards):
    n = len(shards)
    cols = w_ada.shape[1]

    def body(c_ref, wada_ref, bada_ref, *rest):
        srcs = rest[:n]
        mod_ref, call_ref = rest[n], rest[n + 1]
        dsts = rest[n + 2:2 * n + 2]
        modp, wsend, wrecv, wloc, csend, crecv, msend, mrecv = rest[2 * n + 2:]
        me = _my_index()

        x, y, cc = lax.axis_index("x"), lax.axis_index("y"), lax.axis_index("c")
        here, sib = (x, y, cc), (x, y, 1 - cc)
        xn, yn, dg = (1 - x, y, cc), (x, 1 - y, cc), (1 - x, 1 - y, cc)
        half = srcs[0].shape[0] // 2
        parts = [(0, pl.ds(0, half)), (0, pl.ds(half, half))] + [(a, None) for a in range(1, n)]
        via_y = [i % 2 == 0 for i in range(len(parts))]

        def wcopy(i, k, block, to, own=False):
            a, rs = parts[i]
            dst = dsts[a].at[4 * block[0] + 2 * block[1] + block[2]]
            src = srcs[a] if own else dst
            if rs is not None:
                src, dst = src.at[rs], dst.at[rs]
            return pltpu.make_async_remote_copy(src_ref=src, dst_ref=dst,
                                                send_sem=wsend.at[i, k], recv_sem=wrecv.at[i, k],
                                                device_id=to, device_id_type=MESH)

        def ccopy(src_dev, d, to):
            return pltpu.make_async_remote_copy(src_ref=c_ref, dst_ref=call_ref.at[pl.ds(src_dev, 1)],
                                                send_sem=csend.at[d], recv_sem=crecv.at[src_dev],
                                                device_id=_dev(to), device_id_type=MESH)

        def mcopy(src_dev, d, to):
            return pltpu.make_async_remote_copy(src_ref=modp.at[pl.ds(to, 1)], dst_ref=mod_ref.at[pl.ds(src_dev, 1)],
                                                send_sem=msend.at[d], recv_sem=mrecv.at[src_dev],
                                                device_id=_dev(to), device_id_type=MESH)

        local = [pltpu.make_async_copy(srcs[a], dsts[a].at[me], wloc.at[a]) for a in range(n)]
        for cp in local:
            cp.start()
        peers = [(me + d) % N_DEV for d in range(1, N_DEV)]
        sends = []
        for i in range(len(parts)):
            sends += [wcopy(i, 0, here, sib, own=True), wcopy(i, 1, here, xn, own=True), wcopy(i, 2, here, yn, own=True)]
        for cp in sends:
            cp.start()
        call_ref[pl.ds(me, 1), :] = c_ref[...]
        for d, p in enumerate(peers):
            ccopy(me, d, p).start()
        for d, p in enumerate(peers):
            ccopy(p, d, p).wait_recv()
        modp[...] = _mm32(call_ref[...], wada_ref[...]) + bada_ref[pl.ds(me, 1), :]
        mod_ref[pl.ds(me, 1), :] = modp[pl.ds(me, 1), :]
        for d, p in enumerate(peers):
            mcopy(me, d, p).start()
        for d, p in enumerate(peers):
            mcopy(p, d, p).wait_recv()
        def after(i, k, block, nxt):
            wcopy(i, k, block, here).wait_recv()
            for kk, to in nxt:
                cp = wcopy(i, kk, block, to)
                cp.start()
                sends.append(cp)

        for i in range(len(parts)):
            after(i, 1, xn, [(3, sib)] + ([(5, yn)] if via_y[i] else []))
        for i in range(len(parts)):
            after(i, 2, yn, [(4, sib)] + ([] if via_y[i] else [(6, xn)]))
        for i in range(len(parts)):
            after(i, 5 if via_y[i] else 6, dg, [(7, sib)])
        for i in range(len(parts)):
            wcopy(i, 0, sib, here).wait_recv()
            for k, block in ((3, xn), (4, yn), (7, dg)):
                wcopy(i, k, (block[0], block[1], 1 - cc), here).wait_recv()
        for cp in sends:
            cp.wait_send()
        for d, p in enumerate(peers):
            ccopy(me, d, p).wait_send()
            mcopy(me, d, p).wait_send()
        for cp in local:
            cp.wait()

    out_shape = ([jax.ShapeDtypeStruct((N_DEV, cols), F32), jax.ShapeDtypeStruct((N_DEV, D_MODEL), F32)]
                 + [jax.ShapeDtypeStruct((N_DEV,) + s.shape, s.dtype) for s in shards])
    res = pl.pallas_call(
        body, name="comm_in", out_shape=out_shape,
        in_specs=[VMEM, VMEM, VMEM] + [ANY] * n,
        out_specs=[VMEM, VMEM] + [ANY] * n,
        scratch_shapes=[pltpu.VMEM((N_DEV, cols), F32),
                        pltpu.SemaphoreType.DMA((n + 1, N_DEV)), pltpu.SemaphoreType.DMA((n + 1, N_DEV)),
                        pltpu.SemaphoreType.DMA((n,)),
                        pltpu.SemaphoreType.DMA((N_DEV,)), pltpu.SemaphoreType.DMA((N_DEV,)),
                        pltpu.SemaphoreType.DMA((N_DEV,)), pltpu.SemaphoreType.DMA((N_DEV,))],
        compiler_params=_params(vmem=VMEM_MID),
    )(c, w_ada, b_ada8, *shards)
    return res[0], res[1], list(res[2:])


def _direct_copy(srcs, dsts, send, recv, scatter, a, d, receiving):
    me = _my_index()
    p = (me + d) % N_DEV
    slot = p if receiving else me
    return pltpu.make_async_remote_copy(src_ref=srcs[a].at[p] if scatter else srcs[a], dst_ref=dsts[a].at[slot],
                                        send_sem=send.at[a, d], recv_sem=recv.at[a, slot],
                                        device_id=_dev(p), device_id_type=MESH)


def _hosted_exchange(first, last, srcs, dsts, send, recv, loc, scatter):
    me = _my_index()
    n = len(srcs)
    local = lambda a: pltpu.make_async_copy(srcs[a].at[me] if scatter else srcs[a], dsts[a].at[me], loc.at[a])
    pairs = [(a, d) for d in range(1, N_DEV) for a in range(n)]

    @pl.when(first)
    def _():
        for a in range(n):
            local(a).start()
        for a, d in pairs:
            _direct_copy(srcs, dsts, send, recv, scatter, a, d, False).start()

    @pl.when(last)
    def _():
        for a, d in pairs:
            _direct_copy(srcs, dsts, send, recv, scatter, a, d, True).wait_recv()
            _direct_copy(srcs, dsts, send, recv, scatter, a, d, False).wait_send()
        for a in range(n):
            local(a).wait()


def _exchange_scratch(n):
    return [pltpu.SemaphoreType.DMA((n, N_DEV)), pltpu.SemaphoreType.DMA((n, N_DEV)), pltpu.SemaphoreType.DMA((n,))]


def _hosted_chip_exchange(first, last, srcs, dsts, send, recv, loc):
    x, y, cc = lax.axis_index("x"), lax.axis_index("y"), lax.axis_index("c")
    mine = 2 * x + y
    chips = [(1 - x, y), (x, 1 - y), (1 - x, 1 - y)]
    n = len(srcs)

    def copy(a, j, sending):
        chip = chips[j]
        there = 2 * chip[0] + chip[1]
        return pltpu.make_async_remote_copy(src_ref=srcs[a].at[there], dst_ref=dsts[a].at[mine if sending else there],
                                            send_sem=send.at[a, j], recv_sem=recv.at[a, j],
                                            device_id=(*chip, cc), device_id_type=MESH)

    local = lambda a: pltpu.make_async_copy(srcs[a].at[mine], dsts[a].at[mine], loc.at[a])

    @pl.when(first)
    def _():
        for a in range(n):
            local(a).start()
            for j in range(len(chips)):
                copy(a, j, True).start()

    @pl.when(last)
    def _():
        for a in range(n):
            for j in range(len(chips)):
                copy(a, j, False).wait_recv()
                copy(a, j, True).wait_send()
            local(a).wait()


N_CHIP = 4


def _comm_pair(planes):
    n = len(planes)

    def body(*rest):
        srcs, dsts = rest[:n], rest[n:2 * n]
        send, recv = rest[2 * n:]
        x, y, cc = lax.axis_index("x"), lax.axis_index("y"), lax.axis_index("c")
        copies = [pltpu.make_async_remote_copy(src_ref=srcs[a].at[2 * ch + 1 - cc], dst_ref=dsts[a].at[ch],
                                               send_sem=send.at[a, ch], recv_sem=recv.at[a, ch],
                                               device_id=(x, y, 1 - cc), device_id_type=MESH)
                  for a in range(n) for ch in range(N_CHIP)]
        for cp in copies:
            cp.start()
        for cp in copies:
            cp.wait()

    out_shape = [jax.ShapeDtypeStruct((N_CHIP,) + p.shape[1:], p.dtype) for p in planes]
    return pl.pallas_call(
        body, name="comm_pair", out_shape=out_shape, in_specs=[ANY] * n, out_specs=[ANY] * n,
        scratch_shapes=[pltpu.SemaphoreType.DMA((n, N_CHIP)), pltpu.SemaphoreType.DMA((n, N_CHIP))],
    )(*planes)


def _pair_sum(name, planes, from_sib, core, tr):
    _, R, C = from_sib.shape

    def body(core_ref, a_ref, b_ref, o_ref):
        del core_ref
        o_ref[...] = (a_ref[...].astype(F32) + b_ref[...].astype(F32)).astype(o_ref.dtype)

    blk = pl.BlockSpec((None, tr, C), lambda i, j, c: (i, j, 0))
    grid_spec = pltpu.PrefetchScalarGridSpec(
        num_scalar_prefetch=1, grid=(N_CHIP, R // tr),
        in_specs=[pl.BlockSpec((None, tr, C), lambda i, j, c: (2 * i + c[0], j, 0)), blk], out_specs=blk)
    return pl.pallas_call(
        body, name=name, grid_spec=grid_spec, out_shape=jax.ShapeDtypeStruct(from_sib.shape, from_sib.dtype),
        compiler_params=_params(("parallel", "parallel"), VMEM_MID),
    )(core.reshape(1).astype(jnp.int32), planes, from_sib)


def _comm_small(small):
    rows = small.shape[0]
    cut = (rows // 16) * 8

    def body(small_ref, sall_ref, ssend, srecv, sloc):
        me = _my_index()
        x, y, cc = lax.axis_index("x"), lax.axis_index("y"), lax.axis_index("c")
        here, sib = (x, y, cc), (x, y, 1 - cc)
        xn, yn, dg = (1 - x, y, cc), (x, 1 - y, cc), (1 - x, 1 - y, cc)
        sparts = [pl.ds(0, cut), pl.ds(cut, rows - cut)]
        via_y = [True, False]

        def scopy(i, k, block, to, own=False):
            dst = sall_ref.at[4 * block[0] + 2 * block[1] + block[2]].at[sparts[i]]
            src = small_ref.at[sparts[i]] if own else dst
            return pltpu.make_async_remote_copy(src_ref=src, dst_ref=dst, send_sem=ssend.at[i, k], recv_sem=srecv.at[i, k],
                                                device_id=to, device_id_type=MESH)

        local = pltpu.make_async_copy(small_ref, sall_ref.at[me], sloc)
        local.start()
        sends = []
        for i in range(len(sparts)):
            sends += [scopy(i, 0, here, sib, own=True), scopy(i, 1, here, xn, own=True), scopy(i, 2, here, yn, own=True)]
        for cp in sends:
            cp.start()

        def after(i, k, block, nxt):
            scopy(i, k, block, here).wait_recv()
            for kk, to in nxt:
                cp = scopy(i, kk, block, to)
                cp.start()
                sends.append(cp)

        for i in range(len(sparts)):
            after(i, 1, xn, [(3, sib)] + ([(5, yn)] if via_y[i] else []))
        for i in range(len(sparts)):
            after(i, 2, yn, [(4, sib)] + ([] if via_y[i] else [(6, xn)]))
        for i in range(len(sparts)):
            after(i, 5 if via_y[i] else 6, dg, [(7, sib)])
        for i in range(len(sparts)):
            scopy(i, 0, sib, here).wait_recv()
            for k, block in ((3, xn), (4, yn), (7, dg)):
                scopy(i, k, (block[0], block[1], 1 - cc), here).wait_recv()
        for cp in sends:
            cp.wait_send()
        local.wait()

    return pl.pallas_call(
        body, name="comm_small", out_shape=jax.ShapeDtypeStruct((N_DEV,) + small.shape, small.dtype),
        in_specs=[ANY], out_specs=ANY,
        scratch_shapes=[pltpu.SemaphoreType.DMA((2, N_DEV)), pltpu.SemaphoreType.DMA((2, N_DEV)),
                        pltpu.SemaphoreType.DMA(())],
    )(small)


def _proj_fwd(x, shift, scale, g_norm, w_main, w_f, late, ts):
    S = x.shape[0]
    nl = len(late)

    def body(x_ref, sh_ref, sc_ref, gn_ref, w_ref, wf_ref, *rest):
        (q_ref, k_ref, v_ref, za_ref, u_ref, zb_ref, ga_ref, gb_ref, flc_ref, h_ref) = rest[nl:nl + 10]
        i = pl.program_id(0)
        _hosted_exchange(i == 0, i == pl.num_programs(0) - 1, rest[:nl], rest[nl + 10:2 * nl + 10],
                         *rest[2 * nl + 10:], scatter=False)
        xv = x_ref[...]
        r = lax.rsqrt(jnp.mean(xv * xv, axis=-1, keepdims=True) + EPS)
        h = (xv * r) * gn_ref[...] * (1.0 + sc_ref[...]) + sh_ref[...]
        hb = h.astype(MXU_DTYPE)
        h_ref[...] = hb

        def seg(off, n):
            return jnp.dot(hb, w_ref[:, off:off + n], preferred_element_type=F32)

        q_ref[...] = (seg(M_Q, WIDTH) * 0.125).astype(q_ref.dtype)
        k_ref[...] = seg(M_K, WIDTH).astype(k_ref.dtype)
        v_ref[...] = seg(M_V, WIDTH).astype(v_ref.dtype)
        za_ref[...] = seg(M_ZA, WIDTH)
        u_ref[...] = seg(M_U, WIDTH)
        zb_ref[...] = seg(M_ZB, WIDTH)
        ga_ref[...] = seg(M_GA, D_MODEL)
        gb_ref[...] = seg(M_GB, D_MODEL)
        flc_ref[...] = _mm32(h, wf_ref[...])

    row = lambda n: pl.BlockSpec((ts, n), lambda i: (i, 0))
    full = lambda a: pl.BlockSpec(a.shape, lambda i: (0,) * a.ndim)
    sds = jax.ShapeDtypeStruct
    return pl.pallas_call(
        body, name="proj_fwd", grid=(S // ts,),
        in_specs=[row(D_MODEL), full(shift), full(scale), full(g_norm), full(w_main), full(w_f)] + [ANY] * nl,
        out_specs=[row(WIDTH)] * 6 + [row(D_MODEL)] * 2 + [row(HEADS), row(D_MODEL)] + [ANY] * nl,
        out_shape=[sds((S, WIDTH), MXU_DTYPE)] * 3 + [sds((S, WIDTH), F32)] * 3 + [sds((S, D_MODEL), F32)] * 2
                  + [sds((S, HEADS), F32), sds((S, D_MODEL), MXU_DTYPE)]
                  + [sds((N_DEV,) + w.shape, w.dtype) for w in late],
        scratch_shapes=_exchange_scratch(nl),
        compiler_params=_params(("arbitrary",), VMEM_BIG),
    )(x, shift, scale, g_norm, w_main, w_f, *late)


def _log_sigmoid(z):
    return jnp.minimum(z, 0.0) - jnp.log(1.0 + jnp.exp(-jnp.abs(z)))


def _fgate_bwd(dfc, flc, bf_row, ts):
    S = flc.shape[0]
    n = S // ts

    def body(df_ref, flc_ref, bfr_ref, dfl_ref, dbf_ref, carry):
        @pl.when(pl.program_id(0) == 0)
        def _():
            carry[...] = jnp.zeros_like(carry)
            dbf_ref[...] = jnp.zeros_like(dbf_ref)

        ri = lax.broadcasted_iota(jnp.int32, (ts, ts), 0)
        ci = lax.broadcasted_iota(jnp.int32, (ts, ts), 1)
        upper = (ci >= ri).astype(F32)
        rc = _mm32(upper, df_ref[...]) + carry[...]
        carry[...] = rc[0:1, :]
        z = flc_ref[...] + bfr_ref[...]
        dfl = rc * _sigmoid(-z)
        dfl_ref[...] = dfl
        dbf_ref[...] += jnp.sum(dfl, axis=0, keepdims=True)

    col = pl.BlockSpec((ts, HEADS), lambda i: (n - 1 - i, 0))
    one = pl.BlockSpec((1, HEADS), lambda i: (0, 0))
    return pl.pallas_call(
        body, name="fgate_bwd", grid=(n,),
        in_specs=[col, col, one], out_specs=[col, one],
        out_shape=[jax.ShapeDtypeStruct((S, HEADS), F32), jax.ShapeDtypeStruct((1, HEADS), F32)],
        scratch_shapes=[pltpu.VMEM((1, HEADS), F32)],
        compiler_params=_params(("arbitrary",)),
    )(dfc, flc, bf_row)


N_EXTRA = 3


def _attn_prep(q, k, v, flc, bf_row, t):
    S = q.shape[0]
    nb = S // t

    def body(q_ref, k_ref, v_ref, flc_ref, bfr_ref, qh_ref, kh_ref, vt_ref, carry):
        @pl.when(pl.program_id(0) == 0)
        def _():
            carry[...] = jnp.zeros_like(carry)

        ri = lax.broadcasted_iota(jnp.int32, (t, t), 0)
        ci = lax.broadcasted_iota(jnp.int32, (t, t), 1)
        f = _mm32((ci <= ri).astype(F32), _log_sigmoid(flc_ref[...] + bfr_ref[...])) + carry[...]
        carry[...] = f[t - 1:t, :]
        lane = lax.broadcasted_iota(jnp.int32, (t, 128), 1)
        for p in range(PAIRS):
            qp = q_ref[:, p * 128:(p + 1) * 128]
            kp = k_ref[:, p * 128:(p + 1) * 128]
            vt_ref[p, 0] = v_ref[:, p * 128:(p + 1) * 128].T
            for h in range(2):
                own = (lane < 64) if h == 0 else (lane >= 64)
                base = 64 if h == 0 else 0
                fh = f[:, 2 * p + h:2 * p + h + 1]
                parts = []
                rest = fh
                for _ in range(N_EXTRA):
                    part = rest.astype(qh_ref.dtype)
                    parts.append(part)
                    rest = rest - part.astype(F32)
                one = jnp.ones((t, 1), qh_ref.dtype)
                eq = jnp.zeros((t, 128), qh_ref.dtype)
                ek = jnp.zeros((t, 128), qh_ref.dtype)
                for j in range(N_EXTRA):
                    eq = jnp.where(lane == base + j, parts[j], eq)
                    eq = jnp.where(lane == base + N_EXTRA + j, one, eq)
                    ek = jnp.where(lane == base + j, one, ek)
                    ek = jnp.where(lane == base + N_EXTRA + j, -parts[j], ek)
                qh_ref[2 * p + h] = jnp.where(own, qp, eq)
                kh_ref[2 * p + h] = jnp.where(own, kp, ek)

    row = pl.BlockSpec((t, WIDTH), lambda i: (i, 0))
    heads = pl.BlockSpec((HEADS, t, 128), lambda i: (0, i, 0))
    return pl.pallas_call(
        body, name="attn_prep", grid=(nb,),
        in_specs=[row, row, row, pl.BlockSpec((t, HEADS), lambda i: (i, 0)), pl.BlockSpec((1, HEADS), lambda i: (0, 0))],
        out_specs=[heads, heads, pl.BlockSpec((PAIRS, 1, 128, t), lambda i: (0, i, 0, 0))],
        out_shape=[jax.ShapeDtypeStruct((HEADS, S, 128), q.dtype), jax.ShapeDtypeStruct((HEADS, S, 128), k.dtype),
                   jax.ShapeDtypeStruct((PAIRS, nb, 128, t), v.dtype)],
        scratch_shapes=[pltpu.VMEM((1, HEADS), F32)],
        compiler_params=_params(("arbitrary",), VMEM_MID),
    )(q, k, v, flc, bf_row)


def _attn_fwd(qh, kh, vt, t):
    S = qh.shape[1]
    nb = S // t

    def body(q_ref, k_ref, vt_ref, o_ref, lse_ref, acc_s):
        qi = pl.program_id(1)
        acc_s[...] = jnp.zeros_like(acc_s)

        def step(ki, nblk, masked, carry):
            m_old, l_old = carry[:2], carry[2:]
            ks = pl.multiple_of(ki * t, t)
            rows = nblk * t
            sts = [_mm_nt(k_ref[h, pl.ds(ks, rows), :], q_ref[h]) for h in range(2)]
            if masked:
                ri = lax.broadcasted_iota(jnp.int32, (rows, t), 0)
                ci = lax.broadcasted_iota(jnp.int32, (rows, t), 1)
                sts = [jnp.where(ci >= ri - (nblk - 1) * t, st, NEG) for st in sts]
            m_new = [jnp.maximum(m_old[h], jnp.max(sts[h], axis=0, keepdims=True)) for h in range(2)]
            alpha = [jnp.exp(m_old[h] - m_new[h]) for h in range(2)]
            pts = [jnp.exp(sts[h] - m_new[h]) for h in range(2)]
            l_new = [alpha[h] * l_old[h] + jnp.sum(pts[h], axis=0, keepdims=True) for h in range(2)]
            for h in range(2):
                pv = _mm(vt_ref[ki], pts[h][:t])
                for b in range(1, nblk):
                    pv = pv + _mm(vt_ref[ki + b], pts[h][b * t:(b + 1) * t])
                acc_s[h] = alpha[h] * acc_s[h] + pv
            return (*m_new, *l_new)

        init = (jnp.full((1, t), -jnp.inf, F32),) * 2 + (jnp.zeros((1, t), F32),) * 2
        carry = lax.fori_loop(0, qi // 2, lambda j, c: step(2 * j, 2, False, c), init)
        m0, m1, l0, l1 = lax.cond(qi % 2 == 1, lambda c: step(qi - 1, 2, True, c), lambda c: step(qi, 1, True, c), carry)
        first = lax.broadcasted_iota(jnp.int32, (128, t), 0) < 64
        o_ref[...] = jnp.where(first, acc_s[0] / l0, acc_s[1] / l1).T
        lse_ref[...] = jnp.concatenate([m0 + jnp.log(l0), m1 + jnp.log(l1)], axis=0)

    return pl.pallas_call(
        body, name="attn_fwd", grid=(PAIRS, nb),
        in_specs=[pl.BlockSpec((2, t, 128), lambda p, i: (p, i, 0)), pl.BlockSpec((2, S, 128), lambda p, i: (p, 0, 0)),
                  pl.BlockSpec((None, nb, 128, t), lambda p, i: (p, 0, 0, 0))],
        out_specs=[pl.BlockSpec((t, 128), lambda p, i: (i, p)), pl.BlockSpec((None, None, 2, t), lambda p, i: (p, i, 0, 0))],
        out_shape=[jax.ShapeDtypeStruct((S, WIDTH), F32), jax.ShapeDtypeStruct((PAIRS, nb, 2, t), F32)],
        scratch_shapes=[pltpu.VMEM((2, 128, t), F32)],
        compiler_params=_params(("parallel", "parallel"), VMEM_MID),
    )(qh, kh, vt)


def _attn_bwd(qh, do, kh, v, lse4, dl4, early, t):
    S = qh.shape[1]
    nb = S // t
    ne = len(early)

    def body(q_ref, do_ref, k_ref, v_ref, lse_ref, dl_ref, *rest):
        dq_ref, dk_ref, dv_ref, dfq_ref, dfk_ref = rest[ne:ne + 5]
        kc_s, vh_s, dk_s, dv_s, dfk_s = rest[2 * ne + 5:2 * ne + 10]
        kj = pl.program_id(1)
        _hosted_exchange((pl.program_id(0) == 0) & (kj == 0), (pl.program_id(0) == PAIRS - 1) & (kj == nb - 1),
                         rest[:ne], rest[ne + 5:2 * ne + 5], *rest[2 * ne + 10:], scatter=True)
        lane = lax.broadcasted_iota(jnp.int32, (t, 128), 1)
        is_a = lane < 64

        @pl.when(kj == 0)
        def _():
            dq_ref[...] = jnp.zeros_like(dq_ref)
            dfq_ref[...] = jnp.zeros_like(dfq_ref)

        vp = v_ref[...]
        zero = jnp.zeros_like(vp)
        kc_s[0] = jnp.where(is_a, k_ref[0], zero.astype(kc_s.dtype))
        kc_s[1] = jnp.where(is_a, zero.astype(kc_s.dtype), k_ref[1])
        vh_s[0] = jnp.where(is_a, vp, zero)
        vh_s[1] = jnp.where(is_a, zero, vp)
        dk_s[...] = jnp.zeros_like(dk_s)
        dv_s[...] = jnp.zeros_like(dv_s)
        dfk_s[...] = jnp.zeros_like(dfk_s)

        def step(qi, masked):
            qs = pl.multiple_of(qi * t, t)
            dob = do_ref[pl.ds(qs, t), :]
            lse = lse_ref[qi]
            dl = dl_ref[qi]
            zq = jnp.zeros_like(dob)
            over_keys = []
            for h in range(2):
                sel = is_a if h == 0 else jnp.logical_not(is_a)
                qb = q_ref[h, pl.ds(qs, t), :]
                st = _mm_nt(k_ref[h], qb) - lse[h:h + 1, :]
                if masked:
                    ri = lax.broadcasted_iota(jnp.int32, (t, t), 0)
                    ci = lax.broadcasted_iota(jnp.int32, (t, t), 1)
                    st = jnp.where(ci >= ri, st, NEG)
                pt = jnp.exp(st)
                dv_s[...] += _mm(pt, jnp.where(sel, dob, zq))
                dpt = _mm_nt(vh_s[h], dob)
                dst = pt * (dpt - dl[h:h + 1, :])
                dfk_s[h] += jnp.sum(dst, axis=1, keepdims=True)
                over_keys.append(jnp.sum(dst, axis=0, keepdims=True))
                dk_s[...] += _mm(dst, jnp.where(sel, qb, jnp.zeros_like(qb)))
                dq_ref[pl.ds(qs, t), :] += _mm_tn(dst, kc_s[h])
            dfq_ref[qi] += jnp.concatenate(over_keys, axis=0)

        step(kj, True)

        def loop_body(qi, carry):
            step(qi, False)
            return carry

        lax.fori_loop(kj + 1, nb, loop_body, 0)
        dk_ref[...] = dk_s[...].astype(dk_ref.dtype)
        dv_ref[...] = dv_s[...].astype(dv_ref.dtype)
        dfk_ref[...] = jnp.where(lax.broadcasted_iota(jnp.int32, (t, 2), 1) == 0, dfk_s[0], dfk_s[1])

        @pl.when(kj == nb - 1)
        def _():
            dq_ref[...] = dq_ref[...] * 0.125

    blk = pl.BlockSpec((t, 128), lambda p, j: (j, p))
    res = pl.BlockSpec((S, 128), lambda p, j: (0, p))
    rows4 = pl.BlockSpec((None, nb, 2, t), lambda p, j: (p, 0, 0, 0))
    cols4 = pl.BlockSpec((None, t, 2), lambda p, j: (p, j, 0))
    return pl.pallas_call(
        body, name="attn_bwd", grid=(PAIRS, nb),
        in_specs=[pl.BlockSpec((2, S, 128), lambda p, j: (p, 0, 0)), res,
                  pl.BlockSpec((2, t, 128), lambda p, j: (p, j, 0)), blk, rows4, rows4] + [ANY] * ne,
        out_specs=[res, blk, blk, rows4, cols4] + [ANY] * ne,
        out_shape=[jax.ShapeDtypeStruct((S, WIDTH), F32), jax.ShapeDtypeStruct((S, WIDTH), MXU_DTYPE),
                   jax.ShapeDtypeStruct((S, WIDTH), MXU_DTYPE), jax.ShapeDtypeStruct((PAIRS, nb, 2, t), F32),
                   jax.ShapeDtypeStruct((PAIRS, S, 2), F32)] + [jax.ShapeDtypeStruct(e.shape, e.dtype) for e in early],
        scratch_shapes=[pltpu.VMEM((2, t, 128), kh.dtype), pltpu.VMEM((2, t, 128), v.dtype),
                        pltpu.VMEM((t, 128), F32), pltpu.VMEM((t, 128), F32), pltpu.VMEM((2, t, 1), F32)]
                       + _exchange_scratch(ne),
        compiler_params=_params(("arbitrary", "arbitrary"), VMEM_MID),
    )(qh, do, kh, v, lse4, dl4, *early)


def _s5_mats(a_re, a_im, log_dt, b_re, b_im, c_re, c_im, d_skip):
    Lc = CHUNK
    dt = jnp.exp(log_dt)[:, None]
    lr, li = a_re * dt, a_im * dt

    def apow(n):
        n = jnp.asarray(n, F32)[None, :, None]
        mag = jnp.exp(n * lr[:, None, :])
        ang = n * li[:, None, :]
        return mag * jnp.cos(ang), mag * jnp.sin(ang)

    ar, ai = apow([1.0])
    ar, ai = ar[:, 0], ai[:, 0]
    den = a_re * a_re + a_im * a_im
    nr, ni = ar - 1.0, ai
    fr = (nr * a_re + ni * a_im) / den
    fi = (ni * a_re - nr * a_im) / den
    bbr = fr[:, :, None] * b_re - fi[:, :, None] * b_im
    bbi = fr[:, :, None] * b_im + fi[:, :, None] * b_re
    steps = np.arange(Lc, dtype=np.float32)
    pr, pi = apow(steps)
    car = c_re[:, None] * pr[:, :, None, :] - c_im[:, None] * pi[:, :, None, :]
    cai = c_re[:, None] * pi[:, :, None, :] + c_im[:, None] * pr[:, :, None, :]
    kern = (jnp.einsum('glcp,gpd->glcd', car, bbr, precision=HI)
            - jnp.einsum('glcp,gpd->glcd', cai, bbi, precision=HI))
    skip = d_skip.reshape(GROUPS, CG)[:, :, None] * jnp.eye(CG, dtype=F32)[None]
    kern = kern.at[:, 0].add(skip)
    trow = kern.transpose(0, 3, 1, 2).reshape(GROUPS, CG, Lc * CG)
    p1r, p1i = apow(steps + 1.0)
    cr = c_re[:, None] * p1r[:, :, None, :] - c_im[:, None] * p1i[:, :, None, :]
    ci = c_re[:, None] * p1i[:, :, None, :] + c_im[:, None] * p1r[:, :, None, :]
    to_rows = lambda m: m.transpose(0, 3, 1, 2).reshape(GROUPS, STATE, Lc * CG)
    camat = jnp.concatenate([to_rows(cr), -to_rows(ci)], axis=1)
    qr, qi = apow(Lc - 1.0 - steps)
    zr = qr[:, :, None, :] * bbr.transpose(0, 2, 1)[:, None] - qi[:, :, None, :] * bbi.transpose(0, 2, 1)[:, None]
    zi = qr[:, :, None, :] * bbi.transpose(0, 2, 1)[:, None] + qi[:, :, None, :] * bbr.transpose(0, 2, 1)[:, None]
    bzmat = jnp.concatenate([zr, zi], axis=-1).reshape(GROUPS, Lc * CG, 2 * STATE)
    lr_, li_ = apow([float(Lc)])
    al = jnp.concatenate([lr_[:, 0], li_[:, 0]], axis=-1)
    return trow, camat, bzmat, al


def _s5_scan_powers(a_re, a_im, log_dt, n_steps):
    dt = jnp.exp(log_dt)[:, None]
    lr, li = a_re * dt, a_im * dt
    n = (CHUNK * 2.0 ** np.arange(n_steps)).astype(np.float32)[None, :, None]
    mag = jnp.exp(n * lr[:, None, :])
    pr, pi = mag * jnp.cos(n * li[:, None, :]), mag * jnp.sin(n * li[:, None, :])
    fwd = jnp.stack([jnp.concatenate([pr, pr], -1), jnp.concatenate([-pi, pi], -1)], axis=2)
    bwd = jnp.stack([jnp.concatenate([pr, pr], -1), jnp.concatenate([pi, -pi], -1)], axis=2)
    return fwd, bwd


def _shift_rows(x, sh, down):
    n = x.shape[0]
    ri = lax.broadcasted_iota(jnp.int32, x.shape, 0)
    if down:
        return jnp.where(ri >= sh, pltpu.roll(x, sh, 0), 0.0)
    return jnp.where(ri < n - sh, pltpu.roll(x, n - sh, 0), 0.0)


GPB = 128 // CG


def _lane_transpose(arrs):
    lane = lax.broadcasted_iota(jnp.int32, arrs[0].shape, 1)
    arrs = list(arrs)
    k = GPB // 2
    while k >= 1:
        hi = ((lane // CG) & k) != 0
        new = list(arrs)
        for i in range(GPB):
            if i & k:
                continue
            lo_arr, hi_arr = arrs[i], arrs[i + k]
            new[i] = jnp.where(hi, pltpu.roll(hi_arr, CG * k, 1), lo_arr)
            new[i + k] = jnp.where(hi, hi_arr, pltpu.roll(lo_arr, 128 - CG * k, 1))
        arrs = new
        k //= 2
    return arrs


def _gather_block(ref, dst, nch):
    for half in range(CHUNK // GPB):
        outs = _lane_transpose([ref[pl.ds(half * GPB + l8, nch, stride=CHUNK), :] for l8 in range(GPB)])
        for g in range(GPB):
            dst[half, g] = outs[g]


def _scatter_block(src, ref, nch):
    for half in range(CHUNK // GPB):
        outs = _lane_transpose([src[half, g] for g in range(GPB)])
        for l8 in range(GPB):
            ref[pl.ds(half * GPB + l8, nch, stride=CHUNK), :] = outs[l8]


def _toeplitz(trow):
    lane = lax.broadcasted_iota(jnp.int32, (CG, 128), 1)
    x0, x1 = trow[:, :128], trow[:, 128:]
    zero = jnp.zeros_like(x0)
    rows = []
    for s in range(CHUNK):
        sh = (CG * s) % 128
        r0 = pltpu.roll(x0, sh, 1) if sh else x0
        r1 = pltpu.roll(x1, sh, 1) if sh else x1
        if CG * s < 128:
            rows.append(jnp.concatenate([jnp.where(lane >= sh, r0, zero), jnp.where(lane >= sh, r1, r0)], axis=1))
        else:
            rows.append(jnp.concatenate([zero, jnp.where(lane >= sh, r0, zero)], axis=1))
    return jnp.concatenate(rows, axis=0)


def _toeplitz_adjoint(dt):
    lane = lax.broadcasted_iota(jnp.int32, (CG, 128), 1)
    acc0 = jnp.zeros((CG, 128), F32)
    acc1 = jnp.zeros((CG, 128), F32)
    for s in range(CHUNK):
        x0, x1 = dt[CG * s:CG * s + CG, :128], dt[CG * s:CG * s + CG, 128:]
        sh = (CG * s) % 128
        keep = 128 - sh
        r0 = pltpu.roll(x0, keep, 1) if sh else x0
        r1 = pltpu.roll(x1, keep, 1) if sh else x1
        if CG * s < 128:
            acc0 = acc0 + jnp.where(lane < keep, r0, r1)
            acc1 = acc1 + jnp.where(lane < keep, r1, 0.0)
        else:
            acc0 = acc0 + jnp.where(lane < keep, r1, 0.0)
    return jnp.concatenate([acc0, acc1], axis=1)


def _s5_fwd(u, trow, camat, bzmat, pw):
    S = u.shape[0]
    nch = S // CHUNK
    n_steps = pw.shape[1]

    def body(u_ref, t_ref, ca_ref, bz_ref, pw_ref, y_ref, xp_ref, uc_ref, ub_s, yb_s):
        g = pl.program_id(1)

        @pl.when(g == 0)
        def _():
            _gather_block(u_ref, ub_s, nch)

        uc = jnp.concatenate([ub_s[0, g], ub_s[1, g]], axis=1)
        uc_ref[...] = uc
        x = _s5mm(uc, bz_ref[...])
        for kk in range(n_steps):
            xs = _shift_rows(x, 2 ** kk, True)
            m = pw_ref[kk]
            x = x + m[0:1, :] * xs + m[1:2, :] * pltpu.roll(xs, STATE, 1)
        xp = _shift_rows(x, 1, True)
        xp_ref[...] = xp
        yc = _s5mm(uc, _toeplitz(t_ref[...])) + _s5mm(xp, ca_ref[...])
        yb_s[0, g] = yc[:, :128]
        yb_s[1, g] = yc[:, 128:]

        @pl.when(g == GPB - 1)
        def _():
            _scatter_block(yb_s, y_ref, nch)

    per = lambda a: pl.BlockSpec((None,) + a.shape[1:], lambda b, g: (b * GPB + g,) + (0,) * (a.ndim - 1))
    nat = pl.BlockSpec((S, 128), lambda b, g: (0, b))
    return pl.pallas_call(
        body, name="s5_fwd", grid=(GROUPS // GPB, GPB),
        in_specs=[nat, per(trow), per(camat), per(bzmat), per(pw)],
        out_specs=[nat, pl.BlockSpec((None, nch, 2 * STATE), lambda b, g: (b * GPB + g, 0, 0)),
                   pl.BlockSpec((None, nch, CHUNK * CG), lambda b, g: (b * GPB + g, 0, 0))],
        out_shape=[jax.ShapeDtypeStruct((S, GROUPS * CG), F32), jax.ShapeDtypeStruct((GROUPS, nch, 2 * STATE), F32),
                   jax.ShapeDtypeStruct((GROUPS, nch, CHUNK * CG), F32)],
        scratch_shapes=[pltpu.VMEM((CHUNK // GPB, GPB, nch, 128), F32)] * 2,
        compiler_params=_params(("parallel", "arbitrary"), VMEM_BIG),
    )(u, trow, camat, bzmat, pw)


def _s5_bwd(uc, dy, xp, trow, camat, bzmat, pwc):
    S = dy.shape[0]
    nch = S // CHUNK
    n_steps = pwc.shape[1]

    def body(uc_ref, dy_ref, xp_ref, t_ref, ca_ref, bz_ref, pw_ref, du_ref, dt_ref, dca_ref, dbz_ref, dal_ref,
             dyb_s, dub_s):
        g = pl.program_id(1)

        @pl.when(g == 0)
        def _():
            _gather_block(dy_ref, dyb_s, nch)

        uc = uc_ref[...]
        dyc = jnp.concatenate([dyb_s[0, g], dyb_s[1, g]], axis=1)
        xpv = xp_ref[...]
        dt_ref[...] = _toeplitz_adjoint(_s5mm_tn(uc, dyc))
        dca_ref[...] = _s5mm_tn(xpv, dyc)
        dx = _shift_rows(_s5mm_nt(dyc, ca_ref[...]), 1, False)
        for kk in range(n_steps):
            xs = _shift_rows(dx, 2 ** kk, False)
            m = pw_ref[kk]
            dx = dx + m[0:1, :] * xs + m[1:2, :] * pltpu.roll(xs, STATE, 1)
        dbz_ref[...] = _s5mm_tn(uc, dx)
        dal_ref[0:1, :] = jnp.sum(dx * xpv, axis=0, keepdims=True)
        dal_ref[1:2, :] = jnp.sum(dx * pltpu.roll(xpv, STATE, 1), axis=0, keepdims=True)
        duc = _s5mm_nt(dyc, _toeplitz(t_ref[...])) + _s5mm_nt(dx, bz_ref[...])
        dub_s[0, g] = duc[:, :128]
        dub_s[1, g] = duc[:, 128:]

        @pl.when(g == GPB - 1)
        def _():
            _scatter_block(dub_s, du_ref, nch)

    per = lambda a: pl.BlockSpec((None,) + a.shape[1:], lambda b, g: (b * GPB + g,) + (0,) * (a.ndim - 1))
    nat = pl.BlockSpec((S, 128), lambda b, g: (0, b))
    sds = jax.ShapeDtypeStruct
    mats = [sds(trow.shape, F32), sds(camat.shape, F32), sds(bzmat.shape, F32), sds((GROUPS, 2, 2 * STATE), F32)]
    return pl.pallas_call(
        body, name="s5_bwd", grid=(GROUPS // GPB, GPB),
        in_specs=[per(uc), nat, per(xp), per(trow), per(camat), per(bzmat), per(pwc)],
        out_specs=[nat] + [per(o) for o in mats], out_shape=[sds((S, GROUPS * CG), F32)] + mats,
        scratch_shapes=[pltpu.VMEM((CHUNK // GPB, GPB, nch, 128), F32)] * 2,
        compiler_params=_params(("parallel", "arbitrary"), VMEM_BIG),
    )(uc, dy, xp, trow, camat, bzmat, pwc)


GELU_C0 = math.sqrt(2.0 / math.pi)
GELU_C1 = 0.044715


def _mix(o, za, ys, zb, ga, gb, x, tgt, gate, b_glu, g_final, w_glu, w_up_a, w_up_b, w_out, hsel, ts):
    S = o.shape[0]

    def body(o_ref, za_ref, ys_ref, zb_ref, ga_ref, gb_ref, x_ref, t_ref, gate_ref, bglu_ref, gf_ref,
             wglu_ref, wua_ref, wub_ref, wout_ref, hsel_ref,
             dx2_ref, do_ref, dza_ref, dzb_ref, dga_ref, dgb_ref, dys_ref, dl_ref,
             mg_ref, dmo_ref, ya_ref, dua_ref, yb_ref, dub_ref, yg_ref, dgl_ref,
             dbglu_ref, dgate_ref, dgf_ref, loss_ref):
        @pl.when(pl.program_id(0) == 0)
        def _():
            dbglu_ref[...] = jnp.zeros_like(dbglu_ref)
            dgate_ref[...] = jnp.zeros_like(dgate_ref)
            dgf_ref[...] = jnp.zeros_like(dgf_ref)
            loss_ref[...] = jnp.zeros_like(loss_ref)

        ov = o_ref[...]
        za = za_ref[...]
        sza = _sigmoid(za)
        silu_a = za * sza
        ya = ov * silu_a
        ya_b = ya.astype(ya_ref.dtype)
        ya_ref[...] = ya_b
        ysv = ys_ref[...]
        th = jnp.tanh(GELU_C0 * (ysv + GELU_C1 * ysv * ysv * ysv))
        yg = 0.5 * ysv * (1.0 + th)
        yg_b = yg.astype(yg_ref.dtype)
        yg_ref[...] = yg_b
        sg = _sigmoid(_mm(yg_b, wglu_ref[...]) + bglu_ref[...])
        yb1 = yg * sg
        zb = zb_ref[...]
        szb = _sigmoid(zb)
        silu_b = zb * szb
        yb_b = (yb1 * silu_b).astype(yb_ref.dtype)
        yb_ref[...] = yb_b
        ua = _mm(ya_b, wua_ref[...])
        ub = _mm(yb_b, wub_ref[...])
        sa = _sigmoid(ga_ref[...])
        sb = _sigmoid(gb_ref[...])
        merged_b = (sa * ua + sb * ub).astype(mg_ref.dtype)
        mg_ref[...] = merged_b
        mo = _mm(merged_b, wout_ref[...])
        gate_v = gate_ref[...]
        x2 = x_ref[...] + gate_v * mo
        r2 = lax.rsqrt(jnp.mean(x2 * x2, axis=-1, keepdims=True) + EPS)
        x2n = x2 * r2
        gf = gf_ref[...]
        diff = x2n * gf - t_ref[...]
        loss_ref[...] += jnp.sum(jnp.sum(diff * diff, axis=-1, keepdims=True), axis=0, keepdims=True) * (0.5 / D_MODEL)
        dgf_ref[...] += jnp.sum(diff * x2n, axis=0, keepdims=True) * (1.0 / D_MODEL)
        dyg = diff * (gf * (1.0 / D_MODEL))
        dx2 = r2 * (dyg - x2n * jnp.mean(dyg * x2n, axis=-1, keepdims=True))
        dx2_ref[...] = dx2
        dgate_ref[...] += jnp.sum(dx2 * mo, axis=0, keepdims=True)
        dmo_b = (dx2 * gate_v).astype(dmo_ref.dtype)
        dmo_ref[...] = dmo_b
        dmerged = _mm_nt(dmo_b, wout_ref[...])
        dua = dmerged * sa
        dub = dmerged * sb
        dua_b = dua.astype(dua_ref.dtype)
        dub_b = dub.astype(dub_ref.dtype)
        dua_ref[...] = dua_b
        dub_ref[...] = dub_b
        dga_ref[...] = (dua * ua * (1.0 - sa)).astype(dga_ref.dtype)
        dgb_ref[...] = (dub * ub * (1.0 - sb)).astype(dgb_ref.dtype)
        dya = _mm_nt(dua_b, wua_ref[...])
        dyb = _mm_nt(dub_b, wub_ref[...])
        dov = dya * silu_a
        do_ref[...] = dov.astype(do_ref.dtype)
        dl_ref[...] = _mm32_nt(hsel_ref[...], dov * ov)
        dza_ref[...] = (dya * ov * (sza + silu_a * (1.0 - sza))).astype(dza_ref.dtype)
        dyb1 = dyb * silu_b
        dzb_ref[...] = (dyb * yb1 * (szb + silu_b * (1.0 - szb))).astype(dzb_ref.dtype)
        dgl = dyb1 * yg * sg * (1.0 - sg)
        dbglu_ref[...] += jnp.sum(dgl, axis=0, keepdims=True)
        dgl_b = dgl.astype(dgl_ref.dtype)
        dgl_ref[...] = dgl_b
        dyg2 = dyb1 * sg + _mm_nt(dgl_b, wglu_ref[...])
        dgelu = 0.5 * (1.0 + th) + 0.5 * ysv * (1.0 - th * th) * GELU_C0 * (1.0 + 3.0 * GELU_C1 * ysv * ysv)
        dys_ref[...] = dyg2 * dgelu

    row = lambda n: pl.BlockSpec((ts, n), lambda i: (i, 0))
    full = lambda a: pl.BlockSpec(a.shape, lambda i: (0,) * a.ndim)
    vec = lambda n: pl.BlockSpec((1, n), lambda i: (0, 0))
    sds = jax.ShapeDtypeStruct
    W, Dm = WIDTH, D_MODEL
    return pl.pallas_call(
        body, name="mix", grid=(S // ts,),
        in_specs=[row(W), row(W), row(W), row(W), row(Dm), row(Dm), row(Dm), row(Dm),
                  full(gate), full(b_glu), full(g_final), full(w_glu), full(w_up_a), full(w_up_b), full(w_out), full(hsel)],
        out_specs=[row(Dm), row(W), row(W), row(W), row(Dm), row(Dm), row(W), pl.BlockSpec((HEADS, ts), lambda i: (0, i)),
                   row(Dm), row(Dm), row(W), row(Dm), row(W), row(Dm), row(W), row(W),
                   vec(W), vec(Dm), vec(Dm), vec(1)],
        out_shape=[sds((S, Dm), F32), sds((S, W), MXU_DTYPE), sds((S, W), MXU_DTYPE), sds((S, W), MXU_DTYPE),
                   sds((S, Dm), MXU_DTYPE), sds((S, Dm), MXU_DTYPE), sds((S, W), F32), sds((HEADS, S), F32),
                   sds((S, Dm), MXU_DTYPE), sds((S, Dm), MXU_DTYPE), sds((S, W), MXU_DTYPE), sds((S, Dm), MXU_DTYPE),
                   sds((S, W), MXU_DTYPE), sds((S, Dm), MXU_DTYPE), sds((S, W), MXU_DTYPE), sds((S, W), MXU_DTYPE),
                   sds((1, W), F32), sds((1, Dm), F32), sds((1, Dm), F32), sds((1, 1), F32)],
        compiler_params=_params(("arbitrary",), VMEM_BIG),
    )(o, za, ys, zb, ga, gb, x, tgt, gate, b_glu, g_final, w_glu, w_up_a, w_up_b, w_out, hsel)


def _matmul_tn(name, a, b, ts):
    S, M = a.shape
    N = b.shape[1]
    tn = min(N, 1024)

    def body(a_ref, b_ref, o_ref):
        @pl.when(pl.program_id(1) == 0)
        def _():
            o_ref[...] = jnp.zeros_like(o_ref)

        o_ref[...] += _mm_tn(a_ref[...], b_ref[...])

    return pl.pallas_call(
        body, name=name, grid=(N // tn, S // ts),
        in_specs=[pl.BlockSpec((ts, M), lambda j, i: (i, 0)), pl.BlockSpec((ts, tn), lambda j, i: (i, j))],
        out_specs=pl.BlockSpec((M, tn), lambda j, i: (0, j)),
        out_shape=jax.ShapeDtypeStruct((M, N), F32),
        compiler_params=_params(("parallel", "arbitrary"), VMEM_MID),
    )(a, b)


def _matmul_tn_pairs(name, pairs, ts):
    S = pairs[0][0].shape[0]
    n = len(pairs)

    def body(*refs):
        ins, outs = refs[:2 * n], refs[2 * n:]

        @pl.when(pl.program_id(0) == 0)
        def _():
            for o_ref in outs:
                o_ref[...] = jnp.zeros_like(o_ref)

        for k in range(n):
            outs[k][...] += _mm_tn(ins[2 * k][...], ins[2 * k + 1][...])

    flat = [x for p in pairs for x in p]
    return pl.pallas_call(
        body, name=name, grid=(S // ts,),
        in_specs=[pl.BlockSpec((ts, x.shape[1]), lambda i: (i, 0)) for x in flat],
        out_specs=[pl.BlockSpec((a.shape[1], b.shape[1]), lambda i: (0, 0)) for a, b in pairs],
        out_shape=[jax.ShapeDtypeStruct((a.shape[1], b.shape[1]), F32) for a, b in pairs],
        compiler_params=_params(("arbitrary",), VMEM_BIG),
    )(*flat)


def _matmul_tn_multi(name, a, bs, ts):
    S, M = a.shape
    nb = len(bs)

    def body(a_ref, *refs):
        @pl.when(pl.program_id(0) == 0)
        def _():
            for o_ref in refs[nb:]:
                o_ref[...] = jnp.zeros_like(o_ref)

        av = a_ref[...]
        for b_ref, o_ref in zip(refs[:nb], refs[nb:]):
            o_ref[...] += _mm_tn(av, b_ref[...])

    return pl.pallas_call(
        body, name=name, grid=(S // ts,),
        in_specs=[pl.BlockSpec((ts, M), lambda i: (i, 0))] + [pl.BlockSpec((ts, b.shape[1]), lambda i: (i, 0)) for b in bs],
        out_specs=[pl.BlockSpec((M, b.shape[1]), lambda i: (0, 0)) for b in bs],
        out_shape=[jax.ShapeDtypeStruct((M, b.shape[1]), F32) for b in bs],
        compiler_params=_params(("arbitrary",), VMEM_BIG),
    )(a, *bs)


def _proj_bwd(dq, dk, dv, dza, du, dzb, dga, dgb, dfl, x, dx2, shift, scale, g_norm, w_main, w_ft, chip_planes,
              gathers, ts):
    S = x.shape[0]
    nc, ng = len(chip_planes), len(gathers)
    nx = nc + ng

    def body(dq_ref, dk_ref, dv_ref, dza_ref, du_ref, dzb_ref, dga_ref, dgb_ref, dfl_ref, x_ref, dx2_ref,
             sc_ref, gn_ref, w_ref, wft_ref, *rest):
        gx_ref, dsh_ref, dsc_ref, dgn_ref = rest[nx:nx + 4]
        outs, sems = rest[nx + 4:2 * nx + 4], rest[2 * nx + 4:]
        i = pl.program_id(0)
        first, last = i == 0, i == pl.num_programs(0) - 1
        _hosted_chip_exchange(first, last, rest[:nc], outs[:nc], *sems[:3])
        _hosted_exchange(first, last, rest[nc:nx], outs[nc:], *sems[3:], scatter=False)

        @pl.when(pl.program_id(0) == 0)
        def _():
            dsh_ref[...] = jnp.zeros_like(dsh_ref)
            dsc_ref[...] = jnp.zeros_like(dsc_ref)
            dgn_ref[...] = jnp.zeros_like(dgn_ref)

        def seg(ref, off, n):
            return _mm_nt(ref[...], w_ref[:, off:off + n])

        dh = (seg(dq_ref, M_Q, WIDTH) + seg(dk_ref, M_K, WIDTH) + seg(dv_ref, M_V, WIDTH)
              + seg(dza_ref, M_ZA, WIDTH) + seg(du_ref, M_U, WIDTH) + seg(dzb_ref, M_ZB, WIDTH)
              + seg(dga_ref, M_GA, D_MODEL) + seg(dgb_ref, M_GB, D_MODEL)
              + _mm32(dfl_ref[...], wft_ref[...]))
        xv = x_ref[...]
        r = lax.rsqrt(jnp.mean(xv * xv, axis=-1, keepdims=True) + EPS)
        xn = xv * r
        gn = gn_ref[...]
        s1 = 1.0 + sc_ref[...]
        dsh_ref[...] += jnp.sum(dh, axis=0, keepdims=True)
        dhx = dh * xn
        dsc_ref[...] += jnp.sum(dhx, axis=0, keepdims=True) * gn
        dgn_ref[...] += jnp.sum(dhx, axis=0, keepdims=True) * s1
        dxn = dh * (gn * s1)
        gx_ref[...] = dx2_ref[...] + r * (dxn - xn * jnp.mean(dxn * xn, axis=-1, keepdims=True))

    row = lambda n: pl.BlockSpec((ts, n), lambda i: (i, 0))
    full = lambda a: pl.BlockSpec(a.shape, lambda i: (0,) * a.ndim)
    vec = pl.BlockSpec((1, D_MODEL), lambda i: (0, 0))
    W, Dm = WIDTH, D_MODEL
    del shift
    return pl.pallas_call(
        body, name="proj_bwd", grid=(S // ts,),
        in_specs=[row(W)] * 6 + [row(Dm)] * 2 + [row(HEADS), row(Dm), row(Dm),
                                                 full(scale), full(g_norm), full(w_main), full(w_ft)] + [ANY] * nx,
        out_specs=[row(Dm), vec, vec, vec] + [ANY] * nx,
        out_shape=[jax.ShapeDtypeStruct((S, Dm), F32)] + [jax.ShapeDtypeStruct((1, Dm), F32)] * 3
                  + [jax.ShapeDtypeStruct(p.shape, p.dtype) for p in chip_planes]
                  + [jax.ShapeDtypeStruct((N_DEV,) + g.shape, g.dtype) for g in gathers],
        scratch_shapes=[pltpu.SemaphoreType.DMA((nc, N_CHIP)), pltpu.SemaphoreType.DMA((nc, N_CHIP)),
                        pltpu.SemaphoreType.DMA((nc,))] + _exchange_scratch(ng),
        compiler_params=_params(("arbitrary",), VMEM_BIG),
    )(dq, dk, dv, dza, du, dzb, dga, dgb, dfl, x, dx2, scale, g_norm, w_main, w_ft, *chip_planes, *gathers)


def _adamw(name, planes, w, m, v, tr):
    n, R, C = planes.shape
    bc1 = 1.0 - ADAM_B1 ** ADAM_STEP
    bc2 = 1.0 - ADAM_B2 ** ADAM_STEP

    def body(p_ref, w_ref, m_ref, v_ref, g_ref, d_ref, nm_ref, nv_ref):
        g = p_ref[0].astype(F32)
        for i in range(1, n):
            g = g + p_ref[i].astype(F32)
        g_ref[...] = g
        nm = ADAM_B1 * m_ref[...] + (1.0 - ADAM_B1) * g
        nv = ADAM_B2 * v_ref[...] + (1.0 - ADAM_B2) * (g * g)
        nm_ref[...] = nm
        nv_ref[...] = nv
        d_ref[...] = -ADAM_LR * ((nm / bc1) / (jnp.sqrt(nv / bc2) + ADAM_EPS) + ADAM_WD * w_ref[...])

    blk = pl.BlockSpec((tr, C), lambda i: (i, 0))
    return pl.pallas_call(
        body, name=name, grid=(R // tr,),
        in_specs=[pl.BlockSpec((n, tr, C), lambda i: (0, i, 0)), blk, blk, blk],
        out_specs=[blk] * 4, out_shape=[jax.ShapeDtypeStruct((R, C), F32)] * 4,
        compiler_params=_params(("parallel",), VMEM_MID),
    )(planes, w, m, v)


def _wada_grad(c_all, dmod_cols):
    def body(c_ref, d_ref, o_ref):
        o_ref[0] = _mm32_tn(c_ref[...], d_ref[...])

    return pl.pallas_call(
        body, name="wada_grad",
        out_shape=jax.ShapeDtypeStruct((1, c_all.shape[1], dmod_cols.shape[1]), F32),
        in_specs=[VMEM, VMEM], out_specs=VMEM,
    )(c_all, dmod_cols)


SMALL_ORDER = ("b_ada", "g_norm", "b_f", "a_re", "a_im", "log_dt", "b_re", "b_im", "c_re", "c_im",
               "d_skip", "b_glu", "g_final")
BIG_ORDER = ("w_ada", "w_in", "w_glu", "w_up_a", "w_up_b", "w_out")
ALL_ORDER = ("w_ada", "b_ada", "g_norm", "w_in", "b_f", "a_re", "a_im", "log_dt", "b_re", "b_im", "c_re", "c_im",
             "d_skip", "w_glu", "b_glu", "w_up_a", "w_up_b", "w_out", "g_final")


def _pack_small(parts, rows):
    flat = jnp.concatenate([p.reshape(-1).astype(F32) for p in parts])
    return jnp.pad(flat, (0, rows * 128 - flat.shape[0])).reshape(rows, 128)


def kernel(x, c, w_ada, b_ada, g_norm, w_in, b_f, a_re, a_im, log_dt, b_re, b_im, c_re, c_im, d_skip, w_glu, b_glu, w_up_a, w_up_b, w_out, g_final, loss_target, m_w_ada, m_b_ada, m_g_norm, m_w_in, m_b_f, m_a_re, m_a_im, m_log_dt, m_b_re, m_b_im, m_c_re, m_c_im, m_d_skip, m_w_glu, m_b_glu, m_w_up_a, m_w_up_b, m_w_out, m_g_final, v_w_ada, v_b_ada, v_g_norm, v_w_in, v_b_f, v_a_re, v_a_im, v_log_dt, v_b_re, v_b_im, v_c_re, v_c_im, v_d_skip, v_w_glu, v_b_glu, v_w_up_a, v_w_up_b, v_w_out, v_g_final):
    weights = dict(w_ada=w_ada, b_ada=b_ada, g_norm=g_norm, w_in=w_in, b_f=b_f, a_re=a_re, a_im=a_im, log_dt=log_dt,
                   b_re=b_re, b_im=b_im, c_re=c_re, c_im=c_im, d_skip=d_skip, w_glu=w_glu, b_glu=b_glu,
                   w_up_a=w_up_a, w_up_b=w_up_b, w_out=w_out, g_final=g_final)
    mom_m = dict(w_ada=m_w_ada, b_ada=m_b_ada, g_norm=m_g_norm, w_in=m_w_in, b_f=m_b_f, a_re=m_a_re, a_im=m_a_im,
                 log_dt=m_log_dt, b_re=m_b_re, b_im=m_b_im, c_re=m_c_re, c_im=m_c_im, d_skip=m_d_skip, w_glu=m_w_glu,
                 b_glu=m_b_glu, w_up_a=m_w_up_a, w_up_b=m_w_up_b, w_out=m_w_out, g_final=m_g_final)
    mom_v = dict(w_ada=v_w_ada, b_ada=v_b_ada, g_norm=v_g_norm, w_in=v_w_in, b_f=v_b_f, a_re=v_a_re, a_im=v_a_im,
                 log_dt=v_log_dt, b_re=v_b_re, b_im=v_b_im, c_re=v_c_re, c_im=v_c_im, d_skip=v_d_skip, w_glu=v_w_glu,
                 b_glu=v_b_glu, w_up_a=v_w_up_a, w_up_b=v_w_up_b, w_out=v_w_out, g_final=v_g_final)
    xs = x[0]
    tgt = loss_target[0]
    S = xs.shape[0]
    ts = min(256, S)
    ta = min(512, S)
    tw = min(2048, S)
    nch = S // CHUNK
    n_steps = max(1, int(math.ceil(math.log2(nch))))
    me = _my_index()

    shards = [w.astype(MXU_DTYPE) for w in (w_in[0], w_glu[0], w_up_a[0], w_up_b[0], w_out[0])]
    mod8, c_all, gathered = _comm_in(c, w_ada[0], b_ada.reshape(N_DEV, -1), shards[:1])
    mod = mod8.reshape(1, 3 * D_MODEL)
    shift, scale, gate = mod[:, :D_MODEL], mod[:, D_MODEL:2 * D_MODEL], mod[:, 2 * D_MODEL:]
    w_in_full = gathered[0].transpose(1, 0, 2).reshape(D_MODEL, PROJ_WIDTH)
    w_main = jnp.concatenate([w_in_full[:, :OFF_F], w_in_full[:, OFF_F + HEADS:]], axis=1)
    w_f = w_in_full[:, OFF_F:OFF_F + HEADS].astype(F32)
    w_ft = w_f.T

    q, k, v, za, u, zb, ga, gb, flc, hb, *late = _proj_fwd(xs, shift, scale, g_norm, w_main, w_f, shards[1:], ts)
    w_glu_full = late[0].reshape(WIDTH, WIDTH)
    w_up_a_full = late[1].transpose(1, 0, 2).reshape(WIDTH, D_MODEL)
    w_up_b_full = late[2].transpose(1, 0, 2).reshape(WIDTH, D_MODEL)
    w_out_full = late[3].reshape(D_MODEL, D_MODEL)
    nb = S // ta
    rows4 = lambda r: r.reshape(PAIRS, 2, nb, ta).transpose(0, 2, 1, 3)
    qh, kh, vt = _attn_prep(q, k, v, flc, b_f, ta)
    o, lse4 = _attn_fwd(qh, kh, vt, ta)

    s5_params = (a_re[0], a_im[0], log_dt[0], b_re[0], b_im[0], c_re[0], c_im[0], d_skip[0])
    (trow, camat, bzmat, al), mats_vjp = jax.vjp(_s5_mats, *s5_params)
    del al
    pw_f, pw_b = _s5_scan_powers(a_re[0], a_im[0], log_dt[0], n_steps)
    ys, xprev, uc = _s5_fwd(u, trow, camat, bzmat, pw_f)

    hsel = (np.arange(WIDTH)[None, :] // 64 == np.arange(HEADS)[:, None]).astype(np.float32)
    (dx2, do, dza, dzb, dga, dgb, dys, dl_row, merged, dmo, ya, dua, yb, dub, yg, dgl,
     db_glu, dgate, dg_final, loss_part) = _mix(o, za, ys, zb, ga, gb, xs, tgt, gate, b_glu, g_final.reshape(1, -1),
                                                w_glu_full, w_up_a_full, w_up_b_full, w_out_full, jnp.asarray(hsel), ts)

    gw_out, gw_up_a, gw_up_b, gw_glu = _matmul_tn_pairs(
        "dw_merge", [(merged, dmo), (ya, dua), (yb, dub), (yg, dgl)], tw // 2)

    du, d_trow, d_camat, d_bzmat, dal2 = _s5_bwd(uc, dys, xprev, trow, camat, bzmat, pw_b)
    d_al = jnp.concatenate([dal2[:, 0, :STATE] + dal2[:, 0, STATE:], dal2[:, 1, STATE:] - dal2[:, 1, :STATE]], axis=-1)
    gs5 = mats_vjp((d_trow, d_camat, d_bzmat, d_al))

    dl4 = rows4(dl_row)
    early = [p.astype(MXU_DTYPE) for p in (gw_glu.reshape(N_DEV, -1, WIDTH),
                                           gw_up_a.reshape(WIDTH, N_DEV, -1).transpose(1, 0, 2),
                                           gw_up_b.reshape(WIDTH, N_DEV, -1).transpose(1, 0, 2),
                                           gw_out.reshape(N_DEV, -1, D_MODEL))]
    dq, dk, dv, dfq4, dfk4, *early_recv = _attn_bwd(qh, do, kh, v, lse4, dl4, early, ta)
    d_fcol = dfq4.transpose(0, 2, 1, 3).reshape(HEADS, S).T - dfk4.transpose(1, 0, 2).reshape(S, HEADS)
    dfl, db_f = _fgate_bwd(d_fcol, flc, b_f, ta)

    gw_in = jnp.concatenate(_matmul_tn_multi("dw_in_narrow", hb, [dq, dk, dv, dfl, dza, du, dzb], tw // 2)
                            + _matmul_tn_multi("dw_in_wide", hb, [dga, dgb], tw // 2), axis=1)

    planes = [gw_in.reshape(D_MODEL, N_DEV, -1).transpose(1, 0, 2).astype(MXU_DTYPE)]
    from_sib = _comm_pair(planes)
    core = lax.axis_index("c")
    chip_planes = []
    for name, p, s in zip(("w_in",), planes, from_sib):
        tr = 256 if s.shape[1] % 256 == 0 else s.shape[1]
        chip_planes.append(_pair_sum("pair_sum_" + name, p, s, core, tr))
    ready_parts = [db_f, gs5[0], gs5[1], gs5[2], gs5[3], gs5[4], gs5[5], gs5[6], gs5[7], db_glu, dg_final, loss_part]
    n_ready = sum(int(np.prod(p.shape)) for p in ready_parts)
    rows_ready = -(-n_ready // (8 * 128)) * 8
    grad_x, dshift, dscale, dg_norm, recv, ready_all = _proj_bwd(
        dq, dk, dv, dza, du, dzb, dga, dgb, dfl, xs, dx2, shift, scale, g_norm, w_main, w_ft, chip_planes,
        [_pack_small(ready_parts, rows_ready)], ts)
    recv = [recv]

    dmod = jnp.concatenate([dshift, dscale, dgate], axis=1)
    rows_late = 4 * D_MODEL // 128
    late_all = _comm_small(_pack_small([dmod, dg_norm], rows_late))
    small_all = jnp.concatenate([late_all, ready_all], axis=1)
    rows = rows_late + rows_ready

    grads, deltas, new_m, new_v = {}, {}, {}, {}

    def put(name, res, shape):
        grads[name], deltas[name], new_m[name], new_v[name] = [r.reshape(shape) for r in res]

    names = ("w_in", "w_glu", "w_up_a", "w_up_b", "w_out")
    for name, pr in zip(names, recv + early_recv):
        w2 = weights[name][0]
        tr = 256 if w2.shape[0] % 256 == 0 else w2.shape[0]
        put(name, _adamw("adamw_" + name, pr, w2, mom_m[name][0], mom_v[name][0], tr), weights[name].shape)
    cols = w_ada.shape[2]
    dmod_all = small_all[:, :24, :].reshape(N_DEV, 3 * D_MODEL)
    dmod_cols = lax.dynamic_slice_in_dim(dmod_all, me * cols, cols, axis=1)
    g_wada = _wada_grad(c_all, dmod_cols)
    put("w_ada", _adamw("adamw_w_ada", g_wada, w_ada[0], m_w_ada[0], v_w_ada[0], 256), w_ada.shape)
    pack = lambda d: _pack_small([d[n] for n in SMALL_ORDER] + [jnp.zeros((1,), F32)], rows)
    res_small = _adamw("adamw_small", small_all, pack(weights), pack(mom_m), pack(mom_v), rows)
    flat = [r.reshape(-1) for r in res_small]
    off = 0
    for name in SMALL_ORDER:
        shape = weights[name].shape
        size = int(np.prod(shape))
        put(name, [f[off:off + size] for f in flat], shape)
        off += size
    loss = flat[0][off]

    return (loss, grad_x[None], *[grads[n] for n in ALL_ORDER], *[deltas[n] for n in ALL_ORDER],
            *[new_m[n] for n in ALL_ORDER], *[new_v[n] for n in ALL_ORDER])
```
